```python
import jax, jax.numpy as jnp
from jax import lax
import numpy as np

D_MODEL = 2048
BATCH = 8
SEQ = 4096
DEPTH = 4

N_A = DEPTH // 2
N_B = DEPTH - N_A
MIX_W = D_MODEL
MEM_HEADS = 4
MEM_HEAD_DIM = 128
MEM_W = MEM_HEADS * MEM_HEAD_DIM
CHUNK = 128
G_HEADS = 12
G_DIM = 128
G_W = G_HEADS * G_DIM
MLA_HEADS = 12
NOPE_DIM = 128
ROPE_DIM = 64
V_DIM = 128
Q_RANK = 512
KV_RANK = 512
Q_BLOCK = 128
ROPE_THETA = 10000.0
D_FF = 5632
CONV_W = 3
EPS = 1e-6

kernel_name = "yoco_gmlp_mla_memxattn_convffn"


def rmsnorm(x, g):
    x32 = x.astype(jnp.float32)
    y = x32 * lax.rsqrt(jnp.mean(x32 * x32, axis=-1, keepdims=True) + EPS)
    return (y * g.astype(jnp.float32)).astype(x.dtype)


def rope_tables(positions, dtype):
    inv = 1.0 / (ROPE_THETA ** (jnp.arange(0, ROPE_DIM, 2, dtype=jnp.float32) / ROPE_DIM))
    ang = positions.astype(jnp.float32)[..., None] * inv
    return jnp.cos(ang).astype(dtype), jnp.sin(ang).astype(dtype)


def apply_rope(x, cos, sin):
    x1, x2 = jnp.split(x, 2, axis=-1)
    return jnp.concatenate([x1 * cos - x2 * sin, x2 * cos + x1 * sin], axis=-1)


def spatial_gating(z_u, z_v, g_v, w_sp, b_sp):
    B, S, _ = z_u.shape
    u = jax.nn.gelu(z_u, approximate=False)
    v = rmsnorm(jax.nn.gelu(z_v, approximate=False), g_v)
    vb = v.reshape(B, S // CHUNK, CHUNK, G_HEADS, G_DIM)
    w = w_sp * jnp.tril(jnp.ones((CHUNK, CHUNK), w_sp.dtype))
    sv = jnp.einsum('gts,bnsgc->bntgc', w, vb) + b_sp.T[None, None, :, :, None]
    return u * sv.reshape(B, S, G_W)


def mla_attention(q_lat, g_q_lat, w_uq, w_uk, w_uv, c_kv, k_rope, cos, sin):
    B, S, _ = q_lat.shape
    q = (rmsnorm(q_lat, g_q_lat) @ w_uq).reshape(B, S, MLA_HEADS, NOPE_DIM + ROPE_DIM)
    q_nope = q[..., :NOPE_DIM]
    q_rope = apply_rope(q[..., NOPE_DIM:], cos[:, :, None, :], sin[:, :, None, :])
    n_blk = S // Q_BLOCK
    scale = (NOPE_DIM + ROPE_DIM) ** -0.5
    k_pos = jnp.arange(S)

    def to_blocks(t):
        return jnp.moveaxis(t.reshape(B, n_blk, Q_BLOCK, *t.shape[2:]), 1, 0)

    def block(args):
        qn, qr, i = args
        qa = jnp.einsum('bqhn,rhn->bqhr', qn, w_uk)
        s = (jnp.einsum('bqhr,bkr->bhqk', qa, c_kv)
             + jnp.einsum('bqhp,bkp->bhqk', qr, k_rope)).astype(jnp.float32) * scale
        q_pos = i * Q_BLOCK + jnp.arange(Q_BLOCK)
        s = jnp.where(k_pos[None, :] <= q_pos[:, None], s, -jnp.inf)
        p = jax.nn.softmax(s, axis=-1).astype(c_kv.dtype)
        o_lat = jnp.einsum('bhqk,bkr->bqhr', p, c_kv)
        return jnp.einsum('bqhr,rhv->bqhv', o_lat, w_uv)

    o = lax.map(block, (to_blocks(q_nope), to_blocks(q_rope), jnp.arange(n_blk)))
    return jnp.moveaxis(o, 0, 1).reshape(B, S, MLA_HEADS * V_DIM)


def memory_attention(q_m, mem, g_mem, w_mem_kv):
    B, S, _ = q_m.shape
    M = mem.shape[1]
    q = q_m.reshape(B, S, MEM_HEADS, MEM_HEAD_DIM)
    kv = rmsnorm(mem, g_mem) @ w_mem_kv
    k = kv[..., :MEM_W].reshape(B, M, MEM_HEADS, MEM_HEAD_DIM)
    v = kv[..., MEM_W:].reshape(B, M, MEM_HEADS, MEM_HEAD_DIM)
    s = jnp.einsum('bshd,bmhd->bhsm', q, k).astype(jnp.float32) * (MEM_HEAD_DIM ** -0.5)
    p = jax.nn.softmax(s, axis=-1).astype(v.dtype)
    return jnp.einsum('bhsm,bmhd->bshd', p, v).reshape(B, S, MEM_W)


def conv_ffn(h, w_up, cw, cb, w_down):
    S = h.shape[1]
    a = h @ w_up
    ap = jnp.pad(a, ((0, 0), (CONV_W - 1, 0), (0, 0)))
    c = sum(ap[:, k:k + S] * cw[k] for k in range(CONV_W)) + cb
    gate, val = c[..., :D_FF], c[..., D_FF:]
    return (jax.nn.silu(gate) * val) @ w_down


def _fwd_setup_inputs(seed: int = 0) -> dict:
    key = jax.random.key(seed)
    ks = jax.random.split(key, 26)

    def nrm(k, shape, scale):
        return jax.random.normal(k, shape, jnp.float32) * scale

    def gain(k, shape):
        return 1.0 + 0.05 * jax.random.normal(k, shape, jnp.float32)

    D = D_MODEL
    return {
        "x": nrm(ks[0], (BATCH, SEQ, D), 1.0),
        "mem": nrm(ks[1], (BATCH, 256, D), 1.0),
        "positions": (jnp.arange(SEQ, dtype=jnp.int32)[None, :]
                      + jax.random.randint(ks[2], (BATCH, 1), 0, 1024, dtype=jnp.int32)),
        "g_mix": gain(ks[3], (DEPTH, D)),
        "g_ffn": gain(ks[4], (DEPTH, D)),
        "g_final": gain(ks[5], (D,)),
        "w_in_a": nrm(ks[6], (N_A, D, 2 * G_W + MEM_W), D ** -0.5),
        "g_v": gain(ks[7], (N_A, G_W)),
        "w_sp": nrm(ks[8], (N_A, G_HEADS, CHUNK, CHUNK), CHUNK ** -0.5),
        "b_sp": 1.0 + 0.1 * jax.random.normal(ks[9], (N_A, G_HEADS, CHUNK), jnp.float32),
        "g_kv": gain(ks[10], (D,)),
        "w_kv_a": nrm(ks[11], (D, KV_RANK + ROPE_DIM), D ** -0.5),
        "g_kv_lat": gain(ks[12], (KV_RANK,)),
        "w_in_b": nrm(ks[13], (N_B, D, Q_RANK + MEM_W), D ** -0.5),
        "g_q_lat": gain(ks[14], (N_B, Q_RANK)),
        "w_uq": nrm(ks[15], (N_B, Q_RANK, MLA_HEADS * (NOPE_DIM + ROPE_DIM)), Q_RANK ** -0.5),
        "w_uk": nrm(ks[16], (N_B, KV_RANK, MLA_HEADS, NOPE_DIM), KV_RANK ** -0.5),
        "w_uv": nrm(ks[17], (N_B, KV_RANK, MLA_HEADS, V_DIM), KV_RANK ** -0.5),
        "g_mem": gain(ks[18], (DEPTH, D)),
        "w_mem_kv": nrm(ks[19], (DEPTH, D, 2 * MEM_W), D ** -0.5),
        "w_out": nrm(ks[20], (DEPTH, MIX_W, D), MIX_W ** -0.5),
        "w_ffn_up": nrm(ks[21], (DEPTH, D, 2 * D_FF), D ** -0.5),
        "conv_w": nrm(ks[22], (DEPTH, CONV_W, 2 * D_FF), CONV_W ** -0.5),
        "conv_b": nrm(ks[23], (DEPTH, 2 * D_FF), 0.01),
        "w_ffn_down": nrm(ks[24], (DEPTH, D_FF, D), D_FF ** -0.5),
    }


def _fwd_reference(x, mem, positions, g_mix, g_ffn, g_final, w_in_a, g_v, w_sp, b_sp,
              g_kv, w_kv_a, g_kv_lat, w_in_b, g_q_lat, w_uq, w_uk, w_uv,
              g_mem, w_mem_kv, w_out, w_ffn_up, conv_w, conv_b, w_ffn_down):
    cos, sin = rope_tables(positions, x.dtype)
    c_kv = None
    k_rope = None
    for l in range(DEPTH):
        if l == N_A:
            kv = rmsnorm(x, g_kv) @ w_kv_a
            c_kv = rmsnorm(kv[..., :KV_RANK], g_kv_lat)
            k_rope = apply_rope(kv[..., KV_RANK:], cos, sin)
        h = rmsnorm(x, g_mix[l])
        if l < N_A:
            z = h @ w_in_a[l]
            main = spatial_gating(z[..., :G_W], z[..., G_W:2 * G_W], g_v[l], w_sp[l], b_sp[l])
            q_m = z[..., 2 * G_W:]
        else:
            j = l - N_A
            z = h @ w_in_b[j]
            main = mla_attention(z[..., :Q_RANK], g_q_lat[j], w_uq[j], w_uk[j], w_uv[j],
                                 c_kv, k_rope, cos, sin)
            q_m = z[..., Q_RANK:]
        mo = memory_attention(q_m, mem, g_mem[l], w_mem_kv[l])
        x = x + jnp.concatenate([main, mo], axis=-1) @ w_out[l]
        x = x + conv_ffn(rmsnorm(x, g_ffn[l]), w_ffn_up[l], conv_w[l], conv_b[l], w_ffn_down[l])
    return rmsnorm(x, g_final)


import jax as _jax
import jax.numpy as _jnp

TWIN_FORMAT = 'train_step'
FWD_PARAMS = ['x', 'mem', 'positions', 'g_mix', 'g_ffn', 'g_final', 'w_in_a', 'g_v', 'w_sp', 'b_sp', 'g_kv', 'w_kv_a', 'g_kv_lat', 'w_in_b', 'g_q_lat', 'w_uq', 'w_uk', 'w_uv', 'g_mem', 'w_mem_kv', 'w_out', 'w_ffn_up', 'conv_w', 'conv_b', 'w_ffn_down']
TWIN_WEIGHTS = ['g_mix', 'g_ffn', 'g_final', 'w_in_a', 'g_v', 'w_sp', 'b_sp', 'g_kv', 'w_kv_a', 'g_kv_lat', 'w_in_b', 'g_q_lat', 'w_uq', 'w_uk', 'w_uv', 'g_mem', 'w_mem_kv', 'w_out', 'w_ffn_up', 'conv_w', 'conv_b', 'w_ffn_down']
TWIN_DIFF_INPUT = 'x'
TWIN_INPUTS = ['x', 'mem', 'positions', 'g_mix', 'g_ffn', 'g_final', 'w_in_a', 'g_v', 'w_sp', 'b_sp', 'g_kv', 'w_kv_a', 'g_kv_lat', 'w_in_b', 'g_q_lat', 'w_uq', 'w_uk', 'w_uv', 'g_mem', 'w_mem_kv', 'w_out', 'w_ffn_up', 'conv_w', 'conv_b', 'w_ffn_down', 'loss_target', 'm_g_mix', 'm_g_ffn', 'm_g_final', 'm_w_in_a', 'm_g_v', 'm_w_sp', 'm_b_sp', 'm_g_kv', 'm_w_kv_a', 'm_g_kv_lat', 'm_w_in_b', 'm_g_q_lat', 'm_w_uq', 'm_w_uk', 'm_w_uv', 'm_g_mem', 'm_w_mem_kv', 'm_w_out', 'm_w_ffn_up', 'm_conv_w', 'm_conv_b', 'm_w_ffn_down', 'v_g_mix', 'v_g_ffn', 'v_g_final', 'v_w_in_a', 'v_g_v', 'v_w_sp', 'v_b_sp', 'v_g_kv', 'v_w_kv_a', 'v_g_kv_lat', 'v_w_in_b', 'v_g_q_lat', 'v_w_uq', 'v_w_uk', 'v_w_uv', 'v_g_mem', 'v_w_mem_kv', 'v_w_out', 'v_w_ffn_up', 'v_conv_w', 'v_conv_b', 'v_w_ffn_down']
TWIN_OUTPUTS = ['loss', 'grad_x', 'grad_g_mix', 'grad_g_ffn', 'grad_g_final', 'grad_w_in_a', 'grad_g_v', 'grad_w_sp', 'grad_b_sp', 'grad_g_kv', 'grad_w_kv_a', 'grad_g_kv_lat', 'grad_w_in_b', 'grad_g_q_lat', 'grad_w_uq', 'grad_w_uk', 'grad_w_uv', 'grad_g_mem', 'grad_w_mem_kv', 'grad_w_out', 'grad_w_ffn_up', 'grad_conv_w', 'grad_conv_b', 'grad_w_ffn_down', 'delta_g_mix', 'delta_g_ffn', 'delta_g_final', 'delta_w_in_a', 'delta_g_v', 'delta_w_sp', 'delta_b_sp', 'delta_g_kv', 'delta_w_kv_a', 'delta_g_kv_lat', 'delta_w_in_b', 'delta_g_q_lat', 'delta_w_uq', 'delta_w_uk', 'delta_w_uv', 'delta_g_mem', 'delta_w_mem_kv', 'delta_w_out', 'delta_w_ffn_up', 'delta_conv_w', 'delta_conv_b', 'delta_w_ffn_down', 'new_m_g_mix', 'new_m_g_ffn', 'new_m_g_final', 'new_m_w_in_a', 'new_m_g_v', 'new_m_w_sp', 'new_m_b_sp', 'new_m_g_kv', 'new_m_w_kv_a', 'new_m_g_kv_lat', 'new_m_w_in_b', 'new_m_g_q_lat', 'new_m_w_uq', 'new_m_w_uk', 'new_m_w_uv', 'new_m_g_mem', 'new_m_w_mem_kv', 'new_m_w_out', 'new_m_w_ffn_up', 'new_m_conv_w', 'new_m_conv_b', 'new_m_w_ffn_down', 'new_v_g_mix', 'new_v_g_ffn', 'new_v_g_final', 'new_v_w_in_a', 'new_v_g_v', 'new_v_w_sp', 'new_v_b_sp', 'new_v_g_kv', 'new_v_w_kv_a', 'new_v_g_kv_lat', 'new_v_w_in_b', 'new_v_g_q_lat', 'new_v_w_uq', 'new_v_w_uk', 'new_v_w_uv', 'new_v_g_mem', 'new_v_w_mem_kv', 'new_v_w_out', 'new_v_w_ffn_up', 'new_v_conv_w', 'new_v_conv_b', 'new_v_w_ffn_down']
TWIN_LEAF_KINDS = {'loss': 'loss', 'grad_x': 'grad_x', 'grad_g_mix': 'grad_w', 'grad_g_ffn': 'grad_w', 'grad_g_final': 'grad_w', 'grad_w_in_a': 'grad_w', 'grad_g_v': 'grad_w', 'grad_w_sp': 'grad_w', 'grad_b_sp': 'grad_w', 'grad_g_kv': 'grad_w', 'grad_w_kv_a': 'grad_w', 'grad_g_kv_lat': 'grad_w', 'grad_w_in_b': 'grad_w', 'grad_g_q_lat': 'grad_w', 'grad_w_uq': 'grad_w', 'grad_w_uk': 'grad_w', 'grad_w_uv': 'grad_w', 'grad_g_mem': 'grad_w', 'grad_w_mem_kv': 'grad_w', 'grad_w_out': 'grad_w', 'grad_w_ffn_up': 'grad_w', 'grad_conv_w': 'grad_w', 'grad_conv_b': 'grad_w', 'grad_w_ffn_down': 'grad_w', 'delta_g_mix': 'delta_w', 'delta_g_ffn': 'delta_w', 'delta_g_final': 'delta_w', 'delta_w_in_a': 'delta_w', 'delta_g_v': 'delta_w', 'delta_w_sp': 'delta_w', 'delta_b_sp': 'delta_w', 'delta_g_kv': 'delta_w', 'delta_w_kv_a': 'delta_w', 'delta_g_kv_lat': 'delta_w', 'delta_w_in_b': 'delta_w', 'delta_g_q_lat': 'delta_w', 'delta_w_uq': 'delta_w', 'delta_w_uk': 'delta_w', 'delta_w_uv': 'delta_w', 'delta_g_mem': 'delta_w', 'delta_w_mem_kv': 'delta_w', 'delta_w_out': 'delta_w', 'delta_w_ffn_up': 'delta_w', 'delta_conv_w': 'delta_w', 'delta_conv_b': 'delta_w', 'delta_w_ffn_down': 'delta_w', 'new_m_g_mix': 'new_m', 'new_m_g_ffn': 'new_m', 'new_m_g_final': 'new_m', 'new_m_w_in_a': 'new_m', 'new_m_g_v': 'new_m', 'new_m_w_sp': 'new_m', 'new_m_b_sp': 'new_m', 'new_m_g_kv': 'new_m', 'new_m_w_kv_a': 'new_m', 'new_m_g_kv_lat': 'new_m', 'new_m_w_in_b': 'new_m', 'new_m_g_q_lat': 'new_m', 'new_m_w_uq': 'new_m', 'new_m_w_uk': 'new_m', 'new_m_w_uv': 'new_m', 'new_m_g_mem': 'new_m', 'new_m_w_mem_kv': 'new_m', 'new_m_w_out': 'new_m', 'new_m_w_ffn_up': 'new_m', 'new_m_conv_w': 'new_m', 'new_m_conv_b': 'new_m', 'new_m_w_ffn_down': 'new_m', 'new_v_g_mix': 'new_v', 'new_v_g_ffn': 'new_v', 'new_v_g_final': 'new_v', 'new_v_w_in_a': 'new_v', 'new_v_g_v': 'new_v', 'new_v_w_sp': 'new_v', 'new_v_b_sp': 'new_v', 'new_v_g_kv': 'new_v', 'new_v_w_kv_a': 'new_v', 'new_v_g_kv_lat': 'new_v', 'new_v_w_in_b': 'new_v', 'new_v_g_q_lat': 'new_v', 'new_v_w_uq': 'new_v', 'new_v_w_uk': 'new_v', 'new_v_w_uv': 'new_v', 'new_v_g_mem': 'new_v', 'new_v_w_mem_kv': 'new_v', 'new_v_w_out': 'new_v', 'new_v_w_ffn_up': 'new_v', 'new_v_conv_w': 'new_v', 'new_v_conv_b': 'new_v', 'new_v_w_ffn_down': 'new_v'}


def _forward(args):
    return _fwd_reference(*[args[k] for k in FWD_PARAMS])


def _output_shape():
    out = _jax.eval_shape(lambda: _forward(_fwd_setup_inputs(0)))
    return out.shape, out.dtype

N_MICROBATCH = 1
ADAM_LR = 0.001
ADAM_B1 = 0.9
ADAM_B2 = 0.999
ADAM_EPS = 1e-08
ADAM_WD = 0.01
ADAM_STEP = 10
PER_EXAMPLE_BATCH_AXIS = {'x': 0, 'mem': 0, 'positions': 0, 'loss_target': 0}
SHARED_INPUTS = []
_WEIGHT_DTYPES = {'g_mix': _jnp.float32, 'g_ffn': _jnp.float32, 'g_final': _jnp.float32, 'w_in_a': _jnp.float32, 'g_v': _jnp.float32, 'w_sp': _jnp.float32, 'b_sp': _jnp.float32, 'g_kv': _jnp.float32, 'w_kv_a': _jnp.float32, 'g_kv_lat': _jnp.float32, 'w_in_b': _jnp.float32, 'g_q_lat': _jnp.float32, 'w_uq': _jnp.float32, 'w_uk': _jnp.float32, 'w_uv': _jnp.float32, 'g_mem': _jnp.float32, 'w_mem_kv': _jnp.float32, 'w_out': _jnp.float32, 'w_ffn_up': _jnp.float32, 'conv_w': _jnp.float32, 'conv_b': _jnp.float32, 'w_ffn_down': _jnp.float32}
MOMENT_SCALE = {'g_mix': 5.117531e-02, 'g_ffn': 5.607827e-02, 'g_final': 1.604726e+01, 'w_in_a': 5.363967e-02, 'g_v': 3.850135e-02, 'w_sp': 3.928702e-02, 'b_sp': 5.936507e-02, 'g_kv': 2.706384e-02, 'w_kv_a': 5.035712e-02, 'g_kv_lat': 6.257019e-02, 'w_in_b': 1.250482e-02, 'g_q_lat': 1.732106e-02, 'w_uq': 7.865724e-03, 'w_uk': 7.962487e-03, 'w_uv': 2.546837e-02, 'g_mem': 5.905195e-03, 'w_mem_kv': 8.229948e-03, 'w_out': 5.231134e-02, 'w_ffn_up': 2.365005e-02, 'conv_w': 2.354838e-02, 'conv_b': 2.757904e-02, 'w_ffn_down': 3.891037e-02}


def _to_microbatches(a, axis):
    t = _jnp.moveaxis(a, axis, 0)
    t = t.reshape((N_MICROBATCH, t.shape[0] // N_MICROBATCH) + t.shape[1:])
    return _jnp.moveaxis(t, 1, axis + 1)


def setup_inputs(seed: int = 0) -> dict:
    inp = _fwd_setup_inputs(seed)
    key = _jax.random.fold_in(_jax.random.key(seed), 7919)
    shape, _ = _output_shape()
    out = dict(inp)
    out["loss_target"] = _jax.random.normal(_jax.random.fold_in(key, 0), shape, _jnp.float32)
    for i, name in enumerate(TWIN_WEIGHTS):
        w = inp[name].astype(_jnp.float32)
        if MOMENT_SCALE is None:
            s = _jnp.sqrt(_jnp.mean(_jnp.square(w)) + 1e-30)
        else:
            s = MOMENT_SCALE[name]
        km, kv = _jax.random.split(_jax.random.fold_in(key, i + 1))
        out[name] = w
        out["m_" + name] = s * _jax.random.normal(km, w.shape, _jnp.float32)
        out["v_" + name] = (s * s) * _jax.random.uniform(kv, w.shape, _jnp.float32, 0.5, 1.5)
    if N_MICROBATCH > 1:
        for name, axis in PER_EXAMPLE_BATCH_AXIS.items():
            out[name] = _to_microbatches(out[name], axis)
    return {'x': out['x'], 'mem': out['mem'], 'positions': out['positions'], 'g_mix': out['g_mix'], 'g_ffn': out['g_ffn'], 'g_final': out['g_final'], 'w_in_a': out['w_in_a'], 'g_v': out['g_v'], 'w_sp': out['w_sp'], 'b_sp': out['b_sp'], 'g_kv': out['g_kv'], 'w_kv_a': out['w_kv_a'], 'g_kv_lat': out['g_kv_lat'], 'w_in_b': out['w_in_b'], 'g_q_lat': out['g_q_lat'], 'w_uq': out['w_uq'], 'w_uk': out['w_uk'], 'w_uv': out['w_uv'], 'g_mem': out['g_mem'], 'w_mem_kv': out['w_mem_kv'], 'w_out': out['w_out'], 'w_ffn_up': out['w_ffn_up'], 'conv_w': out['conv_w'], 'conv_b': out['conv_b'], 'w_ffn_down': out['w_ffn_down'], 'loss_target': out['loss_target'], 'm_g_mix': out['m_g_mix'], 'm_g_ffn': out['m_g_ffn'], 'm_g_final': out['m_g_final'], 'm_w_in_a': out['m_w_in_a'], 'm_g_v': out['m_g_v'], 'm_w_sp': out['m_w_sp'], 'm_b_sp': out['m_b_sp'], 'm_g_kv': out['m_g_kv'], 'm_w_kv_a': out['m_w_kv_a'], 'm_g_kv_lat': out['m_g_kv_lat'], 'm_w_in_b': out['m_w_in_b'], 'm_g_q_lat': out['m_g_q_lat'], 'm_w_uq': out['m_w_uq'], 'm_w_uk': out['m_w_uk'], 'm_w_uv': out['m_w_uv'], 'm_g_mem': out['m_g_mem'], 'm_w_mem_kv': out['m_w_mem_kv'], 'm_w_out': out['m_w_out'], 'm_w_ffn_up': out['m_w_ffn_up'], 'm_conv_w': out['m_conv_w'], 'm_conv_b': out['m_conv_b'], 'm_w_ffn_down': out['m_w_ffn_down'], 'v_g_mix': out['v_g_mix'], 'v_g_ffn': out['v_g_ffn'], 'v_g_final': out['v_g_final'], 'v_w_in_a': out['v_w_in_a'], 'v_g_v': out['v_g_v'], 'v_w_sp': out['v_w_sp'], 'v_b_sp': out['v_b_sp'], 'v_g_kv': out['v_g_kv'], 'v_w_kv_a': out['v_w_kv_a'], 'v_g_kv_lat': out['v_g_kv_lat'], 'v_w_in_b': out['v_w_in_b'], 'v_g_q_lat': out['v_g_q_lat'], 'v_w_uq': out['v_w_uq'], 'v_w_uk': out['v_w_uk'], 'v_w_uv': out['v_w_uv'], 'v_g_mem': out['v_g_mem'], 'v_w_mem_kv': out['v_w_mem_kv'], 'v_w_out': out['v_w_out'], 'v_w_ffn_up': out['v_w_ffn_up'], 'v_conv_w': out['v_conv_w'], 'v_conv_b': out['v_conv_b'], 'v_w_ffn_down': out['v_w_ffn_down']}


def _loss(weights, diff, rest, loss_target):
    with _jax.named_scope("forward"):
        args = {**rest, TWIN_DIFF_INPUT: diff, **{k: w.astype(_WEIGHT_DTYPES[k]) for k, w in weights.items()}}
        y = _forward(args)
    with _jax.named_scope("loss_head"):
        err = _jnp.square(y.astype(_jnp.float32) - loss_target)
        return 0.5 * _jnp.sum(_jnp.mean(err, axis=-1)) if err.ndim else 0.5 * err


def _adamw(w, g, m, v):
    m = ADAM_B1 * m + (1.0 - ADAM_B1) * g
    v = ADAM_B2 * v + (1.0 - ADAM_B2) * _jnp.square(g)
    m_hat = m / (1.0 - ADAM_B1 ** ADAM_STEP)
    v_hat = v / (1.0 - ADAM_B2 ** ADAM_STEP)
    delta = -ADAM_LR * (m_hat / (_jnp.sqrt(v_hat) + ADAM_EPS) + ADAM_WD * w)
    return delta, m, v


def reference(x, mem, positions, g_mix, g_ffn, g_final, w_in_a, g_v, w_sp, b_sp, g_kv, w_kv_a, g_kv_lat, w_in_b, g_q_lat, w_uq, w_uk, w_uv, g_mem, w_mem_kv, w_out, w_ffn_up, conv_w, conv_b, w_ffn_down, loss_target, m_g_mix, m_g_ffn, m_g_final, m_w_in_a, m_g_v, m_w_sp, m_b_sp, m_g_kv, m_w_kv_a, m_g_kv_lat, m_w_in_b, m_g_q_lat, m_w_uq, m_w_uk, m_w_uv, m_g_mem, m_w_mem_kv, m_w_out, m_w_ffn_up, m_conv_w, m_conv_b, m_w_ffn_down, v_g_mix, v_g_ffn, v_g_final, v_w_in_a, v_g_v, v_w_sp, v_b_sp, v_g_kv, v_w_kv_a, v_g_kv_lat, v_w_in_b, v_g_q_lat, v_w_uq, v_w_uk, v_w_uv, v_g_mem, v_w_mem_kv, v_w_out, v_w_ffn_up, v_conv_w, v_conv_b, v_w_ffn_down):
    given = dict(x=x, mem=mem, positions=positions, g_mix=g_mix, g_ffn=g_ffn, g_final=g_final, w_in_a=w_in_a, g_v=g_v, w_sp=w_sp, b_sp=b_sp, g_kv=g_kv, w_kv_a=w_kv_a, g_kv_lat=g_kv_lat, w_in_b=w_in_b, g_q_lat=g_q_lat, w_uq=w_uq, w_uk=w_uk, w_uv=w_uv, g_mem=g_mem, w_mem_kv=w_mem_kv, w_out=w_out, w_ffn_up=w_ffn_up, conv_w=conv_w, conv_b=conv_b, w_ffn_down=w_ffn_down, loss_target=loss_target, m_g_mix=m_g_mix, m_g_ffn=m_g_ffn, m_g_final=m_g_final, m_w_in_a=m_w_in_a, m_g_v=m_g_v, m_w_sp=m_w_sp, m_b_sp=m_b_sp, m_g_kv=m_g_kv, m_w_kv_a=m_w_kv_a, m_g_kv_lat=m_g_kv_lat, m_w_in_b=m_w_in_b, m_g_q_lat=m_g_q_lat, m_w_uq=m_w_uq, m_w_uk=m_w_uk, m_w_uv=m_w_uv, m_g_mem=m_g_mem, m_w_mem_kv=m_w_mem_kv, m_w_out=m_w_out, m_w_ffn_up=m_w_ffn_up, m_conv_w=m_conv_w, m_conv_b=m_conv_b, m_w_ffn_down=m_w_ffn_down, v_g_mix=v_g_mix, v_g_ffn=v_g_ffn, v_g_final=v_g_final, v_w_in_a=v_w_in_a, v_g_v=v_g_v, v_w_sp=v_w_sp, v_b_sp=v_b_sp, v_g_kv=v_g_kv, v_w_kv_a=v_w_kv_a, v_g_kv_lat=v_g_kv_lat, v_w_in_b=v_w_in_b, v_g_q_lat=v_g_q_lat, v_w_uq=v_w_uq, v_w_uk=v_w_uk, v_w_uv=v_w_uv, v_g_mem=v_g_mem, v_w_mem_kv=v_w_mem_kv, v_w_out=v_w_out, v_w_ffn_up=v_w_ffn_up, v_conv_w=v_conv_w, v_conv_b=v_conv_b, v_w_ffn_down=v_w_ffn_down)
    weights = {n: given[n] for n in TWIN_WEIGHTS}
    shared = {n: given[n] for n in SHARED_INPUTS}
    per_example = {n: given[n] for n in ['x', 'mem', 'positions']}
    grad_fn = _jax.value_and_grad(_loss, argnums=(0, 1))

    def one_microbatch(ex, loss_target):
        ex = dict(ex)
        diff = ex.pop(TWIN_DIFF_INPUT)
        return grad_fn(weights, diff, {**shared, **ex}, loss_target)

    if N_MICROBATCH == 1:
        loss, (grad_w, grad_x) = one_microbatch(per_example, given["loss_target"])
    else:
        def body(carry, xs):
            loss_sum, grad_sum = carry
            l_k, (gw_k, gx_k) = one_microbatch(xs[0], xs[1])
            with _jax.named_scope("update"):
                return (loss_sum + l_k, _jax.tree.map(_jnp.add, grad_sum, gw_k)), gx_k

        init = (_jnp.zeros((), _jnp.float32), _jax.tree.map(_jnp.zeros_like, weights))
        (loss, grad_w), grad_x = _jax.lax.scan(body, init, (per_example, given["loss_target"]))
    with _jax.named_scope("update"):
        delta_w, new_m, new_v = {}, {}, {}
        for n in TWIN_WEIGHTS:
            delta_w[n], new_m[n], new_v[n] = _adamw(weights[n], grad_w[n], given["m_" + n], given["v_" + n])
    return (loss, grad_x, *[grad_w[n] for n in TWIN_WEIGHTS], *[delta_w[n] for n in TWIN_WEIGHTS],
            *[new_m[n] for n in TWIN_WEIGHTS], *[new_v[n] for n in TWIN_WEIGHTS])
```

```python
import math

import jax
import jax.numpy as jnp
from jax import lax
from jax.experimental import pallas as pl
from jax.experimental.pallas import tpu as pltpu

F32 = jnp.float32
BF16 = jnp.bfloat16

N_DEV = 8
N_A = 2
DEPTH = 4
G_HEADS = 12
HEAD = 128
CHUNK = 128
MEM_HEADS = 4
MEM_W = MEM_HEADS * HEAD
G_W = G_HEADS * HEAD
ROPE_DIM = 64
ROPE_HALF = ROPE_DIM // 2
KV_RANK = 512
Q_RANK = 512
KV_PAD = 640
ROPE_THETA = 10000.0
EPS = 1e-6
CONV_W = 3

ADAM_LR = 0.001
ADAM_B1 = 0.9
ADAM_B2 = 0.999
ADAM_EPS = 1e-08
ADAM_WD = 0.01
ADAM_STEP = 10

VMEM_LIMIT_V7X = 56 * 1024 * 1024
MASK_VALUE = -1e30

WEIGHTS = ['g_mix', 'g_ffn', 'g_final', 'w_in_a', 'g_v', 'w_sp', 'b_sp', 'g_kv', 'w_kv_a', 'g_kv_lat',
           'w_in_b', 'g_q_lat', 'w_uq', 'w_uk', 'w_uv', 'g_mem', 'w_mem_kv', 'w_out', 'w_ffn_up',
           'conv_w', 'conv_b', 'w_ffn_down']
REPLICATED = ['g_mix', 'g_ffn', 'g_final', 'w_sp', 'b_sp', 'g_kv', 'g_kv_lat', 'g_q_lat', 'g_mem', 'conv_b']
SMALL_SHARDED = ['g_v', 'conv_w']


def _params(sem=None):
    return pltpu.CompilerParams(dimension_semantics=sem, vmem_limit_bytes=VMEM_LIMIT_V7X)


def _dot(a, b, dims):
    contract = {'nn': ((1,), (0,)), 'nt': ((1,), (1,)), 'tn': ((0,), (0,))}[dims]
    return lax.dot_general(a, b, (contract, ((), ())), preferred_element_type=F32)


def _erf(x):
    return lax.erf(x)


def _gelu(x):
    return 0.5 * x * (1.0 + _erf(x * (2.0 ** -0.5)))


def _gelu_grad(x):
    cdf = 0.5 * (1.0 + _erf(x * (2.0 ** -0.5)))
    pdf = jnp.exp(-0.5 * x * x) * (1.0 / math.sqrt(2.0 * math.pi))
    return cdf + x * pdf


def _sigmoid(x):
    return 1.0 / (1.0 + jnp.exp(-x))


def _operand_spec(shape, lead, blocked, tr, tc, ridx, cidx):
    if blocked:
        per = shape[-1] // tc
        assert shape[-1] % tc == 0, (shape, tc)
        return pl.BlockSpec(
            (None,) * (1 + len(lead)) + (tr, tc),
            lambda *g: (cidx(*g) // per,) + lead + (ridx(*g), cidx(*g) % per))
    return pl.BlockSpec((None,) * len(lead) + (tr, tc), lambda *g: lead + (ridx(*g), cidx(*g)))


def _view2d(x, blocked):
    return (x.shape[-2], x.shape[0] * x.shape[-1]) if blocked else (x.shape[-2], x.shape[-1])


def _mm(a, b, *, dims, out_dtype, name, tm, tn, tk=None, res=None, a_lead=(), b_lead=(),
        a_blocked=False, b_blocked=False, out_block=None, n_outer=False):
    ar, ac = _view2d(a, a_blocked)
    br, bc = _view2d(b, b_blocked)
    m, k = (ac, ar) if dims == 'tn' else (ar, ac)
    n, k2 = (br, bc) if dims == 'nt' else (bc, br)
    assert k == k2, (a.shape, b.shape, dims)
    tm, tn = min(tm, m), min(tn, n)
    tk = k if tk is None else tk
    assert m % tm == 0 and n % tn == 0 and k % tk == 0, (name, m, n, k, tm, tn, tk)
    nk = k // tk
    if n_outer:
        gi, gj = (lambda g0, g1, g2: g1), (lambda g0, g1, g2: g0)
        grid = (n // tn, m // tm, nk)
    else:
        gi, gj = (lambda g0, g1, g2: g0), (lambda g0, g1, g2: g1)
        grid = (m // tm, n // tn, nk)
    gk = lambda g0, g1, g2: g2

    if dims == 'tn':
        a_spec = _operand_spec(a.shape, a_lead, a_blocked, tk, tm, gk, gi)
    else:
        a_spec = _operand_spec(a.shape, a_lead, a_blocked, tm, tk, gi, gk)
    if dims == 'nt':
        b_spec = _operand_spec(b.shape, b_lead, b_blocked, tn, tk, gj, gk)
    else:
        b_spec = _operand_spec(b.shape, b_lead, b_blocked, tk, tn, gk, gj)
    in_specs = [a_spec, b_spec]
    operands = [a, b]
    if res is not None:
        in_specs.append(pl.BlockSpec((tm, tn), lambda *g: (gi(*g), gj(*g))))
        operands.append(res)
    if out_block is not None:
        out_shape = jax.ShapeDtypeStruct((n // out_block, m, out_block), out_dtype)
        out_spec = _operand_spec(out_shape.shape, (), True, tm, tn, gi, gj)
    else:
        out_shape = jax.ShapeDtypeStruct((m, n), out_dtype)
        out_spec = pl.BlockSpec((tm, tn), lambda *g: (gi(*g), gj(*g)))

    def body(*refs):
        a_ref, b_ref = refs[0], refs[1]
        r_ref = refs[2] if res is not None else None
        o_ref = refs[3] if res is not None else refs[2]
        acc_ref = refs[-1] if nk > 1 else None
        part = _dot(a_ref[...].astype(BF16), b_ref[...].astype(BF16), dims)

        def finish(total):
            if r_ref is not None:
                total = total + r_ref[...]
            o_ref[...] = total.astype(o_ref.dtype)

        if nk == 1:
            finish(part)
        else:
            kk = pl.program_id(2)

            @pl.when(kk == 0)
            def _():
                acc_ref[...] = part

            @pl.when(kk > 0)
            def _():
                acc_ref[...] += part

            @pl.when(kk == nk - 1)
            def _():
                finish(acc_ref[...])

    scratch = [pltpu.VMEM((tm, tn), F32)] if nk > 1 else []
    return pl.pallas_call(
        body, name=name, grid=grid, in_specs=in_specs, out_specs=out_spec,
        out_shape=out_shape, scratch_shapes=scratch,
        compiler_params=_params(("parallel", "parallel", "arbitrary")),
    )(*operands)


def _mm_heads(a, b, *, mode, name, out_dtype=BF16, tm=512):
    if mode == 'to_lat':
        s = a.shape[0]
        r = b.shape[0]
        tm = min(tm, s)
        grid = (G_HEADS, s // tm)
        in_specs = [pl.BlockSpec((tm, HEAD), lambda h, i: (i, h)),
                    pl.BlockSpec((r, HEAD), lambda h, i: (0, h))]
        out_spec = pl.BlockSpec((None, tm, r), lambda h, i: (h, i, 0))
        out_shape = jax.ShapeDtypeStruct((G_HEADS, s, r), out_dtype)
        dims = 'nt'
    elif mode == 'from_lat':
        _, s, r = a.shape
        tm = min(tm, s)
        grid = (G_HEADS, s // tm)
        in_specs = [pl.BlockSpec((None, tm, r), lambda h, i: (h, i, 0)),
                    pl.BlockSpec((r, HEAD), lambda h, i: (0, h))]
        out_spec = pl.BlockSpec((tm, HEAD), lambda h, i: (i, h))
        out_shape = jax.ShapeDtypeStruct((s, G_W), out_dtype)
        dims = 'nn'
    else:
        _, s, r = a.shape
        grid = (G_HEADS, 1)
        in_specs = [pl.BlockSpec((None, s, r), lambda h, i: (h, 0, 0)),
                    pl.BlockSpec((s, HEAD), lambda h, i: (0, h))]
        out_spec = pl.BlockSpec((r, HEAD), lambda h, i: (0, h))
        out_shape = jax.ShapeDtypeStruct((r, G_W), out_dtype)
        dims = 'tn'

    def body(a_ref, b_ref, o_ref):
        o_ref[...] = _dot(a_ref[...].astype(BF16), b_ref[...].astype(BF16), dims).astype(o_ref.dtype)

    return pl.pallas_call(
        body, name=name, grid=grid, in_specs=in_specs, out_specs=out_spec, out_shape=out_shape,
        compiler_params=_params(("parallel", "parallel")),
    )(a, b)


def _rmsnorm(x, g, *, name, width=None, out_dtype=BF16, tm=256):
    s = x.shape[0]
    w = x.shape[1] if width is None else width
    tm = min(tm, s)

    def body(x_ref, g_ref, o_ref):
        xv = x_ref[...].astype(F32)
        rstd = lax.rsqrt(jnp.mean(xv * xv, axis=-1, keepdims=True) + EPS)
        o_ref[...] = (xv * rstd * g_ref[...]).astype(o_ref.dtype)

    return pl.pallas_call(
        body, name=name, grid=(s // tm,),
        in_specs=[pl.BlockSpec((tm, w), lambda i: (i, 0)), pl.BlockSpec((1, w), lambda i: (0, 0))],
        out_specs=pl.BlockSpec((tm, w), lambda i: (i, 0)),
        out_shape=jax.ShapeDtypeStruct((s, w), out_dtype),
        compiler_params=_params(("parallel",)),
    )(x, g.reshape(1, w))


def _rmsnorm_bwd(x, g, dy, *, name, width=None, dres=None, out_dtype=F32, tm=256):
    s = x.shape[0]
    w = x.shape[1] if width is None else width
    tm = min(tm, s)

    def body(*refs):
        x_ref, g_ref, dy_ref = refs[0], refs[1], refs[2]
        r_ref = refs[3] if dres is not None else None
        dx_ref, dg_ref = refs[-2], refs[-1]
        xv = x_ref[...].astype(F32)
        rstd = lax.rsqrt(jnp.mean(xv * xv, axis=-1, keepdims=True) + EPS)
        xhat = xv * rstd
        dyv = dy_ref[...].astype(F32)
        gdy = dyv * g_ref[...]
        dx = rstd * (gdy - xhat * jnp.mean(gdy * xhat, axis=-1, keepdims=True))
        if r_ref is not None:
            dx = dx + r_ref[...]
        dx_ref[...] = dx.astype(dx_ref.dtype)
        part = jnp.sum(dyv * xhat, axis=0, keepdims=True)

        @pl.when(pl.program_id(0) == 0)
        def _():
            dg_ref[...] = part

        @pl.when(pl.program_id(0) > 0)
        def _():
            dg_ref[...] += part

    row = pl.BlockSpec((tm, w), lambda i: (i, 0))
    vec = pl.BlockSpec((1, w), lambda i: (0, 0))
    in_specs = [row, vec, row] + ([row] if dres is not None else [])
    operands = [x, g.reshape(1, w), dy] + ([dres] if dres is not None else [])
    return pl.pallas_call(
        body, name=name, grid=(s // tm,), in_specs=in_specs, out_specs=[row, vec],
        out_shape=[jax.ShapeDtypeStruct((s, w), out_dtype), jax.ShapeDtypeStruct((1, w), F32)],
        compiler_params=_params(("arbitrary",)),
    )(*operands)


def _final_loss(x, target, g, *, name, tm=256):
    s, d = x.shape
    tm = min(tm, s)

    def body(x_ref, t_ref, g_ref, sq_ref, dx_ref, dg_ref):
        xv = x_ref[...]
        rstd = lax.rsqrt(jnp.mean(xv * xv, axis=-1, keepdims=True) + EPS)
        xhat = xv * rstd
        err = xhat * g_ref[...] - t_ref[...]
        dyv = err * (1.0 / d)
        gdy = dyv * g_ref[...]
        dx_ref[...] = rstd * (gdy - xhat * jnp.mean(gdy * xhat, axis=-1, keepdims=True))
        sq = jnp.sum(err * err, axis=0, keepdims=True)
        dg = jnp.sum(dyv * xhat, axis=0, keepdims=True)

        @pl.when(pl.program_id(0) == 0)
        def _():
            sq_ref[...] = sq
            dg_ref[...] = dg

        @pl.when(pl.program_id(0) > 0)
        def _():
            sq_ref[...] += sq
            dg_ref[...] += dg

    row = pl.BlockSpec((tm, d), lambda i: (i, 0))
    vec = pl.BlockSpec((1, d), lambda i: (0, 0))
    return pl.pallas_call(
        body, name=name, grid=(s // tm,), in_specs=[row, row, vec], out_specs=[vec, row, vec],
        out_shape=[jax.ShapeDtypeStruct((1, d), F32), jax.ShapeDtypeStruct((s, d), F32),
                   jax.ShapeDtypeStruct((1, d), F32)],
        compiler_params=_params(("arbitrary",)),
    )(x, target, g.reshape(1, d))


def _tril_mask():
    t = lax.broadcasted_iota(jnp.int32, (CHUNK, CHUNK), 0)
    s = lax.broadcasted_iota(jnp.int32, (CHUNK, CHUNK), 1)
    return t >= s


def _sgu_fwd(z, g_v, w_sp, b_sp_t, *, name):
    s = z.shape[0]

    def body(zu_ref, zv_ref, g_ref, w_ref, b_ref, o_ref):
        u = _gelu(zu_ref[...].astype(F32))
        gv = _gelu(zv_ref[...].astype(F32))
        rstd = lax.rsqrt(jnp.mean(gv * gv, axis=-1, keepdims=True) + EPS)
        v = (gv * rstd * g_ref[...]).astype(BF16)
        mask = _tril_mask()
        for grp in range(G_HEADS):
            cols = slice(grp * HEAD, (grp + 1) * HEAD)
            wm = jnp.where(mask, w_ref[grp], 0.0).astype(BF16)
            sv = _dot(wm, v[:, cols], 'nn') + b_ref[:, grp:grp + 1]
            o_ref[:, cols] = (u[:, cols] * sv).astype(o_ref.dtype)

    return pl.pallas_call(
        body, name=name, grid=(s // CHUNK,),
        in_specs=[pl.BlockSpec((CHUNK, G_W), lambda i: (i, 0)),
                  pl.BlockSpec((CHUNK, G_W), lambda i: (i, 1)),
                  pl.BlockSpec((1, G_W), lambda i: (0, 0)),
                  pl.BlockSpec((G_HEADS, CHUNK, CHUNK), lambda i: (0, 0, 0)),
                  pl.BlockSpec((CHUNK, G_HEADS), lambda i: (0, 0))],
        out_specs=pl.BlockSpec((CHUNK, G_W), lambda i: (i, 0)),
        out_shape=jax.ShapeDtypeStruct((s, G_W), BF16),
        compiler_params=_params(("parallel",)),
    )(z, z, g_v.reshape(1, G_W), w_sp, b_sp_t)


def _sgu_bwd(z, dmix, dqm, g_v, w_sp, b_sp_t, *, name):
    s = z.shape[0]
    zw = z.shape[1]

    def body(zu_ref, zv_ref, dm_ref, dq_ref, g_ref, w_ref, b_ref, dz_ref, dw_ref, db_ref, dg_ref):
        first = pl.program_id(0) == 0

        @pl.when(first)
        def _():
            dw_ref[...] = jnp.zeros_like(dw_ref)
            db_ref[...] = jnp.zeros_like(db_ref)
            dg_ref[...] = jnp.zeros_like(dg_ref)

        zu = zu_ref[...].astype(F32)
        zv = zv_ref[...].astype(F32)
        dmain = dm_ref[...].astype(F32)
        u = _gelu(zu)
        gv = _gelu(zv)
        rstd = lax.rsqrt(jnp.mean(gv * gv, axis=-1, keepdims=True) + EPS)
        vhat = gv * rstd
        gvec = g_ref[...]
        v = (vhat * gvec).astype(BF16)
        dsv = dmain * u
        dsv_b = dsv.astype(BF16)
        mask = _tril_mask()
        dv_parts = []
        for grp in range(G_HEADS):
            cols = slice(grp * HEAD, (grp + 1) * HEAD)
            wm = jnp.where(mask, w_ref[grp], 0.0).astype(BF16)
            sv = _dot(wm, v[:, cols], 'nn') + b_ref[:, grp:grp + 1]
            dz_ref[:, cols] = (dmain[:, cols] * sv * _gelu_grad(zu[:, cols])).astype(dz_ref.dtype)
            dwg = _dot(dsv_b[:, cols], v[:, cols], 'nt')
            dw_ref[grp] += jnp.where(mask, dwg, 0.0)
            db_ref[:, grp:grp + 1] += jnp.sum(dsv[:, cols], axis=-1, keepdims=True)
            dv_parts.append(_dot(wm, dsv_b[:, cols], 'tn'))
        dv = jnp.concatenate(dv_parts, axis=-1)
        dg_ref[...] += jnp.sum(dv * vhat, axis=0, keepdims=True)
        gdv = dv * gvec
        dgv = rstd * (gdv - vhat * jnp.mean(gdv * vhat, axis=-1, keepdims=True))
        dz_ref[:, G_W:2 * G_W] = (dgv * _gelu_grad(zv)).astype(dz_ref.dtype)
        dz_ref[:, 2 * G_W:] = dq_ref[...].astype(dz_ref.dtype)

    return pl.pallas_call(
        body, name=name, grid=(s // CHUNK,),
        in_specs=[pl.BlockSpec((CHUNK, G_W), lambda i: (i, 0)),
                  pl.BlockSpec((CHUNK, G_W), lambda i: (i, 1)),
                  pl.BlockSpec((CHUNK, G_W), lambda i: (i, 0)),
                  pl.BlockSpec((CHUNK, MEM_W), lambda i: (i, 0)),
                  pl.BlockSpec((1, G_W), lambda i: (0, 0)),
                  pl.BlockSpec((G_HEADS, CHUNK, CHUNK), lambda i: (0, 0, 0)),
                  pl.BlockSpec((CHUNK, G_HEADS), lambda i: (0, 0))],
        out_specs=[pl.BlockSpec((CHUNK, zw), lambda i: (i, 0)),
                   pl.BlockSpec((G_HEADS, CHUNK, CHUNK), lambda i: (0, 0, 0)),
                   pl.BlockSpec((CHUNK, G_HEADS), lambda i: (0, 0)),
                   pl.BlockSpec((1, G_W), lambda i: (0, 0))],
        out_shape=[jax.ShapeDtypeStruct((s, zw), BF16),
                   jax.ShapeDtypeStruct((G_HEADS, CHUNK, CHUNK), F32),
                   jax.ShapeDtypeStruct((CHUNK, G_HEADS), F32),
                   jax.ShapeDtypeStruct((1, G_W), F32)],
        compiler_params=_params(("arbitrary",)),
    )(z, z, dmix, dqm, g_v.reshape(1, G_W), w_sp, b_sp_t)


def _mem_probs(q, k):
    sc = _dot(q, k, 'nt') * (HEAD ** -0.5)
    sc = sc - jnp.max(sc, axis=-1, keepdims=True)
    e = jnp.exp(sc)
    return e / jnp.sum(e, axis=-1, keepdims=True)


def _memattn_fwd(z, kvm, main, *, qcol, name, tm=512):
    s = z.shape[0]
    m = kvm.shape[0]
    tm = min(tm, s)

    def body(q_ref, kv_ref, main_ref, o_ref):
        o_ref[:, :G_W] = main_ref[...]
        for h in range(MEM_HEADS):
            cols = slice(h * HEAD, (h + 1) * HEAD)
            k = kv_ref[:, cols]
            v = kv_ref[:, MEM_W + h * HEAD:MEM_W + (h + 1) * HEAD]
            p = _mem_probs(q_ref[:, cols], k)
            o_ref[:, G_W + h * HEAD:G_W + (h + 1) * HEAD] = _dot(p.astype(BF16), v, 'nn').astype(o_ref.dtype)

    return pl.pallas_call(
        body, name=name, grid=(s // tm,),
        in_specs=[pl.BlockSpec((tm, MEM_W), lambda i: (i, qcol)),
                  pl.BlockSpec((m, 2 * MEM_W), lambda i: (0, 0)),
                  pl.BlockSpec((tm, G_W), lambda i: (i, 0))],
        out_specs=pl.BlockSpec((tm, G_W + MEM_W), lambda i: (i, 0)),
        out_shape=jax.ShapeDtypeStruct((s, G_W + MEM_W), BF16),
        compiler_params=_params(("parallel",)),
    )(z, kvm, main)


def _memattn_bwd(z, kvm, dmix, *, qcol, name, tm=512):
    s = z.shape[0]
    m = kvm.shape[0]
    tm = min(tm, s)
    scale = HEAD ** -0.5

    def body(q_ref, kv_ref, do_ref, dq_ref, dkv_ref):
        @pl.when(pl.program_id(0) == 0)
        def _():
            dkv_ref[...] = jnp.zeros_like(dkv_ref)

        for h in range(MEM_HEADS):
            cols = slice(h * HEAD, (h + 1) * HEAD)
            vcols = slice(MEM_W + h * HEAD, MEM_W + (h + 1) * HEAD)
            q = q_ref[:, cols]
            k = kv_ref[:, cols]
            v = kv_ref[:, vcols]
            do = do_ref[:, cols]
            p = _mem_probs(q, k)
            dp = _dot(do, v, 'nt')
            ds = (p * (dp - jnp.sum(dp * p, axis=-1, keepdims=True)) * scale).astype(BF16)
            dq_ref[:, cols] = _dot(ds, k, 'nn').astype(dq_ref.dtype)
            dkv_ref[:, cols] += _dot(ds, q, 'tn')
            dkv_ref[:, vcols] += _dot(p.astype(BF16), do, 'tn')

    mo_block = G_W // MEM_W
    return pl.pallas_call(
        body, name=name, grid=(s // tm,),
        in_specs=[pl.BlockSpec((tm, MEM_W), lambda i: (i, qcol)),
                  pl.BlockSpec((m, 2 * MEM_W), lambda i: (0, 0)),
                  pl.BlockSpec((tm, MEM_W), lambda i: (i, mo_block))],
        out_specs=[pl.BlockSpec((tm, MEM_W), lambda i: (i, 0)),
                   pl.BlockSpec((m, 2 * MEM_W), lambda i: (0, 0))],
        out_shape=[jax.ShapeDtypeStruct((s, MEM_W), BF16), jax.ShapeDtypeStruct((m, 2 * MEM_W), F32)],
        compiler_params=_params(("arbitrary",)),
    )(z, kvm, dmix)


def _rope(x1, x2, cos, sin, *, name, inverse=False, out_dtype=BF16, col1=0, col2=0, tm=512):
    s, w = cos.shape
    tm = min(tm, s)
    sign = -1.0 if inverse else 1.0

    def body(a_ref, b_ref, c_ref, s_ref, o1_ref, o2_ref):
        a = a_ref[...].astype(F32)
        b = b_ref[...].astype(F32)
        c = c_ref[...]
        sn = s_ref[...] * sign
        o1_ref[...] = (a * c - b * sn).astype(o1_ref.dtype)
        o2_ref[...] = (b * c + a * sn).astype(o2_ref.dtype)

    row = pl.BlockSpec((tm, w), lambda i: (i, 0))
    return pl.pallas_call(
        body, name=name, grid=(s // tm,),
        in_specs=[pl.BlockSpec((tm, w), lambda i: (i, col1)), pl.BlockSpec((tm, w), lambda i: (i, col2)), row, row],
        out_specs=[row, row],
        out_shape=[jax.ShapeDtypeStruct((s, w), out_dtype)] * 2,
        compiler_params=_params(("parallel",)),
    )(x1, x2, cos, sin)


def _mla_scores(q1, q2, k1, k2, row_tok, kstart, tk, scale):
    sc = (_dot(q1, k1, 'nt') + _dot(q2, k2, 'nt')) * scale
    kpos = kstart + lax.broadcasted_iota(jnp.int32, (1, tk), 1)
    keep = kpos <= row_tok
    return sc, keep


def _mla_fwd(qa, qr, ckv, kr, *, name, tk=512):
    hh, s, r = qa.shape
    tq = CHUNK
    tk = min(tk, s)
    rows = hh * tq
    scale = (HEAD + ROPE_DIM) ** -0.5

    def body(qa_ref, qr_ref, ckv_ref, kr_ref, o_ref, lse_ref, m_ref, l_ref, acc_ref):
        i = pl.program_id(0)
        q1 = qa_ref[...].reshape(rows, r)
        q2 = qr_ref[...].reshape(rows, ROPE_DIM)
        row_tok = i * tq + (lax.broadcasted_iota(jnp.int32, (rows, 1), 0) & (tq - 1))
        m_ref[...] = jnp.full_like(m_ref, MASK_VALUE)
        l_ref[...] = jnp.zeros_like(l_ref)
        acc_ref[...] = jnp.zeros_like(acc_ref)

        def step(j, carry):
            kstart = pl.multiple_of(j * tk, tk)
            k1 = ckv_ref[pl.ds(kstart, tk), :]
            k2 = kr_ref[pl.ds(kstart, tk), :]
            sc, keep = _mla_scores(q1, q2, k1, k2, row_tok, kstart, tk, scale)
            sc = jnp.where(keep, sc, MASK_VALUE)
            m_old = m_ref[...]
            m_new = jnp.maximum(m_old, jnp.max(sc, axis=-1, keepdims=True))
            p = jnp.exp(sc - m_new)
            alpha = jnp.exp(m_old - m_new)
            l_ref[...] = alpha * l_ref[...] + jnp.sum(p, axis=-1, keepdims=True)
            acc_ref[...] = alpha * acc_ref[...] + _dot(p.astype(BF16), k1, 'nn')
            m_ref[...] = m_new
            return carry

        lax.fori_loop(0, (i * tq) // tk + 1, step, 0)
        l = l_ref[...]
        o_ref[...] = (acc_ref[...] / l).astype(o_ref.dtype).reshape(hh, tq, r)
        lse_ref[...] = (m_ref[...] + jnp.log(l)).reshape(hh, tq, 1)

    return pl.pallas_call(
        body, name=name, grid=(s // tq,),
        in_specs=[pl.BlockSpec((hh, tq, r), lambda i: (0, i, 0)),
                  pl.BlockSpec((hh, tq, ROPE_DIM), lambda i: (0, i, 0)),
                  pl.BlockSpec((s, r), lambda i: (0, 0)),
                  pl.BlockSpec((s, ROPE_DIM), lambda i: (0, 0))],
        out_specs=[pl.BlockSpec((hh, tq, r), lambda i: (0, i, 0)),
                   pl.BlockSpec((hh, tq, 1), lambda i: (0, i, 0))],
        out_shape=[jax.ShapeDtypeStruct((hh, s, r), BF16), jax.ShapeDtypeStruct((hh, s, 1), F32)],
        scratch_shapes=[pltpu.VMEM((rows, 1), F32), pltpu.VMEM((rows, 1), F32), pltpu.VMEM((rows, r), F32)],
        compiler_params=_params(("parallel",)),
    )(qa, qr, ckv, kr)


def _mla_bwd(qa, qr, ckv, kr, o, do, lse, *, name, tk=256):
    hh, s, r = qa.shape
    tq = CHUNK
    tk = min(tk, s)
    rows = hh * tq
    nq = s // tq
    scale = (HEAD + ROPE_DIM) ** -0.5

    def body(qa_ref, qr_ref, o_ref, do_ref, lse_ref, ckv_hbm, kr_hbm, dqa_ref, dqr_ref, dckv_hbm, dkr_hbm,
             ckv_ref, kr_ref, dckv_ref, dkr_ref, dq1_ref, dq2_ref, sem):
        i = pl.program_id(0)

        @pl.when(i == 0)
        def _():
            c1 = pltpu.make_async_copy(ckv_hbm, ckv_ref, sem.at[0])
            c2 = pltpu.make_async_copy(kr_hbm, kr_ref, sem.at[1])
            c1.start()
            c2.start()
            dckv_ref[...] = jnp.zeros_like(dckv_ref)
            dkr_ref[...] = jnp.zeros_like(dkr_ref)
            c1.wait()
            c2.wait()

        q1 = qa_ref[...].reshape(rows, r)
        q2 = qr_ref[...].reshape(rows, ROPE_DIM)
        dov = do_ref[...].reshape(rows, r)
        delta = jnp.sum(dov.astype(F32) * o_ref[...].reshape(rows, r).astype(F32), axis=-1, keepdims=True)
        lsev = lse_ref[...].reshape(rows, 1)
        row_tok = i * tq + (lax.broadcasted_iota(jnp.int32, (rows, 1), 0) & (tq - 1))
        dq1_ref[...] = jnp.zeros_like(dq1_ref)
        dq2_ref[...] = jnp.zeros_like(dq2_ref)

        def step(j, carry):
            kstart = pl.multiple_of(j * tk, tk)
            k1 = ckv_ref[pl.ds(kstart, tk), :]
            k2 = kr_ref[pl.ds(kstart, tk), :]
            sc, keep = _mla_scores(q1, q2, k1, k2, row_tok, kstart, tk, scale)
            p = jnp.where(keep, jnp.exp(sc - lsev), 0.0)
            dp = _dot(dov, k1, 'nt')
            ds = (p * (dp - delta) * scale).astype(BF16)
            pb = p.astype(BF16)
            dq1_ref[...] += _dot(ds, k1, 'nn')
            dq2_ref[...] += _dot(ds, k2, 'nn')
            dckv_ref[pl.ds(kstart, tk), :] += _dot(ds, q1, 'tn') + _dot(pb, dov, 'tn')
            dkr_ref[pl.ds(kstart, tk), :] += _dot(ds, q2, 'tn')
            return carry

        lax.fori_loop(0, (i * tq) // tk + 1, step, 0)
        dqa_ref[...] = dq1_ref[...].astype(dqa_ref.dtype).reshape(hh, tq, r)
        dqr_ref[...] = dq2_ref[...].astype(dqr_ref.dtype).reshape(hh, tq, ROPE_DIM)

        @pl.when(i == nq - 1)
        def _():
            c1 = pltpu.make_async_copy(dckv_ref, dckv_hbm, sem.at[0])
            c2 = pltpu.make_async_copy(dkr_ref, dkr_hbm, sem.at[1])
            c1.start()
            c2.start()
            c1.wait()
            c2.wait()

    blk = pl.BlockSpec((hh, tq, r), lambda i: (0, i, 0))
    blk_r = pl.BlockSpec((hh, tq, ROPE_DIM), lambda i: (0, i, 0))
    any_spec = pl.BlockSpec(memory_space=pl.ANY)
    return pl.pallas_call(
        body, name=name, grid=(nq,),
        in_specs=[blk, blk_r, blk, blk, pl.BlockSpec((hh, tq, 1), lambda i: (0, i, 0)), any_spec, any_spec],
        out_specs=[blk, blk_r, any_spec, any_spec],
        out_shape=[jax.ShapeDtypeStruct((hh, s, r), BF16), jax.ShapeDtypeStruct((hh, s, ROPE_DIM), BF16),
                   jax.ShapeDtypeStruct((s, r), F32), jax.ShapeDtypeStruct((s, ROPE_DIM), F32)],
        scratch_shapes=[pltpu.VMEM((s, r), BF16), pltpu.VMEM((s, ROPE_DIM), BF16),
                        pltpu.VMEM((s, r), F32), pltpu.VMEM((s, ROPE_DIM), F32),
                        pltpu.VMEM((rows, r), F32), pltpu.VMEM((rows, ROPE_DIM), F32),
                        pltpu.SemaphoreType.DMA((2,))],
        compiler_params=_params(("arbitrary",)),
    )(qa, qr, o, do, lse, ckv, kr)


HALO = 16


def _shift_down(prev, cur, shift, first_tile):
    tr = cur.shape[0]
    full = jnp.concatenate([prev, cur], axis=0)
    out = pltpu.roll(full, shift, axis=0)[HALO:]
    row = lax.broadcasted_iota(jnp.int32, (tr, 1), 0)
    return jnp.where(jnp.logical_and(first_tile, row < shift), 0.0, out)


def _shift_up(cur, nxt, shift, last_tile):
    tr = cur.shape[0]
    full = jnp.concatenate([cur, nxt], axis=0)
    out = pltpu.roll(full, tr + HALO - shift, axis=0)[:tr]
    row = lax.broadcasted_iota(jnp.int32, (tr, 1), 0)
    return jnp.where(jnp.logical_and(last_tile, row >= tr - shift), 0.0, out)


def _conv_taps(prev_ref, cur_ref, cw_ref, cb_ref, first_tile):
    cur = cur_ref[...].astype(F32)
    prev = prev_ref[...].astype(F32)
    a1 = _shift_down(prev, cur, 1, first_tile)
    a2 = _shift_down(prev, cur, 2, first_tile)
    c = a2 * cw_ref[0:1, :] + a1 * cw_ref[1:2, :] + cur * cw_ref[2:3, :] + cb_ref[...]
    return c, (a2, a1, cur)


def _conv_in_specs(tr, bw, half, layer, row_of, blk_of):
    per = tr // HALO
    specs = []
    for off in (0, half):
        specs.append(pl.BlockSpec((None, HALO, bw), lambda *g, off=off: (blk_of(*g) + off, jnp.maximum(row_of(*g) * per - 1, 0), 0)))
        specs.append(pl.BlockSpec((None, tr, bw), lambda *g, off=off: (blk_of(*g) + off, row_of(*g), 0)))
    for off in (0, half):
        specs.append(pl.BlockSpec((None, None, CONV_W, bw), lambda *g, off=off: (blk_of(*g) + off, layer, 0, 0)))
    for off in (0, half):
        specs.append(pl.BlockSpec((None, 1, bw), lambda *g, off=off: (layer * 2 * half + blk_of(*g) + off, 0, 0)))
    return specs


def _conv_fwd(a, cw, cb, layer, *, name, tr=256):
    nb, s, bw = a.shape
    half = nb // 2
    tr = min(tr, s)

    def body(gp_ref, gc_ref, vp_ref, vc_ref, cwg_ref, cwv_ref, cbg_ref, cbv_ref, o_ref):
        first = pl.program_id(0) == 0
        gate, _ = _conv_taps(gp_ref, gc_ref, cwg_ref, cbg_ref, first)
        val, _ = _conv_taps(vp_ref, vc_ref, cwv_ref, cbv_ref, first)
        o_ref[...] = (gate * _sigmoid(gate) * val).astype(o_ref.dtype)

    return pl.pallas_call(
        body, name=name, grid=(s // tr, half),
        in_specs=_conv_in_specs(tr, bw, half, layer, lambda i, j: i, lambda i, j: j),
        out_specs=pl.BlockSpec((None, tr, bw), lambda i, j: (j, i, 0)),
        out_shape=jax.ShapeDtypeStruct((half, s, bw), BF16),
        compiler_params=_params(("parallel", "parallel")),
    )(a, a, a, a, cw, cw, cb, cb)


def _conv_bwd_dc(a, dact, cw, cb, layer, *, name, tr=256):
    nb, s, bw = a.shape
    half = nb // 2
    tr = min(tr, s)

    def body(gp_ref, gc_ref, vp_ref, vc_ref, cwg_ref, cwv_ref, cbg_ref, cbv_ref, da_ref,
             dc_ref, dw_ref, db_ref):
        first = pl.program_id(1) == 0
        gate, gtaps = _conv_taps(gp_ref, gc_ref, cwg_ref, cbg_ref, first)
        val, vtaps = _conv_taps(vp_ref, vc_ref, cwv_ref, cbv_ref, first)
        dact_v = da_ref[...].astype(F32)
        sg = _sigmoid(gate)
        dgate = dact_v * val * (sg * (1.0 + gate * (1.0 - sg)))
        dval = dact_v * (gate * sg)
        dc_ref[0] = dgate.astype(dc_ref.dtype)
        dc_ref[1] = dval.astype(dc_ref.dtype)

        @pl.when(first)
        def _():
            dw_ref[...] = jnp.zeros_like(dw_ref)
            db_ref[...] = jnp.zeros_like(db_ref)

        for kk in range(CONV_W):
            dw_ref[0, kk:kk + 1, :] += jnp.sum(dgate * gtaps[kk], axis=0, keepdims=True)
            dw_ref[1, kk:kk + 1, :] += jnp.sum(dval * vtaps[kk], axis=0, keepdims=True)
        db_ref[0] += jnp.sum(dgate, axis=0, keepdims=True)
        db_ref[1] += jnp.sum(dval, axis=0, keepdims=True)

    outs = pl.pallas_call(
        body, name=name, grid=(half, s // tr),
        in_specs=_conv_in_specs(tr, bw, half, layer, lambda j, i: i, lambda j, i: j)
        + [pl.BlockSpec((None, tr, bw), lambda j, i: (j, i, 0))],
        out_specs=[pl.BlockSpec((2, None, tr, bw), lambda j, i: (0, j, i, 0)),
                   pl.BlockSpec((2, None, CONV_W, bw), lambda j, i: (0, j, 0, 0)),
                   pl.BlockSpec((2, None, 1, bw), lambda j, i: (0, j, 0, 0))],
        out_shape=[jax.ShapeDtypeStruct((2, half, s, bw), BF16),
                   jax.ShapeDtypeStruct((2, half, CONV_W, bw), F32),
                   jax.ShapeDtypeStruct((2, half, 1, bw), F32)],
        compiler_params=_params(("parallel", "arbitrary")),
    )(a, a, a, a, cw, cw, cb, cb, dact)
    dc, dw, db = outs
    return dc.reshape(nb, s, bw), dw.reshape(nb, CONV_W, bw), db.reshape(nb, 1, bw)


def _conv_bwd_da(dc, cw, layer, *, name, tr=256):
    nb, s, bw = dc.shape
    tr = min(tr, s)
    ni = s // tr
    per = tr // HALO
    last_halo = s // HALO - 1

    def body(c_ref, n_ref, w_ref, o_ref):
        last = pl.program_id(0) == ni - 1
        cur = c_ref[...].astype(F32)
        nxt = n_ref[...].astype(F32)
        da = (cur * w_ref[2:3, :] + _shift_up(cur, nxt, 1, last) * w_ref[1:2, :]
              + _shift_up(cur, nxt, 2, last) * w_ref[0:1, :])
        o_ref[...] = da.astype(o_ref.dtype)

    tile = pl.BlockSpec((None, tr, bw), lambda i, j: (j, i, 0))
    return pl.pallas_call(
        body, name=name, grid=(ni, nb),
        in_specs=[tile,
                  pl.BlockSpec((None, HALO, bw), lambda i, j: (j, jnp.minimum((i + 1) * per, last_halo), 0)),
                  pl.BlockSpec((None, None, CONV_W, bw), lambda i, j: (j, layer, 0, 0))],
        out_specs=tile,
        out_shape=jax.ShapeDtypeStruct((nb, s, bw), BF16),
        compiler_params=_params(("parallel", "parallel")),
    )(dc, dc, cw)


def _rope_tables(positions):
    inv = 1.0 / (ROPE_THETA ** (jnp.arange(0, ROPE_DIM, 2, dtype=F32) / ROPE_DIM))
    ang = positions.astype(F32)[:, None] * inv
    return jnp.cos(ang), jnp.sin(ang)


def _heads_to_major(r1, r2):
    s = r1.shape[0]
    both = jnp.concatenate([r1.reshape(s, G_HEADS, ROPE_HALF), r2.reshape(s, G_HEADS, ROPE_HALF)], axis=-1)
    return both.transpose(1, 0, 2)


def _heads_from_major(qr):
    s = qr.shape[1]
    t = qr.transpose(1, 0, 2)
    return t[:, :, :ROPE_HALF].reshape(s, G_HEADS * ROPE_HALF), t[:, :, ROPE_HALF:].reshape(s, G_HEADS * ROPE_HALF)


def _local_step(x, mem, positions, target, w):
    s, d = x.shape
    n_b = DEPTH - N_A
    tm = min(1024, s)
    cos, sin = _rope_tables(positions)
    cos12 = jnp.tile(cos, (1, G_HEADS))
    sin12 = jnp.tile(sin, (1, G_HEADS))
    r1_col = G_W // (G_HEADS * ROPE_HALF)
    b_sp_t = w['b_sp'].transpose(0, 2, 1)
    bw = w['w_ffn_up'].shape[-1]

    saved = []
    memn, kvm = [], []
    for l in range(DEPTH):
        mn = _rmsnorm(mem, w['g_mem'][l], name=f"memnorm{l}")
        memn.append(mn)
        kvm.append(_mm(mn, w['w_mem_kv'], dims='nn', b_lead=(l,), out_dtype=BF16, name=f"memkv{l}", tm=tm, tn=1024))

    kv = None
    for l in range(DEPTH):
        sv = {'x_in': x}
        if l == N_A:
            xn_kv = _rmsnorm(x, w['g_kv'], name="kvnorm")
            kvx = _mm(xn_kv, w['w_kv_a'], dims='nn', out_dtype=F32, name="kvproj", tm=tm, tn=KV_PAD)
            ckv = _rmsnorm(kvx, w['g_kv_lat'], width=KV_RANK, name="ckvnorm")
            k1, k2 = _rope(kvx[:, KV_RANK:KV_RANK + ROPE_HALF], kvx[:, KV_RANK + ROPE_HALF:KV_RANK + ROPE_DIM],
                           cos, sin, name="krope")
            kr = jnp.concatenate([k1, k2], axis=-1)
            kv = {'x': x, 'xn': xn_kv, 'kvx': kvx, 'ckv': ckv, 'kr': kr}
        h = _rmsnorm(x, w['g_mix'][l], name=f"mixnorm{l}")
        sv['h'] = h
        if l < N_A:
            z = _mm(h, w['w_in_a'], dims='nn', b_lead=(l,), out_dtype=BF16, name=f"in_a{l}", tm=tm, tn=512)
            main = _sgu_fwd(z, w['g_v'][l], w['w_sp'][l], b_sp_t[l], name=f"sgu{l}")
            qcol = 2 * G_W // MEM_W
        else:
            j = l - N_A
            z = _mm(h, w['w_in_b'], dims='nn', b_lead=(j,), out_dtype=BF16, name=f"in_b{j}", tm=tm, tn=1024)
            qn = _rmsnorm(z, w['g_q_lat'][j], width=Q_RANK, name=f"qnorm{j}")
            qp = _mm(qn, w['w_uqp'], dims='nn', b_lead=(j,), out_dtype=BF16, name=f"uq{j}", tm=tm, tn=768)
            rr1, rr2 = _rope(qp, qp, cos12, sin12, col1=r1_col, col2=r1_col + 1, name=f"qrope{j}")
            qr = _heads_to_major(rr1, rr2)
            qa = _mm_heads(qp, w['w_uk'][j], mode='to_lat', name=f"qabsorb{j}")
            o_lat, lse = _mla_fwd(qa, qr, kv['ckv'], kv['kr'], name=f"mla{j}")
            main = _mm_heads(o_lat, w['w_uv'][j], mode='from_lat', name=f"uv{j}")
            qcol = Q_RANK // MEM_W
            sv.update(qn=qn, qp=qp, qr=qr, qa=qa, o_lat=o_lat, lse=lse)
        mix = _memattn_fwd(z, kvm[l], main, qcol=qcol, name=f"memattn{l}")
        x_mid = _mm(mix, w['w_out'], dims='nn', b_lead=(l,), res=x, out_dtype=F32, name=f"out{l}", tm=tm, tn=1024)
        h2 = _rmsnorm(x_mid, w['g_ffn'][l], name=f"ffnnorm{l}")
        a = _mm(h2, w['w_ffn_up'], dims='nn', b_lead=(l,), b_blocked=True, out_dtype=BF16, out_block=bw,
                name=f"up{l}", tm=tm, tn=bw)
        act = _conv_fwd(a, w['conv_w'], w['conv_b'], l, name=f"conv{l}")
        x = _mm(act, w['w_ffn_down'], dims='nn', a_blocked=True, b_lead=(l,), tk=bw, res=x_mid, out_dtype=F32,
                name=f"down{l}", tm=tm, tn=1024)
        sv.update(z=z, qcol=qcol, mix=mix, x_mid=x_mid, h2=h2, a=a, act=act)
        saved.append(sv)

    sq, dx, dg_final = _final_loss(x, target, w['g_final'], name="loss")

    g = {k: [None] * DEPTH for k in ('g_mix', 'g_ffn', 'g_mem', 'w_mem_kv', 'w_out', 'w_ffn_up', 'conv_w', 'conv_b', 'w_ffn_down')}
    for k in ('w_in_a', 'g_v', 'w_sp', 'b_sp'):
        g[k] = [None] * N_A
    for k in ('w_in_b', 'g_q_lat', 'w_uqp', 'w_uk', 'w_uv'):
        g[k] = [None] * n_b
    g['g_final'] = dg_final
    dckv_sum, dkr_sum = None, None

    for l in reversed(range(DEPTH)):
        sv = saved[l]
        dact = _mm(dx, w['w_ffn_down'], dims='nt', b_lead=(l,), out_dtype=BF16, out_block=bw,
                   name=f"d_act{l}", tm=tm, tn=bw)
        g['w_ffn_down'][l] = _mm(sv['act'], dx, dims='tn', a_blocked=True, out_dtype=BF16,
                                 name=f"dw_down{l}", tm=bw, tn=256)
        dc, dcw, dcb = _conv_bwd_dc(sv['a'], dact, w['conv_w'], w['conv_b'], l, name=f"d_conv{l}")
        g['conv_w'][l], g['conv_b'][l] = dcw, dcb
        da = _conv_bwd_da(dc, w['conv_w'], l, name=f"d_convin{l}")
        dh2 = _mm(da, w['w_ffn_up'], dims='nt', a_blocked=True, b_lead=(l,), b_blocked=True, tk=bw,
                  out_dtype=BF16, name=f"d_h2{l}", tm=tm, tn=1024)
        g['w_ffn_up'][l] = _mm(sv['h2'], da, dims='tn', b_blocked=True, out_dtype=BF16, out_block=bw,
                               name=f"dw_up{l}", tm=512, tn=bw, n_outer=True)
        dx_mid, g['g_ffn'][l] = _rmsnorm_bwd(sv['x_mid'], w['g_ffn'][l], dh2, dres=dx, name=f"d_ffnnorm{l}")
        dmix = _mm(dx_mid, w['w_out'], dims='nt', b_lead=(l,), out_dtype=BF16, name=f"d_mix{l}", tm=tm, tn=1024)
        g['w_out'][l] = _mm(sv['mix'], dx_mid, dims='tn', out_dtype=BF16, name=f"dw_out{l}", tm=1024, tn=256)
        dqm, dkvm = _memattn_bwd(sv['z'], kvm[l], dmix, qcol=sv['qcol'], name=f"d_memattn{l}")
        g['w_mem_kv'][l] = _mm(memn[l], dkvm, dims='tn', out_dtype=BF16, name=f"dw_memkv{l}", tm=1024, tn=1024)
        dmemn = _mm(dkvm, w['w_mem_kv'], dims='nt', b_lead=(l,), out_dtype=F32, name=f"d_memn{l}", tm=tm, tn=1024)
        _, g['g_mem'][l] = _rmsnorm_bwd(mem, w['g_mem'][l], dmemn, out_dtype=BF16, name=f"d_memnorm{l}")
        if l < N_A:
            dz, dwsp, dbsp_t, dgv = _sgu_bwd(sv['z'], dmix, dqm, w['g_v'][l], w['w_sp'][l], b_sp_t[l], name=f"d_sgu{l}")
            g['w_sp'][l], g['b_sp'][l], g['g_v'][l] = dwsp, dbsp_t.T, dgv
            dh = _mm(dz, w['w_in_a'], dims='nt', b_lead=(l,), out_dtype=BF16, name=f"d_h_a{l}", tm=tm, tn=1024)
            g['w_in_a'][l] = _mm(sv['h'], dz, dims='tn', out_dtype=BF16, name=f"dw_in_a{l}", tm=1024, tn=512)
        else:
            j = l - N_A
            do_lat = _mm_heads(dmix, w['w_uv'][j], mode='to_lat', name=f"d_olat{j}")
            g['w_uv'][j] = _mm_heads(sv['o_lat'], dmix, mode='wgrad', name=f"dw_uv{j}")
            dqa, dqr, dckv, dkr = _mla_bwd(sv['qa'], sv['qr'], kv['ckv'], kv['kr'], sv['o_lat'], do_lat, sv['lse'],
                                           name=f"d_mla{j}")
            dckv_sum = dckv if dckv_sum is None else dckv_sum + dckv
            dkr_sum = dkr if dkr_sum is None else dkr_sum + dkr
            dq_nope = _mm_heads(dqa, w['w_uk'][j], mode='from_lat', name=f"d_qnope{j}")
            g['w_uk'][j] = _mm_heads(dqa, sv['qp'], mode='wgrad', name=f"dw_uk{j}")
            dr1, dr2 = _heads_from_major(dqr)
            dq1, dq2 = _rope(dr1, dr2, cos12, sin12, inverse=True, name=f"d_qrope{j}")
            dqp = jnp.concatenate([dq_nope, dq1, dq2], axis=-1)
            dqn = _mm(dqp, w['w_uqp'], dims='nt', b_lead=(j,), out_dtype=BF16, name=f"d_qn{j}", tm=tm, tn=512)
            g['w_uqp'][j] = _mm(sv['qn'], dqp, dims='tn', out_dtype=BF16, name=f"dw_uq{j}", tm=512, tn=768)
            dqlat, g['g_q_lat'][j] = _rmsnorm_bwd(sv['z'], w['g_q_lat'][j], dqn, width=Q_RANK, out_dtype=BF16,
                                                 name=f"d_qnorm{j}")
            dz = jnp.concatenate([dqlat, dqm], axis=-1)
            dh = _mm(dz, w['w_in_b'], dims='nt', b_lead=(j,), out_dtype=BF16, name=f"d_h_b{j}", tm=tm, tn=1024)
            g['w_in_b'][j] = _mm(sv['h'], dz, dims='tn', out_dtype=BF16, name=f"dw_in_b{j}", tm=1024, tn=512)
        dx, g['g_mix'][l] = _rmsnorm_bwd(sv['x_in'], w['g_mix'][l], dh, dres=dx_mid, name=f"d_mixnorm{l}")
        if l == N_A:
            dkvx_c, g['g_kv_lat'] = _rmsnorm_bwd(kv['kvx'], w['g_kv_lat'], dckv_sum, width=KV_RANK, out_dtype=BF16,
                                                 name="d_ckvnorm")
            dk1, dk2 = _rope(dkr_sum[:, :ROPE_HALF], dkr_sum[:, ROPE_HALF:], cos, sin, inverse=True, name="d_krope")
            dkvx = jnp.concatenate([dkvx_c, dk1, dk2, jnp.zeros((s, KV_PAD - KV_RANK - ROPE_DIM), BF16)], axis=-1)
            dxn = _mm(dkvx, w['w_kv_a'], dims='nt', out_dtype=BF16, name="d_kvnorm_in", tm=tm, tn=1024)
            g['w_kv_a'] = _mm(kv['xn'], dkvx, dims='tn', out_dtype=BF16, name="dw_kv", tm=1024, tn=KV_PAD)
            dx, g['g_kv'] = _rmsnorm_bwd(kv['x'], w['g_kv'], dxn, dres=dx, name="d_kvnorm")
    return jnp.sum(sq), dx, g


MESH_IDS = pl.DeviceIdType.MESH
PEER_MASKS = tuple((k >> 2 & 1, k >> 1 & 1, k & 1) for k in range(1, N_DEV))


def _my_position():
    return lax.axis_index("x"), lax.axis_index("y"), lax.axis_index("c")


def _flip(pos, mask):
    return tuple(1 - p if f else p for p, f in zip(pos, mask))


def _linear_id(pos):
    return 4 * pos[0] + 2 * pos[1] + pos[2]


def _block_of(ref, mode, idx):
    if mode == 'stack':
        return ref.at[idx]
    r = ref.shape[1] // N_DEV
    return ref.at[:, pl.ds(pl.multiple_of(idx * r, 8), r), :]


def _exchange(arrays, modes, *, name, scatter):
    n = len(arrays)
    out_shapes = []
    for arr, mode in zip(arrays, modes):
        if scatter:
            blk = arr.shape[1:] if mode == 'stack' else (arr.shape[0], arr.shape[1] // N_DEV, arr.shape[2])
            out_shapes.append(jax.ShapeDtypeStruct((N_DEV,) + tuple(blk), arr.dtype))
        elif mode == 'stack':
            out_shapes.append(jax.ShapeDtypeStruct((N_DEV,) + arr.shape, arr.dtype))
        else:
            out_shapes.append(jax.ShapeDtypeStruct((arr.shape[0], arr.shape[1] * N_DEV, arr.shape[2]), arr.dtype))
    n_peer = N_DEV - 1

    def body(*refs):
        srcs, outs = refs[:n], refs[n:2 * n]
        send_sems, recv_sems, local_sems = refs[2 * n:]
        me = _my_position()
        my_id = _linear_id(me)
        started = []
        for a in range(n):
            mode = modes[a]
            if scatter:
                own = pltpu.make_async_copy(_block_of(srcs[a], mode, my_id), outs[a].at[my_id], local_sems.at[a])
            else:
                own = pltpu.make_async_copy(srcs[a], _block_of(outs[a], mode, my_id), local_sems.at[a])
            own.start()
            started.append(own)
        waits = []
        for a in range(n):
            mode = modes[a]
            for k, mask in enumerate(PEER_MASKS):
                peer = _flip(me, mask)
                peer_id = _linear_id(peer)
                if scatter:
                    src = _block_of(srcs[a], mode, peer_id)
                    dst = outs[a].at[my_id]
                    landing = outs[a].at[peer_id]
                else:
                    src = srcs[a]
                    dst = _block_of(outs[a], mode, my_id)
                    landing = _block_of(outs[a], mode, peer_id)
                sem = a * n_peer + k
                send = pltpu.make_async_remote_copy(src_ref=src, dst_ref=dst, send_sem=send_sems.at[sem],
                                                    recv_sem=recv_sems.at[sem], device_id=peer, device_id_type=MESH_IDS)
                send.start()
                recv = pltpu.make_async_remote_copy(src_ref=src, dst_ref=landing, send_sem=send_sems.at[sem],
                                                    recv_sem=recv_sems.at[sem], device_id=peer, device_id_type=MESH_IDS)
                waits.append((send, recv))
        for send, recv in waits:
            recv.wait_recv()
        for send, recv in waits:
            send.wait_send()
        for own in started:
            own.wait()

    any_spec = pl.BlockSpec(memory_space=pl.ANY)
    outs = pl.pallas_call(
        body, name=name, in_specs=[any_spec] * n, out_specs=[any_spec] * n, out_shape=out_shapes,
        scratch_shapes=[pltpu.SemaphoreType.DMA((n * n_peer,)), pltpu.SemaphoreType.DMA((n * n_peer,)),
                        pltpu.SemaphoreType.DMA((n,))],
    )(*arrays)
    return list(outs)


def _sum_slots(parts_ref):
    total = parts_ref[0].astype(F32)
    for q in range(1, parts_ref.shape[0]):
        total = total + parts_ref[q].astype(F32)
    return total


def _row_tile(rows, cols, n_arrays):
    budget = (12 * 1024 * 1024) // (4 * n_arrays * max(cols, 128))
    t = rows
    while t > budget and t % 2 == 0 and (t // 2) % 16 == 0:
        t //= 2
    return t


def _sum_adam(parts, w, m, v, *, name):
    q, r, c = parts.shape
    tr = _row_tile(r, c, q + 7)
    c1 = 1.0 - ADAM_B1 ** ADAM_STEP
    c2 = 1.0 - ADAM_B2 ** ADAM_STEP

    def body(p_ref, w_ref, m_ref, v_ref, g_ref, d_ref, mo_ref, vo_ref):
        grad = _sum_slots(p_ref)
        m_new = ADAM_B1 * m_ref[...] + (1.0 - ADAM_B1) * grad
        v_new = ADAM_B2 * v_ref[...] + (1.0 - ADAM_B2) * (grad * grad)
        m_hat = m_new / c1
        v_hat = v_new / c2
        g_ref[...] = grad
        d_ref[...] = -ADAM_LR * (m_hat / (jnp.sqrt(v_hat) + ADAM_EPS) + ADAM_WD * w_ref[...])
        mo_ref[...] = m_new
        vo_ref[...] = v_new

    tile = pl.BlockSpec((tr, c), lambda i: (i, 0))
    return pl.pallas_call(
        body, name=name, grid=(r // tr,),
        in_specs=[pl.BlockSpec((q, tr, c), lambda i: (0, i, 0)), tile, tile, tile],
        out_specs=[tile] * 4, out_shape=[jax.ShapeDtypeStruct((r, c), F32)] * 4,
        compiler_params=_params(("parallel",)),
    )(parts, w, m, v)


def _sum_parts(parts, *, name):
    q, r, c = parts.shape

    def body(p_ref, o_ref):
        o_ref[...] = _sum_slots(p_ref)

    return pl.pallas_call(
        body, name=name, in_specs=[pl.BlockSpec((q, r, c), lambda: (0, 0, 0))],
        out_specs=pl.BlockSpec((r, c), lambda: (0, 0)), out_shape=jax.ShapeDtypeStruct((r, c), F32),
        compiler_params=_params(),
    )(parts)


INPUT_NAMES = (['x', 'mem', 'positions'] + WEIGHTS + ['loss_target'] + ['m_' + n for n in WEIGHTS]
               + ['v_' + n for n in WEIGHTS])
SMALL_ALIGN = N_DEV * 8 * 128


def _as2d(a):
    return a.reshape(-1, a.shape[-1])


def _permute_uq(w_uq):
    l, r, _ = w_uq.shape
    q = w_uq.reshape(l, r, G_HEADS, HEAD + ROPE_DIM)
    return jnp.concatenate([q[..., :HEAD].reshape(l, r, -1), q[..., HEAD:HEAD + ROPE_HALF].reshape(l, r, -1),
                            q[..., HEAD + ROPE_HALF:].reshape(l, r, -1)], axis=-1)


def _unpermute_uq(w_uqp):
    l, r, _ = w_uqp.shape
    nope = w_uqp[..., :G_W].reshape(l, r, G_HEADS, HEAD)
    r1 = w_uqp[..., G_W:G_W + G_HEADS * ROPE_HALF].reshape(l, r, G_HEADS, ROPE_HALF)
    r2 = w_uqp[..., G_W + G_HEADS * ROPE_HALF:].reshape(l, r, G_HEADS, ROPE_HALF)
    return jnp.concatenate([nope, r1, r2], axis=-1).reshape(l, r, -1)


def _cols_from_stack(st):
    _, l, r, n = st.shape
    return st.transpose(1, 2, 0, 3).reshape(l, r, N_DEV * n)


def _cols_to_stack(wh):
    l, r, c = wh.shape
    return wh.reshape(l, r, N_DEV, c // N_DEV).transpose(2, 0, 1, 3)


def _step(args):
    p = dict(zip(INPUT_NAMES, args))
    x, mem, positions, target = p['x'][0], p['mem'][0], p['positions'][0], p['loss_target'][0]
    d = x.shape[-1]
    my_id = _linear_id(_my_position())

    w_kv_pad = jnp.pad(p['w_kv_a'], ((0, 0), (0, KV_PAD - p['w_kv_a'].shape[1])))
    shard = {
        'w_in_a': p['w_in_a'].astype(BF16), 'w_uq': p['w_uq'].astype(BF16), 'w_ffn_up': p['w_ffn_up'].astype(BF16),
        'conv_w': p['conv_w'], 'g_v': p['g_v'], 'w_kv_a': w_kv_pad.astype(BF16),
        'w_in_b': p['w_in_b'].astype(BF16),
        'w_uk': p['w_uk'].reshape(p['w_uk'].shape[0], p['w_uk'].shape[1], -1).astype(BF16),
        'w_uv': p['w_uv'].reshape(p['w_uv'].shape[0], p['w_uv'].shape[1], -1).astype(BF16),
        'w_mem_kv': p['w_mem_kv'].astype(BF16), 'w_out': p['w_out'].astype(BF16),
        'w_ffn_down': p['w_ffn_down'].astype(BF16),
    }
    stack_names = ['w_in_a', 'w_uq', 'w_ffn_up', 'conv_w', 'g_v', 'w_kv_a']
    row_names = ['w_in_b', 'w_uk', 'w_uv', 'w_mem_kv', 'w_out', 'w_ffn_down']
    names = stack_names + row_names
    whole = dict(zip(names, _exchange([shard[k] for k in names],
                                      ['stack'] * len(stack_names) + ['rows'] * len(row_names),
                                      name="gather_weights", scatter=False)))
    w = {k: p[k] for k in REPLICATED if k != 'conv_b'}
    w['conv_b'] = p['conv_b'].reshape(-1, 1, whole['w_ffn_up'].shape[-1])
    w['w_in_a'] = _cols_from_stack(whole['w_in_a'])
    w['w_uqp'] = _permute_uq(_cols_from_stack(whole['w_uq']))
    w['w_ffn_up'] = whole['w_ffn_up']
    w['conv_w'] = whole['conv_w']
    w['g_v'] = whole['g_v'].transpose(1, 0, 2).reshape(p['g_v'].shape[0], -1)
    w['w_kv_a'] = whole['w_kv_a'].reshape(-1, KV_PAD)
    for k in row_names:
        w[k] = whole[k]

    sq, grad_x, g = _local_step(x, mem, positions, target, w)
    loss = (0.5 / d) * lax.psum(sq, ("x", "y", "c"))

    send = {
        'w_in_a': _cols_to_stack(jnp.stack(g['w_in_a'])),
        'w_uq': _cols_to_stack(_unpermute_uq(jnp.stack(g['w_uqp']))),
        'w_ffn_up': jnp.stack(g['w_ffn_up'], axis=1),
        'w_kv_a': g['w_kv_a'][:, :p['w_kv_a'].shape[1]].reshape(N_DEV, -1, p['w_kv_a'].shape[1]),
    }
    for k in row_names:
        send[k] = jnp.stack(g[k])
    big_stack = ['w_in_a', 'w_uq', 'w_ffn_up', 'w_kv_a']
    big = big_stack + row_names
    parts = dict(zip(big, _exchange([send[k] for k in big], ['stack'] * len(big_stack) + ['rows'] * len(row_names),
                                    name="scatter_grads", scatter=True)))
    out = {}
    for k in big:
        shp = p[k].shape
        w2, m2, v2 = _as2d(p[k]), _as2d(p['m_' + k]), _as2d(p['v_' + k])
        res = _sum_adam(parts[k].reshape((N_DEV,) + w2.shape), w2, m2, v2, name=f"adam_{k}")
        out[k] = [t.reshape(shp) for t in res]

    small = {
        'g_mix': jnp.concatenate(g['g_mix']), 'g_ffn': jnp.concatenate(g['g_ffn']), 'g_final': g['g_final'],
        'w_sp': jnp.stack(g['w_sp']), 'b_sp': jnp.stack(g['b_sp']), 'g_kv': g['g_kv'], 'g_kv_lat': g['g_kv_lat'],
        'g_q_lat': jnp.concatenate(g['g_q_lat']), 'g_mem': jnp.concatenate(g['g_mem']),
        'conv_b': jnp.stack(g['conv_b']),
        'g_v': jnp.concatenate(g['g_v']),
        'conv_w': jnp.stack(g['conv_w']).transpose(0, 2, 1, 3),
    }
    small_names = REPLICATED + SMALL_SHARDED
    flat = jnp.concatenate([small[k].reshape(-1).astype(F32) for k in small_names])
    n_small = flat.shape[0]
    padded = -(-n_small // SMALL_ALIGN) * SMALL_ALIGN
    flat = jnp.pad(flat, (0, padded - n_small)).reshape(N_DEV, -1, 128)
    (small_parts,) = _exchange([flat], ['stack'], name="scatter_small", scatter=True)
    reduced = _sum_parts(small_parts, name="sum_small")
    (small_all,) = _exchange([reduced], ['stack'], name="gather_small", scatter=False)
    small_all = small_all.reshape(-1)
    grads_small, off = {}, 0
    for k in small_names:
        size = small[k].size
        grads_small[k] = small_all[off:off + size].reshape(small[k].shape)
        off += size
    grads_small['g_v'] = lax.dynamic_slice_in_dim(grads_small['g_v'], my_id * p['g_v'].shape[1], p['g_v'].shape[1], axis=1)
    grads_small['conv_w'] = lax.dynamic_index_in_dim(grads_small['conv_w'], my_id, axis=2, keepdims=False)
    gs = jnp.concatenate([grads_small[k].reshape(-1) for k in small_names])
    n_loc = gs.shape[0]
    pad_loc = -(-n_loc // 1024) * 1024 - n_loc

    def pack(prefix):
        t = jnp.concatenate([p[prefix + k].reshape(-1) for k in small_names])
        return jnp.pad(t, (0, pad_loc)).reshape(-1, 128)

    res = _sum_adam(jnp.pad(gs, (0, pad_loc)).reshape(1, -1, 128), pack(''), pack('m_'), pack('v_'), name="adam_small")
    off = 0
    for k in small_names:
        size = p[k].size
        out[k] = [t.reshape(-1)[off:off + size].reshape(p[k].shape) for t in res]
        off += size

    outs = [loss, grad_x[None]]
    for i in range(4):
        outs += [out[k][i] for k in WEIGHTS]
    return tuple(outs)


def kernel(x, mem, positions, g_mix, g_ffn, g_final, w_in_a, g_v, w_sp, b_sp, g_kv, w_kv_a, g_kv_lat, w_in_b, g_q_lat, w_uq, w_uk, w_uv, g_mem, w_mem_kv, w_out, w_ffn_up, conv_w, conv_b, w_ffn_down, loss_target, m_g_mix, m_g_ffn, m_g_final, m_w_in_a, m_g_v, m_w_sp, m_b_sp, m_g_kv, m_w_kv_a, m_g_kv_lat, m_w_in_b, m_g_q_lat, m_w_uq, m_w_uk, m_w_uv, m_g_mem, m_w_mem_kv, m_w_out, m_w_ffn_up, m_conv_w, m_conv_b, m_w_ffn_down, v_g_mix, v_g_ffn, v_g_final, v_w_in_a, v_g_v, v_w_sp, v_b_sp, v_g_kv, v_w_kv_a, v_g_kv_lat, v_w_in_b, v_g_q_lat, v_w_uq, v_w_uk, v_w_uv, v_g_mem, v_w_mem_kv, v_w_out, v_w_ffn_up, v_conv_w, v_conv_b, v_w_ffn_down):
    return _step((x, mem, positions, g_mix, g_ffn, g_final, w_in_a, g_v, w_sp, b_sp, g_kv, w_kv_a, g_kv_lat, w_in_b, g_q_lat, w_uq, w_uk, w_uv, g_mem, w_mem_kv, w_out, w_ffn_up, conv_w, conv_b, w_ffn_down, loss_target, m_g_mix, m_g_ffn, m_g_final, m_w_in_a, m_g_v, m_w_sp, m_b_sp, m_g_kv, m_w_kv_a, m_g_kv_lat, m_w_in_b, m_g_q_lat, m_w_uq, m_w_uk, m_w_uv, m_g_mem, m_w_mem_kv, m_w_out, m_w_ffn_up, m_conv_w, m_conv_b, m_w_ffn_down, v_g_mix, v_g_ffn, v_g_final, v_w_in_a, v_g_v, v_w_sp, v_b_sp, v_g_kv, v_w_kv_a, v_g_kv_lat, v_w_in_b, v_g_q_lat, v_w_uq, v_w_uk, v_w_uv, v_g_mem, v_w_mem_kv, v_w_out, v_w_ffn_up, v_conv_w, v_conv_b, v_w_ffn_down))
```

```python
import math

import jax
import jax.numpy as jnp
from jax import lax
from jax.experimental import pallas as pl
from jax.experimental.pallas import tpu as pltpu

F32 = jnp.float32
BF16 = jnp.bfloat16

N_DEV = 8
N_A = 2
DEPTH = 4
G_HEADS = 12
HEAD = 128
CHUNK = 128
MEM_HEADS = 4
MEM_W = MEM_HEADS * HEAD
G_W = G_HEADS * HEAD
ROPE_DIM = 64
ROPE_HALF = ROPE_DIM // 2
KV_RANK = 512
Q_RANK = 512
KV_PAD = 640
ROPE_THETA = 10000.0
EPS = 1e-6
CONV_W = 3

ADAM_LR = 0.001
ADAM_B1 = 0.9
ADAM_B2 = 0.999
ADAM_EPS = 1e-08
ADAM_WD = 0.01
ADAM_STEP = 10

VMEM_LIMIT_V7X = 56 * 1024 * 1024
MASK_VALUE = -1e30

WEIGHTS = ['g_mix', 'g_ffn', 'g_final', 'w_in_a', 'g_v', 'w_sp', 'b_sp', 'g_kv', 'w_kv_a', 'g_kv_lat',
           'w_in_b', 'g_q_lat', 'w_uq', 'w_uk', 'w_uv', 'g_mem', 'w_mem_kv', 'w_out', 'w_ffn_up',
           'conv_w', 'conv_b', 'w_ffn_down']
REPLICATED = ['g_mix', 'g_ffn', 'g_final', 'w_sp', 'b_sp', 'g_kv', 'g_kv_lat', 'g_q_lat', 'g_mem', 'conv_b']
SMALL_SHARDED = ['g_v', 'conv_w']


def _params(sem=None):
    return pltpu.CompilerParams(dimension_semantics=sem, vmem_limit_bytes=VMEM_LIMIT_V7X)


def _dot(a, b, dims):
    contract = {'nn': ((1,), (0,)), 'nt': ((1,), (1,)), 'tn': ((0,), (0,))}[dims]
    return lax.dot_general(a, b, (contract, ((), ())), preferred_element_type=F32)


def _erf(x):
    return lax.erf(x)


def _gelu(x):
    return 0.5 * x * (1.0 + _erf(x * (2.0 ** -0.5)))


def _gelu_grad(x):
    cdf = 0.5 * (1.0 + _erf(x * (2.0 ** -0.5)))
    pdf = jnp.exp(-0.5 * x * x) * (1.0 / math.sqrt(2.0 * math.pi))
    return cdf + x * pdf


def _sigmoid(x):
    return 1.0 / (1.0 + jnp.exp(-x))


def _operand_spec(shape, lead, blocked, tr, tc, ridx, cidx):
    if blocked:
        per = shape[-1] // tc
        assert shape[-1] % tc == 0, (shape, tc)
        return pl.BlockSpec(
            (None,) * (1 + len(lead)) + (tr, tc),
            lambda *g: (cidx(*g) // per,) + lead + (ridx(*g), cidx(*g) % per))
    return pl.BlockSpec((None,) * len(lead) + (tr, tc), lambda *g: lead + (ridx(*g), cidx(*g)))


def _view2d(x, blocked):
    return (x.shape[-2], x.shape[0] * x.shape[-1]) if blocked else (x.shape[-2], x.shape[-1])


def _mm(a, b, *, dims, out_dtype, name, tm, tn, tk=None, res=None, a_lead=(), b_lead=(),
        a_blocked=False, b_blocked=False, out_block=None, n_outer=False):
    ar, ac = _view2d(a, a_blocked)
    br, bc = _view2d(b, b_blocked)
    m, k = (ac, ar) if dims == 'tn' else (ar, ac)
    n, k2 = (br, bc) if dims == 'nt' else (bc, br)
    assert k == k2, (a.shape, b.shape, dims)
    tm, tn = min(tm, m), min(tn, n)
    tk = k if tk is None else tk
    assert m % tm == 0 and n % tn == 0 and k % tk == 0, (name, m, n, k, tm, tn, tk)
    nk = k // tk
    if n_outer:
        gi, gj = (lambda g0, g1, g2: g1), (lambda g0, g1, g2: g0)
        grid = (n // tn, m // tm, nk)
    else:
        gi, gj = (lambda g0, g1, g2: g0), (lambda g0, g1, g2: g1)
        grid = (m // tm, n // tn, nk)
    gk = lambda g0, g1, g2: g2

    if dims == 'tn':
        a_spec = _operand_spec(a.shape, a_lead, a_blocked, tk, tm, gk, gi)
    else:
        a_spec = _operand_spec(a.shape, a_lead, a_blocked, tm, tk, gi, gk)
    if dims == 'nt':
        b_spec = _operand_spec(b.shape, b_lead, b_blocked, tn, tk, gj, gk)
    else:
        b_spec = _operand_spec(b.shape, b_lead, b_blocked, tk, tn, gk, gj)
    in_specs = [a_spec, b_spec]
    operands = [a, b]
    if res is not None:
        in_specs.append(pl.BlockSpec((tm, tn), lambda *g: (gi(*g), gj(*g))))
        operands.append(res)
    if out_block is not None:
        out_shape = jax.ShapeDtypeStruct((n // out_block, m, out_block), out_dtype)
        out_spec = _operand_spec(out_shape.shape, (), True, tm, tn, gi, gj)
    else:
        out_shape = jax.ShapeDtypeStruct((m, n), out_dtype)
        out_spec = pl.BlockSpec((tm, tn), lambda *g: (gi(*g), gj(*g)))

    def body(*refs):
        a_ref, b_ref = refs[0], refs[1]
        r_ref = refs[2] if res is not None else None
        o_ref = refs[3] if res is not None else refs[2]
        acc_ref = refs[-1] if nk > 1 else None
        part = _dot(a_ref[...].astype(BF16), b_ref[...].astype(BF16), dims)

        def finish(total):
            if r_ref is not None:
                total = total + r_ref[...]
            o_ref[...] = total.astype(o_ref.dtype)

        if nk == 1:
            finish(part)
        else:
            kk = pl.program_id(2)

            @pl.when(kk == 0)
            def _():
                acc_ref[...] = part

            @pl.when(kk > 0)
            def _():
                acc_ref[...] += part

            @pl.when(kk == nk - 1)
            def _():
                finish(acc_ref[...])

    scratch = [pltpu.VMEM((tm, tn), F32)] if nk > 1 else []
    return pl.pallas_call(
        body, name=name, grid=grid, in_specs=in_specs, out_specs=out_spec,
        out_shape=out_shape, scratch_shapes=scratch,
        compiler_params=_params(("parallel", "parallel", "arbitrary")),
    )(*operands)


def _mm_heads(a, b, *, mode, name, out_dtype=BF16, tm=512):
    if mode == 'to_lat':
        s = a.shape[0]
        r = b.shape[0]
        tm = min(tm, s)
        grid = (G_HEADS, s // tm)
        in_specs = [pl.BlockSpec((tm, HEAD), lambda h, i: (i, h)),
                    pl.BlockSpec((r, HEAD), lambda h, i: (0, h))]
        out_spec = pl.BlockSpec((None, tm, r), lambda h, i: (h, i, 0))
        out_shape = jax.ShapeDtypeStruct((G_HEADS, s, r), out_dtype)
        dims = 'nt'
    elif mode == 'from_lat':
        _, s, r = a.shape
        tm = min(tm, s)
        grid = (G_HEADS, s // tm)
        in_specs = [pl.BlockSpec((None, tm, r), lambda h, i: (h, i, 0)),
                    pl.BlockSpec((r, HEAD), lambda h, i: (0, h))]
        out_spec = pl.BlockSpec((tm, HEAD), lambda h, i: (i, h))
        out_shape = jax.ShapeDtypeStruct((s, G_W), out_dtype)
        dims = 'nn'
    else:
        _, s, r = a.shape
        grid = (G_HEADS, 1)
        in_specs = [pl.BlockSpec((None, s, r), lambda h, i: (h, 0, 0)),
                    pl.BlockSpec((s, HEAD), lambda h, i: (0, h))]
        out_spec = pl.BlockSpec((r, HEAD), lambda h, i: (0, h))
        out_shape = jax.ShapeDtypeStruct((r, G_W), out_dtype)
        dims = 'tn'

    def body(a_ref, b_ref, o_ref):
        o_ref[...] = _dot(a_ref[...].astype(BF16), b_ref[...].astype(BF16), dims).astype(o_ref.dtype)

    return pl.pallas_call(
        body, name=name, grid=grid, in_specs=in_specs, out_specs=out_spec, out_shape=out_shape,
        compiler_params=_params(("parallel", "parallel")),
    )(a, b)


def _rmsnorm(x, g, *, name, width=None, out_dtype=BF16, tm=256):
    s = x.shape[0]
    w = x.shape[1] if width is None else width
    tm = min(tm, s)

    def body(x_ref, g_ref, o_ref):
        xv = x_ref[...].astype(F32)
        rstd = lax.rsqrt(jnp.mean(xv * xv, axis=-1, keepdims=True) + EPS)
        o_ref[...] = (xv * rstd * g_ref[...]).astype(o_ref.dtype)

    return pl.pallas_call(
        body, name=name, grid=(s // tm,),
        in_specs=[pl.BlockSpec((tm, w), lambda i: (i, 0)), pl.BlockSpec((1, w), lambda i: (0, 0))],
        out_specs=pl.BlockSpec((tm, w), lambda i: (i, 0)),
        out_shape=jax.ShapeDtypeStruct((s, w), out_dtype),
        compiler_params=_params(("parallel",)),
    )(x, g.reshape(1, w))


def _rmsnorm_bwd(x, g, dy, *, name, width=None, dres=None, after=None, out_dtype=F32, tm=256):
    s = x.shape[0]
    w = x.shape[1] if width is None else width
    tm = min(tm, s)

    def body(*refs):
        x_ref, g_ref, dy_ref = refs[0], refs[1], refs[2]
        r_ref = refs[3] if dres is not None else None
        dx_ref, dg_ref = refs[-2], refs[-1]
        xv = x_ref[...].astype(F32)
        rstd = lax.rsqrt(jnp.mean(xv * xv, axis=-1, keepdims=True) + EPS)
        xhat = xv * rstd
        dyv = dy_ref[...].astype(F32)
        gdy = dyv * g_ref[...]
        dx = rstd * (gdy - xhat * jnp.mean(gdy * xhat, axis=-1, keepdims=True))
        if r_ref is not None:
            dx = dx + r_ref[...]
        dx_ref[...] = dx.astype(dx_ref.dtype)
        part = jnp.sum(dyv * xhat, axis=0, keepdims=True)

        @pl.when(pl.program_id(0) == 0)
        def _():
            dg_ref[...] = part

        @pl.when(pl.program_id(0) > 0)
        def _():
            dg_ref[...] += part

    row = pl.BlockSpec((tm, w), lambda i: (i, 0))
    vec = pl.BlockSpec((1, w), lambda i: (0, 0))
    in_specs = [row, vec, row] + ([row] if dres is not None else [])
    operands = [x, g.reshape(1, w), dy] + ([dres] if dres is not None else [])
    if after is not None:
        in_specs.append(pl.BlockSpec(memory_space=pl.ANY))
        operands.append(after)
    return pl.pallas_call(
        body, name=name, grid=(s // tm,), in_specs=in_specs, out_specs=[row, vec],
        out_shape=[jax.ShapeDtypeStruct((s, w), out_dtype), jax.ShapeDtypeStruct((1, w), F32)],
        compiler_params=_params(("arbitrary",)),
    )(*operands)


def _final_loss(x, target, g, *, name, tm=256):
    s, d = x.shape
    tm = min(tm, s)

    def body(x_ref, t_ref, g_ref, sq_ref, dx_ref, dg_ref):
        xv = x_ref[...]
        rstd = lax.rsqrt(jnp.mean(xv * xv, axis=-1, keepdims=True) + EPS)
        xhat = xv * rstd
        err = xhat * g_ref[...] - t_ref[...]
        dyv = err * (1.0 / d)
        gdy = dyv * g_ref[...]
        dx_ref[...] = rstd * (gdy - xhat * jnp.mean(gdy * xhat, axis=-1, keepdims=True))
        sq = jnp.sum(err * err, axis=0, keepdims=True)
        dg = jnp.sum(dyv * xhat, axis=0, keepdims=True)

        @pl.when(pl.program_id(0) == 0)
        def _():
            sq_ref[...] = sq
            dg_ref[...] = dg

        @pl.when(pl.program_id(0) > 0)
        def _():
            sq_ref[...] += sq
            dg_ref[...] += dg

    row = pl.BlockSpec((tm, d), lambda i: (i, 0))
    vec = pl.BlockSpec((1, d), lambda i: (0, 0))
    return pl.pallas_call(
        body, name=name, grid=(s // tm,), in_specs=[row, row, vec], out_specs=[vec, row, vec],
        out_shape=[jax.ShapeDtypeStruct((1, d), F32), jax.ShapeDtypeStruct((s, d), F32),
                   jax.ShapeDtypeStruct((1, d), F32)],
        compiler_params=_params(("arbitrary",)),
    )(x, target, g.reshape(1, d))


def _tril_mask():
    t = lax.broadcasted_iota(jnp.int32, (CHUNK, CHUNK), 0)
    s = lax.broadcasted_iota(jnp.int32, (CHUNK, CHUNK), 1)
    return t >= s


def _sgu_fwd(z, g_v, w_sp, b_sp_t, *, name):
    s = z.shape[0]

    def body(zu_ref, zv_ref, g_ref, w_ref, b_ref, o_ref):
        u = _gelu(zu_ref[...].astype(F32))
        gv = _gelu(zv_ref[...].astype(F32))
        rstd = lax.rsqrt(jnp.mean(gv * gv, axis=-1, keepdims=True) + EPS)
        v = (gv * rstd * g_ref[...]).astype(BF16)
        mask = _tril_mask()
        for grp in range(G_HEADS):
            cols = slice(grp * HEAD, (grp + 1) * HEAD)
            wm = jnp.where(mask, w_ref[grp], 0.0).astype(BF16)
            sv = _dot(wm, v[:, cols], 'nn') + b_ref[:, grp:grp + 1]
            o_ref[:, cols] = (u[:, cols] * sv).astype(o_ref.dtype)

    return pl.pallas_call(
        body, name=name, grid=(s // CHUNK,),
        in_specs=[pl.BlockSpec((CHUNK, G_W), lambda i: (i, 0)),
                  pl.BlockSpec((CHUNK, G_W), lambda i: (i, 1)),
                  pl.BlockSpec((1, G_W), lambda i: (0, 0)),
                  pl.BlockSpec((G_HEADS, CHUNK, CHUNK), lambda i: (0, 0, 0)),
                  pl.BlockSpec((CHUNK, G_HEADS), lambda i: (0, 0))],
        out_specs=pl.BlockSpec((CHUNK, G_W), lambda i: (i, 0)),
        out_shape=jax.ShapeDtypeStruct((s, G_W), BF16),
        compiler_params=_params(("parallel",)),
    )(z, z, g_v.reshape(1, G_W), w_sp, b_sp_t)


def _sgu_bwd(z, dmix, dqm, g_v, w_sp, b_sp_t, *, name):
    s = z.shape[0]
    zw = z.shape[1]

    def body(zu_ref, zv_ref, dm_ref, dq_ref, g_ref, w_ref, b_ref, dz_ref, dw_ref, db_ref, dg_ref):
        first = pl.program_id(0) == 0

        @pl.when(first)
        def _():
            dw_ref[...] = jnp.zeros_like(dw_ref)
            db_ref[...] = jnp.zeros_like(db_ref)
            dg_ref[...] = jnp.zeros_like(dg_ref)

        zu = zu_ref[...].astype(F32)
        zv = zv_ref[...].astype(F32)
        dmain = dm_ref[...].astype(F32)
        u = _gelu(zu)
        gv = _gelu(zv)
        rstd = lax.rsqrt(jnp.mean(gv * gv, axis=-1, keepdims=True) + EPS)
        vhat = gv * rstd
        gvec = g_ref[...]
        v = (vhat * gvec).astype(BF16)
        dsv = dmain * u
        dsv_b = dsv.astype(BF16)
        mask = _tril_mask()
        dv_parts = []
        for grp in range(G_HEADS):
            cols = slice(grp * HEAD, (grp + 1) * HEAD)
            wm = jnp.where(mask, w_ref[grp], 0.0).astype(BF16)
            sv = _dot(wm, v[:, cols], 'nn') + b_ref[:, grp:grp + 1]
            dz_ref[:, cols] = (dmain[:, cols] * sv * _gelu_grad(zu[:, cols])).astype(dz_ref.dtype)
            dwg = _dot(dsv_b[:, cols], v[:, cols], 'nt')
            dw_ref[grp] += jnp.where(mask, dwg, 0.0)
            db_ref[:, grp:grp + 1] += jnp.sum(dsv[:, cols], axis=-1, keepdims=True)
            dv_parts.append(_dot(wm, dsv_b[:, cols], 'tn'))
        dv = jnp.concatenate(dv_parts, axis=-1)
        dg_ref[...] += jnp.sum(dv * vhat, axis=0, keepdims=True)
        gdv = dv * gvec
        dgv = rstd * (gdv - vhat * jnp.mean(gdv * vhat, axis=-1, keepdims=True))
        dz_ref[:, G_W:2 * G_W] = (dgv * _gelu_grad(zv)).astype(dz_ref.dtype)
        dz_ref[:, 2 * G_W:] = dq_ref[...].astype(dz_ref.dtype)

    return pl.pallas_call(
        body, name=name, grid=(s // CHUNK,),
        in_specs=[pl.BlockSpec((CHUNK, G_W), lambda i: (i, 0)),
                  pl.BlockSpec((CHUNK, G_W), lambda i: (i, 1)),
                  pl.BlockSpec((CHUNK, G_W), lambda i: (i, 0)),
                  pl.BlockSpec((CHUNK, MEM_W), lambda i: (i, 0)),
                  pl.BlockSpec((1, G_W), lambda i: (0, 0)),
                  pl.BlockSpec((G_HEADS, CHUNK, CHUNK), lambda i: (0, 0, 0)),
                  pl.BlockSpec((CHUNK, G_HEADS), lambda i: (0, 0))],
        out_specs=[pl.BlockSpec((CHUNK, zw), lambda i: (i, 0)),
                   pl.BlockSpec((G_HEADS, CHUNK, CHUNK), lambda i: (0, 0, 0)),
                   pl.BlockSpec((CHUNK, G_HEADS), lambda i: (0, 0)),
                   pl.BlockSpec((1, G_W), lambda i: (0, 0))],
        out_shape=[jax.ShapeDtypeStruct((s, zw), BF16),
                   jax.ShapeDtypeStruct((G_HEADS, CHUNK, CHUNK), F32),
                   jax.ShapeDtypeStruct((CHUNK, G_HEADS), F32),
                   jax.ShapeDtypeStruct((1, G_W), F32)],
        compiler_params=_params(("arbitrary",)),
    )(z, z, dmix, dqm, g_v.reshape(1, G_W), w_sp, b_sp_t)


def _mem_probs(q, k):
    sc = _dot(q, k, 'nt') * (HEAD ** -0.5)
    sc = sc - jnp.max(sc, axis=-1, keepdims=True)
    e = jnp.exp(sc)
    return e / jnp.sum(e, axis=-1, keepdims=True)


def _memattn_fwd(z, kvm, main, *, qcol, name, tm=512):
    s = z.shape[0]
    m = kvm.shape[0]
    tm = min(tm, s)

    def body(q_ref, kv_ref, main_ref, o_ref):
        o_ref[:, :G_W] = main_ref[...]
        for h in range(MEM_HEADS):
            cols = slice(h * HEAD, (h + 1) * HEAD)
            k = kv_ref[:, cols]
            v = kv_ref[:, MEM_W + h * HEAD:MEM_W + (h + 1) * HEAD]
            p = _mem_probs(q_ref[:, cols], k)
            o_ref[:, G_W + h * HEAD:G_W + (h + 1) * HEAD] = _dot(p.astype(BF16), v, 'nn').astype(o_ref.dtype)

    return pl.pallas_call(
        body, name=name, grid=(s // tm,),
        in_specs=[pl.BlockSpec((tm, MEM_W), lambda i: (i, qcol)),
                  pl.BlockSpec((m, 2 * MEM_W), lambda i: (0, 0)),
                  pl.BlockSpec((tm, G_W), lambda i: (i, 0))],
        out_specs=pl.BlockSpec((tm, G_W + MEM_W), lambda i: (i, 0)),
        out_shape=jax.ShapeDtypeStruct((s, G_W + MEM_W), BF16),
        compiler_params=_params(("parallel",)),
    )(z, kvm, main)


def _memattn_bwd(z, kvm, dmix, *, qcol, name, tm=512):
    s = z.shape[0]
    m = kvm.shape[0]
    tm = min(tm, s)
    scale = HEAD ** -0.5

    def body(q_ref, kv_ref, do_ref, dq_ref, dkv_ref):
        @pl.when(pl.program_id(0) == 0)
        def _():
            dkv_ref[...] = jnp.zeros_like(dkv_ref)

        for h in range(MEM_HEADS):
            cols = slice(h * HEAD, (h + 1) * HEAD)
            vcols = slice(MEM_W + h * HEAD, MEM_W + (h + 1) * HEAD)
            q = q_ref[:, cols]
            k = kv_ref[:, cols]
            v = kv_ref[:, vcols]
            do = do_ref[:, cols]
            p = _mem_probs(q, k)
            dp = _dot(do, v, 'nt')
            ds = (p * (dp - jnp.sum(dp * p, axis=-1, keepdims=True)) * scale).astype(BF16)
            dq_ref[:, cols] = _dot(ds, k, 'nn').astype(dq_ref.dtype)
            dkv_ref[:, cols] += _dot(ds, q, 'tn')
            dkv_ref[:, vcols] += _dot(p.astype(BF16), do, 'tn')

    mo_block = G_W // MEM_W
    return pl.pallas_call(
        body, name=name, grid=(s // tm,),
        in_specs=[pl.BlockSpec((tm, MEM_W), lambda i: (i, qcol)),
                  pl.BlockSpec((m, 2 * MEM_W), lambda i: (0, 0)),
                  pl.BlockSpec((tm, MEM_W), lambda i: (i, mo_block))],
        out_specs=[pl.BlockSpec((tm, MEM_W), lambda i: (i, 0)),
                   pl.BlockSpec((m, 2 * MEM_W), lambda i: (0, 0))],
        out_shape=[jax.ShapeDtypeStruct((s, MEM_W), BF16), jax.ShapeDtypeStruct((m, 2 * MEM_W), F32)],
        compiler_params=_params(("arbitrary",)),
    )(z, kvm, dmix)


def _rope(x1, x2, cos, sin, *, name, inverse=False, out_dtype=BF16, col1=0, col2=0, tm=512):
    s, w = cos.shape
    tm = min(tm, s)
    sign = -1.0 if inverse else 1.0

    def body(a_ref, b_ref, c_ref, s_ref, o1_ref, o2_ref):
        a = a_ref[...].astype(F32)
        b = b_ref[...].astype(F32)
        c = c_ref[...]
        sn = s_ref[...] * sign
        o1_ref[...] = (a * c - b * sn).astype(o1_ref.dtype)
        o2_ref[...] = (b * c + a * sn).astype(o2_ref.dtype)

    row = pl.BlockSpec((tm, w), lambda i: (i, 0))
    return pl.pallas_call(
        body, name=name, grid=(s // tm,),
        in_specs=[pl.BlockSpec((tm, w), lambda i: (i, col1)), pl.BlockSpec((tm, w), lambda i: (i, col2)), row, row],
        out_specs=[row, row],
        out_shape=[jax.ShapeDtypeStruct((s, w), out_dtype)] * 2,
        compiler_params=_params(("parallel",)),
    )(x1, x2, cos, sin)


def _mla_scores(q1, q2, k1, k2, row_tok, kstart, tk, scale):
    sc = (_dot(q1, k1, 'nt') + _dot(q2, k2, 'nt')) * scale
    kpos = kstart + lax.broadcasted_iota(jnp.int32, (1, tk), 1)
    keep = kpos <= row_tok
    return sc, keep


def _mla_fwd(qa, qr, ckv, kr, *, name, tk=512):
    hh, s, r = qa.shape
    tq = CHUNK
    tk = min(tk, s)
    rows = hh * tq
    scale = (HEAD + ROPE_DIM) ** -0.5

    def body(qa_ref, qr_ref, ckv_ref, kr_ref, o_ref, lse_ref, m_ref, l_ref, acc_ref):
        i = pl.program_id(0)
        q1 = qa_ref[...].reshape(rows, r)
        q2 = qr_ref[...].reshape(rows, ROPE_DIM)
        row_tok = i * tq + (lax.broadcasted_iota(jnp.int32, (rows, 1), 0) & (tq - 1))
        m_ref[...] = jnp.full_like(m_ref, MASK_VALUE)
        l_ref[...] = jnp.zeros_like(l_ref)
        acc_ref[...] = jnp.zeros_like(acc_ref)

        def step(j, carry):
            kstart = pl.multiple_of(j * tk, tk)
            k1 = ckv_ref[pl.ds(kstart, tk), :]
            k2 = kr_ref[pl.ds(kstart, tk), :]
            sc, keep = _mla_scores(q1, q2, k1, k2, row_tok, kstart, tk, scale)
            sc = jnp.where(keep, sc, MASK_VALUE)
            m_old = m_ref[...]
            m_new = jnp.maximum(m_old, jnp.max(sc, axis=-1, keepdims=True))
            p = jnp.exp(sc - m_new)
            alpha = jnp.exp(m_old - m_new)
            l_ref[...] = alpha * l_ref[...] + jnp.sum(p, axis=-1, keepdims=True)
            acc_ref[...] = alpha * acc_ref[...] + _dot(p.astype(BF16), k1, 'nn')
            m_ref[...] = m_new
            return carry

        lax.fori_loop(0, (i * tq) // tk + 1, step, 0)
        l = l_ref[...]
        o_ref[...] = (acc_ref[...] / l).astype(o_ref.dtype).reshape(hh, tq, r)
        lse_ref[...] = (m_ref[...] + jnp.log(l)).reshape(hh, tq, 1)

    return pl.pallas_call(
        body, name=name, grid=(s // tq,),
        in_specs=[pl.BlockSpec((hh, tq, r), lambda i: (0, i, 0)),
                  pl.BlockSpec((hh, tq, ROPE_DIM), lambda i: (0, i, 0)),
                  pl.BlockSpec((s, r), lambda i: (0, 0)),
                  pl.BlockSpec((s, ROPE_DIM), lambda i: (0, 0))],
        out_specs=[pl.BlockSpec((hh, tq, r), lambda i: (0, i, 0)),
                   pl.BlockSpec((hh, tq, 1), lambda i: (0, i, 0))],
        out_shape=[jax.ShapeDtypeStruct((hh, s, r), BF16), jax.ShapeDtypeStruct((hh, s, 1), F32)],
        scratch_shapes=[pltpu.VMEM((rows, 1), F32), pltpu.VMEM((rows, 1), F32), pltpu.VMEM((rows, r), F32)],
        compiler_params=_params(("parallel",)),
    )(qa, qr, ckv, kr)


def _mla_bwd(qa, qr, ckv, kr, o, do, lse, *, name, tk=256):
    hh, s, r = qa.shape
    tq = CHUNK
    tk = min(tk, s)
    rows = hh * tq
    nq = s // tq
    scale = (HEAD + ROPE_DIM) ** -0.5

    def body(qa_ref, qr_ref, o_ref, do_ref, lse_ref, ckv_hbm, kr_hbm, dqa_ref, dqr_ref, dckv_hbm, dkr_hbm,
             ckv_ref, kr_ref, dckv_ref, dkr_ref, dq1_ref, dq2_ref, sem):
        i = pl.program_id(0)

        @pl.when(i == 0)
        def _():
            c1 = pltpu.make_async_copy(ckv_hbm, ckv_ref, sem.at[0])
            c2 = pltpu.make_async_copy(kr_hbm, kr_ref, sem.at[1])
            c1.start()
            c2.start()
            dckv_ref[...] = jnp.zeros_like(dckv_ref)
            dkr_ref[...] = jnp.zeros_like(dkr_ref)
            c1.wait()
            c2.wait()

        q1 = qa_ref[...].reshape(rows, r)
        q2 = qr_ref[...].reshape(rows, ROPE_DIM)
        dov = do_ref[...].reshape(rows, r)
        delta = jnp.sum(dov.astype(F32) * o_ref[...].reshape(rows, r).astype(F32), axis=-1, keepdims=True)
        lsev = lse_ref[...].reshape(rows, 1)
        row_tok = i * tq + (lax.broadcasted_iota(jnp.int32, (rows, 1), 0) & (tq - 1))
        dq1_ref[...] = jnp.zeros_like(dq1_ref)
        dq2_ref[...] = jnp.zeros_like(dq2_ref)

        def step(j, carry):
            kstart = pl.multiple_of(j * tk, tk)
            k1 = ckv_ref[pl.ds(kstart, tk), :]
            k2 = kr_ref[pl.ds(kstart, tk), :]
            sc, keep = _mla_scores(q1, q2, k1, k2, row_tok, kstart, tk, scale)
            p = jnp.where(keep, jnp.exp(sc - lsev), 0.0)
            dp = _dot(dov, k1, 'nt')
            ds = (p * (dp - delta) * scale).astype(BF16)
            pb = p.astype(BF16)
            dq1_ref[...] += _dot(ds, k1, 'nn')
            dq2_ref[...] += _dot(ds, k2, 'nn')
            dckv_ref[pl.ds(kstart, tk), :] += _dot(ds, q1, 'tn') + _dot(pb, dov, 'tn')
            dkr_ref[pl.ds(kstart, tk), :] += _dot(ds, q2, 'tn')
            return carry

        lax.fori_loop(0, (i * tq) // tk + 1, step, 0)
        dqa_ref[...] = dq1_ref[...].astype(dqa_ref.dtype).reshape(hh, tq, r)
        dqr_ref[...] = dq2_ref[...].astype(dqr_ref.dtype).reshape(hh, tq, ROPE_DIM)

        @pl.when(i == nq - 1)
        def _():
            c1 = pltpu.make_async_copy(dckv_ref, dckv_hbm, sem.at[0])
            c2 = pltpu.make_async_copy(dkr_ref, dkr_hbm, sem.at[1])
            c1.start()
            c2.start()
            c1.wait()
            c2.wait()

    blk = pl.BlockSpec((hh, tq, r), lambda i: (0, i, 0))
    blk_r = pl.BlockSpec((hh, tq, ROPE_DIM), lambda i: (0, i, 0))
    any_spec = pl.BlockSpec(memory_space=pl.ANY)
    return pl.pallas_call(
        body, name=name, grid=(nq,),
        in_specs=[blk, blk_r, blk, blk, pl.BlockSpec((hh, tq, 1), lambda i: (0, i, 0)), any_spec, any_spec],
        out_specs=[blk, blk_r, any_spec, any_spec],
        out_shape=[jax.ShapeDtypeStruct((hh, s, r), BF16), jax.ShapeDtypeStruct((hh, s, ROPE_DIM), BF16),
                   jax.ShapeDtypeStruct((s, r), F32), jax.ShapeDtypeStruct((s, ROPE_DIM), F32)],
        scratch_shapes=[pltpu.VMEM((s, r), BF16), pltpu.VMEM((s, ROPE_DIM), BF16),
                        pltpu.VMEM((s, r), F32), pltpu.VMEM((s, ROPE_DIM), F32),
                        pltpu.VMEM((rows, r), F32), pltpu.VMEM((rows, ROPE_DIM), F32),
                        pltpu.SemaphoreType.DMA((2,))],
        compiler_params=_params(("arbitrary",)),
    )(qa, qr, o, do, lse, ckv, kr)


HALO = 16


def _shift_down(prev, cur, shift, first_tile):
    tr = cur.shape[0]
    full = jnp.concatenate([prev, cur], axis=0)
    out = pltpu.roll(full, shift, axis=0)[HALO:]
    row = lax.broadcasted_iota(jnp.int32, (tr, 1), 0)
    return jnp.where(jnp.logical_and(first_tile, row < shift), 0.0, out)


def _shift_up(cur, nxt, shift, last_tile):
    tr = cur.shape[0]
    full = jnp.concatenate([cur, nxt], axis=0)
    out = pltpu.roll(full, tr + HALO - shift, axis=0)[:tr]
    row = lax.broadcasted_iota(jnp.int32, (tr, 1), 0)
    return jnp.where(jnp.logical_and(last_tile, row >= tr - shift), 0.0, out)


def _conv_taps(prev_ref, cur_ref, cw_ref, cb_ref, first_tile):
    cur = cur_ref[...].astype(F32)
    prev = prev_ref[...].astype(F32)
    a1 = _shift_down(prev, cur, 1, first_tile)
    a2 = _shift_down(prev, cur, 2, first_tile)
    c = a2 * cw_ref[0:1, :] + a1 * cw_ref[1:2, :] + cur * cw_ref[2:3, :] + cb_ref[...]
    return c, (a2, a1, cur)


def _conv_in_specs(tr, bw, half, layer, row_of, blk_of):
    per = tr // HALO
    specs = []
    for off in (0, half):
        specs.append(pl.BlockSpec((None, HALO, bw), lambda *g, off=off: (blk_of(*g) + off, jnp.maximum(row_of(*g) * per - 1, 0), 0)))
        specs.append(pl.BlockSpec((None, tr, bw), lambda *g, off=off: (blk_of(*g) + off, row_of(*g), 0)))
    for off in (0, half):
        specs.append(pl.BlockSpec((None, None, CONV_W, bw), lambda *g, off=off: (blk_of(*g) + off, layer, 0, 0)))
    for off in (0, half):
        specs.append(pl.BlockSpec((None, 1, bw), lambda *g, off=off: (layer * 2 * half + blk_of(*g) + off, 0, 0)))
    return specs


def _conv_fwd(a, cw, cb, layer, *, name, tr=256):
    nb, s, bw = a.shape
    half = nb // 2
    tr = min(tr, s)

    def body(gp_ref, gc_ref, vp_ref, vc_ref, cwg_ref, cwv_ref, cbg_ref, cbv_ref, o_ref):
        first = pl.program_id(0) == 0
        gate, _ = _conv_taps(gp_ref, gc_ref, cwg_ref, cbg_ref, first)
        val, _ = _conv_taps(vp_ref, vc_ref, cwv_ref, cbv_ref, first)
        o_ref[...] = (gate * _sigmoid(gate) * val).astype(o_ref.dtype)

    return pl.pallas_call(
        body, name=name, grid=(s // tr, half),
        in_specs=_conv_in_specs(tr, bw, half, layer, lambda i, j: i, lambda i, j: j),
        out_specs=pl.BlockSpec((None, tr, bw), lambda i, j: (j, i, 0)),
        out_shape=jax.ShapeDtypeStruct((half, s, bw), BF16),
        compiler_params=_params(("parallel", "parallel")),
    )(a, a, a, a, cw, cw, cb, cb)


def _conv_bwd_dc(a, dact, cw, cb, layer, *, name, tr=256):
    nb, s, bw = a.shape
    half = nb // 2
    tr = min(tr, s)

    def body(gp_ref, gc_ref, vp_ref, vc_ref, cwg_ref, cwv_ref, cbg_ref, cbv_ref, da_ref,
             dc_ref, dw_ref, db_ref):
        first = pl.program_id(1) == 0
        gate, gtaps = _conv_taps(gp_ref, gc_ref, cwg_ref, cbg_ref, first)
        val, vtaps = _conv_taps(vp_ref, vc_ref, cwv_ref, cbv_ref, first)
        dact_v = da_ref[...].astype(F32)
        sg = _sigmoid(gate)
        dgate = dact_v * val * (sg * (1.0 + gate * (1.0 - sg)))
        dval = dact_v * (gate * sg)
        dc_ref[0] = dgate.astype(dc_ref.dtype)
        dc_ref[1] = dval.astype(dc_ref.dtype)

        @pl.when(first)
        def _():
            dw_ref[...] = jnp.zeros_like(dw_ref)
            db_ref[...] = jnp.zeros_like(db_ref)

        for kk in range(CONV_W):
            dw_ref[0, kk:kk + 1, :] += jnp.sum(dgate * gtaps[kk], axis=0, keepdims=True)
            dw_ref[1, kk:kk + 1, :] += jnp.sum(dval * vtaps[kk], axis=0, keepdims=True)
        db_ref[0] += jnp.sum(dgate, axis=0, keepdims=True)
        db_ref[1] += jnp.sum(dval, axis=0, keepdims=True)

    outs = pl.pallas_call(
        body, name=name, grid=(half, s // tr),
        in_specs=_conv_in_specs(tr, bw, half, layer, lambda j, i: i, lambda j, i: j)
        + [pl.BlockSpec((None, tr, bw), lambda j, i: (j, i, 0))],
        out_specs=[pl.BlockSpec((2, None, tr, bw), lambda j, i: (0, j, i, 0)),
                   pl.BlockSpec((2, None, CONV_W, bw), lambda j, i: (0, j, 0, 0)),
                   pl.BlockSpec((2, None, 1, bw), lambda j, i: (0, j, 0, 0))],
        out_shape=[jax.ShapeDtypeStruct((2, half, s, bw), BF16),
                   jax.ShapeDtypeStruct((2, half, CONV_W, bw), F32),
                   jax.ShapeDtypeStruct((2, half, 1, bw), F32)],
        compiler_params=_params(("parallel", "arbitrary")),
    )(a, a, a, a, cw, cw, cb, cb, dact)
    dc, dw, db = outs
    return dc.reshape(nb, s, bw), dw.reshape(nb, CONV_W, bw), db.reshape(nb, 1, bw)


def _conv_bwd_da(dc, cw, layer, *, name, tr=256):
    nb, s, bw = dc.shape
    tr = min(tr, s)
    ni = s // tr
    per = tr // HALO
    last_halo = s // HALO - 1

    def body(c_ref, n_ref, w_ref, o_ref):
        last = pl.program_id(0) == ni - 1
        cur = c_ref[...].astype(F32)
        nxt = n_ref[...].astype(F32)
        da = (cur * w_ref[2:3, :] + _shift_up(cur, nxt, 1, last) * w_ref[1:2, :]
              + _shift_up(cur, nxt, 2, last) * w_ref[0:1, :])
        o_ref[...] = da.astype(o_ref.dtype)

    tile = pl.BlockSpec((None, tr, bw), lambda i, j: (j, i, 0))
    return pl.pallas_call(
        body, name=name, grid=(ni, nb),
        in_specs=[tile,
                  pl.BlockSpec((None, HALO, bw), lambda i, j: (j, jnp.minimum((i + 1) * per, last_halo), 0)),
                  pl.BlockSpec((None, None, CONV_W, bw), lambda i, j: (j, layer, 0, 0))],
        out_specs=tile,
        out_shape=jax.ShapeDtypeStruct((nb, s, bw), BF16),
        compiler_params=_params(("parallel", "parallel")),
    )(dc, dc, cw)


def _rope_tables(positions):
    inv = 1.0 / (ROPE_THETA ** (jnp.arange(0, ROPE_DIM, 2, dtype=F32) / ROPE_DIM))
    ang = positions.astype(F32)[:, None] * inv
    return jnp.cos(ang), jnp.sin(ang)


def _heads_to_major(r1, r2):
    s = r1.shape[0]
    both = jnp.concatenate([r1.reshape(s, G_HEADS, ROPE_HALF), r2.reshape(s, G_HEADS, ROPE_HALF)], axis=-1)
    return both.transpose(1, 0, 2)


def _heads_from_major(qr):
    s = qr.shape[1]
    t = qr.transpose(1, 0, 2)
    return t[:, :, :ROPE_HALF].reshape(s, G_HEADS * ROPE_HALF), t[:, :, ROPE_HALF:].reshape(s, G_HEADS * ROPE_HALF)


def _local_step(x, mem, positions, target, rep, fetch, emit):
    s, d = x.shape
    n_b = DEPTH - N_A
    tm = min(1024, s)
    cos, sin = _rope_tables(positions)
    cos12 = jnp.tile(cos, (1, G_HEADS))
    sin12 = jnp.tile(sin, (1, G_HEADS))
    r1_col = G_W // (G_HEADS * ROPE_HALF)
    b_sp_t = rep['b_sp'].transpose(0, 2, 1)

    saved = []
    kv = None
    shared = None
    for l in range(DEPTH):
        wm = fetch(('mix', l), x)
        if l == 0:
            shared = {'g_v': wm['g_v'], 'conv_w': wm['conv_w']}
            bw = shared['conv_w'].shape[-1]
            conv_b = rep['conv_b'].reshape(-1, 1, bw)
        sv = {'x_in': x, 'wm': wm}
        memn = _rmsnorm(mem, rep['g_mem'][l], name=f"memnorm{l}")
        kvm = _mm(memn, wm['w_mem_kv'], dims='nn', out_dtype=BF16, name=f"memkv{l}", tm=tm, tn=1024)
        if l == N_A:
            xn_kv = _rmsnorm(x, rep['g_kv'], name="kvnorm")
            kvx = _mm(xn_kv, wm['w_kv_a'], dims='nn', out_dtype=F32, name="kvproj", tm=tm, tn=KV_PAD)
            ckv = _rmsnorm(kvx, rep['g_kv_lat'], width=KV_RANK, name="ckvnorm")
            k1, k2 = _rope(kvx[:, KV_RANK:KV_RANK + ROPE_HALF], kvx[:, KV_RANK + ROPE_HALF:KV_RANK + ROPE_DIM],
                           cos, sin, name="krope")
            kr = jnp.concatenate([k1, k2], axis=-1)
            kv = {'x': x, 'xn': xn_kv, 'kvx': kvx, 'ckv': ckv, 'kr': kr, 'w_kv_a': wm['w_kv_a']}
        h = _rmsnorm(x, rep['g_mix'][l], name=f"mixnorm{l}")
        if l < N_A:
            z = _mm(h, wm['w_in'], dims='nn', out_dtype=BF16, name=f"in_a{l}", tm=tm, tn=512)
            main = _sgu_fwd(z, shared['g_v'][l], rep['w_sp'][l], b_sp_t[l], name=f"sgu{l}")
            qcol = 2 * G_W // MEM_W
        else:
            j = l - N_A
            z = _mm(h, wm['w_in'], dims='nn', out_dtype=BF16, name=f"in_b{j}", tm=tm, tn=1024)
            qn = _rmsnorm(z, rep['g_q_lat'][j], width=Q_RANK, name=f"qnorm{j}")
            qp = _mm(qn, wm['w_uqp'], dims='nn', out_dtype=BF16, name=f"uq{j}", tm=tm, tn=768)
            rr1, rr2 = _rope(qp, qp, cos12, sin12, col1=r1_col, col2=r1_col + 1, name=f"qrope{j}")
            qr = _heads_to_major(rr1, rr2)
            qa = _mm_heads(qp, wm['w_uk'], mode='to_lat', name=f"qabsorb{j}")
            o_lat, lse = _mla_fwd(qa, qr, kv['ckv'], kv['kr'], name=f"mla{j}")
            main = _mm_heads(o_lat, wm['w_uv'], mode='from_lat', name=f"uv{j}")
            qcol = Q_RANK // MEM_W
            sv.update(qn=qn, qp=qp, qr=qr, qa=qa, o_lat=o_lat, lse=lse)
        mix = _memattn_fwd(z, kvm, main, qcol=qcol, name=f"memattn{l}")
        x_mid = _mm(mix, wm['w_out'], dims='nn', res=x, out_dtype=F32, name=f"out{l}", tm=tm, tn=1024)
        wf = fetch(('ffn', l), x_mid)
        h2 = _rmsnorm(x_mid, rep['g_ffn'][l], name=f"ffnnorm{l}")
        a = _mm(h2, wf['w_up'], dims='nn', b_blocked=True, out_dtype=BF16, out_block=bw,
                name=f"up{l}", tm=tm, tn=bw)
        act = _conv_fwd(a, shared['conv_w'], conv_b, l, name=f"conv{l}")
        x = _mm(act, wf['w_down'], dims='nn', a_blocked=True, tk=bw, res=x_mid, out_dtype=F32,
                name=f"down{l}", tm=tm, tn=1024)
        sv.update(h=h, memn=memn, kvm=kvm, z=z, qcol=qcol, mix=mix, x_mid=x_mid, h2=h2, a=a, act=act, wf=wf)
        saved.append(sv)

    sq, dx, dg_final = _final_loss(x, target, rep['g_final'], name="loss")

    g = {k: [None] * DEPTH for k in ('g_mix', 'g_ffn', 'g_mem', 'conv_w', 'conv_b')}
    for k in ('g_v', 'w_sp', 'b_sp'):
        g[k] = [None] * N_A
    g['g_q_lat'] = [None] * n_b
    g['g_final'] = dg_final
    dckv_sum, dkr_sum = None, None

    for l in reversed(range(DEPTH)):
        sv = saved[l]
        wm, wf = sv['wm'], sv['wf']
        dact = _mm(dx, wf['w_down'], dims='nt', out_dtype=BF16, out_block=bw, name=f"d_act{l}", tm=tm, tn=bw)
        dw_down = _mm(sv['act'], dx, dims='tn', a_blocked=True, out_dtype=BF16, name=f"dw_down{l}", tm=bw, tn=256)
        dc, dcw, dcb = _conv_bwd_dc(sv['a'], dact, shared['conv_w'], conv_b, l, name=f"d_conv{l}")
        g['conv_w'][l], g['conv_b'][l] = dcw, dcb
        da = _conv_bwd_da(dc, shared['conv_w'], l, name=f"d_convin{l}")
        dh2 = _mm(da, wf['w_up'], dims='nt', a_blocked=True, b_blocked=True, tk=bw,
                  out_dtype=BF16, name=f"d_h2{l}", tm=tm, tn=1024)
        dw_up = _mm(sv['h2'], da, dims='tn', b_blocked=True, out_dtype=BF16, out_block=bw,
                    name=f"dw_up{l}", tm=512, tn=bw, n_outer=True)
        tok = emit(('ffn', l), {'w_ffn_up': dw_up, 'w_ffn_down': dw_down})
        dx_mid, g['g_ffn'][l] = _rmsnorm_bwd(sv['x_mid'], rep['g_ffn'][l], dh2, dres=dx, after=tok, name=f"d_ffnnorm{l}")
        dmix = _mm(dx_mid, wm['w_out'], dims='nt', out_dtype=BF16, name=f"d_mix{l}", tm=tm, tn=1024)
        gm = {'w_out': _mm(sv['mix'], dx_mid, dims='tn', out_dtype=BF16, name=f"dw_out{l}", tm=1024, tn=256)}
        dqm, dkvm = _memattn_bwd(sv['z'], sv['kvm'], dmix, qcol=sv['qcol'], name=f"d_memattn{l}")
        gm['w_mem_kv'] = _mm(sv['memn'], dkvm, dims='tn', out_dtype=BF16, name=f"dw_memkv{l}", tm=1024, tn=1024)
        dmemn = _mm(dkvm, wm['w_mem_kv'], dims='nt', out_dtype=F32, name=f"d_memn{l}", tm=tm, tn=1024)
        _, g['g_mem'][l] = _rmsnorm_bwd(mem, rep['g_mem'][l], dmemn, out_dtype=BF16, name=f"d_memnorm{l}")
        if l < N_A:
            dz, dwsp, dbsp_t, dgv = _sgu_bwd(sv['z'], dmix, dqm, shared['g_v'][l], rep['w_sp'][l], b_sp_t[l],
                                             name=f"d_sgu{l}")
            g['w_sp'][l], g['b_sp'][l], g['g_v'][l] = dwsp, dbsp_t.T, dgv
            dh = _mm(dz, wm['w_in'], dims='nt', out_dtype=BF16, name=f"d_h_a{l}", tm=tm, tn=1024)
            gm['w_in_a'] = _mm(sv['h'], dz, dims='tn', out_dtype=BF16, name=f"dw_in_a{l}", tm=1024, tn=512)
        else:
            j = l - N_A
            do_lat = _mm_heads(dmix, wm['w_uv'], mode='to_lat', name=f"d_olat{j}")
            gm['w_uv'] = _mm_heads(sv['o_lat'], dmix, mode='wgrad', name=f"dw_uv{j}")
            dqa, dqr, dckv, dkr = _mla_bwd(sv['qa'], sv['qr'], kv['ckv'], kv['kr'], sv['o_lat'], do_lat, sv['lse'],
                                           name=f"d_mla{j}")
            dckv_sum = dckv if dckv_sum is None else dckv_sum + dckv
            dkr_sum = dkr if dkr_sum is None else dkr_sum + dkr
            dq_nope = _mm_heads(dqa, wm['w_uk'], mode='from_lat', name=f"d_qnope{j}")
            gm['w_uk'] = _mm_heads(dqa, sv['qp'], mode='wgrad', name=f"dw_uk{j}")
            dr1, dr2 = _heads_from_major(dqr)
            dq1, dq2 = _rope(dr1, dr2, cos12, sin12, inverse=True, name=f"d_qrope{j}")
            dqp = jnp.concatenate([dq_nope, dq1, dq2], axis=-1)
            dqn = _mm(dqp, wm['w_uqp'], dims='nt', out_dtype=BF16, name=f"d_qn{j}", tm=tm, tn=512)
            gm['w_uqp'] = _mm(sv['qn'], dqp, dims='tn', out_dtype=BF16, name=f"dw_uq{j}", tm=512, tn=768)
            dqlat, g['g_q_lat'][j] = _rmsnorm_bwd(sv['z'], rep['g_q_lat'][j], dqn, width=Q_RANK, out_dtype=BF16,
                                                 name=f"d_qnorm{j}")
            dz = jnp.concatenate([dqlat, dqm], axis=-1)
            dh = _mm(dz, wm['w_in'], dims='nt', out_dtype=BF16, name=f"d_h_b{j}", tm=tm, tn=1024)
            gm['w_in_b'] = _mm(sv['h'], dz, dims='tn', out_dtype=BF16, name=f"dw_in_b{j}", tm=1024, tn=512)
        tok = emit(('mix', l), gm)
        dx, g['g_mix'][l] = _rmsnorm_bwd(sv['x_in'], rep['g_mix'][l], dh, dres=dx_mid, after=tok, name=f"d_mixnorm{l}")
        if l == N_A:
            dkvx_c, g['g_kv_lat'] = _rmsnorm_bwd(kv['kvx'], rep['g_kv_lat'], dckv_sum, width=KV_RANK, out_dtype=BF16,
                                                 name="d_ckvnorm")
            dk1, dk2 = _rope(dkr_sum[:, :ROPE_HALF], dkr_sum[:, ROPE_HALF:], cos, sin, inverse=True, name="d_krope")
            dkvx = jnp.concatenate([dkvx_c, dk1, dk2, jnp.zeros((s, KV_PAD - KV_RANK - ROPE_DIM), BF16)], axis=-1)
            dxn = _mm(dkvx, kv['w_kv_a'], dims='nt', out_dtype=BF16, name="d_kvnorm_in", tm=tm, tn=1024)
            dw_kv = _mm(kv['xn'], dkvx, dims='tn', out_dtype=BF16, name="dw_kv", tm=1024, tn=KV_PAD)
            tok = emit(('kv', 0), {'w_kv_a': dw_kv})
            dx, g['g_kv'] = _rmsnorm_bwd(kv['x'], rep['g_kv'], dxn, dres=dx, after=tok, name="d_kvnorm")
    return jnp.sum(sq), dx, g


MESH_IDS = pl.DeviceIdType.MESH
PEER_MASKS = tuple((k >> 2 & 1, k >> 1 & 1, k & 1) for k in range(1, N_DEV))
N_PEER = N_DEV - 1
SEMS_PER_BUFFER = 2 * N_PEER + 1
DATAFLOW = pltpu.SideEffectType.DATAFLOW_SIDE_EFFECTING
HBM_SPEC = pl.BlockSpec(memory_space=pltpu.HBM)
SEM_SPEC = pl.BlockSpec(memory_space=pltpu.SEMAPHORE)


def _my_position():
    return lax.axis_index("x"), lax.axis_index("y"), lax.axis_index("c")


def _flip(pos, mask):
    return tuple(1 - p if f else p for p, f in zip(pos, mask))


def _linear_id(pos):
    return 4 * pos[0] + 2 * pos[1] + pos[2]


def _hbm(x):
    return pltpu.with_memory_space_constraint(x, pltpu.HBM)


def _buffer_copies(src_ref, lead, land_ref, sems, scatter):
    me = _my_position()
    my_id = _linear_id(me)
    src = src_ref.at[lead] if lead else src_ref
    own = pltpu.make_async_copy(src.at[my_id] if scatter else src, land_ref.at[my_id], sems.at[2 * N_PEER])
    pairs = []
    for k, mask in enumerate(PEER_MASKS):
        peer = _flip(me, mask)
        peer_id = _linear_id(peer)
        block = src.at[peer_id] if scatter else src
        send = pltpu.make_async_remote_copy(src_ref=block, dst_ref=land_ref.at[my_id], send_sem=sems.at[k],
                                            recv_sem=sems.at[N_PEER + k], device_id=peer, device_id_type=MESH_IDS)
        arrival = pltpu.make_async_remote_copy(src_ref=block, dst_ref=land_ref.at[peer_id], send_sem=sems.at[k],
                                               recv_sem=sems.at[N_PEER + k], device_id=peer, device_id_type=MESH_IDS)
        pairs.append((send, arrival))
    return own, pairs


def _exchange_start(srcs, buffers, *, name, scatter):
    ns, nb = len(srcs), len(buffers)
    lands = [_hbm(lax.empty((N_DEV,) + tuple(shape), dtype)) for _, _, shape, dtype in buffers]

    def body(*refs):
        src_refs, land_refs = refs[:ns], refs[ns:ns + nb]
        sem_refs = refs[ns + nb:ns + 2 * nb]
        token = refs[-1]
        for b, (si, lead, _, _) in enumerate(buffers):
            own, pairs = _buffer_copies(src_refs[si], lead, land_refs[b], sem_refs[b], scatter)
            own.start()
            for send, _ in pairs:
                send.start()
        token[...] = jnp.zeros_like(token)

    out_shape = ([pltpu.SemaphoreType.DMA((SEMS_PER_BUFFER,))] * nb
                 + [pltpu.HBM(a.shape, a.dtype) for a in srcs]
                 + [pltpu.HBM(a.shape, a.dtype) for a in lands]
                 + [jax.ShapeDtypeStruct((8, 128), F32)])
    aliases = {i: nb + i for i in range(ns + nb)}
    outs = pl.pallas_call(
        body, name=name, in_specs=[HBM_SPEC] * (ns + nb),
        out_specs=[SEM_SPEC] * nb + [HBM_SPEC] * (ns + nb) + [pl.BlockSpec(memory_space=pltpu.VMEM)],
        out_shape=out_shape, input_output_aliases=aliases,
        compiler_params=pltpu.CompilerParams(has_side_effects=DATAFLOW),
    )(*[_hbm(a) for a in srcs], *lands)
    sems = list(outs[:nb])
    src_thru = list(outs[nb:nb + ns])
    land_thru = list(outs[nb + ns:nb + ns + nb])
    return sems, land_thru, src_thru, outs[-1]


def _exchange_wait(srcs_thru, buffers, sems, lands, after, *, name, scatter):
    ns, nb = len(srcs_thru), len(buffers)
    has_after = after is not None

    def body(*refs):
        src_refs, land_refs = refs[:ns], refs[ns:ns + nb]
        sem_refs = refs[ns + nb:ns + 2 * nb]
        for b, (si, lead, _, _) in enumerate(buffers):
            own, pairs = _buffer_copies(src_refs[si], lead, land_refs[b], sem_refs[b], scatter)
            for send, arrival in pairs:
                send.wait_send()
                arrival.wait_recv()
            own.wait()

    operands = list(srcs_thru) + list(lands) + list(sems) + ([after] if has_after else [])
    in_specs = ([HBM_SPEC] * (ns + nb) + [SEM_SPEC] * nb + ([pl.BlockSpec(memory_space=pl.ANY)] if has_after else []))
    outs = pl.pallas_call(
        body, name=name, in_specs=in_specs, out_specs=[HBM_SPEC] * nb,
        out_shape=[pltpu.HBM(a.shape, a.dtype) for a in lands],
        input_output_aliases={ns + b: b for b in range(nb)},
        compiler_params=pltpu.CompilerParams(has_side_effects=DATAFLOW),
    )(*operands)
    return list(outs)


def _exchange(arrays, *, name, scatter):
    n = len(arrays)
    out_shapes = [jax.ShapeDtypeStruct(a.shape if scatter else (N_DEV,) + a.shape, a.dtype) for a in arrays]

    def body(*refs):
        srcs, outs, sems = refs[:n], refs[n:2 * n], refs[2 * n:]
        started = []
        for a in range(n):
            own, pairs = _buffer_copies(srcs[a], (), outs[a], sems[a], scatter)
            own.start()
            for send, _ in pairs:
                send.start()
            started.append((own, pairs))
        for own, pairs in started:
            for send, arrival in pairs:
                arrival.wait_recv()
                send.wait_send()
            own.wait()

    any_spec = pl.BlockSpec(memory_space=pl.ANY)
    outs = pl.pallas_call(
        body, name=name, in_specs=[any_spec] * n, out_specs=[any_spec] * n, out_shape=out_shapes,
        scratch_shapes=[pltpu.SemaphoreType.DMA((SEMS_PER_BUFFER,))] * n,
    )(*arrays)
    return list(outs)


def _sum_slots(parts_ref):
    total = parts_ref[0].astype(F32)
    for q in range(1, parts_ref.shape[0]):
        total = total + parts_ref[q].astype(F32)
    return total


def _row_tile(rows, cols, n_arrays):
    budget = (12 * 1024 * 1024) // (4 * n_arrays * max(cols, 128))
    t = rows
    while t > budget and t % 2 == 0 and (t // 2) % 16 == 0:
        t //= 2
    return t


def _sum_adam(parts, w, m, v, layer, outs, *, name):
    q, r, c = parts.shape
    nl = w.shape[0]
    tr = _row_tile(r, c, q + 7)
    c1 = 1.0 - ADAM_B1 ** ADAM_STEP
    c2 = 1.0 - ADAM_B2 ** ADAM_STEP
    if outs is None:
        outs = [lax.empty((nl, r, c), F32) for _ in range(4)]

    def body(p_ref, w_ref, m_ref, v_ref, g_in, d_in, mo_in, vo_in, g_ref, d_ref, mo_ref, vo_ref):
        grad = _sum_slots(p_ref)
        m_new = ADAM_B1 * m_ref[...] + (1.0 - ADAM_B1) * grad
        v_new = ADAM_B2 * v_ref[...] + (1.0 - ADAM_B2) * (grad * grad)
        m_hat = m_new / c1
        v_hat = v_new / c2
        g_ref[...] = grad
        d_ref[...] = -ADAM_LR * (m_hat / (jnp.sqrt(v_hat) + ADAM_EPS) + ADAM_WD * w_ref[...])
        mo_ref[...] = m_new
        vo_ref[...] = v_new

    tile = pl.BlockSpec((None, tr, c), lambda i: (layer, i, 0))
    any_spec = pl.BlockSpec(memory_space=pl.ANY)
    return pl.pallas_call(
        body, name=name, grid=(r // tr,),
        in_specs=[pl.BlockSpec((q, tr, c), lambda i: (0, i, 0)), tile, tile, tile] + [any_spec] * 4,
        out_specs=[tile] * 4, out_shape=[jax.ShapeDtypeStruct((nl, r, c), F32)] * 4,
        input_output_aliases={4: 0, 5: 1, 6: 2, 7: 3},
        compiler_params=_params(("parallel",)),
    )(parts, w, m, v, *outs)


def _sum_parts(parts, *, name):
    q, r, c = parts.shape

    def body(p_ref, o_ref):
        o_ref[...] = _sum_slots(p_ref)

    return pl.pallas_call(
        body, name=name, in_specs=[pl.BlockSpec((q, r, c), lambda: (0, 0, 0))],
        out_specs=pl.BlockSpec((r, c), lambda: (0, 0)), out_shape=jax.ShapeDtypeStruct((r, c), F32),
        compiler_params=_params(),
    )(parts)


INPUT_NAMES = (['x', 'mem', 'positions'] + WEIGHTS + ['loss_target'] + ['m_' + n for n in WEIGHTS]
               + ['v_' + n for n in WEIGHTS])
SMALL_ALIGN = N_DEV * 8 * 128
MIX_A = ['w_in_a', 'w_mem_kv', 'w_out']
MIX_B = ['w_in_b', 'w_uq', 'w_uk', 'w_uv', 'w_mem_kv', 'w_out']
FFN = ['w_ffn_up', 'w_ffn_down']
LAYERED = {'w_in_a': 0, 'w_in_b': N_A, 'w_uq': N_A, 'w_uk': N_A, 'w_uv': N_A, 'w_mem_kv': 0, 'w_out': 0,
           'w_ffn_up': 0, 'w_ffn_down': 0}


def _permute_uq(w_uq):
    r = w_uq.shape[0]
    q = w_uq.reshape(r, G_HEADS, HEAD + ROPE_DIM)
    return jnp.concatenate([q[..., :HEAD].reshape(r, -1), q[..., HEAD:HEAD + ROPE_HALF].reshape(r, -1),
                            q[..., HEAD + ROPE_HALF:].reshape(r, -1)], axis=-1)


def _unpermute_uq(w_uqp):
    r = w_uqp.shape[0]
    nope = w_uqp[..., :G_W].reshape(r, G_HEADS, HEAD)
    r1 = w_uqp[..., G_W:G_W + G_HEADS * ROPE_HALF].reshape(r, G_HEADS, ROPE_HALF)
    r2 = w_uqp[..., G_W + G_HEADS * ROPE_HALF:].reshape(r, G_HEADS, ROPE_HALF)
    return jnp.concatenate([nope, r1, r2], axis=-1).reshape(r, -1)


def _cols_from_stack(st):
    _, r, n = st.shape
    return st.transpose(1, 0, 2).reshape(r, N_DEV * n)


def _cols_to_stack(wh):
    r, c = wh.shape
    return wh.reshape(r, N_DEV, c // N_DEV).transpose(1, 0, 2)


def _group_weights(group):
    kind, l = group
    if kind == 'ffn':
        return FFN
    return MIX_A if l < N_A else MIX_B


def _step(args):
    p = dict(zip(INPUT_NAMES, args))
    x, mem, positions, target = p['x'][0], p['mem'][0], p['positions'][0], p['loss_target'][0]
    d = x.shape[-1]
    my_id = _linear_id(_my_position())

    w_kv_pad = jnp.pad(p['w_kv_a'], ((0, 0), (0, KV_PAD - p['w_kv_a'].shape[1])))
    shard = {k: p[k].astype(BF16) for k in LAYERED}
    shard['w_uk'] = shard['w_uk'].reshape(shard['w_uk'].shape[0], shard['w_uk'].shape[1], -1)
    shard['w_uv'] = shard['w_uv'].reshape(shard['w_uv'].shape[0], shard['w_uv'].shape[1], -1)
    shard.update(conv_w=p['conv_w'], g_v=p['g_v'], w_kv_a=w_kv_pad.astype(BF16))
    src_names = list(shard)
    gather_groups = []
    for l in range(DEPTH):
        gather_groups += [('mix', l), ('ffn', l)]
    buffers, owner = [], []
    for group in gather_groups:
        kind, l = group
        for k in _group_weights(group):
            buffers.append((src_names.index(k), (l - LAYERED[k],), shard[k].shape[1:], shard[k].dtype))
            owner.append((group, k))
        if group == ('mix', 0):
            for k in ('g_v', 'conv_w'):
                buffers.append((src_names.index(k), (), shard[k].shape, shard[k].dtype))
                owner.append((group, k))
        if group == ('mix', N_A):
            buffers.append((src_names.index('w_kv_a'), (), shard['w_kv_a'].shape, BF16))
            owner.append((group, 'w_kv_a'))
    g_sems, g_lands, g_srcs, _ = _exchange_start([shard[k] for k in src_names], buffers, name="gather_start",
                                                 scatter=False)

    def fetch(group, after):
        idx = [i for i, (grp, _) in enumerate(owner) if grp == group]
        landed = _exchange_wait(g_srcs, [buffers[i] for i in idx], [g_sems[i] for i in idx],
                                [g_lands[i] for i in idx], after, name=f"gather_wait_{group[0]}{group[1]}",
                                scatter=False)
        got = {owner[i][1]: t for i, t in zip(idx, landed)}
        out = {}
        for k, t in got.items():
            if k in ('w_in_a', 'w_uq'):
                out[k] = _cols_from_stack(t)
            elif k == 'g_v':
                out[k] = t.transpose(1, 0, 2).reshape(t.shape[1], -1)
            elif k in ('w_ffn_up', 'conv_w'):
                out[k] = t
            else:
                out[k] = t.reshape(-1, t.shape[-1])
        if 'w_uq' in out:
            out['w_uqp'] = _permute_uq(out.pop('w_uq'))
        for old, new in (('w_in_a', 'w_in'), ('w_in_b', 'w_in'), ('w_ffn_up', 'w_up'), ('w_ffn_down', 'w_down')):
            if old in out:
                out[new] = out.pop(old)
        return out

    pending = []

    def emit(group, grads):
        send = {}
        for k, t in grads.items():
            if k == 'w_in_a':
                send[k] = _cols_to_stack(t)
            elif k == 'w_uqp':
                send['w_uq'] = _cols_to_stack(_unpermute_uq(t))
            elif k == 'w_ffn_up':
                send[k] = t
            elif k == 'w_kv_a':
                cols = p['w_kv_a'].shape[1]
                send[k] = t[:, :cols].reshape(N_DEV, -1, cols)
            else:
                send[k] = t.reshape(N_DEV, t.shape[0] // N_DEV, t.shape[1])
        keys = list(send)
        bufs = [(i, (), send[k].shape[1:], send[k].dtype) for i, k in enumerate(keys)]
        sems, lands, srcs, token = _exchange_start([send[k] for k in keys], bufs,
                                                   name=f"scatter_start_{group[0]}{group[1]}", scatter=True)
        pending.append((group, keys, bufs, sems, lands, srcs))
        return token

    rep = {k: p[k] for k in REPLICATED}
    sq, grad_x, g = _local_step(x, mem, positions, target, rep, fetch, emit)
    loss = (0.5 / d) * lax.psum(sq, ("x", "y", "c"))

    small = {
        'g_mix': jnp.concatenate(g['g_mix']), 'g_ffn': jnp.concatenate(g['g_ffn']), 'g_final': g['g_final'],
        'w_sp': jnp.stack(g['w_sp']), 'b_sp': jnp.stack(g['b_sp']), 'g_kv': g['g_kv'], 'g_kv_lat': g['g_kv_lat'],
        'g_q_lat': jnp.concatenate(g['g_q_lat']), 'g_mem': jnp.concatenate(g['g_mem']),
        'conv_b': jnp.stack(g['conv_b']),
        'g_v': jnp.concatenate(g['g_v']),
        'conv_w': jnp.stack(g['conv_w']).transpose(0, 2, 1, 3),
    }
    small_names = REPLICATED + SMALL_SHARDED
    flat = jnp.concatenate([small[k].reshape(-1).astype(F32) for k in small_names])
    n_small = flat.shape[0]
    padded = -(-n_small // SMALL_ALIGN) * SMALL_ALIGN
    flat = jnp.pad(flat, (0, padded - n_small)).reshape(N_DEV, -1, 128)
    (small_parts,) = _exchange([flat], name="scatter_small", scatter=True)
    reduced = _sum_parts(small_parts, name="sum_small")
    (small_all,) = _exchange([reduced], name="gather_small", scatter=False)
    small_all = small_all.reshape(-1)
    grads_small, off = {}, 0
    for k in small_names:
        size = small[k].size
        grads_small[k] = small_all[off:off + size].reshape(small[k].shape)
        off += size
    grads_small['g_v'] = lax.dynamic_slice_in_dim(grads_small['g_v'], my_id * p['g_v'].shape[1], p['g_v'].shape[1], axis=1)
    grads_small['conv_w'] = lax.dynamic_index_in_dim(grads_small['conv_w'], my_id, axis=2, keepdims=False)
    gs = jnp.concatenate([grads_small[k].reshape(-1) for k in small_names])
    n_loc = gs.shape[0]
    pad_loc = -(-n_loc // 1024) * 1024 - n_loc

    def pack(prefix):
        t = jnp.concatenate([p[prefix + k].reshape(-1) for k in small_names])
        return jnp.pad(t, (0, pad_loc)).reshape(1, -1, 128)

    res = _sum_adam(jnp.pad(gs, (0, pad_loc)).reshape(1, -1, 128), pack(''), pack('m_'), pack('v_'), 0, None,
                    name="adam_small")
    out, off = {}, 0
    for k in small_names:
        size = p[k].size
        out[k] = [t.reshape(-1)[off:off + size].reshape(p[k].shape) for t in res]
        off += size

    running = {}
    for group, keys, bufs, sems, lands, srcs in pending:
        landed = _exchange_wait(srcs, bufs, sems, lands, None, name=f"scatter_wait_{group[0]}{group[1]}", scatter=True)
        for k, parts in zip(keys, landed):
            stacked = k in LAYERED
            nl = p[k].shape[0] if stacked else 1
            layer = group[1] - LAYERED[k] if stacked else 0
            rows = p[k].size // nl // p[k].shape[-1]
            view = (nl, rows, p[k].shape[-1])
            running[k] = _sum_adam(parts.reshape(N_DEV, rows, view[2]), p[k].reshape(view), p['m_' + k].reshape(view),
                                   p['v_' + k].reshape(view), layer, running.get(k), name=f"adam_{k}{layer}")
    for k, res in running.items():
        out[k] = [t.reshape(p[k].shape) for t in res]

    outs = [loss, grad_x[None]]
    for i in range(4):
        outs += [out[k][i] for k in WEIGHTS]
    return tuple(outs)


def kernel(x, mem, positions, g_mix, g_ffn, g_final, w_in_a, g_v, w_sp, b_sp, g_kv, w_kv_a, g_kv_lat, w_in_b, g_q_lat, w_uq, w_uk, w_uv, g_mem, w_mem_kv, w_out, w_ffn_up, conv_w, conv_b, w_ffn_down, loss_target, m_g_mix, m_g_ffn, m_g_final, m_w_in_a, m_g_v, m_w_sp, m_b_sp, m_g_kv, m_w_kv_a, m_g_kv_lat, m_w_in_b, m_g_q_lat, m_w_uq, m_w_uk, m_w_uv, m_g_mem, m_w_mem_kv, m_w_out, m_w_ffn_up, m_conv_w, m_conv_b, m_w_ffn_down, v_g_mix, v_g_ffn, v_g_final, v_w_in_a, v_g_v, v_w_sp, v_b_sp, v_g_kv, v_w_kv_a, v_g_kv_lat, v_w_in_b, v_g_q_lat, v_w_uq, v_w_uk, v_w_uv, v_g_mem, v_w_mem_kv, v_w_out, v_w_ffn_up, v_conv_w, v_conv_b, v_w_ffn_down):
    return _step((x, mem, positions, g_mix, g_ffn, g_final, w_in_a, g_v, w_sp, b_sp, g_kv, w_kv_a, g_kv_lat, w_in_b, g_q_lat, w_uq, w_uk, w_uv, g_mem, w_mem_kv, w_out, w_ffn_up, conv_w, conv_b, w_ffn_down, loss_target, m_g_mix, m_g_ffn, m_g_final, m_w_in_a, m_g_v, m_w_sp, m_b_sp, m_g_kv, m_w_kv_a, m_g_kv_lat, m_w_in_b, m_g_q_lat, m_w_uq, m_w_uk, m_w_uv, m_g_mem, m_w_mem_kv, m_w_out, m_w_ffn_up, m_conv_w, m_conv_b, m_w_ffn_down, v_g_mix, v_g_ffn, v_g_final, v_w_in_a, v_g_v, v_w_sp, v_b_sp, v_g_kv, v_w_kv_a, v_g_kv_lat, v_w_in_b, v_g_q_lat, v_w_uq, v_w_uk, v_w_uv, v_g_mem, v_w_mem_kv, v_w_out, v_w_ffn_up, v_conv_w, v_conv_b, v_w_ffn_down))
```

```python
import math

import jax
import jax.numpy as jnp
from jax import lax
from jax.experimental import pallas as pl
from jax.experimental.pallas import tpu as pltpu

F32 = jnp.float32
BF16 = jnp.bfloat16

N_DEV = 8
N_A = 2
DEPTH = 4
G_HEADS = 12
HEAD = 128
CHUNK = 128
MEM_HEADS = 4
MEM_W = MEM_HEADS * HEAD
G_W = G_HEADS * HEAD
ROPE_DIM = 64
ROPE_HALF = ROPE_DIM // 2
KV_RANK = 512
Q_RANK = 512
KV_PAD = 640
ROPE_THETA = 10000.0
EPS = 1e-6
CONV_W = 3

ADAM_LR = 0.001
ADAM_B1 = 0.9
ADAM_B2 = 0.999
ADAM_EPS = 1e-08
ADAM_WD = 0.01
ADAM_STEP = 10

VMEM_LIMIT_V7X = 56 * 1024 * 1024
MASK_VALUE = -1e30

WEIGHTS = ['g_mix', 'g_ffn', 'g_final', 'w_in_a', 'g_v', 'w_sp', 'b_sp', 'g_kv', 'w_kv_a', 'g_kv_lat',
           'w_in_b', 'g_q_lat', 'w_uq', 'w_uk', 'w_uv', 'g_mem', 'w_mem_kv', 'w_out', 'w_ffn_up',
           'conv_w', 'conv_b', 'w_ffn_down']
REPLICATED = ['g_mix', 'g_ffn', 'g_final', 'w_sp', 'b_sp', 'g_kv', 'g_kv_lat', 'g_q_lat', 'g_mem', 'conv_b']
SMALL_SHARDED = ['g_v', 'conv_w']


def _params(sem=None):
    return pltpu.CompilerParams(dimension_semantics=sem, vmem_limit_bytes=VMEM_LIMIT_V7X)


def _dot(a, b, dims):
    contract = {'nn': ((1,), (0,)), 'nt': ((1,), (1,)), 'tn': ((0,), (0,))}[dims]
    return lax.dot_general(a, b, (contract, ((), ())), preferred_element_type=F32)


def _erf(x):
    return lax.erf(x)


def _gelu(x):
    return 0.5 * x * (1.0 + _erf(x * (2.0 ** -0.5)))


def _gelu_grad(x):
    cdf = 0.5 * (1.0 + _erf(x * (2.0 ** -0.5)))
    pdf = jnp.exp(-0.5 * x * x) * (1.0 / math.sqrt(2.0 * math.pi))
    return cdf + x * pdf


def _sigmoid(x):
    return 1.0 / (1.0 + jnp.exp(-x))


def _operand_spec(shape, lead, blocked, tr, tc, ridx, cidx):
    if blocked:
        per = shape[-1] // tc
        assert shape[-1] % tc == 0, (shape, tc)
        return pl.BlockSpec(
            (None,) * (1 + len(lead)) + (tr, tc),
            lambda *g: (cidx(*g) // per,) + lead + (ridx(*g), cidx(*g) % per))
    return pl.BlockSpec((None,) * len(lead) + (tr, tc), lambda *g: lead + (ridx(*g), cidx(*g)))


def _view2d(x, blocked):
    return (x.shape[-2], x.shape[0] * x.shape[-1]) if blocked else (x.shape[-2], x.shape[-1])


def _mm(a, b, *, dims, out_dtype, name, tm, tn, tk=None, res=None, a_lead=(), b_lead=(),
        a_blocked=False, b_blocked=False, out_block=None, n_outer=False):
    ar, ac = _view2d(a, a_blocked)
    br, bc = _view2d(b, b_blocked)
    m, k = (ac, ar) if dims == 'tn' else (ar, ac)
    n, k2 = (br, bc) if dims == 'nt' else (bc, br)
    assert k == k2, (a.shape, b.shape, dims)
    tm, tn = min(tm, m), min(tn, n)
    tk = k if tk is None else tk
    assert m % tm == 0 and n % tn == 0 and k % tk == 0, (name, m, n, k, tm, tn, tk)
    nk = k // tk
    if n_outer:
        gi, gj = (lambda g0, g1, g2: g1), (lambda g0, g1, g2: g0)
        grid = (n // tn, m // tm, nk)
    else:
        gi, gj = (lambda g0, g1, g2: g0), (lambda g0, g1, g2: g1)
        grid = (m // tm, n // tn, nk)
    gk = lambda g0, g1, g2: g2

    if dims == 'tn':
        a_spec = _operand_spec(a.shape, a_lead, a_blocked, tk, tm, gk, gi)
    else:
        a_spec = _operand_spec(a.shape, a_lead, a_blocked, tm, tk, gi, gk)
    if dims == 'nt':
        b_spec = _operand_spec(b.shape, b_lead, b_blocked, tn, tk, gj, gk)
    else:
        b_spec = _operand_spec(b.shape, b_lead, b_blocked, tk, tn, gk, gj)
    in_specs = [a_spec, b_spec]
    operands = [a, b]
    if res is not None:
        in_specs.append(pl.BlockSpec((tm, tn), lambda *g: (gi(*g), gj(*g))))
        operands.append(res)
    if out_block is not None:
        out_shape = jax.ShapeDtypeStruct((n // out_block, m, out_block), out_dtype)
        out_spec = _operand_spec(out_shape.shape, (), True, tm, tn, gi, gj)
    else:
        out_shape = jax.ShapeDtypeStruct((m, n), out_dtype)
        out_spec = pl.BlockSpec((tm, tn), lambda *g: (gi(*g), gj(*g)))

    def body(*refs):
        a_ref, b_ref = refs[0], refs[1]
        r_ref = refs[2] if res is not None else None
        o_ref = refs[3] if res is not None else refs[2]
        acc_ref = refs[-1] if nk > 1 else None
        part = _dot(a_ref[...].astype(BF16), b_ref[...].astype(BF16), dims)

        def finish(total):
            if r_ref is not None:
                total = total + r_ref[...]
            o_ref[...] = total.astype(o_ref.dtype)

        if nk == 1:
            finish(part)
        else:
            kk = pl.program_id(2)

            @pl.when(kk == 0)
            def _():
                acc_ref[...] = part

            @pl.when(kk > 0)
            def _():
                acc_ref[...] += part

            @pl.when(kk == nk - 1)
            def _():
                finish(acc_ref[...])

    scratch = [pltpu.VMEM((tm, tn), F32)] if nk > 1 else []
    return pl.pallas_call(
        body, name=name, grid=grid, in_specs=in_specs, out_specs=out_spec,
        out_shape=out_shape, scratch_shapes=scratch,
        compiler_params=_params(("parallel", "parallel", "arbitrary")),
    )(*operands)


def _mm_heads(a, b, *, mode, name, out_dtype=BF16, tm=512):
    if mode in ('to_lat', 'from_lat'):
        r = b.shape[0]
        s = a.shape[0] if mode == 'to_lat' else a.shape[1]
        tm = min(tm, s)
        flat = pl.BlockSpec((tm, G_W), lambda i: (i, 0))
        per_head = pl.BlockSpec((G_HEADS, tm, r), lambda i: (0, i, 0))

        def all_heads(a_ref, b_ref, o_ref):
            for h in range(G_HEADS):
                cols = slice(h * HEAD, (h + 1) * HEAD)
                if mode == 'to_lat':
                    o_ref[h] = _dot(a_ref[:, cols].astype(BF16), b_ref[:, cols].astype(BF16), 'nt').astype(o_ref.dtype)
                else:
                    o_ref[:, cols] = _dot(a_ref[h].astype(BF16), b_ref[:, cols].astype(BF16), 'nn').astype(o_ref.dtype)

        return pl.pallas_call(
            all_heads, name=name, grid=(s // tm,),
            in_specs=[flat if mode == 'to_lat' else per_head, pl.BlockSpec((r, G_W), lambda i: (0, 0))],
            out_specs=per_head if mode == 'to_lat' else flat,
            out_shape=jax.ShapeDtypeStruct((G_HEADS, s, r) if mode == 'to_lat' else (s, G_W), out_dtype),
            compiler_params=_params(("parallel",)),
        )(a, b)
    else:
        _, s, r = a.shape
        grid = (G_HEADS, 1)
        in_specs = [pl.BlockSpec((None, s, r), lambda h, i: (h, 0, 0)),
                    pl.BlockSpec((s, HEAD), lambda h, i: (0, h))]
        out_spec = pl.BlockSpec((r, HEAD), lambda h, i: (0, h))
        out_shape = jax.ShapeDtypeStruct((r, G_W), out_dtype)
        dims = 'tn'

    def body(a_ref, b_ref, o_ref):
        o_ref[...] = _dot(a_ref[...].astype(BF16), b_ref[...].astype(BF16), dims).astype(o_ref.dtype)

    return pl.pallas_call(
        body, name=name, grid=grid, in_specs=in_specs, out_specs=out_spec, out_shape=out_shape,
        compiler_params=_params(("parallel", "parallel")),
    )(a, b)


def _rmsnorm(x, g, *, name, width=None, out_dtype=BF16, tm=256):
    s = x.shape[0]
    w = x.shape[1] if width is None else width
    tm = min(tm, s)

    def body(x_ref, g_ref, o_ref):
        xv = x_ref[...].astype(F32)
        rstd = lax.rsqrt(jnp.mean(xv * xv, axis=-1, keepdims=True) + EPS)
        o_ref[...] = (xv * rstd * g_ref[...]).astype(o_ref.dtype)

    return pl.pallas_call(
        body, name=name, grid=(s // tm,),
        in_specs=[pl.BlockSpec((tm, w), lambda i: (i, 0)), pl.BlockSpec((1, w), lambda i: (0, 0))],
        out_specs=pl.BlockSpec((tm, w), lambda i: (i, 0)),
        out_shape=jax.ShapeDtypeStruct((s, w), out_dtype),
        compiler_params=_params(("parallel",)),
    )(x, g.reshape(1, w))


def _rmsnorm_bwd(x, g, dy, *, name, width=None, dres=None, after=None, out_dtype=F32, tm=256):
    s = x.shape[0]
    w = x.shape[1] if width is None else width
    tm = min(tm, s)

    def body(*refs):
        x_ref, g_ref, dy_ref = refs[0], refs[1], refs[2]
        r_ref = refs[3] if dres is not None else None
        dx_ref, dg_ref = refs[-2], refs[-1]
        xv = x_ref[...].astype(F32)
        rstd = lax.rsqrt(jnp.mean(xv * xv, axis=-1, keepdims=True) + EPS)
        xhat = xv * rstd
        dyv = dy_ref[...].astype(F32)
        gdy = dyv * g_ref[...]
        dx = rstd * (gdy - xhat * jnp.mean(gdy * xhat, axis=-1, keepdims=True))
        if r_ref is not None:
            dx = dx + r_ref[...]
        dx_ref[...] = dx.astype(dx_ref.dtype)
        part = jnp.sum(dyv * xhat, axis=0, keepdims=True)

        @pl.when(pl.program_id(0) == 0)
        def _():
            dg_ref[...] = part

        @pl.when(pl.program_id(0) > 0)
        def _():
            dg_ref[...] += part

    row = pl.BlockSpec((tm, w), lambda i: (i, 0))
    vec = pl.BlockSpec((1, w), lambda i: (0, 0))
    in_specs = [row, vec, row] + ([row] if dres is not None else [])
    operands = [x, g.reshape(1, w), dy] + ([dres] if dres is not None else [])
    if after is not None:
        in_specs.append(pl.BlockSpec(memory_space=pl.ANY))
        operands.append(after)
    return pl.pallas_call(
        body, name=name, grid=(s // tm,), in_specs=in_specs, out_specs=[row, vec],
        out_shape=[jax.ShapeDtypeStruct((s, w), out_dtype), jax.ShapeDtypeStruct((1, w), F32)],
        compiler_params=_params(("arbitrary",)),
    )(*operands)


def _final_loss(x, target, g, *, name, tm=256):
    s, d = x.shape
    tm = min(tm, s)

    def body(x_ref, t_ref, g_ref, sq_ref, dx_ref, dg_ref):
        xv = x_ref[...]
        rstd = lax.rsqrt(jnp.mean(xv * xv, axis=-1, keepdims=True) + EPS)
        xhat = xv * rstd
        err = xhat * g_ref[...] - t_ref[...]
        dyv = err * (1.0 / d)
        gdy = dyv * g_ref[...]
        dx_ref[...] = rstd * (gdy - xhat * jnp.mean(gdy * xhat, axis=-1, keepdims=True))
        sq = jnp.sum(err * err, axis=0, keepdims=True)
        dg = jnp.sum(dyv * xhat, axis=0, keepdims=True)

        @pl.when(pl.program_id(0) == 0)
        def _():
            sq_ref[...] = sq
            dg_ref[...] = dg

        @pl.when(pl.program_id(0) > 0)
        def _():
            sq_ref[...] += sq
            dg_ref[...] += dg

    row = pl.BlockSpec((tm, d), lambda i: (i, 0))
    vec = pl.BlockSpec((1, d), lambda i: (0, 0))
    return pl.pallas_call(
        body, name=name, grid=(s // tm,), in_specs=[row, row, vec], out_specs=[vec, row, vec],
        out_shape=[jax.ShapeDtypeStruct((1, d), F32), jax.ShapeDtypeStruct((s, d), F32),
                   jax.ShapeDtypeStruct((1, d), F32)],
        compiler_params=_params(("arbitrary",)),
    )(x, target, g.reshape(1, d))


def _tril_mask():
    t = lax.broadcasted_iota(jnp.int32, (CHUNK, CHUNK), 0)
    s = lax.broadcasted_iota(jnp.int32, (CHUNK, CHUNK), 1)
    return t >= s


def _sgu_fwd(z, g_v, w_sp, b_sp_t, *, name):
    s = z.shape[0]

    def body(zu_ref, zv_ref, g_ref, w_ref, b_ref, o_ref):
        u = _gelu(zu_ref[...].astype(F32))
        gv = _gelu(zv_ref[...].astype(F32))
        rstd = lax.rsqrt(jnp.mean(gv * gv, axis=-1, keepdims=True) + EPS)
        v = (gv * rstd * g_ref[...]).astype(BF16)
        mask = _tril_mask()
        for grp in range(G_HEADS):
            cols = slice(grp * HEAD, (grp + 1) * HEAD)
            wm = jnp.where(mask, w_ref[grp], 0.0).astype(BF16)
            sv = _dot(wm, v[:, cols], 'nn') + b_ref[:, grp:grp + 1]
            o_ref[:, cols] = (u[:, cols] * sv).astype(o_ref.dtype)

    return pl.pallas_call(
        body, name=name, grid=(s // CHUNK,),
        in_specs=[pl.BlockSpec((CHUNK, G_W), lambda i: (i, 0)),
                  pl.BlockSpec((CHUNK, G_W), lambda i: (i, 1)),
                  pl.BlockSpec((1, G_W), lambda i: (0, 0)),
                  pl.BlockSpec((G_HEADS, CHUNK, CHUNK), lambda i: (0, 0, 0)),
                  pl.BlockSpec((CHUNK, G_HEADS), lambda i: (0, 0))],
        out_specs=pl.BlockSpec((CHUNK, G_W), lambda i: (i, 0)),
        out_shape=jax.ShapeDtypeStruct((s, G_W), BF16),
        compiler_params=_params(("parallel",)),
    )(z, z, g_v.reshape(1, G_W), w_sp, b_sp_t)


def _sgu_bwd(z, dmix, dqm, g_v, w_sp, b_sp_t, *, name):
    s = z.shape[0]
    zw = z.shape[1]

    def body(zu_ref, zv_ref, dm_ref, dq_ref, g_ref, w_ref, b_ref, dz_ref, dw_ref, db_ref, dg_ref):
        first = pl.program_id(0) == 0

        @pl.when(first)
        def _():
            dw_ref[...] = jnp.zeros_like(dw_ref)
            db_ref[...] = jnp.zeros_like(db_ref)
            dg_ref[...] = jnp.zeros_like(dg_ref)

        zu = zu_ref[...].astype(F32)
        zv = zv_ref[...].astype(F32)
        dmain = dm_ref[...].astype(F32)
        u = _gelu(zu)
        gv = _gelu(zv)
        rstd = lax.rsqrt(jnp.mean(gv * gv, axis=-1, keepdims=True) + EPS)
        vhat = gv * rstd
        gvec = g_ref[...]
        v = (vhat * gvec).astype(BF16)
        dsv = dmain * u
        dsv_b = dsv.astype(BF16)
        mask = _tril_mask()
        dv_parts = []
        for grp in range(G_HEADS):
            cols = slice(grp * HEAD, (grp + 1) * HEAD)
            wm = jnp.where(mask, w_ref[grp], 0.0).astype(BF16)
            sv = _dot(wm, v[:, cols], 'nn') + b_ref[:, grp:grp + 1]
            dz_ref[:, cols] = (dmain[:, cols] * sv * _gelu_grad(zu[:, cols])).astype(dz_ref.dtype)
            dwg = _dot(dsv_b[:, cols], v[:, cols], 'nt')
            dw_ref[grp] += jnp.where(mask, dwg, 0.0)
            db_ref[:, grp:grp + 1] += jnp.sum(dsv[:, cols], axis=-1, keepdims=True)
            dv_parts.append(_dot(wm, dsv_b[:, cols], 'tn'))
        dv = jnp.concatenate(dv_parts, axis=-1)
        dg_ref[...] += jnp.sum(dv * vhat, axis=0, keepdims=True)
        gdv = dv * gvec
        dgv = rstd * (gdv - vhat * jnp.mean(gdv * vhat, axis=-1, keepdims=True))
        dz_ref[:, G_W:2 * G_W] = (dgv * _gelu_grad(zv)).astype(dz_ref.dtype)
        dz_ref[:, 2 * G_W:] = dq_ref[...].astype(dz_ref.dtype)

    return pl.pallas_call(
        body, name=name, grid=(s // CHUNK,),
        in_specs=[pl.BlockSpec((CHUNK, G_W), lambda i: (i, 0)),
                  pl.BlockSpec((CHUNK, G_W), lambda i: (i, 1)),
                  pl.BlockSpec((CHUNK, G_W), lambda i: (i, 0)),
                  pl.BlockSpec((CHUNK, MEM_W), lambda i: (i, 0)),
                  pl.BlockSpec((1, G_W), lambda i: (0, 0)),
                  pl.BlockSpec((G_HEADS, CHUNK, CHUNK), lambda i: (0, 0, 0)),
                  pl.BlockSpec((CHUNK, G_HEADS), lambda i: (0, 0))],
        out_specs=[pl.BlockSpec((CHUNK, zw), lambda i: (i, 0)),
                   pl.BlockSpec((G_HEADS, CHUNK, CHUNK), lambda i: (0, 0, 0)),
                   pl.BlockSpec((CHUNK, G_HEADS), lambda i: (0, 0)),
                   pl.BlockSpec((1, G_W), lambda i: (0, 0))],
        out_shape=[jax.ShapeDtypeStruct((s, zw), BF16),
                   jax.ShapeDtypeStruct((G_HEADS, CHUNK, CHUNK), F32),
                   jax.ShapeDtypeStruct((CHUNK, G_HEADS), F32),
                   jax.ShapeDtypeStruct((1, G_W), F32)],
        compiler_params=_params(("arbitrary",)),
    )(z, z, dmix, dqm, g_v.reshape(1, G_W), w_sp, b_sp_t)


def _mem_probs(q, k):
    sc = _dot(q, k, 'nt') * (HEAD ** -0.5)
    sc = sc - jnp.max(sc, axis=-1, keepdims=True)
    e = jnp.exp(sc)
    return e / jnp.sum(e, axis=-1, keepdims=True)


def _memattn_fwd(z, kvm, main, *, qcol, name, tm=512):
    s = z.shape[0]
    m = kvm.shape[0]
    tm = min(tm, s)

    def body(q_ref, kv_ref, main_ref, o_ref):
        o_ref[:, :G_W] = main_ref[...]
        for h in range(MEM_HEADS):
            cols = slice(h * HEAD, (h + 1) * HEAD)
            k = kv_ref[:, cols]
            v = kv_ref[:, MEM_W + h * HEAD:MEM_W + (h + 1) * HEAD]
            p = _mem_probs(q_ref[:, cols], k)
            o_ref[:, G_W + h * HEAD:G_W + (h + 1) * HEAD] = _dot(p.astype(BF16), v, 'nn').astype(o_ref.dtype)

    return pl.pallas_call(
        body, name=name, grid=(s // tm,),
        in_specs=[pl.BlockSpec((tm, MEM_W), lambda i: (i, qcol)),
                  pl.BlockSpec((m, 2 * MEM_W), lambda i: (0, 0)),
                  pl.BlockSpec((tm, G_W), lambda i: (i, 0))],
        out_specs=pl.BlockSpec((tm, G_W + MEM_W), lambda i: (i, 0)),
        out_shape=jax.ShapeDtypeStruct((s, G_W + MEM_W), BF16),
        compiler_params=_params(("parallel",)),
    )(z, kvm, main)


def _memattn_bwd(z, kvm, dmix, *, qcol, name, tm=512):
    s = z.shape[0]
    m = kvm.shape[0]
    tm = min(tm, s)
    scale = HEAD ** -0.5

    def body(q_ref, kv_ref, do_ref, dq_ref, dkv_ref):
        @pl.when(pl.program_id(0) == 0)
        def _():
            dkv_ref[...] = jnp.zeros_like(dkv_ref)

        for h in range(MEM_HEADS):
            cols = slice(h * HEAD, (h + 1) * HEAD)
            vcols = slice(MEM_W + h * HEAD, MEM_W + (h + 1) * HEAD)
            q = q_ref[:, cols]
            k = kv_ref[:, cols]
            v = kv_ref[:, vcols]
            do = do_ref[:, cols]
            p = _mem_probs(q, k)
            dp = _dot(do, v, 'nt')
            ds = (p * (dp - jnp.sum(dp * p, axis=-1, keepdims=True)) * scale).astype(BF16)
            dq_ref[:, cols] = _dot(ds, k, 'nn').astype(dq_ref.dtype)
            dkv_ref[:, cols] += _dot(ds, q, 'tn')
            dkv_ref[:, vcols] += _dot(p.astype(BF16), do, 'tn')

    mo_block = G_W // MEM_W
    return pl.pallas_call(
        body, name=name, grid=(s // tm,),
        in_specs=[pl.BlockSpec((tm, MEM_W), lambda i: (i, qcol)),
                  pl.BlockSpec((m, 2 * MEM_W), lambda i: (0, 0)),
                  pl.BlockSpec((tm, MEM_W), lambda i: (i, mo_block))],
        out_specs=[pl.BlockSpec((tm, MEM_W), lambda i: (i, 0)),
                   pl.BlockSpec((m, 2 * MEM_W), lambda i: (0, 0))],
        out_shape=[jax.ShapeDtypeStruct((s, MEM_W), BF16), jax.ShapeDtypeStruct((m, 2 * MEM_W), F32)],
        compiler_params=_params(("arbitrary",)),
    )(z, kvm, dmix)


def _rope(x1, x2, cos, sin, *, name, inverse=False, out_dtype=BF16, col1=0, col2=0, tm=512):
    s, w = cos.shape
    tm = min(tm, s)
    sign = -1.0 if inverse else 1.0

    def body(a_ref, b_ref, c_ref, s_ref, o1_ref, o2_ref):
        a = a_ref[...].astype(F32)
        b = b_ref[...].astype(F32)
        c = c_ref[...]
        sn = s_ref[...] * sign
        o1_ref[...] = (a * c - b * sn).astype(o1_ref.dtype)
        o2_ref[...] = (b * c + a * sn).astype(o2_ref.dtype)

    row = pl.BlockSpec((tm, w), lambda i: (i, 0))
    return pl.pallas_call(
        body, name=name, grid=(s // tm,),
        in_specs=[pl.BlockSpec((tm, w), lambda i: (i, col1)), pl.BlockSpec((tm, w), lambda i: (i, col2)), row, row],
        out_specs=[row, row],
        out_shape=[jax.ShapeDtypeStruct((s, w), out_dtype)] * 2,
        compiler_params=_params(("parallel",)),
    )(x1, x2, cos, sin)


MLA_CHAINS = 2


def _mla_scores(q1, q2, k1, k2, row_tok, kstart, tk, scale):
    sc = (_dot(q1, k1, 'nt') + _dot(q2, k2, 'nt')) * scale
    kpos = kstart + lax.broadcasted_iota(jnp.int32, (1, tk), 1)
    keep = kpos <= row_tok
    return sc, keep


def _mla_fwd(qa, qr, ckv, kr, *, name, tk=512):
    hh, s, r = qa.shape
    tq = CHUNK
    tk = min(tk, s)
    rows = hh * tq
    hc = hh // MLA_CHAINS
    rc = hc * tq
    scale = (HEAD + ROPE_DIM) ** -0.5

    def body(qa_ref, qr_ref, ckv_ref, kr_ref, o_ref, lse_ref, m_ref, l_ref, acc_ref):
        i = pl.program_id(0)
        row_tok = i * tq + (lax.broadcasted_iota(jnp.int32, (rc, 1), 0) & (tq - 1))
        m_ref[...] = jnp.full_like(m_ref, MASK_VALUE)
        l_ref[...] = jnp.zeros_like(l_ref)
        acc_ref[...] = jnp.zeros_like(acc_ref)

        def step(j, carry):
            kstart = pl.multiple_of(j * tk, tk)
            k1 = ckv_ref[pl.ds(kstart, tk), :]
            k2 = kr_ref[pl.ds(kstart, tk), :]
            for c in range(MLA_CHAINS):
                hs, rs = slice(c * hc, (c + 1) * hc), slice(c * rc, (c + 1) * rc)
                q1 = qa_ref[hs].reshape(rc, r)
                q2 = qr_ref[hs].reshape(rc, ROPE_DIM)
                sc, keep = _mla_scores(q1, q2, k1, k2, row_tok, kstart, tk, scale)
                sc = jnp.where(keep, sc, MASK_VALUE)
                m_old = m_ref[rs, :]
                m_new = jnp.maximum(m_old, jnp.max(sc, axis=-1, keepdims=True))
                p = jnp.exp(sc - m_new)
                alpha = jnp.exp(m_old - m_new)
                l_ref[rs, :] = alpha * l_ref[rs, :] + jnp.sum(p, axis=-1, keepdims=True)
                acc_ref[rs, :] = alpha * acc_ref[rs, :] + _dot(p.astype(BF16), k1, 'nn')
                m_ref[rs, :] = m_new
            return carry

        lax.fori_loop(0, (i * tq) // tk + 1, step, 0)
        l = l_ref[...]
        o_ref[...] = (acc_ref[...] / l).astype(o_ref.dtype).reshape(hh, tq, r)
        lse_ref[...] = (m_ref[...] + jnp.log(l)).reshape(hh, tq, 1)

    return pl.pallas_call(
        body, name=name, grid=(s // tq,),
        in_specs=[pl.BlockSpec((hh, tq, r), lambda i: (0, i, 0)),
                  pl.BlockSpec((hh, tq, ROPE_DIM), lambda i: (0, i, 0)),
                  pl.BlockSpec((s, r), lambda i: (0, 0)),
                  pl.BlockSpec((s, ROPE_DIM), lambda i: (0, 0))],
        out_specs=[pl.BlockSpec((hh, tq, r), lambda i: (0, i, 0)),
                   pl.BlockSpec((hh, tq, 1), lambda i: (0, i, 0))],
        out_shape=[jax.ShapeDtypeStruct((hh, s, r), BF16), jax.ShapeDtypeStruct((hh, s, 1), F32)],
        scratch_shapes=[pltpu.VMEM((rows, 1), F32), pltpu.VMEM((rows, 1), F32), pltpu.VMEM((rows, r), F32)],
        compiler_params=_params(("parallel",)),
    )(qa, qr, ckv, kr)


def _mla_bwd(qa, qr, ckv, kr, o, do, lse, *, name, tk=256):
    hh, s, r = qa.shape
    tq = CHUNK
    tk = min(tk, s)
    rows = hh * tq
    hc = hh // MLA_CHAINS
    rc = hc * tq
    nq = s // tq
    scale = (HEAD + ROPE_DIM) ** -0.5

    def body(qa_ref, qr_ref, o_ref, do_ref, lse_ref, ckv_hbm, kr_hbm, dqa_ref, dqr_ref, dckv_hbm, dkr_hbm,
             ckv_ref, kr_ref, dckv_ref, dkr_ref, dq1_ref, dq2_ref, delta_ref, sem):
        i = pl.program_id(0)

        @pl.when(i == 0)
        def _():
            c1 = pltpu.make_async_copy(ckv_hbm, ckv_ref, sem.at[0])
            c2 = pltpu.make_async_copy(kr_hbm, kr_ref, sem.at[1])
            c1.start()
            c2.start()
            dckv_ref[...] = jnp.zeros_like(dckv_ref)
            dkr_ref[...] = jnp.zeros_like(dkr_ref)
            c1.wait()
            c2.wait()

        delta_ref[...] = jnp.sum(do_ref[...].reshape(rows, r).astype(F32) * o_ref[...].reshape(rows, r).astype(F32),
                                 axis=-1, keepdims=True)
        row_tok = i * tq + (lax.broadcasted_iota(jnp.int32, (rc, 1), 0) & (tq - 1))
        dq1_ref[...] = jnp.zeros_like(dq1_ref)
        dq2_ref[...] = jnp.zeros_like(dq2_ref)

        def step(j, carry):
            kstart = pl.multiple_of(j * tk, tk)
            k1 = ckv_ref[pl.ds(kstart, tk), :]
            k2 = kr_ref[pl.ds(kstart, tk), :]
            dk1, dk2 = None, None
            for c in range(MLA_CHAINS):
                hs, rs = slice(c * hc, (c + 1) * hc), slice(c * rc, (c + 1) * rc)
                q1 = qa_ref[hs].reshape(rc, r)
                q2 = qr_ref[hs].reshape(rc, ROPE_DIM)
                dov = do_ref[hs].reshape(rc, r)
                sc, keep = _mla_scores(q1, q2, k1, k2, row_tok, kstart, tk, scale)
                p = jnp.where(keep, jnp.exp(sc - lse_ref[hs].reshape(rc, 1)), 0.0)
                dp = _dot(dov, k1, 'nt')
                ds = (p * (dp - delta_ref[rs, :]) * scale).astype(BF16)
                pb = p.astype(BF16)
                dq1_ref[rs, :] += _dot(ds, k1, 'nn')
                dq2_ref[rs, :] += _dot(ds, k2, 'nn')
                part1 = _dot(ds, q1, 'tn') + _dot(pb, dov, 'tn')
                part2 = _dot(ds, q2, 'tn')
                dk1 = part1 if dk1 is None else dk1 + part1
                dk2 = part2 if dk2 is None else dk2 + part2
            dckv_ref[pl.ds(kstart, tk), :] += dk1
            dkr_ref[pl.ds(kstart, tk), :] += dk2
            return carry

        lax.fori_loop(0, (i * tq) // tk + 1, step, 0)
        dqa_ref[...] = dq1_ref[...].astype(dqa_ref.dtype).reshape(hh, tq, r)
        dqr_ref[...] = dq2_ref[...].astype(dqr_ref.dtype).reshape(hh, tq, ROPE_DIM)

        @pl.when(i == nq - 1)
        def _():
            c1 = pltpu.make_async_copy(dckv_ref, dckv_hbm, sem.at[0])
            c2 = pltpu.make_async_copy(dkr_ref, dkr_hbm, sem.at[1])
            c1.start()
            c2.start()
            c1.wait()
            c2.wait()

    blk = pl.BlockSpec((hh, tq, r), lambda i: (0, i, 0))
    blk_r = pl.BlockSpec((hh, tq, ROPE_DIM), lambda i: (0, i, 0))
    any_spec = pl.BlockSpec(memory_space=pl.ANY)
    return pl.pallas_call(
        body, name=name, grid=(nq,),
        in_specs=[blk, blk_r, blk, blk, pl.BlockSpec((hh, tq, 1), lambda i: (0, i, 0)), any_spec, any_spec],
        out_specs=[blk, blk_r, any_spec, any_spec],
        out_shape=[jax.ShapeDtypeStruct((hh, s, r), BF16), jax.ShapeDtypeStruct((hh, s, ROPE_DIM), BF16),
                   jax.ShapeDtypeStruct((s, r), F32), jax.ShapeDtypeStruct((s, ROPE_DIM), F32)],
        scratch_shapes=[pltpu.VMEM((s, r), BF16), pltpu.VMEM((s, ROPE_DIM), BF16),
                        pltpu.VMEM((s, r), F32), pltpu.VMEM((s, ROPE_DIM), F32),
                        pltpu.VMEM((rows, r), F32), pltpu.VMEM((rows, ROPE_DIM), F32), pltpu.VMEM((rows, 1), F32),
                        pltpu.SemaphoreType.DMA((2,))],
        compiler_params=_params(("arbitrary",)),
    )(qa, qr, o, do, lse, ckv, kr)


HALO = 16


def _shift_down(prev, cur, shift, first_tile):
    tr = cur.shape[0]
    full = jnp.concatenate([prev, cur], axis=0)
    out = pltpu.roll(full, shift, axis=0)[HALO:]
    row = lax.broadcasted_iota(jnp.int32, (tr, 1), 0)
    return jnp.where(jnp.logical_and(first_tile, row < shift), 0.0, out)


def _shift_up(cur, nxt, shift, last_tile):
    tr = cur.shape[0]
    full = jnp.concatenate([cur, nxt], axis=0)
    out = pltpu.roll(full, tr + HALO - shift, axis=0)[:tr]
    row = lax.broadcasted_iota(jnp.int32, (tr, 1), 0)
    return jnp.where(jnp.logical_and(last_tile, row >= tr - shift), 0.0, out)


def _lane_chunks(width, lanes):
    return [slice(c0, min(c0 + lanes, width)) for c0 in range(0, width, lanes)]


def _conv_taps(prev_ref, cur_ref, cw_ref, cb_ref, first_tile, cs):
    cur = cur_ref[:, cs].astype(F32)
    prev = prev_ref[:, cs].astype(F32)
    a1 = _shift_down(prev, cur, 1, first_tile)
    a2 = _shift_down(prev, cur, 2, first_tile)
    c = a2 * cw_ref[0:1, cs] + a1 * cw_ref[1:2, cs] + cur * cw_ref[2:3, cs] + cb_ref[:, cs]
    return c, (a2, a1, cur)


def _conv_in_specs(tr, bw, half, layer, row_of, blk_of):
    per = tr // HALO
    specs = []
    for off in (0, half):
        specs.append(pl.BlockSpec((None, HALO, bw), lambda *g, off=off: (blk_of(*g) + off, jnp.maximum(row_of(*g) * per - 1, 0), 0)))
        specs.append(pl.BlockSpec((None, tr, bw), lambda *g, off=off: (blk_of(*g) + off, row_of(*g), 0)))
    for off in (0, half):
        specs.append(pl.BlockSpec((None, None, CONV_W, bw), lambda *g, off=off: (blk_of(*g) + off, layer, 0, 0)))
    for off in (0, half):
        specs.append(pl.BlockSpec((None, 1, bw), lambda *g, off=off: (layer * 2 * half + blk_of(*g) + off, 0, 0)))
    return specs


def _conv_fwd(a, cw, cb, layer, *, name, tr=256):
    nb, s, bw = a.shape
    half = nb // 2
    tr = min(tr, s)

    def body(gp_ref, gc_ref, vp_ref, vc_ref, cwg_ref, cwv_ref, cbg_ref, cbv_ref, o_ref):
        first = pl.program_id(0) == 0
        for cs in _lane_chunks(bw, 256):
            gate, _ = _conv_taps(gp_ref, gc_ref, cwg_ref, cbg_ref, first, cs)
            val, _ = _conv_taps(vp_ref, vc_ref, cwv_ref, cbv_ref, first, cs)
            o_ref[:, cs] = (gate * _sigmoid(gate) * val).astype(o_ref.dtype)

    return pl.pallas_call(
        body, name=name, grid=(s // tr, half),
        in_specs=_conv_in_specs(tr, bw, half, layer, lambda i, j: i, lambda i, j: j),
        out_specs=pl.BlockSpec((None, tr, bw), lambda i, j: (j, i, 0)),
        out_shape=jax.ShapeDtypeStruct((half, s, bw), BF16),
        compiler_params=_params(("parallel", "parallel")),
    )(a, a, a, a, cw, cw, cb, cb)


def _conv_bwd_dc(a, dact, cw, cb, layer, *, name, after=None, tr=256):
    nb, s, bw = a.shape
    half = nb // 2
    tr = min(tr, s)

    def body(*refs):
        gp_ref, gc_ref, vp_ref, vc_ref, cwg_ref, cwv_ref, cbg_ref, cbv_ref, da_ref = refs[:9]
        dc_ref, dw_ref, db_ref = refs[-3:]
        first = pl.program_id(1) == 0

        @pl.when(first)
        def _():
            dw_ref[...] = jnp.zeros_like(dw_ref)
            db_ref[...] = jnp.zeros_like(db_ref)

        for cs in _lane_chunks(bw, 128):
            gate, gtaps = _conv_taps(gp_ref, gc_ref, cwg_ref, cbg_ref, first, cs)
            val, vtaps = _conv_taps(vp_ref, vc_ref, cwv_ref, cbv_ref, first, cs)
            dact_v = da_ref[:, cs].astype(F32)
            sg = _sigmoid(gate)
            dgate = dact_v * val * (sg * (1.0 + gate * (1.0 - sg)))
            dval = dact_v * (gate * sg)
            dc_ref[0, :, cs] = dgate.astype(dc_ref.dtype)
            dc_ref[1, :, cs] = dval.astype(dc_ref.dtype)
            for kk in range(CONV_W):
                dw_ref[0, kk:kk + 1, cs] += jnp.sum(dgate * gtaps[kk], axis=0, keepdims=True)
                dw_ref[1, kk:kk + 1, cs] += jnp.sum(dval * vtaps[kk], axis=0, keepdims=True)
            db_ref[0, :, cs] += jnp.sum(dgate, axis=0, keepdims=True)
            db_ref[1, :, cs] += jnp.sum(dval, axis=0, keepdims=True)

    outs = pl.pallas_call(
        body, name=name, grid=(half, s // tr),
        in_specs=_conv_in_specs(tr, bw, half, layer, lambda j, i: i, lambda j, i: j)
        + [pl.BlockSpec((None, tr, bw), lambda j, i: (j, i, 0))]
        + ([pl.BlockSpec(memory_space=pl.ANY)] if after is not None else []),
        out_specs=[pl.BlockSpec((2, None, tr, bw), lambda j, i: (0, j, i, 0)),
                   pl.BlockSpec((2, None, CONV_W, bw), lambda j, i: (0, j, 0, 0)),
                   pl.BlockSpec((2, None, 1, bw), lambda j, i: (0, j, 0, 0))],
        out_shape=[jax.ShapeDtypeStruct((2, half, s, bw), BF16),
                   jax.ShapeDtypeStruct((2, half, CONV_W, bw), F32),
                   jax.ShapeDtypeStruct((2, half, 1, bw), F32)],
        compiler_params=_params(("parallel", "arbitrary")),
    )(a, a, a, a, cw, cw, cb, cb, dact, *([after] if after is not None else []))
    dc, dw, db = outs
    return dc.reshape(nb, s, bw), dw.reshape(nb, CONV_W, bw), db.reshape(nb, 1, bw)


def _conv_bwd_da(dc, cw, layer, *, name, tr=512):
    nb, s, bw = dc.shape
    tr = min(tr, s)
    ni = s // tr
    per = tr // HALO
    last_halo = s // HALO - 1

    def body(c_ref, n_ref, w_ref, o_ref):
        last = pl.program_id(0) == ni - 1
        for cs in _lane_chunks(bw, 256):
            cur = c_ref[:, cs].astype(F32)
            nxt = n_ref[:, cs].astype(F32)
            da = (cur * w_ref[2:3, cs] + _shift_up(cur, nxt, 1, last) * w_ref[1:2, cs]
                  + _shift_up(cur, nxt, 2, last) * w_ref[0:1, cs])
            o_ref[:, cs] = da.astype(o_ref.dtype)

    tile = pl.BlockSpec((None, tr, bw), lambda i, j: (j, i, 0))
    return pl.pallas_call(
        body, name=name, grid=(ni, nb),
        in_specs=[tile,
                  pl.BlockSpec((None, HALO, bw), lambda i, j: (j, jnp.minimum((i + 1) * per, last_halo), 0)),
                  pl.BlockSpec((None, None, CONV_W, bw), lambda i, j: (j, layer, 0, 0))],
        out_specs=tile,
        out_shape=jax.ShapeDtypeStruct((nb, s, bw), BF16),
        compiler_params=_params(("parallel", "parallel")),
    )(dc, dc, cw)


def _rope_tables(positions):
    inv = 1.0 / (ROPE_THETA ** (jnp.arange(0, ROPE_DIM, 2, dtype=F32) / ROPE_DIM))
    ang = positions.astype(F32)[:, None] * inv
    return jnp.cos(ang), jnp.sin(ang)


def _heads_to_major(r1, r2):
    s = r1.shape[0]
    both = jnp.concatenate([r1.reshape(s, G_HEADS, ROPE_HALF), r2.reshape(s, G_HEADS, ROPE_HALF)], axis=-1)
    return both.transpose(1, 0, 2)


def _heads_from_major(qr):
    s = qr.shape[1]
    t = qr.transpose(1, 0, 2)
    return t[:, :, :ROPE_HALF].reshape(s, G_HEADS * ROPE_HALF), t[:, :, ROPE_HALF:].reshape(s, G_HEADS * ROPE_HALF)


def _local_step(x, mem, positions, target, rep, fetch, emit):
    s, d = x.shape
    n_b = DEPTH - N_A
    tm = min(1024, s)
    cos, sin = _rope_tables(positions)
    cos12 = jnp.tile(cos, (1, G_HEADS))
    sin12 = jnp.tile(sin, (1, G_HEADS))
    r1_col = G_W // (G_HEADS * ROPE_HALF)
    b_sp_t = rep['b_sp'].transpose(0, 2, 1)

    saved = []
    kv = None
    shared = None
    for l in range(DEPTH):
        wm = fetch(('in', l), x)
        if l == 0:
            shared = {'g_v': wm['g_v'], 'conv_w': wm['conv_w']}
            bw = shared['conv_w'].shape[-1]
            conv_b = rep['conv_b'].reshape(-1, 1, bw)
        sv = {'x_in': x, 'wm': wm}
        if l == N_A:
            xn_kv = _rmsnorm(x, rep['g_kv'], name="kvnorm")
            kvx = _mm(xn_kv, wm['w_kv_a'], dims='nn', out_dtype=F32, name="kvproj", tm=tm, tn=KV_PAD)
            ckv = _rmsnorm(kvx, rep['g_kv_lat'], width=KV_RANK, name="ckvnorm")
            k1, k2 = _rope(kvx[:, KV_RANK:KV_RANK + ROPE_HALF], kvx[:, KV_RANK + ROPE_HALF:KV_RANK + ROPE_DIM],
                           cos, sin, name="krope")
            kr = jnp.concatenate([k1, k2], axis=-1)
            kv = {'x': x, 'xn': xn_kv, 'kvx': kvx, 'ckv': ckv, 'kr': kr, 'w_kv_a': wm['w_kv_a']}
        h = _rmsnorm(x, rep['g_mix'][l], name=f"mixnorm{l}")
        if l < N_A:
            z = _mm(h, wm['w_in'], dims='nn', out_dtype=BF16, name=f"in_a{l}", tm=tm, tn=512)
            main = _sgu_fwd(z, shared['g_v'][l], rep['w_sp'][l], b_sp_t[l], name=f"sgu{l}")
            qcol = 2 * G_W // MEM_W
        else:
            j = l - N_A
            z = _mm(h, wm['w_in'], dims='nn', out_dtype=BF16, name=f"in_b{j}", tm=tm, tn=1024)
            qn = _rmsnorm(z, rep['g_q_lat'][j], width=Q_RANK, name=f"qnorm{j}")
            qp = _mm(qn, wm['w_uqp'], dims='nn', out_dtype=BF16, name=f"uq{j}", tm=tm, tn=768)
            rr1, rr2 = _rope(qp, qp, cos12, sin12, col1=r1_col, col2=r1_col + 1, name=f"qrope{j}")
            qr = _heads_to_major(rr1, rr2)
            qa = _mm_heads(qp, wm['w_uk'], mode='to_lat', name=f"qabsorb{j}")
            o_lat, lse = _mla_fwd(qa, qr, kv['ckv'], kv['kr'], name=f"mla{j}")
            main = _mm_heads(o_lat, wm['w_uv'], mode='from_lat', name=f"uv{j}")
            qcol = Q_RANK // MEM_W
            sv.update(qn=qn, qp=qp, qr=qr, qa=qa, o_lat=o_lat, lse=lse)
        wm.update(fetch(('rest', l), z))
        memn = _rmsnorm(mem, rep['g_mem'][l], name=f"memnorm{l}")
        kvm = _mm(memn, wm['w_mem_kv'], dims='nn', out_dtype=BF16, name=f"memkv{l}", tm=tm, tn=1024)
        mix = _memattn_fwd(z, kvm, main, qcol=qcol, name=f"memattn{l}")
        x_mid = _mm(mix, wm['w_out'], dims='nn', res=x, out_dtype=F32, name=f"out{l}", tm=tm, tn=1024)
        wf = fetch(('up', l), x_mid)
        h2 = _rmsnorm(x_mid, rep['g_ffn'][l], name=f"ffnnorm{l}")
        a = _mm(h2, wf['w_up'], dims='nn', b_blocked=True, out_dtype=BF16, out_block=bw,
                name=f"up{l}", tm=tm, tn=bw)
        act = _conv_fwd(a, shared['conv_w'], conv_b, l, name=f"conv{l}")
        wf.update(fetch(('down', l), act))
        x = _mm(act, wf['w_down'], dims='nn', a_blocked=True, tk=bw, res=x_mid, out_dtype=F32,
                name=f"down{l}", tm=tm, tn=1024)
        sv.update(h=h, memn=memn, kvm=kvm, z=z, qcol=qcol, mix=mix, x_mid=x_mid, h2=h2, a=a, act=act, wf=wf)
        saved.append(sv)

    sq, dx, dg_final = _final_loss(x, target, rep['g_final'], name="loss")

    g = {k: [None] * DEPTH for k in ('g_mix', 'g_ffn', 'g_mem', 'conv_w', 'conv_b')}
    for k in ('g_v', 'w_sp', 'b_sp'):
        g[k] = [None] * N_A
    g['g_q_lat'] = [None] * n_b
    g['g_final'] = dg_final
    dckv_sum, dkr_sum = None, None

    for l in reversed(range(DEPTH)):
        sv = saved[l]
        wm, wf = sv['wm'], sv['wf']
        dact = _mm(dx, wf['w_down'], dims='nt', out_dtype=BF16, out_block=bw, name=f"d_act{l}", tm=tm, tn=bw)
        dw_down = _mm(sv['act'], dx, dims='tn', a_blocked=True, out_dtype=BF16, name=f"dw_down{l}", tm=bw, tn=256)
        tok = emit(('down', l), {'w_ffn_down': dw_down})
        dc, dcw, dcb = _conv_bwd_dc(sv['a'], dact, shared['conv_w'], conv_b, l, after=tok, name=f"d_conv{l}")
        g['conv_w'][l], g['conv_b'][l] = dcw, dcb
        da = _conv_bwd_da(dc, shared['conv_w'], l, name=f"d_convin{l}")
        dh2 = _mm(da, wf['w_up'], dims='nt', a_blocked=True, b_blocked=True, tk=bw,
                  out_dtype=BF16, name=f"d_h2{l}", tm=tm, tn=1024)
        dw_up = _mm(sv['h2'], da, dims='tn', b_blocked=True, out_dtype=BF16, out_block=bw,
                    name=f"dw_up{l}", tm=512, tn=bw, n_outer=True)
        tok = emit(('up', l), {'w_ffn_up': dw_up})
        dx_mid, g['g_ffn'][l] = _rmsnorm_bwd(sv['x_mid'], rep['g_ffn'][l], dh2, dres=dx, after=tok, name=f"d_ffnnorm{l}")
        dmix = _mm(dx_mid, wm['w_out'], dims='nt', out_dtype=BF16, name=f"d_mix{l}", tm=tm, tn=1024)
        gm = {'w_out': _mm(sv['mix'], dx_mid, dims='tn', out_dtype=BF16, name=f"dw_out{l}", tm=1024, tn=256)}
        dqm, dkvm = _memattn_bwd(sv['z'], sv['kvm'], dmix, qcol=sv['qcol'], name=f"d_memattn{l}")
        gm['w_mem_kv'] = _mm(sv['memn'], dkvm, dims='tn', out_dtype=BF16, name=f"dw_memkv{l}", tm=1024, tn=1024)
        dmemn = _mm(dkvm, wm['w_mem_kv'], dims='nt', out_dtype=F32, name=f"d_memn{l}", tm=tm, tn=1024)
        _, g['g_mem'][l] = _rmsnorm_bwd(mem, rep['g_mem'][l], dmemn, out_dtype=BF16, name=f"d_memnorm{l}")
        if l < N_A:
            dz, dwsp, dbsp_t, dgv = _sgu_bwd(sv['z'], dmix, dqm, shared['g_v'][l], rep['w_sp'][l], b_sp_t[l],
                                             name=f"d_sgu{l}")
            g['w_sp'][l], g['b_sp'][l], g['g_v'][l] = dwsp, dbsp_t.T, dgv
            dh = _mm(dz, wm['w_in'], dims='nt', out_dtype=BF16, name=f"d_h_a{l}", tm=tm, tn=1024)
            gm['w_in_a'] = _mm(sv['h'], dz, dims='tn', out_dtype=BF16, name=f"dw_in_a{l}", tm=1024, tn=512)
        else:
            j = l - N_A
            do_lat = _mm_heads(dmix, wm['w_uv'], mode='to_lat', name=f"d_olat{j}")
            gm['w_uv'] = _mm_heads(sv['o_lat'], dmix, mode='wgrad', name=f"dw_uv{j}")
            dqa, dqr, dckv, dkr = _mla_bwd(sv['qa'], sv['qr'], kv['ckv'], kv['kr'], sv['o_lat'], do_lat, sv['lse'],
                                           name=f"d_mla{j}")
            dckv_sum = dckv if dckv_sum is None else dckv_sum + dckv
            dkr_sum = dkr if dkr_sum is None else dkr_sum + dkr
            dq_nope = _mm_heads(dqa, wm['w_uk'], mode='from_lat', name=f"d_qnope{j}")
            gm['w_uk'] = _mm_heads(dqa, sv['qp'], mode='wgrad', name=f"dw_uk{j}")
            dr1, dr2 = _heads_from_major(dqr)
            dq1, dq2 = _rope(dr1, dr2, cos12, sin12, inverse=True, name=f"d_qrope{j}")
            dqp = jnp.concatenate([dq_nope, dq1, dq2], axis=-1)
            dqn = _mm(dqp, wm['w_uqp'], dims='nt', out_dtype=BF16, name=f"d_qn{j}", tm=tm, tn=512)
            gm['w_uqp'] = _mm(sv['qn'], dqp, dims='tn', out_dtype=BF16, name=f"dw_uq{j}", tm=512, tn=768)
            dqlat, g['g_q_lat'][j] = _rmsnorm_bwd(sv['z'], rep['g_q_lat'][j], dqn, width=Q_RANK, out_dtype=BF16,
                                                 name=f"d_qnorm{j}")
            dz = jnp.concatenate([dqlat, dqm], axis=-1)
            dh = _mm(dz, wm['w_in'], dims='nt', out_dtype=BF16, name=f"d_h_b{j}", tm=tm, tn=1024)
            gm['w_in_b'] = _mm(sv['h'], dz, dims='tn', out_dtype=BF16, name=f"dw_in_b{j}", tm=1024, tn=512)
        tok = emit(('mix', l), gm)
        dx, g['g_mix'][l] = _rmsnorm_bwd(sv['x_in'], rep['g_mix'][l], dh, dres=dx_mid, after=tok, name=f"d_mixnorm{l}")
        if l == N_A:
            dkvx_c, g['g_kv_lat'] = _rmsnorm_bwd(kv['kvx'], rep['g_kv_lat'], dckv_sum, width=KV_RANK, out_dtype=BF16,
                                                 name="d_ckvnorm")
            dk1, dk2 = _rope(dkr_sum[:, :ROPE_HALF], dkr_sum[:, ROPE_HALF:], cos, sin, inverse=True, name="d_krope")
            dkvx = jnp.concatenate([dkvx_c, dk1, dk2, jnp.zeros((s, KV_PAD - KV_RANK - ROPE_DIM), BF16)], axis=-1)
            dxn = _mm(dkvx, kv['w_kv_a'], dims='nt', out_dtype=BF16, name="d_kvnorm_in", tm=tm, tn=1024)
            dw_kv = _mm(kv['xn'], dkvx, dims='tn', out_dtype=BF16, name="dw_kv", tm=1024, tn=KV_PAD)
            tok = emit(('kv', 0), {'w_kv_a': dw_kv})
            dx, g['g_kv'] = _rmsnorm_bwd(kv['x'], rep['g_kv'], dxn, dres=dx, after=tok, name="d_kvnorm")
    return jnp.sum(sq), dx, g


MESH_IDS = pl.DeviceIdType.MESH
PEER_MASKS = tuple((k >> 2 & 1, k >> 1 & 1, k & 1) for k in range(1, N_DEV))
N_PEER = N_DEV - 1
SEMS_PER_BUFFER = 2 * N_PEER + 1
DATAFLOW = pltpu.SideEffectType.DATAFLOW_SIDE_EFFECTING
HBM_SPEC = pl.BlockSpec(memory_space=pltpu.HBM)
SEM_SPEC = pl.BlockSpec(memory_space=pltpu.SEMAPHORE)


def _my_position():
    return lax.axis_index("x"), lax.axis_index("y"), lax.axis_index("c")


def _flip(pos, mask):
    return tuple(1 - p if f else p for p, f in zip(pos, mask))


def _linear_id(pos):
    return 4 * pos[0] + 2 * pos[1] + pos[2]


def _hbm(x):
    return pltpu.with_memory_space_constraint(x, pltpu.HBM)


def _buffer_copies(src_ref, lead, land_ref, sems, scatter):
    me = _my_position()
    my_id = _linear_id(me)
    src = src_ref.at[lead] if lead else src_ref
    own = pltpu.make_async_copy(src.at[my_id] if scatter else src, land_ref.at[my_id], sems.at[2 * N_PEER])
    pairs = []
    for k, mask in enumerate(PEER_MASKS):
        peer = _flip(me, mask)
        peer_id = _linear_id(peer)
        block = src.at[peer_id] if scatter else src
        send = pltpu.make_async_remote_copy(src_ref=block, dst_ref=land_ref.at[my_id], send_sem=sems.at[k],
                                            recv_sem=sems.at[N_PEER + k], device_id=peer, device_id_type=MESH_IDS)
        arrival = pltpu.make_async_remote_copy(src_ref=block, dst_ref=land_ref.at[peer_id], send_sem=sems.at[k],
                                               recv_sem=sems.at[N_PEER + k], device_id=peer, device_id_type=MESH_IDS)
        pairs.append((send, arrival))
    return own, pairs


def _exchange_start(srcs, buffers, *, name, scatter):
    ns, nb = len(srcs), len(buffers)
    lands = [_hbm(lax.empty((N_DEV,) + tuple(shape), dtype)) for _, _, shape, dtype in buffers]

    def body(*refs):
        src_refs, land_refs = refs[:ns], refs[ns:ns + nb]
        sem_refs = refs[ns + nb:ns + 2 * nb]
        token = refs[-1]
        for b, (si, lead, _, _) in enumerate(buffers):
            own, pairs = _buffer_copies(src_refs[si], lead, land_refs[b], sem_refs[b], scatter)
            own.start()
            for send, _ in pairs:
                send.start()
        token[...] = jnp.zeros_like(token)

    out_shape = ([pltpu.SemaphoreType.DMA((SEMS_PER_BUFFER,))] * nb
                 + [pltpu.HBM(a.shape, a.dtype) for a in srcs]
                 + [pltpu.HBM(a.shape, a.dtype) for a in lands]
                 + [jax.ShapeDtypeStruct((8, 128), F32)])
    aliases = {i: nb + i for i in range(ns + nb)}
    outs = pl.pallas_call(
        body, name=name, in_specs=[HBM_SPEC] * (ns + nb),
        out_specs=[SEM_SPEC] * nb + [HBM_SPEC] * (ns + nb) + [pl.BlockSpec(memory_space=pltpu.VMEM)],
        out_shape=out_shape, input_output_aliases=aliases,
        compiler_params=pltpu.CompilerParams(has_side_effects=DATAFLOW),
    )(*[_hbm(a) for a in srcs], *lands)
    sems = list(outs[:nb])
    src_thru = list(outs[nb:nb + ns])
    land_thru = list(outs[nb + ns:nb + ns + nb])
    return sems, land_thru, src_thru, outs[-1]


def _exchange_wait(srcs_thru, buffers, sems, lands, after, *, name, scatter):
    ns, nb = len(srcs_thru), len(buffers)
    has_after = after is not None

    def body(*refs):
        src_refs, land_refs = refs[:ns], refs[ns:ns + nb]
        sem_refs = refs[ns + nb:ns + 2 * nb]
        for b, (si, lead, _, _) in enumerate(buffers):
            own, pairs = _buffer_copies(src_refs[si], lead, land_refs[b], sem_refs[b], scatter)
            for send, arrival in pairs:
                send.wait_send()
                arrival.wait_recv()
            own.wait()

    operands = list(srcs_thru) + list(lands) + list(sems) + ([after] if has_after else [])
    in_specs = ([HBM_SPEC] * (ns + nb) + [SEM_SPEC] * nb + ([pl.BlockSpec(memory_space=pl.ANY)] if has_after else []))
    outs = pl.pallas_call(
        body, name=name, in_specs=in_specs, out_specs=[HBM_SPEC] * nb,
        out_shape=[pltpu.HBM(a.shape, a.dtype) for a in lands],
        input_output_aliases={ns + b: b for b in range(nb)},
        compiler_params=pltpu.CompilerParams(has_side_effects=DATAFLOW),
    )(*operands)
    return list(outs)


def _exchange(arrays, *, name, scatter):
    n = len(arrays)
    out_shapes = [jax.ShapeDtypeStruct(a.shape if scatter else (N_DEV,) + a.shape, a.dtype) for a in arrays]

    def body(*refs):
        srcs, outs, sems = refs[:n], refs[n:2 * n], refs[2 * n:]
        started = []
        for a in range(n):
            own, pairs = _buffer_copies(srcs[a], (), outs[a], sems[a], scatter)
            own.start()
            for send, _ in pairs:
                send.start()
            started.append((own, pairs))
        for own, pairs in started:
            for send, arrival in pairs:
                arrival.wait_recv()
                send.wait_send()
            own.wait()

    any_spec = pl.BlockSpec(memory_space=pl.ANY)
    outs = pl.pallas_call(
        body, name=name, in_specs=[any_spec] * n, out_specs=[any_spec] * n, out_shape=out_shapes,
        scratch_shapes=[pltpu.SemaphoreType.DMA((SEMS_PER_BUFFER,))] * n,
    )(*arrays)
    return list(outs)


def _sum_slots(parts_ref):
    total = parts_ref[0].astype(F32)
    for q in range(1, parts_ref.shape[0]):
        total = total + parts_ref[q].astype(F32)
    return total


def _row_tile(rows, cols, n_arrays):
    budget = (12 * 1024 * 1024) // (4 * n_arrays * max(cols, 128))
    t = rows
    while t > budget and t % 2 == 0 and (t // 2) % 16 == 0:
        t //= 2
    return t


def _sum_adam(parts, w, m, v, layer, outs, *, name):
    q, r, c = parts.shape
    nl = w.shape[0]
    tr = _row_tile(r, c, q + 7)
    c1 = 1.0 - ADAM_B1 ** ADAM_STEP
    c2 = 1.0 - ADAM_B2 ** ADAM_STEP
    if outs is None:
        outs = [lax.empty((nl, r, c), F32) for _ in range(4)]

    def body(p_ref, w_ref, m_ref, v_ref, g_in, d_in, mo_in, vo_in, g_ref, d_ref, mo_ref, vo_ref):
        grad = _sum_slots(p_ref)
        m_new = ADAM_B1 * m_ref[...] + (1.0 - ADAM_B1) * grad
        v_new = ADAM_B2 * v_ref[...] + (1.0 - ADAM_B2) * (grad * grad)
        m_hat = m_new / c1
        v_hat = v_new / c2
        g_ref[...] = grad
        d_ref[...] = -ADAM_LR * (m_hat / (jnp.sqrt(v_hat) + ADAM_EPS) + ADAM_WD * w_ref[...])
        mo_ref[...] = m_new
        vo_ref[...] = v_new

    tile = pl.BlockSpec((None, tr, c), lambda i: (layer, i, 0))
    any_spec = pl.BlockSpec(memory_space=pl.ANY)
    return pl.pallas_call(
        body, name=name, grid=(r // tr,),
        in_specs=[pl.BlockSpec((q, tr, c), lambda i: (0, i, 0)), tile, tile, tile] + [any_spec] * 4,
        out_specs=[tile] * 4, out_shape=[jax.ShapeDtypeStruct((nl, r, c), F32)] * 4,
        input_output_aliases={4: 0, 5: 1, 6: 2, 7: 3},
        compiler_params=_params(("parallel",)),
    )(parts, w, m, v, *outs)


def _sum_parts(parts, *, name):
    q, r, c = parts.shape

    def body(p_ref, o_ref):
        o_ref[...] = _sum_slots(p_ref)

    return pl.pallas_call(
        body, name=name, in_specs=[pl.BlockSpec((q, r, c), lambda: (0, 0, 0))],
        out_specs=pl.BlockSpec((r, c), lambda: (0, 0)), out_shape=jax.ShapeDtypeStruct((r, c), F32),
        compiler_params=_params(),
    )(parts)


INPUT_NAMES = (['x', 'mem', 'positions'] + WEIGHTS + ['loss_target'] + ['m_' + n for n in WEIGHTS]
               + ['v_' + n for n in WEIGHTS])
SMALL_ALIGN = N_DEV * 8 * 128
GROUP_ORDER = ('in', 'rest', 'up', 'down')
GROUP_WEIGHTS = {'in': (['w_in_a'], ['w_in_b', 'w_uq', 'w_uk', 'w_uv']), 'rest': (['w_mem_kv', 'w_out'],) * 2,
                 'up': (['w_ffn_up'],) * 2, 'down': (['w_ffn_down'],) * 2}
LAYERED = {'w_in_a': 0, 'w_in_b': N_A, 'w_uq': N_A, 'w_uk': N_A, 'w_uv': N_A, 'w_mem_kv': 0, 'w_out': 0,
           'w_ffn_up': 0, 'w_ffn_down': 0}


def _permute_uq(w_uq):
    r = w_uq.shape[0]
    q = w_uq.reshape(r, G_HEADS, HEAD + ROPE_DIM)
    return jnp.concatenate([q[..., :HEAD].reshape(r, -1), q[..., HEAD:HEAD + ROPE_HALF].reshape(r, -1),
                            q[..., HEAD + ROPE_HALF:].reshape(r, -1)], axis=-1)


def _unpermute_uq(w_uqp):
    r = w_uqp.shape[0]
    nope = w_uqp[..., :G_W].reshape(r, G_HEADS, HEAD)
    r1 = w_uqp[..., G_W:G_W + G_HEADS * ROPE_HALF].reshape(r, G_HEADS, ROPE_HALF)
    r2 = w_uqp[..., G_W + G_HEADS * ROPE_HALF:].reshape(r, G_HEADS, ROPE_HALF)
    return jnp.concatenate([nope, r1, r2], axis=-1).reshape(r, -1)


def _cols_from_stack(st):
    _, r, n = st.shape
    return st.transpose(1, 0, 2).reshape(r, N_DEV * n)


def _cols_to_stack(wh):
    r, c = wh.shape
    return wh.reshape(r, N_DEV, c // N_DEV).transpose(1, 0, 2)


def _group_weights(group):
    kind, l = group
    return GROUP_WEIGHTS[kind][0 if l < N_A else 1]


def _step(args):
    p = dict(zip(INPUT_NAMES, args))
    x, mem, positions, target = p['x'][0], p['mem'][0], p['positions'][0], p['loss_target'][0]
    d = x.shape[-1]
    my_id = _linear_id(_my_position())

    w_kv_pad = jnp.pad(p['w_kv_a'], ((0, 0), (0, KV_PAD - p['w_kv_a'].shape[1])))
    shard = {k: p[k].astype(BF16) for k in LAYERED}
    shard['w_uk'] = shard['w_uk'].reshape(shard['w_uk'].shape[0], shard['w_uk'].shape[1], -1)
    shard['w_uv'] = shard['w_uv'].reshape(shard['w_uv'].shape[0], shard['w_uv'].shape[1], -1)
    shard.update(conv_w=p['conv_w'], g_v=p['g_v'], w_kv_a=w_kv_pad.astype(BF16))
    src_names = list(shard)
    gather_groups = []
    for l in range(DEPTH):
        gather_groups += [(kind, l) for kind in GROUP_ORDER]
    buffers, owner = [], []
    for group in gather_groups:
        kind, l = group
        for k in _group_weights(group):
            buffers.append((src_names.index(k), (l - LAYERED[k],), shard[k].shape[1:], shard[k].dtype))
            owner.append((group, k))
        if group == ('in', 0):
            for k in ('g_v', 'conv_w'):
                buffers.append((src_names.index(k), (), shard[k].shape, shard[k].dtype))
                owner.append((group, k))
        if group == ('in', N_A):
            buffers.append((src_names.index('w_kv_a'), (), shard['w_kv_a'].shape, BF16))
            owner.append((group, 'w_kv_a'))
    g_sems, g_lands, g_srcs, _ = _exchange_start([shard[k] for k in src_names], buffers, name="gather_start",
                                                 scatter=False)

    def fetch(group, after):
        idx = [i for i, (grp, _) in enumerate(owner) if grp == group]
        landed = _exchange_wait(g_srcs, [buffers[i] for i in idx], [g_sems[i] for i in idx],
                                [g_lands[i] for i in idx], after, name=f"gather_wait_{group[0]}{group[1]}",
                                scatter=False)
        got = {owner[i][1]: t for i, t in zip(idx, landed)}
        out = {}
        for k, t in got.items():
            if k in ('w_in_a', 'w_uq'):
                out[k] = _cols_from_stack(t)
            elif k == 'g_v':
                out[k] = t.transpose(1, 0, 2).reshape(t.shape[1], -1)
            elif k in ('w_ffn_up', 'conv_w'):
                out[k] = t
            else:
                out[k] = t.reshape(-1, t.shape[-1])
        if 'w_uq' in out:
            out['w_uqp'] = _permute_uq(out.pop('w_uq'))
        for old, new in (('w_in_a', 'w_in'), ('w_in_b', 'w_in'), ('w_ffn_up', 'w_up'), ('w_ffn_down', 'w_down')):
            if old in out:
                out[new] = out.pop(old)
        return out

    pending = []

    def emit(group, grads):
        send = {}
        for k, t in grads.items():
            if k == 'w_in_a':
                send[k] = _cols_to_stack(t)
            elif k == 'w_uqp':
                send['w_uq'] = _cols_to_stack(_unpermute_uq(t))
            elif k == 'w_ffn_up':
                send[k] = t
            elif k == 'w_kv_a':
                cols = p['w_kv_a'].shape[1]
                send[k] = t[:, :cols].reshape(N_DEV, -1, cols)
            else:
                send[k] = t.reshape(N_DEV, t.shape[0] // N_DEV, t.shape[1])
        keys = list(send)
        bufs = [(i, (), send[k].shape[1:], send[k].dtype) for i, k in enumerate(keys)]
        sems, lands, srcs, token = _exchange_start([send[k] for k in keys], bufs,
                                                   name=f"scatter_start_{group[0]}{group[1]}", scatter=True)
        pending.append((group, keys, bufs, sems, lands, srcs))
        return token

    rep = {k: p[k] for k in REPLICATED}
    sq, grad_x, g = _local_step(x, mem, positions, target, rep, fetch, emit)
    loss = (0.5 / d) * lax.psum(sq, ("x", "y", "c"))

    small = {
        'g_mix': jnp.concatenate(g['g_mix']), 'g_ffn': jnp.concatenate(g['g_ffn']), 'g_final': g['g_final'],
        'w_sp': jnp.stack(g['w_sp']), 'b_sp': jnp.stack(g['b_sp']), 'g_kv': g['g_kv'], 'g_kv_lat': g['g_kv_lat'],
        'g_q_lat': jnp.concatenate(g['g_q_lat']), 'g_mem': jnp.concatenate(g['g_mem']),
        'conv_b': jnp.stack(g['conv_b']),
        'g_v': jnp.concatenate(g['g_v']),
        'conv_w': jnp.stack(g['conv_w']).transpose(0, 2, 1, 3),
    }
    small_names = REPLICATED + SMALL_SHARDED
    flat = jnp.concatenate([small[k].reshape(-1).astype(F32) for k in small_names])
    n_small = flat.shape[0]
    padded = -(-n_small // SMALL_ALIGN) * SMALL_ALIGN
    flat = jnp.pad(flat, (0, padded - n_small)).reshape(N_DEV, -1, 128)
    (small_parts,) = _exchange([flat], name="scatter_small", scatter=True)
    reduced = _sum_parts(small_parts, name="sum_small")
    (small_all,) = _exchange([reduced], name="gather_small", scatter=False)
    small_all = small_all.reshape(-1)
    grads_small, off = {}, 0
    for k in small_names:
        size = small[k].size
        grads_small[k] = small_all[off:off + size].reshape(small[k].shape)
        off += size
    grads_small['g_v'] = lax.dynamic_slice_in_dim(grads_small['g_v'], my_id * p['g_v'].shape[1], p['g_v'].shape[1], axis=1)
    grads_small['conv_w'] = lax.dynamic_index_in_dim(grads_small['conv_w'], my_id, axis=2, keepdims=False)
    gs = jnp.concatenate([grads_small[k].reshape(-1) for k in small_names])
    n_loc = gs.shape[0]
    pad_loc = -(-n_loc // 1024) * 1024 - n_loc

    def pack(prefix):
        t = jnp.concatenate([p[prefix + k].reshape(-1) for k in small_names])
        return jnp.pad(t, (0, pad_loc)).reshape(1, -1, 128)

    res = _sum_adam(jnp.pad(gs, (0, pad_loc)).reshape(1, -1, 128), pack(''), pack('m_'), pack('v_'), 0, None,
                    name="adam_small")
    out, off = {}, 0
    for k in small_names:
        size = p[k].size
        out[k] = [t.reshape(-1)[off:off + size].reshape(p[k].shape) for t in res]
        off += size

    running = {}
    for group, keys, bufs, sems, lands, srcs in pending:
        landed = _exchange_wait(srcs, bufs, sems, lands, None, name=f"scatter_wait_{group[0]}{group[1]}", scatter=True)
        for k, parts in zip(keys, landed):
            stacked = k in LAYERED
            nl = p[k].shape[0] if stacked else 1
            layer = group[1] - LAYERED[k] if stacked else 0
            rows = p[k].size // nl // p[k].shape[-1]
            view = (nl, rows, p[k].shape[-1])
            running[k] = _sum_adam(parts.reshape(N_DEV, rows, view[2]), p[k].reshape(view), p['m_' + k].reshape(view),
                                   p['v_' + k].reshape(view), layer, running.get(k), name=f"adam_{k}{layer}")
    for k, res in running.items():
        out[k] = [t.reshape(p[k].shape) for t in res]

    outs = [loss, grad_x[None]]
    for i in range(4):
        outs += [out[k][i] for k in WEIGHTS]
    return tuple(outs)


def kernel(x, mem, positions, g_mix, g_ffn, g_final, w_in_a, g_v, w_sp, b_sp, g_kv, w_kv_a, g_kv_lat, w_in_b, g_q_lat, w_uq, w_uk, w_uv, g_mem, w_mem_kv, w_out, w_ffn_up, conv_w, conv_b, w_ffn_down, loss_target, m_g_mix, m_g_ffn, m_g_final, m_w_in_a, m_g_v, m_w_sp, m_b_sp, m_g_kv, m_w_kv_a, m_g_kv_lat, m_w_in_b, m_g_q_lat, m_w_uq, m_w_uk, m_w_uv, m_g_mem, m_w_mem_kv, m_w_out, m_w_ffn_up, m_conv_w, m_conv_b, m_w_ffn_down, v_g_mix, v_g_ffn, v_g_final, v_w_in_a, v_g_v, v_w_sp, v_b_sp, v_g_kv, v_w_kv_a, v_g_kv_lat, v_w_in_b, v_g_q_lat, v_w_uq, v_w_uk, v_w_uv, v_g_mem, v_w_mem_kv, v_w_out, v_w_ffn_up, v_conv_w, v_conv_b, v_w_ffn_down):
    return _step((x, mem, positions, g_mix, g_ffn, g_final, w_in_a, g_v, w_sp, b_sp, g_kv, w_kv_a, g_kv_lat, w_in_b, g_q_lat, w_uq, w_uk, w_uv, g_mem, w_mem_kv, w_out, w_ffn_up, conv_w, conv_b, w_ffn_down, loss_target, m_g_mix, m_g_ffn, m_g_final, m_w_in_a, m_g_v, m_w_sp, m_b_sp, m_g_kv, m_w_kv_a, m_g_kv_lat, m_w_in_b, m_g_q_lat, m_w_uq, m_w_uk, m_w_uv, m_g_mem, m_w_mem_kv, m_w_out, m_w_ffn_up, m_conv_w, m_conv_b, m_w_ffn_down, v_g_mix, v_g_ffn, v_g_final, v_w_in_a, v_g_v, v_w_sp, v_b_sp, v_g_kv, v_w_kv_a, v_g_kv_lat, v_w_in_b, v_g_q_lat, v_w_uq, v_w_uk, v_w_uv, v_g_mem, v_w_mem_kv, v_w_out, v_w_ffn_up, v_conv_w, v_conv_b, v_w_ffn_down))
```

```python
import math

import jax
import jax.numpy as jnp
from jax import lax
from jax.experimental import pallas as pl
from jax.experimental.pallas import tpu as pltpu

F32 = jnp.float32
BF16 = jnp.bfloat16

N_DEV = 8
N_A = 2
DEPTH = 4
G_HEADS = 12
HEAD = 128
CHUNK = 128
MEM_HEADS = 4
MEM_W = MEM_HEADS * HEAD
G_W = G_HEADS * HEAD
ROPE_DIM = 64
ROPE_HALF = ROPE_DIM // 2
KV_RANK = 512
Q_RANK = 512
KV_PAD = 640
ROPE_THETA = 10000.0
EPS = 1e-6
CONV_W = 3

ADAM_LR = 0.001
ADAM_B1 = 0.9
ADAM_B2 = 0.999
ADAM_EPS = 1e-08
ADAM_WD = 0.01
ADAM_STEP = 10

VMEM_LIMIT_V7X = 56 * 1024 * 1024
MASK_VALUE = -1e30

WEIGHTS = ['g_mix', 'g_ffn', 'g_final', 'w_in_a', 'g_v', 'w_sp', 'b_sp', 'g_kv', 'w_kv_a', 'g_kv_lat',
           'w_in_b', 'g_q_lat', 'w_uq', 'w_uk', 'w_uv', 'g_mem', 'w_mem_kv', 'w_out', 'w_ffn_up',
           'conv_w', 'conv_b', 'w_ffn_down']
REPLICATED = ['g_mix', 'g_ffn', 'g_final', 'w_sp', 'b_sp', 'g_kv', 'g_kv_lat', 'g_q_lat', 'g_mem', 'conv_b']
SMALL_SHARDED = ['g_v', 'conv_w']


def _params(sem=None):
    return pltpu.CompilerParams(dimension_semantics=sem, vmem_limit_bytes=VMEM_LIMIT_V7X)


def _dot(a, b, dims):
    contract = {'nn': ((1,), (0,)), 'nt': ((1,), (1,)), 'tn': ((0,), (0,))}[dims]
    return lax.dot_general(a, b, (contract, ((), ())), preferred_element_type=F32)


def _erf(x):
    return lax.erf(x)


def _gelu(x):
    return 0.5 * x * (1.0 + _erf(x * (2.0 ** -0.5)))


def _gelu_grad(x):
    cdf = 0.5 * (1.0 + _erf(x * (2.0 ** -0.5)))
    pdf = jnp.exp(-0.5 * x * x) * (1.0 / math.sqrt(2.0 * math.pi))
    return cdf + x * pdf


def _sigmoid(x):
    return 1.0 / (1.0 + jnp.exp(-x))


def _operand_spec(shape, lead, blocked, tr, tc, ridx, cidx):
    if blocked:
        per = shape[-1] // tc
        assert shape[-1] % tc == 0, (shape, tc)
        return pl.BlockSpec(
            (None,) * (1 + len(lead)) + (tr, tc),
            lambda *g: (cidx(*g) // per,) + lead + (ridx(*g), cidx(*g) % per))
    return pl.BlockSpec((None,) * len(lead) + (tr, tc), lambda *g: lead + (ridx(*g), cidx(*g)))


def _view2d(x, blocked):
    return (x.shape[-2], x.shape[0] * x.shape[-1]) if blocked else (x.shape[-2], x.shape[-1])


def _mm(a, b, *, dims, out_dtype, name, tm, tn, tk=None, res=None, a_lead=(), b_lead=(),
        a_blocked=False, b_blocked=False, out_block=None, n_outer=False):
    ar, ac = _view2d(a, a_blocked)
    br, bc = _view2d(b, b_blocked)
    m, k = (ac, ar) if dims == 'tn' else (ar, ac)
    n, k2 = (br, bc) if dims == 'nt' else (bc, br)
    assert k == k2, (a.shape, b.shape, dims)
    tm, tn = min(tm, m), min(tn, n)
    tk = k if tk is None else tk
    assert m % tm == 0 and n % tn == 0 and k % tk == 0, (name, m, n, k, tm, tn, tk)
    nk = k // tk
    if n_outer:
        gi, gj = (lambda g0, g1, g2: g1), (lambda g0, g1, g2: g0)
        grid = (n // tn, m // tm, nk)
    else:
        gi, gj = (lambda g0, g1, g2: g0), (lambda g0, g1, g2: g1)
        grid = (m // tm, n // tn, nk)
    gk = lambda g0, g1, g2: g2

    if dims == 'tn':
        a_spec = _operand_spec(a.shape, a_lead, a_blocked, tk, tm, gk, gi)
    else:
        a_spec = _operand_spec(a.shape, a_lead, a_blocked, tm, tk, gi, gk)
    if dims == 'nt':
        b_spec = _operand_spec(b.shape, b_lead, b_blocked, tn, tk, gj, gk)
    else:
        b_spec = _operand_spec(b.shape, b_lead, b_blocked, tk, tn, gk, gj)
    in_specs = [a_spec, b_spec]
    operands = [a, b]
    if res is not None:
        in_specs.append(pl.BlockSpec((tm, tn), lambda *g: (gi(*g), gj(*g))))
        operands.append(res)
    if out_block is not None:
        out_shape = jax.ShapeDtypeStruct((n // out_block, m, out_block), out_dtype)
        out_spec = _operand_spec(out_shape.shape, (), True, tm, tn, gi, gj)
    else:
        out_shape = jax.ShapeDtypeStruct((m, n), out_dtype)
        out_spec = pl.BlockSpec((tm, tn), lambda *g: (gi(*g), gj(*g)))

    def body(*refs):
        a_ref, b_ref = refs[0], refs[1]
        r_ref = refs[2] if res is not None else None
        o_ref = refs[3] if res is not None else refs[2]
        acc_ref = refs[-1] if nk > 1 else None
        part = _dot(a_ref[...].astype(BF16), b_ref[...].astype(BF16), dims)

        def finish(total):
            if r_ref is not None:
                total = total + r_ref[...]
            o_ref[...] = total.astype(o_ref.dtype)

        if nk == 1:
            finish(part)
        else:
            kk = pl.program_id(2)

            @pl.when(kk == 0)
            def _():
                acc_ref[...] = part

            @pl.when(kk > 0)
            def _():
                acc_ref[...] += part

            @pl.when(kk == nk - 1)
            def _():
                finish(acc_ref[...])

    scratch = [pltpu.VMEM((tm, tn), F32)] if nk > 1 else []
    return pl.pallas_call(
        body, name=name, grid=grid, in_specs=in_specs, out_specs=out_spec,
        out_shape=out_shape, scratch_shapes=scratch,
        compiler_params=_params(("parallel", "parallel", "arbitrary")),
    )(*operands)


def _mm_heads(a, b, *, mode, name, out_dtype=BF16, tm=512):
    if mode in ('to_lat', 'from_lat'):
        r = b.shape[0]
        s = a.shape[0] if mode == 'to_lat' else a.shape[1]
        tm = min(tm, s)
        flat = pl.BlockSpec((tm, G_W), lambda i: (i, 0))
        per_head = pl.BlockSpec((G_HEADS, tm, r), lambda i: (0, i, 0))

        def all_heads(a_ref, b_ref, o_ref):
            for h in range(G_HEADS):
                cols = slice(h * HEAD, (h + 1) * HEAD)
                if mode == 'to_lat':
                    o_ref[h] = _dot(a_ref[:, cols].astype(BF16), b_ref[:, cols].astype(BF16), 'nt').astype(o_ref.dtype)
                else:
                    o_ref[:, cols] = _dot(a_ref[h].astype(BF16), b_ref[:, cols].astype(BF16), 'nn').astype(o_ref.dtype)

        return pl.pallas_call(
            all_heads, name=name, grid=(s // tm,),
            in_specs=[flat if mode == 'to_lat' else per_head, pl.BlockSpec((r, G_W), lambda i: (0, 0))],
            out_specs=per_head if mode == 'to_lat' else flat,
            out_shape=jax.ShapeDtypeStruct((G_HEADS, s, r) if mode == 'to_lat' else (s, G_W), out_dtype),
            compiler_params=_params(("parallel",)),
        )(a, b)
    else:
        _, s, r = a.shape
        grid = (G_HEADS, 1)
        in_specs = [pl.BlockSpec((None, s, r), lambda h, i: (h, 0, 0)),
                    pl.BlockSpec((s, HEAD), lambda h, i: (0, h))]
        out_spec = pl.BlockSpec((r, HEAD), lambda h, i: (0, h))
        out_shape = jax.ShapeDtypeStruct((r, G_W), out_dtype)
        dims = 'tn'

    def body(a_ref, b_ref, o_ref):
        o_ref[...] = _dot(a_ref[...].astype(BF16), b_ref[...].astype(BF16), dims).astype(o_ref.dtype)

    return pl.pallas_call(
        body, name=name, grid=grid, in_specs=in_specs, out_specs=out_spec, out_shape=out_shape,
        compiler_params=_params(("parallel", "parallel")),
    )(a, b)


def _rmsnorm(x, g, *, name, width=None, out_dtype=BF16, tm=256):
    s = x.shape[0]
    w = x.shape[1] if width is None else width
    tm = min(tm, s)

    def body(x_ref, g_ref, o_ref):
        xv = x_ref[...].astype(F32)
        rstd = lax.rsqrt(jnp.mean(xv * xv, axis=-1, keepdims=True) + EPS)
        o_ref[...] = (xv * rstd * g_ref[...]).astype(o_ref.dtype)

    return pl.pallas_call(
        body, name=name, grid=(s // tm,),
        in_specs=[pl.BlockSpec((tm, w), lambda i: (i, 0)), pl.BlockSpec((1, w), lambda i: (0, 0))],
        out_specs=pl.BlockSpec((tm, w), lambda i: (i, 0)),
        out_shape=jax.ShapeDtypeStruct((s, w), out_dtype),
        compiler_params=_params(("parallel",)),
    )(x, g.reshape(1, w))


def _rmsnorm_bwd(x, g, dy, *, name, width=None, dres=None, after=None, out_dtype=F32, tm=256):
    s = x.shape[0]
    w = x.shape[1] if width is None else width
    tm = min(tm, s)

    def body(*refs):
        x_ref, g_ref, dy_ref = refs[0], refs[1], refs[2]
        r_ref = refs[3] if dres is not None else None
        dx_ref, dg_ref = refs[-2], refs[-1]
        xv = x_ref[...].astype(F32)
        rstd = lax.rsqrt(jnp.mean(xv * xv, axis=-1, keepdims=True) + EPS)
        xhat = xv * rstd
        dyv = dy_ref[...].astype(F32)
        gdy = dyv * g_ref[...]
        dx = rstd * (gdy - xhat * jnp.mean(gdy * xhat, axis=-1, keepdims=True))
        if r_ref is not None:
            dx = dx + r_ref[...]
        dx_ref[...] = dx.astype(dx_ref.dtype)
        part = jnp.sum(dyv * xhat, axis=0, keepdims=True)

        @pl.when(pl.program_id(0) == 0)
        def _():
            dg_ref[...] = part

        @pl.when(pl.program_id(0) > 0)
        def _():
            dg_ref[...] += part

    row = pl.BlockSpec((tm, w), lambda i: (i, 0))
    vec = pl.BlockSpec((1, w), lambda i: (0, 0))
    in_specs = [row, vec, row] + ([row] if dres is not None else [])
    operands = [x, g.reshape(1, w), dy] + ([dres] if dres is not None else [])
    if after is not None:
        in_specs.append(pl.BlockSpec(memory_space=pl.ANY))
        operands.append(after)
    return pl.pallas_call(
        body, name=name, grid=(s // tm,), in_specs=in_specs, out_specs=[row, vec],
        out_shape=[jax.ShapeDtypeStruct((s, w), out_dtype), jax.ShapeDtypeStruct((1, w), F32)],
        compiler_params=_params(("arbitrary",)),
    )(*operands)


def _final_loss(x, target, g, *, name, tm=256):
    s, d = x.shape
    tm = min(tm, s)

    def body(x_ref, t_ref, g_ref, sq_ref, dx_ref, dg_ref):
        xv = x_ref[...]
        rstd = lax.rsqrt(jnp.mean(xv * xv, axis=-1, keepdims=True) + EPS)
        xhat = xv * rstd
        err = xhat * g_ref[...] - t_ref[...]
        dyv = err * (1.0 / d)
        gdy = dyv * g_ref[...]
        dx_ref[...] = rstd * (gdy - xhat * jnp.mean(gdy * xhat, axis=-1, keepdims=True))
        sq = jnp.sum(err * err, axis=0, keepdims=True)
        dg = jnp.sum(dyv * xhat, axis=0, keepdims=True)

        @pl.when(pl.program_id(0) == 0)
        def _():
            sq_ref[...] = sq
            dg_ref[...] = dg

        @pl.when(pl.program_id(0) > 0)
        def _():
            sq_ref[...] += sq
            dg_ref[...] += dg

    row = pl.BlockSpec((tm, d), lambda i: (i, 0))
    vec = pl.BlockSpec((1, d), lambda i: (0, 0))
    return pl.pallas_call(
        body, name=name, grid=(s // tm,), in_specs=[row, row, vec], out_specs=[vec, row, vec],
        out_shape=[jax.ShapeDtypeStruct((1, d), F32), jax.ShapeDtypeStruct((s, d), F32),
                   jax.ShapeDtypeStruct((1, d), F32)],
        compiler_params=_params(("arbitrary",)),
    )(x, target, g.reshape(1, d))


def _tril_mask():
    t = lax.broadcasted_iota(jnp.int32, (CHUNK, CHUNK), 0)
    s = lax.broadcasted_iota(jnp.int32, (CHUNK, CHUNK), 1)
    return t >= s


def _sgu_fwd(z, g_v, w_sp, b_sp_t, *, name):
    s = z.shape[0]

    def body(zu_ref, zv_ref, g_ref, w_ref, b_ref, o_ref):
        u = _gelu(zu_ref[...].astype(F32))
        gv = _gelu(zv_ref[...].astype(F32))
        rstd = lax.rsqrt(jnp.mean(gv * gv, axis=-1, keepdims=True) + EPS)
        v = (gv * rstd * g_ref[...]).astype(BF16)
        mask = _tril_mask()
        for grp in range(G_HEADS):
            cols = slice(grp * HEAD, (grp + 1) * HEAD)
            wm = jnp.where(mask, w_ref[grp], 0.0).astype(BF16)
            sv = _dot(wm, v[:, cols], 'nn') + b_ref[:, grp:grp + 1]
            o_ref[:, cols] = (u[:, cols] * sv).astype(o_ref.dtype)

    return pl.pallas_call(
        body, name=name, grid=(s // CHUNK,),
        in_specs=[pl.BlockSpec((CHUNK, G_W), lambda i: (i, 0)),
                  pl.BlockSpec((CHUNK, G_W), lambda i: (i, 1)),
                  pl.BlockSpec((1, G_W), lambda i: (0, 0)),
                  pl.BlockSpec((G_HEADS, CHUNK, CHUNK), lambda i: (0, 0, 0)),
                  pl.BlockSpec((CHUNK, G_HEADS), lambda i: (0, 0))],
        out_specs=pl.BlockSpec((CHUNK, G_W), lambda i: (i, 0)),
        out_shape=jax.ShapeDtypeStruct((s, G_W), BF16),
        compiler_params=_params(("parallel",)),
    )(z, z, g_v.reshape(1, G_W), w_sp, b_sp_t)


def _sgu_bwd(z, dmix, dqm, g_v, w_sp, b_sp_t, *, name):
    s = z.shape[0]
    zw = z.shape[1]

    def body(zu_ref, zv_ref, dm_ref, dq_ref, g_ref, w_ref, b_ref, dz_ref, dw_ref, db_ref, dg_ref):
        first = pl.program_id(0) == 0

        @pl.when(first)
        def _():
            dw_ref[...] = jnp.zeros_like(dw_ref)
            db_ref[...] = jnp.zeros_like(db_ref)
            dg_ref[...] = jnp.zeros_like(dg_ref)

        zu = zu_ref[...].astype(F32)
        zv = zv_ref[...].astype(F32)
        dmain = dm_ref[...].astype(F32)
        u = _gelu(zu)
        gv = _gelu(zv)
        rstd = lax.rsqrt(jnp.mean(gv * gv, axis=-1, keepdims=True) + EPS)
        vhat = gv * rstd
        gvec = g_ref[...]
        v = (vhat * gvec).astype(BF16)
        dsv = dmain * u
        dsv_b = dsv.astype(BF16)
        mask = _tril_mask()
        dv_parts = []
        for grp in range(G_HEADS):
            cols = slice(grp * HEAD, (grp + 1) * HEAD)
            wm = jnp.where(mask, w_ref[grp], 0.0).astype(BF16)
            sv = _dot(wm, v[:, cols], 'nn') + b_ref[:, grp:grp + 1]
            dz_ref[:, cols] = (dmain[:, cols] * sv * _gelu_grad(zu[:, cols])).astype(dz_ref.dtype)
            dwg = _dot(dsv_b[:, cols], v[:, cols], 'nt')
            dw_ref[grp] += jnp.where(mask, dwg, 0.0)
            db_ref[:, grp:grp + 1] += jnp.sum(dsv[:, cols], axis=-1, keepdims=True)
            dv_parts.append(_dot(wm, dsv_b[:, cols], 'tn'))
        dv = jnp.concatenate(dv_parts, axis=-1)
        dg_ref[...] += jnp.sum(dv * vhat, axis=0, keepdims=True)
        gdv = dv * gvec
        dgv = rstd * (gdv - vhat * jnp.mean(gdv * vhat, axis=-1, keepdims=True))
        dz_ref[:, G_W:2 * G_W] = (dgv * _gelu_grad(zv)).astype(dz_ref.dtype)
        dz_ref[:, 2 * G_W:] = dq_ref[...].astype(dz_ref.dtype)

    return pl.pallas_call(
        body, name=name, grid=(s // CHUNK,),
        in_specs=[pl.BlockSpec((CHUNK, G_W), lambda i: (i, 0)),
                  pl.BlockSpec((CHUNK, G_W), lambda i: (i, 1)),
                  pl.BlockSpec((CHUNK, G_W), lambda i: (i, 0)),
                  pl.BlockSpec((CHUNK, MEM_W), lambda i: (i, 0)),
                  pl.BlockSpec((1, G_W), lambda i: (0, 0)),
                  pl.BlockSpec((G_HEADS, CHUNK, CHUNK), lambda i: (0, 0, 0)),
                  pl.BlockSpec((CHUNK, G_HEADS), lambda i: (0, 0))],
        out_specs=[pl.BlockSpec((CHUNK, zw), lambda i: (i, 0)),
                   pl.BlockSpec((G_HEADS, CHUNK, CHUNK), lambda i: (0, 0, 0)),
                   pl.BlockSpec((CHUNK, G_HEADS), lambda i: (0, 0)),
                   pl.BlockSpec((1, G_W), lambda i: (0, 0))],
        out_shape=[jax.ShapeDtypeStruct((s, zw), BF16),
                   jax.ShapeDtypeStruct((G_HEADS, CHUNK, CHUNK), F32),
                   jax.ShapeDtypeStruct((CHUNK, G_HEADS), F32),
                   jax.ShapeDtypeStruct((1, G_W), F32)],
        compiler_params=_params(("arbitrary",)),
    )(z, z, dmix, dqm, g_v.reshape(1, G_W), w_sp, b_sp_t)


def _mem_probs(q, k):
    sc = _dot(q, k, 'nt') * (HEAD ** -0.5)
    sc = sc - jnp.max(sc, axis=-1, keepdims=True)
    e = jnp.exp(sc)
    return e / jnp.sum(e, axis=-1, keepdims=True)


def _memattn_fwd(z, kvm, main, *, qcol, name, tm=512):
    s = z.shape[0]
    m = kvm.shape[0]
    tm = min(tm, s)

    def body(q_ref, kv_ref, main_ref, o_ref):
        o_ref[:, :G_W] = main_ref[...]
        for h in range(MEM_HEADS):
            cols = slice(h * HEAD, (h + 1) * HEAD)
            k = kv_ref[:, cols]
            v = kv_ref[:, MEM_W + h * HEAD:MEM_W + (h + 1) * HEAD]
            p = _mem_probs(q_ref[:, cols], k)
            o_ref[:, G_W + h * HEAD:G_W + (h + 1) * HEAD] = _dot(p.astype(BF16), v, 'nn').astype(o_ref.dtype)

    return pl.pallas_call(
        body, name=name, grid=(s // tm,),
        in_specs=[pl.BlockSpec((tm, MEM_W), lambda i: (i, qcol)),
                  pl.BlockSpec((m, 2 * MEM_W), lambda i: (0, 0)),
                  pl.BlockSpec((tm, G_W), lambda i: (i, 0))],
        out_specs=pl.BlockSpec((tm, G_W + MEM_W), lambda i: (i, 0)),
        out_shape=jax.ShapeDtypeStruct((s, G_W + MEM_W), BF16),
        compiler_params=_params(("parallel",)),
    )(z, kvm, main)


def _memattn_bwd(z, kvm, dmix, *, qcol, name, tm=512):
    s = z.shape[0]
    m = kvm.shape[0]
    tm = min(tm, s)
    scale = HEAD ** -0.5

    def body(q_ref, kv_ref, do_ref, dq_ref, dkv_ref):
        @pl.when(pl.program_id(0) == 0)
        def _():
            dkv_ref[...] = jnp.zeros_like(dkv_ref)

        for h in range(MEM_HEADS):
            cols = slice(h * HEAD, (h + 1) * HEAD)
            vcols = slice(MEM_W + h * HEAD, MEM_W + (h + 1) * HEAD)
            q = q_ref[:, cols]
            k = kv_ref[:, cols]
            v = kv_ref[:, vcols]
            do = do_ref[:, cols]
            p = _mem_probs(q, k)
            dp = _dot(do, v, 'nt')
            ds = (p * (dp - jnp.sum(dp * p, axis=-1, keepdims=True)) * scale).astype(BF16)
            dq_ref[:, cols] = _dot(ds, k, 'nn').astype(dq_ref.dtype)
            dkv_ref[:, cols] += _dot(ds, q, 'tn')
            dkv_ref[:, vcols] += _dot(p.astype(BF16), do, 'tn')

    mo_block = G_W // MEM_W
    return pl.pallas_call(
        body, name=name, grid=(s // tm,),
        in_specs=[pl.BlockSpec((tm, MEM_W), lambda i: (i, qcol)),
                  pl.BlockSpec((m, 2 * MEM_W), lambda i: (0, 0)),
                  pl.BlockSpec((tm, MEM_W), lambda i: (i, mo_block))],
        out_specs=[pl.BlockSpec((tm, MEM_W), lambda i: (i, 0)),
                   pl.BlockSpec((m, 2 * MEM_W), lambda i: (0, 0))],
        out_shape=[jax.ShapeDtypeStruct((s, MEM_W), BF16), jax.ShapeDtypeStruct((m, 2 * MEM_W), F32)],
        compiler_params=_params(("arbitrary",)),
    )(z, kvm, dmix)


def _rope(x1, x2, cos, sin, *, name, inverse=False, out_dtype=BF16, col1=0, col2=0, tm=512):
    s, w = cos.shape
    tm = min(tm, s)
    sign = -1.0 if inverse else 1.0

    def body(a_ref, b_ref, c_ref, s_ref, o1_ref, o2_ref):
        a = a_ref[...].astype(F32)
        b = b_ref[...].astype(F32)
        c = c_ref[...]
        sn = s_ref[...] * sign
        o1_ref[...] = (a * c - b * sn).astype(o1_ref.dtype)
        o2_ref[...] = (b * c + a * sn).astype(o2_ref.dtype)

    row = pl.BlockSpec((tm, w), lambda i: (i, 0))
    return pl.pallas_call(
        body, name=name, grid=(s // tm,),
        in_specs=[pl.BlockSpec((tm, w), lambda i: (i, col1)), pl.BlockSpec((tm, w), lambda i: (i, col2)), row, row],
        out_specs=[row, row],
        out_shape=[jax.ShapeDtypeStruct((s, w), out_dtype)] * 2,
        compiler_params=_params(("parallel",)),
    )(x1, x2, cos, sin)


MLA_CHAINS = 2


def _mla_scores(q1, q2, k1, k2, row_tok, kstart, tk, scale):
    sc = (_dot(q1, k1, 'nt') + _dot(q2, k2, 'nt')) * scale
    kpos = kstart + lax.broadcasted_iota(jnp.int32, (1, tk), 1)
    keep = kpos <= row_tok
    return sc, keep


def _mla_fwd(qa, qr, ckv, kr, *, name, tk=512):
    hh, s, r = qa.shape
    tq = CHUNK
    tk = min(tk, s)
    rows = hh * tq
    chains = 1
    hc = hh // chains
    rc = hc * tq
    scale = (HEAD + ROPE_DIM) ** -0.5

    def body(qa_ref, qr_ref, ckv_ref, kr_ref, o_ref, lse_ref, m_ref, l_ref, acc_ref):
        i = pl.program_id(0)
        row_tok = i * tq + (lax.broadcasted_iota(jnp.int32, (rc, 1), 0) & (tq - 1))
        m_ref[...] = jnp.full_like(m_ref, MASK_VALUE)
        l_ref[...] = jnp.zeros_like(l_ref)
        acc_ref[...] = jnp.zeros_like(acc_ref)

        def step(j, carry):
            kstart = pl.multiple_of(j * tk, tk)
            k1 = ckv_ref[pl.ds(kstart, tk), :]
            k2 = kr_ref[pl.ds(kstart, tk), :]
            for c in range(chains):
                hs, rs = slice(c * hc, (c + 1) * hc), slice(c * rc, (c + 1) * rc)
                q1 = qa_ref[hs].reshape(rc, r)
                q2 = qr_ref[hs].reshape(rc, ROPE_DIM)
                sc, keep = _mla_scores(q1, q2, k1, k2, row_tok, kstart, tk, scale)
                sc = jnp.where(keep, sc, MASK_VALUE)
                m_old = m_ref[rs, :]
                m_new = jnp.maximum(m_old, jnp.max(sc, axis=-1, keepdims=True))
                p = jnp.exp(sc - m_new)
                alpha = jnp.exp(m_old - m_new)
                l_ref[rs, :] = alpha * l_ref[rs, :] + jnp.sum(p, axis=-1, keepdims=True)
                acc_ref[rs, :] = alpha * acc_ref[rs, :] + _dot(p.astype(BF16), k1, 'nn')
                m_ref[rs, :] = m_new
            return carry

        lax.fori_loop(0, (i * tq) // tk + 1, step, 0)
        l = l_ref[...]
        o_ref[...] = (acc_ref[...] / l).astype(o_ref.dtype).reshape(hh, tq, r)
        lse_ref[...] = (m_ref[...] + jnp.log(l)).reshape(hh, tq, 1)

    return pl.pallas_call(
        body, name=name, grid=(s // tq,),
        in_specs=[pl.BlockSpec((hh, tq, r), lambda i: (0, i, 0)),
                  pl.BlockSpec((hh, tq, ROPE_DIM), lambda i: (0, i, 0)),
                  pl.BlockSpec((s, r), lambda i: (0, 0)),
                  pl.BlockSpec((s, ROPE_DIM), lambda i: (0, 0))],
        out_specs=[pl.BlockSpec((hh, tq, r), lambda i: (0, i, 0)),
                   pl.BlockSpec((hh, tq, 1), lambda i: (0, i, 0))],
        out_shape=[jax.ShapeDtypeStruct((hh, s, r), BF16), jax.ShapeDtypeStruct((hh, s, 1), F32)],
        scratch_shapes=[pltpu.VMEM((rows, 1), F32), pltpu.VMEM((rows, 1), F32), pltpu.VMEM((rows, r), F32)],
        compiler_params=_params(("parallel",)),
    )(qa, qr, ckv, kr)


def _mla_bwd(qa, qr, ckv, kr, o, do, lse, *, name, tk=256):
    hh, s, r = qa.shape
    tq = CHUNK
    tk = min(tk, s)
    rows = hh * tq
    hc = hh // MLA_CHAINS
    rc = hc * tq
    nq = s // tq
    scale = (HEAD + ROPE_DIM) ** -0.5

    def body(qa_ref, qr_ref, o_ref, do_ref, lse_ref, ckv_hbm, kr_hbm, dqa_ref, dqr_ref, dckv_hbm, dkr_hbm,
             ckv_ref, kr_ref, dckv_ref, dkr_ref, dq1_ref, dq2_ref, delta_ref, sem):
        i = pl.program_id(0)

        @pl.when(i == 0)
        def _():
            c1 = pltpu.make_async_copy(ckv_hbm, ckv_ref, sem.at[0])
            c2 = pltpu.make_async_copy(kr_hbm, kr_ref, sem.at[1])
            c1.start()
            c2.start()
            dckv_ref[...] = jnp.zeros_like(dckv_ref)
            dkr_ref[...] = jnp.zeros_like(dkr_ref)
            c1.wait()
            c2.wait()

        delta_ref[...] = jnp.sum(do_ref[...].reshape(rows, r).astype(F32) * o_ref[...].reshape(rows, r).astype(F32),
                                 axis=-1, keepdims=True)
        row_tok = i * tq + (lax.broadcasted_iota(jnp.int32, (rc, 1), 0) & (tq - 1))
        dq1_ref[...] = jnp.zeros_like(dq1_ref)
        dq2_ref[...] = jnp.zeros_like(dq2_ref)

        def step(j, carry):
            kstart = pl.multiple_of(j * tk, tk)
            k1 = ckv_ref[pl.ds(kstart, tk), :]
            k2 = kr_ref[pl.ds(kstart, tk), :]
            dk1, dk2 = None, None
            for c in range(MLA_CHAINS):
                hs, rs = slice(c * hc, (c + 1) * hc), slice(c * rc, (c + 1) * rc)
                q1 = qa_ref[hs].reshape(rc, r)
                q2 = qr_ref[hs].reshape(rc, ROPE_DIM)
                dov = do_ref[hs].reshape(rc, r)
                sc, keep = _mla_scores(q1, q2, k1, k2, row_tok, kstart, tk, scale)
                p = jnp.where(keep, jnp.exp(sc - lse_ref[hs].reshape(rc, 1)), 0.0)
                dp = _dot(dov, k1, 'nt')
                ds = (p * (dp - delta_ref[rs, :]) * scale).astype(BF16)
                pb = p.astype(BF16)
                dq1_ref[rs, :] += _dot(ds, k1, 'nn')
                dq2_ref[rs, :] += _dot(ds, k2, 'nn')
                part1 = _dot(ds, q1, 'tn') + _dot(pb, dov, 'tn')
                part2 = _dot(ds, q2, 'tn')
                dk1 = part1 if dk1 is None else dk1 + part1
                dk2 = part2 if dk2 is None else dk2 + part2
            dckv_ref[pl.ds(kstart, tk), :] += dk1
            dkr_ref[pl.ds(kstart, tk), :] += dk2
            return carry

        lax.fori_loop(0, (i * tq) // tk + 1, step, 0)
        dqa_ref[...] = dq1_ref[...].astype(dqa_ref.dtype).reshape(hh, tq, r)
        dqr_ref[...] = dq2_ref[...].astype(dqr_ref.dtype).reshape(hh, tq, ROPE_DIM)

        @pl.when(i == nq - 1)
        def _():
            c1 = pltpu.make_async_copy(dckv_ref, dckv_hbm, sem.at[0])
            c2 = pltpu.make_async_copy(dkr_ref, dkr_hbm, sem.at[1])
            c1.start()
            c2.start()
            c1.wait()
            c2.wait()

    blk = pl.BlockSpec((hh, tq, r), lambda i: (0, i, 0))
    blk_r = pl.BlockSpec((hh, tq, ROPE_DIM), lambda i: (0, i, 0))
    any_spec = pl.BlockSpec(memory_space=pl.ANY)
    return pl.pallas_call(
        body, name=name, grid=(nq,),
        in_specs=[blk, blk_r, blk, blk, pl.BlockSpec((hh, tq, 1), lambda i: (0, i, 0)), any_spec, any_spec],
        out_specs=[blk, blk_r, any_spec, any_spec],
        out_shape=[jax.ShapeDtypeStruct((hh, s, r), BF16), jax.ShapeDtypeStruct((hh, s, ROPE_DIM), BF16),
                   jax.ShapeDtypeStruct((s, r), F32), jax.ShapeDtypeStruct((s, ROPE_DIM), F32)],
        scratch_shapes=[pltpu.VMEM((s, r), BF16), pltpu.VMEM((s, ROPE_DIM), BF16),
                        pltpu.VMEM((s, r), F32), pltpu.VMEM((s, ROPE_DIM), F32),
                        pltpu.VMEM((rows, r), F32), pltpu.VMEM((rows, ROPE_DIM), F32), pltpu.VMEM((rows, 1), F32),
                        pltpu.SemaphoreType.DMA((2,))],
        compiler_params=_params(("arbitrary",)),
    )(qa, qr, o, do, lse, ckv, kr)


HALO = 16


def _shift_down(prev, cur, shift, first_tile):
    tr = cur.shape[0]
    full = jnp.concatenate([prev, cur], axis=0)
    out = pltpu.roll(full, shift, axis=0)[HALO:]
    row = lax.broadcasted_iota(jnp.int32, (tr, 1), 0)
    return jnp.where(jnp.logical_and(first_tile, row < shift), 0.0, out)


def _shift_up(cur, nxt, shift, last_tile):
    tr = cur.shape[0]
    full = jnp.concatenate([cur, nxt], axis=0)
    out = pltpu.roll(full, tr + HALO - shift, axis=0)[:tr]
    row = lax.broadcasted_iota(jnp.int32, (tr, 1), 0)
    return jnp.where(jnp.logical_and(last_tile, row >= tr - shift), 0.0, out)


def _lane_chunks(width, lanes):
    return [slice(c0, min(c0 + lanes, width)) for c0 in range(0, width, lanes)]


def _conv_taps(prev_ref, cur_ref, cw_ref, cb_ref, first_tile, cs):
    cur = cur_ref[:, cs].astype(F32)
    prev = prev_ref[:, cs].astype(F32)
    a1 = _shift_down(prev, cur, 1, first_tile)
    a2 = _shift_down(prev, cur, 2, first_tile)
    c = a2 * cw_ref[0:1, cs] + a1 * cw_ref[1:2, cs] + cur * cw_ref[2:3, cs] + cb_ref[:, cs]
    return c, (a2, a1, cur)


def _conv_in_specs(tr, bw, half, layer, row_of, blk_of):
    per = tr // HALO
    specs = []
    for off in (0, half):
        specs.append(pl.BlockSpec((None, HALO, bw), lambda *g, off=off: (blk_of(*g) + off, jnp.maximum(row_of(*g) * per - 1, 0), 0)))
        specs.append(pl.BlockSpec((None, tr, bw), lambda *g, off=off: (blk_of(*g) + off, row_of(*g), 0)))
    for off in (0, half):
        specs.append(pl.BlockSpec((None, None, CONV_W, bw), lambda *g, off=off: (blk_of(*g) + off, layer, 0, 0)))
    for off in (0, half):
        specs.append(pl.BlockSpec((None, 1, bw), lambda *g, off=off: (layer * 2 * half + blk_of(*g) + off, 0, 0)))
    return specs


def _conv_fwd(a, cw, cb, layer, *, name, tr=256):
    nb, s, bw = a.shape
    half = nb // 2
    tr = min(tr, s)

    def body(gp_ref, gc_ref, vp_ref, vc_ref, cwg_ref, cwv_ref, cbg_ref, cbv_ref, o_ref):
        first = pl.program_id(0) == 0
        for cs in _lane_chunks(bw, 256):
            gate, _ = _conv_taps(gp_ref, gc_ref, cwg_ref, cbg_ref, first, cs)
            val, _ = _conv_taps(vp_ref, vc_ref, cwv_ref, cbv_ref, first, cs)
            o_ref[:, cs] = (gate * _sigmoid(gate) * val).astype(o_ref.dtype)

    return pl.pallas_call(
        body, name=name, grid=(s // tr, half),
        in_specs=_conv_in_specs(tr, bw, half, layer, lambda i, j: i, lambda i, j: j),
        out_specs=pl.BlockSpec((None, tr, bw), lambda i, j: (j, i, 0)),
        out_shape=jax.ShapeDtypeStruct((half, s, bw), BF16),
        compiler_params=_params(("parallel", "parallel")),
    )(a, a, a, a, cw, cw, cb, cb)


def _conv_bwd_dc(a, dact, cw, cb, layer, *, name, after=None, tr=256):
    nb, s, bw = a.shape
    half = nb // 2
    tr = min(tr, s)

    def body(*refs):
        gp_ref, gc_ref, vp_ref, vc_ref, cwg_ref, cwv_ref, cbg_ref, cbv_ref, da_ref = refs[:9]
        dc_ref, dw_ref, db_ref = refs[-3:]
        first = pl.program_id(1) == 0

        @pl.when(first)
        def _():
            dw_ref[...] = jnp.zeros_like(dw_ref)
            db_ref[...] = jnp.zeros_like(db_ref)

        for cs in _lane_chunks(bw, 128):
            gate, gtaps = _conv_taps(gp_ref, gc_ref, cwg_ref, cbg_ref, first, cs)
            val, vtaps = _conv_taps(vp_ref, vc_ref, cwv_ref, cbv_ref, first, cs)
            dact_v = da_ref[:, cs].astype(F32)
            sg = _sigmoid(gate)
            dgate = dact_v * val * (sg * (1.0 + gate * (1.0 - sg)))
            dval = dact_v * (gate * sg)
            dc_ref[0, :, cs] = dgate.astype(dc_ref.dtype)
            dc_ref[1, :, cs] = dval.astype(dc_ref.dtype)
            for kk in range(CONV_W):
                dw_ref[0, kk:kk + 1, cs] += jnp.sum(dgate * gtaps[kk], axis=0, keepdims=True)
                dw_ref[1, kk:kk + 1, cs] += jnp.sum(dval * vtaps[kk], axis=0, keepdims=True)
            db_ref[0, :, cs] += jnp.sum(dgate, axis=0, keepdims=True)
            db_ref[1, :, cs] += jnp.sum(dval, axis=0, keepdims=True)

    outs = pl.pallas_call(
        body, name=name, grid=(half, s // tr),
        in_specs=_conv_in_specs(tr, bw, half, layer, lambda j, i: i, lambda j, i: j)
        + [pl.BlockSpec((None, tr, bw), lambda j, i: (j, i, 0))]
        + ([pl.BlockSpec(memory_space=pl.ANY)] if after is not None else []),
        out_specs=[pl.BlockSpec((2, None, tr, bw), lambda j, i: (0, j, i, 0)),
                   pl.BlockSpec((2, None, CONV_W, bw), lambda j, i: (0, j, 0, 0)),
                   pl.BlockSpec((2, None, 1, bw), lambda j, i: (0, j, 0, 0))],
        out_shape=[jax.ShapeDtypeStruct((2, half, s, bw), BF16),
                   jax.ShapeDtypeStruct((2, half, CONV_W, bw), F32),
                   jax.ShapeDtypeStruct((2, half, 1, bw), F32)],
        compiler_params=_params(("parallel", "arbitrary")),
    )(a, a, a, a, cw, cw, cb, cb, dact, *([after] if after is not None else []))
    dc, dw, db = outs
    return dc.reshape(nb, s, bw), dw.reshape(nb, CONV_W, bw), db.reshape(nb, 1, bw)


def _conv_bwd_da(dc, cw, layer, *, name, tr=512):
    nb, s, bw = dc.shape
    tr = min(tr, s)
    ni = s // tr
    per = tr // HALO
    last_halo = s // HALO - 1

    def body(c_ref, n_ref, w_ref, o_ref):
        last = pl.program_id(0) == ni - 1
        for cs in _lane_chunks(bw, 256):
            cur = c_ref[:, cs].astype(F32)
            nxt = n_ref[:, cs].astype(F32)
            da = (cur * w_ref[2:3, cs] + _shift_up(cur, nxt, 1, last) * w_ref[1:2, cs]
                  + _shift_up(cur, nxt, 2, last) * w_ref[0:1, cs])
            o_ref[:, cs] = da.astype(o_ref.dtype)

    tile = pl.BlockSpec((None, tr, bw), lambda i, j: (j, i, 0))
    return pl.pallas_call(
        body, name=name, grid=(ni, nb),
        in_specs=[tile,
                  pl.BlockSpec((None, HALO, bw), lambda i, j: (j, jnp.minimum((i + 1) * per, last_halo), 0)),
                  pl.BlockSpec((None, None, CONV_W, bw), lambda i, j: (j, layer, 0, 0))],
        out_specs=tile,
        out_shape=jax.ShapeDtypeStruct((nb, s, bw), BF16),
        compiler_params=_params(("parallel", "parallel")),
    )(dc, dc, cw)


def _rope_tables(positions):
    inv = 1.0 / (ROPE_THETA ** (jnp.arange(0, ROPE_DIM, 2, dtype=F32) / ROPE_DIM))
    ang = positions.astype(F32)[:, None] * inv
    return jnp.cos(ang), jnp.sin(ang)


def _heads_to_major(r1, r2):
    s = r1.shape[0]
    both = jnp.concatenate([r1.reshape(s, G_HEADS, ROPE_HALF), r2.reshape(s, G_HEADS, ROPE_HALF)], axis=-1)
    return both.transpose(1, 0, 2)


def _heads_from_major(qr):
    s = qr.shape[1]
    t = qr.transpose(1, 0, 2)
    return t[:, :, :ROPE_HALF].reshape(s, G_HEADS * ROPE_HALF), t[:, :, ROPE_HALF:].reshape(s, G_HEADS * ROPE_HALF)


def _local_step(x, mem, positions, target, rep, fetch, emit):
    s, d = x.shape
    n_b = DEPTH - N_A
    tm = min(1024, s)
    cos, sin = _rope_tables(positions)
    cos12 = jnp.tile(cos, (1, G_HEADS))
    sin12 = jnp.tile(sin, (1, G_HEADS))
    r1_col = G_W // (G_HEADS * ROPE_HALF)
    b_sp_t = rep['b_sp'].transpose(0, 2, 1)

    saved = []
    kv = None
    shared = None
    for l in range(DEPTH):
        wm = fetch(('in', l), x)
        if l == 0:
            shared = {'g_v': wm['g_v'], 'conv_w': wm['conv_w']}
            bw = shared['conv_w'].shape[-1]
            conv_b = rep['conv_b'].reshape(-1, 1, bw)
        sv = {'x_in': x, 'wm': wm}
        if l == N_A:
            xn_kv = _rmsnorm(x, rep['g_kv'], name="kvnorm")
            kvx = _mm(xn_kv, wm['w_kv_a'], dims='nn', out_dtype=F32, name="kvproj", tm=tm, tn=KV_PAD)
            ckv = _rmsnorm(kvx, rep['g_kv_lat'], width=KV_RANK, name="ckvnorm")
            k1, k2 = _rope(kvx[:, KV_RANK:KV_RANK + ROPE_HALF], kvx[:, KV_RANK + ROPE_HALF:KV_RANK + ROPE_DIM],
                           cos, sin, name="krope")
            kr = jnp.concatenate([k1, k2], axis=-1)
            kv = {'x': x, 'xn': xn_kv, 'kvx': kvx, 'ckv': ckv, 'kr': kr, 'w_kv_a': wm['w_kv_a']}
        h = _rmsnorm(x, rep['g_mix'][l], name=f"mixnorm{l}")
        if l < N_A:
            z = _mm(h, wm['w_in'], dims='nn', out_dtype=BF16, name=f"in_a{l}", tm=tm, tn=512)
            main = _sgu_fwd(z, shared['g_v'][l], rep['w_sp'][l], b_sp_t[l], name=f"sgu{l}")
            qcol = 2 * G_W // MEM_W
        else:
            j = l - N_A
            z = _mm(h, wm['w_in'], dims='nn', out_dtype=BF16, name=f"in_b{j}", tm=tm, tn=1024)
            qn = _rmsnorm(z, rep['g_q_lat'][j], width=Q_RANK, name=f"qnorm{j}")
            qp = _mm(qn, wm['w_uqp'], dims='nn', out_dtype=BF16, name=f"uq{j}", tm=tm, tn=768)
            rr1, rr2 = _rope(qp, qp, cos12, sin12, col1=r1_col, col2=r1_col + 1, name=f"qrope{j}")
            qr = _heads_to_major(rr1, rr2)
            qa = _mm_heads(qp, wm['w_uk'], mode='to_lat', name=f"qabsorb{j}")
            o_lat, lse = _mla_fwd(qa, qr, kv['ckv'], kv['kr'], name=f"mla{j}")
            main = _mm_heads(o_lat, wm['w_uv'], mode='from_lat', name=f"uv{j}")
            qcol = Q_RANK // MEM_W
            sv.update(qn=qn, qp=qp, qr=qr, qa=qa, o_lat=o_lat, lse=lse)
        wm.update(fetch(('rest', l), z))
        memn = _rmsnorm(mem, rep['g_mem'][l], name=f"memnorm{l}")
        kvm = _mm(memn, wm['w_mem_kv'], dims='nn', out_dtype=BF16, name=f"memkv{l}", tm=tm, tn=1024)
        mix = _memattn_fwd(z, kvm, main, qcol=qcol, name=f"memattn{l}")
        x_mid = _mm(mix, wm['w_out'], dims='nn', res=x, out_dtype=F32, name=f"out{l}", tm=tm, tn=1024)
        wf = fetch(('up', l), x_mid)
        h2 = _rmsnorm(x_mid, rep['g_ffn'][l], name=f"ffnnorm{l}")
        a = _mm(h2, wf['w_up'], dims='nn', b_blocked=True, out_dtype=BF16, out_block=bw,
                name=f"up{l}", tm=tm, tn=bw)
        act = _conv_fwd(a, shared['conv_w'], conv_b, l, name=f"conv{l}")
        wf.update(fetch(('down', l), act))
        x = _mm(act, wf['w_down'], dims='nn', a_blocked=True, tk=bw, res=x_mid, out_dtype=F32,
                name=f"down{l}", tm=tm, tn=1024)
        sv.update(h=h, memn=memn, kvm=kvm, z=z, qcol=qcol, mix=mix, x_mid=x_mid, h2=h2, a=a, act=act, wf=wf)
        saved.append(sv)

    sq, dx, dg_final = _final_loss(x, target, rep['g_final'], name="loss")

    g = {k: [None] * DEPTH for k in ('g_mix', 'g_ffn', 'g_mem', 'conv_w', 'conv_b')}
    for k in ('g_v', 'w_sp', 'b_sp'):
        g[k] = [None] * N_A
    g['g_q_lat'] = [None] * n_b
    g['g_final'] = dg_final
    dckv_sum, dkr_sum = None, None

    for l in reversed(range(DEPTH)):
        sv = saved[l]
        wm, wf = sv['wm'], sv['wf']
        dact = _mm(dx, wf['w_down'], dims='nt', out_dtype=BF16, out_block=bw, name=f"d_act{l}", tm=tm, tn=bw)
        dw_down = _mm(sv['act'], dx, dims='tn', a_blocked=True, out_dtype=BF16, name=f"dw_down{l}", tm=bw, tn=256)
        tok = emit(('down', l), {'w_ffn_down': dw_down})
        dc, dcw, dcb = _conv_bwd_dc(sv['a'], dact, shared['conv_w'], conv_b, l, after=tok, name=f"d_conv{l}")
        g['conv_w'][l], g['conv_b'][l] = dcw, dcb
        da = _conv_bwd_da(dc, shared['conv_w'], l, name=f"d_convin{l}")
        dh2 = _mm(da, wf['w_up'], dims='nt', a_blocked=True, b_blocked=True, tk=bw,
                  out_dtype=BF16, name=f"d_h2{l}", tm=tm, tn=1024)
        dw_up = _mm(sv['h2'], da, dims='tn', b_blocked=True, out_dtype=BF16, out_block=bw,
                    name=f"dw_up{l}", tm=512, tn=bw, n_outer=True)
        tok = emit(('up', l), {'w_ffn_up': dw_up})
        dx_mid, g['g_ffn'][l] = _rmsnorm_bwd(sv['x_mid'], rep['g_ffn'][l], dh2, dres=dx, after=tok, name=f"d_ffnnorm{l}")
        dmix = _mm(dx_mid, wm['w_out'], dims='nt', out_dtype=BF16, name=f"d_mix{l}", tm=tm, tn=1024)
        gm = {'w_out': _mm(sv['mix'], dx_mid, dims='tn', out_dtype=BF16, name=f"dw_out{l}", tm=1024, tn=256)}
        dqm, dkvm = _memattn_bwd(sv['z'], sv['kvm'], dmix, qcol=sv['qcol'], name=f"d_memattn{l}")
        gm['w_mem_kv'] = _mm(sv['memn'], dkvm, dims='tn', out_dtype=BF16, name=f"dw_memkv{l}", tm=1024, tn=1024)
        dmemn = _mm(dkvm, wm['w_mem_kv'], dims='nt', out_dtype=F32, name=f"d_memn{l}", tm=tm, tn=1024)
        _, g['g_mem'][l] = _rmsnorm_bwd(mem, rep['g_mem'][l], dmemn, out_dtype=BF16, name=f"d_memnorm{l}")
        if l < N_A:
            dz, dwsp, dbsp_t, dgv = _sgu_bwd(sv['z'], dmix, dqm, shared['g_v'][l], rep['w_sp'][l], b_sp_t[l],
                                             name=f"d_sgu{l}")
            g['w_sp'][l], g['b_sp'][l], g['g_v'][l] = dwsp, dbsp_t.T, dgv
            dh = _mm(dz, wm['w_in'], dims='nt', out_dtype=BF16, name=f"d_h_a{l}", tm=tm, tn=1024)
            gm['w_in_a'] = _mm(sv['h'], dz, dims='tn', out_dtype=BF16, name=f"dw_in_a{l}", tm=1024, tn=512)
        else:
            j = l - N_A
            do_lat = _mm_heads(dmix, wm['w_uv'], mode='to_lat', name=f"d_olat{j}")
            gm['w_uv'] = _mm_heads(sv['o_lat'], dmix, mode='wgrad', name=f"dw_uv{j}")
            dqa, dqr, dckv, dkr = _mla_bwd(sv['qa'], sv['qr'], kv['ckv'], kv['kr'], sv['o_lat'], do_lat, sv['lse'],
                                           name=f"d_mla{j}")
            dckv_sum = dckv if dckv_sum is None else dckv_sum + dckv
            dkr_sum = dkr if dkr_sum is None else dkr_sum + dkr
            dq_nope = _mm_heads(dqa, wm['w_uk'], mode='from_lat', name=f"d_qnope{j}")
            gm['w_uk'] = _mm_heads(dqa, sv['qp'], mode='wgrad', name=f"dw_uk{j}")
            dr1, dr2 = _heads_from_major(dqr)
            dq1, dq2 = _rope(dr1, dr2, cos12, sin12, inverse=True, name=f"d_qrope{j}")
            dqp = jnp.concatenate([dq_nope, dq1, dq2], axis=-1)
            dqn = _mm(dqp, wm['w_uqp'], dims='nt', out_dtype=BF16, name=f"d_qn{j}", tm=tm, tn=512)
            gm['w_uqp'] = _mm(sv['qn'], dqp, dims='tn', out_dtype=BF16, name=f"dw_uq{j}", tm=512, tn=768)
            dqlat, g['g_q_lat'][j] = _rmsnorm_bwd(sv['z'], rep['g_q_lat'][j], dqn, width=Q_RANK, out_dtype=BF16,
                                                 name=f"d_qnorm{j}")
            dz = jnp.concatenate([dqlat, dqm], axis=-1)
            dh = _mm(dz, wm['w_in'], dims='nt', out_dtype=BF16, name=f"d_h_b{j}", tm=tm, tn=1024)
            gm['w_in_b'] = _mm(sv['h'], dz, dims='tn', out_dtype=BF16, name=f"dw_in_b{j}", tm=1024, tn=512)
        tok = emit(('mix', l), gm)
        dx, g['g_mix'][l] = _rmsnorm_bwd(sv['x_in'], rep['g_mix'][l], dh, dres=dx_mid, after=tok, name=f"d_mixnorm{l}")
        if l == N_A:
            dkvx_c, g['g_kv_lat'] = _rmsnorm_bwd(kv['kvx'], rep['g_kv_lat'], dckv_sum, width=KV_RANK, out_dtype=BF16,
                                                 name="d_ckvnorm")
            dk1, dk2 = _rope(dkr_sum[:, :ROPE_HALF], dkr_sum[:, ROPE_HALF:], cos, sin, inverse=True, name="d_krope")
            dkvx = jnp.concatenate([dkvx_c, dk1, dk2, jnp.zeros((s, KV_PAD - KV_RANK - ROPE_DIM), BF16)], axis=-1)
            dxn = _mm(dkvx, kv['w_kv_a'], dims='nt', out_dtype=BF16, name="d_kvnorm_in", tm=tm, tn=1024)
            dw_kv = _mm(kv['xn'], dkvx, dims='tn', out_dtype=BF16, name="dw_kv", tm=1024, tn=KV_PAD)
            tok = emit(('kv', 0), {'w_kv_a': dw_kv})
            dx, g['g_kv'] = _rmsnorm_bwd(kv['x'], rep['g_kv'], dxn, dres=dx, after=tok, name="d_kvnorm")
    return jnp.sum(sq), dx, g


MESH_IDS = pl.DeviceIdType.MESH
PEER_MASKS = tuple((k >> 2 & 1, k >> 1 & 1, k & 1) for k in range(1, N_DEV))
CHIP_MASKS = ((1, 0), (0, 1), (1, 1))
N_PEER = N_DEV - 1
SEMS_PER_BUFFER = 2 * N_PEER + 1
DATAFLOW = pltpu.SideEffectType.DATAFLOW_SIDE_EFFECTING
HBM_SPEC = pl.BlockSpec(memory_space=pltpu.HBM)
SEM_SPEC = pl.BlockSpec(memory_space=pltpu.SEMAPHORE)


def _my_position():
    return lax.axis_index("x"), lax.axis_index("y"), lax.axis_index("c")


def _flip(pos, mask):
    return tuple(1 - p if f else p for p, f in zip(pos, mask))


def _linear_id(pos):
    return 4 * pos[0] + 2 * pos[1] + pos[2]


def _hbm(x):
    return pltpu.with_memory_space_constraint(x, pltpu.HBM)


def _buffer_copies(src_ref, lead, land_ref, sems, scatter, near=False):
    me = _my_position()
    my_id = _linear_id(me)
    src = src_ref.at[lead] if lead else src_ref
    own = pltpu.make_async_copy(src.at[my_id] if scatter else src, land_ref.at[my_id], sems.at[2 * N_PEER])
    pairs = []
    for k, mask in enumerate(PEER_MASKS):
        if near and mask[2] == 1 and mask != (0, 0, 1):
            continue
        peer = _flip(me, mask)
        peer_id = _linear_id(peer)
        block = src.at[peer_id] if scatter else src
        send = pltpu.make_async_remote_copy(src_ref=block, dst_ref=land_ref.at[my_id], send_sem=sems.at[k],
                                            recv_sem=sems.at[N_PEER + k], device_id=peer, device_id_type=MESH_IDS)
        arrival = pltpu.make_async_remote_copy(src_ref=block, dst_ref=land_ref.at[peer_id], send_sem=sems.at[k],
                                               recv_sem=sems.at[N_PEER + k], device_id=peer, device_id_type=MESH_IDS)
        pairs.append((send, arrival))
    return own, pairs


def _exchange_start(srcs, buffers, *, name, scatter):
    ns, nb = len(srcs), len(buffers)
    lands = [_hbm(lax.empty((N_DEV,) + tuple(shape), dtype)) for _, _, shape, dtype, _ in buffers]

    def body(*refs):
        src_refs, land_refs = refs[:ns], refs[ns:ns + nb]
        sem_refs = refs[ns + nb:ns + 2 * nb]
        token = refs[-1]
        for b, (si, lead, _, _, near) in enumerate(buffers):
            own, pairs = _buffer_copies(src_refs[si], lead, land_refs[b], sem_refs[b], scatter, near)
            own.start()
            for send, _ in pairs:
                send.start()
        token[...] = jnp.zeros_like(token)

    out_shape = ([pltpu.SemaphoreType.DMA((SEMS_PER_BUFFER,))] * nb
                 + [pltpu.HBM(a.shape, a.dtype) for a in srcs]
                 + [pltpu.HBM(a.shape, a.dtype) for a in lands]
                 + [jax.ShapeDtypeStruct((8, 128), F32)])
    aliases = {i: nb + i for i in range(ns + nb)}
    outs = pl.pallas_call(
        body, name=name, in_specs=[HBM_SPEC] * (ns + nb),
        out_specs=[SEM_SPEC] * nb + [HBM_SPEC] * (ns + nb) + [pl.BlockSpec(memory_space=pltpu.VMEM)],
        out_shape=out_shape, input_output_aliases=aliases,
        compiler_params=pltpu.CompilerParams(has_side_effects=DATAFLOW),
    )(*[_hbm(a) for a in srcs], *lands)
    sems = list(outs[:nb])
    src_thru = list(outs[nb:nb + ns])
    land_thru = list(outs[nb + ns:nb + ns + nb])
    return sems, land_thru, src_thru, outs[-1]


def _exchange_wait(srcs_thru, buffers, sems, lands, after, *, name, scatter):
    ns, nb = len(srcs_thru), len(buffers)
    has_after = after is not None

    def body(*refs):
        src_refs, land_refs = refs[:ns], refs[ns:ns + nb]
        sem_refs = refs[ns + nb:ns + 2 * nb]
        for b, (si, lead, _, _, near) in enumerate(buffers):
            own, pairs = _buffer_copies(src_refs[si], lead, land_refs[b], sem_refs[b], scatter, near)
            for send, arrival in pairs:
                send.wait_send()
                arrival.wait_recv()
            own.wait()

    operands = list(srcs_thru) + list(lands) + list(sems) + ([after] if has_after else [])
    in_specs = ([HBM_SPEC] * (ns + nb) + [SEM_SPEC] * nb + ([pl.BlockSpec(memory_space=pl.ANY)] if has_after else []))
    outs = pl.pallas_call(
        body, name=name, in_specs=in_specs, out_specs=[HBM_SPEC] * nb,
        out_shape=[pltpu.HBM(a.shape, a.dtype) for a in lands],
        input_output_aliases={ns + b: b for b in range(nb)},
        compiler_params=pltpu.CompilerParams(has_side_effects=DATAFLOW),
    )(*operands)
    return list(outs)


def _exchange(arrays, *, name, scatter, near=None):
    n = len(arrays)
    near = [False] * n if near is None else near
    out_shapes = [jax.ShapeDtypeStruct(a.shape if scatter else (N_DEV,) + a.shape, a.dtype) for a in arrays]

    def body(*refs):
        srcs, outs, sems = refs[:n], refs[n:2 * n], refs[2 * n:]
        started = []
        for a in range(n):
            own, pairs = _buffer_copies(srcs[a], (), outs[a], sems[a], scatter, near[a])
            own.start()
            for send, _ in pairs:
                send.start()
            started.append((own, pairs))
        for own, pairs in started:
            for send, arrival in pairs:
                arrival.wait_recv()
                send.wait_send()
            own.wait()

    any_spec = pl.BlockSpec(memory_space=pl.ANY)
    outs = pl.pallas_call(
        body, name=name, in_specs=[any_spec] * n, out_specs=[any_spec] * n, out_shape=out_shapes,
        scratch_shapes=[pltpu.SemaphoreType.DMA((SEMS_PER_BUFFER,))] * n,
    )(*arrays)
    return list(outs)


def _forward_to_sibling(lands, *, name):
    n = len(lands)

    def body(*refs):
        ins, outs, sems = refs[:n], refs[n:2 * n], refs[2 * n:]
        me = _my_position()
        sibling = _flip(me, (0, 0, 1))
        pairs = []
        for b in range(n):
            for k, (fx, fy) in enumerate(CHIP_MASKS):
                mine = _linear_id(_flip(me, (fx, fy, 0)))
                theirs = _linear_id(_flip(me, (fx, fy, 1)))
                send = pltpu.make_async_remote_copy(
                    src_ref=ins[b].at[mine], dst_ref=outs[b].at[mine], send_sem=sems[b].at[k],
                    recv_sem=sems[b].at[len(CHIP_MASKS) + k], device_id=sibling, device_id_type=MESH_IDS)
                arrival = pltpu.make_async_remote_copy(
                    src_ref=ins[b].at[mine], dst_ref=outs[b].at[theirs], send_sem=sems[b].at[k],
                    recv_sem=sems[b].at[len(CHIP_MASKS) + k], device_id=sibling, device_id_type=MESH_IDS)
                send.start()
                pairs.append((send, arrival))
        for send, arrival in pairs:
            arrival.wait_recv()
            send.wait_send()

    any_spec = pl.BlockSpec(memory_space=pl.ANY)
    outs = pl.pallas_call(
        body, name=name, in_specs=[any_spec] * n, out_specs=[any_spec] * n,
        out_shape=[jax.ShapeDtypeStruct(a.shape, a.dtype) for a in lands],
        input_output_aliases={b: b for b in range(n)},
        scratch_shapes=[pltpu.SemaphoreType.DMA((2 * len(CHIP_MASKS),))] * n,
    )(*lands)
    return list(outs)


def _sum_slots(parts_ref):
    total = parts_ref[0].astype(F32)
    for q in range(1, parts_ref.shape[0]):
        total = total + parts_ref[q].astype(F32)
    return total


def _row_tile(rows, cols, n_arrays):
    budget = (12 * 1024 * 1024) // (4 * n_arrays * max(cols, 128))
    t = rows
    while t > budget and t % 2 == 0 and (t // 2) % 16 == 0:
        t //= 2
    return t


def _sum_adam(parts, w, m, v, layer, outs, *, name):
    q, r, c = parts.shape
    nl = w.shape[0]
    tr = _row_tile(r, c, q + 7)
    c1 = 1.0 - ADAM_B1 ** ADAM_STEP
    c2 = 1.0 - ADAM_B2 ** ADAM_STEP
    if outs is None:
        outs = [lax.empty((nl, r, c), F32) for _ in range(4)]

    def body(p_ref, w_ref, m_ref, v_ref, g_in, d_in, mo_in, vo_in, g_ref, d_ref, mo_ref, vo_ref):
        grad = _sum_slots(p_ref)
        m_new = ADAM_B1 * m_ref[...] + (1.0 - ADAM_B1) * grad
        v_new = ADAM_B2 * v_ref[...] + (1.0 - ADAM_B2) * (grad * grad)
        m_hat = m_new / c1
        v_hat = v_new / c2
        g_ref[...] = grad
        d_ref[...] = -ADAM_LR * (m_hat / (jnp.sqrt(v_hat) + ADAM_EPS) + ADAM_WD * w_ref[...])
        mo_ref[...] = m_new
        vo_ref[...] = v_new

    tile = pl.BlockSpec((None, tr, c), lambda i: (layer, i, 0))
    any_spec = pl.BlockSpec(memory_space=pl.ANY)
    return pl.pallas_call(
        body, name=name, grid=(r // tr,),
        in_specs=[pl.BlockSpec((q, tr, c), lambda i: (0, i, 0)), tile, tile, tile] + [any_spec] * 4,
        out_specs=[tile] * 4, out_shape=[jax.ShapeDtypeStruct((nl, r, c), F32)] * 4,
        input_output_aliases={4: 0, 5: 1, 6: 2, 7: 3},
        compiler_params=_params(("parallel",)),
    )(parts, w, m, v, *outs)


def _sum_parts(parts, *, name):
    q, r, c = parts.shape

    def body(p_ref, o_ref):
        o_ref[...] = _sum_slots(p_ref)

    return pl.pallas_call(
        body, name=name, in_specs=[pl.BlockSpec((q, r, c), lambda: (0, 0, 0))],
        out_specs=pl.BlockSpec((r, c), lambda: (0, 0)), out_shape=jax.ShapeDtypeStruct((r, c), F32),
        compiler_params=_params(),
    )(parts)


INPUT_NAMES = (['x', 'mem', 'positions'] + WEIGHTS + ['loss_target'] + ['m_' + n for n in WEIGHTS]
               + ['v_' + n for n in WEIGHTS])
SMALL_ALIGN = N_DEV * 8 * 128
TWO_LEVEL_LAYERS = N_A
GROUP_ORDER = ('in', 'rest', 'up', 'down')
GROUP_WEIGHTS = {'in': (['w_in_a'], ['w_in_b', 'w_uq', 'w_uk', 'w_uv']), 'rest': (['w_mem_kv', 'w_out'],) * 2,
                 'up': (['w_ffn_up'],) * 2, 'down': (['w_ffn_down'],) * 2}
LAYERED = {'w_in_a': 0, 'w_in_b': N_A, 'w_uq': N_A, 'w_uk': N_A, 'w_uv': N_A, 'w_mem_kv': 0, 'w_out': 0,
           'w_ffn_up': 0, 'w_ffn_down': 0}


def _permute_uq(w_uq):
    r = w_uq.shape[0]
    q = w_uq.reshape(r, G_HEADS, HEAD + ROPE_DIM)
    return jnp.concatenate([q[..., :HEAD].reshape(r, -1), q[..., HEAD:HEAD + ROPE_HALF].reshape(r, -1),
                            q[..., HEAD + ROPE_HALF:].reshape(r, -1)], axis=-1)


def _unpermute_uq(w_uqp):
    r = w_uqp.shape[0]
    nope = w_uqp[..., :G_W].reshape(r, G_HEADS, HEAD)
    r1 = w_uqp[..., G_W:G_W + G_HEADS * ROPE_HALF].reshape(r, G_HEADS, ROPE_HALF)
    r2 = w_uqp[..., G_W + G_HEADS * ROPE_HALF:].reshape(r, G_HEADS, ROPE_HALF)
    return jnp.concatenate([nope, r1, r2], axis=-1).reshape(r, -1)


def _cols_from_stack(st):
    _, r, n = st.shape
    return st.transpose(1, 0, 2).reshape(r, N_DEV * n)


def _cols_to_stack(wh):
    r, c = wh.shape
    return wh.reshape(r, N_DEV, c // N_DEV).transpose(1, 0, 2)


def _group_weights(group):
    kind, l = group
    return GROUP_WEIGHTS[kind][0 if l < N_A else 1]


def _step(args):
    p = dict(zip(INPUT_NAMES, args))
    x, mem, positions, target = p['x'][0], p['mem'][0], p['positions'][0], p['loss_target'][0]
    d = x.shape[-1]
    my_id = _linear_id(_my_position())

    w_kv_pad = jnp.pad(p['w_kv_a'], ((0, 0), (0, KV_PAD - p['w_kv_a'].shape[1])))
    shard = {k: p[k].astype(BF16) for k in LAYERED}
    shard['w_uk'] = shard['w_uk'].reshape(shard['w_uk'].shape[0], shard['w_uk'].shape[1], -1)
    shard['w_uv'] = shard['w_uv'].reshape(shard['w_uv'].shape[0], shard['w_uv'].shape[1], -1)
    shard.update(conv_w=p['conv_w'], g_v=p['g_v'], w_kv_a=w_kv_pad.astype(BF16))
    src_names = list(shard)
    gather_groups = []
    for l in range(DEPTH):
        gather_groups += [(kind, l) for kind in GROUP_ORDER]
    buffers, owner = [], []
    for group in gather_groups:
        kind, l = group
        for k in _group_weights(group):
            buffers.append((src_names.index(k), (l - LAYERED[k],), shard[k].shape[1:], shard[k].dtype, l < TWO_LEVEL_LAYERS))
            owner.append((group, k))
        if group == ('in', 0):
            for k in ('g_v', 'conv_w'):
                buffers.append((src_names.index(k), (), shard[k].shape, shard[k].dtype, True))
                owner.append((group, k))
        if group == ('in', N_A):
            buffers.append((src_names.index('w_kv_a'), (), shard['w_kv_a'].shape, BF16, False))
            owner.append((group, 'w_kv_a'))
    g_sems, g_lands, g_srcs, _ = _exchange_start([shard[k] for k in src_names], buffers, name="gather_start",
                                                 scatter=False)

    def fetch(group, after):
        idx = [i for i, (grp, _) in enumerate(owner) if grp == group]
        landed = _exchange_wait(g_srcs, [buffers[i] for i in idx], [g_sems[i] for i in idx],
                                [g_lands[i] for i in idx], after, name=f"gather_wait_{group[0]}{group[1]}",
                                scatter=False)
        if group[1] < TWO_LEVEL_LAYERS:
            landed = _forward_to_sibling(landed, name=f"gather_forward_{group[0]}{group[1]}")
        got = {owner[i][1]: t for i, t in zip(idx, landed)}
        out = {}
        for k, t in got.items():
            if k in ('w_in_a', 'w_uq'):
                out[k] = _cols_from_stack(t)
            elif k == 'g_v':
                out[k] = t.transpose(1, 0, 2).reshape(t.shape[1], -1)
            elif k in ('w_ffn_up', 'conv_w'):
                out[k] = t
            else:
                out[k] = t.reshape(-1, t.shape[-1])
        if 'w_uq' in out:
            out['w_uqp'] = _permute_uq(out.pop('w_uq'))
        for old, new in (('w_in_a', 'w_in'), ('w_in_b', 'w_in'), ('w_ffn_up', 'w_up'), ('w_ffn_down', 'w_down')):
            if old in out:
                out[new] = out.pop(old)
        return out

    pending = []

    def emit(group, grads):
        send = {}
        for k, t in grads.items():
            if k == 'w_in_a':
                send[k] = _cols_to_stack(t)
            elif k == 'w_uqp':
                send['w_uq'] = _cols_to_stack(_unpermute_uq(t))
            elif k == 'w_ffn_up':
                send[k] = t
            elif k == 'w_kv_a':
                cols = p['w_kv_a'].shape[1]
                send[k] = t[:, :cols].reshape(N_DEV, -1, cols)
            else:
                send[k] = t.reshape(N_DEV, t.shape[0] // N_DEV, t.shape[1])
        keys = list(send)
        bufs = [(i, (), send[k].shape[1:], send[k].dtype, False) for i, k in enumerate(keys)]
        sems, lands, srcs, token = _exchange_start([send[k] for k in keys], bufs,
                                                   name=f"scatter_start_{group[0]}{group[1]}", scatter=True)
        pending.append((group, keys, bufs, sems, lands, srcs))
        return token

    rep = {k: p[k] for k in REPLICATED}
    sq, grad_x, g = _local_step(x, mem, positions, target, rep, fetch, emit)
    loss = (0.5 / d) * lax.psum(sq, ("x", "y", "c"))

    small = {
        'g_mix': jnp.concatenate(g['g_mix']), 'g_ffn': jnp.concatenate(g['g_ffn']), 'g_final': g['g_final'],
        'w_sp': jnp.stack(g['w_sp']), 'b_sp': jnp.stack(g['b_sp']), 'g_kv': g['g_kv'], 'g_kv_lat': g['g_kv_lat'],
        'g_q_lat': jnp.concatenate(g['g_q_lat']), 'g_mem': jnp.concatenate(g['g_mem']),
        'conv_b': jnp.stack(g['conv_b']),
        'g_v': jnp.concatenate(g['g_v']),
        'conv_w': jnp.stack(g['conv_w']).transpose(0, 2, 1, 3),
    }
    small_names = REPLICATED + SMALL_SHARDED
    flat = jnp.concatenate([small[k].reshape(-1).astype(F32) for k in small_names])
    n_small = flat.shape[0]
    padded = -(-n_small // SMALL_ALIGN) * SMALL_ALIGN
    flat = jnp.pad(flat, (0, padded - n_small)).reshape(N_DEV, -1, 128)
    (small_parts,) = _exchange([flat], name="scatter_small", scatter=True)
    reduced = _sum_parts(small_parts, name="sum_small")
    (small_all,) = _exchange([reduced], name="gather_small", scatter=False)
    small_all = small_all.reshape(-1)
    grads_small, off = {}, 0
    for k in small_names:
        size = small[k].size
        grads_small[k] = small_all[off:off + size].reshape(small[k].shape)
        off += size
    grads_small['g_v'] = lax.dynamic_slice_in_dim(grads_small['g_v'], my_id * p['g_v'].shape[1], p['g_v'].shape[1], axis=1)
    grads_small['conv_w'] = lax.dynamic_index_in_dim(grads_small['conv_w'], my_id, axis=2, keepdims=False)
    gs = jnp.concatenate([grads_small[k].reshape(-1) for k in small_names])
    n_loc = gs.shape[0]
    pad_loc = -(-n_loc // 1024) * 1024 - n_loc

    def pack(prefix):
        t = jnp.concatenate([p[prefix + k].reshape(-1) for k in small_names])
        return jnp.pad(t, (0, pad_loc)).reshape(1, -1, 128)

    res = _sum_adam(jnp.pad(gs, (0, pad_loc)).reshape(1, -1, 128), pack(''), pack('m_'), pack('v_'), 0, None,
                    name="adam_small")
    out, off = {}, 0
    for k in small_names:
        size = p[k].size
        out[k] = [t.reshape(-1)[off:off + size].reshape(p[k].shape) for t in res]
        off += size

    running = {}
    for group, keys, bufs, sems, lands, srcs in pending:
        landed = _exchange_wait(srcs, bufs, sems, lands, None, name=f"scatter_wait_{group[0]}{group[1]}", scatter=True)
        for k, parts in zip(keys, landed):
            stacked = k in LAYERED
            nl = p[k].shape[0] if stacked else 1
            layer = group[1] - LAYERED[k] if stacked else 0
            rows = p[k].size // nl // p[k].shape[-1]
            view = (nl, rows, p[k].shape[-1])
            running[k] = _sum_adam(parts.reshape(N_DEV, rows, view[2]), p[k].reshape(view), p['m_' + k].reshape(view),
                                   p['v_' + k].reshape(view), layer, running.get(k), name=f"adam_{k}{layer}")
    for k, res in running.items():
        out[k] = [t.reshape(p[k].shape) for t in res]

    outs = [loss, grad_x[None]]
    for i in range(4):
        outs += [out[k][i] for k in WEIGHTS]
    return tuple(outs)


def kernel(x, mem, positions, g_mix, g_ffn, g_final, w_in_a, g_v, w_sp, b_sp, g_kv, w_kv_a, g_kv_lat, w_in_b, g_q_lat, w_uq, w_uk, w_uv, g_mem, w_mem_kv, w_out, w_ffn_up, conv_w, conv_b, w_ffn_down, loss_target, m_g_mix, m_g_ffn, m_g_final, m_w_in_a, m_g_v, m_w_sp, m_b_sp, m_g_kv, m_w_kv_a, m_g_kv_lat, m_w_in_b, m_g_q_lat, m_w_uq, m_w_uk, m_w_uv, m_g_mem, m_w_mem_kv, m_w_out, m_w_ffn_up, m_conv_w, m_conv_b, m_w_ffn_down, v_g_mix, v_g_ffn, v_g_final, v_w_in_a, v_g_v, v_w_sp, v_b_sp, v_g_kv, v_w_kv_a, v_g_kv_lat, v_w_in_b, v_g_q_lat, v_w_uq, v_w_uk, v_w_uv, v_g_mem, v_w_mem_kv, v_w_out, v_w_ffn_up, v_conv_w, v_conv_b, v_w_ffn_down):
    return _step((x, mem, positions, g_mix, g_ffn, g_final, w_in_a, g_v, w_sp, b_sp, g_kv, w_kv_a, g_kv_lat, w_in_b, g_q_lat, w_uq, w_uk, w_uv, g_mem, w_mem_kv, w_out, w_ffn_up, conv_w, conv_b, w_ffn_down, loss_target, m_g_mix, m_g_ffn, m_g_final, m_w_in_a, m_g_v, m_w_sp, m_b_sp, m_g_kv, m_w_kv_a, m_g_kv_lat, m_w_in_b, m_g_q_lat, m_w_uq, m_w_uk, m_w_uv, m_g_mem, m_w_mem_kv, m_w_out, m_w_ffn_up, m_conv_w, m_conv_b, m_w_ffn_down, v_g_mix, v_g_ffn, v_g_final, v_w_in_a, v_g_v, v_w_sp, v_b_sp, v_g_kv, v_w_kv_a, v_g_kv_lat, v_w_in_b, v_g_q_lat, v_w_uq, v_w_uk, v_w_uv, v_g_mem, v_w_mem_kv, v_w_out, v_w_ffn_up, v_conv_w, v_conv_b, v_w_ffn_down))
```

```python
import math

import jax
import jax.numpy as jnp
from jax import lax
from jax.experimental import pallas as pl
from jax.experimental.pallas import tpu as pltpu

F32 = jnp.float32
BF16 = jnp.bfloat16

N_DEV = 8
N_A = 2
DEPTH = 4
G_HEADS = 12
HEAD = 128
CHUNK = 128
MEM_HEADS = 4
MEM_W = MEM_HEADS * HEAD
G_W = G_HEADS * HEAD
ROPE_DIM = 64
ROPE_HALF = ROPE_DIM // 2
KV_RANK = 512
Q_RANK = 512
KV_PAD = 640
ROPE_THETA = 10000.0
EPS = 1e-6
CONV_W = 3

ADAM_LR = 0.001
ADAM_B1 = 0.9
ADAM_B2 = 0.999
ADAM_EPS = 1e-08
ADAM_WD = 0.01
ADAM_STEP = 10

VMEM_LIMIT_V7X = 56 * 1024 * 1024
MASK_VALUE = -1e30

WEIGHTS = ['g_mix', 'g_ffn', 'g_final', 'w_in_a', 'g_v', 'w_sp', 'b_sp', 'g_kv', 'w_kv_a', 'g_kv_lat',
           'w_in_b', 'g_q_lat', 'w_uq', 'w_uk', 'w_uv', 'g_mem', 'w_mem_kv', 'w_out', 'w_ffn_up',
           'conv_w', 'conv_b', 'w_ffn_down']
REPLICATED = ['g_mix', 'g_ffn', 'g_final', 'w_sp', 'b_sp', 'g_kv', 'g_kv_lat', 'g_q_lat', 'g_mem', 'conv_b']
SMALL_SHARDED = ['g_v', 'conv_w']


def _params(sem=None):
    return pltpu.CompilerParams(dimension_semantics=sem, vmem_limit_bytes=VMEM_LIMIT_V7X)


def _dot(a, b, dims):
    contract = {'nn': ((1,), (0,)), 'nt': ((1,), (1,)), 'tn': ((0,), (0,))}[dims]
    return lax.dot_general(a, b, (contract, ((), ())), preferred_element_type=F32)


def _erf(x):
    return lax.erf(x)


def _gelu(x):
    return 0.5 * x * (1.0 + _erf(x * (2.0 ** -0.5)))


def _gelu_grad(x):
    cdf = 0.5 * (1.0 + _erf(x * (2.0 ** -0.5)))
    pdf = jnp.exp(-0.5 * x * x) * (1.0 / math.sqrt(2.0 * math.pi))
    return cdf + x * pdf


def _sigmoid(x):
    return 1.0 / (1.0 + jnp.exp(-x))


def _operand_spec(shape, lead, blocked, tr, tc, ridx, cidx):
    if blocked:
        per = shape[-1] // tc
        assert shape[-1] % tc == 0, (shape, tc)
        return pl.BlockSpec(
            (None,) * (1 + len(lead)) + (tr, tc),
            lambda *g: (cidx(*g) // per,) + lead + (ridx(*g), cidx(*g) % per))
    return pl.BlockSpec((None,) * len(lead) + (tr, tc), lambda *g: lead + (ridx(*g), cidx(*g)))


def _view2d(x, blocked):
    return (x.shape[-2], x.shape[0] * x.shape[-1]) if blocked else (x.shape[-2], x.shape[-1])


def _mm(a, b, *, dims, out_dtype, name, tm, tn, tk=None, res=None, a_lead=(), b_lead=(),
        a_blocked=False, b_blocked=False, out_block=None, n_outer=False):
    ar, ac = _view2d(a, a_blocked)
    br, bc = _view2d(b, b_blocked)
    m, k = (ac, ar) if dims == 'tn' else (ar, ac)
    n, k2 = (br, bc) if dims == 'nt' else (bc, br)
    assert k == k2, (a.shape, b.shape, dims)
    tm, tn = min(tm, m), min(tn, n)
    tk = k if tk is None else tk
    assert m % tm == 0 and n % tn == 0 and k % tk == 0, (name, m, n, k, tm, tn, tk)
    nk = k // tk
    if n_outer:
        gi, gj = (lambda g0, g1, g2: g1), (lambda g0, g1, g2: g0)
        grid = (n // tn, m // tm, nk)
    else:
        gi, gj = (lambda g0, g1, g2: g0), (lambda g0, g1, g2: g1)
        grid = (m // tm, n // tn, nk)
    gk = lambda g0, g1, g2: g2

    if dims == 'tn':
        a_spec = _operand_spec(a.shape, a_lead, a_blocked, tk, tm, gk, gi)
    else:
        a_spec = _operand_spec(a.shape, a_lead, a_blocked, tm, tk, gi, gk)
    if dims == 'nt':
        b_spec = _operand_spec(b.shape, b_lead, b_blocked, tn, tk, gj, gk)
    else:
        b_spec = _operand_spec(b.shape, b_lead, b_blocked, tk, tn, gk, gj)
    in_specs = [a_spec, b_spec]
    operands = [a, b]
    if res is not None:
        in_specs.append(pl.BlockSpec((tm, tn), lambda *g: (gi(*g), gj(*g))))
        operands.append(res)
    if out_block is not None:
        out_shape = jax.ShapeDtypeStruct((n // out_block, m, out_block), out_dtype)
        out_spec = _operand_spec(out_shape.shape, (), True, tm, tn, gi, gj)
    else:
        out_shape = jax.ShapeDtypeStruct((m, n), out_dtype)
        out_spec = pl.BlockSpec((tm, tn), lambda *g: (gi(*g), gj(*g)))

    def body(*refs):
        a_ref, b_ref = refs[0], refs[1]
        r_ref = refs[2] if res is not None else None
        o_ref = refs[3] if res is not None else refs[2]
        acc_ref = refs[-1] if nk > 1 else None
        part = _dot(a_ref[...].astype(BF16), b_ref[...].astype(BF16), dims)

        def finish(total):
            if r_ref is not None:
                total = total + r_ref[...]
            o_ref[...] = total.astype(o_ref.dtype)

        if nk == 1:
            finish(part)
        else:
            kk = pl.program_id(2)

            @pl.when(kk == 0)
            def _():
                acc_ref[...] = part

            @pl.when(kk > 0)
            def _():
                acc_ref[...] += part

            @pl.when(kk == nk - 1)
            def _():
                finish(acc_ref[...])

    scratch = [pltpu.VMEM((tm, tn), F32)] if nk > 1 else []
    return pl.pallas_call(
        body, name=name, grid=grid, in_specs=in_specs, out_specs=out_spec,
        out_shape=out_shape, scratch_shapes=scratch,
        compiler_params=_params(("parallel", "parallel", "arbitrary")),
    )(*operands)


def _mm_heads(a, b, *, mode, name, out_dtype=BF16, tm=512):
    if mode in ('to_lat', 'from_lat'):
        r = b.shape[0]
        s = a.shape[0] if mode == 'to_lat' else a.shape[1]
        tm = min(tm, s)
        flat = pl.BlockSpec((tm, G_W), lambda i: (i, 0))
        per_head = pl.BlockSpec((G_HEADS, tm, r), lambda i: (0, i, 0))

        def all_heads(a_ref, b_ref, o_ref):
            for h in range(G_HEADS):
                cols = slice(h * HEAD, (h + 1) * HEAD)
                if mode == 'to_lat':
                    o_ref[h] = _dot(a_ref[:, cols].astype(BF16), b_ref[:, cols].astype(BF16), 'nt').astype(o_ref.dtype)
                else:
                    o_ref[:, cols] = _dot(a_ref[h].astype(BF16), b_ref[:, cols].astype(BF16), 'nn').astype(o_ref.dtype)

        return pl.pallas_call(
            all_heads, name=name, grid=(s // tm,),
            in_specs=[flat if mode == 'to_lat' else per_head, pl.BlockSpec((r, G_W), lambda i: (0, 0))],
            out_specs=per_head if mode == 'to_lat' else flat,
            out_shape=jax.ShapeDtypeStruct((G_HEADS, s, r) if mode == 'to_lat' else (s, G_W), out_dtype),
            compiler_params=_params(("parallel",)),
        )(a, b)
    else:
        _, s, r = a.shape
        grid = (G_HEADS, 1)
        in_specs = [pl.BlockSpec((None, s, r), lambda h, i: (h, 0, 0)),
                    pl.BlockSpec((s, HEAD), lambda h, i: (0, h))]
        out_spec = pl.BlockSpec((r, HEAD), lambda h, i: (0, h))
        out_shape = jax.ShapeDtypeStruct((r, G_W), out_dtype)
        dims = 'tn'

    def body(a_ref, b_ref, o_ref):
        o_ref[...] = _dot(a_ref[...].astype(BF16), b_ref[...].astype(BF16), dims).astype(o_ref.dtype)

    return pl.pallas_call(
        body, name=name, grid=grid, in_specs=in_specs, out_specs=out_spec, out_shape=out_shape,
        compiler_params=_params(("parallel", "parallel")),
    )(a, b)


def _rmsnorm(x, g, *, name, width=None, out_dtype=BF16, tm=512):
    s = x.shape[0]
    w = x.shape[1] if width is None else width
    tm = min(tm, s)

    def body(x_ref, g_ref, o_ref):
        xv = x_ref[...].astype(F32)
        rstd = lax.rsqrt(jnp.mean(xv * xv, axis=-1, keepdims=True) + EPS)
        o_ref[...] = (xv * rstd * g_ref[...]).astype(o_ref.dtype)

    return pl.pallas_call(
        body, name=name, grid=(s // tm,),
        in_specs=[pl.BlockSpec((tm, w), lambda i: (i, 0)), pl.BlockSpec((1, w), lambda i: (0, 0))],
        out_specs=pl.BlockSpec((tm, w), lambda i: (i, 0)),
        out_shape=jax.ShapeDtypeStruct((s, w), out_dtype),
        compiler_params=_params(("parallel",)),
    )(x, g.reshape(1, w))


def _rmsnorm_bwd(x, g, dy, *, name, width=None, dres=None, after=None, out_dtype=F32, tm=512):
    s = x.shape[0]
    w = x.shape[1] if width is None else width
    tm = min(tm, s)

    def body(*refs):
        x_ref, g_ref, dy_ref = refs[0], refs[1], refs[2]
        r_ref = refs[3] if dres is not None else None
        dx_ref, dg_ref = refs[-2], refs[-1]
        xv = x_ref[...].astype(F32)
        rstd = lax.rsqrt(jnp.mean(xv * xv, axis=-1, keepdims=True) + EPS)
        xhat = xv * rstd
        dyv = dy_ref[...].astype(F32)
        gdy = dyv * g_ref[...]
        dx = rstd * (gdy - xhat * jnp.mean(gdy * xhat, axis=-1, keepdims=True))
        if r_ref is not None:
            dx = dx + r_ref[...]
        dx_ref[...] = dx.astype(dx_ref.dtype)
        part = jnp.sum(dyv * xhat, axis=0, keepdims=True)

        @pl.when(pl.program_id(0) == 0)
        def _():
            dg_ref[...] = part

        @pl.when(pl.program_id(0) > 0)
        def _():
            dg_ref[...] += part

    row = pl.BlockSpec((tm, w), lambda i: (i, 0))
    vec = pl.BlockSpec((1, w), lambda i: (0, 0))
    in_specs = [row, vec, row] + ([row] if dres is not None else [])
    operands = [x, g.reshape(1, w), dy] + ([dres] if dres is not None else [])
    if after is not None:
        in_specs.append(pl.BlockSpec(memory_space=pl.ANY))
        operands.append(after)
    return pl.pallas_call(
        body, name=name, grid=(s // tm,), in_specs=in_specs, out_specs=[row, vec],
        out_shape=[jax.ShapeDtypeStruct((s, w), out_dtype), jax.ShapeDtypeStruct((1, w), F32)],
        compiler_params=_params(("arbitrary",)),
    )(*operands)


def _final_loss(x, target, g, *, name, tm=256):
    s, d = x.shape
    tm = min(tm, s)

    def body(x_ref, t_ref, g_ref, sq_ref, dx_ref, dg_ref):
        xv = x_ref[...]
        rstd = lax.rsqrt(jnp.mean(xv * xv, axis=-1, keepdims=True) + EPS)
        xhat = xv * rstd
        err = xhat * g_ref[...] - t_ref[...]
        dyv = err * (1.0 / d)
        gdy = dyv * g_ref[...]
        dx_ref[...] = rstd * (gdy - xhat * jnp.mean(gdy * xhat, axis=-1, keepdims=True))
        sq = jnp.sum(err * err, axis=0, keepdims=True)
        dg = jnp.sum(dyv * xhat, axis=0, keepdims=True)

        @pl.when(pl.program_id(0) == 0)
        def _():
            sq_ref[...] = sq
            dg_ref[...] = dg

        @pl.when(pl.program_id(0) > 0)
        def _():
            sq_ref[...] += sq
            dg_ref[...] += dg

    row = pl.BlockSpec((tm, d), lambda i: (i, 0))
    vec = pl.BlockSpec((1, d), lambda i: (0, 0))
    return pl.pallas_call(
        body, name=name, grid=(s // tm,), in_specs=[row, row, vec], out_specs=[vec, row, vec],
        out_shape=[jax.ShapeDtypeStruct((1, d), F32), jax.ShapeDtypeStruct((s, d), F32),
                   jax.ShapeDtypeStruct((1, d), F32)],
        compiler_params=_params(("arbitrary",)),
    )(x, target, g.reshape(1, d))


def _tril_mask():
    t = lax.broadcasted_iota(jnp.int32, (CHUNK, CHUNK), 0)
    s = lax.broadcasted_iota(jnp.int32, (CHUNK, CHUNK), 1)
    return t >= s


def _sgu_fwd(z, g_v, w_sp, b_sp_t, *, name):
    s = z.shape[0]

    def body(zu_ref, zv_ref, g_ref, w_ref, b_ref, o_ref):
        u = _gelu(zu_ref[...].astype(F32))
        gv = _gelu(zv_ref[...].astype(F32))
        rstd = lax.rsqrt(jnp.mean(gv * gv, axis=-1, keepdims=True) + EPS)
        v = (gv * rstd * g_ref[...]).astype(BF16)
        mask = _tril_mask()
        for grp in range(G_HEADS):
            cols = slice(grp * HEAD, (grp + 1) * HEAD)
            wm = jnp.where(mask, w_ref[grp], 0.0).astype(BF16)
            sv = _dot(wm, v[:, cols], 'nn') + b_ref[:, grp:grp + 1]
            o_ref[:, cols] = (u[:, cols] * sv).astype(o_ref.dtype)

    return pl.pallas_call(
        body, name=name, grid=(s // CHUNK,),
        in_specs=[pl.BlockSpec((CHUNK, G_W), lambda i: (i, 0)),
                  pl.BlockSpec((CHUNK, G_W), lambda i: (i, 1)),
                  pl.BlockSpec((1, G_W), lambda i: (0, 0)),
                  pl.BlockSpec((G_HEADS, CHUNK, CHUNK), lambda i: (0, 0, 0)),
                  pl.BlockSpec((CHUNK, G_HEADS), lambda i: (0, 0))],
        out_specs=pl.BlockSpec((CHUNK, G_W), lambda i: (i, 0)),
        out_shape=jax.ShapeDtypeStruct((s, G_W), BF16),
        compiler_params=_params(("parallel",)),
    )(z, z, g_v.reshape(1, G_W), w_sp, b_sp_t)


def _sgu_bwd(z, dmix, dqm, g_v, w_sp, b_sp_t, *, name):
    s = z.shape[0]
    zw = z.shape[1]

    def body(zu_ref, zv_ref, dm_ref, dq_ref, g_ref, w_ref, b_ref, dz_ref, dw_ref, db_ref, dg_ref):
        first = pl.program_id(0) == 0

        @pl.when(first)
        def _():
            dw_ref[...] = jnp.zeros_like(dw_ref)
            db_ref[...] = jnp.zeros_like(db_ref)
            dg_ref[...] = jnp.zeros_like(dg_ref)

        zu = zu_ref[...].astype(F32)
        zv = zv_ref[...].astype(F32)
        dmain = dm_ref[...].astype(F32)
        u = _gelu(zu)
        gv = _gelu(zv)
        rstd = lax.rsqrt(jnp.mean(gv * gv, axis=-1, keepdims=True) + EPS)
        vhat = gv * rstd
        gvec = g_ref[...]
        v = (vhat * gvec).astype(BF16)
        dsv = dmain * u
        dsv_b = dsv.astype(BF16)
        mask = _tril_mask()
        dv_parts = []
        for grp in range(G_HEADS):
            cols = slice(grp * HEAD, (grp + 1) * HEAD)
            wm = jnp.where(mask, w_ref[grp], 0.0).astype(BF16)
            sv = _dot(wm, v[:, cols], 'nn') + b_ref[:, grp:grp + 1]
            dz_ref[:, cols] = (dmain[:, cols] * sv * _gelu_grad(zu[:, cols])).astype(dz_ref.dtype)
            dwg = _dot(dsv_b[:, cols], v[:, cols], 'nt')
            dw_ref[grp] += jnp.where(mask, dwg, 0.0)
            db_ref[:, grp:grp + 1] += jnp.sum(dsv[:, cols], axis=-1, keepdims=True)
            dv_parts.append(_dot(wm, dsv_b[:, cols], 'tn'))
        dv = jnp.concatenate(dv_parts, axis=-1)
        dg_ref[...] += jnp.sum(dv * vhat, axis=0, keepdims=True)
        gdv = dv * gvec
        dgv = rstd * (gdv - vhat * jnp.mean(gdv * vhat, axis=-1, keepdims=True))
        dz_ref[:, G_W:2 * G_W] = (dgv * _gelu_grad(zv)).astype(dz_ref.dtype)
        dz_ref[:, 2 * G_W:] = dq_ref[...].astype(dz_ref.dtype)

    return pl.pallas_call(
        body, name=name, grid=(s // CHUNK,),
        in_specs=[pl.BlockSpec((CHUNK, G_W), lambda i: (i, 0)),
                  pl.BlockSpec((CHUNK, G_W), lambda i: (i, 1)),
                  pl.BlockSpec((CHUNK, G_W), lambda i: (i, 0)),
                  pl.BlockSpec((CHUNK, MEM_W), lambda i: (i, 0)),
                  pl.BlockSpec((1, G_W), lambda i: (0, 0)),
                  pl.BlockSpec((G_HEADS, CHUNK, CHUNK), lambda i: (0, 0, 0)),
                  pl.BlockSpec((CHUNK, G_HEADS), lambda i: (0, 0))],
        out_specs=[pl.BlockSpec((CHUNK, zw), lambda i: (i, 0)),
                   pl.BlockSpec((G_HEADS, CHUNK, CHUNK), lambda i: (0, 0, 0)),
                   pl.BlockSpec((CHUNK, G_HEADS), lambda i: (0, 0)),
                   pl.BlockSpec((1, G_W), lambda i: (0, 0))],
        out_shape=[jax.ShapeDtypeStruct((s, zw), BF16),
                   jax.ShapeDtypeStruct((G_HEADS, CHUNK, CHUNK), F32),
                   jax.ShapeDtypeStruct((CHUNK, G_HEADS), F32),
                   jax.ShapeDtypeStruct((1, G_W), F32)],
        compiler_params=_params(("arbitrary",)),
    )(z, z, dmix, dqm, g_v.reshape(1, G_W), w_sp, b_sp_t)


def _mem_probs(q, k):
    sc = _dot(q, k, 'nt') * (HEAD ** -0.5)
    sc = sc - jnp.max(sc, axis=-1, keepdims=True)
    e = jnp.exp(sc)
    return e / jnp.sum(e, axis=-1, keepdims=True)


def _memattn_fwd(z, kvm, main, *, qcol, name, tm=512):
    s = z.shape[0]
    m = kvm.shape[0]
    tm = min(tm, s)

    def body(q_ref, kv_ref, main_ref, o_ref):
        o_ref[:, :G_W] = main_ref[...]
        for h in range(MEM_HEADS):
            cols = slice(h * HEAD, (h + 1) * HEAD)
            k = kv_ref[:, cols]
            v = kv_ref[:, MEM_W + h * HEAD:MEM_W + (h + 1) * HEAD]
            p = _mem_probs(q_ref[:, cols], k)
            o_ref[:, G_W + h * HEAD:G_W + (h + 1) * HEAD] = _dot(p.astype(BF16), v, 'nn').astype(o_ref.dtype)

    return pl.pallas_call(
        body, name=name, grid=(s // tm,),
        in_specs=[pl.BlockSpec((tm, MEM_W), lambda i: (i, qcol)),
                  pl.BlockSpec((m, 2 * MEM_W), lambda i: (0, 0)),
                  pl.BlockSpec((tm, G_W), lambda i: (i, 0))],
        out_specs=pl.BlockSpec((tm, G_W + MEM_W), lambda i: (i, 0)),
        out_shape=jax.ShapeDtypeStruct((s, G_W + MEM_W), BF16),
        compiler_params=_params(("parallel",)),
    )(z, kvm, main)


def _memattn_bwd(z, kvm, dmix, *, qcol, name, tm=512):
    s = z.shape[0]
    m = kvm.shape[0]
    tm = min(tm, s)
    scale = HEAD ** -0.5

    def body(q_ref, kv_ref, do_ref, dq_ref, dkv_ref):
        @pl.when(pl.program_id(0) == 0)
        def _():
            dkv_ref[...] = jnp.zeros_like(dkv_ref)

        for h in range(MEM_HEADS):
            cols = slice(h * HEAD, (h + 1) * HEAD)
            vcols = slice(MEM_W + h * HEAD, MEM_W + (h + 1) * HEAD)
            q = q_ref[:, cols]
            k = kv_ref[:, cols]
            v = kv_ref[:, vcols]
            do = do_ref[:, cols]
            p = _mem_probs(q, k)
            dp = _dot(do, v, 'nt')
            ds = (p * (dp - jnp.sum(dp * p, axis=-1, keepdims=True)) * scale).astype(BF16)
            dq_ref[:, cols] = _dot(ds, k, 'nn').astype(dq_ref.dtype)
            dkv_ref[:, cols] += _dot(ds, q, 'tn')
            dkv_ref[:, vcols] += _dot(p.astype(BF16), do, 'tn')

    mo_block = G_W // MEM_W
    return pl.pallas_call(
        body, name=name, grid=(s // tm,),
        in_specs=[pl.BlockSpec((tm, MEM_W), lambda i: (i, qcol)),
                  pl.BlockSpec((m, 2 * MEM_W), lambda i: (0, 0)),
                  pl.BlockSpec((tm, MEM_W), lambda i: (i, mo_block))],
        out_specs=[pl.BlockSpec((tm, MEM_W), lambda i: (i, 0)),
                   pl.BlockSpec((m, 2 * MEM_W), lambda i: (0, 0))],
        out_shape=[jax.ShapeDtypeStruct((s, MEM_W), BF16), jax.ShapeDtypeStruct((m, 2 * MEM_W), F32)],
        compiler_params=_params(("arbitrary",)),
    )(z, kvm, dmix)


def _rope(x1, x2, cos, sin, *, name, inverse=False, out_dtype=BF16, col1=0, col2=0, tm=512):
    s, w = cos.shape
    tm = min(tm, s)
    sign = -1.0 if inverse else 1.0

    def body(a_ref, b_ref, c_ref, s_ref, o1_ref, o2_ref):
        a = a_ref[...].astype(F32)
        b = b_ref[...].astype(F32)
        c = c_ref[...]
        sn = s_ref[...] * sign
        o1_ref[...] = (a * c - b * sn).astype(o1_ref.dtype)
        o2_ref[...] = (b * c + a * sn).astype(o2_ref.dtype)

    row = pl.BlockSpec((tm, w), lambda i: (i, 0))
    return pl.pallas_call(
        body, name=name, grid=(s // tm,),
        in_specs=[pl.BlockSpec((tm, w), lambda i: (i, col1)), pl.BlockSpec((tm, w), lambda i: (i, col2)), row, row],
        out_specs=[row, row],
        out_shape=[jax.ShapeDtypeStruct((s, w), out_dtype)] * 2,
        compiler_params=_params(("parallel",)),
    )(x1, x2, cos, sin)


MLA_CHAINS = 2


def _mla_scores(q1, q2, k1, k2, row_tok, kstart, tk, scale):
    sc = (_dot(q1, k1, 'nt') + _dot(q2, k2, 'nt')) * scale
    kpos = kstart + lax.broadcasted_iota(jnp.int32, (1, tk), 1)
    keep = kpos <= row_tok
    return sc, keep


def _mla_fwd(qa, qr, ckv, kr, *, name, tk=512):
    hh, s, r = qa.shape
    tq = CHUNK
    tk = min(tk, s)
    rows = hh * tq
    chains = 1
    hc = hh // chains
    rc = hc * tq
    scale = (HEAD + ROPE_DIM) ** -0.5

    def body(qa_ref, qr_ref, ckv_ref, kr_ref, o_ref, lse_ref, m_ref, l_ref, acc_ref):
        i = pl.program_id(0)
        row_tok = i * tq + (lax.broadcasted_iota(jnp.int32, (rc, 1), 0) & (tq - 1))
        m_ref[...] = jnp.full_like(m_ref, MASK_VALUE)
        l_ref[...] = jnp.zeros_like(l_ref)
        acc_ref[...] = jnp.zeros_like(acc_ref)

        def step(j, carry):
            kstart = pl.multiple_of(j * tk, tk)
            k1 = ckv_ref[pl.ds(kstart, tk), :]
            k2 = kr_ref[pl.ds(kstart, tk), :]
            for c in range(chains):
                hs, rs = slice(c * hc, (c + 1) * hc), slice(c * rc, (c + 1) * rc)
                q1 = qa_ref[hs].reshape(rc, r)
                q2 = qr_ref[hs].reshape(rc, ROPE_DIM)
                sc, keep = _mla_scores(q1, q2, k1, k2, row_tok, kstart, tk, scale)
                sc = jnp.where(keep, sc, MASK_VALUE)
                m_old = m_ref[rs, :]
                m_new = jnp.maximum(m_old, jnp.max(sc, axis=-1, keepdims=True))
                p = jnp.exp(sc - m_new)
                alpha = jnp.exp(m_old - m_new)
                l_ref[rs, :] = alpha * l_ref[rs, :] + jnp.sum(p, axis=-1, keepdims=True)
                acc_ref[rs, :] = alpha * acc_ref[rs, :] + _dot(p.astype(BF16), k1, 'nn')
                m_ref[rs, :] = m_new
            return carry

        lax.fori_loop(0, (i * tq) // tk + 1, step, 0)
        l = l_ref[...]
        o_ref[...] = (acc_ref[...] / l).astype(o_ref.dtype).reshape(hh, tq, r)
        lse_ref[...] = (m_ref[...] + jnp.log(l)).reshape(hh, tq, 1)

    return pl.pallas_call(
        body, name=name, grid=(s // tq,),
        in_specs=[pl.BlockSpec((hh, tq, r), lambda i: (0, i, 0)),
                  pl.BlockSpec((hh, tq, ROPE_DIM), lambda i: (0, i, 0)),
                  pl.BlockSpec((s, r), lambda i: (0, 0)),
                  pl.BlockSpec((s, ROPE_DIM), lambda i: (0, 0))],
        out_specs=[pl.BlockSpec((hh, tq, r), lambda i: (0, i, 0)),
                   pl.BlockSpec((hh, tq, 1), lambda i: (0, i, 0))],
        out_shape=[jax.ShapeDtypeStruct((hh, s, r), BF16), jax.ShapeDtypeStruct((hh, s, 1), F32)],
        scratch_shapes=[pltpu.VMEM((rows, 1), F32), pltpu.VMEM((rows, 1), F32), pltpu.VMEM((rows, r), F32)],
        compiler_params=_params(("parallel",)),
    )(qa, qr, ckv, kr)


def _mla_bwd(qa, qr, ckv, kr, o, do, lse, *, name, tk=256):
    hh, s, r = qa.shape
    tq = CHUNK
    tk = min(tk, s)
    rows = hh * tq
    hc = hh // MLA_CHAINS
    rc = hc * tq
    nq = s // tq
    scale = (HEAD + ROPE_DIM) ** -0.5

    def body(qa_ref, qr_ref, o_ref, do_ref, lse_ref, ckv_hbm, kr_hbm, dqa_ref, dqr_ref, dckv_hbm, dkr_hbm,
             ckv_ref, kr_ref, dckv_ref, dkr_ref, dq1_ref, dq2_ref, delta_ref, sem):
        i = pl.program_id(0)

        @pl.when(i == 0)
        def _():
            c1 = pltpu.make_async_copy(ckv_hbm, ckv_ref, sem.at[0])
            c2 = pltpu.make_async_copy(kr_hbm, kr_ref, sem.at[1])
            c1.start()
            c2.start()
            dckv_ref[...] = jnp.zeros_like(dckv_ref)
            dkr_ref[...] = jnp.zeros_like(dkr_ref)
            c1.wait()
            c2.wait()

        delta_ref[...] = jnp.sum(do_ref[...].reshape(rows, r).astype(F32) * o_ref[...].reshape(rows, r).astype(F32),
                                 axis=-1, keepdims=True)
        row_tok = i * tq + (lax.broadcasted_iota(jnp.int32, (rc, 1), 0) & (tq - 1))
        dq1_ref[...] = jnp.zeros_like(dq1_ref)
        dq2_ref[...] = jnp.zeros_like(dq2_ref)

        def step(j, carry):
            kstart = pl.multiple_of(j * tk, tk)
            k1 = ckv_ref[pl.ds(kstart, tk), :]
            k2 = kr_ref[pl.ds(kstart, tk), :]
            dk1, dk2 = None, None
            for c in range(MLA_CHAINS):
                hs, rs = slice(c * hc, (c + 1) * hc), slice(c * rc, (c + 1) * rc)
                q1 = qa_ref[hs].reshape(rc, r)
                q2 = qr_ref[hs].reshape(rc, ROPE_DIM)
                dov = do_ref[hs].reshape(rc, r)
                sc, keep = _mla_scores(q1, q2, k1, k2, row_tok, kstart, tk, scale)
                p = jnp.where(keep, jnp.exp(sc - lse_ref[hs].reshape(rc, 1)), 0.0)
                dp = _dot(dov, k1, 'nt')
                ds = (p * (dp - delta_ref[rs, :]) * scale).astype(BF16)
                pb = p.astype(BF16)
                dq1_ref[rs, :] += _dot(ds, k1, 'nn')
                dq2_ref[rs, :] += _dot(ds, k2, 'nn')
                part1 = _dot(ds, q1, 'tn') + _dot(pb, dov, 'tn')
                part2 = _dot(ds, q2, 'tn')
                dk1 = part1 if dk1 is None else dk1 + part1
                dk2 = part2 if dk2 is None else dk2 + part2
            dckv_ref[pl.ds(kstart, tk), :] += dk1
            dkr_ref[pl.ds(kstart, tk), :] += dk2
            return carry

        lax.fori_loop(0, (i * tq) // tk + 1, step, 0)
        dqa_ref[...] = dq1_ref[...].astype(dqa_ref.dtype).reshape(hh, tq, r)
        dqr_ref[...] = dq2_ref[...].astype(dqr_ref.dtype).reshape(hh, tq, ROPE_DIM)

        @pl.when(i == nq - 1)
        def _():
            c1 = pltpu.make_async_copy(dckv_ref, dckv_hbm, sem.at[0])
            c2 = pltpu.make_async_copy(dkr_ref, dkr_hbm, sem.at[1])
            c1.start()
            c2.start()
            c1.wait()
            c2.wait()

    blk = pl.BlockSpec((hh, tq, r), lambda i: (0, i, 0))
    blk_r = pl.BlockSpec((hh, tq, ROPE_DIM), lambda i: (0, i, 0))
    any_spec = pl.BlockSpec(memory_space=pl.ANY)
    return pl.pallas_call(
        body, name=name, grid=(nq,),
        in_specs=[blk, blk_r, blk, blk, pl.BlockSpec((hh, tq, 1), lambda i: (0, i, 0)), any_spec, any_spec],
        out_specs=[blk, blk_r, any_spec, any_spec],
        out_shape=[jax.ShapeDtypeStruct((hh, s, r), BF16), jax.ShapeDtypeStruct((hh, s, ROPE_DIM), BF16),
                   jax.ShapeDtypeStruct((s, r), F32), jax.ShapeDtypeStruct((s, ROPE_DIM), F32)],
        scratch_shapes=[pltpu.VMEM((s, r), BF16), pltpu.VMEM((s, ROPE_DIM), BF16),
                        pltpu.VMEM((s, r), F32), pltpu.VMEM((s, ROPE_DIM), F32),
                        pltpu.VMEM((rows, r), F32), pltpu.VMEM((rows, ROPE_DIM), F32), pltpu.VMEM((rows, 1), F32),
                        pltpu.SemaphoreType.DMA((2,))],
        compiler_params=_params(("arbitrary",)),
    )(qa, qr, o, do, lse, ckv, kr)


HALO = 16


def _shift_down(prev, cur, shift, first_tile):
    tr = cur.shape[0]
    full = jnp.concatenate([prev, cur], axis=0)
    out = pltpu.roll(full, shift, axis=0)[HALO:]
    row = lax.broadcasted_iota(jnp.int32, (tr, 1), 0)
    return jnp.where(jnp.logical_and(first_tile, row < shift), 0.0, out)


def _shift_up(cur, nxt, shift, last_tile):
    tr = cur.shape[0]
    full = jnp.concatenate([cur, nxt], axis=0)
    out = pltpu.roll(full, tr + HALO - shift, axis=0)[:tr]
    row = lax.broadcasted_iota(jnp.int32, (tr, 1), 0)
    return jnp.where(jnp.logical_and(last_tile, row >= tr - shift), 0.0, out)


def _lane_chunks(width, lanes):
    return [slice(c0, min(c0 + lanes, width)) for c0 in range(0, width, lanes)]


def _conv_taps(prev_ref, cur_ref, cw_ref, cb_ref, first_tile, cs):
    cur = cur_ref[:, cs].astype(F32)
    prev = prev_ref[:, cs].astype(F32)
    a1 = _shift_down(prev, cur, 1, first_tile)
    a2 = _shift_down(prev, cur, 2, first_tile)
    c = a2 * cw_ref[0:1, cs] + a1 * cw_ref[1:2, cs] + cur * cw_ref[2:3, cs] + cb_ref[:, cs]
    return c, (a2, a1, cur)


def _conv_in_specs(tr, bw, half, layer, row_of, blk_of):
    per = tr // HALO
    specs = []
    for off in (0, half):
        specs.append(pl.BlockSpec((None, HALO, bw), lambda *g, off=off: (blk_of(*g) + off, jnp.maximum(row_of(*g) * per - 1, 0), 0)))
        specs.append(pl.BlockSpec((None, tr, bw), lambda *g, off=off: (blk_of(*g) + off, row_of(*g), 0)))
    for off in (0, half):
        specs.append(pl.BlockSpec((None, None, CONV_W, bw), lambda *g, off=off: (blk_of(*g) + off, layer, 0, 0)))
    for off in (0, half):
        specs.append(pl.BlockSpec((None, 1, bw), lambda *g, off=off: (layer * 2 * half + blk_of(*g) + off, 0, 0)))
    return specs


def _conv_fwd(a, cw, cb, layer, *, name, tr=256):
    nb, s, bw = a.shape
    half = nb // 2
    tr = min(tr, s)

    def body(gp_ref, gc_ref, vp_ref, vc_ref, cwg_ref, cwv_ref, cbg_ref, cbv_ref, o_ref):
        first = pl.program_id(0) == 0
        for cs in _lane_chunks(bw, 256):
            gate, _ = _conv_taps(gp_ref, gc_ref, cwg_ref, cbg_ref, first, cs)
            val, _ = _conv_taps(vp_ref, vc_ref, cwv_ref, cbv_ref, first, cs)
            o_ref[:, cs] = (gate * _sigmoid(gate) * val).astype(o_ref.dtype)

    return pl.pallas_call(
        body, name=name, grid=(s // tr, half),
        in_specs=_conv_in_specs(tr, bw, half, layer, lambda i, j: i, lambda i, j: j),
        out_specs=pl.BlockSpec((None, tr, bw), lambda i, j: (j, i, 0)),
        out_shape=jax.ShapeDtypeStruct((half, s, bw), BF16),
        compiler_params=_params(("parallel", "parallel")),
    )(a, a, a, a, cw, cw, cb, cb)


def _conv_bwd_dc(a, dact, cw, cb, layer, *, name, after=None, tr=256):
    nb, s, bw = a.shape
    half = nb // 2
    tr = min(tr, s)

    def body(*refs):
        gp_ref, gc_ref, vp_ref, vc_ref, cwg_ref, cwv_ref, cbg_ref, cbv_ref, da_ref = refs[:9]
        dc_ref, dw_ref, db_ref = refs[-3:]
        first = pl.program_id(1) == 0

        @pl.when(first)
        def _():
            dw_ref[...] = jnp.zeros_like(dw_ref)
            db_ref[...] = jnp.zeros_like(db_ref)

        for cs in _lane_chunks(bw, 128):
            gate, gtaps = _conv_taps(gp_ref, gc_ref, cwg_ref, cbg_ref, first, cs)
            val, vtaps = _conv_taps(vp_ref, vc_ref, cwv_ref, cbv_ref, first, cs)
            dact_v = da_ref[:, cs].astype(F32)
            sg = _sigmoid(gate)
            dgate = dact_v * val * (sg * (1.0 + gate * (1.0 - sg)))
            dval = dact_v * (gate * sg)
            dc_ref[0, :, cs] = dgate.astype(dc_ref.dtype)
            dc_ref[1, :, cs] = dval.astype(dc_ref.dtype)
            for kk in range(CONV_W):
                dw_ref[0, kk:kk + 1, cs] += jnp.sum(dgate * gtaps[kk], axis=0, keepdims=True)
                dw_ref[1, kk:kk + 1, cs] += jnp.sum(dval * vtaps[kk], axis=0, keepdims=True)
            db_ref[0, :, cs] += jnp.sum(dgate, axis=0, keepdims=True)
            db_ref[1, :, cs] += jnp.sum(dval, axis=0, keepdims=True)

    outs = pl.pallas_call(
        body, name=name, grid=(half, s // tr),
        in_specs=_conv_in_specs(tr, bw, half, layer, lambda j, i: i, lambda j, i: j)
        + [pl.BlockSpec((None, tr, bw), lambda j, i: (j, i, 0))]
        + ([pl.BlockSpec(memory_space=pl.ANY)] if after is not None else []),
        out_specs=[pl.BlockSpec((2, None, tr, bw), lambda j, i: (0, j, i, 0)),
                   pl.BlockSpec((2, None, CONV_W, bw), lambda j, i: (0, j, 0, 0)),
                   pl.BlockSpec((2, None, 1, bw), lambda j, i: (0, j, 0, 0))],
        out_shape=[jax.ShapeDtypeStruct((2, half, s, bw), BF16),
                   jax.ShapeDtypeStruct((2, half, CONV_W, bw), F32),
                   jax.ShapeDtypeStruct((2, half, 1, bw), F32)],
        compiler_params=_params(("parallel", "arbitrary")),
    )(a, a, a, a, cw, cw, cb, cb, dact, *([after] if after is not None else []))
    dc, dw, db = outs
    return dc.reshape(nb, s, bw), dw.reshape(nb, CONV_W, bw), db.reshape(nb, 1, bw)


def _conv_bwd_da(dc, cw, layer, *, name, tr=512):
    nb, s, bw = dc.shape
    tr = min(tr, s)
    ni = s // tr
    per = tr // HALO
    last_halo = s // HALO - 1

    def body(c_ref, n_ref, w_ref, o_ref):
        last = pl.program_id(0) == ni - 1
        for cs in _lane_chunks(bw, 256):
            cur = c_ref[:, cs].astype(F32)
            nxt = n_ref[:, cs].astype(F32)
            da = (cur * w_ref[2:3, cs] + _shift_up(cur, nxt, 1, last) * w_ref[1:2, cs]
                  + _shift_up(cur, nxt, 2, last) * w_ref[0:1, cs])
            o_ref[:, cs] = da.astype(o_ref.dtype)

    tile = pl.BlockSpec((None, tr, bw), lambda i, j: (j, i, 0))
    return pl.pallas_call(
        body, name=name, grid=(ni, nb),
        in_specs=[tile,
                  pl.BlockSpec((None, HALO, bw), lambda i, j: (j, jnp.minimum((i + 1) * per, last_halo), 0)),
                  pl.BlockSpec((None, None, CONV_W, bw), lambda i, j: (j, layer, 0, 0))],
        out_specs=tile,
        out_shape=jax.ShapeDtypeStruct((nb, s, bw), BF16),
        compiler_params=_params(("parallel", "parallel")),
    )(dc, dc, cw)


def _rope_tables(positions):
    inv = 1.0 / (ROPE_THETA ** (jnp.arange(0, ROPE_DIM, 2, dtype=F32) / ROPE_DIM))
    ang = positions.astype(F32)[:, None] * inv
    return jnp.cos(ang), jnp.sin(ang)


def _heads_to_major(r1, r2):
    s = r1.shape[0]
    both = jnp.concatenate([r1.reshape(s, G_HEADS, ROPE_HALF), r2.reshape(s, G_HEADS, ROPE_HALF)], axis=-1)
    return both.transpose(1, 0, 2)


def _heads_from_major(qr):
    s = qr.shape[1]
    t = qr.transpose(1, 0, 2)
    return t[:, :, :ROPE_HALF].reshape(s, G_HEADS * ROPE_HALF), t[:, :, ROPE_HALF:].reshape(s, G_HEADS * ROPE_HALF)


def _local_step(x, mem, positions, target, rep, fetch, emit):
    s, d = x.shape
    n_b = DEPTH - N_A
    tm = min(1024, s)
    cos, sin = _rope_tables(positions)
    cos12 = jnp.tile(cos, (1, G_HEADS))
    sin12 = jnp.tile(sin, (1, G_HEADS))
    r1_col = G_W // (G_HEADS * ROPE_HALF)
    b_sp_t = rep['b_sp'].transpose(0, 2, 1)

    saved = []
    kv = None
    shared = None
    for l in range(DEPTH):
        wm = fetch(('in', l), x)
        if l == 0:
            shared = {'g_v': wm['g_v'], 'conv_w': wm['conv_w']}
            bw = shared['conv_w'].shape[-1]
            conv_b = rep['conv_b'].reshape(-1, 1, bw)
        sv = {'x_in': x, 'wm': wm}
        if l == N_A:
            xn_kv = _rmsnorm(x, rep['g_kv'], name="kvnorm")
            kvx = _mm(xn_kv, wm['w_kv_a'], dims='nn', out_dtype=F32, name="kvproj", tm=tm, tn=KV_PAD)
            ckv = _rmsnorm(kvx, rep['g_kv_lat'], width=KV_RANK, name="ckvnorm")
            k1, k2 = _rope(kvx[:, KV_RANK:KV_RANK + ROPE_HALF], kvx[:, KV_RANK + ROPE_HALF:KV_RANK + ROPE_DIM],
                           cos, sin, name="krope")
            kr = jnp.concatenate([k1, k2], axis=-1)
            kv = {'x': x, 'xn': xn_kv, 'kvx': kvx, 'ckv': ckv, 'kr': kr, 'w_kv_a': wm['w_kv_a']}
        h = _rmsnorm(x, rep['g_mix'][l], name=f"mixnorm{l}")
        if l < N_A:
            z = _mm(h, wm['w_in'], dims='nn', out_dtype=BF16, name=f"in_a{l}", tm=tm, tn=512)
            main = _sgu_fwd(z, shared['g_v'][l], rep['w_sp'][l], b_sp_t[l], name=f"sgu{l}")
            qcol = 2 * G_W // MEM_W
        else:
            j = l - N_A
            z = _mm(h, wm['w_in'], dims='nn', out_dtype=BF16, name=f"in_b{j}", tm=tm, tn=1024)
            qn = _rmsnorm(z, rep['g_q_lat'][j], width=Q_RANK, name=f"qnorm{j}")
            qp = _mm(qn, wm['w_uqp'], dims='nn', out_dtype=BF16, name=f"uq{j}", tm=tm, tn=768)
            rr1, rr2 = _rope(qp, qp, cos12, sin12, col1=r1_col, col2=r1_col + 1, name=f"qrope{j}")
            qr = _heads_to_major(rr1, rr2)
            qa = _mm_heads(qp, wm['w_uk'], mode='to_lat', name=f"qabsorb{j}")
            o_lat, lse = _mla_fwd(qa, qr, kv['ckv'], kv['kr'], name=f"mla{j}")
            main = _mm_heads(o_lat, wm['w_uv'], mode='from_lat', name=f"uv{j}")
            qcol = Q_RANK // MEM_W
            sv.update(qn=qn, qp=qp, qr=qr, qa=qa, o_lat=o_lat, lse=lse)
        wm.update(fetch(('rest', l), z))
        memn = _rmsnorm(mem, rep['g_mem'][l], name=f"memnorm{l}")
        kvm = _mm(memn, wm['w_mem_kv'], dims='nn', out_dtype=BF16, name=f"memkv{l}", tm=tm, tn=1024)
        mix = _memattn_fwd(z, kvm, main, qcol=qcol, name=f"memattn{l}")
        x_mid = _mm(mix, wm['w_out'], dims='nn', res=x, out_dtype=F32, name=f"out{l}", tm=tm, tn=1024)
        wf = fetch(('up', l), x_mid)
        h2 = _rmsnorm(x_mid, rep['g_ffn'][l], name=f"ffnnorm{l}")
        a = _mm(h2, wf['w_up'], dims='nn', b_blocked=True, out_dtype=BF16, out_block=bw,
                name=f"up{l}", tm=tm, tn=bw)
        act = _conv_fwd(a, shared['conv_w'], conv_b, l, name=f"conv{l}")
        wf.update(fetch(('down', l), act))
        x = _mm(act, wf['w_down'], dims='nn', a_blocked=True, tk=bw, res=x_mid, out_dtype=F32,
                name=f"down{l}", tm=tm, tn=1024)
        sv.update(h=h, memn=memn, kvm=kvm, z=z, qcol=qcol, mix=mix, x_mid=x_mid, h2=h2, a=a, act=act, wf=wf)
        saved.append(sv)

    sq, dx, dg_final = _final_loss(x, target, rep['g_final'], name="loss")

    g = {k: [None] * DEPTH for k in ('g_mix', 'g_ffn', 'g_mem', 'conv_w', 'conv_b')}
    for k in ('g_v', 'w_sp', 'b_sp'):
        g[k] = [None] * N_A
    g['g_q_lat'] = [None] * n_b
    g['g_final'] = dg_final
    dckv_sum, dkr_sum = None, None

    for l in reversed(range(DEPTH)):
        sv = saved[l]
        wm, wf = sv['wm'], sv['wf']
        dact = _mm(dx, wf['w_down'], dims='nt', out_dtype=BF16, out_block=bw, name=f"d_act{l}", tm=tm, tn=bw)
        dw_down = _mm(sv['act'], dx, dims='tn', a_blocked=True, out_dtype=BF16, name=f"dw_down{l}", tm=bw, tn=512)
        tok = emit(('down', l), {'w_ffn_down': dw_down})
        dc, dcw, dcb = _conv_bwd_dc(sv['a'], dact, shared['conv_w'], conv_b, l, after=tok, name=f"d_conv{l}")
        g['conv_w'][l], g['conv_b'][l] = dcw, dcb
        da = _conv_bwd_da(dc, shared['conv_w'], l, name=f"d_convin{l}")
        dh2 = _mm(da, wf['w_up'], dims='nt', a_blocked=True, b_blocked=True, tk=bw,
                  out_dtype=BF16, name=f"d_h2{l}", tm=tm, tn=1024)
        dw_up = _mm(sv['h2'], da, dims='tn', b_blocked=True, out_dtype=BF16, out_block=bw,
                    name=f"dw_up{l}", tm=512, tn=bw, n_outer=True)
        tok = emit(('up', l), {'w_ffn_up': dw_up})
        dx_mid, g['g_ffn'][l] = _rmsnorm_bwd(sv['x_mid'], rep['g_ffn'][l], dh2, dres=dx, after=tok, name=f"d_ffnnorm{l}")
        dmix = _mm(dx_mid, wm['w_out'], dims='nt', out_dtype=BF16, name=f"d_mix{l}", tm=tm, tn=1024)
        gm = {'w_out': _mm(sv['mix'], dx_mid, dims='tn', out_dtype=BF16, name=f"dw_out{l}", tm=1024, tn=256)}
        dqm, dkvm = _memattn_bwd(sv['z'], sv['kvm'], dmix, qcol=sv['qcol'], name=f"d_memattn{l}")
        gm['w_mem_kv'] = _mm(sv['memn'], dkvm, dims='tn', out_dtype=BF16, name=f"dw_memkv{l}", tm=1024, tn=1024)
        dmemn = _mm(dkvm, wm['w_mem_kv'], dims='nt', out_dtype=F32, name=f"d_memn{l}", tm=tm, tn=1024)
        _, g['g_mem'][l] = _rmsnorm_bwd(mem, rep['g_mem'][l], dmemn, out_dtype=BF16, name=f"d_memnorm{l}")
        if l < N_A:
            dz, dwsp, dbsp_t, dgv = _sgu_bwd(sv['z'], dmix, dqm, shared['g_v'][l], rep['w_sp'][l], b_sp_t[l],
                                             name=f"d_sgu{l}")
            g['w_sp'][l], g['b_sp'][l], g['g_v'][l] = dwsp, dbsp_t.T, dgv
            dh = _mm(dz, wm['w_in'], dims='nt', out_dtype=BF16, name=f"d_h_a{l}", tm=tm, tn=1024)
            gm['w_in_a'] = _mm(sv['h'], dz, dims='tn', out_dtype=BF16, name=f"dw_in_a{l}", tm=1024, tn=512)
        else:
            j = l - N_A
            do_lat = _mm_heads(dmix, wm['w_uv'], mode='to_lat', name=f"d_olat{j}")
            gm['w_uv'] = _mm_heads(sv['o_lat'], dmix, mode='wgrad', name=f"dw_uv{j}")
            dqa, dqr, dckv, dkr = _mla_bwd(sv['qa'], sv['qr'], kv['ckv'], kv['kr'], sv['o_lat'], do_lat, sv['lse'],
                                           name=f"d_mla{j}")
            dckv_sum = dckv if dckv_sum is None else dckv_sum + dckv
            dkr_sum = dkr if dkr_sum is None else dkr_sum + dkr
            dq_nope = _mm_heads(dqa, wm['w_uk'], mode='from_lat', name=f"d_qnope{j}")
            gm['w_uk'] = _mm_heads(dqa, sv['qp'], mode='wgrad', name=f"dw_uk{j}")
            dr1, dr2 = _heads_from_major(dqr)
            dq1, dq2 = _rope(dr1, dr2, cos12, sin12, inverse=True, name=f"d_qrope{j}")
            dqp = jnp.concatenate([dq_nope, dq1, dq2], axis=-1)
            dqn = _mm(dqp, wm['w_uqp'], dims='nt', out_dtype=BF16, name=f"d_qn{j}", tm=tm, tn=512)
            gm['w_uqp'] = _mm(sv['qn'], dqp, dims='tn', out_dtype=BF16, name=f"dw_uq{j}", tm=512, tn=768)
            dqlat, g['g_q_lat'][j] = _rmsnorm_bwd(sv['z'], rep['g_q_lat'][j], dqn, width=Q_RANK, out_dtype=BF16,
                                                 name=f"d_qnorm{j}")
            dz = jnp.concatenate([dqlat, dqm], axis=-1)
            dh = _mm(dz, wm['w_in'], dims='nt', out_dtype=BF16, name=f"d_h_b{j}", tm=tm, tn=1024)
            gm['w_in_b'] = _mm(sv['h'], dz, dims='tn', out_dtype=BF16, name=f"dw_in_b{j}", tm=1024, tn=512)
        tok = emit(('mix', l), gm)
        dx, g['g_mix'][l] = _rmsnorm_bwd(sv['x_in'], rep['g_mix'][l], dh, dres=dx_mid, after=tok, name=f"d_mixnorm{l}")
        if l == N_A:
            dkvx_c, g['g_kv_lat'] = _rmsnorm_bwd(kv['kvx'], rep['g_kv_lat'], dckv_sum, width=KV_RANK, out_dtype=BF16,
                                                 name="d_ckvnorm")
            dk1, dk2 = _rope(dkr_sum[:, :ROPE_HALF], dkr_sum[:, ROPE_HALF:], cos, sin, inverse=True, name="d_krope")
            dkvx = jnp.concatenate([dkvx_c, dk1, dk2, jnp.zeros((s, KV_PAD - KV_RANK - ROPE_DIM), BF16)], axis=-1)
            dxn = _mm(dkvx, kv['w_kv_a'], dims='nt', out_dtype=BF16, name="d_kvnorm_in", tm=tm, tn=1024)
            dw_kv = _mm(kv['xn'], dkvx, dims='tn', out_dtype=BF16, name="dw_kv", tm=1024, tn=KV_PAD)
            tok = emit(('kv', 0), {'w_kv_a': dw_kv})
            dx, g['g_kv'] = _rmsnorm_bwd(kv['x'], rep['g_kv'], dxn, dres=dx, after=tok, name="d_kvnorm")
    return jnp.sum(sq), dx, g


MESH_IDS = pl.DeviceIdType.MESH
PEER_MASKS = tuple((k >> 2 & 1, k >> 1 & 1, k & 1) for k in range(1, N_DEV))
CHIP_MASKS = ((1, 0), (0, 1), (1, 1))
N_PEER = N_DEV - 1
SEMS_PER_BUFFER = 2 * N_PEER + 1
DATAFLOW = pltpu.SideEffectType.DATAFLOW_SIDE_EFFECTING
HBM_SPEC = pl.BlockSpec(memory_space=pltpu.HBM)
SEM_SPEC = pl.BlockSpec(memory_space=pltpu.SEMAPHORE)


def _my_position():
    return lax.axis_index("x"), lax.axis_index("y"), lax.axis_index("c")


def _flip(pos, mask):
    return tuple(1 - p if f else p for p, f in zip(pos, mask))


def _linear_id(pos):
    return 4 * pos[0] + 2 * pos[1] + pos[2]


def _hbm(x):
    return pltpu.with_memory_space_constraint(x, pltpu.HBM)


def _buffer_copies(src_ref, lead, land_ref, sems, scatter, near=False):
    me = _my_position()
    my_id = _linear_id(me)
    src = src_ref.at[lead] if lead else src_ref
    own = pltpu.make_async_copy(src.at[my_id] if scatter else src, land_ref.at[my_id], sems.at[2 * N_PEER])
    pairs = []
    for k, mask in enumerate(PEER_MASKS):
        if near and mask[2] == 1 and mask != (0, 0, 1):
            continue
        peer = _flip(me, mask)
        peer_id = _linear_id(peer)
        block = src.at[peer_id] if scatter else src
        send = pltpu.make_async_remote_copy(src_ref=block, dst_ref=land_ref.at[my_id], send_sem=sems.at[k],
                                            recv_sem=sems.at[N_PEER + k], device_id=peer, device_id_type=MESH_IDS)
        arrival = pltpu.make_async_remote_copy(src_ref=block, dst_ref=land_ref.at[peer_id], send_sem=sems.at[k],
                                               recv_sem=sems.at[N_PEER + k], device_id=peer, device_id_type=MESH_IDS)
        pairs.append((send, arrival))
    return own, pairs


def _exchange_start(srcs, buffers, *, name, scatter):
    ns, nb = len(srcs), len(buffers)
    lands = [_hbm(lax.empty((N_DEV,) + tuple(shape), dtype)) for _, _, shape, dtype, _ in buffers]

    def body(*refs):
        src_refs, land_refs = refs[:ns], refs[ns:ns + nb]
        sem_refs = refs[ns + nb:ns + 2 * nb]
        token = refs[-1]
        for b, (si, lead, _, _, near) in enumerate(buffers):
            own, pairs = _buffer_copies(src_refs[si], lead, land_refs[b], sem_refs[b], scatter, near)
            own.start()
            for send, _ in pairs:
                send.start()
        token[...] = jnp.zeros_like(token)

    out_shape = ([pltpu.SemaphoreType.DMA((SEMS_PER_BUFFER,))] * nb
                 + [pltpu.HBM(a.shape, a.dtype) for a in srcs]
                 + [pltpu.HBM(a.shape, a.dtype) for a in lands]
                 + [jax.ShapeDtypeStruct((8, 128), F32)])
    aliases = {i: nb + i for i in range(ns + nb)}
    outs = pl.pallas_call(
        body, name=name, in_specs=[HBM_SPEC] * (ns + nb),
        out_specs=[SEM_SPEC] * nb + [HBM_SPEC] * (ns + nb) + [pl.BlockSpec(memory_space=pltpu.VMEM)],
        out_shape=out_shape, input_output_aliases=aliases,
        compiler_params=pltpu.CompilerParams(has_side_effects=DATAFLOW),
    )(*[_hbm(a) for a in srcs], *lands)
    sems = list(outs[:nb])
    src_thru = list(outs[nb:nb + ns])
    land_thru = list(outs[nb + ns:nb + ns + nb])
    return sems, land_thru, src_thru, outs[-1]


def _exchange_wait(srcs_thru, buffers, sems, lands, after, *, name, scatter):
    ns, nb = len(srcs_thru), len(buffers)
    has_after = after is not None

    def body(*refs):
        src_refs, land_refs = refs[:ns], refs[ns:ns + nb]
        sem_refs = refs[ns + nb:ns + 2 * nb]
        for b, (si, lead, _, _, near) in enumerate(buffers):
            own, pairs = _buffer_copies(src_refs[si], lead, land_refs[b], sem_refs[b], scatter, near)
            for send, arrival in pairs:
                send.wait_send()
                arrival.wait_recv()
            own.wait()

    operands = list(srcs_thru) + list(lands) + list(sems) + ([after] if has_after else [])
    in_specs = ([HBM_SPEC] * (ns + nb) + [SEM_SPEC] * nb + ([pl.BlockSpec(memory_space=pl.ANY)] if has_after else []))
    outs = pl.pallas_call(
        body, name=name, in_specs=in_specs, out_specs=[HBM_SPEC] * nb,
        out_shape=[pltpu.HBM(a.shape, a.dtype) for a in lands],
        input_output_aliases={ns + b: b for b in range(nb)},
        compiler_params=pltpu.CompilerParams(has_side_effects=DATAFLOW),
    )(*operands)
    return list(outs)


def _exchange(arrays, *, name, scatter, near=None, after=None):
    n = len(arrays)
    near = [False] * n if near is None else near
    extra = [] if after is None else [after]
    out_shapes = [jax.ShapeDtypeStruct(a.shape if scatter else (N_DEV,) + a.shape, a.dtype) for a in arrays]

    def body(*refs):
        srcs, outs, sems = refs[:n], refs[n + len(extra):2 * n + len(extra)], refs[2 * n + len(extra):]
        started = []
        for a in range(n):
            own, pairs = _buffer_copies(srcs[a], (), outs[a], sems[a], scatter, near[a])
            own.start()
            for send, _ in pairs:
                send.start()
            started.append((own, pairs))
        for own, pairs in started:
            for send, arrival in pairs:
                arrival.wait_recv()
                send.wait_send()
            own.wait()

    any_spec = pl.BlockSpec(memory_space=pl.ANY)
    outs = pl.pallas_call(
        body, name=name, in_specs=[any_spec] * (n + len(extra)), out_specs=[any_spec] * n, out_shape=out_shapes,
        scratch_shapes=[pltpu.SemaphoreType.DMA((SEMS_PER_BUFFER,))] * n,
    )(*arrays, *extra)
    return list(outs)


def _forward_to_sibling(lands, *, name):
    n = len(lands)

    def body(*refs):
        ins, outs, sems = refs[:n], refs[n:2 * n], refs[2 * n:]
        me = _my_position()
        sibling = _flip(me, (0, 0, 1))
        pairs = []
        for b in range(n):
            for k, (fx, fy) in enumerate(CHIP_MASKS):
                mine = _linear_id(_flip(me, (fx, fy, 0)))
                theirs = _linear_id(_flip(me, (fx, fy, 1)))
                send = pltpu.make_async_remote_copy(
                    src_ref=ins[b].at[mine], dst_ref=outs[b].at[mine], send_sem=sems[b].at[k],
                    recv_sem=sems[b].at[len(CHIP_MASKS) + k], device_id=sibling, device_id_type=MESH_IDS)
                arrival = pltpu.make_async_remote_copy(
                    src_ref=ins[b].at[mine], dst_ref=outs[b].at[theirs], send_sem=sems[b].at[k],
                    recv_sem=sems[b].at[len(CHIP_MASKS) + k], device_id=sibling, device_id_type=MESH_IDS)
                send.start()
                pairs.append((send, arrival))
        for send, arrival in pairs:
            arrival.wait_recv()
            send.wait_send()

    any_spec = pl.BlockSpec(memory_space=pl.ANY)
    outs = pl.pallas_call(
        body, name=name, in_specs=[any_spec] * n, out_specs=[any_spec] * n,
        out_shape=[jax.ShapeDtypeStruct(a.shape, a.dtype) for a in lands],
        input_output_aliases={b: b for b in range(n)},
        scratch_shapes=[pltpu.SemaphoreType.DMA((2 * len(CHIP_MASKS),))] * n,
    )(*lands)
    return list(outs)


def _sum_slots(parts_ref):
    total = parts_ref[0].astype(F32)
    for q in range(1, parts_ref.shape[0]):
        total = total + parts_ref[q].astype(F32)
    return total


def _row_tile(rows, cols, n_arrays):
    budget = (12 * 1024 * 1024) // (4 * n_arrays * max(cols, 128))
    t = rows
    while t > budget and t % 2 == 0 and (t // 2) % 16 == 0:
        t //= 2
    return t


def _sum_adam(parts, w, m, v, layer, outs, *, name):
    q, r, c = parts.shape
    nl = w.shape[0]
    tr = _row_tile(r, c, q + 7)
    c1 = 1.0 - ADAM_B1 ** ADAM_STEP
    c2 = 1.0 - ADAM_B2 ** ADAM_STEP
    if outs is None:
        outs = [lax.empty((nl, r, c), F32) for _ in range(4)]

    def body(p_ref, w_ref, m_ref, v_ref, g_in, d_in, mo_in, vo_in, g_ref, d_ref, mo_ref, vo_ref):
        grad = _sum_slots(p_ref)
        m_new = ADAM_B1 * m_ref[...] + (1.0 - ADAM_B1) * grad
        v_new = ADAM_B2 * v_ref[...] + (1.0 - ADAM_B2) * (grad * grad)
        m_hat = m_new / c1
        v_hat = v_new / c2
        g_ref[...] = grad
        d_ref[...] = -ADAM_LR * (m_hat / (jnp.sqrt(v_hat) + ADAM_EPS) + ADAM_WD * w_ref[...])
        mo_ref[...] = m_new
        vo_ref[...] = v_new

    tile = pl.BlockSpec((None, tr, c), lambda i: (layer, i, 0))
    any_spec = pl.BlockSpec(memory_space=pl.ANY)
    return pl.pallas_call(
        body, name=name, grid=(r // tr,),
        in_specs=[pl.BlockSpec((q, tr, c), lambda i: (0, i, 0)), tile, tile, tile] + [any_spec] * 4,
        out_specs=[tile] * 4, out_shape=[jax.ShapeDtypeStruct((nl, r, c), F32)] * 4,
        input_output_aliases={4: 0, 5: 1, 6: 2, 7: 3},
        compiler_params=_params(("parallel",)),
    )(parts, w, m, v, *outs)


def _sum_parts(parts, *, name):
    q, r, c = parts.shape

    def body(p_ref, o_ref):
        o_ref[...] = _sum_slots(p_ref)

    return pl.pallas_call(
        body, name=name, in_specs=[pl.BlockSpec((q, r, c), lambda: (0, 0, 0))],
        out_specs=pl.BlockSpec((r, c), lambda: (0, 0)), out_shape=jax.ShapeDtypeStruct((r, c), F32),
        compiler_params=_params(),
    )(parts)


INPUT_NAMES = (['x', 'mem', 'positions'] + WEIGHTS + ['loss_target'] + ['m_' + n for n in WEIGHTS]
               + ['v_' + n for n in WEIGHTS])
SMALL_ALIGN = N_DEV * 8 * 128
TWO_LEVEL_LAYERS = N_A
GROUP_ORDER = ('in', 'rest', 'up', 'down')
GROUP_WEIGHTS = {'in': (['w_in_a'], ['w_in_b', 'w_uq', 'w_uk', 'w_uv']), 'rest': (['w_mem_kv', 'w_out'],) * 2,
                 'up': (['w_ffn_up'],) * 2, 'down': (['w_ffn_down'],) * 2}
LAYERED = {'w_in_a': 0, 'w_in_b': N_A, 'w_uq': N_A, 'w_uk': N_A, 'w_uv': N_A, 'w_mem_kv': 0, 'w_out': 0,
           'w_ffn_up': 0, 'w_ffn_down': 0}


def _permute_uq(w_uq):
    r = w_uq.shape[0]
    q = w_uq.reshape(r, G_HEADS, HEAD + ROPE_DIM)
    return jnp.concatenate([q[..., :HEAD].reshape(r, -1), q[..., HEAD:HEAD + ROPE_HALF].reshape(r, -1),
                            q[..., HEAD + ROPE_HALF:].reshape(r, -1)], axis=-1)


def _unpermute_uq(w_uqp):
    r = w_uqp.shape[0]
    nope = w_uqp[..., :G_W].reshape(r, G_HEADS, HEAD)
    r1 = w_uqp[..., G_W:G_W + G_HEADS * ROPE_HALF].reshape(r, G_HEADS, ROPE_HALF)
    r2 = w_uqp[..., G_W + G_HEADS * ROPE_HALF:].reshape(r, G_HEADS, ROPE_HALF)
    return jnp.concatenate([nope, r1, r2], axis=-1).reshape(r, -1)


def _cols_from_stack(st):
    _, r, n = st.shape
    return st.transpose(1, 0, 2).reshape(r, N_DEV * n)


def _cols_to_stack(wh):
    r, c = wh.shape
    return wh.reshape(r, N_DEV, c // N_DEV).transpose(1, 0, 2)


def _group_weights(group):
    kind, l = group
    return GROUP_WEIGHTS[kind][0 if l < N_A else 1]


def _step(args):
    p = dict(zip(INPUT_NAMES, args))
    x, mem, positions, target = p['x'][0], p['mem'][0], p['positions'][0], p['loss_target'][0]
    d = x.shape[-1]
    my_id = _linear_id(_my_position())

    w_kv_pad = jnp.pad(p['w_kv_a'], ((0, 0), (0, KV_PAD - p['w_kv_a'].shape[1])))
    shard = {k: p[k].astype(BF16) for k in LAYERED}
    shard['w_uk'] = shard['w_uk'].reshape(shard['w_uk'].shape[0], shard['w_uk'].shape[1], -1)
    shard['w_uv'] = shard['w_uv'].reshape(shard['w_uv'].shape[0], shard['w_uv'].shape[1], -1)
    shard.update(conv_w=p['conv_w'], g_v=p['g_v'], w_kv_a=w_kv_pad.astype(BF16))
    src_names = list(shard)
    gather_groups = []
    for l in range(DEPTH):
        gather_groups += [(kind, l) for kind in GROUP_ORDER]
    buffers, owner = [], []
    for group in gather_groups:
        kind, l = group
        for k in _group_weights(group):
            buffers.append((src_names.index(k), (l - LAYERED[k],), shard[k].shape[1:], shard[k].dtype, l < TWO_LEVEL_LAYERS))
            owner.append((group, k))
        if group == ('in', 0):
            for k in ('g_v', 'conv_w'):
                buffers.append((src_names.index(k), (), shard[k].shape, shard[k].dtype, True))
                owner.append((group, k))
        if group == ('in', N_A):
            buffers.append((src_names.index('w_kv_a'), (), shard['w_kv_a'].shape, BF16, False))
            owner.append((group, 'w_kv_a'))
    g_sems, g_lands, g_srcs, _ = _exchange_start([shard[k] for k in src_names], buffers, name="gather_start",
                                                 scatter=False)

    def fetch(group, after):
        idx = [i for i, (grp, _) in enumerate(owner) if grp == group]
        landed = _exchange_wait(g_srcs, [buffers[i] for i in idx], [g_sems[i] for i in idx],
                                [g_lands[i] for i in idx], after, name=f"gather_wait_{group[0]}{group[1]}",
                                scatter=False)
        if group[1] < TWO_LEVEL_LAYERS:
            landed = _forward_to_sibling(landed, name=f"gather_forward_{group[0]}{group[1]}")
        got = {owner[i][1]: t for i, t in zip(idx, landed)}
        out = {}
        for k, t in got.items():
            if k in ('w_in_a', 'w_uq'):
                out[k] = _cols_from_stack(t)
            elif k == 'g_v':
                out[k] = t.transpose(1, 0, 2).reshape(t.shape[1], -1)
            elif k in ('w_ffn_up', 'conv_w'):
                out[k] = t
            else:
                out[k] = t.reshape(-1, t.shape[-1])
        if 'w_uq' in out:
            out['w_uqp'] = _permute_uq(out.pop('w_uq'))
        for old, new in (('w_in_a', 'w_in'), ('w_in_b', 'w_in'), ('w_ffn_up', 'w_up'), ('w_ffn_down', 'w_down')):
            if old in out:
                out[new] = out.pop(old)
        return out

    pending = []

    def emit(group, grads):
        send = {}
        for k, t in grads.items():
            if k == 'w_in_a':
                send[k] = _cols_to_stack(t)
            elif k == 'w_uqp':
                send['w_uq'] = _cols_to_stack(_unpermute_uq(t))
            elif k == 'w_ffn_up':
                send[k] = t
            elif k == 'w_kv_a':
                cols = p['w_kv_a'].shape[1]
                send[k] = t[:, :cols].reshape(N_DEV, -1, cols)
            else:
                send[k] = t.reshape(N_DEV, t.shape[0] // N_DEV, t.shape[1])
        keys = list(send)
        bufs = [(i, (), send[k].shape[1:], send[k].dtype, False) for i, k in enumerate(keys)]
        sems, lands, srcs, token = _exchange_start([send[k] for k in keys], bufs,
                                                   name=f"scatter_start_{group[0]}{group[1]}", scatter=True)
        pending.append((group, keys, bufs, sems, lands, srcs))
        return token

    rep = {k: p[k] for k in REPLICATED}
    sq, grad_x, g = _local_step(x, mem, positions, target, rep, fetch, emit)
    loss = (0.5 / d) * lax.psum(sq, ("x", "y", "c"))

    out, running = {}, {}
    for group, keys, bufs, sems, lands, srcs in pending:
        landed = _exchange_wait(srcs, bufs, sems, lands, None, name=f"scatter_wait_{group[0]}{group[1]}", scatter=True)
        for k, parts in zip(keys, landed):
            stacked = k in LAYERED
            nl = p[k].shape[0] if stacked else 1
            layer = group[1] - LAYERED[k] if stacked else 0
            rows = p[k].size // nl // p[k].shape[-1]
            view = (nl, rows, p[k].shape[-1])
            running[k] = _sum_adam(parts.reshape(N_DEV, rows, view[2]), p[k].reshape(view), p['m_' + k].reshape(view),
                                   p['v_' + k].reshape(view), layer, running.get(k), name=f"adam_{k}{layer}")
    for k, res in running.items():
        out[k] = [t.reshape(p[k].shape) for t in res]

    small = {
        'g_mix': jnp.concatenate(g['g_mix']), 'g_ffn': jnp.concatenate(g['g_ffn']), 'g_final': g['g_final'],
        'w_sp': jnp.stack(g['w_sp']), 'b_sp': jnp.stack(g['b_sp']), 'g_kv': g['g_kv'], 'g_kv_lat': g['g_kv_lat'],
        'g_q_lat': jnp.concatenate(g['g_q_lat']), 'g_mem': jnp.concatenate(g['g_mem']),
        'conv_b': jnp.stack(g['conv_b']),
        'g_v': jnp.concatenate(g['g_v']),
        'conv_w': jnp.stack(g['conv_w']).transpose(0, 2, 1, 3),
    }
    small_names = REPLICATED + SMALL_SHARDED
    flat = jnp.concatenate([small[k].reshape(-1).astype(F32) for k in small_names])
    n_small = flat.shape[0]
    padded = -(-n_small // SMALL_ALIGN) * SMALL_ALIGN
    flat = jnp.pad(flat, (0, padded - n_small)).reshape(N_DEV, -1, 128)
    last_update = out[pending[-1][1][-1]][1]
    (small_parts,) = _exchange([flat], name="scatter_small", scatter=True, after=last_update)
    reduced = _sum_parts(small_parts, name="sum_small")
    (small_all,) = _exchange([reduced], name="gather_small", scatter=False)
    small_all = small_all.reshape(-1)
    grads_small, off = {}, 0
    for k in small_names:
        size = small[k].size
        grads_small[k] = small_all[off:off + size].reshape(small[k].shape)
        off += size
    grads_small['g_v'] = lax.dynamic_slice_in_dim(grads_small['g_v'], my_id * p['g_v'].shape[1], p['g_v'].shape[1], axis=1)
    grads_small['conv_w'] = lax.dynamic_index_in_dim(grads_small['conv_w'], my_id, axis=2, keepdims=False)
    gs = jnp.concatenate([grads_small[k].reshape(-1) for k in small_names])
    n_loc = gs.shape[0]
    pad_loc = -(-n_loc // 1024) * 1024 - n_loc

    def pack(prefix):
        t = jnp.concatenate([p[prefix + k].reshape(-1) for k in small_names])
        return jnp.pad(t, (0, pad_loc)).reshape(1, -1, 128)

    res = _sum_adam(jnp.pad(gs, (0, pad_loc)).reshape(1, -1, 128), pack(''), pack('m_'), pack('v_'), 0, None,
                    name="adam_small")
    off = 0
    for k in small_names:
        size = p[k].size
        out[k] = [t.reshape(-1)[off:off + size].reshape(p[k].shape) for t in res]
        off += size

    outs = [loss, grad_x[None]]
    for i in range(4):
        outs += [out[k][i] for k in WEIGHTS]
    return tuple(outs)


def kernel(x, mem, positions, g_mix, g_ffn, g_final, w_in_a, g_v, w_sp, b_sp, g_kv, w_kv_a, g_kv_lat, w_in_b, g_q_lat, w_uq, w_uk, w_uv, g_mem, w_mem_kv, w_out, w_ffn_up, conv_w, conv_b, w_ffn_down, loss_target, m_g_mix, m_g_ffn, m_g_final, m_w_in_a, m_g_v, m_w_sp, m_b_sp, m_g_kv, m_w_kv_a, m_g_kv_lat, m_w_in_b, m_g_q_lat, m_w_uq, m_w_uk, m_w_uv, m_g_mem, m_w_mem_kv, m_w_out, m_w_ffn_up, m_conv_w, m_conv_b, m_w_ffn_down, v_g_mix, v_g_ffn, v_g_final, v_w_in_a, v_g_v, v_w_sp, v_b_sp, v_g_kv, v_w_kv_a, v_g_kv_lat, v_w_in_b, v_g_q_lat, v_w_uq, v_w_uk, v_w_uv, v_g_mem, v_w_mem_kv, v_w_out, v_w_ffn_up, v_conv_w, v_conv_b, v_w_ffn_down):
    return _step((x, mem, positions, g_mix, g_ffn, g_final, w_in_a, g_v, w_sp, b_sp, g_kv, w_kv_a, g_kv_lat, w_in_b, g_q_lat, w_uq, w_uk, w_uv, g_mem, w_mem_kv, w_out, w_ffn_up, conv_w, conv_b, w_ffn_down, loss_target, m_g_mix, m_g_ffn, m_g_final, m_w_in_a, m_g_v, m_w_sp, m_b_sp, m_g_kv, m_w_kv_a, m_g_kv_lat, m_w_in_b, m_g_q_lat, m_w_uq, m_w_uk, m_w_uv, m_g_mem, m_w_mem_kv, m_w_out, m_w_ffn_up, m_conv_w, m_conv_b, m_w_ffn_down, v_g_mix, v_g_ffn, v_g_final, v_w_in_a, v_g_v, v_w_sp, v_b_sp, v_g_kv, v_w_kv_a, v_g_kv_lat, v_w_in_b, v_g_q_lat, v_w_uq, v_w_uk, v_w_uv, v_g_mem, v_w_mem_kv, v_w_out, v_w_ffn_up, v_conv_w, v_conv_b, v_w_ffn_down))
```

```python
import math

import jax
import jax.numpy as jnp
from jax import lax
from jax.experimental import pallas as pl
from jax.experimental.pallas import tpu as pltpu

F32 = jnp.float32
BF16 = jnp.bfloat16

N_DEV = 8
N_A = 2
DEPTH = 4
G_HEADS = 12
HEAD = 128
CHUNK = 128
MEM_HEADS = 4
MEM_W = MEM_HEADS * HEAD
G_W = G_HEADS * HEAD
ROPE_DIM = 64
ROPE_HALF = ROPE_DIM // 2
KV_RANK = 512
Q_RANK = 512
KV_PAD = 640
ROPE_THETA = 10000.0
EPS = 1e-6
CONV_W = 3

ADAM_LR = 0.001
ADAM_B1 = 0.9
ADAM_B2 = 0.999
ADAM_EPS = 1e-08
ADAM_WD = 0.01
ADAM_STEP = 10

VMEM_LIMIT_V7X = 56 * 1024 * 1024
MASK_VALUE = -1e30

WEIGHTS = ['g_mix', 'g_ffn', 'g_final', 'w_in_a', 'g_v', 'w_sp', 'b_sp', 'g_kv', 'w_kv_a', 'g_kv_lat',
           'w_in_b', 'g_q_lat', 'w_uq', 'w_uk', 'w_uv', 'g_mem', 'w_mem_kv', 'w_out', 'w_ffn_up',
           'conv_w', 'conv_b', 'w_ffn_down']
REPLICATED = ['g_mix', 'g_ffn', 'g_final', 'w_sp', 'b_sp', 'g_kv', 'g_kv_lat', 'g_q_lat', 'g_mem', 'conv_b']
SMALL_SHARDED = ['g_v', 'conv_w']


def _params(sem=None):
    return pltpu.CompilerParams(dimension_semantics=sem, vmem_limit_bytes=VMEM_LIMIT_V7X)


def _dot(a, b, dims):
    contract = {'nn': ((1,), (0,)), 'nt': ((1,), (1,)), 'tn': ((0,), (0,))}[dims]
    return lax.dot_general(a, b, (contract, ((), ())), preferred_element_type=F32)


def _erf(x):
    return lax.erf(x)


def _gelu(x):
    return 0.5 * x * (1.0 + _erf(x * (2.0 ** -0.5)))


def _gelu_grad(x):
    cdf = 0.5 * (1.0 + _erf(x * (2.0 ** -0.5)))
    pdf = jnp.exp(-0.5 * x * x) * (1.0 / math.sqrt(2.0 * math.pi))
    return cdf + x * pdf


def _sigmoid(x):
    return 1.0 / (1.0 + jnp.exp(-x))


def _operand_spec(shape, lead, blocked, tr, tc, ridx, cidx):
    if blocked:
        per = shape[-1] // tc
        assert shape[-1] % tc == 0, (shape, tc)
        return pl.BlockSpec(
            (None,) * (1 + len(lead)) + (tr, tc),
            lambda *g: (cidx(*g) // per,) + lead + (ridx(*g), cidx(*g) % per))
    return pl.BlockSpec((None,) * len(lead) + (tr, tc), lambda *g: lead + (ridx(*g), cidx(*g)))


def _view2d(x, blocked):
    return (x.shape[-2], x.shape[0] * x.shape[-1]) if blocked else (x.shape[-2], x.shape[-1])


def _mm(a, b, *, dims, out_dtype, name, tm, tn, tk=None, res=None, a_lead=(), b_lead=(),
        a_blocked=False, b_blocked=False, out_block=None, n_outer=False, after=None):
    ar, ac = _view2d(a, a_blocked)
    br, bc = _view2d(b, b_blocked)
    m, k = (ac, ar) if dims == 'tn' else (ar, ac)
    n, k2 = (br, bc) if dims == 'nt' else (bc, br)
    assert k == k2, (a.shape, b.shape, dims)
    tm, tn = min(tm, m), min(tn, n)
    tk = k if tk is None else tk
    assert m % tm == 0 and n % tn == 0 and k % tk == 0, (name, m, n, k, tm, tn, tk)
    nk = k // tk
    if n_outer:
        gi, gj = (lambda g0, g1, g2: g1), (lambda g0, g1, g2: g0)
        grid = (n // tn, m // tm, nk)
    else:
        gi, gj = (lambda g0, g1, g2: g0), (lambda g0, g1, g2: g1)
        grid = (m // tm, n // tn, nk)
    gk = lambda g0, g1, g2: g2

    if dims == 'tn':
        a_spec = _operand_spec(a.shape, a_lead, a_blocked, tk, tm, gk, gi)
    else:
        a_spec = _operand_spec(a.shape, a_lead, a_blocked, tm, tk, gi, gk)
    if dims == 'nt':
        b_spec = _operand_spec(b.shape, b_lead, b_blocked, tn, tk, gj, gk)
    else:
        b_spec = _operand_spec(b.shape, b_lead, b_blocked, tk, tn, gk, gj)
    in_specs = [a_spec, b_spec]
    operands = [a, b]
    if res is not None:
        in_specs.append(pl.BlockSpec((tm, tn), lambda *g: (gi(*g), gj(*g))))
        operands.append(res)
    if after is not None:
        in_specs.append(pl.BlockSpec(memory_space=pl.ANY))
        operands.append(after)
    n_in = len(operands)
    if out_block is not None:
        out_shape = jax.ShapeDtypeStruct((n // out_block, m, out_block), out_dtype)
        out_spec = _operand_spec(out_shape.shape, (), True, tm, tn, gi, gj)
    else:
        out_shape = jax.ShapeDtypeStruct((m, n), out_dtype)
        out_spec = pl.BlockSpec((tm, tn), lambda *g: (gi(*g), gj(*g)))

    def body(*refs):
        a_ref, b_ref = refs[0], refs[1]
        r_ref = refs[2] if res is not None else None
        o_ref = refs[n_in]
        acc_ref = refs[-1] if nk > 1 else None
        part = _dot(a_ref[...].astype(BF16), b_ref[...].astype(BF16), dims)

        def finish(total):
            if r_ref is not None:
                total = total + r_ref[...]
            o_ref[...] = total.astype(o_ref.dtype)

        if nk == 1:
            finish(part)
        else:
            kk = pl.program_id(2)

            @pl.when(kk == 0)
            def _():
                acc_ref[...] = part

            @pl.when(kk > 0)
            def _():
                acc_ref[...] += part

            @pl.when(kk == nk - 1)
            def _():
                finish(acc_ref[...])

    scratch = [pltpu.VMEM((tm, tn), F32)] if nk > 1 else []
    return pl.pallas_call(
        body, name=name, grid=grid, in_specs=in_specs, out_specs=out_spec,
        out_shape=out_shape, scratch_shapes=scratch,
        compiler_params=_params(("parallel", "parallel", "arbitrary")),
    )(*operands)


def _rmsnorm(x, g, *, name, width=None, out_dtype=BF16, tm=512):
    s = x.shape[0]
    w = x.shape[1] if width is None else width
    tm = min(tm, s)

    def body(x_ref, g_ref, o_ref):
        xv = x_ref[...].astype(F32)
        rstd = lax.rsqrt(jnp.mean(xv * xv, axis=-1, keepdims=True) + EPS)
        o_ref[...] = (xv * rstd * g_ref[...]).astype(o_ref.dtype)

    return pl.pallas_call(
        body, name=name, grid=(s // tm,),
        in_specs=[pl.BlockSpec((tm, w), lambda i: (i, 0)), pl.BlockSpec((1, w), lambda i: (0, 0))],
        out_specs=pl.BlockSpec((tm, w), lambda i: (i, 0)),
        out_shape=jax.ShapeDtypeStruct((s, w), out_dtype),
        compiler_params=_params(("parallel",)),
    )(x, g.reshape(1, w))


def _rmsnorm_bwd(x, g, dy, *, name, width=None, dres=None, after=None, out_dtype=F32, tm=512):
    s = x.shape[0]
    w = x.shape[1] if width is None else width
    tm = min(tm, s)

    def body(*refs):
        x_ref, g_ref, dy_ref = refs[0], refs[1], refs[2]
        r_ref = refs[3] if dres is not None else None
        dx_ref, dg_ref = refs[-2], refs[-1]
        xv = x_ref[...].astype(F32)
        rstd = lax.rsqrt(jnp.mean(xv * xv, axis=-1, keepdims=True) + EPS)
        xhat = xv * rstd
        dyv = dy_ref[...].astype(F32)
        gdy = dyv * g_ref[...]
        dx = rstd * (gdy - xhat * jnp.mean(gdy * xhat, axis=-1, keepdims=True))
        if r_ref is not None:
            dx = dx + r_ref[...]
        dx_ref[...] = dx.astype(dx_ref.dtype)
        part = jnp.sum(dyv * xhat, axis=0, keepdims=True)

        @pl.when(pl.program_id(0) == 0)
        def _():
            dg_ref[...] = part

        @pl.when(pl.program_id(0) > 0)
        def _():
            dg_ref[...] += part

    row = pl.BlockSpec((tm, w), lambda i: (i, 0))
    vec = pl.BlockSpec((1, w), lambda i: (0, 0))
    in_specs = [row, vec, row] + ([row] if dres is not None else [])
    operands = [x, g.reshape(1, w), dy] + ([dres] if dres is not None else [])
    if after is not None:
        in_specs.append(pl.BlockSpec(memory_space=pl.ANY))
        operands.append(after)
    return pl.pallas_call(
        body, name=name, grid=(s // tm,), in_specs=in_specs, out_specs=[row, vec],
        out_shape=[jax.ShapeDtypeStruct((s, w), out_dtype), jax.ShapeDtypeStruct((1, w), F32)],
        compiler_params=_params(("arbitrary",)),
    )(*operands)


def _final_loss(x, target, g, *, name, tm=256):
    s, d = x.shape
    tm = min(tm, s)

    def body(x_ref, t_ref, g_ref, sq_ref, dx_ref, dg_ref):
        xv = x_ref[...]
        rstd = lax.rsqrt(jnp.mean(xv * xv, axis=-1, keepdims=True) + EPS)
        xhat = xv * rstd
        err = xhat * g_ref[...] - t_ref[...]
        dyv = err * (1.0 / d)
        gdy = dyv * g_ref[...]
        dx_ref[...] = rstd * (gdy - xhat * jnp.mean(gdy * xhat, axis=-1, keepdims=True))
        sq = jnp.sum(err * err, axis=0, keepdims=True)
        dg = jnp.sum(dyv * xhat, axis=0, keepdims=True)

        @pl.when(pl.program_id(0) == 0)
        def _():
            sq_ref[...] = sq
            dg_ref[...] = dg

        @pl.when(pl.program_id(0) > 0)
        def _():
            sq_ref[...] += sq
            dg_ref[...] += dg

    row = pl.BlockSpec((tm, d), lambda i: (i, 0))
    vec = pl.BlockSpec((1, d), lambda i: (0, 0))
    return pl.pallas_call(
        body, name=name, grid=(s // tm,), in_specs=[row, row, vec], out_specs=[vec, row, vec],
        out_shape=[jax.ShapeDtypeStruct((1, d), F32), jax.ShapeDtypeStruct((s, d), F32),
                   jax.ShapeDtypeStruct((1, d), F32)],
        compiler_params=_params(("arbitrary",)),
    )(x, target, g.reshape(1, d))


def _tril_mask():
    t = lax.broadcasted_iota(jnp.int32, (CHUNK, CHUNK), 0)
    s = lax.broadcasted_iota(jnp.int32, (CHUNK, CHUNK), 1)
    return t >= s


def _sgu_fwd(z, g_v, w_sp, b_sp_t, *, name):
    s = z.shape[0]

    def body(zu_ref, zv_ref, g_ref, w_ref, b_ref, o_ref):
        u = _gelu(zu_ref[...].astype(F32))
        gv = _gelu(zv_ref[...].astype(F32))
        rstd = lax.rsqrt(jnp.mean(gv * gv, axis=-1, keepdims=True) + EPS)
        v = (gv * rstd * g_ref[...]).astype(BF16)
        mask = _tril_mask()
        for grp in range(G_HEADS):
            cols = slice(grp * HEAD, (grp + 1) * HEAD)
            wm = jnp.where(mask, w_ref[grp], 0.0).astype(BF16)
            sv = _dot(wm, v[:, cols], 'nn') + b_ref[:, grp:grp + 1]
            o_ref[:, cols] = (u[:, cols] * sv).astype(o_ref.dtype)

    return pl.pallas_call(
        body, name=name, grid=(s // CHUNK,),
        in_specs=[pl.BlockSpec((CHUNK, G_W), lambda i: (i, 0)),
                  pl.BlockSpec((CHUNK, G_W), lambda i: (i, 1)),
                  pl.BlockSpec((1, G_W), lambda i: (0, 0)),
                  pl.BlockSpec((G_HEADS, CHUNK, CHUNK), lambda i: (0, 0, 0)),
                  pl.BlockSpec((CHUNK, G_HEADS), lambda i: (0, 0))],
        out_specs=pl.BlockSpec((CHUNK, G_W), lambda i: (i, 0)),
        out_shape=jax.ShapeDtypeStruct((s, G_W), BF16),
        compiler_params=_params(("parallel",)),
    )(z, z, g_v.reshape(1, G_W), w_sp, b_sp_t)


def _sgu_bwd(z, dmix, dqm, g_v, w_sp, b_sp_t, *, name):
    s = z.shape[0]
    zw = z.shape[1]

    def body(zu_ref, zv_ref, dm_ref, dq_ref, g_ref, w_ref, b_ref, dz_ref, dw_ref, db_ref, dg_ref):
        first = pl.program_id(0) == 0

        @pl.when(first)
        def _():
            dw_ref[...] = jnp.zeros_like(dw_ref)
            db_ref[...] = jnp.zeros_like(db_ref)
            dg_ref[...] = jnp.zeros_like(dg_ref)

        zu = zu_ref[...].astype(F32)
        zv = zv_ref[...].astype(F32)
        dmain = dm_ref[...].astype(F32)
        u = _gelu(zu)
        gv = _gelu(zv)
        rstd = lax.rsqrt(jnp.mean(gv * gv, axis=-1, keepdims=True) + EPS)
        vhat = gv * rstd
        gvec = g_ref[...]
        v = (vhat * gvec).astype(BF16)
        dsv = dmain * u
        dsv_b = dsv.astype(BF16)
        mask = _tril_mask()
        dv_parts = []
        for grp in range(G_HEADS):
            cols = slice(grp * HEAD, (grp + 1) * HEAD)
            wm = jnp.where(mask, w_ref[grp], 0.0).astype(BF16)
            sv = _dot(wm, v[:, cols], 'nn') + b_ref[:, grp:grp + 1]
            dz_ref[:, cols] = (dmain[:, cols] * sv * _gelu_grad(zu[:, cols])).astype(dz_ref.dtype)
            dwg = _dot(dsv_b[:, cols], v[:, cols], 'nt')
            dw_ref[grp] += jnp.where(mask, dwg, 0.0)
            db_ref[:, grp:grp + 1] += jnp.sum(dsv[:, cols], axis=-1, keepdims=True)
            dv_parts.append(_dot(wm, dsv_b[:, cols], 'tn'))
        dv = jnp.concatenate(dv_parts, axis=-1)
        dg_ref[...] += jnp.sum(dv * vhat, axis=0, keepdims=True)
        gdv = dv * gvec
        dgv = rstd * (gdv - vhat * jnp.mean(gdv * vhat, axis=-1, keepdims=True))
        dz_ref[:, G_W:2 * G_W] = (dgv * _gelu_grad(zv)).astype(dz_ref.dtype)
        dz_ref[:, 2 * G_W:] = dq_ref[...].astype(dz_ref.dtype)

    return pl.pallas_call(
        body, name=name, grid=(s // CHUNK,),
        in_specs=[pl.BlockSpec((CHUNK, G_W), lambda i: (i, 0)),
                  pl.BlockSpec((CHUNK, G_W), lambda i: (i, 1)),
                  pl.BlockSpec((CHUNK, G_W), lambda i: (i, 0)),
                  pl.BlockSpec((CHUNK, MEM_W), lambda i: (i, 0)),
                  pl.BlockSpec((1, G_W), lambda i: (0, 0)),
                  pl.BlockSpec((G_HEADS, CHUNK, CHUNK), lambda i: (0, 0, 0)),
                  pl.BlockSpec((CHUNK, G_HEADS), lambda i: (0, 0))],
        out_specs=[pl.BlockSpec((CHUNK, zw), lambda i: (i, 0)),
                   pl.BlockSpec((G_HEADS, CHUNK, CHUNK), lambda i: (0, 0, 0)),
                   pl.BlockSpec((CHUNK, G_HEADS), lambda i: (0, 0)),
                   pl.BlockSpec((1, G_W), lambda i: (0, 0))],
        out_shape=[jax.ShapeDtypeStruct((s, zw), BF16),
                   jax.ShapeDtypeStruct((G_HEADS, CHUNK, CHUNK), F32),
                   jax.ShapeDtypeStruct((CHUNK, G_HEADS), F32),
                   jax.ShapeDtypeStruct((1, G_W), F32)],
        compiler_params=_params(("arbitrary",)),
    )(z, z, dmix, dqm, g_v.reshape(1, G_W), w_sp, b_sp_t)


def _mem_probs(q, k):
    sc = _dot(q, k, 'nt') * (HEAD ** -0.5)
    sc = sc - jnp.max(sc, axis=-1, keepdims=True)
    e = jnp.exp(sc)
    return e / jnp.sum(e, axis=-1, keepdims=True)


def _memattn_fwd(z, kvm, main, *, qcol, name, tm=512):
    s = z.shape[0]
    m = kvm.shape[0]
    tm = min(tm, s)

    def body(q_ref, kv_ref, main_ref, o_ref):
        o_ref[:, :G_W] = main_ref[...]
        for h in range(MEM_HEADS):
            cols = slice(h * HEAD, (h + 1) * HEAD)
            k = kv_ref[:, cols]
            v = kv_ref[:, MEM_W + h * HEAD:MEM_W + (h + 1) * HEAD]
            p = _mem_probs(q_ref[:, cols], k)
            o_ref[:, G_W + h * HEAD:G_W + (h + 1) * HEAD] = _dot(p.astype(BF16), v, 'nn').astype(o_ref.dtype)

    return pl.pallas_call(
        body, name=name, grid=(s // tm,),
        in_specs=[pl.BlockSpec((tm, MEM_W), lambda i: (i, qcol)),
                  pl.BlockSpec((m, 2 * MEM_W), lambda i: (0, 0)),
                  pl.BlockSpec((tm, G_W), lambda i: (i, 0))],
        out_specs=pl.BlockSpec((tm, G_W + MEM_W), lambda i: (i, 0)),
        out_shape=jax.ShapeDtypeStruct((s, G_W + MEM_W), BF16),
        compiler_params=_params(("parallel",)),
    )(z, kvm, main)


def _memattn_bwd(z, kvm, dmix, *, qcol, name, tm=512):
    s = z.shape[0]
    m = kvm.shape[0]
    tm = min(tm, s)
    scale = HEAD ** -0.5

    def body(q_ref, kv_ref, do_ref, dq_ref, dkv_ref):
        @pl.when(pl.program_id(0) == 0)
        def _():
            dkv_ref[...] = jnp.zeros_like(dkv_ref)

        for h in range(MEM_HEADS):
            cols = slice(h * HEAD, (h + 1) * HEAD)
            vcols = slice(MEM_W + h * HEAD, MEM_W + (h + 1) * HEAD)
            q = q_ref[:, cols]
            k = kv_ref[:, cols]
            v = kv_ref[:, vcols]
            do = do_ref[:, cols]
            p = _mem_probs(q, k)
            dp = _dot(do, v, 'nt')
            ds = (p * (dp - jnp.sum(dp * p, axis=-1, keepdims=True)) * scale).astype(BF16)
            dq_ref[:, cols] = _dot(ds, k, 'nn').astype(dq_ref.dtype)
            dkv_ref[:, cols] += _dot(ds, q, 'tn')
            dkv_ref[:, vcols] += _dot(p.astype(BF16), do, 'tn')

    mo_block = G_W // MEM_W
    return pl.pallas_call(
        body, name=name, grid=(s // tm,),
        in_specs=[pl.BlockSpec((tm, MEM_W), lambda i: (i, qcol)),
                  pl.BlockSpec((m, 2 * MEM_W), lambda i: (0, 0)),
                  pl.BlockSpec((tm, MEM_W), lambda i: (i, mo_block))],
        out_specs=[pl.BlockSpec((tm, MEM_W), lambda i: (i, 0)),
                   pl.BlockSpec((m, 2 * MEM_W), lambda i: (0, 0))],
        out_shape=[jax.ShapeDtypeStruct((s, MEM_W), BF16), jax.ShapeDtypeStruct((m, 2 * MEM_W), F32)],
        compiler_params=_params(("arbitrary",)),
    )(z, kvm, dmix)


def _rope(x1, x2, cos, sin, *, name, inverse=False, out_dtype=BF16, col1=0, col2=0, tm=512):
    s, w = cos.shape
    tm = min(tm, s)
    sign = -1.0 if inverse else 1.0

    def body(a_ref, b_ref, c_ref, s_ref, o1_ref, o2_ref):
        a = a_ref[...].astype(F32)
        b = b_ref[...].astype(F32)
        c = c_ref[...]
        sn = s_ref[...] * sign
        o1_ref[...] = (a * c - b * sn).astype(o1_ref.dtype)
        o2_ref[...] = (b * c + a * sn).astype(o2_ref.dtype)

    row = pl.BlockSpec((tm, w), lambda i: (i, 0))
    return pl.pallas_call(
        body, name=name, grid=(s // tm,),
        in_specs=[pl.BlockSpec((tm, w), lambda i: (i, col1)), pl.BlockSpec((tm, w), lambda i: (i, col2)), row, row],
        out_specs=[row, row],
        out_shape=[jax.ShapeDtypeStruct((s, w), out_dtype)] * 2,
        compiler_params=_params(("parallel",)),
    )(x1, x2, cos, sin)


MHA_BLOCK = 1024


def _mha_scores(q1, q2, k1, k2, scale, diagonal):
    sc = (_dot(q1, k1, 'nt') + _dot(q2, k2, 'nt')) * scale
    if not diagonal:
        return sc, None
    rows = lax.broadcasted_iota(jnp.int32, sc.shape, 0)
    cols = lax.broadcasted_iota(jnp.int32, sc.shape, 1)
    return sc, cols <= rows


def _mha_fwd(qp, qr, kn, kr, vv, *, name):
    s = kn.shape[0]
    tb = min(MHA_BLOCK, s)
    scale = (HEAD + ROPE_DIM) ** -0.5

    def body(qn_ref, qr_ref, kn_ref, kr_ref, v_ref, o_ref, lse_ref, m_ref, l_ref, acc_ref):
        i = pl.program_id(1)
        q1, q2 = qn_ref[...], qr_ref[...]
        m_ref[...] = jnp.full_like(m_ref, MASK_VALUE)
        l_ref[...] = jnp.zeros_like(l_ref)
        acc_ref[...] = jnp.zeros_like(acc_ref)

        def block(j, diagonal):
            ks = pl.multiple_of(j * tb, tb)
            k1, k2, vj = kn_ref[pl.ds(ks, tb), :], kr_ref[pl.ds(ks, tb), :], v_ref[pl.ds(ks, tb), :]
            sc, keep = _mha_scores(q1, q2, k1, k2, scale, diagonal)
            if diagonal:
                sc = jnp.where(keep, sc, MASK_VALUE)
            m_old = m_ref[...]
            m_new = jnp.maximum(m_old, jnp.max(sc, axis=-1, keepdims=True))
            p = jnp.exp(sc - m_new)
            alpha = jnp.exp(m_old - m_new)
            l_ref[...] = alpha * l_ref[...] + jnp.sum(p, axis=-1, keepdims=True)
            acc_ref[...] = alpha * acc_ref[...] + _dot(p.astype(BF16), vj, 'nn')
            m_ref[...] = m_new

        def step(j, carry):
            block(j, False)
            return carry

        lax.fori_loop(0, i, step, 0)
        block(i, True)
        l = l_ref[...]
        o_ref[...] = (acc_ref[...] / l).astype(o_ref.dtype)
        lse_ref[...] = m_ref[...] + jnp.log(l)

    return pl.pallas_call(
        body, name=name, grid=(G_HEADS, s // tb),
        in_specs=[pl.BlockSpec((tb, HEAD), lambda h, i: (i, h)),
                  pl.BlockSpec((None, tb, ROPE_DIM), lambda h, i: (h, i, 0)),
                  pl.BlockSpec((s, HEAD), lambda h, i: (0, h)),
                  pl.BlockSpec((s, ROPE_DIM), lambda h, i: (0, 0)),
                  pl.BlockSpec((s, HEAD), lambda h, i: (0, h))],
        out_specs=[pl.BlockSpec((tb, HEAD), lambda h, i: (i, h)),
                   pl.BlockSpec((None, tb, 1), lambda h, i: (h, i, 0))],
        out_shape=[jax.ShapeDtypeStruct((s, G_W), BF16), jax.ShapeDtypeStruct((G_HEADS, s, 1), F32)],
        scratch_shapes=[pltpu.VMEM((tb, 1), F32), pltpu.VMEM((tb, 1), F32), pltpu.VMEM((tb, HEAD), F32)],
        compiler_params=_params(("parallel", "arbitrary")),
    )(qp, qr, kn, kr, vv)


def _mha_bwd(qp, qr, kn, kr, vv, o, do, lse, *, name):
    s = kn.shape[0]
    tb = min(MHA_BLOCK, s)
    nq = s // tb
    scale = (HEAD + ROPE_DIM) ** -0.5

    def body(qn_ref, qr_ref, kn_ref, kr_ref, v_ref, o_ref, do_ref, lse_ref,
             dqn_ref, dqr_ref, dkn_ref, dkr_ref, dv_ref, dq1_ref, dq2_ref):
        h, i = pl.program_id(0), pl.program_id(1)

        @pl.when(i == 0)
        def _():
            dkn_ref[...] = jnp.zeros_like(dkn_ref)
            dv_ref[...] = jnp.zeros_like(dv_ref)

        @pl.when(jnp.logical_and(h == 0, i == 0))
        def _():
            dkr_ref[...] = jnp.zeros_like(dkr_ref)

        q1, q2, dov = qn_ref[...], qr_ref[...], do_ref[...]
        delta = jnp.sum(dov.astype(F32) * o_ref[...].astype(F32), axis=-1, keepdims=True)
        lsev = lse_ref[...]
        dq1_ref[...] = jnp.zeros_like(dq1_ref)
        dq2_ref[...] = jnp.zeros_like(dq2_ref)

        def block(j, diagonal):
            ks = pl.multiple_of(j * tb, tb)
            k1, k2, vj = kn_ref[pl.ds(ks, tb), :], kr_ref[pl.ds(ks, tb), :], v_ref[pl.ds(ks, tb), :]
            sc, keep = _mha_scores(q1, q2, k1, k2, scale, diagonal)
            p = jnp.exp(sc - lsev)
            if diagonal:
                p = jnp.where(keep, p, 0.0)
            dp = _dot(dov, vj, 'nt')
            ds = (p * (dp - delta) * scale).astype(BF16)
            pb = p.astype(BF16)
            dq1_ref[...] += _dot(ds, k1, 'nn')
            dq2_ref[...] += _dot(ds, k2, 'nn')
            dkn_ref[pl.ds(ks, tb), :] += _dot(ds, q1, 'tn')
            dkr_ref[pl.ds(ks, tb), :] += _dot(ds, q2, 'tn')
            dv_ref[pl.ds(ks, tb), :] += _dot(pb, dov, 'tn')

        def step(j, carry):
            block(j, False)
            return carry

        lax.fori_loop(0, i, step, 0)
        block(i, True)
        dqn_ref[...] = dq1_ref[...].astype(dqn_ref.dtype)
        dqr_ref[...] = dq2_ref[...].astype(dqr_ref.dtype)

    tile = pl.BlockSpec((tb, HEAD), lambda h, i: (i, h))
    rope = pl.BlockSpec((None, tb, ROPE_DIM), lambda h, i: (h, i, 0))
    head_cols = pl.BlockSpec((s, HEAD), lambda h, i: (0, h))
    shared = pl.BlockSpec((s, ROPE_DIM), lambda h, i: (0, 0))
    return pl.pallas_call(
        body, name=name, grid=(G_HEADS, nq),
        in_specs=[tile, rope, head_cols, shared, head_cols, tile, tile, pl.BlockSpec((None, tb, 1), lambda h, i: (h, i, 0))],
        out_specs=[tile, rope, head_cols, shared, head_cols],
        out_shape=[jax.ShapeDtypeStruct((s, G_W), BF16), jax.ShapeDtypeStruct((G_HEADS, s, ROPE_DIM), BF16),
                   jax.ShapeDtypeStruct((s, G_W), F32), jax.ShapeDtypeStruct((s, ROPE_DIM), F32),
                   jax.ShapeDtypeStruct((s, G_W), F32)],
        scratch_shapes=[pltpu.VMEM((tb, HEAD), F32), pltpu.VMEM((tb, ROPE_DIM), F32)],
        compiler_params=_params(("arbitrary", "arbitrary")),
    )(qp, qr, kn, kr, vv, o, do, lse)


HALO = 16


def _shift_down(prev, cur, shift, first_tile):
    tr = cur.shape[0]
    full = jnp.concatenate([prev, cur], axis=0)
    out = pltpu.roll(full, shift, axis=0)[HALO:]
    row = lax.broadcasted_iota(jnp.int32, (tr, 1), 0)
    return jnp.where(jnp.logical_and(first_tile, row < shift), 0.0, out)


def _shift_up(cur, nxt, shift, last_tile):
    tr = cur.shape[0]
    full = jnp.concatenate([cur, nxt], axis=0)
    out = pltpu.roll(full, tr + HALO - shift, axis=0)[:tr]
    row = lax.broadcasted_iota(jnp.int32, (tr, 1), 0)
    return jnp.where(jnp.logical_and(last_tile, row >= tr - shift), 0.0, out)


def _lane_chunks(width, lanes):
    return [slice(c0, min(c0 + lanes, width)) for c0 in range(0, width, lanes)]


def _conv_taps(prev_ref, cur_ref, cw_ref, cb_ref, first_tile, cs):
    cur = cur_ref[:, cs].astype(F32)
    prev = prev_ref[:, cs].astype(F32)
    a1 = _shift_down(prev, cur, 1, first_tile)
    a2 = _shift_down(prev, cur, 2, first_tile)
    c = a2 * cw_ref[0:1, cs] + a1 * cw_ref[1:2, cs] + cur * cw_ref[2:3, cs] + cb_ref[:, cs]
    return c, (a2, a1, cur)


def _conv_in_specs(tr, bw, half, layer, row_of, blk_of):
    per = tr // HALO
    specs = []
    for off in (0, half):
        specs.append(pl.BlockSpec((None, HALO, bw), lambda *g, off=off: (blk_of(*g) + off, jnp.maximum(row_of(*g) * per - 1, 0), 0)))
        specs.append(pl.BlockSpec((None, tr, bw), lambda *g, off=off: (blk_of(*g) + off, row_of(*g), 0)))
    for off in (0, half):
        specs.append(pl.BlockSpec((None, None, CONV_W, bw), lambda *g, off=off: (blk_of(*g) + off, layer, 0, 0)))
    for off in (0, half):
        specs.append(pl.BlockSpec((None, 1, bw), lambda *g, off=off: (layer * 2 * half + blk_of(*g) + off, 0, 0)))
    return specs


def _conv_fwd(a, cw, cb, layer, *, name, tr=256):
    nb, s, bw = a.shape
    half = nb // 2
    tr = min(tr, s)

    def body(gp_ref, gc_ref, vp_ref, vc_ref, cwg_ref, cwv_ref, cbg_ref, cbv_ref, o_ref):
        first = pl.program_id(0) == 0
        for cs in _lane_chunks(bw, 256):
            gate, _ = _conv_taps(gp_ref, gc_ref, cwg_ref, cbg_ref, first, cs)
            val, _ = _conv_taps(vp_ref, vc_ref, cwv_ref, cbv_ref, first, cs)
            o_ref[:, cs] = (gate * _sigmoid(gate) * val).astype(o_ref.dtype)

    return pl.pallas_call(
        body, name=name, grid=(s // tr, half),
        in_specs=_conv_in_specs(tr, bw, half, layer, lambda i, j: i, lambda i, j: j),
        out_specs=pl.BlockSpec((None, tr, bw), lambda i, j: (j, i, 0)),
        out_shape=jax.ShapeDtypeStruct((half, s, bw), BF16),
        compiler_params=_params(("parallel", "parallel")),
    )(a, a, a, a, cw, cw, cb, cb)


def _conv_bwd_dc(a, dact, cw, cb, layer, *, name, after=None, tr=256):
    nb, s, bw = a.shape
    half = nb // 2
    tr = min(tr, s)

    def body(*refs):
        gp_ref, gc_ref, vp_ref, vc_ref, cwg_ref, cwv_ref, cbg_ref, cbv_ref, da_ref = refs[:9]
        dc_ref, dw_ref, db_ref = refs[-3:]
        first = pl.program_id(1) == 0

        @pl.when(first)
        def _():
            dw_ref[...] = jnp.zeros_like(dw_ref)
            db_ref[...] = jnp.zeros_like(db_ref)

        for cs in _lane_chunks(bw, 128):
            gate, gtaps = _conv_taps(gp_ref, gc_ref, cwg_ref, cbg_ref, first, cs)
            val, vtaps = _conv_taps(vp_ref, vc_ref, cwv_ref, cbv_ref, first, cs)
            dact_v = da_ref[:, cs].astype(F32)
            sg = _sigmoid(gate)
            dgate = dact_v * val * (sg * (1.0 + gate * (1.0 - sg)))
            dval = dact_v * (gate * sg)
            dc_ref[0, :, cs] = dgate.astype(dc_ref.dtype)
            dc_ref[1, :, cs] = dval.astype(dc_ref.dtype)
            for kk in range(CONV_W):
                dw_ref[0, kk:kk + 1, cs] += jnp.sum(dgate * gtaps[kk], axis=0, keepdims=True)
                dw_ref[1, kk:kk + 1, cs] += jnp.sum(dval * vtaps[kk], axis=0, keepdims=True)
            db_ref[0, :, cs] += jnp.sum(dgate, axis=0, keepdims=True)
            db_ref[1, :, cs] += jnp.sum(dval, axis=0, keepdims=True)

    outs = pl.pallas_call(
        body, name=name, grid=(half, s // tr),
        in_specs=_conv_in_specs(tr, bw, half, layer, lambda j, i: i, lambda j, i: j)
        + [pl.BlockSpec((None, tr, bw), lambda j, i: (j, i, 0))]
        + ([pl.BlockSpec(memory_space=pl.ANY)] if after is not None else []),
        out_specs=[pl.BlockSpec((2, None, tr, bw), lambda j, i: (0, j, i, 0)),
                   pl.BlockSpec((2, None, CONV_W, bw), lambda j, i: (0, j, 0, 0)),
                   pl.BlockSpec((2, None, 1, bw), lambda j, i: (0, j, 0, 0))],
        out_shape=[jax.ShapeDtypeStruct((2, half, s, bw), BF16),
                   jax.ShapeDtypeStruct((2, half, CONV_W, bw), F32),
                   jax.ShapeDtypeStruct((2, half, 1, bw), F32)],
        compiler_params=_params(("parallel", "arbitrary")),
    )(a, a, a, a, cw, cw, cb, cb, dact, *([after] if after is not None else []))
    dc, dw, db = outs
    return dc.reshape(nb, s, bw), dw.reshape(nb, CONV_W, bw), db.reshape(nb, 1, bw)


def _conv_bwd_da(dc, cw, layer, *, name, tr=512):
    nb, s, bw = dc.shape
    tr = min(tr, s)
    ni = s // tr
    per = tr // HALO
    last_halo = s // HALO - 1

    def body(c_ref, n_ref, w_ref, o_ref):
        last = pl.program_id(0) == ni - 1
        for cs in _lane_chunks(bw, 256):
            cur = c_ref[:, cs].astype(F32)
            nxt = n_ref[:, cs].astype(F32)
            da = (cur * w_ref[2:3, cs] + _shift_up(cur, nxt, 1, last) * w_ref[1:2, cs]
                  + _shift_up(cur, nxt, 2, last) * w_ref[0:1, cs])
            o_ref[:, cs] = da.astype(o_ref.dtype)

    tile = pl.BlockSpec((None, tr, bw), lambda i, j: (j, i, 0))
    return pl.pallas_call(
        body, name=name, grid=(ni, nb),
        in_specs=[tile,
                  pl.BlockSpec((None, HALO, bw), lambda i, j: (j, jnp.minimum((i + 1) * per, last_halo), 0)),
                  pl.BlockSpec((None, None, CONV_W, bw), lambda i, j: (j, layer, 0, 0))],
        out_specs=tile,
        out_shape=jax.ShapeDtypeStruct((nb, s, bw), BF16),
        compiler_params=_params(("parallel", "parallel")),
    )(dc, dc, cw)


def _rope_tables(positions):
    inv = 1.0 / (ROPE_THETA ** (jnp.arange(0, ROPE_DIM, 2, dtype=F32) / ROPE_DIM))
    ang = positions.astype(F32)[:, None] * inv
    return jnp.cos(ang), jnp.sin(ang)


def _heads_to_major(r1, r2):
    s = r1.shape[0]
    both = jnp.concatenate([r1.reshape(s, G_HEADS, ROPE_HALF), r2.reshape(s, G_HEADS, ROPE_HALF)], axis=-1)
    return both.transpose(1, 0, 2)


def _heads_from_major(qr):
    s = qr.shape[1]
    t = qr.transpose(1, 0, 2)
    return t[:, :, :ROPE_HALF].reshape(s, G_HEADS * ROPE_HALF), t[:, :, ROPE_HALF:].reshape(s, G_HEADS * ROPE_HALF)


def _local_step(x, mem, positions, target, rep, fetch, emit):
    s, d = x.shape
    n_b = DEPTH - N_A
    tm = min(1024, s)
    cos, sin = _rope_tables(positions)
    cos12 = jnp.tile(cos, (1, G_HEADS))
    sin12 = jnp.tile(sin, (1, G_HEADS))
    r1_col = G_W // (G_HEADS * ROPE_HALF)
    b_sp_t = rep['b_sp'].transpose(0, 2, 1)

    saved = []
    kv = None
    shared = None
    for l in range(DEPTH):
        wm = fetch(('in', l), x)
        if l == 0:
            shared = {'g_v': wm['g_v'], 'conv_w': wm['conv_w']}
            bw = shared['conv_w'].shape[-1]
            conv_b = rep['conv_b'].reshape(-1, 1, bw)
        sv = {'x_in': x, 'wm': wm}
        if l == N_A:
            xn_kv = _rmsnorm(x, rep['g_kv'], name="kvnorm")
            kvx = _mm(xn_kv, wm['w_kv_a'], dims='nn', out_dtype=F32, name="kvproj", tm=tm, tn=KV_PAD)
            ckv = _rmsnorm(kvx, rep['g_kv_lat'], width=KV_RANK, name="ckvnorm")
            k1, k2 = _rope(kvx[:, KV_RANK:KV_RANK + ROPE_HALF], kvx[:, KV_RANK + ROPE_HALF:KV_RANK + ROPE_DIM],
                           cos, sin, name="krope")
            kr = jnp.concatenate([k1, k2], axis=-1)
            kv = {'x': x, 'xn': xn_kv, 'kvx': kvx, 'ckv': ckv, 'kr': kr, 'w_kv_a': wm['w_kv_a']}
        h = _rmsnorm(x, rep['g_mix'][l], name=f"mixnorm{l}")
        if l < N_A:
            z = _mm(h, wm['w_in'], dims='nn', out_dtype=BF16, name=f"in_a{l}", tm=tm, tn=512)
            main = _sgu_fwd(z, shared['g_v'][l], rep['w_sp'][l], b_sp_t[l], name=f"sgu{l}")
            qcol = 2 * G_W // MEM_W
        else:
            j = l - N_A
            z = _mm(h, wm['w_in'], dims='nn', out_dtype=BF16, name=f"in_b{j}", tm=tm, tn=1024)
            qn = _rmsnorm(z, rep['g_q_lat'][j], width=Q_RANK, name=f"qnorm{j}")
            qp = _mm(qn, wm['w_uqp'], dims='nn', out_dtype=BF16, name=f"uq{j}", tm=tm, tn=768)
            rr1, rr2 = _rope(qp, qp, cos12, sin12, col1=r1_col, col2=r1_col + 1, name=f"qrope{j}")
            qr = _heads_to_major(rr1, rr2)
            kn = _mm(kv['ckv'], wm['w_uk'], dims='nn', out_dtype=BF16, name=f"k_up{j}", tm=tm, tn=768)
            vv = _mm(kv['ckv'], wm['w_uv'], dims='nn', out_dtype=BF16, name=f"v_up{j}", tm=tm, tn=768)
            main, lse = _mha_fwd(qp, qr, kn, kv['kr'], vv, name=f"mha{j}")
            qcol = Q_RANK // MEM_W
            sv.update(qn=qn, qp=qp, qr=qr, kn=kn, vv=vv, lse=lse)
        wm.update(fetch(('rest', l), z))
        memn = _rmsnorm(mem, rep['g_mem'][l], name=f"memnorm{l}")
        kvm = _mm(memn, wm['w_mem_kv'], dims='nn', out_dtype=BF16, name=f"memkv{l}", tm=tm, tn=1024)
        mix = _memattn_fwd(z, kvm, main, qcol=qcol, name=f"memattn{l}")
        x_mid = _mm(mix, wm['w_out'], dims='nn', res=x, out_dtype=F32, name=f"out{l}", tm=tm, tn=1024)
        wf = fetch(('up', l), x_mid)
        h2 = _rmsnorm(x_mid, rep['g_ffn'][l], name=f"ffnnorm{l}")
        a = _mm(h2, wf['w_up'], dims='nn', b_blocked=True, out_dtype=BF16, out_block=bw,
                name=f"up{l}", tm=tm, tn=bw)
        act = _conv_fwd(a, shared['conv_w'], conv_b, l, name=f"conv{l}")
        wf.update(fetch(('down', l), act))
        x = _mm(act, wf['w_down'], dims='nn', a_blocked=True, tk=bw, res=x_mid, out_dtype=F32,
                name=f"down{l}", tm=tm, tn=1024)
        sv.update(h=h, memn=memn, kvm=kvm, z=z, qcol=qcol, mix=mix, x_mid=x_mid, h2=h2, a=a, act=act, wf=wf)
        saved.append(sv)

    sq, dx, dg_final = _final_loss(x, target, rep['g_final'], name="loss")

    g = {k: [None] * DEPTH for k in ('g_mix', 'g_ffn', 'g_mem', 'conv_w', 'conv_b')}
    for k in ('g_v', 'w_sp', 'b_sp'):
        g[k] = [None] * N_A
    g['g_q_lat'] = [None] * n_b
    g['g_final'] = dg_final
    dckv_sum, dkr_sum = None, None

    for l in reversed(range(DEPTH)):
        sv = saved[l]
        wm, wf = sv['wm'], sv['wf']
        dact = _mm(dx, wf['w_down'], dims='nt', out_dtype=BF16, out_block=bw, name=f"d_act{l}", tm=tm, tn=bw)
        dw_down = _mm(sv['act'], dx, dims='tn', a_blocked=True, out_dtype=BF16, name=f"dw_down{l}", tm=bw, tn=512)
        tok = emit(('down', l), {'w_ffn_down': dw_down})
        dc, dcw, dcb = _conv_bwd_dc(sv['a'], dact, shared['conv_w'], conv_b, l, after=tok, name=f"d_conv{l}")
        g['conv_w'][l], g['conv_b'][l] = dcw, dcb
        da = _conv_bwd_da(dc, shared['conv_w'], l, name=f"d_convin{l}")
        dw_up = _mm(sv['h2'], da, dims='tn', b_blocked=True, out_dtype=BF16, out_block=bw,
                    name=f"dw_up{l}", tm=512, tn=bw, n_outer=True)
        tok = emit(('up', l), {'w_ffn_up': dw_up})
        dh2 = _mm(da, wf['w_up'], dims='nt', a_blocked=True, b_blocked=True, tk=bw,
                  out_dtype=BF16, name=f"d_h2{l}", tm=tm, tn=1024, after=tok)
        dx_mid, g['g_ffn'][l] = _rmsnorm_bwd(sv['x_mid'], rep['g_ffn'][l], dh2, dres=dx, name=f"d_ffnnorm{l}")
        dmix = _mm(dx_mid, wm['w_out'], dims='nt', out_dtype=BF16, name=f"d_mix{l}", tm=tm, tn=1024)
        dw_out = _mm(sv['mix'], dx_mid, dims='tn', out_dtype=BF16, name=f"dw_out{l}", tm=1024, tn=256)
        dqm, dkvm = _memattn_bwd(sv['z'], sv['kvm'], dmix, qcol=sv['qcol'], name=f"d_memattn{l}")
        dw_memkv = _mm(sv['memn'], dkvm, dims='tn', out_dtype=BF16, name=f"dw_memkv{l}", tm=1024, tn=1024)
        tok = emit(('rest', l), {'w_out': dw_out, 'w_mem_kv': dw_memkv})
        gm = {}
        dmemn = _mm(dkvm, wm['w_mem_kv'], dims='nt', out_dtype=F32, name=f"d_memn{l}", tm=tm, tn=1024, after=tok)
        _, g['g_mem'][l] = _rmsnorm_bwd(mem, rep['g_mem'][l], dmemn, out_dtype=BF16, name=f"d_memnorm{l}")
        if l < N_A:
            dz, dwsp, dbsp_t, dgv = _sgu_bwd(sv['z'], dmix, dqm, shared['g_v'][l], rep['w_sp'][l], b_sp_t[l],
                                             name=f"d_sgu{l}")
            g['w_sp'][l], g['b_sp'][l], g['g_v'][l] = dwsp, dbsp_t.T, dgv
            dh = _mm(dz, wm['w_in'], dims='nt', out_dtype=BF16, name=f"d_h_a{l}", tm=tm, tn=1024)
            gm['w_in_a'] = _mm(sv['h'], dz, dims='tn', out_dtype=BF16, name=f"dw_in_a{l}", tm=1024, tn=512)
        else:
            j = l - N_A
            dq_nope, dqr, dkn, dkr, dvv = _mha_bwd(sv['qp'], sv['qr'], sv['kn'], kv['kr'], sv['vv'], sv['mix'], dmix,
                                                   sv['lse'], name=f"d_mha{j}")
            gm['w_uk'] = _mm(kv['ckv'], dkn, dims='tn', out_dtype=BF16, name=f"dw_uk{j}", tm=512, tn=768)
            gm['w_uv'] = _mm(kv['ckv'], dvv, dims='tn', out_dtype=BF16, name=f"dw_uv{j}", tm=512, tn=768)
            dckv = _mm(dkn, wm['w_uk'], dims='nt', out_dtype=F32, res=dckv_sum, name=f"d_ckv_k{j}", tm=tm, tn=512)
            dckv_sum = _mm(dvv, wm['w_uv'], dims='nt', out_dtype=F32, res=dckv, name=f"d_ckv_v{j}", tm=tm, tn=512)
            dkr_sum = dkr if dkr_sum is None else dkr_sum + dkr
            dr1, dr2 = _heads_from_major(dqr)
            dq1, dq2 = _rope(dr1, dr2, cos12, sin12, inverse=True, name=f"d_qrope{j}")
            dqp = jnp.concatenate([dq_nope, dq1, dq2], axis=-1)
            dqn = _mm(dqp, wm['w_uqp'], dims='nt', out_dtype=BF16, name=f"d_qn{j}", tm=tm, tn=512)
            gm['w_uqp'] = _mm(sv['qn'], dqp, dims='tn', out_dtype=BF16, name=f"dw_uq{j}", tm=512, tn=768)
            dqlat, g['g_q_lat'][j] = _rmsnorm_bwd(sv['z'], rep['g_q_lat'][j], dqn, width=Q_RANK, out_dtype=BF16,
                                                 name=f"d_qnorm{j}")
            dz = jnp.concatenate([dqlat, dqm], axis=-1)
            dh = _mm(dz, wm['w_in'], dims='nt', out_dtype=BF16, name=f"d_h_b{j}", tm=tm, tn=1024)
            gm['w_in_b'] = _mm(sv['h'], dz, dims='tn', out_dtype=BF16, name=f"dw_in_b{j}", tm=1024, tn=512)
        tok = emit(('mix', l), gm)
        dx, g['g_mix'][l] = _rmsnorm_bwd(sv['x_in'], rep['g_mix'][l], dh, dres=dx_mid, after=tok, name=f"d_mixnorm{l}")
        if l == N_A:
            dkvx_c, g['g_kv_lat'] = _rmsnorm_bwd(kv['kvx'], rep['g_kv_lat'], dckv_sum, width=KV_RANK, out_dtype=BF16,
                                                 name="d_ckvnorm")
            dk1, dk2 = _rope(dkr_sum[:, :ROPE_HALF], dkr_sum[:, ROPE_HALF:], cos, sin, inverse=True, name="d_krope")
            dkvx = jnp.concatenate([dkvx_c, dk1, dk2, jnp.zeros((s, KV_PAD - KV_RANK - ROPE_DIM), BF16)], axis=-1)
            dxn = _mm(dkvx, kv['w_kv_a'], dims='nt', out_dtype=BF16, name="d_kvnorm_in", tm=tm, tn=1024)
            dw_kv = _mm(kv['xn'], dkvx, dims='tn', out_dtype=BF16, name="dw_kv", tm=1024, tn=KV_PAD)
            tok = emit(('kv', 0), {'w_kv_a': dw_kv})
            dx, g['g_kv'] = _rmsnorm_bwd(kv['x'], rep['g_kv'], dxn, dres=dx, after=tok, name="d_kvnorm")
    return jnp.sum(sq), dx, g


MESH_IDS = pl.DeviceIdType.MESH
PEER_MASKS = tuple((k >> 2 & 1, k >> 1 & 1, k & 1) for k in range(1, N_DEV))
CHIP_MASKS = ((1, 0), (0, 1), (1, 1))
N_PEER = N_DEV - 1
SEMS_PER_BUFFER = 2 * N_PEER + 1
DATAFLOW = pltpu.SideEffectType.DATAFLOW_SIDE_EFFECTING
HBM_SPEC = pl.BlockSpec(memory_space=pltpu.HBM)
SEM_SPEC = pl.BlockSpec(memory_space=pltpu.SEMAPHORE)


def _my_position():
    return lax.axis_index("x"), lax.axis_index("y"), lax.axis_index("c")


def _flip(pos, mask):
    return tuple(1 - p if f else p for p, f in zip(pos, mask))


def _linear_id(pos):
    return 4 * pos[0] + 2 * pos[1] + pos[2]


def _hbm(x):
    return pltpu.with_memory_space_constraint(x, pltpu.HBM)


def _buffer_copies(src_ref, lead, land_ref, sems, scatter, near=False):
    me = _my_position()
    my_id = _linear_id(me)
    src = src_ref.at[lead] if lead else src_ref
    own = pltpu.make_async_copy(src.at[my_id] if scatter else src, land_ref.at[my_id], sems.at[2 * N_PEER])
    pairs = []
    for k, mask in enumerate(PEER_MASKS):
        if near and mask[2] == 1 and mask != (0, 0, 1):
            continue
        peer = _flip(me, mask)
        peer_id = _linear_id(peer)
        block = src.at[peer_id] if scatter else src
        send = pltpu.make_async_remote_copy(src_ref=block, dst_ref=land_ref.at[my_id], send_sem=sems.at[k],
                                            recv_sem=sems.at[N_PEER + k], device_id=peer, device_id_type=MESH_IDS)
        arrival = pltpu.make_async_remote_copy(src_ref=block, dst_ref=land_ref.at[peer_id], send_sem=sems.at[k],
                                               recv_sem=sems.at[N_PEER + k], device_id=peer, device_id_type=MESH_IDS)
        pairs.append((send, arrival))
    return own, pairs


def _exchange_start(srcs, buffers, *, name, scatter):
    ns, nb = len(srcs), len(buffers)
    lands = [_hbm(lax.empty((N_DEV,) + tuple(shape), dtype)) for _, _, shape, dtype, _ in buffers]

    def body(*refs):
        src_refs, land_refs = refs[:ns], refs[ns:ns + nb]
        sem_refs = refs[ns + nb:ns + 2 * nb]
        token = refs[-1]
        for b, (si, lead, _, _, near) in enumerate(buffers):
            own, pairs = _buffer_copies(src_refs[si], lead, land_refs[b], sem_refs[b], scatter, near)
            own.start()
            for send, _ in pairs:
                send.start()
        token[...] = jnp.zeros_like(token)

    out_shape = ([pltpu.SemaphoreType.DMA((SEMS_PER_BUFFER,))] * nb
                 + [pltpu.HBM(a.shape, a.dtype) for a in srcs]
                 + [pltpu.HBM(a.shape, a.dtype) for a in lands]
                 + [jax.ShapeDtypeStruct((8, 128), F32)])
    aliases = {i: nb + i for i in range(ns + nb)}
    outs = pl.pallas_call(
        body, name=name, in_specs=[HBM_SPEC] * (ns + nb),
        out_specs=[SEM_SPEC] * nb + [HBM_SPEC] * (ns + nb) + [pl.BlockSpec(memory_space=pltpu.VMEM)],
        out_shape=out_shape, input_output_aliases=aliases,
        compiler_params=pltpu.CompilerParams(has_side_effects=DATAFLOW),
    )(*[_hbm(a) for a in srcs], *lands)
    sems = list(outs[:nb])
    src_thru = list(outs[nb:nb + ns])
    land_thru = list(outs[nb + ns:nb + ns + nb])
    return sems, land_thru, src_thru, outs[-1]


def _exchange_wait(srcs_thru, buffers, sems, lands, after, *, name, scatter):
    ns, nb = len(srcs_thru), len(buffers)
    has_after = after is not None

    def body(*refs):
        src_refs, land_refs = refs[:ns], refs[ns:ns + nb]
        sem_refs = refs[ns + nb:ns + 2 * nb]
        for b, (si, lead, _, _, near) in enumerate(buffers):
            own, pairs = _buffer_copies(src_refs[si], lead, land_refs[b], sem_refs[b], scatter, near)
            for send, arrival in pairs:
                send.wait_send()
                arrival.wait_recv()
            own.wait()

    operands = list(srcs_thru) + list(lands) + list(sems) + ([after] if has_after else [])
    in_specs = ([HBM_SPEC] * (ns + nb) + [SEM_SPEC] * nb + ([pl.BlockSpec(memory_space=pl.ANY)] if has_after else []))
    outs = pl.pallas_call(
        body, name=name, in_specs=in_specs, out_specs=[HBM_SPEC] * nb,
        out_shape=[pltpu.HBM(a.shape, a.dtype) for a in lands],
        input_output_aliases={ns + b: b for b in range(nb)},
        compiler_params=pltpu.CompilerParams(has_side_effects=DATAFLOW),
    )(*operands)
    return list(outs)


def _exchange(arrays, *, name, scatter, near=None, after=None):
    n = len(arrays)
    near = [False] * n if near is None else near
    extra = [] if after is None else [after]
    out_shapes = [jax.ShapeDtypeStruct(a.shape if scatter else (N_DEV,) + a.shape, a.dtype) for a in arrays]

    def body(*refs):
        srcs, outs, sems = refs[:n], refs[n + len(extra):2 * n + len(extra)], refs[2 * n + len(extra):]
        started = []
        for a in range(n):
            own, pairs = _buffer_copies(srcs[a], (), outs[a], sems[a], scatter, near[a])
            own.start()
            for send, _ in pairs:
                send.start()
            started.append((own, pairs))
        for own, pairs in started:
            for send, arrival in pairs:
                arrival.wait_recv()
                send.wait_send()
            own.wait()

    any_spec = pl.BlockSpec(memory_space=pl.ANY)
    outs = pl.pallas_call(
        body, name=name, in_specs=[any_spec] * (n + len(extra)), out_specs=[any_spec] * n, out_shape=out_shapes,
        scratch_shapes=[pltpu.SemaphoreType.DMA((SEMS_PER_BUFFER,))] * n,
    )(*arrays, *extra)
    return list(outs)


def _forward_to_sibling(lands, *, name):
    n = len(lands)

    def body(*refs):
        ins, outs, sems = refs[:n], refs[n:2 * n], refs[2 * n:]
        me = _my_position()
        sibling = _flip(me, (0, 0, 1))
        pairs = []
        for b in range(n):
            for k, (fx, fy) in enumerate(CHIP_MASKS):
                mine = _linear_id(_flip(me, (fx, fy, 0)))
                theirs = _linear_id(_flip(me, (fx, fy, 1)))
                send = pltpu.make_async_remote_copy(
                    src_ref=ins[b].at[mine], dst_ref=outs[b].at[mine], send_sem=sems[b].at[k],
                    recv_sem=sems[b].at[len(CHIP_MASKS) + k], device_id=sibling, device_id_type=MESH_IDS)
                arrival = pltpu.make_async_remote_copy(
                    src_ref=ins[b].at[mine], dst_ref=outs[b].at[theirs], send_sem=sems[b].at[k],
                    recv_sem=sems[b].at[len(CHIP_MASKS) + k], device_id=sibling, device_id_type=MESH_IDS)
                send.start()
                pairs.append((send, arrival))
        for send, arrival in pairs:
            arrival.wait_recv()
            send.wait_send()

    any_spec = pl.BlockSpec(memory_space=pl.ANY)
    outs = pl.pallas_call(
        body, name=name, in_specs=[any_spec] * n, out_specs=[any_spec] * n,
        out_shape=[jax.ShapeDtypeStruct(a.shape, a.dtype) for a in lands],
        input_output_aliases={b: b for b in range(n)},
        scratch_shapes=[pltpu.SemaphoreType.DMA((2 * len(CHIP_MASKS),))] * n,
    )(*lands)
    return list(outs)


def _sum_slots(parts_ref):
    total = parts_ref[0].astype(F32)
    for q in range(1, parts_ref.shape[0]):
        total = total + parts_ref[q].astype(F32)
    return total


def _row_tile(rows, cols, n_arrays):
    budget = (12 * 1024 * 1024) // (4 * n_arrays * max(cols, 128))
    t = rows
    while t > budget and t % 2 == 0 and (t // 2) % 16 == 0:
        t //= 2
    return t


def _sum_adam(parts, w, m, v, layer, outs, *, name):
    q, r, c = parts.shape
    nl = w.shape[0]
    tr = _row_tile(r, c, q + 7)
    c1 = 1.0 - ADAM_B1 ** ADAM_STEP
    c2 = 1.0 - ADAM_B2 ** ADAM_STEP
    if outs is None:
        outs = [lax.empty((nl, r, c), F32) for _ in range(4)]

    def body(p_ref, w_ref, m_ref, v_ref, g_in, d_in, mo_in, vo_in, g_ref, d_ref, mo_ref, vo_ref):
        grad = _sum_slots(p_ref)
        m_new = ADAM_B1 * m_ref[...] + (1.0 - ADAM_B1) * grad
        v_new = ADAM_B2 * v_ref[...] + (1.0 - ADAM_B2) * (grad * grad)
        m_hat = m_new / c1
        v_hat = v_new / c2
        g_ref[...] = grad
        d_ref[...] = -ADAM_LR * (m_hat / (jnp.sqrt(v_hat) + ADAM_EPS) + ADAM_WD * w_ref[...])
        mo_ref[...] = m_new
        vo_ref[...] = v_new

    tile = pl.BlockSpec((None, tr, c), lambda i: (layer, i, 0))
    any_spec = pl.BlockSpec(memory_space=pl.ANY)
    return pl.pallas_call(
        body, name=name, grid=(r // tr,),
        in_specs=[pl.BlockSpec((q, tr, c), lambda i: (0, i, 0)), tile, tile, tile] + [any_spec] * 4,
        out_specs=[tile] * 4, out_shape=[jax.ShapeDtypeStruct((nl, r, c), F32)] * 4,
        input_output_aliases={4: 0, 5: 1, 6: 2, 7: 3},
        compiler_params=_params(("parallel",)),
    )(parts, w, m, v, *outs)


def _sum_parts(parts, *, name):
    q, r, c = parts.shape

    def body(p_ref, o_ref):
        o_ref[...] = _sum_slots(p_ref)

    return pl.pallas_call(
        body, name=name, in_specs=[pl.BlockSpec((q, r, c), lambda: (0, 0, 0))],
        out_specs=pl.BlockSpec((r, c), lambda: (0, 0)), out_shape=jax.ShapeDtypeStruct((r, c), F32),
        compiler_params=_params(),
    )(parts)


INPUT_NAMES = (['x', 'mem', 'positions'] + WEIGHTS + ['loss_target'] + ['m_' + n for n in WEIGHTS]
               + ['v_' + n for n in WEIGHTS])
SMALL_ALIGN = N_DEV * 8 * 128
TWO_LEVEL_LAYERS = N_A
GROUP_ORDER = ('in', 'rest', 'up', 'down')
GROUP_WEIGHTS = {'in': (['w_in_a'], ['w_in_b', 'w_uq', 'w_uk', 'w_uv']), 'rest': (['w_mem_kv', 'w_out'],) * 2,
                 'up': (['w_ffn_up'],) * 2, 'down': (['w_ffn_down'],) * 2}
LAYERED = {'w_in_a': 0, 'w_in_b': N_A, 'w_uq': N_A, 'w_uk': N_A, 'w_uv': N_A, 'w_mem_kv': 0, 'w_out': 0,
           'w_ffn_up': 0, 'w_ffn_down': 0}


def _permute_uq(w_uq):
    r = w_uq.shape[0]
    q = w_uq.reshape(r, G_HEADS, HEAD + ROPE_DIM)
    return jnp.concatenate([q[..., :HEAD].reshape(r, -1), q[..., HEAD:HEAD + ROPE_HALF].reshape(r, -1),
                            q[..., HEAD + ROPE_HALF:].reshape(r, -1)], axis=-1)


def _unpermute_uq(w_uqp):
    r = w_uqp.shape[0]
    nope = w_uqp[..., :G_W].reshape(r, G_HEADS, HEAD)
    r1 = w_uqp[..., G_W:G_W + G_HEADS * ROPE_HALF].reshape(r, G_HEADS, ROPE_HALF)
    r2 = w_uqp[..., G_W + G_HEADS * ROPE_HALF:].reshape(r, G_HEADS, ROPE_HALF)
    return jnp.concatenate([nope, r1, r2], axis=-1).reshape(r, -1)


def _cols_from_stack(st):
    _, r, n = st.shape
    return st.transpose(1, 0, 2).reshape(r, N_DEV * n)


def _cols_to_stack(wh):
    r, c = wh.shape
    return wh.reshape(r, N_DEV, c // N_DEV).transpose(1, 0, 2)


def _group_weights(group):
    kind, l = group
    return GROUP_WEIGHTS[kind][0 if l < N_A else 1]


def _step(args):
    p = dict(zip(INPUT_NAMES, args))
    x, mem, positions, target = p['x'][0], p['mem'][0], p['positions'][0], p['loss_target'][0]
    d = x.shape[-1]
    my_id = _linear_id(_my_position())

    w_kv_pad = jnp.pad(p['w_kv_a'], ((0, 0), (0, KV_PAD - p['w_kv_a'].shape[1])))
    shard = {k: p[k].astype(BF16) for k in LAYERED}
    shard['w_uk'] = shard['w_uk'].reshape(shard['w_uk'].shape[0], shard['w_uk'].shape[1], -1)
    shard['w_uv'] = shard['w_uv'].reshape(shard['w_uv'].shape[0], shard['w_uv'].shape[1], -1)
    shard.update(conv_w=p['conv_w'], g_v=p['g_v'], w_kv_a=w_kv_pad.astype(BF16))
    src_names = list(shard)
    gather_groups = []
    for l in range(DEPTH):
        gather_groups += [(kind, l) for kind in GROUP_ORDER]
    buffers, owner = [], []
    for group in gather_groups:
        kind, l = group
        for k in _group_weights(group):
            buffers.append((src_names.index(k), (l - LAYERED[k],), shard[k].shape[1:], shard[k].dtype, l < TWO_LEVEL_LAYERS))
            owner.append((group, k))
        if group == ('in', 0):
            for k in ('g_v', 'conv_w'):
                buffers.append((src_names.index(k), (), shard[k].shape, shard[k].dtype, True))
                owner.append((group, k))
        if group == ('in', N_A):
            buffers.append((src_names.index('w_kv_a'), (), shard['w_kv_a'].shape, BF16, False))
            owner.append((group, 'w_kv_a'))
    g_sems, g_lands, g_srcs, _ = _exchange_start([shard[k] for k in src_names], buffers, name="gather_start",
                                                 scatter=False)

    def fetch(group, after):
        idx = [i for i, (grp, _) in enumerate(owner) if grp == group]
        landed = _exchange_wait(g_srcs, [buffers[i] for i in idx], [g_sems[i] for i in idx],
                                [g_lands[i] for i in idx], after, name=f"gather_wait_{group[0]}{group[1]}",
                                scatter=False)
        if group[1] < TWO_LEVEL_LAYERS:
            landed = _forward_to_sibling(landed, name=f"gather_forward_{group[0]}{group[1]}")
        got = {owner[i][1]: t for i, t in zip(idx, landed)}
        out = {}
        for k, t in got.items():
            if k in ('w_in_a', 'w_uq'):
                out[k] = _cols_from_stack(t)
            elif k == 'g_v':
                out[k] = t.transpose(1, 0, 2).reshape(t.shape[1], -1)
            elif k in ('w_ffn_up', 'conv_w'):
                out[k] = t
            else:
                out[k] = t.reshape(-1, t.shape[-1])
        if 'w_uq' in out:
            out['w_uqp'] = _permute_uq(out.pop('w_uq'))
        for old, new in (('w_in_a', 'w_in'), ('w_in_b', 'w_in'), ('w_ffn_up', 'w_up'), ('w_ffn_down', 'w_down')):
            if old in out:
                out[new] = out.pop(old)
        return out

    pending = []

    def emit(group, grads):
        send = {}
        for k, t in grads.items():
            if k == 'w_in_a':
                send[k] = _cols_to_stack(t)
            elif k == 'w_uqp':
                send['w_uq'] = _cols_to_stack(_unpermute_uq(t))
            elif k == 'w_ffn_up':
                send[k] = t
            elif k == 'w_kv_a':
                cols = p['w_kv_a'].shape[1]
                send[k] = t[:, :cols].reshape(N_DEV, -1, cols)
            else:
                send[k] = t.reshape(N_DEV, t.shape[0] // N_DEV, t.shape[1])
        keys = list(send)
        bufs = [(i, (), send[k].shape[1:], send[k].dtype, False) for i, k in enumerate(keys)]
        sems, lands, srcs, token = _exchange_start([send[k] for k in keys], bufs,
                                                   name=f"scatter_start_{group[0]}{group[1]}", scatter=True)
        pending.append((group, keys, bufs, sems, lands, srcs))
        return token

    rep = {k: p[k] for k in REPLICATED}
    sq, grad_x, g = _local_step(x, mem, positions, target, rep, fetch, emit)
    loss = (0.5 / d) * lax.psum(sq, ("x", "y", "c"))

    out, running = {}, {}
    for group, keys, bufs, sems, lands, srcs in pending:
        landed = _exchange_wait(srcs, bufs, sems, lands, None, name=f"scatter_wait_{group[0]}{group[1]}", scatter=True)
        for k, parts in zip(keys, landed):
            stacked = k in LAYERED
            nl = p[k].shape[0] if stacked else 1
            layer = group[1] - LAYERED[k] if stacked else 0
            rows = p[k].size // nl // p[k].shape[-1]
            view = (nl, rows, p[k].shape[-1])
            running[k] = _sum_adam(parts.reshape(N_DEV, rows, view[2]), p[k].reshape(view), p['m_' + k].reshape(view),
                                   p['v_' + k].reshape(view), layer, running.get(k), name=f"adam_{k}{layer}")
    for k, res in running.items():
        out[k] = [t.reshape(p[k].shape) for t in res]

    small = {
        'g_mix': jnp.concatenate(g['g_mix']), 'g_ffn': jnp.concatenate(g['g_ffn']), 'g_final': g['g_final'],
        'w_sp': jnp.stack(g['w_sp']), 'b_sp': jnp.stack(g['b_sp']), 'g_kv': g['g_kv'], 'g_kv_lat': g['g_kv_lat'],
        'g_q_lat': jnp.concatenate(g['g_q_lat']), 'g_mem': jnp.concatenate(g['g_mem']),
        'conv_b': jnp.stack(g['conv_b']),
        'g_v': jnp.concatenate(g['g_v']),
        'conv_w': jnp.stack(g['conv_w']).transpose(0, 2, 1, 3),
    }
    small_names = REPLICATED + SMALL_SHARDED
    flat = jnp.concatenate([small[k].reshape(-1).astype(F32) for k in small_names])
    n_small = flat.shape[0]
    padded = -(-n_small // SMALL_ALIGN) * SMALL_ALIGN
    flat = jnp.pad(flat, (0, padded - n_small)).reshape(N_DEV, -1, 128)
    last_update = out[pending[-1][1][-1]][1]
    (small_parts,) = _exchange([flat], name="scatter_small", scatter=True, after=last_update)
    reduced = _sum_parts(small_parts, name="sum_small")
    (small_all,) = _exchange([reduced], name="gather_small", scatter=False)
    small_all = small_all.reshape(-1)
    grads_small, off = {}, 0
    for k in small_names:
        size = small[k].size
        grads_small[k] = small_all[off:off + size].reshape(small[k].shape)
        off += size
    grads_small['g_v'] = lax.dynamic_slice_in_dim(grads_small['g_v'], my_id * p['g_v'].shape[1], p['g_v'].shape[1], axis=1)
    grads_small['conv_w'] = lax.dynamic_index_in_dim(grads_small['conv_w'], my_id, axis=2, keepdims=False)
    gs = jnp.concatenate([grads_small[k].reshape(-1) for k in small_names])
    n_loc = gs.shape[0]
    pad_loc = -(-n_loc // 1024) * 1024 - n_loc

    def pack(prefix):
        t = jnp.concatenate([p[prefix + k].reshape(-1) for k in small_names])
        return jnp.pad(t, (0, pad_loc)).reshape(1, -1, 128)

    res = _sum_adam(jnp.pad(gs, (0, pad_loc)).reshape(1, -1, 128), pack(''), pack('m_'), pack('v_'), 0, None,
                    name="adam_small")
    off = 0
    for k in small_names:
        size = p[k].size
        out[k] = [t.reshape(-1)[off:off + size].reshape(p[k].shape) for t in res]
        off += size

    outs = [loss, grad_x[None]]
    for i in range(4):
        outs += [out[k][i] for k in WEIGHTS]
    return tuple(outs)


def kernel(x, mem, positions, g_mix, g_ffn, g_final, w_in_a, g_v, w_sp, b_sp, g_kv, w_kv_a, g_kv_lat, w_in_b, g_q_lat, w_uq, w_uk, w_uv, g_mem, w_mem_kv, w_out, w_ffn_up, conv_w, conv_b, w_ffn_down, loss_target, m_g_mix, m_g_ffn, m_g_final, m_w_in_a, m_g_v, m_w_sp, m_b_sp, m_g_kv, m_w_kv_a, m_g_kv_lat, m_w_in_b, m_g_q_lat, m_w_uq, m_w_uk, m_w_uv, m_g_mem, m_w_mem_kv, m_w_out, m_w_ffn_up, m_conv_w, m_conv_b, m_w_ffn_down, v_g_mix, v_g_ffn, v_g_final, v_w_in_a, v_g_v, v_w_sp, v_b_sp, v_g_kv, v_w_kv_a, v_g_kv_lat, v_w_in_b, v_g_q_lat, v_w_uq, v_w_uk, v_w_uv, v_g_mem, v_w_mem_kv, v_w_out, v_w_ffn_up, v_conv_w, v_conv_b, v_w_ffn_down):
    return _step((x, mem, positions, g_mix, g_ffn, g_final, w_in_a, g_v, w_sp, b_sp, g_kv, w_kv_a, g_kv_lat, w_in_b, g_q_lat, w_uq, w_uk, w_uv, g_mem, w_mem_kv, w_out, w_ffn_up, conv_w, conv_b, w_ffn_down, loss_target, m_g_mix, m_g_ffn, m_g_final, m_w_in_a, m_g_v, m_w_sp, m_b_sp, m_g_kv, m_w_kv_a, m_g_kv_lat, m_w_in_b, m_g_q_lat, m_w_uq, m_w_uk, m_w_uv, m_g_mem, m_w_mem_kv, m_w_out, m_w_ffn_up, m_conv_w, m_conv_b, m_w_ffn_down, v_g_mix, v_g_ffn, v_g_final, v_w_in_a, v_g_v, v_w_sp, v_b_sp, v_g_kv, v_w_kv_a, v_g_kv_lat, v_w_in_b, v_g_q_lat, v_w_uq, v_w_uk, v_w_uv, v_g_mem, v_w_mem_kv, v_w_out, v_w_ffn_up, v_conv_w, v_conv_b, v_w_ffn_down))
```

```python
import math

import jax
import jax.numpy as jnp
from jax import lax
from jax.experimental import pallas as pl
from jax.experimental.pallas import tpu as pltpu

F32 = jnp.float32
BF16 = jnp.bfloat16

N_DEV = 8
N_A = 2
DEPTH = 4
G_HEADS = 12
HEAD = 128
CHUNK = 128
MEM_HEADS = 4
MEM_W = MEM_HEADS * HEAD
G_W = G_HEADS * HEAD
ROPE_DIM = 64
ROPE_HALF = ROPE_DIM // 2
KV_RANK = 512
Q_RANK = 512
KV_PAD = 640
ROPE_THETA = 10000.0
EPS = 1e-6
CONV_W = 3

ADAM_LR = 0.001
ADAM_B1 = 0.9
ADAM_B2 = 0.999
ADAM_EPS = 1e-08
ADAM_WD = 0.01
ADAM_STEP = 10

VMEM_LIMIT_V7X = 56 * 1024 * 1024
MASK_VALUE = -1e30

WEIGHTS = ['g_mix', 'g_ffn', 'g_final', 'w_in_a', 'g_v', 'w_sp', 'b_sp', 'g_kv', 'w_kv_a', 'g_kv_lat',
           'w_in_b', 'g_q_lat', 'w_uq', 'w_uk', 'w_uv', 'g_mem', 'w_mem_kv', 'w_out', 'w_ffn_up',
           'conv_w', 'conv_b', 'w_ffn_down']
REPLICATED = ['g_mix', 'g_ffn', 'g_final', 'w_sp', 'b_sp', 'g_kv', 'g_kv_lat', 'g_q_lat', 'g_mem', 'conv_b']
SMALL_SHARDED = ['g_v', 'conv_w']


def _params(sem=None):
    return pltpu.CompilerParams(dimension_semantics=sem, vmem_limit_bytes=VMEM_LIMIT_V7X)


def _dot(a, b, dims):
    contract = {'nn': ((1,), (0,)), 'nt': ((1,), (1,)), 'tn': ((0,), (0,))}[dims]
    return lax.dot_general(a, b, (contract, ((), ())), preferred_element_type=F32)


def _erf(x):
    return lax.erf(x)


def _gelu(x):
    return 0.5 * x * (1.0 + _erf(x * (2.0 ** -0.5)))


def _gelu_grad(x):
    cdf = 0.5 * (1.0 + _erf(x * (2.0 ** -0.5)))
    pdf = jnp.exp(-0.5 * x * x) * (1.0 / math.sqrt(2.0 * math.pi))
    return cdf + x * pdf


def _sigmoid(x):
    return 1.0 / (1.0 + jnp.exp(-x))


def _operand_spec(shape, lead, blocked, tr, tc, ridx, cidx):
    if blocked:
        per = shape[-1] // tc
        assert shape[-1] % tc == 0, (shape, tc)
        return pl.BlockSpec(
            (None,) * (1 + len(lead)) + (tr, tc),
            lambda *g: (cidx(*g) // per,) + lead + (ridx(*g), cidx(*g) % per))
    return pl.BlockSpec((None,) * len(lead) + (tr, tc), lambda *g: lead + (ridx(*g), cidx(*g)))


def _view2d(x, blocked):
    return (x.shape[-2], x.shape[0] * x.shape[-1]) if blocked else (x.shape[-2], x.shape[-1])


def _mm(a, b, *, dims, out_dtype, name, tm, tn, tk=None, res=None, a_lead=(), b_lead=(),
        a_blocked=False, b_blocked=False, out_block=None, n_outer=False, after=None):
    ar, ac = _view2d(a, a_blocked)
    br, bc = _view2d(b, b_blocked)
    m, k = (ac, ar) if dims == 'tn' else (ar, ac)
    n, k2 = (br, bc) if dims == 'nt' else (bc, br)
    assert k == k2, (a.shape, b.shape, dims)
    tm, tn = min(tm, m), min(tn, n)
    tk = k if tk is None else tk
    assert m % tm == 0 and n % tn == 0 and k % tk == 0, (name, m, n, k, tm, tn, tk)
    nk = k // tk
    if n_outer:
        gi, gj = (lambda g0, g1, g2: g1), (lambda g0, g1, g2: g0)
        grid = (n // tn, m // tm, nk)
    else:
        gi, gj = (lambda g0, g1, g2: g0), (lambda g0, g1, g2: g1)
        grid = (m // tm, n // tn, nk)
    gk = lambda g0, g1, g2: g2

    if dims == 'tn':
        a_spec = _operand_spec(a.shape, a_lead, a_blocked, tk, tm, gk, gi)
    else:
        a_spec = _operand_spec(a.shape, a_lead, a_blocked, tm, tk, gi, gk)
    if dims == 'nt':
        b_spec = _operand_spec(b.shape, b_lead, b_blocked, tn, tk, gj, gk)
    else:
        b_spec = _operand_spec(b.shape, b_lead, b_blocked, tk, tn, gk, gj)
    in_specs = [a_spec, b_spec]
    operands = [a, b]
    if res is not None:
        in_specs.append(pl.BlockSpec((tm, tn), lambda *g: (gi(*g), gj(*g))))
        operands.append(res)
    if after is not None:
        in_specs.append(pl.BlockSpec(memory_space=pl.ANY))
        operands.append(after)
    n_in = len(operands)
    if out_block is not None:
        out_shape = jax.ShapeDtypeStruct((n // out_block, m, out_block), out_dtype)
        out_spec = _operand_spec(out_shape.shape, (), True, tm, tn, gi, gj)
    else:
        out_shape = jax.ShapeDtypeStruct((m, n), out_dtype)
        out_spec = pl.BlockSpec((tm, tn), lambda *g: (gi(*g), gj(*g)))

    def body(*refs):
        a_ref, b_ref = refs[0], refs[1]
        r_ref = refs[2] if res is not None else None
        o_ref = refs[n_in]
        acc_ref = refs[-1] if nk > 1 else None
        part = _dot(a_ref[...].astype(BF16), b_ref[...].astype(BF16), dims)

        def finish(total):
            if r_ref is not None:
                total = total + r_ref[...]
            o_ref[...] = total.astype(o_ref.dtype)

        if nk == 1:
            finish(part)
        else:
            kk = pl.program_id(2)

            @pl.when(kk == 0)
            def _():
                acc_ref[...] = part

            @pl.when(kk > 0)
            def _():
                acc_ref[...] += part

            @pl.when(kk == nk - 1)
            def _():
                finish(acc_ref[...])

    scratch = [pltpu.VMEM((tm, tn), F32)] if nk > 1 else []
    return pl.pallas_call(
        body, name=name, grid=grid, in_specs=in_specs, out_specs=out_spec,
        out_shape=out_shape, scratch_shapes=scratch,
        compiler_params=_params(("parallel", "parallel", "arbitrary")),
    )(*operands)


def _rmsnorm(x, g, *, name, width=None, out_dtype=BF16, tm=512):
    s = x.shape[0]
    w = x.shape[1] if width is None else width
    tm = min(tm, s)

    def body(x_ref, g_ref, o_ref):
        xv = x_ref[...].astype(F32)
        rstd = lax.rsqrt(jnp.mean(xv * xv, axis=-1, keepdims=True) + EPS)
        o_ref[...] = (xv * rstd * g_ref[...]).astype(o_ref.dtype)

    return pl.pallas_call(
        body, name=name, grid=(s // tm,),
        in_specs=[pl.BlockSpec((tm, w), lambda i: (i, 0)), pl.BlockSpec((1, w), lambda i: (0, 0))],
        out_specs=pl.BlockSpec((tm, w), lambda i: (i, 0)),
        out_shape=jax.ShapeDtypeStruct((s, w), out_dtype),
        compiler_params=_params(("parallel",)),
    )(x, g.reshape(1, w))


def _rmsnorm_bwd(x, g, dy, *, name, width=None, dres=None, after=None, out_dtype=F32, tm=512):
    s = x.shape[0]
    w = x.shape[1] if width is None else width
    tm = min(tm, s)

    def body(*refs):
        x_ref, g_ref, dy_ref = refs[0], refs[1], refs[2]
        r_ref = refs[3] if dres is not None else None
        dx_ref, dg_ref = refs[-2], refs[-1]
        xv = x_ref[...].astype(F32)
        rstd = lax.rsqrt(jnp.mean(xv * xv, axis=-1, keepdims=True) + EPS)
        xhat = xv * rstd
        dyv = dy_ref[...].astype(F32)
        gdy = dyv * g_ref[...]
        dx = rstd * (gdy - xhat * jnp.mean(gdy * xhat, axis=-1, keepdims=True))
        if r_ref is not None:
            dx = dx + r_ref[...]
        dx_ref[...] = dx.astype(dx_ref.dtype)
        part = jnp.sum(dyv * xhat, axis=0, keepdims=True)

        @pl.when(pl.program_id(0) == 0)
        def _():
            dg_ref[...] = part

        @pl.when(pl.program_id(0) > 0)
        def _():
            dg_ref[...] += part

    row = pl.BlockSpec((tm, w), lambda i: (i, 0))
    vec = pl.BlockSpec((1, w), lambda i: (0, 0))
    in_specs = [row, vec, row] + ([row] if dres is not None else [])
    operands = [x, g.reshape(1, w), dy] + ([dres] if dres is not None else [])
    if after is not None:
        in_specs.append(pl.BlockSpec(memory_space=pl.ANY))
        operands.append(after)
    return pl.pallas_call(
        body, name=name, grid=(s // tm,), in_specs=in_specs, out_specs=[row, vec],
        out_shape=[jax.ShapeDtypeStruct((s, w), out_dtype), jax.ShapeDtypeStruct((1, w), F32)],
        compiler_params=_params(("arbitrary",)),
    )(*operands)


def _final_loss(x, target, g, *, name, tm=256):
    s, d = x.shape
    tm = min(tm, s)

    def body(x_ref, t_ref, g_ref, sq_ref, dx_ref, dg_ref):
        xv = x_ref[...]
        rstd = lax.rsqrt(jnp.mean(xv * xv, axis=-1, keepdims=True) + EPS)
        xhat = xv * rstd
        err = xhat * g_ref[...] - t_ref[...]
        dyv = err * (1.0 / d)
        gdy = dyv * g_ref[...]
        dx_ref[...] = rstd * (gdy - xhat * jnp.mean(gdy * xhat, axis=-1, keepdims=True))
        sq = jnp.sum(err * err, axis=0, keepdims=True)
        dg = jnp.sum(dyv * xhat, axis=0, keepdims=True)

        @pl.when(pl.program_id(0) == 0)
        def _():
            sq_ref[...] = sq
            dg_ref[...] = dg

        @pl.when(pl.program_id(0) > 0)
        def _():
            sq_ref[...] += sq
            dg_ref[...] += dg

    row = pl.BlockSpec((tm, d), lambda i: (i, 0))
    vec = pl.BlockSpec((1, d), lambda i: (0, 0))
    return pl.pallas_call(
        body, name=name, grid=(s // tm,), in_specs=[row, row, vec], out_specs=[vec, row, vec],
        out_shape=[jax.ShapeDtypeStruct((1, d), F32), jax.ShapeDtypeStruct((s, d), F32),
                   jax.ShapeDtypeStruct((1, d), F32)],
        compiler_params=_params(("arbitrary",)),
    )(x, target, g.reshape(1, d))


def _tril_mask():
    t = lax.broadcasted_iota(jnp.int32, (CHUNK, CHUNK), 0)
    s = lax.broadcasted_iota(jnp.int32, (CHUNK, CHUNK), 1)
    return t >= s


def _sgu_fwd(z, g_v, w_sp, b_sp_t, *, name):
    s = z.shape[0]

    def body(zu_ref, zv_ref, g_ref, w_ref, b_ref, o_ref):
        u = _gelu(zu_ref[...].astype(F32))
        gv = _gelu(zv_ref[...].astype(F32))
        rstd = lax.rsqrt(jnp.mean(gv * gv, axis=-1, keepdims=True) + EPS)
        v = (gv * rstd * g_ref[...]).astype(BF16)
        mask = _tril_mask()
        for grp in range(G_HEADS):
            cols = slice(grp * HEAD, (grp + 1) * HEAD)
            wm = jnp.where(mask, w_ref[grp], 0.0).astype(BF16)
            sv = _dot(wm, v[:, cols], 'nn') + b_ref[:, grp:grp + 1]
            o_ref[:, cols] = (u[:, cols] * sv).astype(o_ref.dtype)

    return pl.pallas_call(
        body, name=name, grid=(s // CHUNK,),
        in_specs=[pl.BlockSpec((CHUNK, G_W), lambda i: (i, 0)),
                  pl.BlockSpec((CHUNK, G_W), lambda i: (i, 1)),
                  pl.BlockSpec((1, G_W), lambda i: (0, 0)),
                  pl.BlockSpec((G_HEADS, CHUNK, CHUNK), lambda i: (0, 0, 0)),
                  pl.BlockSpec((CHUNK, G_HEADS), lambda i: (0, 0))],
        out_specs=pl.BlockSpec((CHUNK, G_W), lambda i: (i, 0)),
        out_shape=jax.ShapeDtypeStruct((s, G_W), BF16),
        compiler_params=_params(("parallel",)),
    )(z, z, g_v.reshape(1, G_W), w_sp, b_sp_t)


def _sgu_bwd(z, dmix, dqm, g_v, w_sp, b_sp_t, *, name):
    s = z.shape[0]
    zw = z.shape[1]

    def body(zu_ref, zv_ref, dm_ref, dq_ref, g_ref, w_ref, b_ref, dz_ref, dw_ref, db_ref, dg_ref):
        first = pl.program_id(0) == 0

        @pl.when(first)
        def _():
            dw_ref[...] = jnp.zeros_like(dw_ref)
            db_ref[...] = jnp.zeros_like(db_ref)
            dg_ref[...] = jnp.zeros_like(dg_ref)

        zu = zu_ref[...].astype(F32)
        zv = zv_ref[...].astype(F32)
        dmain = dm_ref[...].astype(F32)
        u = _gelu(zu)
        gv = _gelu(zv)
        rstd = lax.rsqrt(jnp.mean(gv * gv, axis=-1, keepdims=True) + EPS)
        vhat = gv * rstd
        gvec = g_ref[...]
        v = (vhat * gvec).astype(BF16)
        dsv = dmain * u
        dsv_b = dsv.astype(BF16)
        mask = _tril_mask()
        dv_parts = []
        for grp in range(G_HEADS):
            cols = slice(grp * HEAD, (grp + 1) * HEAD)
            wm = jnp.where(mask, w_ref[grp], 0.0).astype(BF16)
            sv = _dot(wm, v[:, cols], 'nn') + b_ref[:, grp:grp + 1]
            dz_ref[:, cols] = (dmain[:, cols] * sv * _gelu_grad(zu[:, cols])).astype(dz_ref.dtype)
            dwg = _dot(dsv_b[:, cols], v[:, cols], 'nt')
            dw_ref[grp] += jnp.where(mask, dwg, 0.0)
            db_ref[:, grp:grp + 1] += jnp.sum(dsv[:, cols], axis=-1, keepdims=True)
            dv_parts.append(_dot(wm, dsv_b[:, cols], 'tn'))
        dv = jnp.concatenate(dv_parts, axis=-1)
        dg_ref[...] += jnp.sum(dv * vhat, axis=0, keepdims=True)
        gdv = dv * gvec
        dgv = rstd * (gdv - vhat * jnp.mean(gdv * vhat, axis=-1, keepdims=True))
        dz_ref[:, G_W:2 * G_W] = (dgv * _gelu_grad(zv)).astype(dz_ref.dtype)
        dz_ref[:, 2 * G_W:] = dq_ref[...].astype(dz_ref.dtype)

    return pl.pallas_call(
        body, name=name, grid=(s // CHUNK,),
        in_specs=[pl.BlockSpec((CHUNK, G_W), lambda i: (i, 0)),
                  pl.BlockSpec((CHUNK, G_W), lambda i: (i, 1)),
                  pl.BlockSpec((CHUNK, G_W), lambda i: (i, 0)),
                  pl.BlockSpec((CHUNK, MEM_W), lambda i: (i, 0)),
                  pl.BlockSpec((1, G_W), lambda i: (0, 0)),
                  pl.BlockSpec((G_HEADS, CHUNK, CHUNK), lambda i: (0, 0, 0)),
                  pl.BlockSpec((CHUNK, G_HEADS), lambda i: (0, 0))],
        out_specs=[pl.BlockSpec((CHUNK, zw), lambda i: (i, 0)),
                   pl.BlockSpec((G_HEADS, CHUNK, CHUNK), lambda i: (0, 0, 0)),
                   pl.BlockSpec((CHUNK, G_HEADS), lambda i: (0, 0)),
                   pl.BlockSpec((1, G_W), lambda i: (0, 0))],
        out_shape=[jax.ShapeDtypeStruct((s, zw), BF16),
                   jax.ShapeDtypeStruct((G_HEADS, CHUNK, CHUNK), F32),
                   jax.ShapeDtypeStruct((CHUNK, G_HEADS), F32),
                   jax.ShapeDtypeStruct((1, G_W), F32)],
        compiler_params=_params(("arbitrary",)),
    )(z, z, dmix, dqm, g_v.reshape(1, G_W), w_sp, b_sp_t)


def _mem_probs(q, k):
    sc = _dot(q, k, 'nt') * (HEAD ** -0.5)
    sc = sc - jnp.max(sc, axis=-1, keepdims=True)
    e = jnp.exp(sc)
    return e / jnp.sum(e, axis=-1, keepdims=True)


def _memattn_fwd(z, kvm, main, *, qcol, name, tm=512):
    s = z.shape[0]
    m = kvm.shape[0]
    tm = min(tm, s)

    def body(q_ref, kv_ref, main_ref, o_ref):
        o_ref[:, :G_W] = main_ref[...]
        for h in range(MEM_HEADS):
            cols = slice(h * HEAD, (h + 1) * HEAD)
            k = kv_ref[:, cols]
            v = kv_ref[:, MEM_W + h * HEAD:MEM_W + (h + 1) * HEAD]
            p = _mem_probs(q_ref[:, cols], k)
            o_ref[:, G_W + h * HEAD:G_W + (h + 1) * HEAD] = _dot(p.astype(BF16), v, 'nn').astype(o_ref.dtype)

    return pl.pallas_call(
        body, name=name, grid=(s // tm,),
        in_specs=[pl.BlockSpec((tm, MEM_W), lambda i: (i, qcol)),
                  pl.BlockSpec((m, 2 * MEM_W), lambda i: (0, 0)),
                  pl.BlockSpec((tm, G_W), lambda i: (i, 0))],
        out_specs=pl.BlockSpec((tm, G_W + MEM_W), lambda i: (i, 0)),
        out_shape=jax.ShapeDtypeStruct((s, G_W + MEM_W), BF16),
        compiler_params=_params(("parallel",)),
    )(z, kvm, main)


def _memattn_bwd(z, kvm, dmix, *, qcol, name, tm=512):
    s = z.shape[0]
    m = kvm.shape[0]
    tm = min(tm, s)
    scale = HEAD ** -0.5

    def body(q_ref, kv_ref, do_ref, dq_ref, dkv_ref):
        @pl.when(pl.program_id(0) == 0)
        def _():
            dkv_ref[...] = jnp.zeros_like(dkv_ref)

        for h in range(MEM_HEADS):
            cols = slice(h * HEAD, (h + 1) * HEAD)
            vcols = slice(MEM_W + h * HEAD, MEM_W + (h + 1) * HEAD)
            q = q_ref[:, cols]
            k = kv_ref[:, cols]
            v = kv_ref[:, vcols]
            do = do_ref[:, cols]
            p = _mem_probs(q, k)
            dp = _dot(do, v, 'nt')
            ds = (p * (dp - jnp.sum(dp * p, axis=-1, keepdims=True)) * scale).astype(BF16)
            dq_ref[:, cols] = _dot(ds, k, 'nn').astype(dq_ref.dtype)
            dkv_ref[:, cols] += _dot(ds, q, 'tn')
            dkv_ref[:, vcols] += _dot(p.astype(BF16), do, 'tn')

    mo_block = G_W // MEM_W
    return pl.pallas_call(
        body, name=name, grid=(s // tm,),
        in_specs=[pl.BlockSpec((tm, MEM_W), lambda i: (i, qcol)),
                  pl.BlockSpec((m, 2 * MEM_W), lambda i: (0, 0)),
                  pl.BlockSpec((tm, MEM_W), lambda i: (i, mo_block))],
        out_specs=[pl.BlockSpec((tm, MEM_W), lambda i: (i, 0)),
                   pl.BlockSpec((m, 2 * MEM_W), lambda i: (0, 0))],
        out_shape=[jax.ShapeDtypeStruct((s, MEM_W), BF16), jax.ShapeDtypeStruct((m, 2 * MEM_W), F32)],
        compiler_params=_params(("arbitrary",)),
    )(z, kvm, dmix)


def _rope(x1, x2, cos, sin, *, name, inverse=False, out_dtype=BF16, col1=0, col2=0, tm=512):
    s, w = cos.shape
    tm = min(tm, s)
    sign = -1.0 if inverse else 1.0

    def body(a_ref, b_ref, c_ref, s_ref, o1_ref, o2_ref):
        a = a_ref[...].astype(F32)
        b = b_ref[...].astype(F32)
        c = c_ref[...]
        sn = s_ref[...] * sign
        o1_ref[...] = (a * c - b * sn).astype(o1_ref.dtype)
        o2_ref[...] = (b * c + a * sn).astype(o2_ref.dtype)

    row = pl.BlockSpec((tm, w), lambda i: (i, 0))
    return pl.pallas_call(
        body, name=name, grid=(s // tm,),
        in_specs=[pl.BlockSpec((tm, w), lambda i: (i, col1)), pl.BlockSpec((tm, w), lambda i: (i, col2)), row, row],
        out_specs=[row, row],
        out_shape=[jax.ShapeDtypeStruct((s, w), out_dtype)] * 2,
        compiler_params=_params(("parallel",)),
    )(x1, x2, cos, sin)


MHA_BLOCK = 1024


def _mha_scores(q1, q2, k1, k2, scale, diagonal):
    sc = (_dot(q1, k1, 'nt') + _dot(q2, k2, 'nt')) * scale
    if not diagonal:
        return sc, None
    rows = lax.broadcasted_iota(jnp.int32, sc.shape, 0)
    cols = lax.broadcasted_iota(jnp.int32, sc.shape, 1)
    return sc, cols <= rows


def _mha_fwd(qp, qr, kn, kr, vv, *, name):
    s = kn.shape[0]
    tb = min(MHA_BLOCK, s)
    scale = (HEAD + ROPE_DIM) ** -0.5

    def body(qn_ref, qr_ref, kn_ref, kr_ref, v_ref, o_ref, lse_ref, m_ref, l_ref, acc_ref):
        i = pl.program_id(1)
        q1, q2 = qn_ref[...], qr_ref[...]
        m_ref[...] = jnp.full_like(m_ref, MASK_VALUE)
        l_ref[...] = jnp.zeros_like(l_ref)
        acc_ref[...] = jnp.zeros_like(acc_ref)

        def block(j, diagonal):
            ks = pl.multiple_of(j * tb, tb)
            k1, k2, vj = kn_ref[pl.ds(ks, tb), :], kr_ref[pl.ds(ks, tb), :], v_ref[pl.ds(ks, tb), :]
            sc, keep = _mha_scores(q1, q2, k1, k2, scale, diagonal)
            if diagonal:
                sc = jnp.where(keep, sc, MASK_VALUE)
            m_old = m_ref[...]
            m_new = jnp.maximum(m_old, jnp.max(sc, axis=-1, keepdims=True))
            p = jnp.exp(sc - m_new)
            alpha = jnp.exp(m_old - m_new)
            l_ref[...] = alpha * l_ref[...] + jnp.sum(p, axis=-1, keepdims=True)
            acc_ref[...] = alpha * acc_ref[...] + _dot(p.astype(BF16), vj, 'nn')
            m_ref[...] = m_new

        def step(j, carry):
            block(j, False)
            return carry

        lax.fori_loop(0, i, step, 0)
        block(i, True)
        l = l_ref[...]
        o_ref[...] = (acc_ref[...] / l).astype(o_ref.dtype)
        lse_ref[...] = m_ref[...] + jnp.log(l)

    return pl.pallas_call(
        body, name=name, grid=(G_HEADS, s // tb),
        in_specs=[pl.BlockSpec((tb, HEAD), lambda h, i: (i, h)),
                  pl.BlockSpec((None, tb, ROPE_DIM), lambda h, i: (h, i, 0)),
                  pl.BlockSpec((s, HEAD), lambda h, i: (0, h)),
                  pl.BlockSpec((s, ROPE_DIM), lambda h, i: (0, 0)),
                  pl.BlockSpec((s, HEAD), lambda h, i: (0, h))],
        out_specs=[pl.BlockSpec((tb, HEAD), lambda h, i: (i, h)),
                   pl.BlockSpec((None, tb, 1), lambda h, i: (h, i, 0))],
        out_shape=[jax.ShapeDtypeStruct((s, G_W), BF16), jax.ShapeDtypeStruct((G_HEADS, s, 1), F32)],
        scratch_shapes=[pltpu.VMEM((tb, 1), F32), pltpu.VMEM((tb, 1), F32), pltpu.VMEM((tb, HEAD), F32)],
        compiler_params=_params(("parallel", "arbitrary")),
    )(qp, qr, kn, kr, vv)


def _mha_bwd(qp, qr, kn, kr, vv, o, do, lse, *, name):
    s = kn.shape[0]
    tb = min(MHA_BLOCK, s)
    nq = s // tb
    scale = (HEAD + ROPE_DIM) ** -0.5

    def body(qn_ref, qr_ref, kn_ref, kr_ref, v_ref, o_ref, do_ref, lse_ref,
             dqn_ref, dqr_ref, dkn_ref, dkr_ref, dv_ref, dq1_ref, dq2_ref):
        h, i = pl.program_id(0), pl.program_id(1)

        @pl.when(i == 0)
        def _():
            dkn_ref[...] = jnp.zeros_like(dkn_ref)
            dv_ref[...] = jnp.zeros_like(dv_ref)

        @pl.when(jnp.logical_and(h == 0, i == 0))
        def _():
            dkr_ref[...] = jnp.zeros_like(dkr_ref)

        q1, q2, dov = qn_ref[...], qr_ref[...], do_ref[...]
        delta = jnp.sum(dov.astype(F32) * o_ref[...].astype(F32), axis=-1, keepdims=True)
        lsev = lse_ref[...]
        dq1_ref[...] = jnp.zeros_like(dq1_ref)
        dq2_ref[...] = jnp.zeros_like(dq2_ref)

        def block(j, diagonal):
            ks = pl.multiple_of(j * tb, tb)
            k1, k2, vj = kn_ref[pl.ds(ks, tb), :], kr_ref[pl.ds(ks, tb), :], v_ref[pl.ds(ks, tb), :]
            sc, keep = _mha_scores(q1, q2, k1, k2, scale, diagonal)
            p = jnp.exp(sc - lsev)
            if diagonal:
                p = jnp.where(keep, p, 0.0)
            dp = _dot(dov, vj, 'nt')
            ds = (p * (dp - delta) * scale).astype(BF16)
            pb = p.astype(BF16)
            dq1_ref[...] += _dot(ds, k1, 'nn')
            dq2_ref[...] += _dot(ds, k2, 'nn')
            dkn_ref[pl.ds(ks, tb), :] += _dot(ds, q1, 'tn')
            dkr_ref[pl.ds(ks, tb), :] += _dot(ds, q2, 'tn')
            dv_ref[pl.ds(ks, tb), :] += _dot(pb, dov, 'tn')

        def step(j, carry):
            block(j, False)
            return carry

        lax.fori_loop(0, i, step, 0)
        block(i, True)
        dqn_ref[...] = dq1_ref[...].astype(dqn_ref.dtype)
        dqr_ref[...] = dq2_ref[...].astype(dqr_ref.dtype)

    tile = pl.BlockSpec((tb, HEAD), lambda h, i: (i, h))
    rope = pl.BlockSpec((None, tb, ROPE_DIM), lambda h, i: (h, i, 0))
    head_cols = pl.BlockSpec((s, HEAD), lambda h, i: (0, h))
    shared = pl.BlockSpec((s, ROPE_DIM), lambda h, i: (0, 0))
    return pl.pallas_call(
        body, name=name, grid=(G_HEADS, nq),
        in_specs=[tile, rope, head_cols, shared, head_cols, tile, tile, pl.BlockSpec((None, tb, 1), lambda h, i: (h, i, 0))],
        out_specs=[tile, rope, head_cols, shared, head_cols],
        out_shape=[jax.ShapeDtypeStruct((s, G_W), BF16), jax.ShapeDtypeStruct((G_HEADS, s, ROPE_DIM), BF16),
                   jax.ShapeDtypeStruct((s, G_W), F32), jax.ShapeDtypeStruct((s, ROPE_DIM), F32),
                   jax.ShapeDtypeStruct((s, G_W), F32)],
        scratch_shapes=[pltpu.VMEM((tb, HEAD), F32), pltpu.VMEM((tb, ROPE_DIM), F32)],
        compiler_params=_params(("arbitrary", "arbitrary")),
    )(qp, qr, kn, kr, vv, o, do, lse)


HALO = 16


def _shift_down(prev, cur, shift, first_tile):
    tr = cur.shape[0]
    full = jnp.concatenate([prev, cur], axis=0)
    out = pltpu.roll(full, shift, axis=0)[HALO:]
    row = lax.broadcasted_iota(jnp.int32, (tr, 1), 0)
    return jnp.where(jnp.logical_and(first_tile, row < shift), 0.0, out)


def _shift_up(cur, nxt, shift, last_tile):
    tr = cur.shape[0]
    full = jnp.concatenate([cur, nxt], axis=0)
    out = pltpu.roll(full, tr + HALO - shift, axis=0)[:tr]
    row = lax.broadcasted_iota(jnp.int32, (tr, 1), 0)
    return jnp.where(jnp.logical_and(last_tile, row >= tr - shift), 0.0, out)


def _lane_chunks(width, lanes):
    return [slice(c0, min(c0 + lanes, width)) for c0 in range(0, width, lanes)]


def _conv_taps(prev_ref, cur_ref, cw_ref, cb_ref, first_tile, cs):
    cur = cur_ref[:, cs].astype(F32)
    prev = prev_ref[:, cs].astype(F32)
    a1 = _shift_down(prev, cur, 1, first_tile)
    a2 = _shift_down(prev, cur, 2, first_tile)
    c = a2 * cw_ref[0:1, cs] + a1 * cw_ref[1:2, cs] + cur * cw_ref[2:3, cs] + cb_ref[:, cs]
    return c, (a2, a1, cur)


def _conv_in_specs(tr, bw, half, layer, row_of, blk_of):
    per = tr // HALO
    specs = []
    for off in (0, half):
        specs.append(pl.BlockSpec((None, HALO, bw), lambda *g, off=off: (blk_of(*g) + off, jnp.maximum(row_of(*g) * per - 1, 0), 0)))
        specs.append(pl.BlockSpec((None, tr, bw), lambda *g, off=off: (blk_of(*g) + off, row_of(*g), 0)))
    for off in (0, half):
        specs.append(pl.BlockSpec((None, None, CONV_W, bw), lambda *g, off=off: (blk_of(*g) + off, layer, 0, 0)))
    for off in (0, half):
        specs.append(pl.BlockSpec((None, 1, bw), lambda *g, off=off: (layer * 2 * half + blk_of(*g) + off, 0, 0)))
    return specs


def _conv_fwd(a, cw, cb, layer, *, name, tr=256):
    nb, s, bw = a.shape
    half = nb // 2
    tr = min(tr, s)

    def body(gp_ref, gc_ref, vp_ref, vc_ref, cwg_ref, cwv_ref, cbg_ref, cbv_ref, o_ref):
        first = pl.program_id(0) == 0
        for cs in _lane_chunks(bw, 256):
            gate, _ = _conv_taps(gp_ref, gc_ref, cwg_ref, cbg_ref, first, cs)
            val, _ = _conv_taps(vp_ref, vc_ref, cwv_ref, cbv_ref, first, cs)
            o_ref[:, cs] = (gate * _sigmoid(gate) * val).astype(o_ref.dtype)

    return pl.pallas_call(
        body, name=name, grid=(s // tr, half),
        in_specs=_conv_in_specs(tr, bw, half, layer, lambda i, j: i, lambda i, j: j),
        out_specs=pl.BlockSpec((tr, bw), lambda i, j: (i, j)),
        out_shape=jax.ShapeDtypeStruct((s, half * bw), BF16),
        compiler_params=_params(("parallel", "parallel")),
    )(a, a, a, a, cw, cw, cb, cb)


def _conv_bwd_dc(a, dact, cw, cb, layer, *, name, after=None, tr=256):
    nb, s, bw = a.shape
    half = nb // 2
    tr = min(tr, s)

    def body(*refs):
        gp_ref, gc_ref, vp_ref, vc_ref, cwg_ref, cwv_ref, cbg_ref, cbv_ref, da_ref = refs[:9]
        dc_ref, dw_ref, db_ref = refs[-3:]
        first = pl.program_id(1) == 0

        @pl.when(first)
        def _():
            dw_ref[...] = jnp.zeros_like(dw_ref)
            db_ref[...] = jnp.zeros_like(db_ref)

        for cs in _lane_chunks(bw, 128):
            gate, gtaps = _conv_taps(gp_ref, gc_ref, cwg_ref, cbg_ref, first, cs)
            val, vtaps = _conv_taps(vp_ref, vc_ref, cwv_ref, cbv_ref, first, cs)
            dact_v = da_ref[:, cs].astype(F32)
            sg = _sigmoid(gate)
            dgate = dact_v * val * (sg * (1.0 + gate * (1.0 - sg)))
            dval = dact_v * (gate * sg)
            dc_ref[0, :, cs] = dgate.astype(dc_ref.dtype)
            dc_ref[1, :, cs] = dval.astype(dc_ref.dtype)
            for kk in range(CONV_W):
                dw_ref[0, kk:kk + 1, cs] += jnp.sum(dgate * gtaps[kk], axis=0, keepdims=True)
                dw_ref[1, kk:kk + 1, cs] += jnp.sum(dval * vtaps[kk], axis=0, keepdims=True)
            db_ref[0, :, cs] += jnp.sum(dgate, axis=0, keepdims=True)
            db_ref[1, :, cs] += jnp.sum(dval, axis=0, keepdims=True)

    outs = pl.pallas_call(
        body, name=name, grid=(half, s // tr),
        in_specs=_conv_in_specs(tr, bw, half, layer, lambda j, i: i, lambda j, i: j)
        + [pl.BlockSpec((tr, bw), lambda j, i: (i, j))]
        + ([pl.BlockSpec(memory_space=pl.ANY)] if after is not None else []),
        out_specs=[pl.BlockSpec((2, None, tr, bw), lambda j, i: (0, j, i, 0)),
                   pl.BlockSpec((2, None, CONV_W, bw), lambda j, i: (0, j, 0, 0)),
                   pl.BlockSpec((2, None, 1, bw), lambda j, i: (0, j, 0, 0))],
        out_shape=[jax.ShapeDtypeStruct((2, half, s, bw), BF16),
                   jax.ShapeDtypeStruct((2, half, CONV_W, bw), F32),
                   jax.ShapeDtypeStruct((2, half, 1, bw), F32)],
        compiler_params=_params(("parallel", "arbitrary")),
    )(a, a, a, a, cw, cw, cb, cb, dact, *([after] if after is not None else []))
    dc, dw, db = outs
    return dc.reshape(nb, s, bw), dw.reshape(nb, CONV_W, bw), db.reshape(nb, 1, bw)


def _conv_bwd_da(dc, cw, layer, *, name, tr=512):
    nb, s, bw = dc.shape
    tr = min(tr, s)
    ni = s // tr
    per = tr // HALO
    last_halo = s // HALO - 1

    def body(c_ref, n_ref, w_ref, o_ref):
        last = pl.program_id(0) == ni - 1
        for cs in _lane_chunks(bw, 256):
            cur = c_ref[:, cs].astype(F32)
            nxt = n_ref[:, cs].astype(F32)
            da = (cur * w_ref[2:3, cs] + _shift_up(cur, nxt, 1, last) * w_ref[1:2, cs]
                  + _shift_up(cur, nxt, 2, last) * w_ref[0:1, cs])
            o_ref[:, cs] = da.astype(o_ref.dtype)

    tile = pl.BlockSpec((None, tr, bw), lambda i, j: (j, i, 0))
    return pl.pallas_call(
        body, name=name, grid=(ni, nb),
        in_specs=[tile,
                  pl.BlockSpec((None, HALO, bw), lambda i, j: (j, jnp.minimum((i + 1) * per, last_halo), 0)),
                  pl.BlockSpec((None, None, CONV_W, bw), lambda i, j: (j, layer, 0, 0))],
        out_specs=tile,
        out_shape=jax.ShapeDtypeStruct((nb, s, bw), BF16),
        compiler_params=_params(("parallel", "parallel")),
    )(dc, dc, cw)


def _rope_tables(positions):
    inv = 1.0 / (ROPE_THETA ** (jnp.arange(0, ROPE_DIM, 2, dtype=F32) / ROPE_DIM))
    ang = positions.astype(F32)[:, None] * inv
    return jnp.cos(ang), jnp.sin(ang)


def _heads_to_major(r1, r2):
    s = r1.shape[0]
    both = jnp.concatenate([r1.reshape(s, G_HEADS, ROPE_HALF), r2.reshape(s, G_HEADS, ROPE_HALF)], axis=-1)
    return both.transpose(1, 0, 2)


def _heads_from_major(qr):
    s = qr.shape[1]
    t = qr.transpose(1, 0, 2)
    return t[:, :, :ROPE_HALF].reshape(s, G_HEADS * ROPE_HALF), t[:, :, ROPE_HALF:].reshape(s, G_HEADS * ROPE_HALF)


def _local_step(x, mem, positions, target, rep, fetch, emit):
    s, d = x.shape
    n_b = DEPTH - N_A
    tm = min(1024, s)
    cos, sin = _rope_tables(positions)
    cos12 = jnp.tile(cos, (1, G_HEADS))
    sin12 = jnp.tile(sin, (1, G_HEADS))
    r1_col = G_W // (G_HEADS * ROPE_HALF)
    b_sp_t = rep['b_sp'].transpose(0, 2, 1)

    saved = []
    kv = None
    shared = None
    for l in range(DEPTH):
        wm = fetch(('in', l), x)
        if l == 0:
            shared = {'g_v': wm['g_v'], 'conv_w': wm['conv_w']}
            bw = shared['conv_w'].shape[-1]
            conv_b = rep['conv_b'].reshape(-1, 1, bw)
        sv = {'x_in': x, 'wm': wm}
        if l == N_A:
            xn_kv = _rmsnorm(x, rep['g_kv'], name="kvnorm")
            kvx = _mm(xn_kv, wm['w_kv_a'], dims='nn', out_dtype=F32, name="kvproj", tm=tm, tn=KV_PAD)
            ckv = _rmsnorm(kvx, rep['g_kv_lat'], width=KV_RANK, name="ckvnorm")
            k1, k2 = _rope(kvx[:, KV_RANK:KV_RANK + ROPE_HALF], kvx[:, KV_RANK + ROPE_HALF:KV_RANK + ROPE_DIM],
                           cos, sin, name="krope")
            kr = jnp.concatenate([k1, k2], axis=-1)
            kv = {'x': x, 'xn': xn_kv, 'kvx': kvx, 'ckv': ckv, 'kr': kr, 'w_kv_a': wm['w_kv_a']}
        h = _rmsnorm(x, rep['g_mix'][l], name=f"mixnorm{l}")
        if l < N_A:
            z = _mm(h, wm['w_in'], dims='nn', out_dtype=BF16, name=f"in_a{l}", tm=tm, tn=512)
            main = _sgu_fwd(z, shared['g_v'][l], rep['w_sp'][l], b_sp_t[l], name=f"sgu{l}")
            qcol = 2 * G_W // MEM_W
        else:
            j = l - N_A
            z = _mm(h, wm['w_in'], dims='nn', out_dtype=BF16, name=f"in_b{j}", tm=tm, tn=1024)
            qn = _rmsnorm(z, rep['g_q_lat'][j], width=Q_RANK, name=f"qnorm{j}")
            qp = _mm(qn, wm['w_uqp'], dims='nn', out_dtype=BF16, name=f"uq{j}", tm=tm, tn=768)
            rr1, rr2 = _rope(qp, qp, cos12, sin12, col1=r1_col, col2=r1_col + 1, name=f"qrope{j}")
            qr = _heads_to_major(rr1, rr2)
            kn = _mm(kv['ckv'], wm['w_uk'], dims='nn', out_dtype=BF16, name=f"k_up{j}", tm=tm, tn=768)
            vv = _mm(kv['ckv'], wm['w_uv'], dims='nn', out_dtype=BF16, name=f"v_up{j}", tm=tm, tn=768)
            main, lse = _mha_fwd(qp, qr, kn, kv['kr'], vv, name=f"mha{j}")
            qcol = Q_RANK // MEM_W
            sv.update(qn=qn, qp=qp, qr=qr, kn=kn, vv=vv, lse=lse)
        wm.update(fetch(('rest', l), z))
        memn = _rmsnorm(mem, rep['g_mem'][l], name=f"memnorm{l}")
        kvm = _mm(memn, wm['w_mem_kv'], dims='nn', out_dtype=BF16, name=f"memkv{l}", tm=tm, tn=1024)
        mix = _memattn_fwd(z, kvm, main, qcol=qcol, name=f"memattn{l}")
        x_mid = _mm(mix, wm['w_out'], dims='nn', res=x, out_dtype=F32, name=f"out{l}", tm=tm, tn=1024)
        wf = fetch(('up', l), x_mid)
        h2 = _rmsnorm(x_mid, rep['g_ffn'][l], name=f"ffnnorm{l}")
        a = _mm(h2, wf['w_up'], dims='nn', b_blocked=True, out_dtype=BF16, out_block=bw,
                name=f"up{l}", tm=tm, tn=bw)
        act = _conv_fwd(a, shared['conv_w'], conv_b, l, name=f"conv{l}")
        wf.update(fetch(('down', l), act))
        x = _mm(act, wf['w_down'], dims='nn', res=x_mid, out_dtype=F32, name=f"down{l}", tm=512, tn=1024)
        sv.update(h=h, memn=memn, kvm=kvm, z=z, qcol=qcol, mix=mix, x_mid=x_mid, h2=h2, a=a, act=act, wf=wf)
        saved.append(sv)

    sq, dx, dg_final = _final_loss(x, target, rep['g_final'], name="loss")

    g = {k: [None] * DEPTH for k in ('g_mix', 'g_ffn', 'g_mem', 'conv_w', 'conv_b')}
    for k in ('g_v', 'w_sp', 'b_sp'):
        g[k] = [None] * N_A
    g['g_q_lat'] = [None] * n_b
    g['g_final'] = dg_final
    dckv_sum, dkr_sum = None, None

    for l in reversed(range(DEPTH)):
        sv = saved[l]
        wm, wf = sv['wm'], sv['wf']
        dact = _mm(dx, wf['w_down'], dims='nt', out_dtype=BF16, name=f"d_act{l}", tm=tm, tn=bw)
        dw_down = _mm(sv['act'], dx, dims='tn', out_dtype=BF16, name=f"dw_down{l}", tm=bw, tn=512)
        tok = emit(('down', l), {'w_ffn_down': dw_down})
        dc, dcw, dcb = _conv_bwd_dc(sv['a'], dact, shared['conv_w'], conv_b, l, after=tok, name=f"d_conv{l}")
        g['conv_w'][l], g['conv_b'][l] = dcw, dcb
        da = _conv_bwd_da(dc, shared['conv_w'], l, name=f"d_convin{l}")
        dw_up = _mm(sv['h2'], da, dims='tn', b_blocked=True, out_dtype=BF16, out_block=bw,
                    name=f"dw_up{l}", tm=512, tn=bw, n_outer=True)
        tok = emit(('up', l), {'w_ffn_up': dw_up})
        dh2 = _mm(da, wf['w_up'], dims='nt', a_blocked=True, b_blocked=True, tk=bw,
                  out_dtype=BF16, name=f"d_h2{l}", tm=tm, tn=1024, after=tok)
        dx_mid, g['g_ffn'][l] = _rmsnorm_bwd(sv['x_mid'], rep['g_ffn'][l], dh2, dres=dx, name=f"d_ffnnorm{l}")
        dmix = _mm(dx_mid, wm['w_out'], dims='nt', out_dtype=BF16, name=f"d_mix{l}", tm=tm, tn=1024)
        dw_out = _mm(sv['mix'], dx_mid, dims='tn', out_dtype=BF16, name=f"dw_out{l}", tm=1024, tn=256)
        dqm, dkvm = _memattn_bwd(sv['z'], sv['kvm'], dmix, qcol=sv['qcol'], name=f"d_memattn{l}")
        dw_memkv = _mm(sv['memn'], dkvm, dims='tn', out_dtype=BF16, name=f"dw_memkv{l}", tm=1024, tn=1024)
        tok = emit(('rest', l), {'w_out': dw_out, 'w_mem_kv': dw_memkv})
        gm = {}
        dmemn = _mm(dkvm, wm['w_mem_kv'], dims='nt', out_dtype=F32, name=f"d_memn{l}", tm=tm, tn=1024, after=tok)
        _, g['g_mem'][l] = _rmsnorm_bwd(mem, rep['g_mem'][l], dmemn, out_dtype=BF16, name=f"d_memnorm{l}")
        if l < N_A:
            dz, dwsp, dbsp_t, dgv = _sgu_bwd(sv['z'], dmix, dqm, shared['g_v'][l], rep['w_sp'][l], b_sp_t[l],
                                             name=f"d_sgu{l}")
            g['w_sp'][l], g['b_sp'][l], g['g_v'][l] = dwsp, dbsp_t.T, dgv
            dh = _mm(dz, wm['w_in'], dims='nt', out_dtype=BF16, name=f"d_h_a{l}", tm=tm, tn=1024)
            gm['w_in_a'] = _mm(sv['h'], dz, dims='tn', out_dtype=BF16, name=f"dw_in_a{l}", tm=1024, tn=512)
        else:
            j = l - N_A
            dq_nope, dqr, dkn, dkr, dvv = _mha_bwd(sv['qp'], sv['qr'], sv['kn'], kv['kr'], sv['vv'], sv['mix'], dmix,
                                                   sv['lse'], name=f"d_mha{j}")
            gm['w_uk'] = _mm(kv['ckv'], dkn, dims='tn', out_dtype=BF16, name=f"dw_uk{j}", tm=512, tn=768)
            gm['w_uv'] = _mm(kv['ckv'], dvv, dims='tn', out_dtype=BF16, name=f"dw_uv{j}", tm=512, tn=768)
            dckv = _mm(dkn, wm['w_uk'], dims='nt', out_dtype=F32, res=dckv_sum, name=f"d_ckv_k{j}", tm=tm, tn=512)
            dckv_sum = _mm(dvv, wm['w_uv'], dims='nt', out_dtype=F32, res=dckv, name=f"d_ckv_v{j}", tm=tm, tn=512)
            dkr_sum = dkr if dkr_sum is None else dkr_sum + dkr
            dr1, dr2 = _heads_from_major(dqr)
            dq1, dq2 = _rope(dr1, dr2, cos12, sin12, inverse=True, name=f"d_qrope{j}")
            dqp = jnp.concatenate([dq_nope, dq1, dq2], axis=-1)
            dqn = _mm(dqp, wm['w_uqp'], dims='nt', out_dtype=BF16, name=f"d_qn{j}", tm=tm, tn=512)
            gm['w_uqp'] = _mm(sv['qn'], dqp, dims='tn', out_dtype=BF16, name=f"dw_uq{j}", tm=512, tn=768)
            dqlat, g['g_q_lat'][j] = _rmsnorm_bwd(sv['z'], rep['g_q_lat'][j], dqn, width=Q_RANK, out_dtype=BF16,
                                                 name=f"d_qnorm{j}")
            dz = jnp.concatenate([dqlat, dqm], axis=-1)
            dh = _mm(dz, wm['w_in'], dims='nt', out_dtype=BF16, name=f"d_h_b{j}", tm=tm, tn=1024)
            gm['w_in_b'] = _mm(sv['h'], dz, dims='tn', out_dtype=BF16, name=f"dw_in_b{j}", tm=1024, tn=512)
        tok = emit(('mix', l), gm)
        dx, g['g_mix'][l] = _rmsnorm_bwd(sv['x_in'], rep['g_mix'][l], dh, dres=dx_mid, after=tok, name=f"d_mixnorm{l}")
        if l == N_A:
            dkvx_c, g['g_kv_lat'] = _rmsnorm_bwd(kv['kvx'], rep['g_kv_lat'], dckv_sum, width=KV_RANK, out_dtype=BF16,
                                                 name="d_ckvnorm")
            dk1, dk2 = _rope(dkr_sum[:, :ROPE_HALF], dkr_sum[:, ROPE_HALF:], cos, sin, inverse=True, name="d_krope")
            dkvx = jnp.concatenate([dkvx_c, dk1, dk2, jnp.zeros((s, KV_PAD - KV_RANK - ROPE_DIM), BF16)], axis=-1)
            dxn = _mm(dkvx, kv['w_kv_a'], dims='nt', out_dtype=BF16, name="d_kvnorm_in", tm=tm, tn=1024)
            dw_kv = _mm(kv['xn'], dkvx, dims='tn', out_dtype=BF16, name="dw_kv", tm=1024, tn=KV_PAD)
            tok = emit(('kv', 0), {'w_kv_a': dw_kv})
            dx, g['g_kv'] = _rmsnorm_bwd(kv['x'], rep['g_kv'], dxn, dres=dx, after=tok, name="d_kvnorm")
    return jnp.sum(sq), dx, g


MESH_IDS = pl.DeviceIdType.MESH
PEER_MASKS = tuple((k >> 2 & 1, k >> 1 & 1, k & 1) for k in range(1, N_DEV))
CHIP_MASKS = ((1, 0), (0, 1), (1, 1))
N_PEER = N_DEV - 1
SEMS_PER_BUFFER = 2 * N_PEER + 1
DATAFLOW = pltpu.SideEffectType.DATAFLOW_SIDE_EFFECTING
HBM_SPEC = pl.BlockSpec(memory_space=pltpu.HBM)
SEM_SPEC = pl.BlockSpec(memory_space=pltpu.SEMAPHORE)


def _my_position():
    return lax.axis_index("x"), lax.axis_index("y"), lax.axis_index("c")


def _flip(pos, mask):
    return tuple(1 - p if f else p for p, f in zip(pos, mask))


def _linear_id(pos):
    return 4 * pos[0] + 2 * pos[1] + pos[2]


def _hbm(x):
    return pltpu.with_memory_space_constraint(x, pltpu.HBM)


def _buffer_copies(src_ref, lead, land_ref, sems, scatter, near=False):
    me = _my_position()
    my_id = _linear_id(me)
    src = src_ref.at[lead] if lead else src_ref
    own = pltpu.make_async_copy(src.at[my_id] if scatter else src, land_ref.at[my_id], sems.at[2 * N_PEER])
    pairs = []
    for k, mask in enumerate(PEER_MASKS):
        if near and mask[2] == 1 and mask != (0, 0, 1):
            continue
        peer = _flip(me, mask)
        peer_id = _linear_id(peer)
        block = src.at[peer_id] if scatter else src
        send = pltpu.make_async_remote_copy(src_ref=block, dst_ref=land_ref.at[my_id], send_sem=sems.at[k],
                                            recv_sem=sems.at[N_PEER + k], device_id=peer, device_id_type=MESH_IDS)
        arrival = pltpu.make_async_remote_copy(src_ref=block, dst_ref=land_ref.at[peer_id], send_sem=sems.at[k],
                                               recv_sem=sems.at[N_PEER + k], device_id=peer, device_id_type=MESH_IDS)
        pairs.append((send, arrival))
    return own, pairs


def _exchange_start(srcs, buffers, *, name, scatter):
    ns, nb = len(srcs), len(buffers)
    lands = [_hbm(lax.empty((N_DEV,) + tuple(shape), dtype)) for _, _, shape, dtype, _ in buffers]

    def body(*refs):
        src_refs, land_refs = refs[:ns], refs[ns:ns + nb]
        sem_refs = refs[ns + nb:ns + 2 * nb]
        token = refs[-1]
        for b, (si, lead, _, _, near) in enumerate(buffers):
            own, pairs = _buffer_copies(src_refs[si], lead, land_refs[b], sem_refs[b], scatter, near)
            own.start()
            for send, _ in pairs:
                send.start()
        token[...] = jnp.zeros_like(token)

    out_shape = ([pltpu.SemaphoreType.DMA((SEMS_PER_BUFFER,))] * nb
                 + [pltpu.HBM(a.shape, a.dtype) for a in srcs]
                 + [pltpu.HBM(a.shape, a.dtype) for a in lands]
                 + [jax.ShapeDtypeStruct((8, 128), F32)])
    aliases = {i: nb + i for i in range(ns + nb)}
    outs = pl.pallas_call(
        body, name=name, in_specs=[HBM_SPEC] * (ns + nb),
        out_specs=[SEM_SPEC] * nb + [HBM_SPEC] * (ns + nb) + [pl.BlockSpec(memory_space=pltpu.VMEM)],
        out_shape=out_shape, input_output_aliases=aliases,
        compiler_params=pltpu.CompilerParams(has_side_effects=DATAFLOW),
    )(*[_hbm(a) for a in srcs], *lands)
    sems = list(outs[:nb])
    src_thru = list(outs[nb:nb + ns])
    land_thru = list(outs[nb + ns:nb + ns + nb])
    return sems, land_thru, src_thru, outs[-1]


def _exchange_wait(srcs_thru, buffers, sems, lands, after, *, name, scatter):
    ns, nb = len(srcs_thru), len(buffers)
    has_after = after is not None

    def body(*refs):
        src_refs, land_refs = refs[:ns], refs[ns:ns + nb]
        sem_refs = refs[ns + nb:ns + 2 * nb]
        for b, (si, lead, _, _, near) in enumerate(buffers):
            own, pairs = _buffer_copies(src_refs[si], lead, land_refs[b], sem_refs[b], scatter, near)
            for send, arrival in pairs:
                send.wait_send()
                arrival.wait_recv()
            own.wait()

    operands = list(srcs_thru) + list(lands) + list(sems) + ([after] if has_after else [])
    in_specs = ([HBM_SPEC] * (ns + nb) + [SEM_SPEC] * nb + ([pl.BlockSpec(memory_space=pl.ANY)] if has_after else []))
    outs = pl.pallas_call(
        body, name=name, in_specs=in_specs, out_specs=[HBM_SPEC] * nb,
        out_shape=[pltpu.HBM(a.shape, a.dtype) for a in lands],
        input_output_aliases={ns + b: b for b in range(nb)},
        compiler_params=pltpu.CompilerParams(has_side_effects=DATAFLOW),
    )(*operands)
    return list(outs)


def _exchange(arrays, *, name, scatter, near=None, after=None):
    n = len(arrays)
    near = [False] * n if near is None else near
    extra = [] if after is None else [after]
    out_shapes = [jax.ShapeDtypeStruct(a.shape if scatter else (N_DEV,) + a.shape, a.dtype) for a in arrays]

    def body(*refs):
        srcs, outs, sems = refs[:n], refs[n + len(extra):2 * n + len(extra)], refs[2 * n + len(extra):]
        started = []
        for a in range(n):
            own, pairs = _buffer_copies(srcs[a], (), outs[a], sems[a], scatter, near[a])
            own.start()
            for send, _ in pairs:
                send.start()
            started.append((own, pairs))
        for own, pairs in started:
            for send, arrival in pairs:
                arrival.wait_recv()
                send.wait_send()
            own.wait()

    any_spec = pl.BlockSpec(memory_space=pl.ANY)
    outs = pl.pallas_call(
        body, name=name, in_specs=[any_spec] * (n + len(extra)), out_specs=[any_spec] * n, out_shape=out_shapes,
        scratch_shapes=[pltpu.SemaphoreType.DMA((SEMS_PER_BUFFER,))] * n,
    )(*arrays, *extra)
    return list(outs)


def _forward_to_sibling(lands, *, name):
    n = len(lands)

    def body(*refs):
        ins, outs, sems = refs[:n], refs[n:2 * n], refs[2 * n:]
        me = _my_position()
        sibling = _flip(me, (0, 0, 1))
        pairs = []
        for b in range(n):
            for k, (fx, fy) in enumerate(CHIP_MASKS):
                mine = _linear_id(_flip(me, (fx, fy, 0)))
                theirs = _linear_id(_flip(me, (fx, fy, 1)))
                send = pltpu.make_async_remote_copy(
                    src_ref=ins[b].at[mine], dst_ref=outs[b].at[mine], send_sem=sems[b].at[k],
                    recv_sem=sems[b].at[len(CHIP_MASKS) + k], device_id=sibling, device_id_type=MESH_IDS)
                arrival = pltpu.make_async_remote_copy(
                    src_ref=ins[b].at[mine], dst_ref=outs[b].at[theirs], send_sem=sems[b].at[k],
                    recv_sem=sems[b].at[len(CHIP_MASKS) + k], device_id=sibling, device_id_type=MESH_IDS)
                send.start()
                pairs.append((send, arrival))
        for send, arrival in pairs:
            arrival.wait_recv()
            send.wait_send()

    any_spec = pl.BlockSpec(memory_space=pl.ANY)
    outs = pl.pallas_call(
        body, name=name, in_specs=[any_spec] * n, out_specs=[any_spec] * n,
        out_shape=[jax.ShapeDtypeStruct(a.shape, a.dtype) for a in lands],
        input_output_aliases={b: b for b in range(n)},
        scratch_shapes=[pltpu.SemaphoreType.DMA((2 * len(CHIP_MASKS),))] * n,
    )(*lands)
    return list(outs)


def _sum_slots(parts_ref):
    total = parts_ref[0].astype(F32)
    for q in range(1, parts_ref.shape[0]):
        total = total + parts_ref[q].astype(F32)
    return total


def _row_tile(rows, cols, n_arrays):
    budget = (12 * 1024 * 1024) // (4 * n_arrays * max(cols, 128))
    t = rows
    while t > budget and t % 2 == 0 and (t // 2) % 16 == 0:
        t //= 2
    return t


def _sum_adam(parts, w, m, v, layer, outs, *, name):
    q, r, c = parts.shape
    nl = w.shape[0]
    tr = _row_tile(r, c, q + 7)
    c1 = 1.0 - ADAM_B1 ** ADAM_STEP
    c2 = 1.0 - ADAM_B2 ** ADAM_STEP
    if outs is None:
        outs = [lax.empty((nl, r, c), F32) for _ in range(4)]

    def body(p_ref, w_ref, m_ref, v_ref, g_in, d_in, mo_in, vo_in, g_ref, d_ref, mo_ref, vo_ref):
        grad = _sum_slots(p_ref)
        m_new = ADAM_B1 * m_ref[...] + (1.0 - ADAM_B1) * grad
        v_new = ADAM_B2 * v_ref[...] + (1.0 - ADAM_B2) * (grad * grad)
        m_hat = m_new / c1
        v_hat = v_new / c2
        g_ref[...] = grad
        d_ref[...] = -ADAM_LR * (m_hat / (jnp.sqrt(v_hat) + ADAM_EPS) + ADAM_WD * w_ref[...])
        mo_ref[...] = m_new
        vo_ref[...] = v_new

    tile = pl.BlockSpec((None, tr, c), lambda i: (layer, i, 0))
    any_spec = pl.BlockSpec(memory_space=pl.ANY)
    return pl.pallas_call(
        body, name=name, grid=(r // tr,),
        in_specs=[pl.BlockSpec((q, tr, c), lambda i: (0, i, 0)), tile, tile, tile] + [any_spec] * 4,
        out_specs=[tile] * 4, out_shape=[jax.ShapeDtypeStruct((nl, r, c), F32)] * 4,
        input_output_aliases={4: 0, 5: 1, 6: 2, 7: 3},
        compiler_params=_params(("parallel",)),
    )(parts, w, m, v, *outs)


def _sum_parts(parts, *, name):
    q, r, c = parts.shape

    def body(p_ref, o_ref):
        o_ref[...] = _sum_slots(p_ref)

    return pl.pallas_call(
        body, name=name, in_specs=[pl.BlockSpec((q, r, c), lambda: (0, 0, 0))],
        out_specs=pl.BlockSpec((r, c), lambda: (0, 0)), out_shape=jax.ShapeDtypeStruct((r, c), F32),
        compiler_params=_params(),
    )(parts)


INPUT_NAMES = (['x', 'mem', 'positions'] + WEIGHTS + ['loss_target'] + ['m_' + n for n in WEIGHTS]
               + ['v_' + n for n in WEIGHTS])
SMALL_ALIGN = N_DEV * 8 * 128
TWO_LEVEL_LAYERS = N_A
GROUP_ORDER = ('in', 'rest', 'up', 'down')
GROUP_WEIGHTS = {'in': (['w_in_a'], ['w_in_b', 'w_uq', 'w_uk', 'w_uv']), 'rest': (['w_mem_kv', 'w_out'],) * 2,
                 'up': (['w_ffn_up'],) * 2, 'down': (['w_ffn_down'],) * 2}
LAYERED = {'w_in_a': 0, 'w_in_b': N_A, 'w_uq': N_A, 'w_uk': N_A, 'w_uv': N_A, 'w_mem_kv': 0, 'w_out': 0,
           'w_ffn_up': 0, 'w_ffn_down': 0}


def _permute_uq(w_uq):
    r = w_uq.shape[0]
    q = w_uq.reshape(r, G_HEADS, HEAD + ROPE_DIM)
    return jnp.concatenate([q[..., :HEAD].reshape(r, -1), q[..., HEAD:HEAD + ROPE_HALF].reshape(r, -1),
                            q[..., HEAD + ROPE_HALF:].reshape(r, -1)], axis=-1)


def _unpermute_uq(w_uqp):
    r = w_uqp.shape[0]
    nope = w_uqp[..., :G_W].reshape(r, G_HEADS, HEAD)
    r1 = w_uqp[..., G_W:G_W + G_HEADS * ROPE_HALF].reshape(r, G_HEADS, ROPE_HALF)
    r2 = w_uqp[..., G_W + G_HEADS * ROPE_HALF:].reshape(r, G_HEADS, ROPE_HALF)
    return jnp.concatenate([nope, r1, r2], axis=-1).reshape(r, -1)


def _cols_from_stack(st):
    _, r, n = st.shape
    return st.transpose(1, 0, 2).reshape(r, N_DEV * n)


def _cols_to_stack(wh):
    r, c = wh.shape
    return wh.reshape(r, N_DEV, c // N_DEV).transpose(1, 0, 2)


def _group_weights(group):
    kind, l = group
    return GROUP_WEIGHTS[kind][0 if l < N_A else 1]


def _step(args):
    p = dict(zip(INPUT_NAMES, args))
    x, mem, positions, target = p['x'][0], p['mem'][0], p['positions'][0], p['loss_target'][0]
    d = x.shape[-1]
    my_id = _linear_id(_my_position())

    w_kv_pad = jnp.pad(p['w_kv_a'], ((0, 0), (0, KV_PAD - p['w_kv_a'].shape[1])))
    shard = {k: p[k].astype(BF16) for k in LAYERED}
    shard['w_uk'] = shard['w_uk'].reshape(shard['w_uk'].shape[0], shard['w_uk'].shape[1], -1)
    shard['w_uv'] = shard['w_uv'].reshape(shard['w_uv'].shape[0], shard['w_uv'].shape[1], -1)
    shard.update(conv_w=p['conv_w'], g_v=p['g_v'], w_kv_a=w_kv_pad.astype(BF16))
    src_names = list(shard)
    gather_groups = []
    for l in range(DEPTH):
        gather_groups += [(kind, l) for kind in GROUP_ORDER]
    buffers, owner = [], []
    for group in gather_groups:
        kind, l = group
        for k in _group_weights(group):
            buffers.append((src_names.index(k), (l - LAYERED[k],), shard[k].shape[1:], shard[k].dtype, l < TWO_LEVEL_LAYERS))
            owner.append((group, k))
        if group == ('in', 0):
            for k in ('g_v', 'conv_w'):
                buffers.append((src_names.index(k), (), shard[k].shape, shard[k].dtype, True))
                owner.append((group, k))
        if group == ('in', N_A):
            buffers.append((src_names.index('w_kv_a'), (), shard['w_kv_a'].shape, BF16, False))
            owner.append((group, 'w_kv_a'))
    g_sems, g_lands, g_srcs, _ = _exchange_start([shard[k] for k in src_names], buffers, name="gather_start",
                                                 scatter=False)

    def fetch(group, after):
        idx = [i for i, (grp, _) in enumerate(owner) if grp == group]
        landed = _exchange_wait(g_srcs, [buffers[i] for i in idx], [g_sems[i] for i in idx],
                                [g_lands[i] for i in idx], after, name=f"gather_wait_{group[0]}{group[1]}",
                                scatter=False)
        if group[1] < TWO_LEVEL_LAYERS:
            landed = _forward_to_sibling(landed, name=f"gather_forward_{group[0]}{group[1]}")
        got = {owner[i][1]: t for i, t in zip(idx, landed)}
        out = {}
        for k, t in got.items():
            if k in ('w_in_a', 'w_uq'):
                out[k] = _cols_from_stack(t)
            elif k == 'g_v':
                out[k] = t.transpose(1, 0, 2).reshape(t.shape[1], -1)
            elif k in ('w_ffn_up', 'conv_w'):
                out[k] = t
            else:
                out[k] = t.reshape(-1, t.shape[-1])
        if 'w_uq' in out:
            out['w_uqp'] = _permute_uq(out.pop('w_uq'))
        for old, new in (('w_in_a', 'w_in'), ('w_in_b', 'w_in'), ('w_ffn_up', 'w_up'), ('w_ffn_down', 'w_down')):
            if old in out:
                out[new] = out.pop(old)
        return out

    pending = []

    def emit(group, grads):
        send = {}
        for k, t in grads.items():
            if k == 'w_in_a':
                send[k] = _cols_to_stack(t)
            elif k == 'w_uqp':
                send['w_uq'] = _cols_to_stack(_unpermute_uq(t))
            elif k == 'w_ffn_up':
                send[k] = t
            elif k == 'w_kv_a':
                cols = p['w_kv_a'].shape[1]
                send[k] = t[:, :cols].reshape(N_DEV, -1, cols)
            else:
                send[k] = t.reshape(N_DEV, t.shape[0] // N_DEV, t.shape[1])
        keys = list(send)
        bufs = [(i, (), send[k].shape[1:], send[k].dtype, False) for i, k in enumerate(keys)]
        sems, lands, srcs, token = _exchange_start([send[k] for k in keys], bufs,
                                                   name=f"scatter_start_{group[0]}{group[1]}", scatter=True)
        pending.append((group, keys, bufs, sems, lands, srcs))
        return token

    rep = {k: p[k] for k in REPLICATED}
    sq, grad_x, g = _local_step(x, mem, positions, target, rep, fetch, emit)
    loss = (0.5 / d) * lax.psum(sq, ("x", "y", "c"))

    out, running = {}, {}
    for group, keys, bufs, sems, lands, srcs in pending:
        landed = _exchange_wait(srcs, bufs, sems, lands, grad_x, name=f"scatter_wait_{group[0]}{group[1]}", scatter=True)
        for k, parts in zip(keys, landed):
            stacked = k in LAYERED
            nl = p[k].shape[0] if stacked else 1
            layer = group[1] - LAYERED[k] if stacked else 0
            rows = p[k].size // nl // p[k].shape[-1]
            view = (nl, rows, p[k].shape[-1])
            running[k] = _sum_adam(parts.reshape(N_DEV, rows, view[2]), p[k].reshape(view), p['m_' + k].reshape(view),
                                   p['v_' + k].reshape(view), layer, running.get(k), name=f"adam_{k}{layer}")
    for k, res in running.items():
        out[k] = [t.reshape(p[k].shape) for t in res]

    small = {
        'g_mix': jnp.concatenate(g['g_mix']), 'g_ffn': jnp.concatenate(g['g_ffn']), 'g_final': g['g_final'],
        'w_sp': jnp.stack(g['w_sp']), 'b_sp': jnp.stack(g['b_sp']), 'g_kv': g['g_kv'], 'g_kv_lat': g['g_kv_lat'],
        'g_q_lat': jnp.concatenate(g['g_q_lat']), 'g_mem': jnp.concatenate(g['g_mem']),
        'conv_b': jnp.stack(g['conv_b']),
        'g_v': jnp.concatenate(g['g_v']),
        'conv_w': jnp.stack(g['conv_w']).transpose(0, 2, 1, 3),
    }
    small_names = REPLICATED + SMALL_SHARDED
    flat = jnp.concatenate([small[k].reshape(-1).astype(F32) for k in small_names])
    n_small = flat.shape[0]
    padded = -(-n_small // SMALL_ALIGN) * SMALL_ALIGN
    flat = jnp.pad(flat, (0, padded - n_small)).reshape(N_DEV, -1, 128)
    last_update = out[pending[-1][1][-1]][1]
    (small_parts,) = _exchange([flat], name="scatter_small", scatter=True, after=last_update)
    reduced = _sum_parts(small_parts, name="sum_small")
    (small_all,) = _exchange([reduced], name="gather_small", scatter=False)
    small_all = small_all.reshape(-1)
    grads_small, off = {}, 0
    for k in small_names:
        size = small[k].size
        grads_small[k] = small_all[off:off + size].reshape(small[k].shape)
        off += size
    grads_small['g_v'] = lax.dynamic_slice_in_dim(grads_small['g_v'], my_id * p['g_v'].shape[1], p['g_v'].shape[1], axis=1)
    grads_small['conv_w'] = lax.dynamic_index_in_dim(grads_small['conv_w'], my_id, axis=2, keepdims=False)
    gs = jnp.concatenate([grads_small[k].reshape(-1) for k in small_names])
    n_loc = gs.shape[0]
    pad_loc = -(-n_loc // 1024) * 1024 - n_loc

    def pack(prefix):
        t = jnp.concatenate([p[prefix + k].reshape(-1) for k in small_names])
        return jnp.pad(t, (0, pad_loc)).reshape(1, -1, 128)

    res = _sum_adam(jnp.pad(gs, (0, pad_loc)).reshape(1, -1, 128), pack(''), pack('m_'), pack('v_'), 0, None,
                    name="adam_small")
    off = 0
    for k in small_names:
        size = p[k].size
        out[k] = [t.reshape(-1)[off:off + size].reshape(p[k].shape) for t in res]
        off += size

    outs = [loss, grad_x[None]]
    for i in range(4):
        outs += [out[k][i] for k in WEIGHTS]
    return tuple(outs)


def kernel(x, mem, positions, g_mix, g_ffn, g_final, w_in_a, g_v, w_sp, b_sp, g_kv, w_kv_a, g_kv_lat, w_in_b, g_q_lat, w_uq, w_uk, w_uv, g_mem, w_mem_kv, w_out, w_ffn_up, conv_w, conv_b, w_ffn_down, loss_target, m_g_mix, m_g_ffn, m_g_final, m_w_in_a, m_g_v, m_w_sp, m_b_sp, m_g_kv, m_w_kv_a, m_g_kv_lat, m_w_in_b, m_g_q_lat, m_w_uq, m_w_uk, m_w_uv, m_g_mem, m_w_mem_kv, m_w_out, m_w_ffn_up, m_conv_w, m_conv_b, m_w_ffn_down, v_g_mix, v_g_ffn, v_g_final, v_w_in_a, v_g_v, v_w_sp, v_b_sp, v_g_kv, v_w_kv_a, v_g_kv_lat, v_w_in_b, v_g_q_lat, v_w_uq, v_w_uk, v_w_uv, v_g_mem, v_w_mem_kv, v_w_out, v_w_ffn_up, v_conv_w, v_conv_b, v_w_ffn_down):
    return _step((x, mem, positions, g_mix, g_ffn, g_final, w_in_a, g_v, w_sp, b_sp, g_kv, w_kv_a, g_kv_lat, w_in_b, g_q_lat, w_uq, w_uk, w_uv, g_mem, w_mem_kv, w_out, w_ffn_up, conv_w, conv_b, w_ffn_down, loss_target, m_g_mix, m_g_ffn, m_g_final, m_w_in_a, m_g_v, m_w_sp, m_b_sp, m_g_kv, m_w_kv_a, m_g_kv_lat, m_w_in_b, m_g_q_lat, m_w_uq, m_w_uk, m_w_uv, m_g_mem, m_w_mem_kv, m_w_out, m_w_ffn_up, m_conv_w, m_conv_b, m_w_ffn_down, v_g_mix, v_g_ffn, v_g_final, v_w_in_a, v_g_v, v_w_sp, v_b_sp, v_g_kv, v_w_kv_a, v_g_kv_lat, v_w_in_b, v_g_q_lat, v_w_uq, v_w_uk, v_w_uv, v_g_mem, v_w_mem_kv, v_w_out, v_w_ffn_up, v_conv_w, v_conv_b, v_w_ffn_down))
```

```python
import math

import jax
import jax.numpy as jnp
from jax import lax
from jax.experimental import pallas as pl
from jax.experimental.pallas import tpu as pltpu

F32 = jnp.float32
BF16 = jnp.bfloat16

N_DEV = 8
N_A = 2
DEPTH = 4
G_HEADS = 12
HEAD = 128
CHUNK = 128
MEM_HEADS = 4
MEM_W = MEM_HEADS * HEAD
G_W = G_HEADS * HEAD
ROPE_DIM = 64
ROPE_HALF = ROPE_DIM // 2
KV_RANK = 512
Q_RANK = 512
KV_PAD = 640
ROPE_THETA = 10000.0
EPS = 1e-6
CONV_W = 3

ADAM_LR = 0.001
ADAM_B1 = 0.9
ADAM_B2 = 0.999
ADAM_EPS = 1e-08
ADAM_WD = 0.01
ADAM_STEP = 10

VMEM_LIMIT_V7X = 56 * 1024 * 1024
MASK_VALUE = -1e30

WEIGHTS = ['g_mix', 'g_ffn', 'g_final', 'w_in_a', 'g_v', 'w_sp', 'b_sp', 'g_kv', 'w_kv_a', 'g_kv_lat',
           'w_in_b', 'g_q_lat', 'w_uq', 'w_uk', 'w_uv', 'g_mem', 'w_mem_kv', 'w_out', 'w_ffn_up',
           'conv_w', 'conv_b', 'w_ffn_down']
REPLICATED = ['g_mix', 'g_ffn', 'g_final', 'w_sp', 'b_sp', 'g_kv', 'g_kv_lat', 'g_q_lat', 'g_mem', 'conv_b']
SMALL_SHARDED = ['g_v', 'conv_w']


def _params(sem=None):
    return pltpu.CompilerParams(dimension_semantics=sem, vmem_limit_bytes=VMEM_LIMIT_V7X)


def _dot(a, b, dims):
    contract = {'nn': ((1,), (0,)), 'nt': ((1,), (1,)), 'tn': ((0,), (0,))}[dims]
    return lax.dot_general(a, b, (contract, ((), ())), preferred_element_type=F32)


def _erf(x):
    return lax.erf(x)


def _gelu(x):
    return 0.5 * x * (1.0 + _erf(x * (2.0 ** -0.5)))


def _gelu_grad(x):
    cdf = 0.5 * (1.0 + _erf(x * (2.0 ** -0.5)))
    pdf = jnp.exp(-0.5 * x * x) * (1.0 / math.sqrt(2.0 * math.pi))
    return cdf + x * pdf


def _sigmoid(x):
    return 1.0 / (1.0 + jnp.exp(-x))


def _operand_spec(shape, lead, blocked, tr, tc, ridx, cidx):
    if blocked:
        per = shape[-1] // tc
        assert shape[-1] % tc == 0, (shape, tc)
        return pl.BlockSpec(
            (None,) * (1 + len(lead)) + (tr, tc),
            lambda *g: (cidx(*g) // per,) + lead + (ridx(*g), cidx(*g) % per))
    return pl.BlockSpec((None,) * len(lead) + (tr, tc), lambda *g: lead + (ridx(*g), cidx(*g)))


def _view2d(x, blocked):
    return (x.shape[-2], x.shape[0] * x.shape[-1]) if blocked else (x.shape[-2], x.shape[-1])


def _mm(a, b, *, dims, out_dtype, name, tm, tn, tk=None, res=None, a_lead=(), b_lead=(),
        a_blocked=False, b_blocked=False, out_block=None, n_outer=False, after=None):
    ar, ac = _view2d(a, a_blocked)
    br, bc = _view2d(b, b_blocked)
    m, k = (ac, ar) if dims == 'tn' else (ar, ac)
    n, k2 = (br, bc) if dims == 'nt' else (bc, br)
    assert k == k2, (a.shape, b.shape, dims)
    tm, tn = min(tm, m), min(tn, n)
    tk = k if tk is None else tk
    assert m % tm == 0 and n % tn == 0 and k % tk == 0, (name, m, n, k, tm, tn, tk)
    nk = k // tk
    if n_outer:
        gi, gj = (lambda g0, g1, g2: g1), (lambda g0, g1, g2: g0)
        grid = (n // tn, m // tm, nk)
    else:
        gi, gj = (lambda g0, g1, g2: g0), (lambda g0, g1, g2: g1)
        grid = (m // tm, n // tn, nk)
    gk = lambda g0, g1, g2: g2

    if dims == 'tn':
        a_spec = _operand_spec(a.shape, a_lead, a_blocked, tk, tm, gk, gi)
    else:
        a_spec = _operand_spec(a.shape, a_lead, a_blocked, tm, tk, gi, gk)
    if dims == 'nt':
        b_spec = _operand_spec(b.shape, b_lead, b_blocked, tn, tk, gj, gk)
    else:
        b_spec = _operand_spec(b.shape, b_lead, b_blocked, tk, tn, gk, gj)
    in_specs = [a_spec, b_spec]
    operands = [a, b]
    if res is not None:
        in_specs.append(pl.BlockSpec((tm, tn), lambda *g: (gi(*g), gj(*g))))
        operands.append(res)
    if after is not None:
        in_specs.append(pl.BlockSpec(memory_space=pl.ANY))
        operands.append(after)
    n_in = len(operands)
    if out_block is not None:
        out_shape = jax.ShapeDtypeStruct((n // out_block, m, out_block), out_dtype)
        out_spec = _operand_spec(out_shape.shape, (), True, tm, tn, gi, gj)
    else:
        out_shape = jax.ShapeDtypeStruct((m, n), out_dtype)
        out_spec = pl.BlockSpec((tm, tn), lambda *g: (gi(*g), gj(*g)))

    def body(*refs):
        a_ref, b_ref = refs[0], refs[1]
        r_ref = refs[2] if res is not None else None
        o_ref = refs[n_in]
        acc_ref = refs[-1] if nk > 1 else None
        part = _dot(a_ref[...].astype(BF16), b_ref[...].astype(BF16), dims)

        def finish(total):
            if r_ref is not None:
                total = total + r_ref[...]
            o_ref[...] = total.astype(o_ref.dtype)

        if nk == 1:
            finish(part)
        else:
            kk = pl.program_id(2)

            @pl.when(kk == 0)
            def _():
                acc_ref[...] = part

            @pl.when(kk > 0)
            def _():
                acc_ref[...] += part

            @pl.when(kk == nk - 1)
            def _():
                finish(acc_ref[...])

    scratch = [pltpu.VMEM((tm, tn), F32)] if nk > 1 else []
    return pl.pallas_call(
        body, name=name, grid=grid, in_specs=in_specs, out_specs=out_spec,
        out_shape=out_shape, scratch_shapes=scratch,
        compiler_params=_params(("parallel", "parallel", "arbitrary")),
    )(*operands)


def _mm_blocked_nt(a, b, *, out_dtype, name, tm, tn, blocks_per_step, after=None):
    nb, m, bw = a.shape
    n = b.shape[1]
    tm, tn = min(tm, m), min(tn, n)
    assert nb % blocks_per_step == 0 and m % tm == 0 and n % tn == 0
    nk = nb // blocks_per_step

    def body(*refs):
        a_ref, b_ref, o_ref, acc_ref = refs[0], refs[1], refs[-2], refs[-1]
        kk = pl.program_id(2)
        part = _dot(a_ref[0], b_ref[0], 'nt')
        for t in range(1, blocks_per_step):
            part = part + _dot(a_ref[t], b_ref[t], 'nt')

        @pl.when(kk == 0)
        def _():
            acc_ref[...] = part

        @pl.when(kk > 0)
        def _():
            acc_ref[...] += part

        @pl.when(kk == nk - 1)
        def _():
            o_ref[...] = acc_ref[...].astype(o_ref.dtype)

    in_specs = [pl.BlockSpec((blocks_per_step, tm, bw), lambda i, j, k: (k, i, 0)),
                pl.BlockSpec((blocks_per_step, tn, bw), lambda i, j, k: (k, j, 0))]
    operands = [a, b]
    if after is not None:
        in_specs.append(pl.BlockSpec(memory_space=pl.ANY))
        operands.append(after)
    return pl.pallas_call(
        body, name=name, grid=(m // tm, n // tn, nk), in_specs=in_specs,
        out_specs=pl.BlockSpec((tm, tn), lambda i, j, k: (i, j)), out_shape=jax.ShapeDtypeStruct((m, n), out_dtype),
        scratch_shapes=[pltpu.VMEM((tm, tn), F32)],
        compiler_params=_params(("parallel", "parallel", "arbitrary")),
    )(*operands)


def _rmsnorm(x, g, *, name, width=None, out_dtype=BF16, tm=512):
    s = x.shape[0]
    w = x.shape[1] if width is None else width
    tm = min(tm, s)

    def body(x_ref, g_ref, o_ref):
        xv = x_ref[...].astype(F32)
        rstd = lax.rsqrt(jnp.mean(xv * xv, axis=-1, keepdims=True) + EPS)
        o_ref[...] = (xv * rstd * g_ref[...]).astype(o_ref.dtype)

    return pl.pallas_call(
        body, name=name, grid=(s // tm,),
        in_specs=[pl.BlockSpec((tm, w), lambda i: (i, 0)), pl.BlockSpec((1, w), lambda i: (0, 0))],
        out_specs=pl.BlockSpec((tm, w), lambda i: (i, 0)),
        out_shape=jax.ShapeDtypeStruct((s, w), out_dtype),
        compiler_params=_params(("parallel",)),
    )(x, g.reshape(1, w))


def _rmsnorm_bwd(x, g, dy, *, name, width=None, dres=None, after=None, out_dtype=F32, tm=512):
    s = x.shape[0]
    w = x.shape[1] if width is None else width
    tm = min(tm, s)

    def body(*refs):
        x_ref, g_ref, dy_ref = refs[0], refs[1], refs[2]
        r_ref = refs[3] if dres is not None else None
        dx_ref, dg_ref = refs[-2], refs[-1]
        xv = x_ref[...].astype(F32)
        rstd = lax.rsqrt(jnp.mean(xv * xv, axis=-1, keepdims=True) + EPS)
        xhat = xv * rstd
        dyv = dy_ref[...].astype(F32)
        gdy = dyv * g_ref[...]
        dx = rstd * (gdy - xhat * jnp.mean(gdy * xhat, axis=-1, keepdims=True))
        if r_ref is not None:
            dx = dx + r_ref[...]
        dx_ref[...] = dx.astype(dx_ref.dtype)
        part = jnp.sum(dyv * xhat, axis=0, keepdims=True)

        @pl.when(pl.program_id(0) == 0)
        def _():
            dg_ref[...] = part

        @pl.when(pl.program_id(0) > 0)
        def _():
            dg_ref[...] += part

    row = pl.BlockSpec((tm, w), lambda i: (i, 0))
    vec = pl.BlockSpec((1, w), lambda i: (0, 0))
    in_specs = [row, vec, row] + ([row] if dres is not None else [])
    operands = [x, g.reshape(1, w), dy] + ([dres] if dres is not None else [])
    if after is not None:
        in_specs.append(pl.BlockSpec(memory_space=pl.ANY))
        operands.append(after)
    return pl.pallas_call(
        body, name=name, grid=(s // tm,), in_specs=in_specs, out_specs=[row, vec],
        out_shape=[jax.ShapeDtypeStruct((s, w), out_dtype), jax.ShapeDtypeStruct((1, w), F32)],
        compiler_params=_params(("arbitrary",)),
    )(*operands)


def _final_loss(x, target, g, *, name, tm=256):
    s, d = x.shape
    tm = min(tm, s)

    def body(x_ref, t_ref, g_ref, sq_ref, dx_ref, dg_ref):
        xv = x_ref[...]
        rstd = lax.rsqrt(jnp.mean(xv * xv, axis=-1, keepdims=True) + EPS)
        xhat = xv * rstd
        err = xhat * g_ref[...] - t_ref[...]
        dyv = err * (1.0 / d)
        gdy = dyv * g_ref[...]
        dx_ref[...] = rstd * (gdy - xhat * jnp.mean(gdy * xhat, axis=-1, keepdims=True))
        sq = jnp.sum(err * err, axis=0, keepdims=True)
        dg = jnp.sum(dyv * xhat, axis=0, keepdims=True)

        @pl.when(pl.program_id(0) == 0)
        def _():
            sq_ref[...] = sq
            dg_ref[...] = dg

        @pl.when(pl.program_id(0) > 0)
        def _():
            sq_ref[...] += sq
            dg_ref[...] += dg

    row = pl.BlockSpec((tm, d), lambda i: (i, 0))
    vec = pl.BlockSpec((1, d), lambda i: (0, 0))
    return pl.pallas_call(
        body, name=name, grid=(s // tm,), in_specs=[row, row, vec], out_specs=[vec, row, vec],
        out_shape=[jax.ShapeDtypeStruct((1, d), F32), jax.ShapeDtypeStruct((s, d), F32),
                   jax.ShapeDtypeStruct((1, d), F32)],
        compiler_params=_params(("arbitrary",)),
    )(x, target, g.reshape(1, d))


def _tril_mask():
    t = lax.broadcasted_iota(jnp.int32, (CHUNK, CHUNK), 0)
    s = lax.broadcasted_iota(jnp.int32, (CHUNK, CHUNK), 1)
    return t >= s


def _sgu_fwd(z, g_v, w_sp, b_sp_t, *, name):
    s = z.shape[0]

    def body(zu_ref, zv_ref, g_ref, w_ref, b_ref, o_ref):
        u = _gelu(zu_ref[...].astype(F32))
        gv = _gelu(zv_ref[...].astype(F32))
        rstd = lax.rsqrt(jnp.mean(gv * gv, axis=-1, keepdims=True) + EPS)
        v = (gv * rstd * g_ref[...]).astype(BF16)
        mask = _tril_mask()
        for grp in range(G_HEADS):
            cols = slice(grp * HEAD, (grp + 1) * HEAD)
            wm = jnp.where(mask, w_ref[grp], 0.0).astype(BF16)
            sv = _dot(wm, v[:, cols], 'nn') + b_ref[:, grp:grp + 1]
            o_ref[:, cols] = (u[:, cols] * sv).astype(o_ref.dtype)

    return pl.pallas_call(
        body, name=name, grid=(s // CHUNK,),
        in_specs=[pl.BlockSpec((CHUNK, G_W), lambda i: (i, 0)),
                  pl.BlockSpec((CHUNK, G_W), lambda i: (i, 1)),
                  pl.BlockSpec((1, G_W), lambda i: (0, 0)),
                  pl.BlockSpec((G_HEADS, CHUNK, CHUNK), lambda i: (0, 0, 0)),
                  pl.BlockSpec((CHUNK, G_HEADS), lambda i: (0, 0))],
        out_specs=pl.BlockSpec((CHUNK, G_W), lambda i: (i, 0)),
        out_shape=jax.ShapeDtypeStruct((s, G_W), BF16),
        compiler_params=_params(("parallel",)),
    )(z, z, g_v.reshape(1, G_W), w_sp, b_sp_t)


def _sgu_bwd(z, dmix, dqm, g_v, w_sp, b_sp_t, *, name):
    s = z.shape[0]
    zw = z.shape[1]

    def body(zu_ref, zv_ref, dm_ref, dq_ref, g_ref, w_ref, b_ref, dz_ref, dw_ref, db_ref, dg_ref):
        first = pl.program_id(0) == 0

        @pl.when(first)
        def _():
            dw_ref[...] = jnp.zeros_like(dw_ref)
            db_ref[...] = jnp.zeros_like(db_ref)
            dg_ref[...] = jnp.zeros_like(dg_ref)

        zu = zu_ref[...].astype(F32)
        zv = zv_ref[...].astype(F32)
        dmain = dm_ref[...].astype(F32)
        u = _gelu(zu)
        gv = _gelu(zv)
        rstd = lax.rsqrt(jnp.mean(gv * gv, axis=-1, keepdims=True) + EPS)
        vhat = gv * rstd
        gvec = g_ref[...]
        v = (vhat * gvec).astype(BF16)
        dsv = dmain * u
        dsv_b = dsv.astype(BF16)
        mask = _tril_mask()
        dv_parts = []
        for grp in range(G_HEADS):
            cols = slice(grp * HEAD, (grp + 1) * HEAD)
            wm = jnp.where(mask, w_ref[grp], 0.0).astype(BF16)
            sv = _dot(wm, v[:, cols], 'nn') + b_ref[:, grp:grp + 1]
            dz_ref[:, cols] = (dmain[:, cols] * sv * _gelu_grad(zu[:, cols])).astype(dz_ref.dtype)
            dwg = _dot(dsv_b[:, cols], v[:, cols], 'nt')
            dw_ref[grp] += jnp.where(mask, dwg, 0.0)
            db_ref[:, grp:grp + 1] += jnp.sum(dsv[:, cols], axis=-1, keepdims=True)
            dv_parts.append(_dot(wm, dsv_b[:, cols], 'tn'))
        dv = jnp.concatenate(dv_parts, axis=-1)
        dg_ref[...] += jnp.sum(dv * vhat, axis=0, keepdims=True)
        gdv = dv * gvec
        dgv = rstd * (gdv - vhat * jnp.mean(gdv * vhat, axis=-1, keepdims=True))
        dz_ref[:, G_W:2 * G_W] = (dgv * _gelu_grad(zv)).astype(dz_ref.dtype)
        dz_ref[:, 2 * G_W:] = dq_ref[...].astype(dz_ref.dtype)

    return pl.pallas_call(
        body, name=name, grid=(s // CHUNK,),
        in_specs=[pl.BlockSpec((CHUNK, G_W), lambda i: (i, 0)),
                  pl.BlockSpec((CHUNK, G_W), lambda i: (i, 1)),
                  pl.BlockSpec((CHUNK, G_W), lambda i: (i, 0)),
                  pl.BlockSpec((CHUNK, MEM_W), lambda i: (i, 0)),
                  pl.BlockSpec((1, G_W), lambda i: (0, 0)),
                  pl.BlockSpec((G_HEADS, CHUNK, CHUNK), lambda i: (0, 0, 0)),
                  pl.BlockSpec((CHUNK, G_HEADS), lambda i: (0, 0))],
        out_specs=[pl.BlockSpec((CHUNK, zw), lambda i: (i, 0)),
                   pl.BlockSpec((G_HEADS, CHUNK, CHUNK), lambda i: (0, 0, 0)),
                   pl.BlockSpec((CHUNK, G_HEADS), lambda i: (0, 0)),
                   pl.BlockSpec((1, G_W), lambda i: (0, 0))],
        out_shape=[jax.ShapeDtypeStruct((s, zw), BF16),
                   jax.ShapeDtypeStruct((G_HEADS, CHUNK, CHUNK), F32),
                   jax.ShapeDtypeStruct((CHUNK, G_HEADS), F32),
                   jax.ShapeDtypeStruct((1, G_W), F32)],
        compiler_params=_params(("arbitrary",)),
    )(z, z, dmix, dqm, g_v.reshape(1, G_W), w_sp, b_sp_t)


def _mem_probs(q, k):
    sc = _dot(q, k, 'nt') * (HEAD ** -0.5)
    sc = sc - jnp.max(sc, axis=-1, keepdims=True)
    e = jnp.exp(sc)
    return e / jnp.sum(e, axis=-1, keepdims=True)


def _memattn_fwd(z, kvm, main, *, qcol, name, tm=512):
    s = z.shape[0]
    m = kvm.shape[0]
    tm = min(tm, s)

    def body(q_ref, kv_ref, main_ref, o_ref):
        o_ref[:, :G_W] = main_ref[...]
        for h in range(MEM_HEADS):
            cols = slice(h * HEAD, (h + 1) * HEAD)
            k = kv_ref[:, cols]
            v = kv_ref[:, MEM_W + h * HEAD:MEM_W + (h + 1) * HEAD]
            p = _mem_probs(q_ref[:, cols], k)
            o_ref[:, G_W + h * HEAD:G_W + (h + 1) * HEAD] = _dot(p.astype(BF16), v, 'nn').astype(o_ref.dtype)

    return pl.pallas_call(
        body, name=name, grid=(s // tm,),
        in_specs=[pl.BlockSpec((tm, MEM_W), lambda i: (i, qcol)),
                  pl.BlockSpec((m, 2 * MEM_W), lambda i: (0, 0)),
                  pl.BlockSpec((tm, G_W), lambda i: (i, 0))],
        out_specs=pl.BlockSpec((tm, G_W + MEM_W), lambda i: (i, 0)),
        out_shape=jax.ShapeDtypeStruct((s, G_W + MEM_W), BF16),
        compiler_params=_params(("parallel",)),
    )(z, kvm, main)


def _memattn_bwd(z, kvm, dmix, *, qcol, name, tm=512):
    s = z.shape[0]
    m = kvm.shape[0]
    tm = min(tm, s)
    scale = HEAD ** -0.5

    def body(q_ref, kv_ref, do_ref, dq_ref, dkv_ref):
        @pl.when(pl.program_id(0) == 0)
        def _():
            dkv_ref[...] = jnp.zeros_like(dkv_ref)

        for h in range(MEM_HEADS):
            cols = slice(h * HEAD, (h + 1) * HEAD)
            vcols = slice(MEM_W + h * HEAD, MEM_W + (h + 1) * HEAD)
            q = q_ref[:, cols]
            k = kv_ref[:, cols]
            v = kv_ref[:, vcols]
            do = do_ref[:, cols]
            p = _mem_probs(q, k)
            dp = _dot(do, v, 'nt')
            ds = (p * (dp - jnp.sum(dp * p, axis=-1, keepdims=True)) * scale).astype(BF16)
            dq_ref[:, cols] = _dot(ds, k, 'nn').astype(dq_ref.dtype)
            dkv_ref[:, cols] += _dot(ds, q, 'tn')
            dkv_ref[:, vcols] += _dot(p.astype(BF16), do, 'tn')

    mo_block = G_W // MEM_W
    return pl.pallas_call(
        body, name=name, grid=(s // tm,),
        in_specs=[pl.BlockSpec((tm, MEM_W), lambda i: (i, qcol)),
                  pl.BlockSpec((m, 2 * MEM_W), lambda i: (0, 0)),
                  pl.BlockSpec((tm, MEM_W), lambda i: (i, mo_block))],
        out_specs=[pl.BlockSpec((tm, MEM_W), lambda i: (i, 0)),
                   pl.BlockSpec((m, 2 * MEM_W), lambda i: (0, 0))],
        out_shape=[jax.ShapeDtypeStruct((s, MEM_W), BF16), jax.ShapeDtypeStruct((m, 2 * MEM_W), F32)],
        compiler_params=_params(("arbitrary",)),
    )(z, kvm, dmix)


def _rope(x1, x2, cos, sin, *, name, inverse=False, out_dtype=BF16, col1=0, col2=0, tm=512):
    s, w = cos.shape
    tm = min(tm, s)
    sign = -1.0 if inverse else 1.0

    def body(a_ref, b_ref, c_ref, s_ref, o1_ref, o2_ref):
        a = a_ref[...].astype(F32)
        b = b_ref[...].astype(F32)
        c = c_ref[...]
        sn = s_ref[...] * sign
        o1_ref[...] = (a * c - b * sn).astype(o1_ref.dtype)
        o2_ref[...] = (b * c + a * sn).astype(o2_ref.dtype)

    row = pl.BlockSpec((tm, w), lambda i: (i, 0))
    return pl.pallas_call(
        body, name=name, grid=(s // tm,),
        in_specs=[pl.BlockSpec((tm, w), lambda i: (i, col1)), pl.BlockSpec((tm, w), lambda i: (i, col2)), row, row],
        out_specs=[row, row],
        out_shape=[jax.ShapeDtypeStruct((s, w), out_dtype)] * 2,
        compiler_params=_params(("parallel",)),
    )(x1, x2, cos, sin)


MHA_BLOCK = 1024


def _mha_scores(q1, q2, k1, k2, scale, diagonal):
    sc = (_dot(q1, k1, 'nt') + _dot(q2, k2, 'nt')) * scale
    if not diagonal:
        return sc, None
    rows = lax.broadcasted_iota(jnp.int32, sc.shape, 0)
    cols = lax.broadcasted_iota(jnp.int32, sc.shape, 1)
    return sc, cols <= rows


def _mha_fwd(qp, qr, kn, kr, vv, *, name):
    s = kn.shape[0]
    tb = min(MHA_BLOCK, s)
    scale = (HEAD + ROPE_DIM) ** -0.5

    def body(qn_ref, qr_ref, kn_ref, kr_ref, v_ref, o_ref, lse_ref, m_ref, l_ref, acc_ref):
        i = pl.program_id(1)
        q1, q2 = qn_ref[...], qr_ref[...]
        m_ref[...] = jnp.full_like(m_ref, MASK_VALUE)
        l_ref[...] = jnp.zeros_like(l_ref)
        acc_ref[...] = jnp.zeros_like(acc_ref)

        def block(j, diagonal):
            ks = pl.multiple_of(j * tb, tb)
            k1, k2, vj = kn_ref[pl.ds(ks, tb), :], kr_ref[pl.ds(ks, tb), :], v_ref[pl.ds(ks, tb), :]
            sc, keep = _mha_scores(q1, q2, k1, k2, scale, diagonal)
            if diagonal:
                sc = jnp.where(keep, sc, MASK_VALUE)
            m_old = m_ref[...]
            m_new = jnp.maximum(m_old, jnp.max(sc, axis=-1, keepdims=True))
            p = jnp.exp(sc - m_new)
            alpha = jnp.exp(m_old - m_new)
            l_ref[...] = alpha * l_ref[...] + jnp.sum(p, axis=-1, keepdims=True)
            acc_ref[...] = alpha * acc_ref[...] + _dot(p.astype(BF16), vj, 'nn')
            m_ref[...] = m_new

        def step(j, carry):
            block(j, False)
            return carry

        lax.fori_loop(0, i, step, 0)
        block(i, True)
        l = l_ref[...]
        o_ref[...] = (acc_ref[...] / l).astype(o_ref.dtype)
        lse_ref[...] = m_ref[...] + jnp.log(l)

    return pl.pallas_call(
        body, name=name, grid=(G_HEADS, s // tb),
        in_specs=[pl.BlockSpec((tb, HEAD), lambda h, i: (i, h)),
                  pl.BlockSpec((None, tb, ROPE_DIM), lambda h, i: (h, i, 0)),
                  pl.BlockSpec((s, HEAD), lambda h, i: (0, h)),
                  pl.BlockSpec((s, ROPE_DIM), lambda h, i: (0, 0)),
                  pl.BlockSpec((s, HEAD), lambda h, i: (0, h))],
        out_specs=[pl.BlockSpec((tb, HEAD), lambda h, i: (i, h)),
                   pl.BlockSpec((None, tb, 1), lambda h, i: (h, i, 0))],
        out_shape=[jax.ShapeDtypeStruct((s, G_W), BF16), jax.ShapeDtypeStruct((G_HEADS, s, 1), F32)],
        scratch_shapes=[pltpu.VMEM((tb, 1), F32), pltpu.VMEM((tb, 1), F32), pltpu.VMEM((tb, HEAD), F32)],
        compiler_params=_params(("parallel", "arbitrary")),
    )(qp, qr, kn, kr, vv)


def _mha_bwd(qp, qr, kn, kr, vv, o, do, lse, *, name):
    s = kn.shape[0]
    tb = min(MHA_BLOCK, s)
    nq = s // tb
    scale = (HEAD + ROPE_DIM) ** -0.5

    def body(qn_ref, qr_ref, kn_ref, kr_ref, v_ref, o_ref, do_ref, lse_ref,
             dqn_ref, dqr_ref, dkn_ref, dkr_ref, dv_ref, dq1_ref, dq2_ref):
        h, i = pl.program_id(0), pl.program_id(1)

        @pl.when(i == 0)
        def _():
            dkn_ref[...] = jnp.zeros_like(dkn_ref)
            dv_ref[...] = jnp.zeros_like(dv_ref)

        @pl.when(jnp.logical_and(h == 0, i == 0))
        def _():
            dkr_ref[...] = jnp.zeros_like(dkr_ref)

        q1, q2, dov = qn_ref[...], qr_ref[...], do_ref[...]
        delta = jnp.sum(dov.astype(F32) * o_ref[...].astype(F32), axis=-1, keepdims=True)
        lsev = lse_ref[...]
        dq1_ref[...] = jnp.zeros_like(dq1_ref)
        dq2_ref[...] = jnp.zeros_like(dq2_ref)

        def block(j, diagonal):
            ks = pl.multiple_of(j * tb, tb)
            k1, k2, vj = kn_ref[pl.ds(ks, tb), :], kr_ref[pl.ds(ks, tb), :], v_ref[pl.ds(ks, tb), :]
            sc, keep = _mha_scores(q1, q2, k1, k2, scale, diagonal)
            p = jnp.exp(sc - lsev)
            if diagonal:
                p = jnp.where(keep, p, 0.0)
            dp = _dot(dov, vj, 'nt')
            ds = (p * (dp - delta) * scale).astype(BF16)
            pb = p.astype(BF16)
            dq1_ref[...] += _dot(ds, k1, 'nn')
            dq2_ref[...] += _dot(ds, k2, 'nn')
            dkn_ref[pl.ds(ks, tb), :] += _dot(ds, q1, 'tn')
            dkr_ref[pl.ds(ks, tb), :] += _dot(ds, q2, 'tn')
            dv_ref[pl.ds(ks, tb), :] += _dot(pb, dov, 'tn')

        def step(j, carry):
            block(j, False)
            return carry

        lax.fori_loop(0, i, step, 0)
        block(i, True)
        dqn_ref[...] = dq1_ref[...].astype(dqn_ref.dtype)
        dqr_ref[...] = dq2_ref[...].astype(dqr_ref.dtype)

    tile = pl.BlockSpec((tb, HEAD), lambda h, i: (i, h))
    rope = pl.BlockSpec((None, tb, ROPE_DIM), lambda h, i: (h, i, 0))
    head_cols = pl.BlockSpec((s, HEAD), lambda h, i: (0, h))
    shared = pl.BlockSpec((s, ROPE_DIM), lambda h, i: (0, 0))
    return pl.pallas_call(
        body, name=name, grid=(G_HEADS, nq),
        in_specs=[tile, rope, head_cols, shared, head_cols, tile, tile, pl.BlockSpec((None, tb, 1), lambda h, i: (h, i, 0))],
        out_specs=[tile, rope, head_cols, shared, head_cols],
        out_shape=[jax.ShapeDtypeStruct((s, G_W), BF16), jax.ShapeDtypeStruct((G_HEADS, s, ROPE_DIM), BF16),
                   jax.ShapeDtypeStruct((s, G_W), F32), jax.ShapeDtypeStruct((s, ROPE_DIM), F32),
                   jax.ShapeDtypeStruct((s, G_W), F32)],
        scratch_shapes=[pltpu.VMEM((tb, HEAD), F32), pltpu.VMEM((tb, ROPE_DIM), F32)],
        compiler_params=_params(("arbitrary", "arbitrary")),
    )(qp, qr, kn, kr, vv, o, do, lse)


HALO = 16


def _shift_down(prev, cur, shift, first_tile):
    tr = cur.shape[0]
    full = jnp.concatenate([prev, cur], axis=0)
    out = pltpu.roll(full, shift, axis=0)[HALO:]
    row = lax.broadcasted_iota(jnp.int32, (tr, 1), 0)
    return jnp.where(jnp.logical_and(first_tile, row < shift), 0.0, out)


def _shift_up(cur, nxt, shift, last_tile):
    tr = cur.shape[0]
    full = jnp.concatenate([cur, nxt], axis=0)
    out = pltpu.roll(full, tr + HALO - shift, axis=0)[:tr]
    row = lax.broadcasted_iota(jnp.int32, (tr, 1), 0)
    return jnp.where(jnp.logical_and(last_tile, row >= tr - shift), 0.0, out)


def _lane_chunks(width, lanes):
    return [slice(c0, min(c0 + lanes, width)) for c0 in range(0, width, lanes)]


def _conv_taps(prev_ref, cur_ref, cw_ref, cb_ref, first_tile, cs):
    cur = cur_ref[:, cs].astype(F32)
    prev = prev_ref[:, cs].astype(F32)
    a1 = _shift_down(prev, cur, 1, first_tile)
    a2 = _shift_down(prev, cur, 2, first_tile)
    c = a2 * cw_ref[0:1, cs] + a1 * cw_ref[1:2, cs] + cur * cw_ref[2:3, cs] + cb_ref[:, cs]
    return c, (a2, a1, cur)


def _conv_in_specs(tr, bw, half, layer, row_of, blk_of):
    per = tr // HALO
    specs = []
    for off in (0, half):
        specs.append(pl.BlockSpec((None, HALO, bw), lambda *g, off=off: (blk_of(*g) + off, jnp.maximum(row_of(*g) * per - 1, 0), 0)))
        specs.append(pl.BlockSpec((None, tr, bw), lambda *g, off=off: (blk_of(*g) + off, row_of(*g), 0)))
    for off in (0, half):
        specs.append(pl.BlockSpec((None, None, CONV_W, bw), lambda *g, off=off: (blk_of(*g) + off, layer, 0, 0)))
    for off in (0, half):
        specs.append(pl.BlockSpec((None, 1, bw), lambda *g, off=off: (layer * 2 * half + blk_of(*g) + off, 0, 0)))
    return specs


def _conv_fwd(a, cw, cb, layer, *, name, tr=256):
    nb, s, bw = a.shape
    half = nb // 2
    tr = min(tr, s)

    def body(gp_ref, gc_ref, vp_ref, vc_ref, cwg_ref, cwv_ref, cbg_ref, cbv_ref, o_ref):
        first = pl.program_id(0) == 0
        for cs in _lane_chunks(bw, 256):
            gate, _ = _conv_taps(gp_ref, gc_ref, cwg_ref, cbg_ref, first, cs)
            val, _ = _conv_taps(vp_ref, vc_ref, cwv_ref, cbv_ref, first, cs)
            o_ref[:, cs] = (gate * _sigmoid(gate) * val).astype(o_ref.dtype)

    return pl.pallas_call(
        body, name=name, grid=(s // tr, half),
        in_specs=_conv_in_specs(tr, bw, half, layer, lambda i, j: i, lambda i, j: j),
        out_specs=pl.BlockSpec((tr, bw), lambda i, j: (i, j)),
        out_shape=jax.ShapeDtypeStruct((s, half * bw), BF16),
        compiler_params=_params(("parallel", "parallel")),
    )(a, a, a, a, cw, cw, cb, cb)


def _conv_bwd_dc(a, dact, cw, cb, layer, *, name, after=None, tr=256):
    nb, s, bw = a.shape
    half = nb // 2
    tr = min(tr, s)

    def body(*refs):
        gp_ref, gc_ref, vp_ref, vc_ref, cwg_ref, cwv_ref, cbg_ref, cbv_ref, da_ref = refs[:9]
        dc_ref, dw_ref, db_ref = refs[-3:]
        first = pl.program_id(1) == 0

        @pl.when(first)
        def _():
            dw_ref[...] = jnp.zeros_like(dw_ref)
            db_ref[...] = jnp.zeros_like(db_ref)

        for cs in _lane_chunks(bw, 128):
            gate, gtaps = _conv_taps(gp_ref, gc_ref, cwg_ref, cbg_ref, first, cs)
            val, vtaps = _conv_taps(vp_ref, vc_ref, cwv_ref, cbv_ref, first, cs)
            dact_v = da_ref[:, cs].astype(F32)
            sg = _sigmoid(gate)
            dgate = dact_v * val * (sg * (1.0 + gate * (1.0 - sg)))
            dval = dact_v * (gate * sg)
            dc_ref[0, :, cs] = dgate.astype(dc_ref.dtype)
            dc_ref[1, :, cs] = dval.astype(dc_ref.dtype)
            for kk in range(CONV_W):
                dw_ref[0, kk:kk + 1, cs] += jnp.sum(dgate * gtaps[kk], axis=0, keepdims=True)
                dw_ref[1, kk:kk + 1, cs] += jnp.sum(dval * vtaps[kk], axis=0, keepdims=True)
            db_ref[0, :, cs] += jnp.sum(dgate, axis=0, keepdims=True)
            db_ref[1, :, cs] += jnp.sum(dval, axis=0, keepdims=True)

    outs = pl.pallas_call(
        body, name=name, grid=(half, s // tr),
        in_specs=_conv_in_specs(tr, bw, half, layer, lambda j, i: i, lambda j, i: j)
        + [pl.BlockSpec((tr, bw), lambda j, i: (i, j))]
        + ([pl.BlockSpec(memory_space=pl.ANY)] if after is not None else []),
        out_specs=[pl.BlockSpec((2, None, tr, bw), lambda j, i: (0, j, i, 0)),
                   pl.BlockSpec((2, None, CONV_W, bw), lambda j, i: (0, j, 0, 0)),
                   pl.BlockSpec((2, None, 1, bw), lambda j, i: (0, j, 0, 0))],
        out_shape=[jax.ShapeDtypeStruct((2, half, s, bw), BF16),
                   jax.ShapeDtypeStruct((2, half, CONV_W, bw), F32),
                   jax.ShapeDtypeStruct((2, half, 1, bw), F32)],
        compiler_params=_params(("parallel", "arbitrary")),
    )(a, a, a, a, cw, cw, cb, cb, dact, *([after] if after is not None else []))
    dc, dw, db = outs
    return dc.reshape(nb, s, bw), dw.reshape(nb, CONV_W, bw), db.reshape(nb, 1, bw)


def _conv_bwd_da(dc, cw, layer, *, name, tr=512):
    nb, s, bw = dc.shape
    tr = min(tr, s)
    ni = s // tr
    per = tr // HALO
    last_halo = s // HALO - 1

    def body(c_ref, n_ref, w_ref, o_ref):
        last = pl.program_id(0) == ni - 1
        for cs in _lane_chunks(bw, 256):
            cur = c_ref[:, cs].astype(F32)
            nxt = n_ref[:, cs].astype(F32)
            da = (cur * w_ref[2:3, cs] + _shift_up(cur, nxt, 1, last) * w_ref[1:2, cs]
                  + _shift_up(cur, nxt, 2, last) * w_ref[0:1, cs])
            o_ref[:, cs] = da.astype(o_ref.dtype)

    tile = pl.BlockSpec((None, tr, bw), lambda i, j: (j, i, 0))
    return pl.pallas_call(
        body, name=name, grid=(ni, nb),
        in_specs=[tile,
                  pl.BlockSpec((None, HALO, bw), lambda i, j: (j, jnp.minimum((i + 1) * per, last_halo), 0)),
                  pl.BlockSpec((None, None, CONV_W, bw), lambda i, j: (j, layer, 0, 0))],
        out_specs=tile,
        out_shape=jax.ShapeDtypeStruct((nb, s, bw), BF16),
        compiler_params=_params(("parallel", "parallel")),
    )(dc, dc, cw)


def _rope_tables(positions):
    inv = 1.0 / (ROPE_THETA ** (jnp.arange(0, ROPE_DIM, 2, dtype=F32) / ROPE_DIM))
    ang = positions.astype(F32)[:, None] * inv
    return jnp.cos(ang), jnp.sin(ang)


def _heads_to_major(r1, r2):
    s = r1.shape[0]
    both = jnp.concatenate([r1.reshape(s, G_HEADS, ROPE_HALF), r2.reshape(s, G_HEADS, ROPE_HALF)], axis=-1)
    return both.transpose(1, 0, 2)


def _heads_from_major(qr):
    s = qr.shape[1]
    t = qr.transpose(1, 0, 2)
    return t[:, :, :ROPE_HALF].reshape(s, G_HEADS * ROPE_HALF), t[:, :, ROPE_HALF:].reshape(s, G_HEADS * ROPE_HALF)


def _local_step(x, mem, positions, target, rep, fetch, emit):
    s, d = x.shape
    n_b = DEPTH - N_A
    tm = min(1024, s)
    cos, sin = _rope_tables(positions)
    cos12 = jnp.tile(cos, (1, G_HEADS))
    sin12 = jnp.tile(sin, (1, G_HEADS))
    r1_col = G_W // (G_HEADS * ROPE_HALF)
    b_sp_t = rep['b_sp'].transpose(0, 2, 1)

    saved = []
    kv = None
    shared = None
    for l in range(DEPTH):
        wm = fetch(('in', l), x)
        if l == 0:
            shared = {'g_v': wm['g_v'], 'conv_w': wm['conv_w']}
            bw = shared['conv_w'].shape[-1]
            conv_b = rep['conv_b'].reshape(-1, 1, bw)
        sv = {'x_in': x, 'wm': wm}
        if l == N_A:
            xn_kv = _rmsnorm(x, rep['g_kv'], name="kvnorm")
            kvx = _mm(xn_kv, wm['w_kv_a'], dims='nn', out_dtype=F32, name="kvproj", tm=tm, tn=KV_PAD)
            ckv = _rmsnorm(kvx, rep['g_kv_lat'], width=KV_RANK, name="ckvnorm")
            k1, k2 = _rope(kvx[:, KV_RANK:KV_RANK + ROPE_HALF], kvx[:, KV_RANK + ROPE_HALF:KV_RANK + ROPE_DIM],
                           cos, sin, name="krope")
            kr = jnp.concatenate([k1, k2], axis=-1)
            kv = {'x': x, 'xn': xn_kv, 'kvx': kvx, 'ckv': ckv, 'kr': kr, 'w_kv_a': wm['w_kv_a']}
        h = _rmsnorm(x, rep['g_mix'][l], name=f"mixnorm{l}")
        if l < N_A:
            z = _mm(h, wm['w_in'], dims='nn', out_dtype=BF16, name=f"in_a{l}", tm=tm, tn=512)
            main = _sgu_fwd(z, shared['g_v'][l], rep['w_sp'][l], b_sp_t[l], name=f"sgu{l}")
            qcol = 2 * G_W // MEM_W
        else:
            j = l - N_A
            z = _mm(h, wm['w_in'], dims='nn', out_dtype=BF16, name=f"in_b{j}", tm=tm, tn=1024)
            qn = _rmsnorm(z, rep['g_q_lat'][j], width=Q_RANK, name=f"qnorm{j}")
            qp = _mm(qn, wm['w_uqp'], dims='nn', out_dtype=BF16, name=f"uq{j}", tm=tm, tn=768)
            rr1, rr2 = _rope(qp, qp, cos12, sin12, col1=r1_col, col2=r1_col + 1, name=f"qrope{j}")
            qr = _heads_to_major(rr1, rr2)
            kn = _mm(kv['ckv'], wm['w_uk'], dims='nn', out_dtype=BF16, name=f"k_up{j}", tm=tm, tn=768)
            vv = _mm(kv['ckv'], wm['w_uv'], dims='nn', out_dtype=BF16, name=f"v_up{j}", tm=tm, tn=768)
            main, lse = _mha_fwd(qp, qr, kn, kv['kr'], vv, name=f"mha{j}")
            qcol = Q_RANK // MEM_W
            sv.update(qn=qn, qp=qp, qr=qr, kn=kn, vv=vv, lse=lse)
        wm.update(fetch(('rest', l), z))
        memn = _rmsnorm(mem, rep['g_mem'][l], name=f"memnorm{l}")
        kvm = _mm(memn, wm['w_mem_kv'], dims='nn', out_dtype=BF16, name=f"memkv{l}", tm=tm, tn=1024)
        mix = _memattn_fwd(z, kvm, main, qcol=qcol, name=f"memattn{l}")
        x_mid = _mm(mix, wm['w_out'], dims='nn', res=x, out_dtype=F32, name=f"out{l}", tm=tm, tn=1024)
        wf = fetch(('up', l), x_mid)
        h2 = _rmsnorm(x_mid, rep['g_ffn'][l], name=f"ffnnorm{l}")
        a = _mm(h2, wf['w_up'], dims='nn', b_blocked=True, out_dtype=BF16, out_block=bw,
                name=f"up{l}", tm=tm, tn=bw)
        act = _conv_fwd(a, shared['conv_w'], conv_b, l, name=f"conv{l}")
        wf.update(fetch(('down', l), act))
        x = _mm(act, wf['w_down'], dims='nn', res=x_mid, out_dtype=F32, name=f"down{l}", tm=512, tn=1024)
        sv.update(h=h, memn=memn, kvm=kvm, z=z, qcol=qcol, mix=mix, x_mid=x_mid, h2=h2, a=a, act=act, wf=wf)
        saved.append(sv)

    sq, dx, dg_final = _final_loss(x, target, rep['g_final'], name="loss")

    g = {k: [None] * DEPTH for k in ('g_mix', 'g_ffn', 'g_mem', 'conv_w', 'conv_b')}
    for k in ('g_v', 'w_sp', 'b_sp'):
        g[k] = [None] * N_A
    g['g_q_lat'] = [None] * n_b
    g['g_final'] = dg_final
    dckv_sum, dkr_sum = None, None

    for l in reversed(range(DEPTH)):
        sv = saved[l]
        wm, wf = sv['wm'], sv['wf']
        dact = _mm(dx, wf['w_down'], dims='nt', out_dtype=BF16, name=f"d_act{l}", tm=tm, tn=bw)
        dw_down = _mm(sv['act'], dx, dims='tn', out_dtype=BF16, name=f"dw_down{l}", tm=bw, tn=512)
        tok = emit(('down', l), {'w_ffn_down': dw_down})
        dc, dcw, dcb = _conv_bwd_dc(sv['a'], dact, shared['conv_w'], conv_b, l, after=tok, name=f"d_conv{l}")
        g['conv_w'][l], g['conv_b'][l] = dcw, dcb
        da = _conv_bwd_da(dc, shared['conv_w'], l, name=f"d_convin{l}")
        dw_up = _mm(sv['h2'], da, dims='tn', b_blocked=True, out_dtype=BF16, out_block=bw,
                    name=f"dw_up{l}", tm=512, tn=bw, n_outer=True)
        tok = emit(('up', l), {'w_ffn_up': dw_up})
        dh2 = _mm_blocked_nt(da, wf['w_up'], out_dtype=BF16, name=f"d_h2{l}", tm=512, tn=1024, blocks_per_step=4,
                             after=tok)
        dx_mid, g['g_ffn'][l] = _rmsnorm_bwd(sv['x_mid'], rep['g_ffn'][l], dh2, dres=dx, name=f"d_ffnnorm{l}")
        dmix = _mm(dx_mid, wm['w_out'], dims='nt', out_dtype=BF16, name=f"d_mix{l}", tm=tm, tn=1024)
        dw_out = _mm(sv['mix'], dx_mid, dims='tn', out_dtype=BF16, name=f"dw_out{l}", tm=1024, tn=256)
        dqm, dkvm = _memattn_bwd(sv['z'], sv['kvm'], dmix, qcol=sv['qcol'], name=f"d_memattn{l}")
        dw_memkv = _mm(sv['memn'], dkvm, dims='tn', out_dtype=BF16, name=f"dw_memkv{l}", tm=1024, tn=1024)
        tok = emit(('rest', l), {'w_out': dw_out, 'w_mem_kv': dw_memkv})
        gm = {}
        dmemn = _mm(dkvm, wm['w_mem_kv'], dims='nt', out_dtype=F32, name=f"d_memn{l}", tm=tm, tn=1024, after=tok)
        _, g['g_mem'][l] = _rmsnorm_bwd(mem, rep['g_mem'][l], dmemn, out_dtype=BF16, name=f"d_memnorm{l}")
        if l < N_A:
            dz, dwsp, dbsp_t, dgv = _sgu_bwd(sv['z'], dmix, dqm, shared['g_v'][l], rep['w_sp'][l], b_sp_t[l],
                                             name=f"d_sgu{l}")
            g['w_sp'][l], g['b_sp'][l], g['g_v'][l] = dwsp, dbsp_t.T, dgv
            dh = _mm(dz, wm['w_in'], dims='nt', out_dtype=BF16, name=f"d_h_a{l}", tm=tm, tn=1024)
            gm['w_in_a'] = _mm(sv['h'], dz, dims='tn', out_dtype=BF16, name=f"dw_in_a{l}", tm=1024, tn=512)
        else:
            j = l - N_A
            dq_nope, dqr, dkn, dkr, dvv = _mha_bwd(sv['qp'], sv['qr'], sv['kn'], kv['kr'], sv['vv'], sv['mix'], dmix,
                                                   sv['lse'], name=f"d_mha{j}")
            gm['w_uk'] = _mm(kv['ckv'], dkn, dims='tn', out_dtype=BF16, name=f"dw_uk{j}", tm=512, tn=768)
            gm['w_uv'] = _mm(kv['ckv'], dvv, dims='tn', out_dtype=BF16, name=f"dw_uv{j}", tm=512, tn=768)
            dckv = _mm(dkn, wm['w_uk'], dims='nt', out_dtype=F32, res=dckv_sum, name=f"d_ckv_k{j}", tm=tm, tn=512)
            dckv_sum = _mm(dvv, wm['w_uv'], dims='nt', out_dtype=F32, res=dckv, name=f"d_ckv_v{j}", tm=tm, tn=512)
            dkr_sum = dkr if dkr_sum is None else dkr_sum + dkr
            dr1, dr2 = _heads_from_major(dqr)
            dq1, dq2 = _rope(dr1, dr2, cos12, sin12, inverse=True, name=f"d_qrope{j}")
            dqp = jnp.concatenate([dq_nope, dq1, dq2], axis=-1)
            dqn = _mm(dqp, wm['w_uqp'], dims='nt', out_dtype=BF16, name=f"d_qn{j}", tm=tm, tn=512)
            gm['w_uqp'] = _mm(sv['qn'], dqp, dims='tn', out_dtype=BF16, name=f"dw_uq{j}", tm=512, tn=768)
            dqlat, g['g_q_lat'][j] = _rmsnorm_bwd(sv['z'], rep['g_q_lat'][j], dqn, width=Q_RANK, out_dtype=BF16,
                                                 name=f"d_qnorm{j}")
            dz = jnp.concatenate([dqlat, dqm], axis=-1)
            dh = _mm(dz, wm['w_in'], dims='nt', out_dtype=BF16, name=f"d_h_b{j}", tm=tm, tn=1024)
            gm['w_in_b'] = _mm(sv['h'], dz, dims='tn', out_dtype=BF16, name=f"dw_in_b{j}", tm=1024, tn=512)
        tok = emit(('mix', l), gm)
        dx, g['g_mix'][l] = _rmsnorm_bwd(sv['x_in'], rep['g_mix'][l], dh, dres=dx_mid, after=tok, name=f"d_mixnorm{l}")
        if l == N_A:
            dkvx_c, g['g_kv_lat'] = _rmsnorm_bwd(kv['kvx'], rep['g_kv_lat'], dckv_sum, width=KV_RANK, out_dtype=BF16,
                                                 name="d_ckvnorm")
            dk1, dk2 = _rope(dkr_sum[:, :ROPE_HALF], dkr_sum[:, ROPE_HALF:], cos, sin, inverse=True, name="d_krope")
            dkvx = jnp.concatenate([dkvx_c, dk1, dk2, jnp.zeros((s, KV_PAD - KV_RANK - ROPE_DIM), BF16)], axis=-1)
            dxn = _mm(dkvx, kv['w_kv_a'], dims='nt', out_dtype=BF16, name="d_kvnorm_in", tm=tm, tn=1024)
            dw_kv = _mm(kv['xn'], dkvx, dims='tn', out_dtype=BF16, name="dw_kv", tm=1024, tn=KV_PAD)
            tok = emit(('kv', 0), {'w_kv_a': dw_kv})
            dx, g['g_kv'] = _rmsnorm_bwd(kv['x'], rep['g_kv'], dxn, dres=dx, after=tok, name="d_kvnorm")
    return jnp.sum(sq), dx, g


MESH_IDS = pl.DeviceIdType.MESH
PEER_MASKS = tuple((k >> 2 & 1, k >> 1 & 1, k & 1) for k in range(1, N_DEV))
CHIP_MASKS = ((1, 0), (0, 1), (1, 1))
N_PEER = N_DEV - 1
SEMS_PER_BUFFER = 2 * N_PEER + 1
DATAFLOW = pltpu.SideEffectType.DATAFLOW_SIDE_EFFECTING
HBM_SPEC = pl.BlockSpec(memory_space=pltpu.HBM)
SEM_SPEC = pl.BlockSpec(memory_space=pltpu.SEMAPHORE)


def _my_position():
    return lax.axis_index("x"), lax.axis_index("y"), lax.axis_index("c")


def _flip(pos, mask):
    return tuple(1 - p if f else p for p, f in zip(pos, mask))


def _linear_id(pos):
    return 4 * pos[0] + 2 * pos[1] + pos[2]


def _hbm(x):
    return pltpu.with_memory_space_constraint(x, pltpu.HBM)


def _buffer_copies(src_ref, lead, land_ref, sems, scatter, near=False):
    me = _my_position()
    my_id = _linear_id(me)
    src = src_ref.at[lead] if lead else src_ref
    own = pltpu.make_async_copy(src.at[my_id] if scatter else src, land_ref.at[my_id], sems.at[2 * N_PEER])
    pairs = []
    for k, mask in enumerate(PEER_MASKS):
        if near and mask[2] == 1 and mask != (0, 0, 1):
            continue
        peer = _flip(me, mask)
        peer_id = _linear_id(peer)
        block = src.at[peer_id] if scatter else src
        send = pltpu.make_async_remote_copy(src_ref=block, dst_ref=land_ref.at[my_id], send_sem=sems.at[k],
                                            recv_sem=sems.at[N_PEER + k], device_id=peer, device_id_type=MESH_IDS)
        arrival = pltpu.make_async_remote_copy(src_ref=block, dst_ref=land_ref.at[peer_id], send_sem=sems.at[k],
                                               recv_sem=sems.at[N_PEER + k], device_id=peer, device_id_type=MESH_IDS)
        pairs.append((send, arrival))
    return own, pairs


def _exchange_start(srcs, buffers, *, name, scatter):
    ns, nb = len(srcs), len(buffers)
    lands = [_hbm(lax.empty((N_DEV,) + tuple(shape), dtype)) for _, _, shape, dtype, _ in buffers]

    def body(*refs):
        src_refs, land_refs = refs[:ns], refs[ns:ns + nb]
        sem_refs = refs[ns + nb:ns + 2 * nb]
        token = refs[-1]
        for b, (si, lead, _, _, near) in enumerate(buffers):
            own, pairs = _buffer_copies(src_refs[si], lead, land_refs[b], sem_refs[b], scatter, near)
            own.start()
            for send, _ in pairs:
                send.start()
        token[...] = jnp.zeros_like(token)

    out_shape = ([pltpu.SemaphoreType.DMA((SEMS_PER_BUFFER,))] * nb
                 + [pltpu.HBM(a.shape, a.dtype) for a in srcs]
                 + [pltpu.HBM(a.shape, a.dtype) for a in lands]
                 + [jax.ShapeDtypeStruct((8, 128), F32)])
    aliases = {i: nb + i for i in range(ns + nb)}
    outs = pl.pallas_call(
        body, name=name, in_specs=[HBM_SPEC] * (ns + nb),
        out_specs=[SEM_SPEC] * nb + [HBM_SPEC] * (ns + nb) + [pl.BlockSpec(memory_space=pltpu.VMEM)],
        out_shape=out_shape, input_output_aliases=aliases,
        compiler_params=pltpu.CompilerParams(has_side_effects=DATAFLOW),
    )(*[_hbm(a) for a in srcs], *lands)
    sems = list(outs[:nb])
    src_thru = list(outs[nb:nb + ns])
    land_thru = list(outs[nb + ns:nb + ns + nb])
    return sems, land_thru, src_thru, outs[-1]


def _exchange_wait(srcs_thru, buffers, sems, lands, after, *, name, scatter):
    ns, nb = len(srcs_thru), len(buffers)
    has_after = after is not None

    def body(*refs):
        src_refs, land_refs = refs[:ns], refs[ns:ns + nb]
        sem_refs = refs[ns + nb:ns + 2 * nb]
        for b, (si, lead, _, _, near) in enumerate(buffers):
            own, pairs = _buffer_copies(src_refs[si], lead, land_refs[b], sem_refs[b], scatter, near)
            for send, arrival in pairs:
                send.wait_send()
                arrival.wait_recv()
            own.wait()

    operands = list(srcs_thru) + list(lands) + list(sems) + ([after] if has_after else [])
    in_specs = ([HBM_SPEC] * (ns + nb) + [SEM_SPEC] * nb + ([pl.BlockSpec(memory_space=pl.ANY)] if has_after else []))
    outs = pl.pallas_call(
        body, name=name, in_specs=in_specs, out_specs=[HBM_SPEC] * nb,
        out_shape=[pltpu.HBM(a.shape, a.dtype) for a in lands],
        input_output_aliases={ns + b: b for b in range(nb)},
        compiler_params=pltpu.CompilerParams(has_side_effects=DATAFLOW),
    )(*operands)
    return list(outs)


def _exchange(arrays, *, name, scatter, near=None, after=None):
    n = len(arrays)
    near = [False] * n if near is None else near
    extra = [] if after is None else [after]
    out_shapes = [jax.ShapeDtypeStruct(a.shape if scatter else (N_DEV,) + a.shape, a.dtype) for a in arrays]

    def body(*refs):
        srcs, outs, sems = refs[:n], refs[n + len(extra):2 * n + len(extra)], refs[2 * n + len(extra):]
        started = []
        for a in range(n):
            own, pairs = _buffer_copies(srcs[a], (), outs[a], sems[a], scatter, near[a])
            own.start()
            for send, _ in pairs:
                send.start()
            started.append((own, pairs))
        for own, pairs in started:
            for send, arrival in pairs:
                arrival.wait_recv()
                send.wait_send()
            own.wait()

    any_spec = pl.BlockSpec(memory_space=pl.ANY)
    outs = pl.pallas_call(
        body, name=name, in_specs=[any_spec] * (n + len(extra)), out_specs=[any_spec] * n, out_shape=out_shapes,
        scratch_shapes=[pltpu.SemaphoreType.DMA((SEMS_PER_BUFFER,))] * n,
    )(*arrays, *extra)
    return list(outs)


def _forward_to_sibling(lands, *, name):
    n = len(lands)

    def body(*refs):
        ins, outs, sems = refs[:n], refs[n:2 * n], refs[2 * n:]
        me = _my_position()
        sibling = _flip(me, (0, 0, 1))
        pairs = []
        for b in range(n):
            for k, (fx, fy) in enumerate(CHIP_MASKS):
                mine = _linear_id(_flip(me, (fx, fy, 0)))
                theirs = _linear_id(_flip(me, (fx, fy, 1)))
                send = pltpu.make_async_remote_copy(
                    src_ref=ins[b].at[mine], dst_ref=outs[b].at[mine], send_sem=sems[b].at[k],
                    recv_sem=sems[b].at[len(CHIP_MASKS) + k], device_id=sibling, device_id_type=MESH_IDS)
                arrival = pltpu.make_async_remote_copy(
                    src_ref=ins[b].at[mine], dst_ref=outs[b].at[theirs], send_sem=sems[b].at[k],
                    recv_sem=sems[b].at[len(CHIP_MASKS) + k], device_id=sibling, device_id_type=MESH_IDS)
                send.start()
                pairs.append((send, arrival))
        for send, arrival in pairs:
            arrival.wait_recv()
            send.wait_send()

    any_spec = pl.BlockSpec(memory_space=pl.ANY)
    outs = pl.pallas_call(
        body, name=name, in_specs=[any_spec] * n, out_specs=[any_spec] * n,
        out_shape=[jax.ShapeDtypeStruct(a.shape, a.dtype) for a in lands],
        input_output_aliases={b: b for b in range(n)},
        scratch_shapes=[pltpu.SemaphoreType.DMA((2 * len(CHIP_MASKS),))] * n,
    )(*lands)
    return list(outs)


def _sum_slots(parts_ref):
    total = parts_ref[0].astype(F32)
    for q in range(1, parts_ref.shape[0]):
        total = total + parts_ref[q].astype(F32)
    return total


def _row_tile(rows, cols, n_arrays):
    budget = (12 * 1024 * 1024) // (4 * n_arrays * max(cols, 128))
    t = rows
    while t > budget and t % 2 == 0 and (t // 2) % 16 == 0:
        t //= 2
    return t


def _sum_adam(parts, w, m, v, layer, outs, *, name):
    q, r, c = parts.shape
    nl = w.shape[0]
    tr = _row_tile(r, c, q + 7)
    c1 = 1.0 - ADAM_B1 ** ADAM_STEP
    c2 = 1.0 - ADAM_B2 ** ADAM_STEP
    if outs is None:
        outs = [lax.empty((nl, r, c), F32) for _ in range(4)]

    def body(p_ref, w_ref, m_ref, v_ref, g_in, d_in, mo_in, vo_in, g_ref, d_ref, mo_ref, vo_ref):
        grad = _sum_slots(p_ref)
        m_new = ADAM_B1 * m_ref[...] + (1.0 - ADAM_B1) * grad
        v_new = ADAM_B2 * v_ref[...] + (1.0 - ADAM_B2) * (grad * grad)
        m_hat = m_new / c1
        v_hat = v_new / c2
        g_ref[...] = grad
        d_ref[...] = -ADAM_LR * (m_hat / (jnp.sqrt(v_hat) + ADAM_EPS) + ADAM_WD * w_ref[...])
        mo_ref[...] = m_new
        vo_ref[...] = v_new

    tile = pl.BlockSpec((None, tr, c), lambda i: (layer, i, 0))
    any_spec = pl.BlockSpec(memory_space=pl.ANY)
    return pl.pallas_call(
        body, name=name, grid=(r // tr,),
        in_specs=[pl.BlockSpec((q, tr, c), lambda i: (0, i, 0)), tile, tile, tile] + [any_spec] * 4,
        out_specs=[tile] * 4, out_shape=[jax.ShapeDtypeStruct((nl, r, c), F32)] * 4,
        input_output_aliases={4: 0, 5: 1, 6: 2, 7: 3},
        compiler_params=_params(("parallel",)),
    )(parts, w, m, v, *outs)


def _sum_parts(parts, *, name):
    q, r, c = parts.shape

    def body(p_ref, o_ref):
        o_ref[...] = _sum_slots(p_ref)

    return pl.pallas_call(
        body, name=name, in_specs=[pl.BlockSpec((q, r, c), lambda: (0, 0, 0))],
        out_specs=pl.BlockSpec((r, c), lambda: (0, 0)), out_shape=jax.ShapeDtypeStruct((r, c), F32),
        compiler_params=_params(),
    )(parts)


INPUT_NAMES = (['x', 'mem', 'positions'] + WEIGHTS + ['loss_target'] + ['m_' + n for n in WEIGHTS]
               + ['v_' + n for n in WEIGHTS])
SMALL_ALIGN = N_DEV * 8 * 128
TWO_LEVEL_LAYERS = N_A
GROUP_ORDER = ('in', 'rest', 'up', 'down')
GROUP_WEIGHTS = {'in': (['w_in_a'], ['w_in_b', 'w_uq', 'w_uk', 'w_uv']), 'rest': (['w_mem_kv', 'w_out'],) * 2,
                 'up': (['w_ffn_up'],) * 2, 'down': (['w_ffn_down'],) * 2}
LAYERED = {'w_in_a': 0, 'w_in_b': N_A, 'w_uq': N_A, 'w_uk': N_A, 'w_uv': N_A, 'w_mem_kv': 0, 'w_out': 0,
           'w_ffn_up': 0, 'w_ffn_down': 0}


def _permute_uq(w_uq):
    r = w_uq.shape[0]
    q = w_uq.reshape(r, G_HEADS, HEAD + ROPE_DIM)
    return jnp.concatenate([q[..., :HEAD].reshape(r, -1), q[..., HEAD:HEAD + ROPE_HALF].reshape(r, -1),
                            q[..., HEAD + ROPE_HALF:].reshape(r, -1)], axis=-1)


def _unpermute_uq(w_uqp):
    r = w_uqp.shape[0]
    nope = w_uqp[..., :G_W].reshape(r, G_HEADS, HEAD)
    r1 = w_uqp[..., G_W:G_W + G_HEADS * ROPE_HALF].reshape(r, G_HEADS, ROPE_HALF)
    r2 = w_uqp[..., G_W + G_HEADS * ROPE_HALF:].reshape(r, G_HEADS, ROPE_HALF)
    return jnp.concatenate([nope, r1, r2], axis=-1).reshape(r, -1)


def _cols_from_stack(st):
    _, r, n = st.shape
    return st.transpose(1, 0, 2).reshape(r, N_DEV * n)


def _cols_to_stack(wh):
    r, c = wh.shape
    return wh.reshape(r, N_DEV, c // N_DEV).transpose(1, 0, 2)


def _group_weights(group):
    kind, l = group
    return GROUP_WEIGHTS[kind][0 if l < N_A else 1]


def _step(args):
    p = dict(zip(INPUT_NAMES, args))
    x, mem, positions, target = p['x'][0], p['mem'][0], p['positions'][0], p['loss_target'][0]
    d = x.shape[-1]
    my_id = _linear_id(_my_position())

    w_kv_pad = jnp.pad(p['w_kv_a'], ((0, 0), (0, KV_PAD - p['w_kv_a'].shape[1])))
    shard = {k: p[k].astype(BF16) for k in LAYERED}
    shard['w_uk'] = shard['w_uk'].reshape(shard['w_uk'].shape[0], shard['w_uk'].shape[1], -1)
    shard['w_uv'] = shard['w_uv'].reshape(shard['w_uv'].shape[0], shard['w_uv'].shape[1], -1)
    shard.update(conv_w=p['conv_w'], g_v=p['g_v'], w_kv_a=w_kv_pad.astype(BF16))
    src_names = list(shard)
    gather_groups = []
    for l in range(DEPTH):
        gather_groups += [(kind, l) for kind in GROUP_ORDER]
    buffers, owner = [], []
    for group in gather_groups:
        kind, l = group
        for k in _group_weights(group):
            buffers.append((src_names.index(k), (l - LAYERED[k],), shard[k].shape[1:], shard[k].dtype, l < TWO_LEVEL_LAYERS))
            owner.append((group, k))
        if group == ('in', 0):
            for k in ('g_v', 'conv_w'):
                buffers.append((src_names.index(k), (), shard[k].shape, shard[k].dtype, True))
                owner.append((group, k))
        if group == ('in', N_A):
            buffers.append((src_names.index('w_kv_a'), (), shard['w_kv_a'].shape, BF16, False))
            owner.append((group, 'w_kv_a'))
    g_sems, g_lands, g_srcs, _ = _exchange_start([shard[k] for k in src_names], buffers, name="gather_start",
                                                 scatter=False)

    def fetch(group, after):
        idx = [i for i, (grp, _) in enumerate(owner) if grp == group]
        landed = _exchange_wait(g_srcs, [buffers[i] for i in idx], [g_sems[i] for i in idx],
                                [g_lands[i] for i in idx], after, name=f"gather_wait_{group[0]}{group[1]}",
                                scatter=False)
        if group[1] < TWO_LEVEL_LAYERS:
            landed = _forward_to_sibling(landed, name=f"gather_forward_{group[0]}{group[1]}")
        got = {owner[i][1]: t for i, t in zip(idx, landed)}
        out = {}
        for k, t in got.items():
            if k in ('w_in_a', 'w_uq'):
                out[k] = _cols_from_stack(t)
            elif k == 'g_v':
                out[k] = t.transpose(1, 0, 2).reshape(t.shape[1], -1)
            elif k in ('w_ffn_up', 'conv_w'):
                out[k] = t
            else:
                out[k] = t.reshape(-1, t.shape[-1])
        if 'w_uq' in out:
            out['w_uqp'] = _permute_uq(out.pop('w_uq'))
        for old, new in (('w_in_a', 'w_in'), ('w_in_b', 'w_in'), ('w_ffn_up', 'w_up'), ('w_ffn_down', 'w_down')):
            if old in out:
                out[new] = out.pop(old)
        return out

    pending = []

    def emit(group, grads):
        send = {}
        for k, t in grads.items():
            if k == 'w_in_a':
                send[k] = _cols_to_stack(t)
            elif k == 'w_uqp':
                send['w_uq'] = _cols_to_stack(_unpermute_uq(t))
            elif k == 'w_ffn_up':
                send[k] = t
            elif k == 'w_kv_a':
                cols = p['w_kv_a'].shape[1]
                send[k] = t[:, :cols].reshape(N_DEV, -1, cols)
            else:
                send[k] = t.reshape(N_DEV, t.shape[0] // N_DEV, t.shape[1])
        keys = list(send)
        bufs = [(i, (), send[k].shape[1:], send[k].dtype, False) for i, k in enumerate(keys)]
        sems, lands, srcs, token = _exchange_start([send[k] for k in keys], bufs,
                                                   name=f"scatter_start_{group[0]}{group[1]}", scatter=True)
        pending.append((group, keys, bufs, sems, lands, srcs))
        return token

    rep = {k: p[k] for k in REPLICATED}
    sq, grad_x, g = _local_step(x, mem, positions, target, rep, fetch, emit)
    loss = (0.5 / d) * lax.psum(sq, ("x", "y", "c"))

    out, running = {}, {}
    order = grad_x
    for group, keys, bufs, sems, lands, srcs in pending:
        landed = _exchange_wait(srcs, bufs, sems, lands, order, name=f"scatter_wait_{group[0]}{group[1]}", scatter=True)
        for k, parts in zip(keys, landed):
            stacked = k in LAYERED
            nl = p[k].shape[0] if stacked else 1
            layer = group[1] - LAYERED[k] if stacked else 0
            rows = p[k].size // nl // p[k].shape[-1]
            view = (nl, rows, p[k].shape[-1])
            running[k] = _sum_adam(parts.reshape(N_DEV, rows, view[2]), p[k].reshape(view), p['m_' + k].reshape(view),
                                   p['v_' + k].reshape(view), layer, running.get(k), name=f"adam_{k}{layer}")
            order = running[k][1]
    for k, res in running.items():
        out[k] = [t.reshape(p[k].shape) for t in res]

    small = {
        'g_mix': jnp.concatenate(g['g_mix']), 'g_ffn': jnp.concatenate(g['g_ffn']), 'g_final': g['g_final'],
        'w_sp': jnp.stack(g['w_sp']), 'b_sp': jnp.stack(g['b_sp']), 'g_kv': g['g_kv'], 'g_kv_lat': g['g_kv_lat'],
        'g_q_lat': jnp.concatenate(g['g_q_lat']), 'g_mem': jnp.concatenate(g['g_mem']),
        'conv_b': jnp.stack(g['conv_b']),
        'g_v': jnp.concatenate(g['g_v']),
        'conv_w': jnp.stack(g['conv_w']).transpose(0, 2, 1, 3),
    }
    small_names = REPLICATED + SMALL_SHARDED
    flat = jnp.concatenate([small[k].reshape(-1).astype(F32) for k in small_names])
    n_small = flat.shape[0]
    padded = -(-n_small // SMALL_ALIGN) * SMALL_ALIGN
    flat = jnp.pad(flat, (0, padded - n_small)).reshape(N_DEV, -1, 128)
    last_update = out[pending[-1][1][-1]][1]
    (small_parts,) = _exchange([flat], name="scatter_small", scatter=True, after=last_update)
    reduced = _sum_parts(small_parts, name="sum_small")
    (small_all,) = _exchange([reduced], name="gather_small", scatter=False)
    small_all = small_all.reshape(-1)
    grads_small, off = {}, 0
    for k in small_names:
        size = small[k].size
        grads_small[k] = small_all[off:off + size].reshape(small[k].shape)
        off += size
    grads_small['g_v'] = lax.dynamic_slice_in_dim(grads_small['g_v'], my_id * p['g_v'].shape[1], p['g_v'].shape[1], axis=1)
    grads_small['conv_w'] = lax.dynamic_index_in_dim(grads_small['conv_w'], my_id, axis=2, keepdims=False)
    gs = jnp.concatenate([grads_small[k].reshape(-1) for k in small_names])
    n_loc = gs.shape[0]
    pad_loc = -(-n_loc // 1024) * 1024 - n_loc

    def pack(prefix):
        t = jnp.concatenate([p[prefix + k].reshape(-1) for k in small_names])
        return jnp.pad(t, (0, pad_loc)).reshape(1, -1, 128)

    res = _sum_adam(jnp.pad(gs, (0, pad_loc)).reshape(1, -1, 128), pack(''), pack('m_'), pack('v_'), 0, None,
                    name="adam_small")
    off = 0
    for k in small_names:
        size = p[k].size
        out[k] = [t.reshape(-1)[off:off + size].reshape(p[k].shape) for t in res]
        off += size

    outs = [loss, grad_x[None]]
    for i in range(4):
        outs += [out[k][i] for k in WEIGHTS]
    return tuple(outs)


def kernel(x, mem, positions, g_mix, g_ffn, g_final, w_in_a, g_v, w_sp, b_sp, g_kv, w_kv_a, g_kv_lat, w_in_b, g_q_lat, w_uq, w_uk, w_uv, g_mem, w_mem_kv, w_out, w_ffn_up, conv_w, conv_b, w_ffn_down, loss_target, m_g_mix, m_g_ffn, m_g_final, m_w_in_a, m_g_v, m_w_sp, m_b_sp, m_g_kv, m_w_kv_a, m_g_kv_lat, m_w_in_b, m_g_q_lat, m_w_uq, m_w_uk, m_w_uv, m_g_mem, m_w_mem_kv, m_w_out, m_w_ffn_up, m_conv_w, m_conv_b, m_w_ffn_down, v_g_mix, v_g_ffn, v_g_final, v_w_in_a, v_g_v, v_w_sp, v_b_sp, v_g_kv, v_w_kv_a, v_g_kv_lat, v_w_in_b, v_g_q_lat, v_w_uq, v_w_uk, v_w_uv, v_g_mem, v_w_mem_kv, v_w_out, v_w_ffn_up, v_conv_w, v_conv_b, v_w_ffn_down):
    return _step((x, mem, positions, g_mix, g_ffn, g_final, w_in_a, g_v, w_sp, b_sp, g_kv, w_kv_a, g_kv_lat, w_in_b, g_q_lat, w_uq, w_uk, w_uv, g_mem, w_mem_kv, w_out, w_ffn_up, conv_w, conv_b, w_ffn_down, loss_target, m_g_mix, m_g_ffn, m_g_final, m_w_in_a, m_g_v, m_w_sp, m_b_sp, m_g_kv, m_w_kv_a, m_g_kv_lat, m_w_in_b, m_g_q_lat, m_w_uq, m_w_uk, m_w_uv, m_g_mem, m_w_mem_kv, m_w_out, m_w_ffn_up, m_conv_w, m_conv_b, m_w_ffn_down, v_g_mix, v_g_ffn, v_g_final, v_w_in_a, v_g_v, v_w_sp, v_b_sp, v_g_kv, v_w_kv_a, v_g_kv_lat, v_w_in_b, v_g_q_lat, v_w_uq, v_w_uk, v_w_uv, v_g_mem, v_w_mem_kv, v_w_out, v_w_ffn_up, v_conv_w, v_conv_b, v_w_ffn_down))
```

```python
import math

import jax
import jax.numpy as jnp
from jax import lax
from jax.experimental import pallas as pl
from jax.experimental.pallas import tpu as pltpu

F32 = jnp.float32
BF16 = jnp.bfloat16

N_DEV = 8
N_A = 2
DEPTH = 4
G_HEADS = 12
HEAD = 128
CHUNK = 128
MEM_HEADS = 4
MEM_W = MEM_HEADS * HEAD
G_W = G_HEADS * HEAD
ROPE_DIM = 64
ROPE_HALF = ROPE_DIM // 2
KV_RANK = 512
Q_RANK = 512
KV_PAD = 640
ROPE_THETA = 10000.0
EPS = 1e-6
CONV_W = 3

ADAM_LR = 0.001
ADAM_B1 = 0.9
ADAM_B2 = 0.999
ADAM_EPS = 1e-08
ADAM_WD = 0.01
ADAM_STEP = 10

VMEM_LIMIT_V7X = 56 * 1024 * 1024
MASK_VALUE = -1e30

WEIGHTS = ['g_mix', 'g_ffn', 'g_final', 'w_in_a', 'g_v', 'w_sp', 'b_sp', 'g_kv', 'w_kv_a', 'g_kv_lat',
           'w_in_b', 'g_q_lat', 'w_uq', 'w_uk', 'w_uv', 'g_mem', 'w_mem_kv', 'w_out', 'w_ffn_up',
           'conv_w', 'conv_b', 'w_ffn_down']
REPLICATED = ['g_mix', 'g_ffn', 'g_final', 'w_sp', 'b_sp', 'g_kv', 'g_kv_lat', 'g_q_lat', 'g_mem', 'conv_b']
SMALL_SHARDED = ['g_v', 'conv_w']


def _params(sem=None):
    return pltpu.CompilerParams(dimension_semantics=sem, vmem_limit_bytes=VMEM_LIMIT_V7X)


def _dot(a, b, dims):
    contract = {'nn': ((1,), (0,)), 'nt': ((1,), (1,)), 'tn': ((0,), (0,))}[dims]
    return lax.dot_general(a, b, (contract, ((), ())), preferred_element_type=F32)


def _erf(x):
    return lax.erf(x)


def _gelu(x):
    return 0.5 * x * (1.0 + _erf(x * (2.0 ** -0.5)))


def _gelu_grad(x):
    cdf = 0.5 * (1.0 + _erf(x * (2.0 ** -0.5)))
    pdf = jnp.exp(-0.5 * x * x) * (1.0 / math.sqrt(2.0 * math.pi))
    return cdf + x * pdf


def _sigmoid(x):
    return 1.0 / (1.0 + jnp.exp(-x))


def _operand_spec(shape, lead, blocked, tr, tc, ridx, cidx):
    if blocked:
        per = shape[-1] // tc
        assert shape[-1] % tc == 0, (shape, tc)
        return pl.BlockSpec(
            (None,) * (1 + len(lead)) + (tr, tc),
            lambda *g: (cidx(*g) // per,) + lead + (ridx(*g), cidx(*g) % per))
    return pl.BlockSpec((None,) * len(lead) + (tr, tc), lambda *g: lead + (ridx(*g), cidx(*g)))


def _view2d(x, blocked):
    return (x.shape[-2], x.shape[0] * x.shape[-1]) if blocked else (x.shape[-2], x.shape[-1])


def _mm(a, b, *, dims, out_dtype, name, tm, tn, tk=None, res=None, a_lead=(), b_lead=(),
        a_blocked=False, b_blocked=False, out_block=None, n_outer=False, after=None):
    ar, ac = _view2d(a, a_blocked)
    br, bc = _view2d(b, b_blocked)
    m, k = (ac, ar) if dims == 'tn' else (ar, ac)
    n, k2 = (br, bc) if dims == 'nt' else (bc, br)
    assert k == k2, (a.shape, b.shape, dims)
    tm, tn = min(tm, m), min(tn, n)
    tk = k if tk is None else tk
    assert m % tm == 0 and n % tn == 0 and k % tk == 0, (name, m, n, k, tm, tn, tk)
    nk = k // tk
    if n_outer:
        gi, gj = (lambda g0, g1, g2: g1), (lambda g0, g1, g2: g0)
        grid = (n // tn, m // tm, nk)
    else:
        gi, gj = (lambda g0, g1, g2: g0), (lambda g0, g1, g2: g1)
        grid = (m // tm, n // tn, nk)
    gk = lambda g0, g1, g2: g2

    if dims == 'tn':
        a_spec = _operand_spec(a.shape, a_lead, a_blocked, tk, tm, gk, gi)
    else:
        a_spec = _operand_spec(a.shape, a_lead, a_blocked, tm, tk, gi, gk)
    if dims == 'nt':
        b_spec = _operand_spec(b.shape, b_lead, b_blocked, tn, tk, gj, gk)
    else:
        b_spec = _operand_spec(b.shape, b_lead, b_blocked, tk, tn, gk, gj)
    in_specs = [a_spec, b_spec]
    operands = [a, b]
    if res is not None:
        in_specs.append(pl.BlockSpec((tm, tn), lambda *g: (gi(*g), gj(*g))))
        operands.append(res)
    if after is not None:
        in_specs.append(pl.BlockSpec(memory_space=pl.ANY))
        operands.append(after)
    n_in = len(operands)
    if out_block is not None:
        out_shape = jax.ShapeDtypeStruct((n // out_block, m, out_block), out_dtype)
        out_spec = _operand_spec(out_shape.shape, (), True, tm, tn, gi, gj)
    else:
        out_shape = jax.ShapeDtypeStruct((m, n), out_dtype)
        out_spec = pl.BlockSpec((tm, tn), lambda *g: (gi(*g), gj(*g)))

    def body(*refs):
        a_ref, b_ref = refs[0], refs[1]
        r_ref = refs[2] if res is not None else None
        o_ref = refs[n_in]
        acc_ref = refs[-1] if nk > 1 else None
        part = _dot(a_ref[...].astype(BF16), b_ref[...].astype(BF16), dims)

        def finish(total):
            if r_ref is not None:
                total = total + r_ref[...]
            o_ref[...] = total.astype(o_ref.dtype)

        if nk == 1:
            finish(part)
        else:
            kk = pl.program_id(2)

            @pl.when(kk == 0)
            def _():
                acc_ref[...] = part

            @pl.when(kk > 0)
            def _():
                acc_ref[...] += part

            @pl.when(kk == nk - 1)
            def _():
                finish(acc_ref[...])

    scratch = [pltpu.VMEM((tm, tn), F32)] if nk > 1 else []
    return pl.pallas_call(
        body, name=name, grid=grid, in_specs=in_specs, out_specs=out_spec,
        out_shape=out_shape, scratch_shapes=scratch,
        compiler_params=_params(("parallel", "parallel", "arbitrary")),
    )(*operands)


def _mm_blocked_nt(a, b, *, out_dtype, name, tm, tn, blocks_per_step, after=None):
    nb, m, bw = a.shape
    n = b.shape[1]
    tm, tn = min(tm, m), min(tn, n)
    assert nb % blocks_per_step == 0 and m % tm == 0 and n % tn == 0
    nk = nb // blocks_per_step

    def body(*refs):
        a_ref, b_ref, o_ref, acc_ref = refs[0], refs[1], refs[-2], refs[-1]
        kk = pl.program_id(2)
        part = _dot(a_ref[0], b_ref[0], 'nt')
        for t in range(1, blocks_per_step):
            part = part + _dot(a_ref[t], b_ref[t], 'nt')

        @pl.when(kk == 0)
        def _():
            acc_ref[...] = part

        @pl.when(kk > 0)
        def _():
            acc_ref[...] += part

        @pl.when(kk == nk - 1)
        def _():
            o_ref[...] = acc_ref[...].astype(o_ref.dtype)

    in_specs = [pl.BlockSpec((blocks_per_step, tm, bw), lambda i, j, k: (k, i, 0)),
                pl.BlockSpec((blocks_per_step, tn, bw), lambda i, j, k: (k, j, 0))]
    operands = [a, b]
    if after is not None:
        in_specs.append(pl.BlockSpec(memory_space=pl.ANY))
        operands.append(after)
    return pl.pallas_call(
        body, name=name, grid=(m // tm, n // tn, nk), in_specs=in_specs,
        out_specs=pl.BlockSpec((tm, tn), lambda i, j, k: (i, j)), out_shape=jax.ShapeDtypeStruct((m, n), out_dtype),
        scratch_shapes=[pltpu.VMEM((tm, tn), F32)],
        compiler_params=_params(("parallel", "parallel", "arbitrary")),
    )(*operands)


def _rmsnorm(x, g, *, name, width=None, out_dtype=BF16, tm=512):
    s = x.shape[0]
    w = x.shape[1] if width is None else width
    tm = min(tm, s)

    def body(x_ref, g_ref, o_ref):
        xv = x_ref[...].astype(F32)
        rstd = lax.rsqrt(jnp.mean(xv * xv, axis=-1, keepdims=True) + EPS)
        o_ref[...] = (xv * rstd * g_ref[...]).astype(o_ref.dtype)

    return pl.pallas_call(
        body, name=name, grid=(s // tm,),
        in_specs=[pl.BlockSpec((tm, w), lambda i: (i, 0)), pl.BlockSpec((1, w), lambda i: (0, 0))],
        out_specs=pl.BlockSpec((tm, w), lambda i: (i, 0)),
        out_shape=jax.ShapeDtypeStruct((s, w), out_dtype),
        compiler_params=_params(("parallel",)),
    )(x, g.reshape(1, w))


def _rmsnorm_bwd(x, g, dy, *, name, width=None, dres=None, after=None, out_dtype=F32, tm=512):
    s = x.shape[0]
    w = x.shape[1] if width is None else width
    tm = min(tm, s)

    def body(*refs):
        x_ref, g_ref, dy_ref = refs[0], refs[1], refs[2]
        r_ref = refs[3] if dres is not None else None
        dx_ref, dg_ref = refs[-2], refs[-1]
        xv = x_ref[...].astype(F32)
        rstd = lax.rsqrt(jnp.mean(xv * xv, axis=-1, keepdims=True) + EPS)
        xhat = xv * rstd
        dyv = dy_ref[...].astype(F32)
        gdy = dyv * g_ref[...]
        dx = rstd * (gdy - xhat * jnp.mean(gdy * xhat, axis=-1, keepdims=True))
        if r_ref is not None:
            dx = dx + r_ref[...]
        dx_ref[...] = dx.astype(dx_ref.dtype)
        part = jnp.sum(dyv * xhat, axis=0, keepdims=True)

        @pl.when(pl.program_id(0) == 0)
        def _():
            dg_ref[...] = part

        @pl.when(pl.program_id(0) > 0)
        def _():
            dg_ref[...] += part

    row = pl.BlockSpec((tm, w), lambda i: (i, 0))
    vec = pl.BlockSpec((1, w), lambda i: (0, 0))
    in_specs = [row, vec, row] + ([row] if dres is not None else [])
    operands = [x, g.reshape(1, w), dy] + ([dres] if dres is not None else [])
    if after is not None:
        in_specs.append(pl.BlockSpec(memory_space=pl.ANY))
        operands.append(after)
    return pl.pallas_call(
        body, name=name, grid=(s // tm,), in_specs=in_specs, out_specs=[row, vec],
        out_shape=[jax.ShapeDtypeStruct((s, w), out_dtype), jax.ShapeDtypeStruct((1, w), F32)],
        compiler_params=_params(("arbitrary",)),
    )(*operands)


def _final_loss(x, target, g, *, name, tm=256):
    s, d = x.shape
    tm = min(tm, s)

    def body(x_ref, t_ref, g_ref, sq_ref, dx_ref, dg_ref):
        xv = x_ref[...]
        rstd = lax.rsqrt(jnp.mean(xv * xv, axis=-1, keepdims=True) + EPS)
        xhat = xv * rstd
        err = xhat * g_ref[...] - t_ref[...]
        dyv = err * (1.0 / d)
        gdy = dyv * g_ref[...]
        dx_ref[...] = rstd * (gdy - xhat * jnp.mean(gdy * xhat, axis=-1, keepdims=True))
        sq = jnp.sum(err * err, axis=0, keepdims=True)
        dg = jnp.sum(dyv * xhat, axis=0, keepdims=True)

        @pl.when(pl.program_id(0) == 0)
        def _():
            sq_ref[...] = sq
            dg_ref[...] = dg

        @pl.when(pl.program_id(0) > 0)
        def _():
            sq_ref[...] += sq
            dg_ref[...] += dg

    row = pl.BlockSpec((tm, d), lambda i: (i, 0))
    vec = pl.BlockSpec((1, d), lambda i: (0, 0))
    return pl.pallas_call(
        body, name=name, grid=(s // tm,), in_specs=[row, row, vec], out_specs=[vec, row, vec],
        out_shape=[jax.ShapeDtypeStruct((1, d), F32), jax.ShapeDtypeStruct((s, d), F32),
                   jax.ShapeDtypeStruct((1, d), F32)],
        compiler_params=_params(("arbitrary",)),
    )(x, target, g.reshape(1, d))


def _tril_mask():
    t = lax.broadcasted_iota(jnp.int32, (CHUNK, CHUNK), 0)
    s = lax.broadcasted_iota(jnp.int32, (CHUNK, CHUNK), 1)
    return t >= s


def _sgu_fwd(z, g_v, w_sp, b_sp_t, *, name):
    s = z.shape[0]

    def body(zu_ref, zv_ref, g_ref, w_ref, b_ref, o_ref):
        u = _gelu(zu_ref[...].astype(F32))
        gv = _gelu(zv_ref[...].astype(F32))
        rstd = lax.rsqrt(jnp.mean(gv * gv, axis=-1, keepdims=True) + EPS)
        v = (gv * rstd * g_ref[...]).astype(BF16)
        mask = _tril_mask()
        for grp in range(G_HEADS):
            cols = slice(grp * HEAD, (grp + 1) * HEAD)
            wm = jnp.where(mask, w_ref[grp], 0.0).astype(BF16)
            sv = _dot(wm, v[:, cols], 'nn') + b_ref[:, grp:grp + 1]
            o_ref[:, cols] = (u[:, cols] * sv).astype(o_ref.dtype)

    return pl.pallas_call(
        body, name=name, grid=(s // CHUNK,),
        in_specs=[pl.BlockSpec((CHUNK, G_W), lambda i: (i, 0)),
                  pl.BlockSpec((CHUNK, G_W), lambda i: (i, 1)),
                  pl.BlockSpec((1, G_W), lambda i: (0, 0)),
                  pl.BlockSpec((G_HEADS, CHUNK, CHUNK), lambda i: (0, 0, 0)),
                  pl.BlockSpec((CHUNK, G_HEADS), lambda i: (0, 0))],
        out_specs=pl.BlockSpec((CHUNK, G_W), lambda i: (i, 0)),
        out_shape=jax.ShapeDtypeStruct((s, G_W), BF16),
        compiler_params=_params(("parallel",)),
    )(z, z, g_v.reshape(1, G_W), w_sp, b_sp_t)


def _sgu_bwd(z, dmix, dqm, g_v, w_sp, b_sp_t, *, name):
    s = z.shape[0]
    zw = z.shape[1]

    def body(zu_ref, zv_ref, dm_ref, dq_ref, g_ref, w_ref, b_ref, dz_ref, dw_ref, db_ref, dg_ref):
        first = pl.program_id(0) == 0

        @pl.when(first)
        def _():
            dw_ref[...] = jnp.zeros_like(dw_ref)
            db_ref[...] = jnp.zeros_like(db_ref)
            dg_ref[...] = jnp.zeros_like(dg_ref)

        zu = zu_ref[...].astype(F32)
        zv = zv_ref[...].astype(F32)
        dmain = dm_ref[...].astype(F32)
        u = _gelu(zu)
        gv = _gelu(zv)
        rstd = lax.rsqrt(jnp.mean(gv * gv, axis=-1, keepdims=True) + EPS)
        vhat = gv * rstd
        gvec = g_ref[...]
        v = (vhat * gvec).astype(BF16)
        dsv = dmain * u
        dsv_b = dsv.astype(BF16)
        mask = _tril_mask()
        dv_parts = []
        for grp in range(G_HEADS):
            cols = slice(grp * HEAD, (grp + 1) * HEAD)
            wm = jnp.where(mask, w_ref[grp], 0.0).astype(BF16)
            sv = _dot(wm, v[:, cols], 'nn') + b_ref[:, grp:grp + 1]
            dz_ref[:, cols] = (dmain[:, cols] * sv * _gelu_grad(zu[:, cols])).astype(dz_ref.dtype)
            dwg = _dot(dsv_b[:, cols], v[:, cols], 'nt')
            dw_ref[grp] += jnp.where(mask, dwg, 0.0)
            db_ref[:, grp:grp + 1] += jnp.sum(dsv[:, cols], axis=-1, keepdims=True)
            dv_parts.append(_dot(wm, dsv_b[:, cols], 'tn'))
        dv = jnp.concatenate(dv_parts, axis=-1)
        dg_ref[...] += jnp.sum(dv * vhat, axis=0, keepdims=True)
        gdv = dv * gvec
        dgv = rstd * (gdv - vhat * jnp.mean(gdv * vhat, axis=-1, keepdims=True))
        dz_ref[:, G_W:2 * G_W] = (dgv * _gelu_grad(zv)).astype(dz_ref.dtype)
        dz_ref[:, 2 * G_W:] = dq_ref[...].astype(dz_ref.dtype)

    return pl.pallas_call(
        body, name=name, grid=(s // CHUNK,),
        in_specs=[pl.BlockSpec((CHUNK, G_W), lambda i: (i, 0)),
                  pl.BlockSpec((CHUNK, G_W), lambda i: (i, 1)),
                  pl.BlockSpec((CHUNK, G_W), lambda i: (i, 0)),
                  pl.BlockSpec((CHUNK, MEM_W), lambda i: (i, 0)),
                  pl.BlockSpec((1, G_W), lambda i: (0, 0)),
                  pl.BlockSpec((G_HEADS, CHUNK, CHUNK), lambda i: (0, 0, 0)),
                  pl.BlockSpec((CHUNK, G_HEADS), lambda i: (0, 0))],
        out_specs=[pl.BlockSpec((CHUNK, zw), lambda i: (i, 0)),
                   pl.BlockSpec((G_HEADS, CHUNK, CHUNK), lambda i: (0, 0, 0)),
                   pl.BlockSpec((CHUNK, G_HEADS), lambda i: (0, 0)),
                   pl.BlockSpec((1, G_W), lambda i: (0, 0))],
        out_shape=[jax.ShapeDtypeStruct((s, zw), BF16),
                   jax.ShapeDtypeStruct((G_HEADS, CHUNK, CHUNK), F32),
                   jax.ShapeDtypeStruct((CHUNK, G_HEADS), F32),
                   jax.ShapeDtypeStruct((1, G_W), F32)],
        compiler_params=_params(("arbitrary",)),
    )(z, z, dmix, dqm, g_v.reshape(1, G_W), w_sp, b_sp_t)


def _mem_probs(q, k):
    sc = _dot(q, k, 'nt') * (HEAD ** -0.5)
    sc = sc - jnp.max(sc, axis=-1, keepdims=True)
    e = jnp.exp(sc)
    return e / jnp.sum(e, axis=-1, keepdims=True)


def _memattn_fwd(z, kvm, main, *, qcol, name, tm=512):
    s = z.shape[0]
    m = kvm.shape[0]
    tm = min(tm, s)

    def body(q_ref, kv_ref, main_ref, o_ref):
        o_ref[:, :G_W] = main_ref[...]
        for h in range(MEM_HEADS):
            cols = slice(h * HEAD, (h + 1) * HEAD)
            k = kv_ref[:, cols]
            v = kv_ref[:, MEM_W + h * HEAD:MEM_W + (h + 1) * HEAD]
            p = _mem_probs(q_ref[:, cols], k)
            o_ref[:, G_W + h * HEAD:G_W + (h + 1) * HEAD] = _dot(p.astype(BF16), v, 'nn').astype(o_ref.dtype)

    return pl.pallas_call(
        body, name=name, grid=(s // tm,),
        in_specs=[pl.BlockSpec((tm, MEM_W), lambda i: (i, qcol)),
                  pl.BlockSpec((m, 2 * MEM_W), lambda i: (0, 0)),
                  pl.BlockSpec((tm, G_W), lambda i: (i, 0))],
        out_specs=pl.BlockSpec((tm, G_W + MEM_W), lambda i: (i, 0)),
        out_shape=jax.ShapeDtypeStruct((s, G_W + MEM_W), BF16),
        compiler_params=_params(("parallel",)),
    )(z, kvm, main)


def _memattn_bwd(z, kvm, dmix, *, qcol, name, tm=512):
    s = z.shape[0]
    m = kvm.shape[0]
    tm = min(tm, s)
    scale = HEAD ** -0.5

    def body(q_ref, kv_ref, do_ref, dq_ref, dkv_ref):
        @pl.when(pl.program_id(0) == 0)
        def _():
            dkv_ref[...] = jnp.zeros_like(dkv_ref)

        for h in range(MEM_HEADS):
            cols = slice(h * HEAD, (h + 1) * HEAD)
            vcols = slice(MEM_W + h * HEAD, MEM_W + (h + 1) * HEAD)
            q = q_ref[:, cols]
            k = kv_ref[:, cols]
            v = kv_ref[:, vcols]
            do = do_ref[:, cols]
            p = _mem_probs(q, k)
            dp = _dot(do, v, 'nt')
            ds = (p * (dp - jnp.sum(dp * p, axis=-1, keepdims=True)) * scale).astype(BF16)
            dq_ref[:, cols] = _dot(ds, k, 'nn').astype(dq_ref.dtype)
            dkv_ref[:, cols] += _dot(ds, q, 'tn')
            dkv_ref[:, vcols] += _dot(p.astype(BF16), do, 'tn')

    mo_block = G_W // MEM_W
    return pl.pallas_call(
        body, name=name, grid=(s // tm,),
        in_specs=[pl.BlockSpec((tm, MEM_W), lambda i: (i, qcol)),
                  pl.BlockSpec((m, 2 * MEM_W), lambda i: (0, 0)),
                  pl.BlockSpec((tm, MEM_W), lambda i: (i, mo_block))],
        out_specs=[pl.BlockSpec((tm, MEM_W), lambda i: (i, 0)),
                   pl.BlockSpec((m, 2 * MEM_W), lambda i: (0, 0))],
        out_shape=[jax.ShapeDtypeStruct((s, MEM_W), BF16), jax.ShapeDtypeStruct((m, 2 * MEM_W), F32)],
        compiler_params=_params(("arbitrary",)),
    )(z, kvm, dmix)


def _rope(x1, x2, cos, sin, *, name, inverse=False, out_dtype=BF16, col1=0, col2=0, tm=512):
    s, w = cos.shape
    tm = min(tm, s)
    sign = -1.0 if inverse else 1.0

    def body(a_ref, b_ref, c_ref, s_ref, o1_ref, o2_ref):
        a = a_ref[...].astype(F32)
        b = b_ref[...].astype(F32)
        c = c_ref[...]
        sn = s_ref[...] * sign
        o1_ref[...] = (a * c - b * sn).astype(o1_ref.dtype)
        o2_ref[...] = (b * c + a * sn).astype(o2_ref.dtype)

    row = pl.BlockSpec((tm, w), lambda i: (i, 0))
    return pl.pallas_call(
        body, name=name, grid=(s // tm,),
        in_specs=[pl.BlockSpec((tm, w), lambda i: (i, col1)), pl.BlockSpec((tm, w), lambda i: (i, col2)), row, row],
        out_specs=[row, row],
        out_shape=[jax.ShapeDtypeStruct((s, w), out_dtype)] * 2,
        compiler_params=_params(("parallel",)),
    )(x1, x2, cos, sin)


MHA_BLOCK = 1024
QK_DIM = HEAD + ROPE_DIM


def _mha_scores(q, k, scale, diagonal):
    sc = _dot(q, k, 'nt') * scale
    if not diagonal:
        return sc, None
    rows = lax.broadcasted_iota(jnp.int32, sc.shape, 0)
    cols = lax.broadcasted_iota(jnp.int32, sc.shape, 1)
    return sc, cols <= rows


def _mha_fwd(q, k, vv, *, name):
    s = k.shape[1]
    tb = min(MHA_BLOCK, s)
    scale = QK_DIM ** -0.5

    def body(q_ref, k_ref, v_ref, o_ref, lse_ref, m_ref, l_ref, acc_ref):
        i = pl.program_id(1)
        qh = q_ref[...]
        m_ref[...] = jnp.full_like(m_ref, MASK_VALUE)
        l_ref[...] = jnp.zeros_like(l_ref)
        acc_ref[...] = jnp.zeros_like(acc_ref)

        def block(j, diagonal):
            ks = pl.multiple_of(j * tb, tb)
            kj, vj = k_ref[pl.ds(ks, tb), :], v_ref[pl.ds(ks, tb), :]
            sc, keep = _mha_scores(qh, kj, scale, diagonal)
            if diagonal:
                sc = jnp.where(keep, sc, MASK_VALUE)
            m_old = m_ref[...]
            m_new = jnp.maximum(m_old, jnp.max(sc, axis=-1, keepdims=True))
            p = jnp.exp(sc - m_new)
            alpha = jnp.exp(m_old - m_new)
            l_ref[...] = alpha * l_ref[...] + jnp.sum(p, axis=-1, keepdims=True)
            acc_ref[...] = alpha * acc_ref[...] + _dot(p.astype(BF16), vj, 'nn')
            m_ref[...] = m_new

        def step(j, carry):
            block(j, False)
            return carry

        lax.fori_loop(0, i, step, 0)
        block(i, True)
        l = l_ref[...]
        o_ref[...] = (acc_ref[...] / l).astype(o_ref.dtype)
        lse_ref[...] = m_ref[...] + jnp.log(l)

    return pl.pallas_call(
        body, name=name, grid=(G_HEADS, s // tb),
        in_specs=[pl.BlockSpec((None, tb, QK_DIM), lambda h, i: (h, i, 0)),
                  pl.BlockSpec((None, s, QK_DIM), lambda h, i: (h, 0, 0)),
                  pl.BlockSpec((s, HEAD), lambda h, i: (0, h))],
        out_specs=[pl.BlockSpec((tb, HEAD), lambda h, i: (i, h)),
                   pl.BlockSpec((None, tb, 1), lambda h, i: (h, i, 0))],
        out_shape=[jax.ShapeDtypeStruct((s, G_W), BF16), jax.ShapeDtypeStruct((G_HEADS, s, 1), F32)],
        scratch_shapes=[pltpu.VMEM((tb, 1), F32), pltpu.VMEM((tb, 1), F32), pltpu.VMEM((tb, HEAD), F32)],
        compiler_params=_params(("parallel", "arbitrary")),
    )(q, k, vv)


def _mha_bwd(q, k, vv, o, do, lse, *, name):
    s = k.shape[1]
    tb = min(MHA_BLOCK, s)
    nq = s // tb
    scale = QK_DIM ** -0.5

    def body(q_ref, k_ref, v_ref, o_ref, do_ref, lse_ref, dq_ref, dk_ref, dv_ref, dqa_ref):
        i = pl.program_id(1)

        @pl.when(i == 0)
        def _():
            dk_ref[...] = jnp.zeros_like(dk_ref)
            dv_ref[...] = jnp.zeros_like(dv_ref)

        qh, dov = q_ref[...], do_ref[...]
        delta = jnp.sum(dov.astype(F32) * o_ref[...].astype(F32), axis=-1, keepdims=True)
        lsev = lse_ref[...]
        dqa_ref[...] = jnp.zeros_like(dqa_ref)

        def block(j, diagonal):
            ks = pl.multiple_of(j * tb, tb)
            kj, vj = k_ref[pl.ds(ks, tb), :], v_ref[pl.ds(ks, tb), :]
            sc, keep = _mha_scores(qh, kj, scale, diagonal)
            p = jnp.exp(sc - lsev)
            if diagonal:
                p = jnp.where(keep, p, 0.0)
            dp = _dot(dov, vj, 'nt')
            ds = (p * (dp - delta) * scale).astype(BF16)
            dqa_ref[...] += _dot(ds, kj, 'nn')
            dk_ref[pl.ds(ks, tb), :] += _dot(ds, qh, 'tn')
            dv_ref[pl.ds(ks, tb), :] += _dot(p.astype(BF16), dov, 'tn')

        def step(j, carry):
            block(j, False)
            return carry

        lax.fori_loop(0, i, step, 0)
        block(i, True)
        dq_ref[...] = dqa_ref[...].astype(dq_ref.dtype)

    q_tile = pl.BlockSpec((None, tb, QK_DIM), lambda h, i: (h, i, 0))
    k_head = pl.BlockSpec((None, s, QK_DIM), lambda h, i: (h, 0, 0))
    tile = pl.BlockSpec((tb, HEAD), lambda h, i: (i, h))
    v_head = pl.BlockSpec((s, HEAD), lambda h, i: (0, h))
    return pl.pallas_call(
        body, name=name, grid=(G_HEADS, nq),
        in_specs=[q_tile, k_head, v_head, tile, tile, pl.BlockSpec((None, tb, 1), lambda h, i: (h, i, 0))],
        out_specs=[q_tile, k_head, v_head],
        out_shape=[jax.ShapeDtypeStruct((G_HEADS, s, QK_DIM), BF16), jax.ShapeDtypeStruct((G_HEADS, s, QK_DIM), F32),
                   jax.ShapeDtypeStruct((s, G_W), F32)],
        scratch_shapes=[pltpu.VMEM((tb, QK_DIM), F32)],
        compiler_params=_params(("parallel", "arbitrary")),
    )(q, k, vv, o, do, lse)


HALO = 16


def _shift_down(prev, cur, shift, first_tile):
    tr = cur.shape[0]
    full = jnp.concatenate([prev, cur], axis=0)
    out = pltpu.roll(full, shift, axis=0)[HALO:]
    row = lax.broadcasted_iota(jnp.int32, (tr, 1), 0)
    return jnp.where(jnp.logical_and(first_tile, row < shift), 0.0, out)


def _shift_up(cur, nxt, shift, last_tile):
    tr = cur.shape[0]
    full = jnp.concatenate([cur, nxt], axis=0)
    out = pltpu.roll(full, tr + HALO - shift, axis=0)[:tr]
    row = lax.broadcasted_iota(jnp.int32, (tr, 1), 0)
    return jnp.where(jnp.logical_and(last_tile, row >= tr - shift), 0.0, out)


def _lane_chunks(width, lanes):
    return [slice(c0, min(c0 + lanes, width)) for c0 in range(0, width, lanes)]


def _conv_taps(prev_ref, cur_ref, cw_ref, cb_ref, first_tile, cs):
    cur = cur_ref[:, cs].astype(F32)
    prev = prev_ref[:, cs].astype(F32)
    a1 = _shift_down(prev, cur, 1, first_tile)
    a2 = _shift_down(prev, cur, 2, first_tile)
    c = a2 * cw_ref[0:1, cs] + a1 * cw_ref[1:2, cs] + cur * cw_ref[2:3, cs] + cb_ref[:, cs]
    return c, (a2, a1, cur)


def _conv_in_specs(tr, bw, half, layer, row_of, blk_of):
    per = tr // HALO
    specs = []
    for off in (0, half):
        specs.append(pl.BlockSpec((None, HALO, bw), lambda *g, off=off: (blk_of(*g) + off, jnp.maximum(row_of(*g) * per - 1, 0), 0)))
        specs.append(pl.BlockSpec((None, tr, bw), lambda *g, off=off: (blk_of(*g) + off, row_of(*g), 0)))
    for off in (0, half):
        specs.append(pl.BlockSpec((None, None, CONV_W, bw), lambda *g, off=off: (blk_of(*g) + off, layer, 0, 0)))
    for off in (0, half):
        specs.append(pl.BlockSpec((None, 1, bw), lambda *g, off=off: (layer * 2 * half + blk_of(*g) + off, 0, 0)))
    return specs


def _conv_fwd(a, cw, cb, layer, *, name, tr=256):
    nb, s, bw = a.shape
    half = nb // 2
    tr = min(tr, s)

    def body(gp_ref, gc_ref, vp_ref, vc_ref, cwg_ref, cwv_ref, cbg_ref, cbv_ref, o_ref):
        first = pl.program_id(0) == 0
        for cs in _lane_chunks(bw, 256):
            gate, _ = _conv_taps(gp_ref, gc_ref, cwg_ref, cbg_ref, first, cs)
            val, _ = _conv_taps(vp_ref, vc_ref, cwv_ref, cbv_ref, first, cs)
            o_ref[:, cs] = (gate * _sigmoid(gate) * val).astype(o_ref.dtype)

    return pl.pallas_call(
        body, name=name, grid=(s // tr, half),
        in_specs=_conv_in_specs(tr, bw, half, layer, lambda i, j: i, lambda i, j: j),
        out_specs=pl.BlockSpec((tr, bw), lambda i, j: (i, j)),
        out_shape=jax.ShapeDtypeStruct((s, half * bw), BF16),
        compiler_params=_params(("parallel", "parallel")),
    )(a, a, a, a, cw, cw, cb, cb)


def _conv_bwd_dc(a, dact, cw, cb, layer, *, name, after=None, tr=256):
    nb, s, bw = a.shape
    half = nb // 2
    tr = min(tr, s)

    def body(*refs):
        gp_ref, gc_ref, vp_ref, vc_ref, cwg_ref, cwv_ref, cbg_ref, cbv_ref, da_ref = refs[:9]
        dc_ref, dw_ref, db_ref = refs[-3:]
        first = pl.program_id(1) == 0

        @pl.when(first)
        def _():
            dw_ref[...] = jnp.zeros_like(dw_ref)
            db_ref[...] = jnp.zeros_like(db_ref)

        for cs in _lane_chunks(bw, 128):
            gate, gtaps = _conv_taps(gp_ref, gc_ref, cwg_ref, cbg_ref, first, cs)
            val, vtaps = _conv_taps(vp_ref, vc_ref, cwv_ref, cbv_ref, first, cs)
            dact_v = da_ref[:, cs].astype(F32)
            sg = _sigmoid(gate)
            dgate = dact_v * val * (sg * (1.0 + gate * (1.0 - sg)))
            dval = dact_v * (gate * sg)
            dc_ref[0, :, cs] = dgate.astype(dc_ref.dtype)
            dc_ref[1, :, cs] = dval.astype(dc_ref.dtype)
            for kk in range(CONV_W):
                dw_ref[0, kk:kk + 1, cs] += jnp.sum(dgate * gtaps[kk], axis=0, keepdims=True)
                dw_ref[1, kk:kk + 1, cs] += jnp.sum(dval * vtaps[kk], axis=0, keepdims=True)
            db_ref[0, :, cs] += jnp.sum(dgate, axis=0, keepdims=True)
            db_ref[1, :, cs] += jnp.sum(dval, axis=0, keepdims=True)

    outs = pl.pallas_call(
        body, name=name, grid=(half, s // tr),
        in_specs=_conv_in_specs(tr, bw, half, layer, lambda j, i: i, lambda j, i: j)
        + [pl.BlockSpec((tr, bw), lambda j, i: (i, j))]
        + ([pl.BlockSpec(memory_space=pl.ANY)] if after is not None else []),
        out_specs=[pl.BlockSpec((2, None, tr, bw), lambda j, i: (0, j, i, 0)),
                   pl.BlockSpec((2, None, CONV_W, bw), lambda j, i: (0, j, 0, 0)),
                   pl.BlockSpec((2, None, 1, bw), lambda j, i: (0, j, 0, 0))],
        out_shape=[jax.ShapeDtypeStruct((2, half, s, bw), BF16),
                   jax.ShapeDtypeStruct((2, half, CONV_W, bw), F32),
                   jax.ShapeDtypeStruct((2, half, 1, bw), F32)],
        compiler_params=_params(("parallel", "arbitrary")),
    )(a, a, a, a, cw, cw, cb, cb, dact, *([after] if after is not None else []))
    dc, dw, db = outs
    return dc.reshape(nb, s, bw), dw.reshape(nb, CONV_W, bw), db.reshape(nb, 1, bw)


def _conv_bwd_da(dc, cw, layer, *, name, tr=512):
    nb, s, bw = dc.shape
    tr = min(tr, s)
    ni = s // tr
    per = tr // HALO
    last_halo = s // HALO - 1

    def body(c_ref, n_ref, w_ref, o_ref):
        last = pl.program_id(0) == ni - 1
        for cs in _lane_chunks(bw, 256):
            cur = c_ref[:, cs].astype(F32)
            nxt = n_ref[:, cs].astype(F32)
            da = (cur * w_ref[2:3, cs] + _shift_up(cur, nxt, 1, last) * w_ref[1:2, cs]
                  + _shift_up(cur, nxt, 2, last) * w_ref[0:1, cs])
            o_ref[:, cs] = da.astype(o_ref.dtype)

    tile = pl.BlockSpec((None, tr, bw), lambda i, j: (j, i, 0))
    return pl.pallas_call(
        body, name=name, grid=(ni, nb),
        in_specs=[tile,
                  pl.BlockSpec((None, HALO, bw), lambda i, j: (j, jnp.minimum((i + 1) * per, last_halo), 0)),
                  pl.BlockSpec((None, None, CONV_W, bw), lambda i, j: (j, layer, 0, 0))],
        out_specs=tile,
        out_shape=jax.ShapeDtypeStruct((nb, s, bw), BF16),
        compiler_params=_params(("parallel", "parallel")),
    )(dc, dc, cw)


def _rope_tables(positions):
    inv = 1.0 / (ROPE_THETA ** (jnp.arange(0, ROPE_DIM, 2, dtype=F32) / ROPE_DIM))
    ang = positions.astype(F32)[:, None] * inv
    return jnp.cos(ang), jnp.sin(ang)


def _heads_to_major(nope, r1, r2):
    s = r1.shape[0]
    parts = [nope[:, :G_W].reshape(s, G_HEADS, HEAD)]
    for r in (r1, r2):
        parts.append(jnp.broadcast_to(r.reshape(s, -1, ROPE_HALF), (s, G_HEADS, ROPE_HALF)))
    return jnp.concatenate(parts, axis=-1).transpose(1, 0, 2)


def _heads_from_major(t):
    s = t.shape[1]
    t = t.transpose(1, 0, 2)
    return t[:, :, :HEAD].reshape(s, G_W), t[:, :, HEAD:HEAD + ROPE_HALF], t[:, :, HEAD + ROPE_HALF:]


def _local_step(x, mem, positions, target, rep, fetch, emit):
    s, d = x.shape
    n_b = DEPTH - N_A
    tm = min(1024, s)
    cos, sin = _rope_tables(positions)
    cos12 = jnp.tile(cos, (1, G_HEADS))
    sin12 = jnp.tile(sin, (1, G_HEADS))
    r1_col = G_W // (G_HEADS * ROPE_HALF)
    b_sp_t = rep['b_sp'].transpose(0, 2, 1)

    saved = []
    kv = None
    shared = None
    for l in range(DEPTH):
        wm = fetch(('in', l), x)
        if l == 0:
            shared = {'g_v': wm['g_v'], 'conv_w': wm['conv_w']}
            bw = shared['conv_w'].shape[-1]
            conv_b = rep['conv_b'].reshape(-1, 1, bw)
        sv = {'x_in': x, 'wm': wm}
        if l == N_A:
            xn_kv = _rmsnorm(x, rep['g_kv'], name="kvnorm")
            kvx = _mm(xn_kv, wm['w_kv_a'], dims='nn', out_dtype=F32, name="kvproj", tm=tm, tn=KV_PAD)
            ckv = _rmsnorm(kvx, rep['g_kv_lat'], width=KV_RANK, name="ckvnorm")
            k1, k2 = _rope(kvx[:, KV_RANK:KV_RANK + ROPE_HALF], kvx[:, KV_RANK + ROPE_HALF:KV_RANK + ROPE_DIM],
                           cos, sin, name="krope")
            kv = {'x': x, 'xn': xn_kv, 'kvx': kvx, 'ckv': ckv, 'k1': k1, 'k2': k2, 'w_kv_a': wm['w_kv_a']}
        h = _rmsnorm(x, rep['g_mix'][l], name=f"mixnorm{l}")
        if l < N_A:
            z = _mm(h, wm['w_in'], dims='nn', out_dtype=BF16, name=f"in_a{l}", tm=tm, tn=512)
            main = _sgu_fwd(z, shared['g_v'][l], rep['w_sp'][l], b_sp_t[l], name=f"sgu{l}")
            qcol = 2 * G_W // MEM_W
        else:
            j = l - N_A
            z = _mm(h, wm['w_in'], dims='nn', out_dtype=BF16, name=f"in_b{j}", tm=tm, tn=1024)
            qn = _rmsnorm(z, rep['g_q_lat'][j], width=Q_RANK, name=f"qnorm{j}")
            qp = _mm(qn, wm['w_uqp'], dims='nn', out_dtype=BF16, name=f"uq{j}", tm=tm, tn=768)
            rr1, rr2 = _rope(qp, qp, cos12, sin12, col1=r1_col, col2=r1_col + 1, name=f"qrope{j}")
            qh = _heads_to_major(qp, rr1, rr2)
            kn = _mm(kv['ckv'], wm['w_uk'], dims='nn', out_dtype=BF16, name=f"k_up{j}", tm=tm, tn=768)
            kh = _heads_to_major(kn, kv['k1'], kv['k2'])
            vv = _mm(kv['ckv'], wm['w_uv'], dims='nn', out_dtype=BF16, name=f"v_up{j}", tm=tm, tn=768)
            main, lse = _mha_fwd(qh, kh, vv, name=f"mha{j}")
            qcol = Q_RANK // MEM_W
            sv.update(qn=qn, qh=qh, kh=kh, vv=vv, lse=lse)
        wm.update(fetch(('rest', l), z))
        memn = _rmsnorm(mem, rep['g_mem'][l], name=f"memnorm{l}")
        kvm = _mm(memn, wm['w_mem_kv'], dims='nn', out_dtype=BF16, name=f"memkv{l}", tm=tm, tn=1024)
        mix = _memattn_fwd(z, kvm, main, qcol=qcol, name=f"memattn{l}")
        x_mid = _mm(mix, wm['w_out'], dims='nn', res=x, out_dtype=F32, name=f"out{l}", tm=tm, tn=1024)
        wf = fetch(('up', l), x_mid)
        h2 = _rmsnorm(x_mid, rep['g_ffn'][l], name=f"ffnnorm{l}")
        a = _mm(h2, wf['w_up'], dims='nn', b_blocked=True, out_dtype=BF16, out_block=bw,
                name=f"up{l}", tm=tm, tn=bw)
        act = _conv_fwd(a, shared['conv_w'], conv_b, l, name=f"conv{l}")
        wf.update(fetch(('down', l), act))
        x = _mm(act, wf['w_down'], dims='nn', res=x_mid, out_dtype=F32, name=f"down{l}", tm=512, tn=1024)
        sv.update(h=h, memn=memn, kvm=kvm, z=z, qcol=qcol, mix=mix, x_mid=x_mid, h2=h2, a=a, act=act, wf=wf)
        saved.append(sv)

    sq, dx, dg_final = _final_loss(x, target, rep['g_final'], name="loss")

    g = {k: [None] * DEPTH for k in ('g_mix', 'g_ffn', 'g_mem', 'conv_w', 'conv_b')}
    for k in ('g_v', 'w_sp', 'b_sp'):
        g[k] = [None] * N_A
    g['g_q_lat'] = [None] * n_b
    g['g_final'] = dg_final
    dckv_sum, dkr_sum = None, None

    for l in reversed(range(DEPTH)):
        sv = saved[l]
        wm, wf = sv['wm'], sv['wf']
        dact = _mm(dx, wf['w_down'], dims='nt', out_dtype=BF16, name=f"d_act{l}", tm=tm, tn=bw)
        dw_down = _mm(sv['act'], dx, dims='tn', out_dtype=BF16, name=f"dw_down{l}", tm=bw, tn=512)
        tok = emit(('down', l), {'w_ffn_down': dw_down})
        dc, dcw, dcb = _conv_bwd_dc(sv['a'], dact, shared['conv_w'], conv_b, l, after=tok, name=f"d_conv{l}")
        g['conv_w'][l], g['conv_b'][l] = dcw, dcb
        da = _conv_bwd_da(dc, shared['conv_w'], l, name=f"d_convin{l}")
        dw_up = _mm(sv['h2'], da, dims='tn', b_blocked=True, out_dtype=BF16, out_block=bw,
                    name=f"dw_up{l}", tm=512, tn=bw, n_outer=True)
        tok = emit(('up', l), {'w_ffn_up': dw_up})
        dh2 = _mm_blocked_nt(da, wf['w_up'], out_dtype=BF16, name=f"d_h2{l}", tm=512, tn=1024, blocks_per_step=4,
                             after=tok)
        dx_mid, g['g_ffn'][l] = _rmsnorm_bwd(sv['x_mid'], rep['g_ffn'][l], dh2, dres=dx, name=f"d_ffnnorm{l}")
        dmix = _mm(dx_mid, wm['w_out'], dims='nt', out_dtype=BF16, name=f"d_mix{l}", tm=tm, tn=1024)
        dw_out = _mm(sv['mix'], dx_mid, dims='tn', out_dtype=BF16, name=f"dw_out{l}", tm=1024, tn=256)
        dqm, dkvm = _memattn_bwd(sv['z'], sv['kvm'], dmix, qcol=sv['qcol'], name=f"d_memattn{l}")
        dw_memkv = _mm(sv['memn'], dkvm, dims='tn', out_dtype=BF16, name=f"dw_memkv{l}", tm=1024, tn=1024)
        tok = emit(('rest', l), {'w_out': dw_out, 'w_mem_kv': dw_memkv})
        gm = {}
        dmemn = _mm(dkvm, wm['w_mem_kv'], dims='nt', out_dtype=F32, name=f"d_memn{l}", tm=tm, tn=1024, after=tok)
        _, g['g_mem'][l] = _rmsnorm_bwd(mem, rep['g_mem'][l], dmemn, out_dtype=BF16, name=f"d_memnorm{l}")
        if l < N_A:
            dz, dwsp, dbsp_t, dgv = _sgu_bwd(sv['z'], dmix, dqm, shared['g_v'][l], rep['w_sp'][l], b_sp_t[l],
                                             name=f"d_sgu{l}")
            g['w_sp'][l], g['b_sp'][l], g['g_v'][l] = dwsp, dbsp_t.T, dgv
            dh = _mm(dz, wm['w_in'], dims='nt', out_dtype=BF16, name=f"d_h_a{l}", tm=tm, tn=1024)
            gm['w_in_a'] = _mm(sv['h'], dz, dims='tn', out_dtype=BF16, name=f"dw_in_a{l}", tm=1024, tn=512)
        else:
            j = l - N_A
            dqh, dkh, dvv = _mha_bwd(sv['qh'], sv['kh'], sv['vv'], sv['mix'], dmix, sv['lse'], name=f"d_mha{j}")
            dq_nope, dr1, dr2 = _heads_from_major(dqh)
            dkn, dk1, dk2 = _heads_from_major(dkh)
            dkr = jnp.concatenate([dk1.sum(axis=1), dk2.sum(axis=1)], axis=-1)
            gm['w_uk'] = _mm(kv['ckv'], dkn, dims='tn', out_dtype=BF16, name=f"dw_uk{j}", tm=512, tn=768)
            gm['w_uv'] = _mm(kv['ckv'], dvv, dims='tn', out_dtype=BF16, name=f"dw_uv{j}", tm=512, tn=768)
            dckv = _mm(dkn, wm['w_uk'], dims='nt', out_dtype=F32, res=dckv_sum, name=f"d_ckv_k{j}", tm=tm, tn=512)
            dckv_sum = _mm(dvv, wm['w_uv'], dims='nt', out_dtype=F32, res=dckv, name=f"d_ckv_v{j}", tm=tm, tn=512)
            dkr_sum = dkr if dkr_sum is None else dkr_sum + dkr
            dq1, dq2 = _rope(dr1.reshape(s, -1), dr2.reshape(s, -1), cos12, sin12, inverse=True, name=f"d_qrope{j}")
            dqp = jnp.concatenate([dq_nope, dq1, dq2], axis=-1)
            dqn = _mm(dqp, wm['w_uqp'], dims='nt', out_dtype=BF16, name=f"d_qn{j}", tm=tm, tn=512)
            gm['w_uqp'] = _mm(sv['qn'], dqp, dims='tn', out_dtype=BF16, name=f"dw_uq{j}", tm=512, tn=768)
            dqlat, g['g_q_lat'][j] = _rmsnorm_bwd(sv['z'], rep['g_q_lat'][j], dqn, width=Q_RANK, out_dtype=BF16,
                                                 name=f"d_qnorm{j}")
            dz = jnp.concatenate([dqlat, dqm], axis=-1)
            dh = _mm(dz, wm['w_in'], dims='nt', out_dtype=BF16, name=f"d_h_b{j}", tm=tm, tn=1024)
            gm['w_in_b'] = _mm(sv['h'], dz, dims='tn', out_dtype=BF16, name=f"dw_in_b{j}", tm=1024, tn=512)
        tok = emit(('mix', l), gm)
        dx, g['g_mix'][l] = _rmsnorm_bwd(sv['x_in'], rep['g_mix'][l], dh, dres=dx_mid, after=tok, name=f"d_mixnorm{l}")
        if l == N_A:
            dkvx_c, g['g_kv_lat'] = _rmsnorm_bwd(kv['kvx'], rep['g_kv_lat'], dckv_sum, width=KV_RANK, out_dtype=BF16,
                                                 name="d_ckvnorm")
            dk1, dk2 = _rope(dkr_sum[:, :ROPE_HALF], dkr_sum[:, ROPE_HALF:], cos, sin, inverse=True, name="d_krope")
            dkvx = jnp.concatenate([dkvx_c, dk1, dk2, jnp.zeros((s, KV_PAD - KV_RANK - ROPE_DIM), BF16)], axis=-1)
            dxn = _mm(dkvx, kv['w_kv_a'], dims='nt', out_dtype=BF16, name="d_kvnorm_in", tm=tm, tn=1024)
            dw_kv = _mm(kv['xn'], dkvx, dims='tn', out_dtype=BF16, name="dw_kv", tm=1024, tn=KV_PAD)
            tok = emit(('kv', 0), {'w_kv_a': dw_kv})
            dx, g['g_kv'] = _rmsnorm_bwd(kv['x'], rep['g_kv'], dxn, dres=dx, after=tok, name="d_kvnorm")
    return jnp.sum(sq), dx, g


MESH_IDS = pl.DeviceIdType.MESH
PEER_MASKS = tuple((k >> 2 & 1, k >> 1 & 1, k & 1) for k in range(1, N_DEV))
CHIP_MASKS = ((1, 0), (0, 1), (1, 1))
N_PEER = N_DEV - 1
SEMS_PER_BUFFER = 2 * N_PEER + 1
DATAFLOW = pltpu.SideEffectType.DATAFLOW_SIDE_EFFECTING
HBM_SPEC = pl.BlockSpec(memory_space=pltpu.HBM)
SEM_SPEC = pl.BlockSpec(memory_space=pltpu.SEMAPHORE)


def _my_position():
    return lax.axis_index("x"), lax.axis_index("y"), lax.axis_index("c")


def _flip(pos, mask):
    return tuple(1 - p if f else p for p, f in zip(pos, mask))


def _linear_id(pos):
    return 4 * pos[0] + 2 * pos[1] + pos[2]


def _hbm(x):
    return pltpu.with_memory_space_constraint(x, pltpu.HBM)


def _buffer_copies(src_ref, lead, land_ref, sems, scatter, near=False):
    me = _my_position()
    my_id = _linear_id(me)
    src = src_ref.at[lead] if lead else src_ref
    own = pltpu.make_async_copy(src.at[my_id] if scatter else src, land_ref.at[my_id], sems.at[2 * N_PEER])
    pairs = []
    for k, mask in enumerate(PEER_MASKS):
        if near and mask[2] == 1 and mask != (0, 0, 1):
            continue
        peer = _flip(me, mask)
        peer_id = _linear_id(peer)
        block = src.at[peer_id] if scatter else src
        send = pltpu.make_async_remote_copy(src_ref=block, dst_ref=land_ref.at[my_id], send_sem=sems.at[k],
                                            recv_sem=sems.at[N_PEER + k], device_id=peer, device_id_type=MESH_IDS)
        arrival = pltpu.make_async_remote_copy(src_ref=block, dst_ref=land_ref.at[peer_id], send_sem=sems.at[k],
                                               recv_sem=sems.at[N_PEER + k], device_id=peer, device_id_type=MESH_IDS)
        pairs.append((send, arrival))
    return own, pairs


def _exchange_start(srcs, buffers, *, name, scatter):
    ns, nb = len(srcs), len(buffers)
    lands = [_hbm(lax.empty((N_DEV,) + tuple(shape), dtype)) for _, _, shape, dtype, _ in buffers]

    def body(*refs):
        src_refs, land_refs = refs[:ns], refs[ns:ns + nb]
        sem_refs = refs[ns + nb:ns + 2 * nb]
        token = refs[-1]
        for b, (si, lead, _, _, near) in enumerate(buffers):
            own, pairs = _buffer_copies(src_refs[si], lead, land_refs[b], sem_refs[b], scatter, near)
            own.start()
            for send, _ in pairs:
                send.start()
        token[...] = jnp.zeros_like(token)

    out_shape = ([pltpu.SemaphoreType.DMA((SEMS_PER_BUFFER,))] * nb
                 + [pltpu.HBM(a.shape, a.dtype) for a in srcs]
                 + [pltpu.HBM(a.shape, a.dtype) for a in lands]
                 + [jax.ShapeDtypeStruct((8, 128), F32)])
    aliases = {i: nb + i for i in range(ns + nb)}
    outs = pl.pallas_call(
        body, name=name, in_specs=[HBM_SPEC] * (ns + nb),
        out_specs=[SEM_SPEC] * nb + [HBM_SPEC] * (ns + nb) + [pl.BlockSpec(memory_space=pltpu.VMEM)],
        out_shape=out_shape, input_output_aliases=aliases,
        compiler_params=pltpu.CompilerParams(has_side_effects=DATAFLOW),
    )(*[_hbm(a) for a in srcs], *lands)
    sems = list(outs[:nb])
    src_thru = list(outs[nb:nb + ns])
    land_thru = list(outs[nb + ns:nb + ns + nb])
    return sems, land_thru, src_thru, outs[-1]


def _exchange_wait(srcs_thru, buffers, sems, lands, after, *, name, scatter):
    ns, nb = len(srcs_thru), len(buffers)
    has_after = after is not None

    def body(*refs):
        src_refs, land_refs = refs[:ns], refs[ns:ns + nb]
        sem_refs = refs[ns + nb:ns + 2 * nb]
        for b, (si, lead, _, _, near) in enumerate(buffers):
            own, pairs = _buffer_copies(src_refs[si], lead, land_refs[b], sem_refs[b], scatter, near)
            for send, arrival in pairs:
                send.wait_send()
                arrival.wait_recv()
            own.wait()

    operands = list(srcs_thru) + list(lands) + list(sems) + ([after] if has_after else [])
    in_specs = ([HBM_SPEC] * (ns + nb) + [SEM_SPEC] * nb + ([pl.BlockSpec(memory_space=pl.ANY)] if has_after else []))
    outs = pl.pallas_call(
        body, name=name, in_specs=in_specs, out_specs=[HBM_SPEC] * nb,
        out_shape=[pltpu.HBM(a.shape, a.dtype) for a in lands],
        input_output_aliases={ns + b: b for b in range(nb)},
        compiler_params=pltpu.CompilerParams(has_side_effects=DATAFLOW),
    )(*operands)
    return list(outs)


def _exchange(arrays, *, name, scatter, near=None, after=None):
    n = len(arrays)
    near = [False] * n if near is None else near
    extra = [] if after is None else [after]
    out_shapes = [jax.ShapeDtypeStruct(a.shape if scatter else (N_DEV,) + a.shape, a.dtype) for a in arrays]

    def body(*refs):
        srcs, outs, sems = refs[:n], refs[n + len(extra):2 * n + len(extra)], refs[2 * n + len(extra):]
        started = []
        for a in range(n):
            own, pairs = _buffer_copies(srcs[a], (), outs[a], sems[a], scatter, near[a])
            own.start()
            for send, _ in pairs:
                send.start()
            started.append((own, pairs))
        for own, pairs in started:
            for send, arrival in pairs:
                arrival.wait_recv()
                send.wait_send()
            own.wait()

    any_spec = pl.BlockSpec(memory_space=pl.ANY)
    outs = pl.pallas_call(
        body, name=name, in_specs=[any_spec] * (n + len(extra)), out_specs=[any_spec] * n, out_shape=out_shapes,
        scratch_shapes=[pltpu.SemaphoreType.DMA((SEMS_PER_BUFFER,))] * n,
    )(*arrays, *extra)
    return list(outs)


def _forward_to_sibling(lands, *, name):
    n = len(lands)

    def body(*refs):
        ins, outs, sems = refs[:n], refs[n:2 * n], refs[2 * n:]
        me = _my_position()
        sibling = _flip(me, (0, 0, 1))
        pairs = []
        for b in range(n):
            for k, (fx, fy) in enumerate(CHIP_MASKS):
                mine = _linear_id(_flip(me, (fx, fy, 0)))
                theirs = _linear_id(_flip(me, (fx, fy, 1)))
                send = pltpu.make_async_remote_copy(
                    src_ref=ins[b].at[mine], dst_ref=outs[b].at[mine], send_sem=sems[b].at[k],
                    recv_sem=sems[b].at[len(CHIP_MASKS) + k], device_id=sibling, device_id_type=MESH_IDS)
                arrival = pltpu.make_async_remote_copy(
                    src_ref=ins[b].at[mine], dst_ref=outs[b].at[theirs], send_sem=sems[b].at[k],
                    recv_sem=sems[b].at[len(CHIP_MASKS) + k], device_id=sibling, device_id_type=MESH_IDS)
                send.start()
                pairs.append((send, arrival))
        for send, arrival in pairs:
            arrival.wait_recv()
            send.wait_send()

    any_spec = pl.BlockSpec(memory_space=pl.ANY)
    outs = pl.pallas_call(
        body, name=name, in_specs=[any_spec] * n, out_specs=[any_spec] * n,
        out_shape=[jax.ShapeDtypeStruct(a.shape, a.dtype) for a in lands],
        input_output_aliases={b: b for b in range(n)},
        scratch_shapes=[pltpu.SemaphoreType.DMA((2 * len(CHIP_MASKS),))] * n,
    )(*lands)
    return list(outs)


def _sum_slots(parts_ref):
    total = parts_ref[0].astype(F32)
    for q in range(1, parts_ref.shape[0]):
        total = total + parts_ref[q].astype(F32)
    return total


def _row_tile(rows, cols, n_arrays):
    budget = (12 * 1024 * 1024) // (4 * n_arrays * max(cols, 128))
    t = rows
    while t > budget and t % 2 == 0 and (t // 2) % 16 == 0:
        t //= 2
    return t


def _sum_adam(parts, w, m, v, layer, outs, *, name):
    q, r, c = parts.shape
    nl = w.shape[0]
    tr = _row_tile(r, c, q + 7)
    c1 = 1.0 - ADAM_B1 ** ADAM_STEP
    c2 = 1.0 - ADAM_B2 ** ADAM_STEP
    if outs is None:
        outs = [lax.empty((nl, r, c), F32) for _ in range(4)]

    def body(p_ref, w_ref, m_ref, v_ref, g_in, d_in, mo_in, vo_in, g_ref, d_ref, mo_ref, vo_ref):
        grad = _sum_slots(p_ref)
        m_new = ADAM_B1 * m_ref[...] + (1.0 - ADAM_B1) * grad
        v_new = ADAM_B2 * v_ref[...] + (1.0 - ADAM_B2) * (grad * grad)
        m_hat = m_new / c1
        v_hat = v_new / c2
        g_ref[...] = grad
        d_ref[...] = -ADAM_LR * (m_hat / (jnp.sqrt(v_hat) + ADAM_EPS) + ADAM_WD * w_ref[...])
        mo_ref[...] = m_new
        vo_ref[...] = v_new

    tile = pl.BlockSpec((None, tr, c), lambda i: (layer, i, 0))
    any_spec = pl.BlockSpec(memory_space=pl.ANY)
    return pl.pallas_call(
        body, name=name, grid=(r // tr,),
        in_specs=[pl.BlockSpec((q, tr, c), lambda i: (0, i, 0)), tile, tile, tile] + [any_spec] * 4,
        out_specs=[tile] * 4, out_shape=[jax.ShapeDtypeStruct((nl, r, c), F32)] * 4,
        input_output_aliases={4: 0, 5: 1, 6: 2, 7: 3},
        compiler_params=_params(("parallel",)),
    )(parts, w, m, v, *outs)


def _sum_parts(parts, *, name):
    q, r, c = parts.shape

    def body(p_ref, o_ref):
        o_ref[...] = _sum_slots(p_ref)

    return pl.pallas_call(
        body, name=name, in_specs=[pl.BlockSpec((q, r, c), lambda: (0, 0, 0))],
        out_specs=pl.BlockSpec((r, c), lambda: (0, 0)), out_shape=jax.ShapeDtypeStruct((r, c), F32),
        compiler_params=_params(),
    )(parts)


INPUT_NAMES = (['x', 'mem', 'positions'] + WEIGHTS + ['loss_target'] + ['m_' + n for n in WEIGHTS]
               + ['v_' + n for n in WEIGHTS])
SMALL_ALIGN = N_DEV * 8 * 128
TWO_LEVEL_LAYERS = N_A
GROUP_ORDER = ('in', 'rest', 'up', 'down')
GROUP_WEIGHTS = {'in': (['w_in_a'], ['w_in_b', 'w_uq', 'w_uk', 'w_uv']), 'rest': (['w_mem_kv', 'w_out'],) * 2,
                 'up': (['w_ffn_up'],) * 2, 'down': (['w_ffn_down'],) * 2}
LAYERED = {'w_in_a': 0, 'w_in_b': N_A, 'w_uq': N_A, 'w_uk': N_A, 'w_uv': N_A, 'w_mem_kv': 0, 'w_out': 0,
           'w_ffn_up': 0, 'w_ffn_down': 0}


def _permute_uq(w_uq):
    r = w_uq.shape[0]
    q = w_uq.reshape(r, G_HEADS, HEAD + ROPE_DIM)
    return jnp.concatenate([q[..., :HEAD].reshape(r, -1), q[..., HEAD:HEAD + ROPE_HALF].reshape(r, -1),
                            q[..., HEAD + ROPE_HALF:].reshape(r, -1)], axis=-1)


def _unpermute_uq(w_uqp):
    r = w_uqp.shape[0]
    nope = w_uqp[..., :G_W].reshape(r, G_HEADS, HEAD)
    r1 = w_uqp[..., G_W:G_W + G_HEADS * ROPE_HALF].reshape(r, G_HEADS, ROPE_HALF)
    r2 = w_uqp[..., G_W + G_HEADS * ROPE_HALF:].reshape(r, G_HEADS, ROPE_HALF)
    return jnp.concatenate([nope, r1, r2], axis=-1).reshape(r, -1)


def _cols_from_stack(st):
    _, r, n = st.shape
    return st.transpose(1, 0, 2).reshape(r, N_DEV * n)


def _cols_to_stack(wh):
    r, c = wh.shape
    return wh.reshape(r, N_DEV, c // N_DEV).transpose(1, 0, 2)


def _group_weights(group):
    kind, l = group
    return GROUP_WEIGHTS[kind][0 if l < N_A else 1]


def _step(args):
    p = dict(zip(INPUT_NAMES, args))
    x, mem, positions, target = p['x'][0], p['mem'][0], p['positions'][0], p['loss_target'][0]
    d = x.shape[-1]
    my_id = _linear_id(_my_position())

    w_kv_pad = jnp.pad(p['w_kv_a'], ((0, 0), (0, KV_PAD - p['w_kv_a'].shape[1])))
    shard = {k: p[k].astype(BF16) for k in LAYERED}
    shard['w_uk'] = shard['w_uk'].reshape(shard['w_uk'].shape[0], shard['w_uk'].shape[1], -1)
    shard['w_uv'] = shard['w_uv'].reshape(shard['w_uv'].shape[0], shard['w_uv'].shape[1], -1)
    shard.update(conv_w=p['conv_w'], g_v=p['g_v'], w_kv_a=w_kv_pad.astype(BF16))
    src_names = list(shard)
    gather_groups = []
    for l in range(DEPTH):
        gather_groups += [(kind, l) for kind in GROUP_ORDER]
    buffers, owner = [], []
    for group in gather_groups:
        kind, l = group
        for k in _group_weights(group):
            buffers.append((src_names.index(k), (l - LAYERED[k],), shard[k].shape[1:], shard[k].dtype, l < TWO_LEVEL_LAYERS))
            owner.append((group, k))
        if group == ('in', 0):
            for k in ('g_v', 'conv_w'):
                buffers.append((src_names.index(k), (), shard[k].shape, shard[k].dtype, True))
                owner.append((group, k))
        if group == ('in', N_A):
            buffers.append((src_names.index('w_kv_a'), (), shard['w_kv_a'].shape, BF16, False))
            owner.append((group, 'w_kv_a'))
    g_sems, g_lands, g_srcs, _ = _exchange_start([shard[k] for k in src_names], buffers, name="gather_start",
                                                 scatter=False)

    def fetch(group, after):
        idx = [i for i, (grp, _) in enumerate(owner) if grp == group]
        landed = _exchange_wait(g_srcs, [buffers[i] for i in idx], [g_sems[i] for i in idx],
                                [g_lands[i] for i in idx], after, name=f"gather_wait_{group[0]}{group[1]}",
                                scatter=False)
        if group[1] < TWO_LEVEL_LAYERS:
            landed = _forward_to_sibling(landed, name=f"gather_forward_{group[0]}{group[1]}")
        got = {owner[i][1]: t for i, t in zip(idx, landed)}
        out = {}
        for k, t in got.items():
            if k in ('w_in_a', 'w_uq'):
                out[k] = _cols_from_stack(t)
            elif k == 'g_v':
                out[k] = t.transpose(1, 0, 2).reshape(t.shape[1], -1)
            elif k in ('w_ffn_up', 'conv_w'):
                out[k] = t
            else:
                out[k] = t.reshape(-1, t.shape[-1])
        if 'w_uq' in out:
            out['w_uqp'] = _permute_uq(out.pop('w_uq'))
        for old, new in (('w_in_a', 'w_in'), ('w_in_b', 'w_in'), ('w_ffn_up', 'w_up'), ('w_ffn_down', 'w_down')):
            if old in out:
                out[new] = out.pop(old)
        return out

    pending = []

    def emit(group, grads):
        send = {}
        for k, t in grads.items():
            if k == 'w_in_a':
                send[k] = _cols_to_stack(t)
            elif k == 'w_uqp':
                send['w_uq'] = _cols_to_stack(_unpermute_uq(t))
            elif k == 'w_ffn_up':
                send[k] = t
            elif k == 'w_kv_a':
                cols = p['w_kv_a'].shape[1]
                send[k] = t[:, :cols].reshape(N_DEV, -1, cols)
            else:
                send[k] = t.reshape(N_DEV, t.shape[0] // N_DEV, t.shape[1])
        keys = list(send)
        bufs = [(i, (), send[k].shape[1:], send[k].dtype, False) for i, k in enumerate(keys)]
        sems, lands, srcs, token = _exchange_start([send[k] for k in keys], bufs,
                                                   name=f"scatter_start_{group[0]}{group[1]}", scatter=True)
        pending.append((group, keys, bufs, sems, lands, srcs))
        return token

    rep = {k: p[k] for k in REPLICATED}
    sq, grad_x, g = _local_step(x, mem, positions, target, rep, fetch, emit)
    loss = (0.5 / d) * lax.psum(sq, ("x", "y", "c"))

    out, running = {}, {}
    order = grad_x
    for group, keys, bufs, sems, lands, srcs in pending:
        landed = _exchange_wait(srcs, bufs, sems, lands, order, name=f"scatter_wait_{group[0]}{group[1]}", scatter=True)
        for k, parts in zip(keys, landed):
            stacked = k in LAYERED
            nl = p[k].shape[0] if stacked else 1
            layer = group[1] - LAYERED[k] if stacked else 0
            rows = p[k].size // nl // p[k].shape[-1]
            view = (nl, rows, p[k].shape[-1])
            running[k] = _sum_adam(parts.reshape(N_DEV, rows, view[2]), p[k].reshape(view), p['m_' + k].reshape(view),
                                   p['v_' + k].reshape(view), layer, running.get(k), name=f"adam_{k}{layer}")
            order = running[k][1]
    for k, res in running.items():
        out[k] = [t.reshape(p[k].shape) for t in res]

    small = {
        'g_mix': jnp.concatenate(g['g_mix']), 'g_ffn': jnp.concatenate(g['g_ffn']), 'g_final': g['g_final'],
        'w_sp': jnp.stack(g['w_sp']), 'b_sp': jnp.stack(g['b_sp']), 'g_kv': g['g_kv'], 'g_kv_lat': g['g_kv_lat'],
        'g_q_lat': jnp.concatenate(g['g_q_lat']), 'g_mem': jnp.concatenate(g['g_mem']),
        'conv_b': jnp.stack(g['conv_b']),
        'g_v': jnp.concatenate(g['g_v']),
        'conv_w': jnp.stack(g['conv_w']).transpose(0, 2, 1, 3),
    }
    small_names = REPLICATED + SMALL_SHARDED
    flat = jnp.concatenate([small[k].reshape(-1).astype(F32) for k in small_names])
    n_small = flat.shape[0]
    padded = -(-n_small // SMALL_ALIGN) * SMALL_ALIGN
    flat = jnp.pad(flat, (0, padded - n_small)).reshape(N_DEV, -1, 128)
    last_update = out[pending[-1][1][-1]][1]
    (small_parts,) = _exchange([flat], name="scatter_small", scatter=True, after=last_update)
    reduced = _sum_parts(small_parts, name="sum_small")
    (small_all,) = _exchange([reduced], name="gather_small", scatter=False)
    small_all = small_all.reshape(-1)
    grads_small, off = {}, 0
    for k in small_names:
        size = small[k].size
        grads_small[k] = small_all[off:off + size].reshape(small[k].shape)
        off += size
    grads_small['g_v'] = lax.dynamic_slice_in_dim(grads_small['g_v'], my_id * p['g_v'].shape[1], p['g_v'].shape[1], axis=1)
    grads_small['conv_w'] = lax.dynamic_index_in_dim(grads_small['conv_w'], my_id, axis=2, keepdims=False)
    gs = jnp.concatenate([grads_small[k].reshape(-1) for k in small_names])
    n_loc = gs.shape[0]
    pad_loc = -(-n_loc // 1024) * 1024 - n_loc

    def pack(prefix):
        t = jnp.concatenate([p[prefix + k].reshape(-1) for k in small_names])
        return jnp.pad(t, (0, pad_loc)).reshape(1, -1, 128)

    res = _sum_adam(jnp.pad(gs, (0, pad_loc)).reshape(1, -1, 128), pack(''), pack('m_'), pack('v_'), 0, None,
                    name="adam_small")
    off = 0
    for k in small_names:
        size = p[k].size
        out[k] = [t.reshape(-1)[off:off + size].reshape(p[k].shape) for t in res]
        off += size

    outs = [loss, grad_x[None]]
    for i in range(4):
        outs += [out[k][i] for k in WEIGHTS]
    return tuple(outs)


def kernel(x, mem, positions, g_mix, g_ffn, g_final, w_in_a, g_v, w_sp, b_sp, g_kv, w_kv_a, g_kv_lat, w_in_b, g_q_lat, w_uq, w_uk, w_uv, g_mem, w_mem_kv, w_out, w_ffn_up, conv_w, conv_b, w_ffn_down, loss_target, m_g_mix, m_g_ffn, m_g_final, m_w_in_a, m_g_v, m_w_sp, m_b_sp, m_g_kv, m_w_kv_a, m_g_kv_lat, m_w_in_b, m_g_q_lat, m_w_uq, m_w_uk, m_w_uv, m_g_mem, m_w_mem_kv, m_w_out, m_w_ffn_up, m_conv_w, m_conv_b, m_w_ffn_down, v_g_mix, v_g_ffn, v_g_final, v_w_in_a, v_g_v, v_w_sp, v_b_sp, v_g_kv, v_w_kv_a, v_g_kv_lat, v_w_in_b, v_g_q_lat, v_w_uq, v_w_uk, v_w_uv, v_g_mem, v_w_mem_kv, v_w_out, v_w_ffn_up, v_conv_w, v_conv_b, v_w_ffn_down):
    return _step((x, mem, positions, g_mix, g_ffn, g_final, w_in_a, g_v, w_sp, b_sp, g_kv, w_kv_a, g_kv_lat, w_in_b, g_q_lat, w_uq, w_uk, w_uv, g_mem, w_mem_kv, w_out, w_ffn_up, conv_w, conv_b, w_ffn_down, loss_target, m_g_mix, m_g_ffn, m_g_final, m_w_in_a, m_g_v, m_w_sp, m_b_sp, m_g_kv, m_w_kv_a, m_g_kv_lat, m_w_in_b, m_g_q_lat, m_w_uq, m_w_uk, m_w_uv, m_g_mem, m_w_mem_kv, m_w_out, m_w_ffn_up, m_conv_w, m_conv_b, m_w_ffn_down, v_g_mix, v_g_ffn, v_g_final, v_w_in_a, v_g_v, v_w_sp, v_b_sp, v_g_kv, v_w_kv_a, v_g_kv_lat, v_w_in_b, v_g_q_lat, v_w_uq, v_w_uk, v_w_uv, v_g_mem, v_w_mem_kv, v_w_out, v_w_ffn_up, v_conv_w, v_conv_b, v_w_ffn_down))
```

```python
import math

import jax
import jax.numpy as jnp
from jax import lax
from jax.experimental import pallas as pl
from jax.experimental.pallas import tpu as pltpu

F32 = jnp.float32
BF16 = jnp.bfloat16

N_DEV = 8
N_A = 2
DEPTH = 4
G_HEADS = 12
HEAD = 128
CHUNK = 128
MEM_HEADS = 4
MEM_W = MEM_HEADS * HEAD
G_W = G_HEADS * HEAD
ROPE_DIM = 64
ROPE_HALF = ROPE_DIM // 2
KV_RANK = 512
Q_RANK = 512
KV_PAD = 640
ROPE_THETA = 10000.0
EPS = 1e-6
CONV_W = 3

ADAM_LR = 0.001
ADAM_B1 = 0.9
ADAM_B2 = 0.999
ADAM_EPS = 1e-08
ADAM_WD = 0.01
ADAM_STEP = 10

VMEM_LIMIT_V7X = 56 * 1024 * 1024
MASK_VALUE = -1e30

WEIGHTS = ['g_mix', 'g_ffn', 'g_final', 'w_in_a', 'g_v', 'w_sp', 'b_sp', 'g_kv', 'w_kv_a', 'g_kv_lat',
           'w_in_b', 'g_q_lat', 'w_uq', 'w_uk', 'w_uv', 'g_mem', 'w_mem_kv', 'w_out', 'w_ffn_up',
           'conv_w', 'conv_b', 'w_ffn_down']
REPLICATED = ['g_mix', 'g_ffn', 'g_final', 'w_sp', 'b_sp', 'g_kv', 'g_kv_lat', 'g_q_lat', 'g_mem', 'conv_b']
SMALL_SHARDED = ['g_v', 'conv_w']


def _params(sem=None):
    return pltpu.CompilerParams(dimension_semantics=sem, vmem_limit_bytes=VMEM_LIMIT_V7X)


def _dot(a, b, dims):
    contract = {'nn': ((1,), (0,)), 'nt': ((1,), (1,)), 'tn': ((0,), (0,))}[dims]
    return lax.dot_general(a, b, (contract, ((), ())), preferred_element_type=F32)


def _erf(x):
    return lax.erf(x)


def _gelu(x):
    return 0.5 * x * (1.0 + _erf(x * (2.0 ** -0.5)))


def _gelu_grad(x):
    cdf = 0.5 * (1.0 + _erf(x * (2.0 ** -0.5)))
    pdf = jnp.exp(-0.5 * x * x) * (1.0 / math.sqrt(2.0 * math.pi))
    return cdf + x * pdf


def _sigmoid(x):
    return 1.0 / (1.0 + jnp.exp(-x))


def _operand_spec(shape, lead, blocked, tr, tc, ridx, cidx):
    if blocked:
        per = shape[-1] // tc
        assert shape[-1] % tc == 0, (shape, tc)
        return pl.BlockSpec(
            (None,) * (1 + len(lead)) + (tr, tc),
            lambda *g: (cidx(*g) // per,) + lead + (ridx(*g), cidx(*g) % per))
    return pl.BlockSpec((None,) * len(lead) + (tr, tc), lambda *g: lead + (ridx(*g), cidx(*g)))


def _view2d(x, blocked):
    return (x.shape[-2], x.shape[0] * x.shape[-1]) if blocked else (x.shape[-2], x.shape[-1])


def _mm(a, b, *, dims, out_dtype, name, tm, tn, tk=None, res=None, a_lead=(), b_lead=(),
        a_blocked=False, b_blocked=False, out_block=None, n_outer=False, after=None):
    ar, ac = _view2d(a, a_blocked)
    br, bc = _view2d(b, b_blocked)
    m, k = (ac, ar) if dims == 'tn' else (ar, ac)
    n, k2 = (br, bc) if dims == 'nt' else (bc, br)
    assert k == k2, (a.shape, b.shape, dims)
    tm, tn = min(tm, m), min(tn, n)
    tk = k if tk is None else tk
    assert m % tm == 0 and n % tn == 0 and k % tk == 0, (name, m, n, k, tm, tn, tk)
    nk = k // tk
    if n_outer:
        gi, gj = (lambda g0, g1, g2: g1), (lambda g0, g1, g2: g0)
        grid = (n // tn, m // tm, nk)
    else:
        gi, gj = (lambda g0, g1, g2: g0), (lambda g0, g1, g2: g1)
        grid = (m // tm, n // tn, nk)
    gk = lambda g0, g1, g2: g2

    if dims == 'tn':
        a_spec = _operand_spec(a.shape, a_lead, a_blocked, tk, tm, gk, gi)
    else:
        a_spec = _operand_spec(a.shape, a_lead, a_blocked, tm, tk, gi, gk)
    if dims == 'nt':
        b_spec = _operand_spec(b.shape, b_lead, b_blocked, tn, tk, gj, gk)
    else:
        b_spec = _operand_spec(b.shape, b_lead, b_blocked, tk, tn, gk, gj)
    in_specs = [a_spec, b_spec]
    operands = [a, b]
    if res is not None:
        in_specs.append(pl.BlockSpec((tm, tn), lambda *g: (gi(*g), gj(*g))))
        operands.append(res)
    if after is not None:
        in_specs.append(pl.BlockSpec(memory_space=pl.ANY))
        operands.append(after)
    n_in = len(operands)
    if out_block is not None:
        out_shape = jax.ShapeDtypeStruct((n // out_block, m, out_block), out_dtype)
        out_spec = _operand_spec(out_shape.shape, (), True, tm, tn, gi, gj)
    else:
        out_shape = jax.ShapeDtypeStruct((m, n), out_dtype)
        out_spec = pl.BlockSpec((tm, tn), lambda *g: (gi(*g), gj(*g)))

    def body(*refs):
        a_ref, b_ref = refs[0], refs[1]
        r_ref = refs[2] if res is not None else None
        o_ref = refs[n_in]
        acc_ref = refs[-1] if nk > 1 else None
        part = _dot(a_ref[...].astype(BF16), b_ref[...].astype(BF16), dims)

        def finish(total):
            if r_ref is not None:
                total = total + r_ref[...]
            o_ref[...] = total.astype(o_ref.dtype)

        if nk == 1:
            finish(part)
        else:
            kk = pl.program_id(2)

            @pl.when(kk == 0)
            def _():
                acc_ref[...] = part

            @pl.when(kk > 0)
            def _():
                acc_ref[...] += part

            @pl.when(kk == nk - 1)
            def _():
                finish(acc_ref[...])

    scratch = [pltpu.VMEM((tm, tn), F32)] if nk > 1 else []
    return pl.pallas_call(
        body, name=name, grid=grid, in_specs=in_specs, out_specs=out_spec,
        out_shape=out_shape, scratch_shapes=scratch,
        compiler_params=_params(("parallel", "parallel", "arbitrary")),
    )(*operands)


def _mm_blocked_nt(a, b, *, out_dtype, name, tm, tn, blocks_per_step, after=None):
    nb, m, bw = a.shape
    n = b.shape[1]
    tm, tn = min(tm, m), min(tn, n)
    assert nb % blocks_per_step == 0 and m % tm == 0 and n % tn == 0
    nk = nb // blocks_per_step

    def body(*refs):
        a_ref, b_ref, o_ref, acc_ref = refs[0], refs[1], refs[-2], refs[-1]
        kk = pl.program_id(2)
        part = _dot(a_ref[0], b_ref[0], 'nt')
        for t in range(1, blocks_per_step):
            part = part + _dot(a_ref[t], b_ref[t], 'nt')

        @pl.when(kk == 0)
        def _():
            acc_ref[...] = part

        @pl.when(kk > 0)
        def _():
            acc_ref[...] += part

        @pl.when(kk == nk - 1)
        def _():
            o_ref[...] = acc_ref[...].astype(o_ref.dtype)

    in_specs = [pl.BlockSpec((blocks_per_step, tm, bw), lambda i, j, k: (k, i, 0)),
                pl.BlockSpec((blocks_per_step, tn, bw), lambda i, j, k: (k, j, 0))]
    operands = [a, b]
    if after is not None:
        in_specs.append(pl.BlockSpec(memory_space=pl.ANY))
        operands.append(after)
    return pl.pallas_call(
        body, name=name, grid=(m // tm, n // tn, nk), in_specs=in_specs,
        out_specs=pl.BlockSpec((tm, tn), lambda i, j, k: (i, j)), out_shape=jax.ShapeDtypeStruct((m, n), out_dtype),
        scratch_shapes=[pltpu.VMEM((tm, tn), F32)],
        compiler_params=_params(("parallel", "parallel", "arbitrary")),
    )(*operands)


def _rmsnorm(x, g, *, name, width=None, out_dtype=BF16, tm=512):
    s = x.shape[0]
    w = x.shape[1] if width is None else width
    tm = min(tm, s)

    def body(x_ref, g_ref, o_ref):
        xv = x_ref[...].astype(F32)
        rstd = lax.rsqrt(jnp.mean(xv * xv, axis=-1, keepdims=True) + EPS)
        o_ref[...] = (xv * rstd * g_ref[...]).astype(o_ref.dtype)

    return pl.pallas_call(
        body, name=name, grid=(s // tm,),
        in_specs=[pl.BlockSpec((tm, w), lambda i: (i, 0)), pl.BlockSpec((1, w), lambda i: (0, 0))],
        out_specs=pl.BlockSpec((tm, w), lambda i: (i, 0)),
        out_shape=jax.ShapeDtypeStruct((s, w), out_dtype),
        compiler_params=_params(("parallel",)),
    )(x, g.reshape(1, w))


def _rmsnorm_bwd(x, g, dy, *, name, width=None, dres=None, after=None, out_dtype=F32, tm=512):
    s = x.shape[0]
    w = x.shape[1] if width is None else width
    tm = min(tm, s)

    def body(*refs):
        x_ref, g_ref, dy_ref = refs[0], refs[1], refs[2]
        r_ref = refs[3] if dres is not None else None
        dx_ref, dg_ref = refs[-2], refs[-1]
        xv = x_ref[...].astype(F32)
        rstd = lax.rsqrt(jnp.mean(xv * xv, axis=-1, keepdims=True) + EPS)
        xhat = xv * rstd
        dyv = dy_ref[...].astype(F32)
        gdy = dyv * g_ref[...]
        dx = rstd * (gdy - xhat * jnp.mean(gdy * xhat, axis=-1, keepdims=True))
        if r_ref is not None:
            dx = dx + r_ref[...]
        dx_ref[...] = dx.astype(dx_ref.dtype)
        part = jnp.sum(dyv * xhat, axis=0, keepdims=True)

        @pl.when(pl.program_id(0) == 0)
        def _():
            dg_ref[...] = part

        @pl.when(pl.program_id(0) > 0)
        def _():
            dg_ref[...] += part

    row = pl.BlockSpec((tm, w), lambda i: (i, 0))
    vec = pl.BlockSpec((1, w), lambda i: (0, 0))
    in_specs = [row, vec, row] + ([row] if dres is not None else [])
    operands = [x, g.reshape(1, w), dy] + ([dres] if dres is not None else [])
    if after is not None:
        in_specs.append(pl.BlockSpec(memory_space=pl.ANY))
        operands.append(after)
    return pl.pallas_call(
        body, name=name, grid=(s // tm,), in_specs=in_specs, out_specs=[row, vec],
        out_shape=[jax.ShapeDtypeStruct((s, w), out_dtype), jax.ShapeDtypeStruct((1, w), F32)],
        compiler_params=_params(("arbitrary",)),
    )(*operands)


def _final_loss(x, target, g, *, name, tm=256):
    s, d = x.shape
    tm = min(tm, s)

    def body(x_ref, t_ref, g_ref, sq_ref, dx_ref, dg_ref):
        xv = x_ref[...]
        rstd = lax.rsqrt(jnp.mean(xv * xv, axis=-1, keepdims=True) + EPS)
        xhat = xv * rstd
        err = xhat * g_ref[...] - t_ref[...]
        dyv = err * (1.0 / d)
        gdy = dyv * g_ref[...]
        dx_ref[...] = rstd * (gdy - xhat * jnp.mean(gdy * xhat, axis=-1, keepdims=True))
        sq = jnp.sum(err * err, axis=0, keepdims=True)
        dg = jnp.sum(dyv * xhat, axis=0, keepdims=True)

        @pl.when(pl.program_id(0) == 0)
        def _():
            sq_ref[...] = sq
            dg_ref[...] = dg

        @pl.when(pl.program_id(0) > 0)
        def _():
            sq_ref[...] += sq
            dg_ref[...] += dg

    row = pl.BlockSpec((tm, d), lambda i: (i, 0))
    vec = pl.BlockSpec((1, d), lambda i: (0, 0))
    return pl.pallas_call(
        body, name=name, grid=(s // tm,), in_specs=[row, row, vec], out_specs=[vec, row, vec],
        out_shape=[jax.ShapeDtypeStruct((1, d), F32), jax.ShapeDtypeStruct((s, d), F32),
                   jax.ShapeDtypeStruct((1, d), F32)],
        compiler_params=_params(("arbitrary",)),
    )(x, target, g.reshape(1, d))


def _tril_mask():
    t = lax.broadcasted_iota(jnp.int32, (CHUNK, CHUNK), 0)
    s = lax.broadcasted_iota(jnp.int32, (CHUNK, CHUNK), 1)
    return t >= s


def _sgu_fwd(z, g_v, w_sp, b_sp_t, *, name):
    s = z.shape[0]

    def body(zu_ref, zv_ref, g_ref, w_ref, b_ref, o_ref):
        u = _gelu(zu_ref[...].astype(F32))
        gv = _gelu(zv_ref[...].astype(F32))
        rstd = lax.rsqrt(jnp.mean(gv * gv, axis=-1, keepdims=True) + EPS)
        v = (gv * rstd * g_ref[...]).astype(BF16)
        mask = _tril_mask()
        for grp in range(G_HEADS):
            cols = slice(grp * HEAD, (grp + 1) * HEAD)
            wm = jnp.where(mask, w_ref[grp], 0.0).astype(BF16)
            sv = _dot(wm, v[:, cols], 'nn') + b_ref[:, grp:grp + 1]
            o_ref[:, cols] = (u[:, cols] * sv).astype(o_ref.dtype)

    return pl.pallas_call(
        body, name=name, grid=(s // CHUNK,),
        in_specs=[pl.BlockSpec((CHUNK, G_W), lambda i: (i, 0)),
                  pl.BlockSpec((CHUNK, G_W), lambda i: (i, 1)),
                  pl.BlockSpec((1, G_W), lambda i: (0, 0)),
                  pl.BlockSpec((G_HEADS, CHUNK, CHUNK), lambda i: (0, 0, 0)),
                  pl.BlockSpec((CHUNK, G_HEADS), lambda i: (0, 0))],
        out_specs=pl.BlockSpec((CHUNK, G_W), lambda i: (i, 0)),
        out_shape=jax.ShapeDtypeStruct((s, G_W), BF16),
        compiler_params=_params(("parallel",)),
    )(z, z, g_v.reshape(1, G_W), w_sp, b_sp_t)


def _sgu_bwd(z, dmix, dqm, g_v, w_sp, b_sp_t, *, name):
    s = z.shape[0]
    zw = z.shape[1]

    def body(zu_ref, zv_ref, dm_ref, dq_ref, g_ref, w_ref, b_ref, dz_ref, dw_ref, db_ref, dg_ref):
        first = pl.program_id(0) == 0

        @pl.when(first)
        def _():
            dw_ref[...] = jnp.zeros_like(dw_ref)
            db_ref[...] = jnp.zeros_like(db_ref)
            dg_ref[...] = jnp.zeros_like(dg_ref)

        zu = zu_ref[...].astype(F32)
        zv = zv_ref[...].astype(F32)
        dmain = dm_ref[...].astype(F32)
        u = _gelu(zu)
        gv = _gelu(zv)
        rstd = lax.rsqrt(jnp.mean(gv * gv, axis=-1, keepdims=True) + EPS)
        vhat = gv * rstd
        gvec = g_ref[...]
        v = (vhat * gvec).astype(BF16)
        dsv = dmain * u
        dsv_b = dsv.astype(BF16)
        mask = _tril_mask()
        dv_parts = []
        for grp in range(G_HEADS):
            cols = slice(grp * HEAD, (grp + 1) * HEAD)
            wm = jnp.where(mask, w_ref[grp], 0.0).astype(BF16)
            sv = _dot(wm, v[:, cols], 'nn') + b_ref[:, grp:grp + 1]
            dz_ref[:, cols] = (dmain[:, cols] * sv * _gelu_grad(zu[:, cols])).astype(dz_ref.dtype)
            dwg = _dot(dsv_b[:, cols], v[:, cols], 'nt')
            dw_ref[grp] += jnp.where(mask, dwg, 0.0)
            db_ref[:, grp:grp + 1] += jnp.sum(dsv[:, cols], axis=-1, keepdims=True)
            dv_parts.append(_dot(wm, dsv_b[:, cols], 'tn'))
        dv = jnp.concatenate(dv_parts, axis=-1)
        dg_ref[...] += jnp.sum(dv * vhat, axis=0, keepdims=True)
        gdv = dv * gvec
        dgv = rstd * (gdv - vhat * jnp.mean(gdv * vhat, axis=-1, keepdims=True))
        dz_ref[:, G_W:2 * G_W] = (dgv * _gelu_grad(zv)).astype(dz_ref.dtype)
        dz_ref[:, 2 * G_W:] = dq_ref[...].astype(dz_ref.dtype)

    return pl.pallas_call(
        body, name=name, grid=(s // CHUNK,),
        in_specs=[pl.BlockSpec((CHUNK, G_W), lambda i: (i, 0)),
                  pl.BlockSpec((CHUNK, G_W), lambda i: (i, 1)),
                  pl.BlockSpec((CHUNK, G_W), lambda i: (i, 0)),
                  pl.BlockSpec((CHUNK, MEM_W), lambda i: (i, 0)),
                  pl.BlockSpec((1, G_W), lambda i: (0, 0)),
                  pl.BlockSpec((G_HEADS, CHUNK, CHUNK), lambda i: (0, 0, 0)),
                  pl.BlockSpec((CHUNK, G_HEADS), lambda i: (0, 0))],
        out_specs=[pl.BlockSpec((CHUNK, zw), lambda i: (i, 0)),
                   pl.BlockSpec((G_HEADS, CHUNK, CHUNK), lambda i: (0, 0, 0)),
                   pl.BlockSpec((CHUNK, G_HEADS), lambda i: (0, 0)),
                   pl.BlockSpec((1, G_W), lambda i: (0, 0))],
        out_shape=[jax.ShapeDtypeStruct((s, zw), BF16),
                   jax.ShapeDtypeStruct((G_HEADS, CHUNK, CHUNK), F32),
                   jax.ShapeDtypeStruct((CHUNK, G_HEADS), F32),
                   jax.ShapeDtypeStruct((1, G_W), F32)],
        compiler_params=_params(("arbitrary",)),
    )(z, z, dmix, dqm, g_v.reshape(1, G_W), w_sp, b_sp_t)


def _mem_probs(q, k):
    sc = _dot(q, k, 'nt') * (HEAD ** -0.5)
    sc = sc - jnp.max(sc, axis=-1, keepdims=True)
    e = jnp.exp(sc)
    return e / jnp.sum(e, axis=-1, keepdims=True)


def _memattn_fwd(z, kvm, main, *, qcol, name, tm=512):
    s = z.shape[0]
    m = kvm.shape[0]
    tm = min(tm, s)

    def body(q_ref, kv_ref, main_ref, o_ref):
        o_ref[:, :G_W] = main_ref[...]
        for h in range(MEM_HEADS):
            cols = slice(h * HEAD, (h + 1) * HEAD)
            k = kv_ref[:, cols]
            v = kv_ref[:, MEM_W + h * HEAD:MEM_W + (h + 1) * HEAD]
            p = _mem_probs(q_ref[:, cols], k)
            o_ref[:, G_W + h * HEAD:G_W + (h + 1) * HEAD] = _dot(p.astype(BF16), v, 'nn').astype(o_ref.dtype)

    return pl.pallas_call(
        body, name=name, grid=(s // tm,),
        in_specs=[pl.BlockSpec((tm, MEM_W), lambda i: (i, qcol)),
                  pl.BlockSpec((m, 2 * MEM_W), lambda i: (0, 0)),
                  pl.BlockSpec((tm, G_W), lambda i: (i, 0))],
        out_specs=pl.BlockSpec((tm, G_W + MEM_W), lambda i: (i, 0)),
        out_shape=jax.ShapeDtypeStruct((s, G_W + MEM_W), BF16),
        compiler_params=_params(("parallel",)),
    )(z, kvm, main)


def _memattn_bwd(z, kvm, dmix, *, qcol, name, tm=512):
    s = z.shape[0]
    m = kvm.shape[0]
    tm = min(tm, s)
    scale = HEAD ** -0.5

    def body(q_ref, kv_ref, do_ref, dq_ref, dkv_ref):
        @pl.when(pl.program_id(0) == 0)
        def _():
            dkv_ref[...] = jnp.zeros_like(dkv_ref)

        for h in range(MEM_HEADS):
            cols = slice(h * HEAD, (h + 1) * HEAD)
            vcols = slice(MEM_W + h * HEAD, MEM_W + (h + 1) * HEAD)
            q = q_ref[:, cols]
            k = kv_ref[:, cols]
            v = kv_ref[:, vcols]
            do = do_ref[:, cols]
            p = _mem_probs(q, k)
            dp = _dot(do, v, 'nt')
            ds = (p * (dp - jnp.sum(dp * p, axis=-1, keepdims=True)) * scale).astype(BF16)
            dq_ref[:, cols] = _dot(ds, k, 'nn').astype(dq_ref.dtype)
            dkv_ref[:, cols] += _dot(ds, q, 'tn')
            dkv_ref[:, vcols] += _dot(p.astype(BF16), do, 'tn')

    mo_block = G_W // MEM_W
    return pl.pallas_call(
        body, name=name, grid=(s // tm,),
        in_specs=[pl.BlockSpec((tm, MEM_W), lambda i: (i, qcol)),
                  pl.BlockSpec((m, 2 * MEM_W), lambda i: (0, 0)),
                  pl.BlockSpec((tm, MEM_W), lambda i: (i, mo_block))],
        out_specs=[pl.BlockSpec((tm, MEM_W), lambda i: (i, 0)),
                   pl.BlockSpec((m, 2 * MEM_W), lambda i: (0, 0))],
        out_shape=[jax.ShapeDtypeStruct((s, MEM_W), BF16), jax.ShapeDtypeStruct((m, 2 * MEM_W), F32)],
        compiler_params=_params(("arbitrary",)),
    )(z, kvm, dmix)


def _rope(x1, x2, cos, sin, *, name, inverse=False, out_dtype=BF16, col1=0, col2=0, tm=512):
    s, w = cos.shape
    tm = min(tm, s)
    sign = -1.0 if inverse else 1.0

    def body(a_ref, b_ref, c_ref, s_ref, o1_ref, o2_ref):
        a = a_ref[...].astype(F32)
        b = b_ref[...].astype(F32)
        c = c_ref[...]
        sn = s_ref[...] * sign
        o1_ref[...] = (a * c - b * sn).astype(o1_ref.dtype)
        o2_ref[...] = (b * c + a * sn).astype(o2_ref.dtype)

    row = pl.BlockSpec((tm, w), lambda i: (i, 0))
    return pl.pallas_call(
        body, name=name, grid=(s // tm,),
        in_specs=[pl.BlockSpec((tm, w), lambda i: (i, col1)), pl.BlockSpec((tm, w), lambda i: (i, col2)), row, row],
        out_specs=[row, row],
        out_shape=[jax.ShapeDtypeStruct((s, w), out_dtype)] * 2,
        compiler_params=_params(("parallel",)),
    )(x1, x2, cos, sin)


MHA_BLOCK = 1024
QK_DIM = HEAD + ROPE_DIM


def _mha_scores(q, k, scale, diagonal):
    sc = _dot(q, k, 'nt') * scale
    if not diagonal:
        return sc, None
    rows = lax.broadcasted_iota(jnp.int32, sc.shape, 0)
    cols = lax.broadcasted_iota(jnp.int32, sc.shape, 1)
    return sc, cols <= rows


def _mha_fwd(q, k, vv, *, name):
    s = k.shape[1]
    tb = min(MHA_BLOCK, s)
    scale = QK_DIM ** -0.5

    def body(q_ref, k_ref, v_ref, o_ref, lse_ref, m_ref, l_ref, acc_ref):
        i = pl.program_id(1)
        qh = q_ref[...]
        m_ref[...] = jnp.full_like(m_ref, MASK_VALUE)
        l_ref[...] = jnp.zeros_like(l_ref)
        acc_ref[...] = jnp.zeros_like(acc_ref)

        def block(j, diagonal):
            ks = pl.multiple_of(j * tb, tb)
            kj, vj = k_ref[pl.ds(ks, tb), :], v_ref[pl.ds(ks, tb), :]
            sc, keep = _mha_scores(qh, kj, scale, diagonal)
            if diagonal:
                sc = jnp.where(keep, sc, MASK_VALUE)
            m_old = m_ref[...]
            m_new = jnp.maximum(m_old, jnp.max(sc, axis=-1, keepdims=True))
            p = jnp.exp(sc - m_new)
            alpha = jnp.exp(m_old - m_new)
            l_ref[...] = alpha * l_ref[...] + jnp.sum(p, axis=-1, keepdims=True)
            acc_ref[...] = alpha * acc_ref[...] + _dot(p.astype(BF16), vj, 'nn')
            m_ref[...] = m_new

        def step(j, carry):
            block(j, False)
            return carry

        lax.fori_loop(0, i, step, 0)
        block(i, True)
        l = l_ref[...]
        o_ref[...] = (acc_ref[...] / l).astype(o_ref.dtype)
        lse_ref[...] = m_ref[...] + jnp.log(l)

    return pl.pallas_call(
        body, name=name, grid=(G_HEADS, s // tb),
        in_specs=[pl.BlockSpec((None, tb, QK_DIM), lambda h, i: (h, i, 0)),
                  pl.BlockSpec((None, s, QK_DIM), lambda h, i: (h, 0, 0)),
                  pl.BlockSpec((s, HEAD), lambda h, i: (0, h))],
        out_specs=[pl.BlockSpec((tb, HEAD), lambda h, i: (i, h)),
                   pl.BlockSpec((None, tb, 1), lambda h, i: (h, i, 0))],
        out_shape=[jax.ShapeDtypeStruct((s, G_W), BF16), jax.ShapeDtypeStruct((G_HEADS, s, 1), F32)],
        scratch_shapes=[pltpu.VMEM((tb, 1), F32), pltpu.VMEM((tb, 1), F32), pltpu.VMEM((tb, HEAD), F32)],
        compiler_params=_params(("parallel", "arbitrary")),
    )(q, k, vv)


def _mha_bwd(q, k, vv, o, do, lse, *, name):
    s = k.shape[1]
    tb = min(MHA_BLOCK, s)
    nq = s // tb
    scale = QK_DIM ** -0.5

    def body(q_ref, k_ref, v_ref, o_ref, do_ref, lse_ref, dq_ref, dk_ref, dv_ref, dqa_ref, dka_ref, dva_ref):
        i = pl.program_id(1)

        @pl.when(i == 0)
        def _():
            dka_ref[...] = jnp.zeros_like(dka_ref)
            dva_ref[...] = jnp.zeros_like(dva_ref)

        qh, dov = q_ref[...], do_ref[...]
        delta = jnp.sum(dov.astype(F32) * o_ref[...].astype(F32), axis=-1, keepdims=True)
        lsev = lse_ref[...]
        dqa_ref[...] = jnp.zeros_like(dqa_ref)

        def block(j, diagonal):
            ks = pl.multiple_of(j * tb, tb)
            kj, vj = k_ref[pl.ds(ks, tb), :], v_ref[pl.ds(ks, tb), :]
            sc, keep = _mha_scores(qh, kj, scale, diagonal)
            p = jnp.exp(sc - lsev)
            if diagonal:
                p = jnp.where(keep, p, 0.0)
            dp = _dot(dov, vj, 'nt')
            ds = (p * (dp - delta) * scale).astype(BF16)
            dqa_ref[...] += _dot(ds, kj, 'nn')
            dka_ref[pl.ds(ks, tb), :] += _dot(ds, qh, 'tn')
            dva_ref[pl.ds(ks, tb), :] += _dot(p.astype(BF16), dov, 'tn')

        def step(j, carry):
            block(j, False)
            return carry

        lax.fori_loop(0, i, step, 0)
        block(i, True)
        dq_ref[...] = dqa_ref[...].astype(dq_ref.dtype)

        @pl.when(i == nq - 1)
        def _():
            dk_ref[...] = dka_ref[...].astype(dk_ref.dtype)
            dv_ref[...] = dva_ref[...].astype(dv_ref.dtype)

    q_tile = pl.BlockSpec((None, tb, QK_DIM), lambda h, i: (h, i, 0))
    k_head = pl.BlockSpec((None, s, QK_DIM), lambda h, i: (h, 0, 0))
    tile = pl.BlockSpec((tb, HEAD), lambda h, i: (i, h))
    v_head = pl.BlockSpec((s, HEAD), lambda h, i: (0, h))
    return pl.pallas_call(
        body, name=name, grid=(G_HEADS, nq),
        in_specs=[q_tile, k_head, v_head, tile, tile, pl.BlockSpec((None, tb, 1), lambda h, i: (h, i, 0))],
        out_specs=[q_tile, k_head, v_head],
        out_shape=[jax.ShapeDtypeStruct((G_HEADS, s, QK_DIM), BF16), jax.ShapeDtypeStruct((G_HEADS, s, QK_DIM), BF16),
                   jax.ShapeDtypeStruct((s, G_W), BF16)],
        scratch_shapes=[pltpu.VMEM((tb, QK_DIM), F32), pltpu.VMEM((s, QK_DIM), F32), pltpu.VMEM((s, HEAD), F32)],
        compiler_params=_params(("parallel", "arbitrary")),
    )(q, k, vv, o, do, lse)


HALO = 16


def _shift_down(prev, cur, shift, first_tile):
    tr = cur.shape[0]
    full = jnp.concatenate([prev, cur], axis=0)
    out = pltpu.roll(full, shift, axis=0)[HALO:]
    row = lax.broadcasted_iota(jnp.int32, (tr, 1), 0)
    return jnp.where(jnp.logical_and(first_tile, row < shift), 0.0, out)


def _shift_up(cur, nxt, shift, last_tile):
    tr = cur.shape[0]
    full = jnp.concatenate([cur, nxt], axis=0)
    out = pltpu.roll(full, tr + HALO - shift, axis=0)[:tr]
    row = lax.broadcasted_iota(jnp.int32, (tr, 1), 0)
    return jnp.where(jnp.logical_and(last_tile, row >= tr - shift), 0.0, out)


def _lane_chunks(width, lanes):
    return [slice(c0, min(c0 + lanes, width)) for c0 in range(0, width, lanes)]


def _conv_taps(prev_ref, cur_ref, cw_ref, cb_ref, first_tile, cs):
    cur = cur_ref[:, cs].astype(F32)
    prev = prev_ref[:, cs].astype(F32)
    a1 = _shift_down(prev, cur, 1, first_tile)
    a2 = _shift_down(prev, cur, 2, first_tile)
    c = a2 * cw_ref[0:1, cs] + a1 * cw_ref[1:2, cs] + cur * cw_ref[2:3, cs] + cb_ref[:, cs]
    return c, (a2, a1, cur)


def _conv_in_specs(tr, bw, half, layer, row_of, blk_of):
    per = tr // HALO
    specs = []
    for off in (0, half):
        specs.append(pl.BlockSpec((None, HALO, bw), lambda *g, off=off: (blk_of(*g) + off, jnp.maximum(row_of(*g) * per - 1, 0), 0)))
        specs.append(pl.BlockSpec((None, tr, bw), lambda *g, off=off: (blk_of(*g) + off, row_of(*g), 0)))
    for off in (0, half):
        specs.append(pl.BlockSpec((None, None, CONV_W, bw), lambda *g, off=off: (blk_of(*g) + off, layer, 0, 0)))
    for off in (0, half):
        specs.append(pl.BlockSpec((None, 1, bw), lambda *g, off=off: (layer * 2 * half + blk_of(*g) + off, 0, 0)))
    return specs


def _conv_fwd(a, cw, cb, layer, *, name, tr=256):
    nb, s, bw = a.shape
    half = nb // 2
    tr = min(tr, s)

    def body(gp_ref, gc_ref, vp_ref, vc_ref, cwg_ref, cwv_ref, cbg_ref, cbv_ref, o_ref):
        first = pl.program_id(0) == 0
        for cs in _lane_chunks(bw, 256):
            gate, _ = _conv_taps(gp_ref, gc_ref, cwg_ref, cbg_ref, first, cs)
            val, _ = _conv_taps(vp_ref, vc_ref, cwv_ref, cbv_ref, first, cs)
            o_ref[:, cs] = (gate * _sigmoid(gate) * val).astype(o_ref.dtype)

    return pl.pallas_call(
        body, name=name, grid=(s // tr, half),
        in_specs=_conv_in_specs(tr, bw, half, layer, lambda i, j: i, lambda i, j: j),
        out_specs=pl.BlockSpec((tr, bw), lambda i, j: (i, j)),
        out_shape=jax.ShapeDtypeStruct((s, half * bw), BF16),
        compiler_params=_params(("parallel", "parallel")),
    )(a, a, a, a, cw, cw, cb, cb)


def _conv_bwd_dc(a, dact, cw, cb, layer, *, name, after=None, tr=256):
    nb, s, bw = a.shape
    half = nb // 2
    tr = min(tr, s)

    def body(*refs):
        gp_ref, gc_ref, vp_ref, vc_ref, cwg_ref, cwv_ref, cbg_ref, cbv_ref, da_ref = refs[:9]
        dc_ref, dw_ref, db_ref = refs[-3:]
        first = pl.program_id(1) == 0

        @pl.when(first)
        def _():
            dw_ref[...] = jnp.zeros_like(dw_ref)
            db_ref[...] = jnp.zeros_like(db_ref)

        for cs in _lane_chunks(bw, 128):
            gate, gtaps = _conv_taps(gp_ref, gc_ref, cwg_ref, cbg_ref, first, cs)
            val, vtaps = _conv_taps(vp_ref, vc_ref, cwv_ref, cbv_ref, first, cs)
            dact_v = da_ref[:, cs].astype(F32)
            sg = _sigmoid(gate)
            dgate = dact_v * val * (sg * (1.0 + gate * (1.0 - sg)))
            dval = dact_v * (gate * sg)
            dc_ref[0, :, cs] = dgate.astype(dc_ref.dtype)
            dc_ref[1, :, cs] = dval.astype(dc_ref.dtype)
            for kk in range(CONV_W):
                dw_ref[0, kk:kk + 1, cs] += jnp.sum(dgate * gtaps[kk], axis=0, keepdims=True)
                dw_ref[1, kk:kk + 1, cs] += jnp.sum(dval * vtaps[kk], axis=0, keepdims=True)
            db_ref[0, :, cs] += jnp.sum(dgate, axis=0, keepdims=True)
            db_ref[1, :, cs] += jnp.sum(dval, axis=0, keepdims=True)

    outs = pl.pallas_call(
        body, name=name, grid=(half, s // tr),
        in_specs=_conv_in_specs(tr, bw, half, layer, lambda j, i: i, lambda j, i: j)
        + [pl.BlockSpec((tr, bw), lambda j, i: (i, j))]
        + ([pl.BlockSpec(memory_space=pl.ANY)] if after is not None else []),
        out_specs=[pl.BlockSpec((2, None, tr, bw), lambda j, i: (0, j, i, 0)),
                   pl.BlockSpec((2, None, CONV_W, bw), lambda j, i: (0, j, 0, 0)),
                   pl.BlockSpec((2, None, 1, bw), lambda j, i: (0, j, 0, 0))],
        out_shape=[jax.ShapeDtypeStruct((2, half, s, bw), BF16),
                   jax.ShapeDtypeStruct((2, half, CONV_W, bw), F32),
                   jax.ShapeDtypeStruct((2, half, 1, bw), F32)],
        compiler_params=_params(("parallel", "arbitrary")),
    )(a, a, a, a, cw, cw, cb, cb, dact, *([after] if after is not None else []))
    dc, dw, db = outs
    return dc.reshape(nb, s, bw), dw.reshape(nb, CONV_W, bw), db.reshape(nb, 1, bw)


def _conv_bwd_da(dc, cw, layer, *, name, tr=512):
    nb, s, bw = dc.shape
    tr = min(tr, s)
    ni = s // tr
    per = tr // HALO
    last_halo = s // HALO - 1

    def body(c_ref, n_ref, w_ref, o_ref):
        last = pl.program_id(0) == ni - 1
        for cs in _lane_chunks(bw, 256):
            cur = c_ref[:, cs].astype(F32)
            nxt = n_ref[:, cs].astype(F32)
            da = (cur * w_ref[2:3, cs] + _shift_up(cur, nxt, 1, last) * w_ref[1:2, cs]
                  + _shift_up(cur, nxt, 2, last) * w_ref[0:1, cs])
            o_ref[:, cs] = da.astype(o_ref.dtype)

    tile = pl.BlockSpec((None, tr, bw), lambda i, j: (j, i, 0))
    return pl.pallas_call(
        body, name=name, grid=(ni, nb),
        in_specs=[tile,
                  pl.BlockSpec((None, HALO, bw), lambda i, j: (j, jnp.minimum((i + 1) * per, last_halo), 0)),
                  pl.BlockSpec((None, None, CONV_W, bw), lambda i, j: (j, layer, 0, 0))],
        out_specs=tile,
        out_shape=jax.ShapeDtypeStruct((nb, s, bw), BF16),
        compiler_params=_params(("parallel", "parallel")),
    )(dc, dc, cw)


def _rope_tables(positions):
    inv = 1.0 / (ROPE_THETA ** (jnp.arange(0, ROPE_DIM, 2, dtype=F32) / ROPE_DIM))
    ang = positions.astype(F32)[:, None] * inv
    return jnp.cos(ang), jnp.sin(ang)


def _heads_to_major(nope, r1, r2):
    s = r1.shape[0]
    parts = [nope[:, :G_W].reshape(s, G_HEADS, HEAD)]
    for r in (r1, r2):
        parts.append(jnp.broadcast_to(r.reshape(s, -1, ROPE_HALF), (s, G_HEADS, ROPE_HALF)))
    return jnp.concatenate(parts, axis=-1).transpose(1, 0, 2)


def _heads_from_major(t):
    s = t.shape[1]
    t = t.transpose(1, 0, 2)
    return t[:, :, :HEAD].reshape(s, G_W), t[:, :, HEAD:HEAD + ROPE_HALF], t[:, :, HEAD + ROPE_HALF:]


def _local_step(x, mem, positions, target, rep, fetch, emit):
    s, d = x.shape
    n_b = DEPTH - N_A
    tm = min(1024, s)
    cos, sin = _rope_tables(positions)
    cos12 = jnp.tile(cos, (1, G_HEADS))
    sin12 = jnp.tile(sin, (1, G_HEADS))
    r1_col = G_W // (G_HEADS * ROPE_HALF)
    b_sp_t = rep['b_sp'].transpose(0, 2, 1)

    saved = []
    kv = None
    shared = None
    for l in range(DEPTH):
        wm = fetch(('in', l), x)
        if l == 0:
            shared = {'g_v': wm['g_v'], 'conv_w': wm['conv_w']}
            bw = shared['conv_w'].shape[-1]
            conv_b = rep['conv_b'].reshape(-1, 1, bw)
        sv = {'x_in': x, 'wm': wm}
        if l == N_A:
            xn_kv = _rmsnorm(x, rep['g_kv'], name="kvnorm")
            kvx = _mm(xn_kv, wm['w_kv_a'], dims='nn', out_dtype=F32, name="kvproj", tm=tm, tn=KV_PAD)
            ckv = _rmsnorm(kvx, rep['g_kv_lat'], width=KV_RANK, name="ckvnorm")
            k1, k2 = _rope(kvx[:, KV_RANK:KV_RANK + ROPE_HALF], kvx[:, KV_RANK + ROPE_HALF:KV_RANK + ROPE_DIM],
                           cos, sin, name="krope")
            kv = {'x': x, 'xn': xn_kv, 'kvx': kvx, 'ckv': ckv, 'k1': k1, 'k2': k2, 'w_kv_a': wm['w_kv_a']}
        h = _rmsnorm(x, rep['g_mix'][l], name=f"mixnorm{l}")
        if l < N_A:
            z = _mm(h, wm['w_in'], dims='nn', out_dtype=BF16, name=f"in_a{l}", tm=tm, tn=512)
            main = _sgu_fwd(z, shared['g_v'][l], rep['w_sp'][l], b_sp_t[l], name=f"sgu{l}")
            qcol = 2 * G_W // MEM_W
        else:
            j = l - N_A
            z = _mm(h, wm['w_in'], dims='nn', out_dtype=BF16, name=f"in_b{j}", tm=tm, tn=1024)
            qn = _rmsnorm(z, rep['g_q_lat'][j], width=Q_RANK, name=f"qnorm{j}")
            qp = _mm(qn, wm['w_uqp'], dims='nn', out_dtype=BF16, name=f"uq{j}", tm=tm, tn=768)
            rr1, rr2 = _rope(qp, qp, cos12, sin12, col1=r1_col, col2=r1_col + 1, name=f"qrope{j}")
            qh = _heads_to_major(qp, rr1, rr2)
            kn = _mm(kv['ckv'], wm['w_uk'], dims='nn', out_dtype=BF16, name=f"k_up{j}", tm=tm, tn=768)
            kh = _heads_to_major(kn, kv['k1'], kv['k2'])
            vv = _mm(kv['ckv'], wm['w_uv'], dims='nn', out_dtype=BF16, name=f"v_up{j}", tm=tm, tn=768)
            main, lse = _mha_fwd(qh, kh, vv, name=f"mha{j}")
            qcol = Q_RANK // MEM_W
            sv.update(qn=qn, qh=qh, kh=kh, vv=vv, lse=lse)
        wm.update(fetch(('rest', l), z))
        memn = _rmsnorm(mem, rep['g_mem'][l], name=f"memnorm{l}")
        kvm = _mm(memn, wm['w_mem_kv'], dims='nn', out_dtype=BF16, name=f"memkv{l}", tm=tm, tn=1024)
        mix = _memattn_fwd(z, kvm, main, qcol=qcol, name=f"memattn{l}")
        x_mid = _mm(mix, wm['w_out'], dims='nn', res=x, out_dtype=F32, name=f"out{l}", tm=tm, tn=1024)
        wf = fetch(('up', l), x_mid)
        h2 = _rmsnorm(x_mid, rep['g_ffn'][l], name=f"ffnnorm{l}")
        a = _mm(h2, wf['w_up'], dims='nn', b_blocked=True, out_dtype=BF16, out_block=bw,
                name=f"up{l}", tm=tm, tn=bw)
        act = _conv_fwd(a, shared['conv_w'], conv_b, l, name=f"conv{l}")
        wf.update(fetch(('down', l), act))
        x = _mm(act, wf['w_down'], dims='nn', res=x_mid, out_dtype=F32, name=f"down{l}", tm=512, tn=1024)
        sv.update(h=h, memn=memn, kvm=kvm, z=z, qcol=qcol, mix=mix, x_mid=x_mid, h2=h2, a=a, act=act, wf=wf)
        saved.append(sv)

    sq, dx, dg_final = _final_loss(x, target, rep['g_final'], name="loss")

    g = {k: [None] * DEPTH for k in ('g_mix', 'g_ffn', 'g_mem', 'conv_w', 'conv_b')}
    for k in ('g_v', 'w_sp', 'b_sp'):
        g[k] = [None] * N_A
    g['g_q_lat'] = [None] * n_b
    g['g_final'] = dg_final
    dckv_sum, dkr_sum = None, None

    for l in reversed(range(DEPTH)):
        sv = saved[l]
        wm, wf = sv['wm'], sv['wf']
        dact = _mm(dx, wf['w_down'], dims='nt', out_dtype=BF16, name=f"d_act{l}", tm=tm, tn=bw)
        dw_down = _mm(sv['act'], dx, dims='tn', out_dtype=BF16, name=f"dw_down{l}", tm=bw, tn=512)
        tok = emit(('down', l), {'w_ffn_down': dw_down})
        dc, dcw, dcb = _conv_bwd_dc(sv['a'], dact, shared['conv_w'], conv_b, l, after=tok, name=f"d_conv{l}")
        g['conv_w'][l], g['conv_b'][l] = dcw, dcb
        da = _conv_bwd_da(dc, shared['conv_w'], l, name=f"d_convin{l}")
        dw_up = _mm(sv['h2'], da, dims='tn', b_blocked=True, out_dtype=BF16, out_block=bw,
                    name=f"dw_up{l}", tm=512, tn=bw, n_outer=True)
        tok = emit(('up', l), {'w_ffn_up': dw_up})
        dh2 = _mm_blocked_nt(da, wf['w_up'], out_dtype=BF16, name=f"d_h2{l}", tm=512, tn=1024, blocks_per_step=4,
                             after=tok)
        dx_mid, g['g_ffn'][l] = _rmsnorm_bwd(sv['x_mid'], rep['g_ffn'][l], dh2, dres=dx, name=f"d_ffnnorm{l}")
        dmix = _mm(dx_mid, wm['w_out'], dims='nt', out_dtype=BF16, name=f"d_mix{l}", tm=tm, tn=1024)
        dw_out = _mm(sv['mix'], dx_mid, dims='tn', out_dtype=BF16, name=f"dw_out{l}", tm=1024, tn=256)
        dqm, dkvm = _memattn_bwd(sv['z'], sv['kvm'], dmix, qcol=sv['qcol'], name=f"d_memattn{l}")
        dw_memkv = _mm(sv['memn'], dkvm, dims='tn', out_dtype=BF16, name=f"dw_memkv{l}", tm=1024, tn=1024)
        tok = emit(('rest', l), {'w_out': dw_out, 'w_mem_kv': dw_memkv})
        gm = {}
        dmemn = _mm(dkvm, wm['w_mem_kv'], dims='nt', out_dtype=F32, name=f"d_memn{l}", tm=tm, tn=1024, after=tok)
        _, g['g_mem'][l] = _rmsnorm_bwd(mem, rep['g_mem'][l], dmemn, out_dtype=BF16, name=f"d_memnorm{l}")
        if l < N_A:
            dz, dwsp, dbsp_t, dgv = _sgu_bwd(sv['z'], dmix, dqm, shared['g_v'][l], rep['w_sp'][l], b_sp_t[l],
                                             name=f"d_sgu{l}")
            g['w_sp'][l], g['b_sp'][l], g['g_v'][l] = dwsp, dbsp_t.T, dgv
            dh = _mm(dz, wm['w_in'], dims='nt', out_dtype=BF16, name=f"d_h_a{l}", tm=tm, tn=1024)
            gm['w_in_a'] = _mm(sv['h'], dz, dims='tn', out_dtype=BF16, name=f"dw_in_a{l}", tm=1024, tn=512)
        else:
            j = l - N_A
            dqh, dkh, dvv = _mha_bwd(sv['qh'], sv['kh'], sv['vv'], sv['mix'], dmix, sv['lse'], name=f"d_mha{j}")
            dq_nope, dr1, dr2 = _heads_from_major(dqh)
            dkn, dk1, dk2 = _heads_from_major(dkh)
            dkr = jnp.concatenate([dk1.astype(F32).sum(axis=1), dk2.astype(F32).sum(axis=1)], axis=-1)
            gm['w_uk'] = _mm(kv['ckv'], dkn, dims='tn', out_dtype=BF16, name=f"dw_uk{j}", tm=512, tn=768)
            gm['w_uv'] = _mm(kv['ckv'], dvv, dims='tn', out_dtype=BF16, name=f"dw_uv{j}", tm=512, tn=768)
            dckv = _mm(dkn, wm['w_uk'], dims='nt', out_dtype=F32, res=dckv_sum, name=f"d_ckv_k{j}", tm=tm, tn=512)
            dckv_sum = _mm(dvv, wm['w_uv'], dims='nt', out_dtype=F32, res=dckv, name=f"d_ckv_v{j}", tm=tm, tn=512)
            dkr_sum = dkr if dkr_sum is None else dkr_sum + dkr
            dq1, dq2 = _rope(dr1.reshape(s, -1), dr2.reshape(s, -1), cos12, sin12, inverse=True, name=f"d_qrope{j}")
            dqp = jnp.concatenate([dq_nope, dq1, dq2], axis=-1)
            dqn = _mm(dqp, wm['w_uqp'], dims='nt', out_dtype=BF16, name=f"d_qn{j}", tm=tm, tn=512)
            gm['w_uqp'] = _mm(sv['qn'], dqp, dims='tn', out_dtype=BF16, name=f"dw_uq{j}", tm=512, tn=768)
            dqlat, g['g_q_lat'][j] = _rmsnorm_bwd(sv['z'], rep['g_q_lat'][j], dqn, width=Q_RANK, out_dtype=BF16,
                                                 name=f"d_qnorm{j}")
            dz = jnp.concatenate([dqlat, dqm], axis=-1)
            dh = _mm(dz, wm['w_in'], dims='nt', out_dtype=BF16, name=f"d_h_b{j}", tm=tm, tn=1024)
            gm['w_in_b'] = _mm(sv['h'], dz, dims='tn', out_dtype=BF16, name=f"dw_in_b{j}", tm=1024, tn=512)
        tok = emit(('mix', l), gm)
        dx, g['g_mix'][l] = _rmsnorm_bwd(sv['x_in'], rep['g_mix'][l], dh, dres=dx_mid, after=tok, name=f"d_mixnorm{l}")
        if l == N_A:
            dkvx_c, g['g_kv_lat'] = _rmsnorm_bwd(kv['kvx'], rep['g_kv_lat'], dckv_sum, width=KV_RANK, out_dtype=BF16,
                                                 name="d_ckvnorm")
            dk1, dk2 = _rope(dkr_sum[:, :ROPE_HALF], dkr_sum[:, ROPE_HALF:], cos, sin, inverse=True, name="d_krope")
            dkvx = jnp.concatenate([dkvx_c, dk1, dk2, jnp.zeros((s, KV_PAD - KV_RANK - ROPE_DIM), BF16)], axis=-1)
            dxn = _mm(dkvx, kv['w_kv_a'], dims='nt', out_dtype=BF16, name="d_kvnorm_in", tm=tm, tn=1024)
            dw_kv = _mm(kv['xn'], dkvx, dims='tn', out_dtype=BF16, name="dw_kv", tm=1024, tn=KV_PAD)
            tok = emit(('kv', 0), {'w_kv_a': dw_kv})
            dx, g['g_kv'] = _rmsnorm_bwd(kv['x'], rep['g_kv'], dxn, dres=dx, after=tok, name="d_kvnorm")
    return jnp.sum(sq), dx, g


MESH_IDS = pl.DeviceIdType.MESH
PEER_MASKS = tuple((k >> 2 & 1, k >> 1 & 1, k & 1) for k in range(1, N_DEV))
CHIP_MASKS = ((1, 0), (0, 1), (1, 1))
N_PEER = N_DEV - 1
SEMS_PER_BUFFER = 2 * N_PEER + 1
DATAFLOW = pltpu.SideEffectType.DATAFLOW_SIDE_EFFECTING
HBM_SPEC = pl.BlockSpec(memory_space=pltpu.HBM)
SEM_SPEC = pl.BlockSpec(memory_space=pltpu.SEMAPHORE)


def _my_position():
    return lax.axis_index("x"), lax.axis_index("y"), lax.axis_index("c")


def _flip(pos, mask):
    return tuple(1 - p if f else p for p, f in zip(pos, mask))


def _linear_id(pos):
    return 4 * pos[0] + 2 * pos[1] + pos[2]


def _hbm(x):
    return pltpu.with_memory_space_constraint(x, pltpu.HBM)


def _buffer_copies(src_ref, lead, land_ref, sems, scatter, near=False):
    me = _my_position()
    my_id = _linear_id(me)
    src = src_ref.at[lead] if lead else src_ref
    own = pltpu.make_async_copy(src.at[my_id] if scatter else src, land_ref.at[my_id], sems.at[2 * N_PEER])
    pairs = []
    for k, mask in enumerate(PEER_MASKS):
        if near and mask[2] == 1 and mask != (0, 0, 1):
            continue
        peer = _flip(me, mask)
        peer_id = _linear_id(peer)
        block = src.at[peer_id] if scatter else src
        send = pltpu.make_async_remote_copy(src_ref=block, dst_ref=land_ref.at[my_id], send_sem=sems.at[k],
                                            recv_sem=sems.at[N_PEER + k], device_id=peer, device_id_type=MESH_IDS)
        arrival = pltpu.make_async_remote_copy(src_ref=block, dst_ref=land_ref.at[peer_id], send_sem=sems.at[k],
                                               recv_sem=sems.at[N_PEER + k], device_id=peer, device_id_type=MESH_IDS)
        pairs.append((send, arrival))
    return own, pairs


def _exchange_start(srcs, buffers, *, name, scatter):
    ns, nb = len(srcs), len(buffers)
    lands = [_hbm(lax.empty((N_DEV,) + tuple(shape), dtype)) for _, _, shape, dtype, _ in buffers]

    def body(*refs):
        src_refs, land_refs = refs[:ns], refs[ns:ns + nb]
        sem_refs = refs[ns + nb:ns + 2 * nb]
        token = refs[-1]
        for b, (si, lead, _, _, near) in enumerate(buffers):
            own, pairs = _buffer_copies(src_refs[si], lead, land_refs[b], sem_refs[b], scatter, near)
            own.start()
            for send, _ in pairs:
                send.start()
        token[...] = jnp.zeros_like(token)

    out_shape = ([pltpu.SemaphoreType.DMA((SEMS_PER_BUFFER,))] * nb
                 + [pltpu.HBM(a.shape, a.dtype) for a in srcs]
                 + [pltpu.HBM(a.shape, a.dtype) for a in lands]
                 + [jax.ShapeDtypeStruct((8, 128), F32)])
    aliases = {i: nb + i for i in range(ns + nb)}
    outs = pl.pallas_call(
        body, name=name, in_specs=[HBM_SPEC] * (ns + nb),
        out_specs=[SEM_SPEC] * nb + [HBM_SPEC] * (ns + nb) + [pl.BlockSpec(memory_space=pltpu.VMEM)],
        out_shape=out_shape, input_output_aliases=aliases,
        compiler_params=pltpu.CompilerParams(has_side_effects=DATAFLOW),
    )(*[_hbm(a) for a in srcs], *lands)
    sems = list(outs[:nb])
    src_thru = list(outs[nb:nb + ns])
    land_thru = list(outs[nb + ns:nb + ns + nb])
    return sems, land_thru, src_thru, outs[-1]


def _exchange_wait(srcs_thru, buffers, sems, lands, after, *, name, scatter):
    ns, nb = len(srcs_thru), len(buffers)
    has_after = after is not None

    def body(*refs):
        src_refs, land_refs = refs[:ns], refs[ns:ns + nb]
        sem_refs = refs[ns + nb:ns + 2 * nb]
        for b, (si, lead, _, _, near) in enumerate(buffers):
            own, pairs = _buffer_copies(src_refs[si], lead, land_refs[b], sem_refs[b], scatter, near)
            for send, arrival in pairs:
                send.wait_send()
                arrival.wait_recv()
            own.wait()

    operands = list(srcs_thru) + list(lands) + list(sems) + ([after] if has_after else [])
    in_specs = ([HBM_SPEC] * (ns + nb) + [SEM_SPEC] * nb + ([pl.BlockSpec(memory_space=pl.ANY)] if has_after else []))
    outs = pl.pallas_call(
        body, name=name, in_specs=in_specs, out_specs=[HBM_SPEC] * nb,
        out_shape=[pltpu.HBM(a.shape, a.dtype) for a in lands],
        input_output_aliases={ns + b: b for b in range(nb)},
        compiler_params=pltpu.CompilerParams(has_side_effects=DATAFLOW),
    )(*operands)
    return list(outs)


def _exchange(arrays, *, name, scatter, near=None, after=None):
    n = len(arrays)
    near = [False] * n if near is None else near
    extra = [] if after is None else [after]
    out_shapes = [jax.ShapeDtypeStruct(a.shape if scatter else (N_DEV,) + a.shape, a.dtype) for a in arrays]

    def body(*refs):
        srcs, outs, sems = refs[:n], refs[n + len(extra):2 * n + len(extra)], refs[2 * n + len(extra):]
        started = []
        for a in range(n):
            own, pairs = _buffer_copies(srcs[a], (), outs[a], sems[a], scatter, near[a])
            own.start()
            for send, _ in pairs:
                send.start()
            started.append((own, pairs))
        for own, pairs in started:
            for send, arrival in pairs:
                arrival.wait_recv()
                send.wait_send()
            own.wait()

    any_spec = pl.BlockSpec(memory_space=pl.ANY)
    outs = pl.pallas_call(
        body, name=name, in_specs=[any_spec] * (n + len(extra)), out_specs=[any_spec] * n, out_shape=out_shapes,
        scratch_shapes=[pltpu.SemaphoreType.DMA((SEMS_PER_BUFFER,))] * n,
    )(*arrays, *extra)
    return list(outs)


def _forward_to_sibling(lands, *, name):
    n = len(lands)

    def body(*refs):
        ins, outs, sems = refs[:n], refs[n:2 * n], refs[2 * n:]
        me = _my_position()
        sibling = _flip(me, (0, 0, 1))
        pairs = []
        for b in range(n):
            for k, (fx, fy) in enumerate(CHIP_MASKS):
                mine = _linear_id(_flip(me, (fx, fy, 0)))
                theirs = _linear_id(_flip(me, (fx, fy, 1)))
                send = pltpu.make_async_remote_copy(
                    src_ref=ins[b].at[mine], dst_ref=outs[b].at[mine], send_sem=sems[b].at[k],
                    recv_sem=sems[b].at[len(CHIP_MASKS) + k], device_id=sibling, device_id_type=MESH_IDS)
                arrival = pltpu.make_async_remote_copy(
                    src_ref=ins[b].at[mine], dst_ref=outs[b].at[theirs], send_sem=sems[b].at[k],
                    recv_sem=sems[b].at[len(CHIP_MASKS) + k], device_id=sibling, device_id_type=MESH_IDS)
                send.start()
                pairs.append((send, arrival))
        for send, arrival in pairs:
            arrival.wait_recv()
            send.wait_send()

    any_spec = pl.BlockSpec(memory_space=pl.ANY)
    outs = pl.pallas_call(
        body, name=name, in_specs=[any_spec] * n, out_specs=[any_spec] * n,
        out_shape=[jax.ShapeDtypeStruct(a.shape, a.dtype) for a in lands],
        input_output_aliases={b: b for b in range(n)},
        scratch_shapes=[pltpu.SemaphoreType.DMA((2 * len(CHIP_MASKS),))] * n,
    )(*lands)
    return list(outs)


def _sum_slots(parts_ref):
    total = parts_ref[0].astype(F32)
    for q in range(1, parts_ref.shape[0]):
        total = total + parts_ref[q].astype(F32)
    return total


def _row_tile(rows, cols, n_arrays):
    budget = (12 * 1024 * 1024) // (4 * n_arrays * max(cols, 128))
    t = rows
    while t > budget and t % 2 == 0 and (t // 2) % 16 == 0:
        t //= 2
    return t


def _sum_adam(parts, w, m, v, layer, outs, *, name):
    q, r, c = parts.shape
    nl = w.shape[0]
    tr = _row_tile(r, c, q + 7)
    c1 = 1.0 - ADAM_B1 ** ADAM_STEP
    c2 = 1.0 - ADAM_B2 ** ADAM_STEP
    if outs is None:
        outs = [lax.empty((nl, r, c), F32) for _ in range(4)]

    def body(p_ref, w_ref, m_ref, v_ref, g_in, d_in, mo_in, vo_in, g_ref, d_ref, mo_ref, vo_ref):
        grad = _sum_slots(p_ref)
        m_new = ADAM_B1 * m_ref[...] + (1.0 - ADAM_B1) * grad
        v_new = ADAM_B2 * v_ref[...] + (1.0 - ADAM_B2) * (grad * grad)
        m_hat = m_new / c1
        v_hat = v_new / c2
        g_ref[...] = grad
        d_ref[...] = -ADAM_LR * (m_hat / (jnp.sqrt(v_hat) + ADAM_EPS) + ADAM_WD * w_ref[...])
        mo_ref[...] = m_new
        vo_ref[...] = v_new

    tile = pl.BlockSpec((None, tr, c), lambda i: (layer, i, 0))
    any_spec = pl.BlockSpec(memory_space=pl.ANY)
    return pl.pallas_call(
        body, name=name, grid=(r // tr,),
        in_specs=[pl.BlockSpec((q, tr, c), lambda i: (0, i, 0)), tile, tile, tile] + [any_spec] * 4,
        out_specs=[tile] * 4, out_shape=[jax.ShapeDtypeStruct((nl, r, c), F32)] * 4,
        input_output_aliases={4: 0, 5: 1, 6: 2, 7: 3},
        compiler_params=_params(("parallel",)),
    )(parts, w, m, v, *outs)


def _sum_parts(parts, *, name):
    q, r, c = parts.shape

    def body(p_ref, o_ref):
        o_ref[...] = _sum_slots(p_ref)

    return pl.pallas_call(
        body, name=name, in_specs=[pl.BlockSpec((q, r, c), lambda: (0, 0, 0))],
        out_specs=pl.BlockSpec((r, c), lambda: (0, 0)), out_shape=jax.ShapeDtypeStruct((r, c), F32),
        compiler_params=_params(),
    )(parts)


INPUT_NAMES = (['x', 'mem', 'positions'] + WEIGHTS + ['loss_target'] + ['m_' + n for n in WEIGHTS]
               + ['v_' + n for n in WEIGHTS])
SMALL_ALIGN = N_DEV * 8 * 128
TWO_LEVEL_LAYERS = N_A
GROUP_ORDER = ('in', 'rest', 'up', 'down')
GROUP_WEIGHTS = {'in': (['w_in_a'], ['w_in_b', 'w_uq', 'w_uk', 'w_uv']), 'rest': (['w_mem_kv', 'w_out'],) * 2,
                 'up': (['w_ffn_up'],) * 2, 'down': (['w_ffn_down'],) * 2}
LAYERED = {'w_in_a': 0, 'w_in_b': N_A, 'w_uq': N_A, 'w_uk': N_A, 'w_uv': N_A, 'w_mem_kv': 0, 'w_out': 0,
           'w_ffn_up': 0, 'w_ffn_down': 0}


def _permute_uq(w_uq):
    r = w_uq.shape[0]
    q = w_uq.reshape(r, G_HEADS, HEAD + ROPE_DIM)
    return jnp.concatenate([q[..., :HEAD].reshape(r, -1), q[..., HEAD:HEAD + ROPE_HALF].reshape(r, -1),
                            q[..., HEAD + ROPE_HALF:].reshape(r, -1)], axis=-1)


def _unpermute_uq(w_uqp):
    r = w_uqp.shape[0]
    nope = w_uqp[..., :G_W].reshape(r, G_HEADS, HEAD)
    r1 = w_uqp[..., G_W:G_W + G_HEADS * ROPE_HALF].reshape(r, G_HEADS, ROPE_HALF)
    r2 = w_uqp[..., G_W + G_HEADS * ROPE_HALF:].reshape(r, G_HEADS, ROPE_HALF)
    return jnp.concatenate([nope, r1, r2], axis=-1).reshape(r, -1)


def _cols_from_stack(st):
    _, r, n = st.shape
    return st.transpose(1, 0, 2).reshape(r, N_DEV * n)


def _cols_to_stack(wh):
    r, c = wh.shape
    return wh.reshape(r, N_DEV, c // N_DEV).transpose(1, 0, 2)


def _group_weights(group):
    kind, l = group
    return GROUP_WEIGHTS[kind][0 if l < N_A else 1]


def _step(args):
    p = dict(zip(INPUT_NAMES, args))
    x, mem, positions, target = p['x'][0], p['mem'][0], p['positions'][0], p['loss_target'][0]
    d = x.shape[-1]
    my_id = _linear_id(_my_position())

    w_kv_pad = jnp.pad(p['w_kv_a'], ((0, 0), (0, KV_PAD - p['w_kv_a'].shape[1])))
    shard = {k: p[k].astype(BF16) for k in LAYERED}
    shard['w_uk'] = shard['w_uk'].reshape(shard['w_uk'].shape[0], shard['w_uk'].shape[1], -1)
    shard['w_uv'] = shard['w_uv'].reshape(shard['w_uv'].shape[0], shard['w_uv'].shape[1], -1)
    shard.update(conv_w=p['conv_w'], g_v=p['g_v'], w_kv_a=w_kv_pad.astype(BF16))
    src_names = list(shard)
    gather_groups = []
    for l in range(DEPTH):
        gather_groups += [(kind, l) for kind in GROUP_ORDER]
    buffers, owner = [], []
    for group in gather_groups:
        kind, l = group
        for k in _group_weights(group):
            buffers.append((src_names.index(k), (l - LAYERED[k],), shard[k].shape[1:], shard[k].dtype, l < TWO_LEVEL_LAYERS))
            owner.append((group, k))
        if group == ('in', 0):
            for k in ('g_v', 'conv_w'):
                buffers.append((src_names.index(k), (), shard[k].shape, shard[k].dtype, True))
                owner.append((group, k))
        if group == ('in', N_A):
            buffers.append((src_names.index('w_kv_a'), (), shard['w_kv_a'].shape, BF16, False))
            owner.append((group, 'w_kv_a'))
    g_sems, g_lands, g_srcs, _ = _exchange_start([shard[k] for k in src_names], buffers, name="gather_start",
                                                 scatter=False)

    def fetch(group, after):
        idx = [i for i, (grp, _) in enumerate(owner) if grp == group]
        landed = _exchange_wait(g_srcs, [buffers[i] for i in idx], [g_sems[i] for i in idx],
                                [g_lands[i] for i in idx], after, name=f"gather_wait_{group[0]}{group[1]}",
                                scatter=False)
        if group[1] < TWO_LEVEL_LAYERS:
            landed = _forward_to_sibling(landed, name=f"gather_forward_{group[0]}{group[1]}")
        got = {owner[i][1]: t for i, t in zip(idx, landed)}
        out = {}
        for k, t in got.items():
            if k in ('w_in_a', 'w_uq'):
                out[k] = _cols_from_stack(t)
            elif k == 'g_v':
                out[k] = t.transpose(1, 0, 2).reshape(t.shape[1], -1)
            elif k in ('w_ffn_up', 'conv_w'):
                out[k] = t
            else:
                out[k] = t.reshape(-1, t.shape[-1])
        if 'w_uq' in out:
            out['w_uqp'] = _permute_uq(out.pop('w_uq'))
        for old, new in (('w_in_a', 'w_in'), ('w_in_b', 'w_in'), ('w_ffn_up', 'w_up'), ('w_ffn_down', 'w_down')):
            if old in out:
                out[new] = out.pop(old)
        return out

    pending = []

    def emit(group, grads):
        send = {}
        for k, t in grads.items():
            if k == 'w_in_a':
                send[k] = _cols_to_stack(t)
            elif k == 'w_uqp':
                send['w_uq'] = _cols_to_stack(_unpermute_uq(t))
            elif k == 'w_ffn_up':
                send[k] = t
            elif k == 'w_kv_a':
                cols = p['w_kv_a'].shape[1]
                send[k] = t[:, :cols].reshape(N_DEV, -1, cols)
            else:
                send[k] = t.reshape(N_DEV, t.shape[0] // N_DEV, t.shape[1])
        keys = list(send)
        bufs = [(i, (), send[k].shape[1:], send[k].dtype, False) for i, k in enumerate(keys)]
        sems, lands, srcs, token = _exchange_start([send[k] for k in keys], bufs,
                                                   name=f"scatter_start_{group[0]}{group[1]}", scatter=True)
        pending.append((group, keys, bufs, sems, lands, srcs))
        return token

    rep = {k: p[k] for k in REPLICATED}
    sq, grad_x, g = _local_step(x, mem, positions, target, rep, fetch, emit)
    loss = (0.5 / d) * lax.psum(sq, ("x", "y", "c"))

    out, running = {}, {}
    order = grad_x
    for group, keys, bufs, sems, lands, srcs in pending:
        landed = _exchange_wait(srcs, bufs, sems, lands, order, name=f"scatter_wait_{group[0]}{group[1]}", scatter=True)
        for k, parts in zip(keys, landed):
            stacked = k in LAYERED
            nl = p[k].shape[0] if stacked else 1
            layer = group[1] - LAYERED[k] if stacked else 0
            rows = p[k].size // nl // p[k].shape[-1]
            view = (nl, rows, p[k].shape[-1])
            running[k] = _sum_adam(parts.reshape(N_DEV, rows, view[2]), p[k].reshape(view), p['m_' + k].reshape(view),
                                   p['v_' + k].reshape(view), layer, running.get(k), name=f"adam_{k}{layer}")
            order = running[k][1]
    for k, res in running.items():
        out[k] = [t.reshape(p[k].shape) for t in res]

    small = {
        'g_mix': jnp.concatenate(g['g_mix']), 'g_ffn': jnp.concatenate(g['g_ffn']), 'g_final': g['g_final'],
        'w_sp': jnp.stack(g['w_sp']), 'b_sp': jnp.stack(g['b_sp']), 'g_kv': g['g_kv'], 'g_kv_lat': g['g_kv_lat'],
        'g_q_lat': jnp.concatenate(g['g_q_lat']), 'g_mem': jnp.concatenate(g['g_mem']),
        'conv_b': jnp.stack(g['conv_b']),
        'g_v': jnp.concatenate(g['g_v']),
        'conv_w': jnp.stack(g['conv_w']).transpose(0, 2, 1, 3),
    }
    small_names = REPLICATED + SMALL_SHARDED
    flat = jnp.concatenate([small[k].reshape(-1).astype(F32) for k in small_names])
    n_small = flat.shape[0]
    padded = -(-n_small // SMALL_ALIGN) * SMALL_ALIGN
    flat = jnp.pad(flat, (0, padded - n_small)).reshape(N_DEV, -1, 128)
    last_update = out[pending[-1][1][-1]][1]
    (small_parts,) = _exchange([flat], name="scatter_small", scatter=True, after=last_update)
    reduced = _sum_parts(small_parts, name="sum_small")
    (small_all,) = _exchange([reduced], name="gather_small", scatter=False)
    small_all = small_all.reshape(-1)
    grads_small, off = {}, 0
    for k in small_names:
        size = small[k].size
        grads_small[k] = small_all[off:off + size].reshape(small[k].shape)
        off += size
    grads_small['g_v'] = lax.dynamic_slice_in_dim(grads_small['g_v'], my_id * p['g_v'].shape[1], p['g_v'].shape[1], axis=1)
    grads_small['conv_w'] = lax.dynamic_index_in_dim(grads_small['conv_w'], my_id, axis=2, keepdims=False)
    gs = jnp.concatenate([grads_small[k].reshape(-1) for k in small_names])
    n_loc = gs.shape[0]
    pad_loc = -(-n_loc // 1024) * 1024 - n_loc

    def pack(prefix):
        t = jnp.concatenate([p[prefix + k].reshape(-1) for k in small_names])
        return jnp.pad(t, (0, pad_loc)).reshape(1, -1, 128)

    res = _sum_adam(jnp.pad(gs, (0, pad_loc)).reshape(1, -1, 128), pack(''), pack('m_'), pack('v_'), 0, None,
                    name="adam_small")
    off = 0
    for k in small_names:
        size = p[k].size
        out[k] = [t.reshape(-1)[off:off + size].reshape(p[k].shape) for t in res]
        off += size

    outs = [loss, grad_x[None]]
    for i in range(4):
        outs += [out[k][i] for k in WEIGHTS]
    return tuple(outs)


def kernel(x, mem, positions, g_mix, g_ffn, g_final, w_in_a, g_v, w_sp, b_sp, g_kv, w_kv_a, g_kv_lat, w_in_b, g_q_lat, w_uq, w_uk, w_uv, g_mem, w_mem_kv, w_out, w_ffn_up, conv_w, conv_b, w_ffn_down, loss_target, m_g_mix, m_g_ffn, m_g_final, m_w_in_a, m_g_v, m_w_sp, m_b_sp, m_g_kv, m_w_kv_a, m_g_kv_lat, m_w_in_b, m_g_q_lat, m_w_uq, m_w_uk, m_w_uv, m_g_mem, m_w_mem_kv, m_w_out, m_w_ffn_up, m_conv_w, m_conv_b, m_w_ffn_down, v_g_mix, v_g_ffn, v_g_final, v_w_in_a, v_g_v, v_w_sp, v_b_sp, v_g_kv, v_w_kv_a, v_g_kv_lat, v_w_in_b, v_g_q_lat, v_w_uq, v_w_uk, v_w_uv, v_g_mem, v_w_mem_kv, v_w_out, v_w_ffn_up, v_conv_w, v_conv_b, v_w_ffn_down):
    return _step((x, mem, positions, g_mix, g_ffn, g_final, w_in_a, g_v, w_sp, b_sp, g_kv, w_kv_a, g_kv_lat, w_in_b, g_q_lat, w_uq, w_uk, w_uv, g_mem, w_mem_kv, w_out, w_ffn_up, conv_w, conv_b, w_ffn_down, loss_target, m_g_mix, m_g_ffn, m_g_final, m_w_in_a, m_g_v, m_w_sp, m_b_sp, m_g_kv, m_w_kv_a, m_g_kv_lat, m_w_in_b, m_g_q_lat, m_w_uq, m_w_uk, m_w_uv, m_g_mem, m_w_mem_kv, m_w_out, m_w_ffn_up, m_conv_w, m_conv_b, m_w_ffn_down, v_g_mix, v_g_ffn, v_g_final, v_w_in_a, v_g_v, v_w_sp, v_b_sp, v_g_kv, v_w_kv_a, v_g_kv_lat, v_w_in_b, v_g_q_lat, v_w_uq, v_w_uk, v_w_uv, v_g_mem, v_w_mem_kv, v_w_out, v_w_ffn_up, v_conv_w, v_conv_b, v_w_ffn_down))
```

```python
import math

import jax
import jax.numpy as jnp
from jax import lax
from jax.experimental import pallas as pl
from jax.experimental.pallas import tpu as pltpu

F32 = jnp.float32
BF16 = jnp.bfloat16

N_DEV = 8
N_A = 2
DEPTH = 4
G_HEADS = 12
HEAD = 128
CHUNK = 128
MEM_HEADS = 4
MEM_W = MEM_HEADS * HEAD
G_W = G_HEADS * HEAD
ROPE_DIM = 64
ROPE_HALF = ROPE_DIM // 2
KV_RANK = 512
Q_RANK = 512
KV_PAD = 640
ROPE_THETA = 10000.0
EPS = 1e-6
CONV_W = 3

ADAM_LR = 0.001
ADAM_B1 = 0.9
ADAM_B2 = 0.999
ADAM_EPS = 1e-08
ADAM_WD = 0.01
ADAM_STEP = 10

VMEM_LIMIT_V7X = 56 * 1024 * 1024
MASK_VALUE = -1e30

WEIGHTS = ['g_mix', 'g_ffn', 'g_final', 'w_in_a', 'g_v', 'w_sp', 'b_sp', 'g_kv', 'w_kv_a', 'g_kv_lat',
           'w_in_b', 'g_q_lat', 'w_uq', 'w_uk', 'w_uv', 'g_mem', 'w_mem_kv', 'w_out', 'w_ffn_up',
           'conv_w', 'conv_b', 'w_ffn_down']
REPLICATED = ['g_mix', 'g_ffn', 'g_final', 'w_sp', 'b_sp', 'g_kv', 'g_kv_lat', 'g_q_lat', 'g_mem', 'conv_b']
SMALL_SHARDED = ['g_v', 'conv_w']


def _params(sem=None):
    return pltpu.CompilerParams(dimension_semantics=sem, vmem_limit_bytes=VMEM_LIMIT_V7X)


def _dot(a, b, dims):
    contract = {'nn': ((1,), (0,)), 'nt': ((1,), (1,)), 'tn': ((0,), (0,))}[dims]
    return lax.dot_general(a, b, (contract, ((), ())), preferred_element_type=F32)


def _erf(x):
    return lax.erf(x)


def _gelu(x):
    return 0.5 * x * (1.0 + _erf(x * (2.0 ** -0.5)))


def _gelu_grad(x):
    cdf = 0.5 * (1.0 + _erf(x * (2.0 ** -0.5)))
    pdf = jnp.exp(-0.5 * x * x) * (1.0 / math.sqrt(2.0 * math.pi))
    return cdf + x * pdf


def _sigmoid(x):
    return 1.0 / (1.0 + jnp.exp(-x))


def _operand_spec(shape, lead, blocked, tr, tc, ridx, cidx):
    if blocked:
        per = shape[-1] // tc
        assert shape[-1] % tc == 0, (shape, tc)
        return pl.BlockSpec(
            (None,) * (1 + len(lead)) + (tr, tc),
            lambda *g: (cidx(*g) // per,) + lead + (ridx(*g), cidx(*g) % per))
    return pl.BlockSpec((None,) * len(lead) + (tr, tc), lambda *g: lead + (ridx(*g), cidx(*g)))


def _view2d(x, blocked):
    return (x.shape[-2], x.shape[0] * x.shape[-1]) if blocked else (x.shape[-2], x.shape[-1])


def _mm(a, b, *, dims, out_dtype, name, tm, tn, tk=None, res=None, a_lead=(), b_lead=(),
        a_blocked=False, b_blocked=False, out_block=None, n_outer=False, after=None):
    ar, ac = _view2d(a, a_blocked)
    br, bc = _view2d(b, b_blocked)
    m, k = (ac, ar) if dims == 'tn' else (ar, ac)
    n, k2 = (br, bc) if dims == 'nt' else (bc, br)
    assert k == k2, (a.shape, b.shape, dims)
    tm, tn = min(tm, m), min(tn, n)
    tk = k if tk is None else tk
    assert m % tm == 0 and n % tn == 0 and k % tk == 0, (name, m, n, k, tm, tn, tk)
    nk = k // tk
    if n_outer:
        gi, gj = (lambda g0, g1, g2: g1), (lambda g0, g1, g2: g0)
        grid = (n // tn, m // tm, nk)
    else:
        gi, gj = (lambda g0, g1, g2: g0), (lambda g0, g1, g2: g1)
        grid = (m // tm, n // tn, nk)
    gk = lambda g0, g1, g2: g2

    if dims == 'tn':
        a_spec = _operand_spec(a.shape, a_lead, a_blocked, tk, tm, gk, gi)
    else:
        a_spec = _operand_spec(a.shape, a_lead, a_blocked, tm, tk, gi, gk)
    if dims == 'nt':
        b_spec = _operand_spec(b.shape, b_lead, b_blocked, tn, tk, gj, gk)
    else:
        b_spec = _operand_spec(b.shape, b_lead, b_blocked, tk, tn, gk, gj)
    in_specs = [a_spec, b_spec]
    operands = [a, b]
    if res is not None:
        in_specs.append(pl.BlockSpec((tm, tn), lambda *g: (gi(*g), gj(*g))))
        operands.append(res)
    if after is not None:
        in_specs.append(pl.BlockSpec(memory_space=pl.ANY))
        operands.append(after)
    n_in = len(operands)
    if out_block is not None:
        out_shape = jax.ShapeDtypeStruct((n // out_block, m, out_block), out_dtype)
        out_spec = _operand_spec(out_shape.shape, (), True, tm, tn, gi, gj)
    else:
        out_shape = jax.ShapeDtypeStruct((m, n), out_dtype)
        out_spec = pl.BlockSpec((tm, tn), lambda *g: (gi(*g), gj(*g)))

    def body(*refs):
        a_ref, b_ref = refs[0], refs[1]
        r_ref = refs[2] if res is not None else None
        o_ref = refs[n_in]
        acc_ref = refs[-1] if nk > 1 else None
        part = _dot(a_ref[...].astype(BF16), b_ref[...].astype(BF16), dims)

        def finish(total):
            if r_ref is not None:
                total = total + r_ref[...]
            o_ref[...] = total.astype(o_ref.dtype)

        if nk == 1:
            finish(part)
        else:
            kk = pl.program_id(2)

            @pl.when(kk == 0)
            def _():
                acc_ref[...] = part

            @pl.when(kk > 0)
            def _():
                acc_ref[...] += part

            @pl.when(kk == nk - 1)
            def _():
                finish(acc_ref[...])

    scratch = [pltpu.VMEM((tm, tn), F32)] if nk > 1 else []
    return pl.pallas_call(
        body, name=name, grid=grid, in_specs=in_specs, out_specs=out_spec,
        out_shape=out_shape, scratch_shapes=scratch,
        compiler_params=_params(("parallel", "parallel", "arbitrary")),
    )(*operands)


def _mm_blocked_nt(a, b, *, out_dtype, name, tm, tn, blocks_per_step, after=None):
    nb, m, bw = a.shape
    n = b.shape[1]
    tm, tn = min(tm, m), min(tn, n)
    assert nb % blocks_per_step == 0 and m % tm == 0 and n % tn == 0
    nk = nb // blocks_per_step

    def body(*refs):
        a_ref, b_ref, o_ref, acc_ref = refs[0], refs[1], refs[-2], refs[-1]
        kk = pl.program_id(2)
        part = _dot(a_ref[0], b_ref[0], 'nt')
        for t in range(1, blocks_per_step):
            part = part + _dot(a_ref[t], b_ref[t], 'nt')

        @pl.when(kk == 0)
        def _():
            acc_ref[...] = part

        @pl.when(kk > 0)
        def _():
            acc_ref[...] += part

        @pl.when(kk == nk - 1)
        def _():
            o_ref[...] = acc_ref[...].astype(o_ref.dtype)

    in_specs = [pl.BlockSpec((blocks_per_step, tm, bw), lambda i, j, k: (k, i, 0)),
                pl.BlockSpec((blocks_per_step, tn, bw), lambda i, j, k: (k, j, 0))]
    operands = [a, b]
    if after is not None:
        in_specs.append(pl.BlockSpec(memory_space=pl.ANY))
        operands.append(after)
    return pl.pallas_call(
        body, name=name, grid=(m // tm, n // tn, nk), in_specs=in_specs,
        out_specs=pl.BlockSpec((tm, tn), lambda i, j, k: (i, j)), out_shape=jax.ShapeDtypeStruct((m, n), out_dtype),
        scratch_shapes=[pltpu.VMEM((tm, tn), F32)],
        compiler_params=_params(("parallel", "parallel", "arbitrary")),
    )(*operands)


def _rmsnorm(x, g, *, name, width=None, out_dtype=BF16, tm=512):
    s = x.shape[0]
    w = x.shape[1] if width is None else width
    tm = min(tm, s)

    def body(x_ref, g_ref, o_ref):
        xv = x_ref[...].astype(F32)
        rstd = lax.rsqrt(jnp.mean(xv * xv, axis=-1, keepdims=True) + EPS)
        o_ref[...] = (xv * rstd * g_ref[...]).astype(o_ref.dtype)

    return pl.pallas_call(
        body, name=name, grid=(s // tm,),
        in_specs=[pl.BlockSpec((tm, w), lambda i: (i, 0)), pl.BlockSpec((1, w), lambda i: (0, 0))],
        out_specs=pl.BlockSpec((tm, w), lambda i: (i, 0)),
        out_shape=jax.ShapeDtypeStruct((s, w), out_dtype),
        compiler_params=_params(("parallel",)),
    )(x, g.reshape(1, w))


def _rmsnorm_bwd(x, g, dy, *, name, width=None, dres=None, after=None, out_dtype=F32, tm=512):
    s = x.shape[0]
    w = x.shape[1] if width is None else width
    tm = min(tm, s)

    def body(*refs):
        x_ref, g_ref, dy_ref = refs[0], refs[1], refs[2]
        r_ref = refs[3] if dres is not None else None
        dx_ref, dg_ref = refs[-2], refs[-1]
        xv = x_ref[...].astype(F32)
        rstd = lax.rsqrt(jnp.mean(xv * xv, axis=-1, keepdims=True) + EPS)
        xhat = xv * rstd
        dyv = dy_ref[...].astype(F32)
        gdy = dyv * g_ref[...]
        dx = rstd * (gdy - xhat * jnp.mean(gdy * xhat, axis=-1, keepdims=True))
        if r_ref is not None:
            dx = dx + r_ref[...]
        dx_ref[...] = dx.astype(dx_ref.dtype)
        part = jnp.sum(dyv * xhat, axis=0, keepdims=True)

        @pl.when(pl.program_id(0) == 0)
        def _():
            dg_ref[...] = part

        @pl.when(pl.program_id(0) > 0)
        def _():
            dg_ref[...] += part

    row = pl.BlockSpec((tm, w), lambda i: (i, 0))
    vec = pl.BlockSpec((1, w), lambda i: (0, 0))
    in_specs = [row, vec, row] + ([row] if dres is not None else [])
    operands = [x, g.reshape(1, w), dy] + ([dres] if dres is not None else [])
    if after is not None:
        in_specs.append(pl.BlockSpec(memory_space=pl.ANY))
        operands.append(after)
    return pl.pallas_call(
        body, name=name, grid=(s // tm,), in_specs=in_specs, out_specs=[row, vec],
        out_shape=[jax.ShapeDtypeStruct((s, w), out_dtype), jax.ShapeDtypeStruct((1, w), F32)],
        compiler_params=_params(("arbitrary",)),
    )(*operands)


def _final_loss(x, target, g, *, name, tm=256):
    s, d = x.shape
    tm = min(tm, s)

    def body(x_ref, t_ref, g_ref, sq_ref, dx_ref, dg_ref):
        xv = x_ref[...]
        rstd = lax.rsqrt(jnp.mean(xv * xv, axis=-1, keepdims=True) + EPS)
        xhat = xv * rstd
        err = xhat * g_ref[...] - t_ref[...]
        dyv = err * (1.0 / d)
        gdy = dyv * g_ref[...]
        dx_ref[...] = rstd * (gdy - xhat * jnp.mean(gdy * xhat, axis=-1, keepdims=True))
        sq = jnp.sum(err * err, axis=0, keepdims=True)
        dg = jnp.sum(dyv * xhat, axis=0, keepdims=True)

        @pl.when(pl.program_id(0) == 0)
        def _():
            sq_ref[...] = sq
            dg_ref[...] = dg

        @pl.when(pl.program_id(0) > 0)
        def _():
            sq_ref[...] += sq
            dg_ref[...] += dg

    row = pl.BlockSpec((tm, d), lambda i: (i, 0))
    vec = pl.BlockSpec((1, d), lambda i: (0, 0))
    return pl.pallas_call(
        body, name=name, grid=(s // tm,), in_specs=[row, row, vec], out_specs=[vec, row, vec],
        out_shape=[jax.ShapeDtypeStruct((1, d), F32), jax.ShapeDtypeStruct((s, d), F32),
                   jax.ShapeDtypeStruct((1, d), F32)],
        compiler_params=_params(("arbitrary",)),
    )(x, target, g.reshape(1, d))


def _tril_mask():
    t = lax.broadcasted_iota(jnp.int32, (CHUNK, CHUNK), 0)
    s = lax.broadcasted_iota(jnp.int32, (CHUNK, CHUNK), 1)
    return t >= s


def _sgu_fwd(z, g_v, w_sp, b_sp_t, *, name):
    s = z.shape[0]

    def body(zu_ref, zv_ref, g_ref, w_ref, b_ref, o_ref):
        u = _gelu(zu_ref[...].astype(F32))
        gv = _gelu(zv_ref[...].astype(F32))
        rstd = lax.rsqrt(jnp.mean(gv * gv, axis=-1, keepdims=True) + EPS)
        v = (gv * rstd * g_ref[...]).astype(BF16)
        mask = _tril_mask()
        for grp in range(G_HEADS):
            cols = slice(grp * HEAD, (grp + 1) * HEAD)
            wm = jnp.where(mask, w_ref[grp], 0.0).astype(BF16)
            sv = _dot(wm, v[:, cols], 'nn') + b_ref[:, grp:grp + 1]
            o_ref[:, cols] = (u[:, cols] * sv).astype(o_ref.dtype)

    return pl.pallas_call(
        body, name=name, grid=(s // CHUNK,),
        in_specs=[pl.BlockSpec((CHUNK, G_W), lambda i: (i, 0)),
                  pl.BlockSpec((CHUNK, G_W), lambda i: (i, 1)),
                  pl.BlockSpec((1, G_W), lambda i: (0, 0)),
                  pl.BlockSpec((G_HEADS, CHUNK, CHUNK), lambda i: (0, 0, 0)),
                  pl.BlockSpec((CHUNK, G_HEADS), lambda i: (0, 0))],
        out_specs=pl.BlockSpec((CHUNK, G_W), lambda i: (i, 0)),
        out_shape=jax.ShapeDtypeStruct((s, G_W), BF16),
        compiler_params=_params(("parallel",)),
    )(z, z, g_v.reshape(1, G_W), w_sp, b_sp_t)


def _sgu_bwd(z, dmix, dqm, g_v, w_sp, b_sp_t, *, name):
    s = z.shape[0]
    zw = z.shape[1]

    def body(zu_ref, zv_ref, dm_ref, dq_ref, g_ref, w_ref, b_ref, dz_ref, dw_ref, db_ref, dg_ref):
        first = pl.program_id(0) == 0

        @pl.when(first)
        def _():
            dw_ref[...] = jnp.zeros_like(dw_ref)
            db_ref[...] = jnp.zeros_like(db_ref)
            dg_ref[...] = jnp.zeros_like(dg_ref)

        zu = zu_ref[...].astype(F32)
        zv = zv_ref[...].astype(F32)
        dmain = dm_ref[...].astype(F32)
        u = _gelu(zu)
        gv = _gelu(zv)
        rstd = lax.rsqrt(jnp.mean(gv * gv, axis=-1, keepdims=True) + EPS)
        vhat = gv * rstd
        gvec = g_ref[...]
        v = (vhat * gvec).astype(BF16)
        dsv = dmain * u
        dsv_b = dsv.astype(BF16)
        mask = _tril_mask()
        dv_parts = []
        for grp in range(G_HEADS):
            cols = slice(grp * HEAD, (grp + 1) * HEAD)
            wm = jnp.where(mask, w_ref[grp], 0.0).astype(BF16)
            sv = _dot(wm, v[:, cols], 'nn') + b_ref[:, grp:grp + 1]
            dz_ref[:, cols] = (dmain[:, cols] * sv * _gelu_grad(zu[:, cols])).astype(dz_ref.dtype)
            dwg = _dot(dsv_b[:, cols], v[:, cols], 'nt')
            dw_ref[grp] += jnp.where(mask, dwg, 0.0)
            db_ref[:, grp:grp + 1] += jnp.sum(dsv[:, cols], axis=-1, keepdims=True)
            dv_parts.append(_dot(wm, dsv_b[:, cols], 'tn'))
        dv = jnp.concatenate(dv_parts, axis=-1)
        dg_ref[...] += jnp.sum(dv * vhat, axis=0, keepdims=True)
        gdv = dv * gvec
        dgv = rstd * (gdv - vhat * jnp.mean(gdv * vhat, axis=-1, keepdims=True))
        dz_ref[:, G_W:2 * G_W] = (dgv * _gelu_grad(zv)).astype(dz_ref.dtype)
        dz_ref[:, 2 * G_W:] = dq_ref[...].astype(dz_ref.dtype)

    return pl.pallas_call(
        body, name=name, grid=(s // CHUNK,),
        in_specs=[pl.BlockSpec((CHUNK, G_W), lambda i: (i, 0)),
                  pl.BlockSpec((CHUNK, G_W), lambda i: (i, 1)),
                  pl.BlockSpec((CHUNK, G_W), lambda i: (i, 0)),
                  pl.BlockSpec((CHUNK, MEM_W), lambda i: (i, 0)),
                  pl.BlockSpec((1, G_W), lambda i: (0, 0)),
                  pl.BlockSpec((G_HEADS, CHUNK, CHUNK), lambda i: (0, 0, 0)),
                  pl.BlockSpec((CHUNK, G_HEADS), lambda i: (0, 0))],
        out_specs=[pl.BlockSpec((CHUNK, zw), lambda i: (i, 0)),
                   pl.BlockSpec((G_HEADS, CHUNK, CHUNK), lambda i: (0, 0, 0)),
                   pl.BlockSpec((CHUNK, G_HEADS), lambda i: (0, 0)),
                   pl.BlockSpec((1, G_W), lambda i: (0, 0))],
        out_shape=[jax.ShapeDtypeStruct((s, zw), BF16),
                   jax.ShapeDtypeStruct((G_HEADS, CHUNK, CHUNK), F32),
                   jax.ShapeDtypeStruct((CHUNK, G_HEADS), F32),
                   jax.ShapeDtypeStruct((1, G_W), F32)],
        compiler_params=_params(("arbitrary",)),
    )(z, z, dmix, dqm, g_v.reshape(1, G_W), w_sp, b_sp_t)


def _mem_probs(q, k):
    sc = _dot(q, k, 'nt') * (HEAD ** -0.5)
    sc = sc - jnp.max(sc, axis=-1, keepdims=True)
    e = jnp.exp(sc)
    return e / jnp.sum(e, axis=-1, keepdims=True)


def _memattn_fwd(z, kvm, main, *, qcol, name, tm=512):
    s = z.shape[0]
    m = kvm.shape[0]
    tm = min(tm, s)

    def body(q_ref, kv_ref, main_ref, o_ref):
        o_ref[:, :G_W] = main_ref[...]
        for h in range(MEM_HEADS):
            cols = slice(h * HEAD, (h + 1) * HEAD)
            k = kv_ref[:, cols]
            v = kv_ref[:, MEM_W + h * HEAD:MEM_W + (h + 1) * HEAD]
            p = _mem_probs(q_ref[:, cols], k)
            o_ref[:, G_W + h * HEAD:G_W + (h + 1) * HEAD] = _dot(p.astype(BF16), v, 'nn').astype(o_ref.dtype)

    return pl.pallas_call(
        body, name=name, grid=(s // tm,),
        in_specs=[pl.BlockSpec((tm, MEM_W), lambda i: (i, qcol)),
                  pl.BlockSpec((m, 2 * MEM_W), lambda i: (0, 0)),
                  pl.BlockSpec((tm, G_W), lambda i: (i, 0))],
        out_specs=pl.BlockSpec((tm, G_W + MEM_W), lambda i: (i, 0)),
        out_shape=jax.ShapeDtypeStruct((s, G_W + MEM_W), BF16),
        compiler_params=_params(("parallel",)),
    )(z, kvm, main)


def _memattn_bwd(z, kvm, dmix, *, qcol, name, tm=512):
    s = z.shape[0]
    m = kvm.shape[0]
    tm = min(tm, s)
    scale = HEAD ** -0.5

    def body(q_ref, kv_ref, do_ref, dq_ref, dkv_ref):
        @pl.when(pl.program_id(0) == 0)
        def _():
            dkv_ref[...] = jnp.zeros_like(dkv_ref)

        for h in range(MEM_HEADS):
            cols = slice(h * HEAD, (h + 1) * HEAD)
            vcols = slice(MEM_W + h * HEAD, MEM_W + (h + 1) * HEAD)
            q = q_ref[:, cols]
            k = kv_ref[:, cols]
            v = kv_ref[:, vcols]
            do = do_ref[:, cols]
            p = _mem_probs(q, k)
            dp = _dot(do, v, 'nt')
            ds = (p * (dp - jnp.sum(dp * p, axis=-1, keepdims=True)) * scale).astype(BF16)
            dq_ref[:, cols] = _dot(ds, k, 'nn').astype(dq_ref.dtype)
            dkv_ref[:, cols] += _dot(ds, q, 'tn')
            dkv_ref[:, vcols] += _dot(p.astype(BF16), do, 'tn')

    mo_block = G_W // MEM_W
    return pl.pallas_call(
        body, name=name, grid=(s // tm,),
        in_specs=[pl.BlockSpec((tm, MEM_W), lambda i: (i, qcol)),
                  pl.BlockSpec((m, 2 * MEM_W), lambda i: (0, 0)),
                  pl.BlockSpec((tm, MEM_W), lambda i: (i, mo_block))],
        out_specs=[pl.BlockSpec((tm, MEM_W), lambda i: (i, 0)),
                   pl.BlockSpec((m, 2 * MEM_W), lambda i: (0, 0))],
        out_shape=[jax.ShapeDtypeStruct((s, MEM_W), BF16), jax.ShapeDtypeStruct((m, 2 * MEM_W), F32)],
        compiler_params=_params(("arbitrary",)),
    )(z, kvm, dmix)


def _rope(x1, x2, cos, sin, *, name, inverse=False, out_dtype=BF16, col1=0, col2=0, tm=512):
    s, w = cos.shape
    tm = min(tm, s)
    sign = -1.0 if inverse else 1.0

    def body(a_ref, b_ref, c_ref, s_ref, o1_ref, o2_ref):
        a = a_ref[...].astype(F32)
        b = b_ref[...].astype(F32)
        c = c_ref[...]
        sn = s_ref[...] * sign
        o1_ref[...] = (a * c - b * sn).astype(o1_ref.dtype)
        o2_ref[...] = (b * c + a * sn).astype(o2_ref.dtype)

    row = pl.BlockSpec((tm, w), lambda i: (i, 0))
    return pl.pallas_call(
        body, name=name, grid=(s // tm,),
        in_specs=[pl.BlockSpec((tm, w), lambda i: (i, col1)), pl.BlockSpec((tm, w), lambda i: (i, col2)), row, row],
        out_specs=[row, row],
        out_shape=[jax.ShapeDtypeStruct((s, w), out_dtype)] * 2,
        compiler_params=_params(("parallel",)),
    )(x1, x2, cos, sin)


MHA_BLOCK = 1024
QK_DIM = HEAD + ROPE_DIM


def _mha_scores(q, k, scale, diagonal):
    sc = _dot(q, k, 'nt') * scale
    if not diagonal:
        return sc, None
    rows = lax.broadcasted_iota(jnp.int32, sc.shape, 0)
    cols = lax.broadcasted_iota(jnp.int32, sc.shape, 1)
    return sc, cols <= rows


def _mha_fwd(q, k, vv, *, name):
    s = k.shape[1]
    tb = min(MHA_BLOCK, s)
    scale = QK_DIM ** -0.5

    def body(q_ref, k_ref, v_ref, o_ref, lse_ref, m_ref, l_ref, acc_ref):
        i = pl.program_id(1)
        qh = q_ref[...]
        m_ref[...] = jnp.full_like(m_ref, MASK_VALUE)
        l_ref[...] = jnp.zeros_like(l_ref)
        acc_ref[...] = jnp.zeros_like(acc_ref)

        def block(j, diagonal):
            ks = pl.multiple_of(j * tb, tb)
            kj, vj = k_ref[pl.ds(ks, tb), :], v_ref[pl.ds(ks, tb), :]
            sc, keep = _mha_scores(qh, kj, scale, diagonal)
            if diagonal:
                sc = jnp.where(keep, sc, MASK_VALUE)
            m_old = m_ref[...]
            m_new = jnp.maximum(m_old, jnp.max(sc, axis=-1, keepdims=True))
            p = jnp.exp(sc - m_new)
            alpha = jnp.exp(m_old - m_new)
            l_ref[...] = alpha * l_ref[...] + jnp.sum(p, axis=-1, keepdims=True)
            acc_ref[...] = alpha * acc_ref[...] + _dot(p.astype(BF16), vj, 'nn')
            m_ref[...] = m_new

        def step(j, carry):
            block(j, False)
            return carry

        lax.fori_loop(0, i, step, 0)
        block(i, True)
        l = l_ref[...]
        o_ref[...] = (acc_ref[...] / l).astype(o_ref.dtype)
        lse_ref[...] = m_ref[...] + jnp.log(l)

    return pl.pallas_call(
        body, name=name, grid=(G_HEADS, s // tb),
        in_specs=[pl.BlockSpec((None, tb, QK_DIM), lambda h, i: (h, i, 0)),
                  pl.BlockSpec((None, s, QK_DIM), lambda h, i: (h, 0, 0)),
                  pl.BlockSpec((s, HEAD), lambda h, i: (0, h))],
        out_specs=[pl.BlockSpec((tb, HEAD), lambda h, i: (i, h)),
                   pl.BlockSpec((None, tb, 1), lambda h, i: (h, i, 0))],
        out_shape=[jax.ShapeDtypeStruct((s, G_W), BF16), jax.ShapeDtypeStruct((G_HEADS, s, 1), F32)],
        scratch_shapes=[pltpu.VMEM((tb, 1), F32), pltpu.VMEM((tb, 1), F32), pltpu.VMEM((tb, HEAD), F32)],
        compiler_params=_params(("parallel", "arbitrary")),
    )(q, k, vv)


def _mha_bwd(q, k, vv, o, do, lse, *, name):
    s = k.shape[1]
    tb = min(MHA_BLOCK, s)
    nq = s // tb
    scale = QK_DIM ** -0.5

    def body(q_ref, k_ref, v_ref, o_ref, do_ref, lse_ref, dq_ref, dk_ref, dv_ref, dqa_ref, dka_ref, dva_ref):
        i = pl.program_id(1)

        @pl.when(i == 0)
        def _():
            dka_ref[...] = jnp.zeros_like(dka_ref)
            dva_ref[...] = jnp.zeros_like(dva_ref)

        qh, dov = q_ref[...], do_ref[...]
        delta = jnp.sum(dov.astype(F32) * o_ref[...].astype(F32), axis=-1, keepdims=True)
        lsev = lse_ref[...]
        dqa_ref[...] = jnp.zeros_like(dqa_ref)

        def block(j, diagonal):
            ks = pl.multiple_of(j * tb, tb)
            kj, vj = k_ref[pl.ds(ks, tb), :], v_ref[pl.ds(ks, tb), :]
            sc, keep = _mha_scores(qh, kj, scale, diagonal)
            p = jnp.exp(sc - lsev)
            if diagonal:
                p = jnp.where(keep, p, 0.0)
            dp = _dot(dov, vj, 'nt')
            ds = (p * (dp - delta) * scale).astype(BF16)
            dqa_ref[...] += _dot(ds, kj, 'nn')
            dka_ref[pl.ds(ks, tb), :] += _dot(ds, qh, 'tn')
            dva_ref[pl.ds(ks, tb), :] += _dot(p.astype(BF16), dov, 'tn')

        def step(j, carry):
            block(j, False)
            return carry

        lax.fori_loop(0, i, step, 0)
        block(i, True)
        dq_ref[...] = dqa_ref[...].astype(dq_ref.dtype)

        @pl.when(i == nq - 1)
        def _():
            dk_ref[...] = dka_ref[...].astype(dk_ref.dtype)
            dv_ref[...] = dva_ref[...].astype(dv_ref.dtype)

    q_tile = pl.BlockSpec((None, tb, QK_DIM), lambda h, i: (h, i, 0))
    k_head = pl.BlockSpec((None, s, QK_DIM), lambda h, i: (h, 0, 0))
    tile = pl.BlockSpec((tb, HEAD), lambda h, i: (i, h))
    v_head = pl.BlockSpec((s, HEAD), lambda h, i: (0, h))
    return pl.pallas_call(
        body, name=name, grid=(G_HEADS, nq),
        in_specs=[q_tile, k_head, v_head, tile, tile, pl.BlockSpec((None, tb, 1), lambda h, i: (h, i, 0))],
        out_specs=[q_tile, k_head, v_head],
        out_shape=[jax.ShapeDtypeStruct((G_HEADS, s, QK_DIM), BF16), jax.ShapeDtypeStruct((G_HEADS, s, QK_DIM), BF16),
                   jax.ShapeDtypeStruct((s, G_W), BF16)],
        scratch_shapes=[pltpu.VMEM((tb, QK_DIM), F32), pltpu.VMEM((s, QK_DIM), F32), pltpu.VMEM((s, HEAD), F32)],
        compiler_params=_params(("parallel", "arbitrary")),
    )(q, k, vv, o, do, lse)


HALO = 16


def _shift_down(prev, cur, shift, first_tile):
    tr = cur.shape[0]
    full = jnp.concatenate([prev, cur], axis=0)
    out = pltpu.roll(full, shift, axis=0)[HALO:]
    row = lax.broadcasted_iota(jnp.int32, (tr, 1), 0)
    return jnp.where(jnp.logical_and(first_tile, row < shift), 0.0, out)


def _shift_up(cur, nxt, shift, last_tile):
    tr = cur.shape[0]
    full = jnp.concatenate([cur, nxt], axis=0)
    out = pltpu.roll(full, tr + HALO - shift, axis=0)[:tr]
    row = lax.broadcasted_iota(jnp.int32, (tr, 1), 0)
    return jnp.where(jnp.logical_and(last_tile, row >= tr - shift), 0.0, out)


def _lane_chunks(width, lanes):
    return [slice(c0, min(c0 + lanes, width)) for c0 in range(0, width, lanes)]


def _conv_taps(prev_ref, cur_ref, cw_ref, cb_ref, first_tile, cs):
    cur = cur_ref[:, cs].astype(F32)
    prev = prev_ref[:, cs].astype(F32)
    a1 = _shift_down(prev, cur, 1, first_tile)
    a2 = _shift_down(prev, cur, 2, first_tile)
    c = a2 * cw_ref[0:1, cs] + a1 * cw_ref[1:2, cs] + cur * cw_ref[2:3, cs] + cb_ref[:, cs]
    return c, (a2, a1, cur)


def _conv_in_specs(tr, bw, half, layer, row_of, blk_of):
    per = tr // HALO
    specs = []
    for off in (0, half):
        specs.append(pl.BlockSpec((None, HALO, bw), lambda *g, off=off: (blk_of(*g) + off, jnp.maximum(row_of(*g) * per - 1, 0), 0)))
        specs.append(pl.BlockSpec((None, tr, bw), lambda *g, off=off: (blk_of(*g) + off, row_of(*g), 0)))
    for off in (0, half):
        specs.append(pl.BlockSpec((None, None, CONV_W, bw), lambda *g, off=off: (blk_of(*g) + off, layer, 0, 0)))
    for off in (0, half):
        specs.append(pl.BlockSpec((None, 1, bw), lambda *g, off=off: (layer * 2 * half + blk_of(*g) + off, 0, 0)))
    return specs


def _conv_fwd(a, cw, cb, layer, *, name, tr=256):
    nb, s, bw = a.shape
    half = nb // 2
    tr = min(tr, s)

    def body(gp_ref, gc_ref, vp_ref, vc_ref, cwg_ref, cwv_ref, cbg_ref, cbv_ref, o_ref):
        first = pl.program_id(0) == 0
        for cs in _lane_chunks(bw, 256):
            gate, _ = _conv_taps(gp_ref, gc_ref, cwg_ref, cbg_ref, first, cs)
            val, _ = _conv_taps(vp_ref, vc_ref, cwv_ref, cbv_ref, first, cs)
            o_ref[:, cs] = (gate * _sigmoid(gate) * val).astype(o_ref.dtype)

    return pl.pallas_call(
        body, name=name, grid=(s // tr, half),
        in_specs=_conv_in_specs(tr, bw, half, layer, lambda i, j: i, lambda i, j: j),
        out_specs=pl.BlockSpec((tr, bw), lambda i, j: (i, j)),
        out_shape=jax.ShapeDtypeStruct((s, half * bw), BF16),
        compiler_params=_params(("parallel", "parallel")),
    )(a, a, a, a, cw, cw, cb, cb)


def _conv_bwd_dc(a, dact, cw, cb, layer, *, name, after=None, tr=256):
    nb, s, bw = a.shape
    half = nb // 2
    tr = min(tr, s)

    def body(*refs):
        gp_ref, gc_ref, vp_ref, vc_ref, cwg_ref, cwv_ref, cbg_ref, cbv_ref, da_ref = refs[:9]
        dc_ref, dw_ref, db_ref = refs[-3:]
        first = pl.program_id(1) == 0

        @pl.when(first)
        def _():
            dw_ref[...] = jnp.zeros_like(dw_ref)
            db_ref[...] = jnp.zeros_like(db_ref)

        for cs in _lane_chunks(bw, 128):
            gate, gtaps = _conv_taps(gp_ref, gc_ref, cwg_ref, cbg_ref, first, cs)
            val, vtaps = _conv_taps(vp_ref, vc_ref, cwv_ref, cbv_ref, first, cs)
            dact_v = da_ref[:, cs].astype(F32)
            sg = _sigmoid(gate)
            dgate = dact_v * val * (sg * (1.0 + gate * (1.0 - sg)))
            dval = dact_v * (gate * sg)
            dc_ref[0, :, cs] = dgate.astype(dc_ref.dtype)
            dc_ref[1, :, cs] = dval.astype(dc_ref.dtype)
            for kk in range(CONV_W):
                dw_ref[0, kk:kk + 1, cs] += jnp.sum(dgate * gtaps[kk], axis=0, keepdims=True)
                dw_ref[1, kk:kk + 1, cs] += jnp.sum(dval * vtaps[kk], axis=0, keepdims=True)
            db_ref[0, :, cs] += jnp.sum(dgate, axis=0, keepdims=True)
            db_ref[1, :, cs] += jnp.sum(dval, axis=0, keepdims=True)

    outs = pl.pallas_call(
        body, name=name, grid=(half, s // tr),
        in_specs=_conv_in_specs(tr, bw, half, layer, lambda j, i: i, lambda j, i: j)
        + [pl.BlockSpec((tr, bw), lambda j, i: (i, j))]
        + ([pl.BlockSpec(memory_space=pl.ANY)] if after is not None else []),
        out_specs=[pl.BlockSpec((2, None, tr, bw), lambda j, i: (0, j, i, 0)),
                   pl.BlockSpec((2, None, CONV_W, bw), lambda j, i: (0, j, 0, 0)),
                   pl.BlockSpec((2, None, 1, bw), lambda j, i: (0, j, 0, 0))],
        out_shape=[jax.ShapeDtypeStruct((2, half, s, bw), BF16),
                   jax.ShapeDtypeStruct((2, half, CONV_W, bw), F32),
                   jax.ShapeDtypeStruct((2, half, 1, bw), F32)],
        compiler_params=_params(("parallel", "arbitrary")),
    )(a, a, a, a, cw, cw, cb, cb, dact, *([after] if after is not None else []))
    dc, dw, db = outs
    return dc.reshape(nb, s, bw), dw.reshape(nb, CONV_W, bw), db.reshape(nb, 1, bw)


def _conv_bwd_da(dc, cw, layer, *, name, tr=512):
    nb, s, bw = dc.shape
    tr = min(tr, s)
    ni = s // tr
    per = tr // HALO
    last_halo = s // HALO - 1

    def body(c_ref, n_ref, w_ref, o_ref):
        last = pl.program_id(0) == ni - 1
        for cs in _lane_chunks(bw, 256):
            cur = c_ref[:, cs].astype(F32)
            nxt = n_ref[:, cs].astype(F32)
            da = (cur * w_ref[2:3, cs] + _shift_up(cur, nxt, 1, last) * w_ref[1:2, cs]
                  + _shift_up(cur, nxt, 2, last) * w_ref[0:1, cs])
            o_ref[:, cs] = da.astype(o_ref.dtype)

    tile = pl.BlockSpec((None, tr, bw), lambda i, j: (j, i, 0))
    return pl.pallas_call(
        body, name=name, grid=(ni, nb),
        in_specs=[tile,
                  pl.BlockSpec((None, HALO, bw), lambda i, j: (j, jnp.minimum((i + 1) * per, last_halo), 0)),
                  pl.BlockSpec((None, None, CONV_W, bw), lambda i, j: (j, layer, 0, 0))],
        out_specs=tile,
        out_shape=jax.ShapeDtypeStruct((nb, s, bw), BF16),
        compiler_params=_params(("parallel", "parallel")),
    )(dc, dc, cw)


def _rope_tables(positions):
    inv = 1.0 / (ROPE_THETA ** (jnp.arange(0, ROPE_DIM, 2, dtype=F32) / ROPE_DIM))
    ang = positions.astype(F32)[:, None] * inv
    return jnp.cos(ang), jnp.sin(ang)


def _heads_to_major(nope, r1, r2):
    s = r1.shape[0]
    parts = [nope[:, :G_W].reshape(s, G_HEADS, HEAD)]
    for r in (r1, r2):
        parts.append(jnp.broadcast_to(r.reshape(s, -1, ROPE_HALF), (s, G_HEADS, ROPE_HALF)))
    return jnp.concatenate(parts, axis=-1).transpose(1, 0, 2)


def _heads_from_major(t):
    s = t.shape[1]
    t = t.transpose(1, 0, 2)
    return t[:, :, :HEAD].reshape(s, G_W), t[:, :, HEAD:HEAD + ROPE_HALF], t[:, :, HEAD + ROPE_HALF:]


def _local_step(x, mem, positions, target, rep, fetch, emit):
    s, d = x.shape
    n_b = DEPTH - N_A
    tm = min(1024, s)
    cos, sin = _rope_tables(positions)
    cos12 = jnp.tile(cos, (1, G_HEADS))
    sin12 = jnp.tile(sin, (1, G_HEADS))
    r1_col = G_W // (G_HEADS * ROPE_HALF)
    b_sp_t = rep['b_sp'].transpose(0, 2, 1)

    saved = []
    kv = None
    shared = None
    for l in range(DEPTH):
        wm = fetch(('in', l), x)
        if l == 0:
            shared = {'g_v': wm['g_v'], 'conv_w': wm['conv_w']}
            bw = shared['conv_w'].shape[-1]
            conv_b = rep['conv_b'].reshape(-1, 1, bw)
        sv = {'x_in': x, 'wm': wm}
        if l == N_A:
            xn_kv = _rmsnorm(x, rep['g_kv'], name="kvnorm")
            kvx = _mm(xn_kv, wm['w_kv_a'], dims='nn', out_dtype=F32, name="kvproj", tm=tm, tn=KV_PAD)
            ckv = _rmsnorm(kvx, rep['g_kv_lat'], width=KV_RANK, name="ckvnorm")
            k1, k2 = _rope(kvx[:, KV_RANK:KV_RANK + ROPE_HALF], kvx[:, KV_RANK + ROPE_HALF:KV_RANK + ROPE_DIM],
                           cos, sin, name="krope")
            kv = {'x': x, 'xn': xn_kv, 'kvx': kvx, 'ckv': ckv, 'k1': k1, 'k2': k2, 'w_kv_a': wm['w_kv_a']}
        h = _rmsnorm(x, rep['g_mix'][l], name=f"mixnorm{l}")
        if l < N_A:
            z = _mm(h, wm['w_in'], dims='nn', out_dtype=BF16, name=f"in_a{l}", tm=tm, tn=wm['w_in'].shape[1] // 2)
            main = _sgu_fwd(z, shared['g_v'][l], rep['w_sp'][l], b_sp_t[l], name=f"sgu{l}")
            qcol = 2 * G_W // MEM_W
        else:
            j = l - N_A
            z = _mm(h, wm['w_in'], dims='nn', out_dtype=BF16, name=f"in_b{j}", tm=tm, tn=1024)
            qn = _rmsnorm(z, rep['g_q_lat'][j], width=Q_RANK, name=f"qnorm{j}")
            qp = _mm(qn, wm['w_uqp'], dims='nn', out_dtype=BF16, name=f"uq{j}", tm=tm, tn=768)
            rr1, rr2 = _rope(qp, qp, cos12, sin12, col1=r1_col, col2=r1_col + 1, name=f"qrope{j}")
            qh = _heads_to_major(qp, rr1, rr2)
            kn = _mm(kv['ckv'], wm['w_uk'], dims='nn', out_dtype=BF16, name=f"k_up{j}", tm=tm, tn=768)
            kh = _heads_to_major(kn, kv['k1'], kv['k2'])
            vv = _mm(kv['ckv'], wm['w_uv'], dims='nn', out_dtype=BF16, name=f"v_up{j}", tm=tm, tn=768)
            main, lse = _mha_fwd(qh, kh, vv, name=f"mha{j}")
            qcol = Q_RANK // MEM_W
            sv.update(qn=qn, qh=qh, kh=kh, vv=vv, lse=lse)
        wm.update(fetch(('rest', l), z))
        memn = _rmsnorm(mem, rep['g_mem'][l], name=f"memnorm{l}")
        kvm = _mm(memn, wm['w_mem_kv'], dims='nn', out_dtype=BF16, name=f"memkv{l}", tm=tm, tn=1024)
        mix = _memattn_fwd(z, kvm, main, qcol=qcol, name=f"memattn{l}")
        x_mid = _mm(mix, wm['w_out'], dims='nn', res=x, out_dtype=F32, name=f"out{l}", tm=tm, tn=1024)
        wf = fetch(('up', l), x_mid)
        h2 = _rmsnorm(x_mid, rep['g_ffn'][l], name=f"ffnnorm{l}")
        a = _mm(h2, wf['w_up'], dims='nn', b_blocked=True, out_dtype=BF16, out_block=bw,
                name=f"up{l}", tm=tm, tn=bw)
        act = _conv_fwd(a, shared['conv_w'], conv_b, l, name=f"conv{l}")
        wf.update(fetch(('down', l), act))
        x = _mm(act, wf['w_down'], dims='nn', res=x_mid, out_dtype=F32, name=f"down{l}", tm=512, tn=1024)
        sv.update(h=h, memn=memn, kvm=kvm, z=z, qcol=qcol, mix=mix, x_mid=x_mid, h2=h2, a=a, act=act, wf=wf)
        saved.append(sv)

    sq, dx, dg_final = _final_loss(x, target, rep['g_final'], name="loss")

    g = {k: [None] * DEPTH for k in ('g_mix', 'g_ffn', 'g_mem', 'conv_w', 'conv_b')}
    for k in ('g_v', 'w_sp', 'b_sp'):
        g[k] = [None] * N_A
    g['g_q_lat'] = [None] * n_b
    g['g_final'] = dg_final
    dckv_sum, dkr_sum = None, None

    for l in reversed(range(DEPTH)):
        sv = saved[l]
        wm, wf = sv['wm'], sv['wf']
        dact = _mm(dx, wf['w_down'], dims='nt', out_dtype=BF16, name=f"d_act{l}", tm=tm, tn=bw)
        dw_down = _mm(sv['act'], dx, dims='tn', out_dtype=BF16, name=f"dw_down{l}", tm=bw, tn=512)
        tok = emit(('down', l), {'w_ffn_down': dw_down})
        dc, dcw, dcb = _conv_bwd_dc(sv['a'], dact, shared['conv_w'], conv_b, l, after=tok, name=f"d_conv{l}")
        g['conv_w'][l], g['conv_b'][l] = dcw, dcb
        da = _conv_bwd_da(dc, shared['conv_w'], l, name=f"d_convin{l}")
        dw_up = _mm(sv['h2'], da, dims='tn', b_blocked=True, out_dtype=BF16, out_block=bw,
                    name=f"dw_up{l}", tm=512, tn=bw, n_outer=True)
        tok = emit(('up', l), {'w_ffn_up': dw_up})
        dh2 = _mm_blocked_nt(da, wf['w_up'], out_dtype=BF16, name=f"d_h2{l}", tm=512, tn=1024, blocks_per_step=4,
                             after=tok)
        dx_mid, g['g_ffn'][l] = _rmsnorm_bwd(sv['x_mid'], rep['g_ffn'][l], dh2, dres=dx, name=f"d_ffnnorm{l}")
        dmix = _mm(dx_mid, wm['w_out'], dims='nt', out_dtype=BF16, name=f"d_mix{l}", tm=tm, tn=1024)
        dw_out = _mm(sv['mix'], dx_mid, dims='tn', out_dtype=BF16, name=f"dw_out{l}", tm=1024, tn=512)
        dqm, dkvm = _memattn_bwd(sv['z'], sv['kvm'], dmix, qcol=sv['qcol'], name=f"d_memattn{l}")
        dw_memkv = _mm(sv['memn'], dkvm, dims='tn', out_dtype=BF16, name=f"dw_memkv{l}", tm=1024, tn=1024)
        tok = emit(('rest', l), {'w_out': dw_out, 'w_mem_kv': dw_memkv})
        gm = {}
        dmemn = _mm(dkvm, wm['w_mem_kv'], dims='nt', out_dtype=F32, name=f"d_memn{l}", tm=tm, tn=1024, after=tok)
        _, g['g_mem'][l] = _rmsnorm_bwd(mem, rep['g_mem'][l], dmemn, out_dtype=BF16, name=f"d_memnorm{l}")
        if l < N_A:
            dz, dwsp, dbsp_t, dgv = _sgu_bwd(sv['z'], dmix, dqm, shared['g_v'][l], rep['w_sp'][l], b_sp_t[l],
                                             name=f"d_sgu{l}")
            g['w_sp'][l], g['b_sp'][l], g['g_v'][l] = dwsp, dbsp_t.T, dgv
            dh = _mm(dz, wm['w_in'], dims='nt', out_dtype=BF16, name=f"d_h_a{l}", tm=tm, tn=1024)
            gm['w_in_a'] = _mm(sv['h'], dz, dims='tn', out_dtype=BF16, name=f"dw_in_a{l}", tm=1024, tn=512)
        else:
            j = l - N_A
            dqh, dkh, dvv = _mha_bwd(sv['qh'], sv['kh'], sv['vv'], sv['mix'], dmix, sv['lse'], name=f"d_mha{j}")
            dq_nope, dr1, dr2 = _heads_from_major(dqh)
            dkn, dk1, dk2 = _heads_from_major(dkh)
            dkr = jnp.concatenate([dk1.astype(F32).sum(axis=1), dk2.astype(F32).sum(axis=1)], axis=-1)
            gm['w_uk'] = _mm(kv['ckv'], dkn, dims='tn', out_dtype=BF16, name=f"dw_uk{j}", tm=512, tn=768)
            gm['w_uv'] = _mm(kv['ckv'], dvv, dims='tn', out_dtype=BF16, name=f"dw_uv{j}", tm=512, tn=768)
            dckv = _mm(dkn, wm['w_uk'], dims='nt', out_dtype=F32, res=dckv_sum, name=f"d_ckv_k{j}", tm=tm, tn=512)
            dckv_sum = _mm(dvv, wm['w_uv'], dims='nt', out_dtype=F32, res=dckv, name=f"d_ckv_v{j}", tm=tm, tn=512)
            dkr_sum = dkr if dkr_sum is None else dkr_sum + dkr
            dq1, dq2 = _rope(dr1.reshape(s, -1), dr2.reshape(s, -1), cos12, sin12, inverse=True, name=f"d_qrope{j}")
            dqp = jnp.concatenate([dq_nope, dq1, dq2], axis=-1)
            dqn = _mm(dqp, wm['w_uqp'], dims='nt', out_dtype=BF16, name=f"d_qn{j}", tm=tm, tn=512)
            gm['w_uqp'] = _mm(sv['qn'], dqp, dims='tn', out_dtype=BF16, name=f"dw_uq{j}", tm=512, tn=768)
            dqlat, g['g_q_lat'][j] = _rmsnorm_bwd(sv['z'], rep['g_q_lat'][j], dqn, width=Q_RANK, out_dtype=BF16,
                                                 name=f"d_qnorm{j}")
            dz = jnp.concatenate([dqlat, dqm], axis=-1)
            dh = _mm(dz, wm['w_in'], dims='nt', out_dtype=BF16, name=f"d_h_b{j}", tm=tm, tn=1024)
            gm['w_in_b'] = _mm(sv['h'], dz, dims='tn', out_dtype=BF16, name=f"dw_in_b{j}", tm=1024, tn=512)
        tok = emit(('mix', l), gm)
        dx, g['g_mix'][l] = _rmsnorm_bwd(sv['x_in'], rep['g_mix'][l], dh, dres=dx_mid, after=tok, name=f"d_mixnorm{l}")
        if l == N_A:
            dkvx_c, g['g_kv_lat'] = _rmsnorm_bwd(kv['kvx'], rep['g_kv_lat'], dckv_sum, width=KV_RANK, out_dtype=BF16,
                                                 name="d_ckvnorm")
            dk1, dk2 = _rope(dkr_sum[:, :ROPE_HALF], dkr_sum[:, ROPE_HALF:], cos, sin, inverse=True, name="d_krope")
            dkvx = jnp.concatenate([dkvx_c, dk1, dk2, jnp.zeros((s, KV_PAD - KV_RANK - ROPE_DIM), BF16)], axis=-1)
            dxn = _mm(dkvx, kv['w_kv_a'], dims='nt', out_dtype=BF16, name="d_kvnorm_in", tm=tm, tn=1024)
            dw_kv = _mm(kv['xn'], dkvx, dims='tn', out_dtype=BF16, name="dw_kv", tm=1024, tn=KV_PAD)
            tok = emit(('kv', 0), {'w_kv_a': dw_kv})
            dx, g['g_kv'] = _rmsnorm_bwd(kv['x'], rep['g_kv'], dxn, dres=dx, after=tok, name="d_kvnorm")
    return jnp.sum(sq), dx, g


MESH_IDS = pl.DeviceIdType.MESH
PEER_MASKS = tuple((k >> 2 & 1, k >> 1 & 1, k & 1) for k in range(1, N_DEV))
CHIP_MASKS = ((1, 0), (0, 1), (1, 1))
N_PEER = N_DEV - 1
SEMS_PER_BUFFER = 2 * N_PEER + 1
DATAFLOW = pltpu.SideEffectType.DATAFLOW_SIDE_EFFECTING
HBM_SPEC = pl.BlockSpec(memory_space=pltpu.HBM)
SEM_SPEC = pl.BlockSpec(memory_space=pltpu.SEMAPHORE)


def _my_position():
    return lax.axis_index("x"), lax.axis_index("y"), lax.axis_index("c")


def _flip(pos, mask):
    return tuple(1 - p if f else p for p, f in zip(pos, mask))


def _linear_id(pos):
    return 4 * pos[0] + 2 * pos[1] + pos[2]


def _hbm(x):
    return pltpu.with_memory_space_constraint(x, pltpu.HBM)


def _buffer_copies(src_ref, lead, land_ref, sems, scatter, near=False):
    me = _my_position()
    my_id = _linear_id(me)
    src = src_ref.at[lead] if lead else src_ref
    own = pltpu.make_async_copy(src.at[my_id] if scatter else src, land_ref.at[my_id], sems.at[2 * N_PEER])
    pairs = []
    for k, mask in enumerate(PEER_MASKS):
        if near and mask[2] == 1 and mask != (0, 0, 1):
            continue
        peer = _flip(me, mask)
        peer_id = _linear_id(peer)
        block = src.at[peer_id] if scatter else src
        send = pltpu.make_async_remote_copy(src_ref=block, dst_ref=land_ref.at[my_id], send_sem=sems.at[k],
                                            recv_sem=sems.at[N_PEER + k], device_id=peer, device_id_type=MESH_IDS)
        arrival = pltpu.make_async_remote_copy(src_ref=block, dst_ref=land_ref.at[peer_id], send_sem=sems.at[k],
                                               recv_sem=sems.at[N_PEER + k], device_id=peer, device_id_type=MESH_IDS)
        pairs.append((send, arrival))
    return own, pairs


def _exchange_start(srcs, buffers, *, name, scatter):
    ns, nb = len(srcs), len(buffers)
    lands = [_hbm(lax.empty((N_DEV,) + tuple(shape), dtype)) for _, _, shape, dtype, _ in buffers]

    def body(*refs):
        src_refs, land_refs = refs[:ns], refs[ns:ns + nb]
        sem_refs = refs[ns + nb:ns + 2 * nb]
        token = refs[-1]
        for b, (si, lead, _, _, near) in enumerate(buffers):
            own, pairs = _buffer_copies(src_refs[si], lead, land_refs[b], sem_refs[b], scatter, near)
            own.start()
            for send, _ in pairs:
                send.start()
        token[...] = jnp.zeros_like(token)

    out_shape = ([pltpu.SemaphoreType.DMA((SEMS_PER_BUFFER,))] * nb
                 + [pltpu.HBM(a.shape, a.dtype) for a in srcs]
                 + [pltpu.HBM(a.shape, a.dtype) for a in lands]
                 + [jax.ShapeDtypeStruct((8, 128), F32)])
    aliases = {i: nb + i for i in range(ns + nb)}
    outs = pl.pallas_call(
        body, name=name, in_specs=[HBM_SPEC] * (ns + nb),
        out_specs=[SEM_SPEC] * nb + [HBM_SPEC] * (ns + nb) + [pl.BlockSpec(memory_space=pltpu.VMEM)],
        out_shape=out_shape, input_output_aliases=aliases,
        compiler_params=pltpu.CompilerParams(has_side_effects=DATAFLOW),
    )(*[_hbm(a) for a in srcs], *lands)
    sems = list(outs[:nb])
    src_thru = list(outs[nb:nb + ns])
    land_thru = list(outs[nb + ns:nb + ns + nb])
    return sems, land_thru, src_thru, outs[-1]


def _exchange_wait(srcs_thru, buffers, sems, lands, after, *, name, scatter):
    ns, nb = len(srcs_thru), len(buffers)
    has_after = after is not None

    def body(*refs):
        src_refs, land_refs = refs[:ns], refs[ns:ns + nb]
        sem_refs = refs[ns + nb:ns + 2 * nb]
        for b, (si, lead, _, _, near) in enumerate(buffers):
            own, pairs = _buffer_copies(src_refs[si], lead, land_refs[b], sem_refs[b], scatter, near)
            for send, arrival in pairs:
                send.wait_send()
                arrival.wait_recv()
            own.wait()

    operands = list(srcs_thru) + list(lands) + list(sems) + ([after] if has_after else [])
    in_specs = ([HBM_SPEC] * (ns + nb) + [SEM_SPEC] * nb + ([pl.BlockSpec(memory_space=pl.ANY)] if has_after else []))
    outs = pl.pallas_call(
        body, name=name, in_specs=in_specs, out_specs=[HBM_SPEC] * nb,
        out_shape=[pltpu.HBM(a.shape, a.dtype) for a in lands],
        input_output_aliases={ns + b: b for b in range(nb)},
        compiler_params=pltpu.CompilerParams(has_side_effects=DATAFLOW),
    )(*operands)
    return list(outs)


def _exchange(arrays, *, name, scatter, near=None, after=None):
    n = len(arrays)
    near = [False] * n if near is None else near
    extra = [] if after is None else [after]
    out_shapes = [jax.ShapeDtypeStruct(a.shape if scatter else (N_DEV,) + a.shape, a.dtype) for a in arrays]

    def body(*refs):
        srcs, outs, sems = refs[:n], refs[n + len(extra):2 * n + len(extra)], refs[2 * n + len(extra):]
        started = []
        for a in range(n):
            own, pairs = _buffer_copies(srcs[a], (), outs[a], sems[a], scatter, near[a])
            own.start()
            for send, _ in pairs:
                send.start()
            started.append((own, pairs))
        for own, pairs in started:
            for send, arrival in pairs:
                arrival.wait_recv()
                send.wait_send()
            own.wait()

    any_spec = pl.BlockSpec(memory_space=pl.ANY)
    outs = pl.pallas_call(
        body, name=name, in_specs=[any_spec] * (n + len(extra)), out_specs=[any_spec] * n, out_shape=out_shapes,
        scratch_shapes=[pltpu.SemaphoreType.DMA((SEMS_PER_BUFFER,))] * n,
    )(*arrays, *extra)
    return list(outs)


def _forward_to_sibling(lands, *, name):
    n = len(lands)

    def body(*refs):
        ins, outs, sems = refs[:n], refs[n:2 * n], refs[2 * n:]
        me = _my_position()
        sibling = _flip(me, (0, 0, 1))
        pairs = []
        for b in range(n):
            for k, (fx, fy) in enumerate(CHIP_MASKS):
                mine = _linear_id(_flip(me, (fx, fy, 0)))
                theirs = _linear_id(_flip(me, (fx, fy, 1)))
                send = pltpu.make_async_remote_copy(
                    src_ref=ins[b].at[mine], dst_ref=outs[b].at[mine], send_sem=sems[b].at[k],
                    recv_sem=sems[b].at[len(CHIP_MASKS) + k], device_id=sibling, device_id_type=MESH_IDS)
                arrival = pltpu.make_async_remote_copy(
                    src_ref=ins[b].at[mine], dst_ref=outs[b].at[theirs], send_sem=sems[b].at[k],
                    recv_sem=sems[b].at[len(CHIP_MASKS) + k], device_id=sibling, device_id_type=MESH_IDS)
                send.start()
                pairs.append((send, arrival))
        for send, arrival in pairs:
            arrival.wait_recv()
            send.wait_send()

    any_spec = pl.BlockSpec(memory_space=pl.ANY)
    outs = pl.pallas_call(
        body, name=name, in_specs=[any_spec] * n, out_specs=[any_spec] * n,
        out_shape=[jax.ShapeDtypeStruct(a.shape, a.dtype) for a in lands],
        input_output_aliases={b: b for b in range(n)},
        scratch_shapes=[pltpu.SemaphoreType.DMA((2 * len(CHIP_MASKS),))] * n,
    )(*lands)
    return list(outs)


def _sum_slots(parts_ref):
    total = parts_ref[0].astype(F32)
    for q in range(1, parts_ref.shape[0]):
        total = total + parts_ref[q].astype(F32)
    return total


def _row_tile(rows, cols, n_arrays):
    budget = (24 * 1024 * 1024) // (4 * n_arrays * max(cols, 128))
    t = rows
    while t > budget and t % 2 == 0 and (t // 2) % 16 == 0:
        t //= 2
    return t


def _sum_adam(parts, w, m, v, layer, outs, *, name):
    q, r, c = parts.shape
    nl = w.shape[0]
    tr = _row_tile(r, c, q + 7)
    c1 = 1.0 - ADAM_B1 ** ADAM_STEP
    c2 = 1.0 - ADAM_B2 ** ADAM_STEP
    if outs is None:
        outs = [lax.empty((nl, r, c), F32) for _ in range(4)]

    def body(p_ref, w_ref, m_ref, v_ref, g_in, d_in, mo_in, vo_in, g_ref, d_ref, mo_ref, vo_ref):
        grad = _sum_slots(p_ref)
        m_new = ADAM_B1 * m_ref[...] + (1.0 - ADAM_B1) * grad
        v_new = ADAM_B2 * v_ref[...] + (1.0 - ADAM_B2) * (grad * grad)
        m_hat = m_new / c1
        v_hat = v_new / c2
        g_ref[...] = grad
        d_ref[...] = -ADAM_LR * (m_hat / (jnp.sqrt(v_hat) + ADAM_EPS) + ADAM_WD * w_ref[...])
        mo_ref[...] = m_new
        vo_ref[...] = v_new

    tile = pl.BlockSpec((None, tr, c), lambda i: (layer, i, 0))
    any_spec = pl.BlockSpec(memory_space=pl.ANY)
    return pl.pallas_call(
        body, name=name, grid=(r // tr,),
        in_specs=[pl.BlockSpec((q, tr, c), lambda i: (0, i, 0)), tile, tile, tile] + [any_spec] * 4,
        out_specs=[tile] * 4, out_shape=[jax.ShapeDtypeStruct((nl, r, c), F32)] * 4,
        input_output_aliases={4: 0, 5: 1, 6: 2, 7: 3},
        compiler_params=_params(("parallel",)),
    )(parts, w, m, v, *outs)


def _sum_parts(parts, *, name):
    q, r, c = parts.shape

    def body(p_ref, o_ref):
        o_ref[...] = _sum_slots(p_ref)

    return pl.pallas_call(
        body, name=name, in_specs=[pl.BlockSpec((q, r, c), lambda: (0, 0, 0))],
        out_specs=pl.BlockSpec((r, c), lambda: (0, 0)), out_shape=jax.ShapeDtypeStruct((r, c), F32),
        compiler_params=_params(),
    )(parts)


INPUT_NAMES = (['x', 'mem', 'positions'] + WEIGHTS + ['loss_target'] + ['m_' + n for n in WEIGHTS]
               + ['v_' + n for n in WEIGHTS])
SMALL_ALIGN = N_DEV * 8 * 128
TWO_LEVEL_LAYERS = N_A
GROUP_ORDER = ('in', 'rest', 'up', 'down')
GROUP_WEIGHTS = {'in': (['w_in_a'], ['w_in_b', 'w_uq', 'w_uk', 'w_uv']), 'rest': (['w_mem_kv', 'w_out'],) * 2,
                 'up': (['w_ffn_up'],) * 2, 'down': (['w_ffn_down'],) * 2}
LAYERED = {'w_in_a': 0, 'w_in_b': N_A, 'w_uq': N_A, 'w_uk': N_A, 'w_uv': N_A, 'w_mem_kv': 0, 'w_out': 0,
           'w_ffn_up': 0, 'w_ffn_down': 0}


def _permute_uq(w_uq):
    r = w_uq.shape[0]
    q = w_uq.reshape(r, G_HEADS, HEAD + ROPE_DIM)
    return jnp.concatenate([q[..., :HEAD].reshape(r, -1), q[..., HEAD:HEAD + ROPE_HALF].reshape(r, -1),
                            q[..., HEAD + ROPE_HALF:].reshape(r, -1)], axis=-1)


def _unpermute_uq(w_uqp):
    r = w_uqp.shape[0]
    nope = w_uqp[..., :G_W].reshape(r, G_HEADS, HEAD)
    r1 = w_uqp[..., G_W:G_W + G_HEADS * ROPE_HALF].reshape(r, G_HEADS, ROPE_HALF)
    r2 = w_uqp[..., G_W + G_HEADS * ROPE_HALF:].reshape(r, G_HEADS, ROPE_HALF)
    return jnp.concatenate([nope, r1, r2], axis=-1).reshape(r, -1)


def _cols_from_stack(st):
    _, r, n = st.shape
    return st.transpose(1, 0, 2).reshape(r, N_DEV * n)


def _cols_to_stack(wh):
    r, c = wh.shape
    return wh.reshape(r, N_DEV, c // N_DEV).transpose(1, 0, 2)


def _group_weights(group):
    kind, l = group
    return GROUP_WEIGHTS[kind][0 if l < N_A else 1]


def _step(args):
    p = dict(zip(INPUT_NAMES, args))
    x, mem, positions, target = p['x'][0], p['mem'][0], p['positions'][0], p['loss_target'][0]
    d = x.shape[-1]
    my_id = _linear_id(_my_position())

    w_kv_pad = jnp.pad(p['w_kv_a'], ((0, 0), (0, KV_PAD - p['w_kv_a'].shape[1])))
    shard = {k: p[k].astype(BF16) for k in LAYERED}
    shard['w_uk'] = shard['w_uk'].reshape(shard['w_uk'].shape[0], shard['w_uk'].shape[1], -1)
    shard['w_uv'] = shard['w_uv'].reshape(shard['w_uv'].shape[0], shard['w_uv'].shape[1], -1)
    shard.update(conv_w=p['conv_w'], g_v=p['g_v'], w_kv_a=w_kv_pad.astype(BF16))
    src_names = list(shard)
    gather_groups = []
    for l in range(DEPTH):
        gather_groups += [(kind, l) for kind in GROUP_ORDER]
    buffers, owner = [], []
    for group in gather_groups:
        kind, l = group
        for k in _group_weights(group):
            buffers.append((src_names.index(k), (l - LAYERED[k],), shard[k].shape[1:], shard[k].dtype, l < TWO_LEVEL_LAYERS))
            owner.append((group, k))
        if group == ('in', 0):
            for k in ('g_v', 'conv_w'):
                buffers.append((src_names.index(k), (), shard[k].shape, shard[k].dtype, True))
                owner.append((group, k))
        if group == ('in', N_A):
            buffers.append((src_names.index('w_kv_a'), (), shard['w_kv_a'].shape, BF16, False))
            owner.append((group, 'w_kv_a'))
    g_sems, g_lands, g_srcs, _ = _exchange_start([shard[k] for k in src_names], buffers, name="gather_start",
                                                 scatter=False)

    def fetch(group, after):
        idx = [i for i, (grp, _) in enumerate(owner) if grp == group]
        landed = _exchange_wait(g_srcs, [buffers[i] for i in idx], [g_sems[i] for i in idx],
                                [g_lands[i] for i in idx], after, name=f"gather_wait_{group[0]}{group[1]}",
                                scatter=False)
        if group[1] < TWO_LEVEL_LAYERS:
            landed = _forward_to_sibling(landed, name=f"gather_forward_{group[0]}{group[1]}")
        got = {owner[i][1]: t for i, t in zip(idx, landed)}
        out = {}
        for k, t in got.items():
            if k in ('w_in_a', 'w_uq'):
                out[k] = _cols_from_stack(t)
            elif k == 'g_v':
                out[k] = t.transpose(1, 0, 2).reshape(t.shape[1], -1)
            elif k in ('w_ffn_up', 'conv_w'):
                out[k] = t
            else:
                out[k] = t.reshape(-1, t.shape[-1])
        if 'w_uq' in out:
            out['w_uqp'] = _permute_uq(out.pop('w_uq'))
        for old, new in (('w_in_a', 'w_in'), ('w_in_b', 'w_in'), ('w_ffn_up', 'w_up'), ('w_ffn_down', 'w_down')):
            if old in out:
                out[new] = out.pop(old)
        return out

    pending = []

    def emit(group, grads):
        send = {}
        for k, t in grads.items():
            if k == 'w_in_a':
                send[k] = _cols_to_stack(t)
            elif k == 'w_uqp':
                send['w_uq'] = _cols_to_stack(_unpermute_uq(t))
            elif k == 'w_ffn_up':
                send[k] = t
            elif k == 'w_kv_a':
                cols = p['w_kv_a'].shape[1]
                send[k] = t[:, :cols].reshape(N_DEV, -1, cols)
            else:
                send[k] = t.reshape(N_DEV, t.shape[0] // N_DEV, t.shape[1])
        keys = list(send)
        bufs = [(i, (), send[k].shape[1:], send[k].dtype, False) for i, k in enumerate(keys)]
        sems, lands, srcs, token = _exchange_start([send[k] for k in keys], bufs,
                                                   name=f"scatter_start_{group[0]}{group[1]}", scatter=True)
        pending.append((group, keys, bufs, sems, lands, srcs))
        return token

    rep = {k: p[k] for k in REPLICATED}
    sq, grad_x, g = _local_step(x, mem, positions, target, rep, fetch, emit)
    loss = (0.5 / d) * lax.psum(sq, ("x", "y", "c"))

    out, running = {}, {}
    order = grad_x
    for group, keys, bufs, sems, lands, srcs in pending:
        landed = _exchange_wait(srcs, bufs, sems, lands, order, name=f"scatter_wait_{group[0]}{group[1]}", scatter=True)
        for k, parts in zip(keys, landed):
            stacked = k in LAYERED
            nl = p[k].shape[0] if stacked else 1
            layer = group[1] - LAYERED[k] if stacked else 0
            rows = p[k].size // nl // p[k].shape[-1]
            view = (nl, rows, p[k].shape[-1])
            running[k] = _sum_adam(parts.reshape(N_DEV, rows, view[2]), p[k].reshape(view), p['m_' + k].reshape(view),
                                   p['v_' + k].reshape(view), layer, running.get(k), name=f"adam_{k}{layer}")
            order = running[k][1]
    for k, res in running.items():
        out[k] = [t.reshape(p[k].shape) for t in res]

    small = {
        'g_mix': jnp.concatenate(g['g_mix']), 'g_ffn': jnp.concatenate(g['g_ffn']), 'g_final': g['g_final'],
        'w_sp': jnp.stack(g['w_sp']), 'b_sp': jnp.stack(g['b_sp']), 'g_kv': g['g_kv'], 'g_kv_lat': g['g_kv_lat'],
        'g_q_lat': jnp.concatenate(g['g_q_lat']), 'g_mem': jnp.concatenate(g['g_mem']),
        'conv_b': jnp.stack(g['conv_b']),
        'g_v': jnp.concatenate(g['g_v']),
        'conv_w': jnp.stack(g['conv_w']).transpose(0, 2, 1, 3),
    }
    small_names = REPLICATED + SMALL_SHARDED
    flat = jnp.concatenate([small[k].reshape(-1).astype(F32) for k in small_names])
    n_small = flat.shape[0]
    padded = -(-n_small // SMALL_ALIGN) * SMALL_ALIGN
    flat = jnp.pad(flat, (0, padded - n_small)).reshape(N_DEV, -1, 128)
    last_update = out[pending[-1][1][-1]][1]
    (small_parts,) = _exchange([flat], name="scatter_small", scatter=True, after=last_update)
    reduced = _sum_parts(small_parts, name="sum_small")
    (small_all,) = _exchange([reduced], name="gather_small", scatter=False)
    small_all = small_all.reshape(-1)
    grads_small, off = {}, 0
    for k in small_names:
        size = small[k].size
        grads_small[k] = small_all[off:off + size].reshape(small[k].shape)
        off += size
    grads_small['g_v'] = lax.dynamic_slice_in_dim(grads_small['g_v'], my_id * p['g_v'].shape[1], p['g_v'].shape[1], axis=1)
    grads_small['conv_w'] = lax.dynamic_index_in_dim(grads_small['conv_w'], my_id, axis=2, keepdims=False)
    gs = jnp.concatenate([grads_small[k].reshape(-1) for k in small_names])
    n_loc = gs.shape[0]
    pad_loc = -(-n_loc // 1024) * 1024 - n_loc

    def pack(prefix):
        t = jnp.concatenate([p[prefix + k].reshape(-1) for k in small_names])
        return jnp.pad(t, (0, pad_loc)).reshape(1, -1, 128)

    res = _sum_adam(jnp.pad(gs, (0, pad_loc)).reshape(1, -1, 128), pack(''), pack('m_'), pack('v_'), 0, None,
                    name="adam_small")
    off = 0
    for k in small_names:
        size = p[k].size
        out[k] = [t.reshape(-1)[off:off + size].reshape(p[k].shape) for t in res]
        off += size

    outs = [loss, grad_x[None]]
    for i in range(4):
        outs += [out[k][i] for k in WEIGHTS]
    return tuple(outs)


def kernel(x, mem, positions, g_mix, g_ffn, g_final, w_in_a, g_v, w_sp, b_sp, g_kv, w_kv_a, g_kv_lat, w_in_b, g_q_lat, w_uq, w_uk, w_uv, g_mem, w_mem_kv, w_out, w_ffn_up, conv_w, conv_b, w_ffn_down, loss_target, m_g_mix, m_g_ffn, m_g_final, m_w_in_a, m_g_v, m_w_sp, m_b_sp, m_g_kv, m_w_kv_a, m_g_kv_lat, m_w_in_b, m_g_q_lat, m_w_uq, m_w_uk, m_w_uv, m_g_mem, m_w_mem_kv, m_w_out, m_w_ffn_up, m_conv_w, m_conv_b, m_w_ffn_down, v_g_mix, v_g_ffn, v_g_final, v_w_in_a, v_g_v, v_w_sp, v_b_sp, v_g_kv, v_w_kv_a, v_g_kv_lat, v_w_in_b, v_g_q_lat, v_w_uq, v_w_uk, v_w_uv, v_g_mem, v_w_mem_kv, v_w_out, v_w_ffn_up, v_conv_w, v_conv_b, v_w_ffn_down):
    return _step((x, mem, positions, g_mix, g_ffn, g_final, w_in_a, g_v, w_sp, b_sp, g_kv, w_kv_a, g_kv_lat, w_in_b, g_q_lat, w_uq, w_uk, w_uv, g_mem, w_mem_kv, w_out, w_ffn_up, conv_w, conv_b, w_ffn_down, loss_target, m_g_mix, m_g_ffn, m_g_final, m_w_in_a, m_g_v, m_w_sp, m_b_sp, m_g_kv, m_w_kv_a, m_g_kv_lat, m_w_in_b, m_g_q_lat, m_w_uq, m_w_uk, m_w_uv, m_g_mem, m_w_mem_kv, m_w_out, m_w_ffn_up, m_conv_w, m_conv_b, m_w_ffn_down, v_g_mix, v_g_ffn, v_g_final, v_w_in_a, v_g_v, v_w_sp, v_b_sp, v_g_kv, v_w_kv_a, v_g_kv_lat, v_w_in_b, v_g_q_lat, v_w_uq, v_w_uk, v_w_uv, v_g_mem, v_w_mem_kv, v_w_out, v_w_ffn_up, v_conv_w, v_conv_b, v_w_ffn_down))
```

```python
import math

import jax
import jax.numpy as jnp
from jax import lax
from jax.experimental import pallas as pl
from jax.experimental.pallas import tpu as pltpu

F32 = jnp.float32
BF16 = jnp.bfloat16

N_DEV = 8
N_A = 2
DEPTH = 4
G_HEADS = 12
HEAD = 128
CHUNK = 128
MEM_HEADS = 4
MEM_W = MEM_HEADS * HEAD
G_W = G_HEADS * HEAD
ROPE_DIM = 64
ROPE_HALF = ROPE_DIM // 2
KV_RANK = 512
Q_RANK = 512
KV_PAD = 640
ROPE_THETA = 10000.0
EPS = 1e-6
CONV_W = 3

ADAM_LR = 0.001
ADAM_B1 = 0.9
ADAM_B2 = 0.999
ADAM_EPS = 1e-08
ADAM_WD = 0.01
ADAM_STEP = 10

VMEM_LIMIT_V7X = 56 * 1024 * 1024
MASK_VALUE = -1e30

WEIGHTS = ['g_mix', 'g_ffn', 'g_final', 'w_in_a', 'g_v', 'w_sp', 'b_sp', 'g_kv', 'w_kv_a', 'g_kv_lat',
           'w_in_b', 'g_q_lat', 'w_uq', 'w_uk', 'w_uv', 'g_mem', 'w_mem_kv', 'w_out', 'w_ffn_up',
           'conv_w', 'conv_b', 'w_ffn_down']
REPLICATED = ['g_mix', 'g_ffn', 'g_final', 'w_sp', 'b_sp', 'g_kv', 'g_kv_lat', 'g_q_lat', 'g_mem', 'conv_b']
SMALL_SHARDED = ['g_v', 'conv_w']


def _params(sem=None):
    return pltpu.CompilerParams(dimension_semantics=sem, vmem_limit_bytes=VMEM_LIMIT_V7X)


def _dot(a, b, dims):
    contract = {'nn': ((1,), (0,)), 'nt': ((1,), (1,)), 'tn': ((0,), (0,))}[dims]
    return lax.dot_general(a, b, (contract, ((), ())), preferred_element_type=F32)


def _erf(x):
    return lax.erf(x)


def _gelu(x):
    return 0.5 * x * (1.0 + _erf(x * (2.0 ** -0.5)))


def _gelu_grad(x):
    cdf = 0.5 * (1.0 + _erf(x * (2.0 ** -0.5)))
    pdf = jnp.exp(-0.5 * x * x) * (1.0 / math.sqrt(2.0 * math.pi))
    return cdf + x * pdf


def _sigmoid(x):
    return 1.0 / (1.0 + jnp.exp(-x))


def _operand_spec(shape, lead, blocked, tr, tc, ridx, cidx):
    if blocked:
        per = shape[-1] // tc
        assert shape[-1] % tc == 0, (shape, tc)
        return pl.BlockSpec(
            (None,) * (1 + len(lead)) + (tr, tc),
            lambda *g: (cidx(*g) // per,) + lead + (ridx(*g), cidx(*g) % per))
    return pl.BlockSpec((None,) * len(lead) + (tr, tc), lambda *g: lead + (ridx(*g), cidx(*g)))


def _view2d(x, blocked):
    return (x.shape[-2], x.shape[0] * x.shape[-1]) if blocked else (x.shape[-2], x.shape[-1])


def _mm(a, b, *, dims, out_dtype, name, tm, tn, tk=None, res=None, a_lead=(), b_lead=(),
        a_blocked=False, b_blocked=False, out_block=None, n_outer=False, after=None):
    ar, ac = _view2d(a, a_blocked)
    br, bc = _view2d(b, b_blocked)
    m, k = (ac, ar) if dims == 'tn' else (ar, ac)
    n, k2 = (br, bc) if dims == 'nt' else (bc, br)
    assert k == k2, (a.shape, b.shape, dims)
    tm, tn = min(tm, m), min(tn, n)
    tk = k if tk is None else tk
    assert m % tm == 0 and n % tn == 0 and k % tk == 0, (name, m, n, k, tm, tn, tk)
    nk = k // tk
    if n_outer:
        gi, gj = (lambda g0, g1, g2: g1), (lambda g0, g1, g2: g0)
        grid = (n // tn, m // tm, nk)
    else:
        gi, gj = (lambda g0, g1, g2: g0), (lambda g0, g1, g2: g1)
        grid = (m // tm, n // tn, nk)
    gk = lambda g0, g1, g2: g2

    if dims == 'tn':
        a_spec = _operand_spec(a.shape, a_lead, a_blocked, tk, tm, gk, gi)
    else:
        a_spec = _operand_spec(a.shape, a_lead, a_blocked, tm, tk, gi, gk)
    if dims == 'nt':
        b_spec = _operand_spec(b.shape, b_lead, b_blocked, tn, tk, gj, gk)
    else:
        b_spec = _operand_spec(b.shape, b_lead, b_blocked, tk, tn, gk, gj)
    in_specs = [a_spec, b_spec]
    operands = [a, b]
    if res is not None:
        in_specs.append(pl.BlockSpec((tm, tn), lambda *g: (gi(*g), gj(*g))))
        operands.append(res)
    if after is not None:
        in_specs.append(pl.BlockSpec(memory_space=pl.ANY))
        operands.append(after)
    n_in = len(operands)
    if out_block is not None:
        out_shape = jax.ShapeDtypeStruct((n // out_block, m, out_block), out_dtype)
        out_spec = _operand_spec(out_shape.shape, (), True, tm, tn, gi, gj)
    else:
        out_shape = jax.ShapeDtypeStruct((m, n), out_dtype)
        out_spec = pl.BlockSpec((tm, tn), lambda *g: (gi(*g), gj(*g)))

    def body(*refs):
        a_ref, b_ref = refs[0], refs[1]
        r_ref = refs[2] if res is not None else None
        o_ref = refs[n_in]
        acc_ref = refs[-1] if nk > 1 else None
        part = _dot(a_ref[...].astype(BF16), b_ref[...].astype(BF16), dims)

        def finish(total):
            if r_ref is not None:
                total = total + r_ref[...]
            o_ref[...] = total.astype(o_ref.dtype)

        if nk == 1:
            finish(part)
        else:
            kk = pl.program_id(2)

            @pl.when(kk == 0)
            def _():
                acc_ref[...] = part

            @pl.when(kk > 0)
            def _():
                acc_ref[...] += part

            @pl.when(kk == nk - 1)
            def _():
                finish(acc_ref[...])

    scratch = [pltpu.VMEM((tm, tn), F32)] if nk > 1 else []
    return pl.pallas_call(
        body, name=name, grid=grid, in_specs=in_specs, out_specs=out_spec,
        out_shape=out_shape, scratch_shapes=scratch,
        compiler_params=_params(("parallel", "parallel", "arbitrary")),
    )(*operands)


def _mm_blocked_nt(a, b, *, out_dtype, name, tm, tn, blocks_per_step, after=None):
    nb, m, bw = a.shape
    n = b.shape[1]
    tm, tn = min(tm, m), min(tn, n)
    assert nb % blocks_per_step == 0 and m % tm == 0 and n % tn == 0
    nk = nb // blocks_per_step

    def body(*refs):
        a_ref, b_ref, o_ref, acc_ref = refs[0], refs[1], refs[-2], refs[-1]
        kk = pl.program_id(2)
        part = _dot(a_ref[0], b_ref[0], 'nt')
        for t in range(1, blocks_per_step):
            part = part + _dot(a_ref[t], b_ref[t], 'nt')

        @pl.when(kk == 0)
        def _():
            acc_ref[...] = part

        @pl.when(kk > 0)
        def _():
            acc_ref[...] += part

        @pl.when(kk == nk - 1)
        def _():
            o_ref[...] = acc_ref[...].astype(o_ref.dtype)

    in_specs = [pl.BlockSpec((blocks_per_step, tm, bw), lambda i, j, k: (k, i, 0)),
                pl.BlockSpec((blocks_per_step, tn, bw), lambda i, j, k: (k, j, 0))]
    operands = [a, b]
    if after is not None:
        in_specs.append(pl.BlockSpec(memory_space=pl.ANY))
        operands.append(after)
    return pl.pallas_call(
        body, name=name, grid=(m // tm, n // tn, nk), in_specs=in_specs,
        out_specs=pl.BlockSpec((tm, tn), lambda i, j, k: (i, j)), out_shape=jax.ShapeDtypeStruct((m, n), out_dtype),
        scratch_shapes=[pltpu.VMEM((tm, tn), F32)],
        compiler_params=_params(("parallel", "parallel", "arbitrary")),
    )(*operands)


def _rmsnorm(x, g, *, name, width=None, out_dtype=BF16, tm=512):
    s = x.shape[0]
    w = x.shape[1] if width is None else width
    tm = min(tm, s)

    def body(x_ref, g_ref, o_ref):
        xv = x_ref[...].astype(F32)
        rstd = lax.rsqrt(jnp.mean(xv * xv, axis=-1, keepdims=True) + EPS)
        o_ref[...] = (xv * rstd * g_ref[...]).astype(o_ref.dtype)

    return pl.pallas_call(
        body, name=name, grid=(s // tm,),
        in_specs=[pl.BlockSpec((tm, w), lambda i: (i, 0)), pl.BlockSpec((1, w), lambda i: (0, 0))],
        out_specs=pl.BlockSpec((tm, w), lambda i: (i, 0)),
        out_shape=jax.ShapeDtypeStruct((s, w), out_dtype),
        compiler_params=_params(("parallel",)),
    )(x, g.reshape(1, w))


def _rmsnorm_bwd(x, g, dy, *, name, width=None, dres=None, after=None, out_dtype=F32, tm=512):
    s = x.shape[0]
    w = x.shape[1] if width is None else width
    tm = min(tm, s)

    def body(*refs):
        x_ref, g_ref, dy_ref = refs[0], refs[1], refs[2]
        r_ref = refs[3] if dres is not None else None
        dx_ref, dg_ref = refs[-2], refs[-1]
        xv = x_ref[...].astype(F32)
        rstd = lax.rsqrt(jnp.mean(xv * xv, axis=-1, keepdims=True) + EPS)
        xhat = xv * rstd
        dyv = dy_ref[...].astype(F32)
        gdy = dyv * g_ref[...]
        dx = rstd * (gdy - xhat * jnp.mean(gdy * xhat, axis=-1, keepdims=True))
        if r_ref is not None:
            dx = dx + r_ref[...]
        dx_ref[...] = dx.astype(dx_ref.dtype)
        part = jnp.sum(dyv * xhat, axis=0, keepdims=True)

        @pl.when(pl.program_id(0) == 0)
        def _():
            dg_ref[...] = part

        @pl.when(pl.program_id(0) > 0)
        def _():
            dg_ref[...] += part

    row = pl.BlockSpec((tm, w), lambda i: (i, 0))
    vec = pl.BlockSpec((1, w), lambda i: (0, 0))
    in_specs = [row, vec, row] + ([row] if dres is not None else [])
    operands = [x, g.reshape(1, w), dy] + ([dres] if dres is not None else [])
    if after is not None:
        in_specs.append(pl.BlockSpec(memory_space=pl.ANY))
        operands.append(after)
    return pl.pallas_call(
        body, name=name, grid=(s // tm,), in_specs=in_specs, out_specs=[row, vec],
        out_shape=[jax.ShapeDtypeStruct((s, w), out_dtype), jax.ShapeDtypeStruct((1, w), F32)],
        compiler_params=_params(("arbitrary",)),
    )(*operands)


def _final_loss(x, target, g, *, name, tm=256):
    s, d = x.shape
    tm = min(tm, s)

    def body(x_ref, t_ref, g_ref, sq_ref, dx_ref, dg_ref):
        xv = x_ref[...]
        rstd = lax.rsqrt(jnp.mean(xv * xv, axis=-1, keepdims=True) + EPS)
        xhat = xv * rstd
        err = xhat * g_ref[...] - t_ref[...]
        dyv = err * (1.0 / d)
        gdy = dyv * g_ref[...]
        dx_ref[...] = rstd * (gdy - xhat * jnp.mean(gdy * xhat, axis=-1, keepdims=True))
        sq = jnp.sum(err * err, axis=0, keepdims=True)
        dg = jnp.sum(dyv * xhat, axis=0, keepdims=True)

        @pl.when(pl.program_id(0) == 0)
        def _():
            sq_ref[...] = sq
            dg_ref[...] = dg

        @pl.when(pl.program_id(0) > 0)
        def _():
            sq_ref[...] += sq
            dg_ref[...] += dg

    row = pl.BlockSpec((tm, d), lambda i: (i, 0))
    vec = pl.BlockSpec((1, d), lambda i: (0, 0))
    return pl.pallas_call(
        body, name=name, grid=(s // tm,), in_specs=[row, row, vec], out_specs=[vec, row, vec],
        out_shape=[jax.ShapeDtypeStruct((1, d), F32), jax.ShapeDtypeStruct((s, d), F32),
                   jax.ShapeDtypeStruct((1, d), F32)],
        compiler_params=_params(("arbitrary",)),
    )(x, target, g.reshape(1, d))


def _tril_mask():
    t = lax.broadcasted_iota(jnp.int32, (CHUNK, CHUNK), 0)
    s = lax.broadcasted_iota(jnp.int32, (CHUNK, CHUNK), 1)
    return t >= s


def _sgu_fwd(z, g_v, w_sp, b_sp_t, *, name):
    s = z.shape[0]

    def body(zu_ref, zv_ref, g_ref, w_ref, b_ref, o_ref):
        u = _gelu(zu_ref[...].astype(F32))
        gv = _gelu(zv_ref[...].astype(F32))
        rstd = lax.rsqrt(jnp.mean(gv * gv, axis=-1, keepdims=True) + EPS)
        v = (gv * rstd * g_ref[...]).astype(BF16)
        mask = _tril_mask()
        for grp in range(G_HEADS):
            cols = slice(grp * HEAD, (grp + 1) * HEAD)
            wm = jnp.where(mask, w_ref[grp], 0.0).astype(BF16)
            sv = _dot(wm, v[:, cols], 'nn') + b_ref[:, grp:grp + 1]
            o_ref[:, cols] = (u[:, cols] * sv).astype(o_ref.dtype)

    return pl.pallas_call(
        body, name=name, grid=(s // CHUNK,),
        in_specs=[pl.BlockSpec((CHUNK, G_W), lambda i: (i, 0)),
                  pl.BlockSpec((CHUNK, G_W), lambda i: (i, 1)),
                  pl.BlockSpec((1, G_W), lambda i: (0, 0)),
                  pl.BlockSpec((G_HEADS, CHUNK, CHUNK), lambda i: (0, 0, 0)),
                  pl.BlockSpec((CHUNK, G_HEADS), lambda i: (0, 0))],
        out_specs=pl.BlockSpec((CHUNK, G_W), lambda i: (i, 0)),
        out_shape=jax.ShapeDtypeStruct((s, G_W), BF16),
        compiler_params=_params(("parallel",)),
    )(z, z, g_v.reshape(1, G_W), w_sp, b_sp_t)


def _sgu_bwd(z, dmix, dqm, g_v, w_sp, b_sp_t, *, name):
    s = z.shape[0]
    zw = z.shape[1]

    def body(zu_ref, zv_ref, dm_ref, dq_ref, g_ref, w_ref, b_ref, dz_ref, dw_ref, db_ref, dg_ref):
        first = pl.program_id(0) == 0

        @pl.when(first)
        def _():
            dw_ref[...] = jnp.zeros_like(dw_ref)
            db_ref[...] = jnp.zeros_like(db_ref)
            dg_ref[...] = jnp.zeros_like(dg_ref)

        zu = zu_ref[...].astype(F32)
        zv = zv_ref[...].astype(F32)
        dmain = dm_ref[...].astype(F32)
        u = _gelu(zu)
        gv = _gelu(zv)
        rstd = lax.rsqrt(jnp.mean(gv * gv, axis=-1, keepdims=True) + EPS)
        vhat = gv * rstd
        gvec = g_ref[...]
        v = (vhat * gvec).astype(BF16)
        dsv = dmain * u
        dsv_b = dsv.astype(BF16)
        mask = _tril_mask()
        dv_parts = []
        for grp in range(G_HEADS):
            cols = slice(grp * HEAD, (grp + 1) * HEAD)
            wm = jnp.where(mask, w_ref[grp], 0.0).astype(BF16)
            sv = _dot(wm, v[:, cols], 'nn') + b_ref[:, grp:grp + 1]
            dz_ref[:, cols] = (dmain[:, cols] * sv * _gelu_grad(zu[:, cols])).astype(dz_ref.dtype)
            dwg = _dot(dsv_b[:, cols], v[:, cols], 'nt')
            dw_ref[grp] += jnp.where(mask, dwg, 0.0)
            db_ref[:, grp:grp + 1] += jnp.sum(dsv[:, cols], axis=-1, keepdims=True)
            dv_parts.append(_dot(wm, dsv_b[:, cols], 'tn'))
        dv = jnp.concatenate(dv_parts, axis=-1)
        dg_ref[...] += jnp.sum(dv * vhat, axis=0, keepdims=True)
        gdv = dv * gvec
        dgv = rstd * (gdv - vhat * jnp.mean(gdv * vhat, axis=-1, keepdims=True))
        dz_ref[:, G_W:2 * G_W] = (dgv * _gelu_grad(zv)).astype(dz_ref.dtype)
        dz_ref[:, 2 * G_W:] = dq_ref[...].astype(dz_ref.dtype)

    return pl.pallas_call(
        body, name=name, grid=(s // CHUNK,),
        in_specs=[pl.BlockSpec((CHUNK, G_W), lambda i: (i, 0)),
                  pl.BlockSpec((CHUNK, G_W), lambda i: (i, 1)),
                  pl.BlockSpec((CHUNK, G_W), lambda i: (i, 0)),
                  pl.BlockSpec((CHUNK, MEM_W), lambda i: (i, 0)),
                  pl.BlockSpec((1, G_W), lambda i: (0, 0)),
                  pl.BlockSpec((G_HEADS, CHUNK, CHUNK), lambda i: (0, 0, 0)),
                  pl.BlockSpec((CHUNK, G_HEADS), lambda i: (0, 0))],
        out_specs=[pl.BlockSpec((CHUNK, zw), lambda i: (i, 0)),
                   pl.BlockSpec((G_HEADS, CHUNK, CHUNK), lambda i: (0, 0, 0)),
                   pl.BlockSpec((CHUNK, G_HEADS), lambda i: (0, 0)),
                   pl.BlockSpec((1, G_W), lambda i: (0, 0))],
        out_shape=[jax.ShapeDtypeStruct((s, zw), BF16),
                   jax.ShapeDtypeStruct((G_HEADS, CHUNK, CHUNK), F32),
                   jax.ShapeDtypeStruct((CHUNK, G_HEADS), F32),
                   jax.ShapeDtypeStruct((1, G_W), F32)],
        compiler_params=_params(("arbitrary",)),
    )(z, z, dmix, dqm, g_v.reshape(1, G_W), w_sp, b_sp_t)


def _mem_probs(q, k):
    sc = _dot(q, k, 'nt') * (HEAD ** -0.5)
    sc = sc - jnp.max(sc, axis=-1, keepdims=True)
    e = jnp.exp(sc)
    return e / jnp.sum(e, axis=-1, keepdims=True)


def _memattn_fwd(z, kvm, main, *, qcol, name, tm=512):
    s = z.shape[0]
    m = kvm.shape[0]
    tm = min(tm, s)

    def body(q_ref, kv_ref, main_ref, o_ref):
        o_ref[:, :G_W] = main_ref[...]
        for h in range(MEM_HEADS):
            cols = slice(h * HEAD, (h + 1) * HEAD)
            k = kv_ref[:, cols]
            v = kv_ref[:, MEM_W + h * HEAD:MEM_W + (h + 1) * HEAD]
            p = _mem_probs(q_ref[:, cols], k)
            o_ref[:, G_W + h * HEAD:G_W + (h + 1) * HEAD] = _dot(p.astype(BF16), v, 'nn').astype(o_ref.dtype)

    return pl.pallas_call(
        body, name=name, grid=(s // tm,),
        in_specs=[pl.BlockSpec((tm, MEM_W), lambda i: (i, qcol)),
                  pl.BlockSpec((m, 2 * MEM_W), lambda i: (0, 0)),
                  pl.BlockSpec((tm, G_W), lambda i: (i, 0))],
        out_specs=pl.BlockSpec((tm, G_W + MEM_W), lambda i: (i, 0)),
        out_shape=jax.ShapeDtypeStruct((s, G_W + MEM_W), BF16),
        compiler_params=_params(("parallel",)),
    )(z, kvm, main)


def _memattn_bwd(z, kvm, dmix, *, qcol, name, tm=512):
    s = z.shape[0]
    m = kvm.shape[0]
    tm = min(tm, s)
    scale = HEAD ** -0.5

    def body(q_ref, kv_ref, do_ref, dq_ref, dkv_ref):
        @pl.when(pl.program_id(0) == 0)
        def _():
            dkv_ref[...] = jnp.zeros_like(dkv_ref)

        for h in range(MEM_HEADS):
            cols = slice(h * HEAD, (h + 1) * HEAD)
            vcols = slice(MEM_W + h * HEAD, MEM_W + (h + 1) * HEAD)
            q = q_ref[:, cols]
            k = kv_ref[:, cols]
            v = kv_ref[:, vcols]
            do = do_ref[:, cols]
            p = _mem_probs(q, k)
            dp = _dot(do, v, 'nt')
            ds = (p * (dp - jnp.sum(dp * p, axis=-1, keepdims=True)) * scale).astype(BF16)
            dq_ref[:, cols] = _dot(ds, k, 'nn').astype(dq_ref.dtype)
            dkv_ref[:, cols] += _dot(ds, q, 'tn')
            dkv_ref[:, vcols] += _dot(p.astype(BF16), do, 'tn')

    mo_block = G_W // MEM_W
    return pl.pallas_call(
        body, name=name, grid=(s // tm,),
        in_specs=[pl.BlockSpec((tm, MEM_W), lambda i: (i, qcol)),
                  pl.BlockSpec((m, 2 * MEM_W), lambda i: (0, 0)),
                  pl.BlockSpec((tm, MEM_W), lambda i: (i, mo_block))],
        out_specs=[pl.BlockSpec((tm, MEM_W), lambda i: (i, 0)),
                   pl.BlockSpec((m, 2 * MEM_W), lambda i: (0, 0))],
        out_shape=[jax.ShapeDtypeStruct((s, MEM_W), BF16), jax.ShapeDtypeStruct((m, 2 * MEM_W), F32)],
        compiler_params=_params(("arbitrary",)),
    )(z, kvm, dmix)


def _rope(x1, x2, cos, sin, *, name, inverse=False, out_dtype=BF16, col1=0, col2=0, tm=512):
    s, w = cos.shape
    tm = min(tm, s)
    sign = -1.0 if inverse else 1.0

    def body(a_ref, b_ref, c_ref, s_ref, o1_ref, o2_ref):
        a = a_ref[...].astype(F32)
        b = b_ref[...].astype(F32)
        c = c_ref[...]
        sn = s_ref[...] * sign
        o1_ref[...] = (a * c - b * sn).astype(o1_ref.dtype)
        o2_ref[...] = (b * c + a * sn).astype(o2_ref.dtype)

    row = pl.BlockSpec((tm, w), lambda i: (i, 0))
    return pl.pallas_call(
        body, name=name, grid=(s // tm,),
        in_specs=[pl.BlockSpec((tm, w), lambda i: (i, col1)), pl.BlockSpec((tm, w), lambda i: (i, col2)), row, row],
        out_specs=[row, row],
        out_shape=[jax.ShapeDtypeStruct((s, w), out_dtype)] * 2,
        compiler_params=_params(("parallel",)),
    )(x1, x2, cos, sin)


MHA_BLOCK = 1024
QK_DIM = HEAD + ROPE_DIM


def _mha_scores(q, k, scale, diagonal):
    sc = _dot(q, k, 'nt') * scale
    if not diagonal:
        return sc, None
    rows = lax.broadcasted_iota(jnp.int32, sc.shape, 0)
    cols = lax.broadcasted_iota(jnp.int32, sc.shape, 1)
    return sc, cols <= rows


def _mha_fwd(q, k, vv, *, name):
    s = k.shape[1]
    tb = min(MHA_BLOCK, s)
    scale = QK_DIM ** -0.5

    def body(q_ref, k_ref, v_ref, o_ref, lse_ref, m_ref, l_ref, acc_ref):
        i = pl.program_id(1)
        qh = q_ref[...]
        m_ref[...] = jnp.full_like(m_ref, MASK_VALUE)
        l_ref[...] = jnp.zeros_like(l_ref)
        acc_ref[...] = jnp.zeros_like(acc_ref)

        def block(j, diagonal):
            ks = pl.multiple_of(j * tb, tb)
            kj, vj = k_ref[pl.ds(ks, tb), :], v_ref[pl.ds(ks, tb), :]
            sc, keep = _mha_scores(qh, kj, scale, diagonal)
            if diagonal:
                sc = jnp.where(keep, sc, MASK_VALUE)
            m_old = m_ref[...]
            m_new = jnp.maximum(m_old, jnp.max(sc, axis=-1, keepdims=True))
            p = jnp.exp(sc - m_new)
            alpha = jnp.exp(m_old - m_new)
            l_ref[...] = alpha * l_ref[...] + jnp.sum(p, axis=-1, keepdims=True)
            acc_ref[...] = alpha * acc_ref[...] + _dot(p.astype(BF16), vj, 'nn')
            m_ref[...] = m_new

        def step(j, carry):
            block(j, False)
            return carry

        lax.fori_loop(0, i, step, 0)
        block(i, True)
        l = l_ref[...]
        o_ref[...] = (acc_ref[...] / l).astype(o_ref.dtype)
        lse_ref[...] = m_ref[...] + jnp.log(l)

    return pl.pallas_call(
        body, name=name, grid=(G_HEADS, s // tb),
        in_specs=[pl.BlockSpec((None, tb, QK_DIM), lambda h, i: (h, i, 0)),
                  pl.BlockSpec((None, s, QK_DIM), lambda h, i: (h, 0, 0)),
                  pl.BlockSpec((s, HEAD), lambda h, i: (0, h))],
        out_specs=[pl.BlockSpec((tb, HEAD), lambda h, i: (i, h)),
                   pl.BlockSpec((None, tb, 1), lambda h, i: (h, i, 0))],
        out_shape=[jax.ShapeDtypeStruct((s, G_W), BF16), jax.ShapeDtypeStruct((G_HEADS, s, 1), F32)],
        scratch_shapes=[pltpu.VMEM((tb, 1), F32), pltpu.VMEM((tb, 1), F32), pltpu.VMEM((tb, HEAD), F32)],
        compiler_params=_params(("parallel", "arbitrary")),
    )(q, k, vv)


def _mha_bwd(q, k, vv, o, do, lse, *, name):
    s = k.shape[1]
    tb = min(MHA_BLOCK, s)
    nq = s // tb
    scale = QK_DIM ** -0.5

    def body(q_ref, k_ref, v_ref, o_ref, do_ref, lse_ref, dq_ref, dk_ref, dv_ref, dqa_ref, dka_ref, dva_ref):
        i = pl.program_id(1)

        @pl.when(i == 0)
        def _():
            dka_ref[...] = jnp.zeros_like(dka_ref)
            dva_ref[...] = jnp.zeros_like(dva_ref)

        qh, dov = q_ref[...], do_ref[...]
        delta = jnp.sum(dov.astype(F32) * o_ref[...].astype(F32), axis=-1, keepdims=True)
        lsev = lse_ref[...]
        dqa_ref[...] = jnp.zeros_like(dqa_ref)

        def block(j, diagonal):
            ks = pl.multiple_of(j * tb, tb)
            kj, vj = k_ref[pl.ds(ks, tb), :], v_ref[pl.ds(ks, tb), :]
            sc, keep = _mha_scores(qh, kj, scale, diagonal)
            p = jnp.exp(sc - lsev)
            if diagonal:
                p = jnp.where(keep, p, 0.0)
            dp = _dot(dov, vj, 'nt')
            ds = (p * (dp - delta) * scale).astype(BF16)
            dqa_ref[...] += _dot(ds, kj, 'nn')
            dka_ref[pl.ds(ks, tb), :] += _dot(ds, qh, 'tn')
            dva_ref[pl.ds(ks, tb), :] += _dot(p.astype(BF16), dov, 'tn')

        def step(j, carry):
            block(j, False)
            return carry

        lax.fori_loop(0, i, step, 0)
        block(i, True)
        dq_ref[...] = dqa_ref[...].astype(dq_ref.dtype)

        @pl.when(i == nq - 1)
        def _():
            dk_ref[...] = dka_ref[...].astype(dk_ref.dtype)
            dv_ref[...] = dva_ref[...].astype(dv_ref.dtype)

    q_tile = pl.BlockSpec((None, tb, QK_DIM), lambda h, i: (h, i, 0))
    k_head = pl.BlockSpec((None, s, QK_DIM), lambda h, i: (h, 0, 0))
    tile = pl.BlockSpec((tb, HEAD), lambda h, i: (i, h))
    v_head = pl.BlockSpec((s, HEAD), lambda h, i: (0, h))
    return pl.pallas_call(
        body, name=name, grid=(G_HEADS, nq),
        in_specs=[q_tile, k_head, v_head, tile, tile, pl.BlockSpec((None, tb, 1), lambda h, i: (h, i, 0))],
        out_specs=[q_tile, k_head, v_head],
        out_shape=[jax.ShapeDtypeStruct((G_HEADS, s, QK_DIM), BF16), jax.ShapeDtypeStruct((G_HEADS, s, QK_DIM), BF16),
                   jax.ShapeDtypeStruct((s, G_W), BF16)],
        scratch_shapes=[pltpu.VMEM((tb, QK_DIM), F32), pltpu.VMEM((s, QK_DIM), F32), pltpu.VMEM((s, HEAD), F32)],
        compiler_params=_params(("parallel", "arbitrary")),
    )(q, k, vv, o, do, lse)


HALO = 16


def _shift_down(prev, cur, shift, first_tile):
    tr = cur.shape[0]
    full = jnp.concatenate([prev, cur], axis=0)
    out = pltpu.roll(full, shift, axis=0)[HALO:]
    row = lax.broadcasted_iota(jnp.int32, (tr, 1), 0)
    return jnp.where(jnp.logical_and(first_tile, row < shift), 0.0, out)


def _shift_up(cur, nxt, shift, last_tile):
    tr = cur.shape[0]
    full = jnp.concatenate([cur, nxt], axis=0)
    out = pltpu.roll(full, tr + HALO - shift, axis=0)[:tr]
    row = lax.broadcasted_iota(jnp.int32, (tr, 1), 0)
    return jnp.where(jnp.logical_and(last_tile, row >= tr - shift), 0.0, out)


def _lane_chunks(width, lanes):
    return [slice(c0, min(c0 + lanes, width)) for c0 in range(0, width, lanes)]


def _conv_taps(prev_ref, cur_ref, cw_ref, cb_ref, first_tile, cs):
    cur = cur_ref[:, cs].astype(F32)
    prev = prev_ref[:, cs].astype(F32)
    a1 = _shift_down(prev, cur, 1, first_tile)
    a2 = _shift_down(prev, cur, 2, first_tile)
    c = a2 * cw_ref[0:1, cs] + a1 * cw_ref[1:2, cs] + cur * cw_ref[2:3, cs] + cb_ref[:, cs]
    return c, (a2, a1, cur)


def _conv_in_specs(tr, bw, half, layer, row_of, blk_of):
    per = tr // HALO
    specs = []
    for off in (0, half):
        specs.append(pl.BlockSpec((None, HALO, bw), lambda *g, off=off: (blk_of(*g) + off, jnp.maximum(row_of(*g) * per - 1, 0), 0)))
        specs.append(pl.BlockSpec((None, tr, bw), lambda *g, off=off: (blk_of(*g) + off, row_of(*g), 0)))
    for off in (0, half):
        specs.append(pl.BlockSpec((None, None, CONV_W, bw), lambda *g, off=off: (blk_of(*g) + off, layer, 0, 0)))
    for off in (0, half):
        specs.append(pl.BlockSpec((None, 1, bw), lambda *g, off=off: (layer * 2 * half + blk_of(*g) + off, 0, 0)))
    return specs


def _conv_fwd(a, cw, cb, layer, *, name, tr=512):
    nb, s, bw = a.shape
    half = nb // 2
    tr = min(tr, s)

    def body(gp_ref, gc_ref, vp_ref, vc_ref, cwg_ref, cwv_ref, cbg_ref, cbv_ref, o_ref):
        first = pl.program_id(0) == 0
        for cs in _lane_chunks(bw, 256):
            gate, _ = _conv_taps(gp_ref, gc_ref, cwg_ref, cbg_ref, first, cs)
            val, _ = _conv_taps(vp_ref, vc_ref, cwv_ref, cbv_ref, first, cs)
            o_ref[:, cs] = (gate * _sigmoid(gate) * val).astype(o_ref.dtype)

    return pl.pallas_call(
        body, name=name, grid=(s // tr, half),
        in_specs=_conv_in_specs(tr, bw, half, layer, lambda i, j: i, lambda i, j: j),
        out_specs=pl.BlockSpec((tr, bw), lambda i, j: (i, j)),
        out_shape=jax.ShapeDtypeStruct((s, half * bw), BF16),
        compiler_params=_params(("parallel", "parallel")),
    )(a, a, a, a, cw, cw, cb, cb)


def _conv_bwd_dc(a, dact, cw, cb, layer, *, name, after=None, tr=512):
    nb, s, bw = a.shape
    half = nb // 2
    tr = min(tr, s)

    def body(*refs):
        gp_ref, gc_ref, vp_ref, vc_ref, cwg_ref, cwv_ref, cbg_ref, cbv_ref, da_ref = refs[:9]
        dc_ref, dw_ref, db_ref = refs[-3:]
        first = pl.program_id(1) == 0

        @pl.when(first)
        def _():
            dw_ref[...] = jnp.zeros_like(dw_ref)
            db_ref[...] = jnp.zeros_like(db_ref)

        for cs in _lane_chunks(bw, 128):
            gate, gtaps = _conv_taps(gp_ref, gc_ref, cwg_ref, cbg_ref, first, cs)
            val, vtaps = _conv_taps(vp_ref, vc_ref, cwv_ref, cbv_ref, first, cs)
            dact_v = da_ref[:, cs].astype(F32)
            sg = _sigmoid(gate)
            dgate = dact_v * val * (sg * (1.0 + gate * (1.0 - sg)))
            dval = dact_v * (gate * sg)
            dc_ref[0, :, cs] = dgate.astype(dc_ref.dtype)
            dc_ref[1, :, cs] = dval.astype(dc_ref.dtype)
            for kk in range(CONV_W):
                dw_ref[0, kk:kk + 1, cs] += jnp.sum(dgate * gtaps[kk], axis=0, keepdims=True)
                dw_ref[1, kk:kk + 1, cs] += jnp.sum(dval * vtaps[kk], axis=0, keepdims=True)
            db_ref[0, :, cs] += jnp.sum(dgate, axis=0, keepdims=True)
            db_ref[1, :, cs] += jnp.sum(dval, axis=0, keepdims=True)

    outs = pl.pallas_call(
        body, name=name, grid=(half, s // tr),
        in_specs=_conv_in_specs(tr, bw, half, layer, lambda j, i: i, lambda j, i: j)
        + [pl.BlockSpec((tr, bw), lambda j, i: (i, j))]
        + ([pl.BlockSpec(memory_space=pl.ANY)] if after is not None else []),
        out_specs=[pl.BlockSpec((2, None, tr, bw), lambda j, i: (0, j, i, 0)),
                   pl.BlockSpec((2, None, CONV_W, bw), lambda j, i: (0, j, 0, 0)),
                   pl.BlockSpec((2, None, 1, bw), lambda j, i: (0, j, 0, 0))],
        out_shape=[jax.ShapeDtypeStruct((2, half, s, bw), BF16),
                   jax.ShapeDtypeStruct((2, half, CONV_W, bw), F32),
                   jax.ShapeDtypeStruct((2, half, 1, bw), F32)],
        compiler_params=_params(("parallel", "arbitrary")),
    )(a, a, a, a, cw, cw, cb, cb, dact, *([after] if after is not None else []))
    dc, dw, db = outs
    return dc.reshape(nb, s, bw), dw.reshape(nb, CONV_W, bw), db.reshape(nb, 1, bw)


def _conv_bwd_da(dc, cw, layer, *, name, tr=512):
    nb, s, bw = dc.shape
    tr = min(tr, s)
    ni = s // tr
    per = tr // HALO
    last_halo = s // HALO - 1

    def body(c_ref, n_ref, w_ref, o_ref):
        last = pl.program_id(0) == ni - 1
        for cs in _lane_chunks(bw, 256):
            cur = c_ref[:, cs].astype(F32)
            nxt = n_ref[:, cs].astype(F32)
            da = (cur * w_ref[2:3, cs] + _shift_up(cur, nxt, 1, last) * w_ref[1:2, cs]
                  + _shift_up(cur, nxt, 2, last) * w_ref[0:1, cs])
            o_ref[:, cs] = da.astype(o_ref.dtype)

    tile = pl.BlockSpec((None, tr, bw), lambda i, j: (j, i, 0))
    return pl.pallas_call(
        body, name=name, grid=(ni, nb),
        in_specs=[tile,
                  pl.BlockSpec((None, HALO, bw), lambda i, j: (j, jnp.minimum((i + 1) * per, last_halo), 0)),
                  pl.BlockSpec((None, None, CONV_W, bw), lambda i, j: (j, layer, 0, 0))],
        out_specs=tile,
        out_shape=jax.ShapeDtypeStruct((nb, s, bw), BF16),
        compiler_params=_params(("parallel", "parallel")),
    )(dc, dc, cw)


def _rope_tables(positions):
    inv = 1.0 / (ROPE_THETA ** (jnp.arange(0, ROPE_DIM, 2, dtype=F32) / ROPE_DIM))
    ang = positions.astype(F32)[:, None] * inv
    return jnp.cos(ang), jnp.sin(ang)


def _heads_to_major(nope, r1, r2):
    s = r1.shape[0]
    parts = [nope[:, :G_W].reshape(s, G_HEADS, HEAD)]
    for r in (r1, r2):
        parts.append(jnp.broadcast_to(r.reshape(s, -1, ROPE_HALF), (s, G_HEADS, ROPE_HALF)))
    return jnp.concatenate(parts, axis=-1).transpose(1, 0, 2)


def _heads_from_major(t):
    s = t.shape[1]
    t = t.transpose(1, 0, 2)
    return t[:, :, :HEAD].reshape(s, G_W), t[:, :, HEAD:HEAD + ROPE_HALF], t[:, :, HEAD + ROPE_HALF:]


def _local_step(x, mem, positions, target, rep, fetch, emit):
    s, d = x.shape
    n_b = DEPTH - N_A
    tm = min(1024, s)
    cos, sin = _rope_tables(positions)
    cos12 = jnp.tile(cos, (1, G_HEADS))
    sin12 = jnp.tile(sin, (1, G_HEADS))
    r1_col = G_W // (G_HEADS * ROPE_HALF)
    b_sp_t = rep['b_sp'].transpose(0, 2, 1)

    saved = []
    kv = None
    shared = None
    for l in range(DEPTH):
        wm = fetch(('in', l), x)
        if l == 0:
            shared = {'g_v': wm['g_v'], 'conv_w': wm['conv_w']}
            bw = shared['conv_w'].shape[-1]
            conv_b = rep['conv_b'].reshape(-1, 1, bw)
        sv = {'x_in': x, 'wm': wm}
        if l == N_A:
            xn_kv = _rmsnorm(x, rep['g_kv'], name="kvnorm")
            kvx = _mm(xn_kv, wm['w_kv_a'], dims='nn', out_dtype=F32, name="kvproj", tm=tm, tn=KV_PAD)
            ckv = _rmsnorm(kvx, rep['g_kv_lat'], width=KV_RANK, name="ckvnorm")
            k1, k2 = _rope(kvx[:, KV_RANK:KV_RANK + ROPE_HALF], kvx[:, KV_RANK + ROPE_HALF:KV_RANK + ROPE_DIM],
                           cos, sin, name="krope")
            kv = {'x': x, 'xn': xn_kv, 'kvx': kvx, 'ckv': ckv, 'k1': k1, 'k2': k2, 'w_kv_a': wm['w_kv_a']}
        h = _rmsnorm(x, rep['g_mix'][l], name=f"mixnorm{l}")
        if l < N_A:
            z = _mm(h, wm['w_in'], dims='nn', out_dtype=BF16, name=f"in_a{l}", tm=tm, tn=wm['w_in'].shape[1] // 2)
            main = _sgu_fwd(z, shared['g_v'][l], rep['w_sp'][l], b_sp_t[l], name=f"sgu{l}")
            qcol = 2 * G_W // MEM_W
        else:
            j = l - N_A
            z = _mm(h, wm['w_in'], dims='nn', out_dtype=BF16, name=f"in_b{j}", tm=tm, tn=1024)
            qn = _rmsnorm(z, rep['g_q_lat'][j], width=Q_RANK, name=f"qnorm{j}")
            qp = _mm(qn, wm['w_uqp'], dims='nn', out_dtype=BF16, name=f"uq{j}", tm=tm, tn=768)
            rr1, rr2 = _rope(qp, qp, cos12, sin12, col1=r1_col, col2=r1_col + 1, name=f"qrope{j}")
            qh = _heads_to_major(qp, rr1, rr2)
            kn = _mm(kv['ckv'], wm['w_uk'], dims='nn', out_dtype=BF16, name=f"k_up{j}", tm=tm, tn=768)
            kh = _heads_to_major(kn, kv['k1'], kv['k2'])
            vv = _mm(kv['ckv'], wm['w_uv'], dims='nn', out_dtype=BF16, name=f"v_up{j}", tm=tm, tn=768)
            main, lse = _mha_fwd(qh, kh, vv, name=f"mha{j}")
            qcol = Q_RANK // MEM_W
            sv.update(qn=qn, qh=qh, kh=kh, vv=vv, lse=lse)
        wm.update(fetch(('rest', l), z))
        memn = _rmsnorm(mem, rep['g_mem'][l], name=f"memnorm{l}")
        kvm = _mm(memn, wm['w_mem_kv'], dims='nn', out_dtype=BF16, name=f"memkv{l}", tm=tm, tn=1024)
        mix = _memattn_fwd(z, kvm, main, qcol=qcol, name=f"memattn{l}")
        x_mid = _mm(mix, wm['w_out'], dims='nn', res=x, out_dtype=F32, name=f"out{l}", tm=tm, tn=1024)
        wf = fetch(('up', l), x_mid)
        h2 = _rmsnorm(x_mid, rep['g_ffn'][l], name=f"ffnnorm{l}")
        a = _mm(h2, wf['w_up'], dims='nn', b_blocked=True, out_dtype=BF16, out_block=bw,
                name=f"up{l}", tm=tm, tn=bw)
        act = _conv_fwd(a, shared['conv_w'], conv_b, l, name=f"conv{l}")
        wf.update(fetch(('down', l), act))
        x = _mm(act, wf['w_down'], dims='nn', res=x_mid, out_dtype=F32, name=f"down{l}", tm=512, tn=1024)
        sv.update(h=h, memn=memn, kvm=kvm, z=z, qcol=qcol, mix=mix, x_mid=x_mid, h2=h2, a=a, act=act, wf=wf)
        saved.append(sv)

    sq, dx, dg_final = _final_loss(x, target, rep['g_final'], name="loss")

    g = {k: [None] * DEPTH for k in ('g_mix', 'g_ffn', 'g_mem', 'conv_w', 'conv_b')}
    for k in ('g_v', 'w_sp', 'b_sp'):
        g[k] = [None] * N_A
    g['g_q_lat'] = [None] * n_b
    g['g_final'] = dg_final
    dckv_sum, dkr_sum = None, None

    for l in reversed(range(DEPTH)):
        sv = saved[l]
        wm, wf = sv['wm'], sv['wf']
        dact = _mm(dx, wf['w_down'], dims='nt', out_dtype=BF16, name=f"d_act{l}", tm=tm, tn=bw)
        dw_down = _mm(sv['act'], dx, dims='tn', out_dtype=BF16, name=f"dw_down{l}", tm=bw, tn=512)
        tok = emit(('down', l), {'w_ffn_down': dw_down})
        dc, dcw, dcb = _conv_bwd_dc(sv['a'], dact, shared['conv_w'], conv_b, l, after=tok, name=f"d_conv{l}")
        g['conv_w'][l], g['conv_b'][l] = dcw, dcb
        da = _conv_bwd_da(dc, shared['conv_w'], l, name=f"d_convin{l}")
        dw_up = _mm(sv['h2'], da, dims='tn', b_blocked=True, out_dtype=BF16, out_block=bw,
                    name=f"dw_up{l}", tm=512, tn=bw, n_outer=True)
        tok = emit(('up', l), {'w_ffn_up': dw_up})
        dh2 = _mm_blocked_nt(da, wf['w_up'], out_dtype=BF16, name=f"d_h2{l}", tm=512, tn=1024, blocks_per_step=4,
                             after=tok)
        dx_mid, g['g_ffn'][l] = _rmsnorm_bwd(sv['x_mid'], rep['g_ffn'][l], dh2, dres=dx, name=f"d_ffnnorm{l}")
        dmix = _mm(dx_mid, wm['w_out'], dims='nt', out_dtype=BF16, name=f"d_mix{l}", tm=tm, tn=1024)
        dw_out = _mm(sv['mix'], dx_mid, dims='tn', out_dtype=BF16, name=f"dw_out{l}", tm=1024, tn=512)
        dqm, dkvm = _memattn_bwd(sv['z'], sv['kvm'], dmix, qcol=sv['qcol'], name=f"d_memattn{l}")
        dw_memkv = _mm(sv['memn'], dkvm, dims='tn', out_dtype=BF16, name=f"dw_memkv{l}", tm=1024, tn=1024)
        tok = emit(('rest', l), {'w_out': dw_out, 'w_mem_kv': dw_memkv})
        gm = {}
        dmemn = _mm(dkvm, wm['w_mem_kv'], dims='nt', out_dtype=F32, name=f"d_memn{l}", tm=tm, tn=1024, after=tok)
        _, g['g_mem'][l] = _rmsnorm_bwd(mem, rep['g_mem'][l], dmemn, out_dtype=BF16, name=f"d_memnorm{l}")
        if l < N_A:
            dz, dwsp, dbsp_t, dgv = _sgu_bwd(sv['z'], dmix, dqm, shared['g_v'][l], rep['w_sp'][l], b_sp_t[l],
                                             name=f"d_sgu{l}")
            g['w_sp'][l], g['b_sp'][l], g['g_v'][l] = dwsp, dbsp_t.T, dgv
            dh = _mm(dz, wm['w_in'], dims='nt', out_dtype=BF16, name=f"d_h_a{l}", tm=tm, tn=1024)
            gm['w_in_a'] = _mm(sv['h'], dz, dims='tn', out_dtype=BF16, name=f"dw_in_a{l}", tm=1024, tn=512)
        else:
            j = l - N_A
            dqh, dkh, dvv = _mha_bwd(sv['qh'], sv['kh'], sv['vv'], sv['mix'], dmix, sv['lse'], name=f"d_mha{j}")
            dq_nope, dr1, dr2 = _heads_from_major(dqh)
            dkn, dk1, dk2 = _heads_from_major(dkh)
            dkr = jnp.concatenate([dk1.astype(F32).sum(axis=1), dk2.astype(F32).sum(axis=1)], axis=-1)
            gm['w_uk'] = _mm(kv['ckv'], dkn, dims='tn', out_dtype=BF16, name=f"dw_uk{j}", tm=512, tn=768)
            gm['w_uv'] = _mm(kv['ckv'], dvv, dims='tn', out_dtype=BF16, name=f"dw_uv{j}", tm=512, tn=768)
            dckv = _mm(dkn, wm['w_uk'], dims='nt', out_dtype=F32, res=dckv_sum, name=f"d_ckv_k{j}", tm=tm, tn=512)
            dckv_sum = _mm(dvv, wm['w_uv'], dims='nt', out_dtype=F32, res=dckv, name=f"d_ckv_v{j}", tm=tm, tn=512)
            dkr_sum = dkr if dkr_sum is None else dkr_sum + dkr
            dq1, dq2 = _rope(dr1.reshape(s, -1), dr2.reshape(s, -1), cos12, sin12, inverse=True, name=f"d_qrope{j}")
            dqp = jnp.concatenate([dq_nope, dq1, dq2], axis=-1)
            dqn = _mm(dqp, wm['w_uqp'], dims='nt', out_dtype=BF16, name=f"d_qn{j}", tm=tm, tn=512)
            gm['w_uqp'] = _mm(sv['qn'], dqp, dims='tn', out_dtype=BF16, name=f"dw_uq{j}", tm=512, tn=768)
            dqlat, g['g_q_lat'][j] = _rmsnorm_bwd(sv['z'], rep['g_q_lat'][j], dqn, width=Q_RANK, out_dtype=BF16,
                                                 name=f"d_qnorm{j}")
            dz = jnp.concatenate([dqlat, dqm], axis=-1)
            dh = _mm(dz, wm['w_in'], dims='nt', out_dtype=BF16, name=f"d_h_b{j}", tm=tm, tn=1024)
            gm['w_in_b'] = _mm(sv['h'], dz, dims='tn', out_dtype=BF16, name=f"dw_in_b{j}", tm=1024, tn=512)
        tok = emit(('mix', l), gm)
        dx, g['g_mix'][l] = _rmsnorm_bwd(sv['x_in'], rep['g_mix'][l], dh, dres=dx_mid, after=tok, name=f"d_mixnorm{l}")
        if l == N_A:
            dkvx_c, g['g_kv_lat'] = _rmsnorm_bwd(kv['kvx'], rep['g_kv_lat'], dckv_sum, width=KV_RANK, out_dtype=BF16,
                                                 name="d_ckvnorm")
            dk1, dk2 = _rope(dkr_sum[:, :ROPE_HALF], dkr_sum[:, ROPE_HALF:], cos, sin, inverse=True, name="d_krope")
            dkvx = jnp.concatenate([dkvx_c, dk1, dk2, jnp.zeros((s, KV_PAD - KV_RANK - ROPE_DIM), BF16)], axis=-1)
            dxn = _mm(dkvx, kv['w_kv_a'], dims='nt', out_dtype=BF16, name="d_kvnorm_in", tm=tm, tn=1024)
            dw_kv = _mm(kv['xn'], dkvx, dims='tn', out_dtype=BF16, name="dw_kv", tm=1024, tn=KV_PAD)
            tok = emit(('kv', 0), {'w_kv_a': dw_kv})
            dx, g['g_kv'] = _rmsnorm_bwd(kv['x'], rep['g_kv'], dxn, dres=dx, after=tok, name="d_kvnorm")
    return jnp.sum(sq), dx, g


MESH_IDS = pl.DeviceIdType.MESH
PEER_MASKS = tuple((k >> 2 & 1, k >> 1 & 1, k & 1) for k in range(1, N_DEV))
CHIP_MASKS = ((1, 0), (0, 1), (1, 1))
N_PEER = N_DEV - 1
SEMS_PER_BUFFER = 2 * N_PEER + 1
DATAFLOW = pltpu.SideEffectType.DATAFLOW_SIDE_EFFECTING
HBM_SPEC = pl.BlockSpec(memory_space=pltpu.HBM)
SEM_SPEC = pl.BlockSpec(memory_space=pltpu.SEMAPHORE)


def _my_position():
    return lax.axis_index("x"), lax.axis_index("y"), lax.axis_index("c")


def _flip(pos, mask):
    return tuple(1 - p if f else p for p, f in zip(pos, mask))


def _linear_id(pos):
    return 4 * pos[0] + 2 * pos[1] + pos[2]


def _hbm(x):
    return pltpu.with_memory_space_constraint(x, pltpu.HBM)


def _buffer_copies(src_ref, lead, land_ref, sems, scatter, near=False):
    me = _my_position()
    my_id = _linear_id(me)
    src = src_ref.at[lead] if lead else src_ref
    own = pltpu.make_async_copy(src.at[my_id] if scatter else src, land_ref.at[my_id], sems.at[2 * N_PEER])
    pairs = []
    for k, mask in enumerate(PEER_MASKS):
        if near and mask[2] == 1 and mask != (0, 0, 1):
            continue
        peer = _flip(me, mask)
        peer_id = _linear_id(peer)
        block = src.at[peer_id] if scatter else src
        send = pltpu.make_async_remote_copy(src_ref=block, dst_ref=land_ref.at[my_id], send_sem=sems.at[k],
                                            recv_sem=sems.at[N_PEER + k], device_id=peer, device_id_type=MESH_IDS)
        arrival = pltpu.make_async_remote_copy(src_ref=block, dst_ref=land_ref.at[peer_id], send_sem=sems.at[k],
                                               recv_sem=sems.at[N_PEER + k], device_id=peer, device_id_type=MESH_IDS)
        pairs.append((send, arrival))
    return own, pairs


def _exchange_start(srcs, buffers, *, name, scatter):
    ns, nb = len(srcs), len(buffers)
    lands = [_hbm(lax.empty((N_DEV,) + tuple(shape), dtype)) for _, _, shape, dtype, _ in buffers]

    def body(*refs):
        src_refs, land_refs = refs[:ns], refs[ns:ns + nb]
        sem_refs = refs[ns + nb:ns + 2 * nb]
        token = refs[-1]
        for b, (si, lead, _, _, near) in enumerate(buffers):
            own, pairs = _buffer_copies(src_refs[si], lead, land_refs[b], sem_refs[b], scatter, near)
            own.start()
            for send, _ in pairs:
                send.start()
        token[...] = jnp.zeros_like(token)

    out_shape = ([pltpu.SemaphoreType.DMA((SEMS_PER_BUFFER,))] * nb
                 + [pltpu.HBM(a.shape, a.dtype) for a in srcs]
                 + [pltpu.HBM(a.shape, a.dtype) for a in lands]
                 + [jax.ShapeDtypeStruct((8, 128), F32)])
    aliases = {i: nb + i for i in range(ns + nb)}
    outs = pl.pallas_call(
        body, name=name, in_specs=[HBM_SPEC] * (ns + nb),
        out_specs=[SEM_SPEC] * nb + [HBM_SPEC] * (ns + nb) + [pl.BlockSpec(memory_space=pltpu.VMEM)],
        out_shape=out_shape, input_output_aliases=aliases,
        compiler_params=pltpu.CompilerParams(has_side_effects=DATAFLOW),
    )(*[_hbm(a) for a in srcs], *lands)
    sems = list(outs[:nb])
    src_thru = list(outs[nb:nb + ns])
    land_thru = list(outs[nb + ns:nb + ns + nb])
    return sems, land_thru, src_thru, outs[-1]


def _exchange_wait(srcs_thru, buffers, sems, lands, after, *, name, scatter):
    ns, nb = len(srcs_thru), len(buffers)
    has_after = after is not None

    def body(*refs):
        src_refs, land_refs = refs[:ns], refs[ns:ns + nb]
        sem_refs = refs[ns + nb:ns + 2 * nb]
        for b, (si, lead, _, _, near) in enumerate(buffers):
            own, pairs = _buffer_copies(src_refs[si], lead, land_refs[b], sem_refs[b], scatter, near)
            for send, arrival in pairs:
                send.wait_send()
                arrival.wait_recv()
            own.wait()

    operands = list(srcs_thru) + list(lands) + list(sems) + ([after] if has_after else [])
    in_specs = ([HBM_SPEC] * (ns + nb) + [SEM_SPEC] * nb + ([pl.BlockSpec(memory_space=pl.ANY)] if has_after else []))
    outs = pl.pallas_call(
        body, name=name, in_specs=in_specs, out_specs=[HBM_SPEC] * nb,
        out_shape=[pltpu.HBM(a.shape, a.dtype) for a in lands],
        input_output_aliases={ns + b: b for b in range(nb)},
        compiler_params=pltpu.CompilerParams(has_side_effects=DATAFLOW),
    )(*operands)
    return list(outs)


def _exchange(arrays, *, name, scatter, near=None, after=None):
    n = len(arrays)
    near = [False] * n if near is None else near
    extra = [] if after is None else [after]
    out_shapes = [jax.ShapeDtypeStruct(a.shape if scatter else (N_DEV,) + a.shape, a.dtype) for a in arrays]

    def body(*refs):
        srcs, outs, sems = refs[:n], refs[n + len(extra):2 * n + len(extra)], refs[2 * n + len(extra):]
        started = []
        for a in range(n):
            own, pairs = _buffer_copies(srcs[a], (), outs[a], sems[a], scatter, near[a])
            own.start()
            for send, _ in pairs:
                send.start()
            started.append((own, pairs))
        for own, pairs in started:
            for send, arrival in pairs:
                arrival.wait_recv()
                send.wait_send()
            own.wait()

    any_spec = pl.BlockSpec(memory_space=pl.ANY)
    outs = pl.pallas_call(
        body, name=name, in_specs=[any_spec] * (n + len(extra)), out_specs=[any_spec] * n, out_shape=out_shapes,
        scratch_shapes=[pltpu.SemaphoreType.DMA((SEMS_PER_BUFFER,))] * n,
    )(*arrays, *extra)
    return list(outs)


def _forward_to_sibling(lands, *, name):
    n = len(lands)

    def body(*refs):
        ins, outs, sems = refs[:n], refs[n:2 * n], refs[2 * n:]
        me = _my_position()
        sibling = _flip(me, (0, 0, 1))
        pairs = []
        for b in range(n):
            for k, (fx, fy) in enumerate(CHIP_MASKS):
                mine = _linear_id(_flip(me, (fx, fy, 0)))
                theirs = _linear_id(_flip(me, (fx, fy, 1)))
                send = pltpu.make_async_remote_copy(
                    src_ref=ins[b].at[mine], dst_ref=outs[b].at[mine], send_sem=sems[b].at[k],
                    recv_sem=sems[b].at[len(CHIP_MASKS) + k], device_id=sibling, device_id_type=MESH_IDS)
                arrival = pltpu.make_async_remote_copy(
                    src_ref=ins[b].at[mine], dst_ref=outs[b].at[theirs], send_sem=sems[b].at[k],
                    recv_sem=sems[b].at[len(CHIP_MASKS) + k], device_id=sibling, device_id_type=MESH_IDS)
                send.start()
                pairs.append((send, arrival))
        for send, arrival in pairs:
            arrival.wait_recv()
            send.wait_send()

    any_spec = pl.BlockSpec(memory_space=pl.ANY)
    outs = pl.pallas_call(
        body, name=name, in_specs=[any_spec] * n, out_specs=[any_spec] * n,
        out_shape=[jax.ShapeDtypeStruct(a.shape, a.dtype) for a in lands],
        input_output_aliases={b: b for b in range(n)},
        scratch_shapes=[pltpu.SemaphoreType.DMA((2 * len(CHIP_MASKS),))] * n,
    )(*lands)
    return list(outs)


def _sum_slots(parts_ref):
    total = parts_ref[0].astype(F32)
    for q in range(1, parts_ref.shape[0]):
        total = total + parts_ref[q].astype(F32)
    return total


def _row_tile(rows, cols, n_arrays):
    budget = (24 * 1024 * 1024) // (4 * n_arrays * max(cols, 128))
    t = rows
    while t > budget and t % 2 == 0 and (t // 2) % 16 == 0:
        t //= 2
    return t


def _sum_adam(parts, w, m, v, layer, outs, *, name):
    q, r, c = parts.shape
    nl = w.shape[0]
    tr = _row_tile(r, c, q + 7)
    c1 = 1.0 - ADAM_B1 ** ADAM_STEP
    c2 = 1.0 - ADAM_B2 ** ADAM_STEP
    if outs is None:
        outs = [lax.empty((nl, r, c), F32) for _ in range(4)]

    def body(p_ref, w_ref, m_ref, v_ref, g_in, d_in, mo_in, vo_in, g_ref, d_ref, mo_ref, vo_ref):
        grad = _sum_slots(p_ref)
        m_new = ADAM_B1 * m_ref[...] + (1.0 - ADAM_B1) * grad
        v_new = ADAM_B2 * v_ref[...] + (1.0 - ADAM_B2) * (grad * grad)
        m_hat = m_new / c1
        v_hat = v_new / c2
        g_ref[...] = grad
        d_ref[...] = -ADAM_LR * (m_hat / (jnp.sqrt(v_hat) + ADAM_EPS) + ADAM_WD * w_ref[...])
        mo_ref[...] = m_new
        vo_ref[...] = v_new

    tile = pl.BlockSpec((None, tr, c), lambda i: (layer, i, 0))
    any_spec = pl.BlockSpec(memory_space=pl.ANY)
    return pl.pallas_call(
        body, name=name, grid=(r // tr,),
        in_specs=[pl.BlockSpec((q, tr, c), lambda i: (0, i, 0)), tile, tile, tile] + [any_spec] * 4,
        out_specs=[tile] * 4, out_shape=[jax.ShapeDtypeStruct((nl, r, c), F32)] * 4,
        input_output_aliases={4: 0, 5: 1, 6: 2, 7: 3},
        compiler_params=_params(("parallel",)),
    )(parts, w, m, v, *outs)


def _sum_parts(parts, *, name):
    q, r, c = parts.shape

    def body(p_ref, o_ref):
        o_ref[...] = _sum_slots(p_ref)

    return pl.pallas_call(
        body, name=name, in_specs=[pl.BlockSpec((q, r, c), lambda: (0, 0, 0))],
        out_specs=pl.BlockSpec((r, c), lambda: (0, 0)), out_shape=jax.ShapeDtypeStruct((r, c), F32),
        compiler_params=_params(),
    )(parts)


INPUT_NAMES = (['x', 'mem', 'positions'] + WEIGHTS + ['loss_target'] + ['m_' + n for n in WEIGHTS]
               + ['v_' + n for n in WEIGHTS])
SMALL_ALIGN = N_DEV * 8 * 128
TWO_LEVEL_LAYERS = N_A
GROUP_ORDER = ('in', 'rest', 'up', 'down')
GROUP_WEIGHTS = {'in': (['w_in_a'], ['w_in_b', 'w_uq', 'w_uk', 'w_uv']), 'rest': (['w_mem_kv', 'w_out'],) * 2,
                 'up': (['w_ffn_up'],) * 2, 'down': (['w_ffn_down'],) * 2}
LAYERED = {'w_in_a': 0, 'w_in_b': N_A, 'w_uq': N_A, 'w_uk': N_A, 'w_uv': N_A, 'w_mem_kv': 0, 'w_out': 0,
           'w_ffn_up': 0, 'w_ffn_down': 0}


def _permute_uq(w_uq):
    r = w_uq.shape[0]
    q = w_uq.reshape(r, G_HEADS, HEAD + ROPE_DIM)
    return jnp.concatenate([q[..., :HEAD].reshape(r, -1), q[..., HEAD:HEAD + ROPE_HALF].reshape(r, -1),
                            q[..., HEAD + ROPE_HALF:].reshape(r, -1)], axis=-1)


def _unpermute_uq(w_uqp):
    r = w_uqp.shape[0]
    nope = w_uqp[..., :G_W].reshape(r, G_HEADS, HEAD)
    r1 = w_uqp[..., G_W:G_W + G_HEADS * ROPE_HALF].reshape(r, G_HEADS, ROPE_HALF)
    r2 = w_uqp[..., G_W + G_HEADS * ROPE_HALF:].reshape(r, G_HEADS, ROPE_HALF)
    return jnp.concatenate([nope, r1, r2], axis=-1).reshape(r, -1)


def _cols_from_stack(st):
    _, r, n = st.shape
    return st.transpose(1, 0, 2).reshape(r, N_DEV * n)


def _cols_to_stack(wh):
    r, c = wh.shape
    return wh.reshape(r, N_DEV, c // N_DEV).transpose(1, 0, 2)


def _group_weights(group):
    kind, l = group
    return GROUP_WEIGHTS[kind][0 if l < N_A else 1]


def _step(args):
    p = dict(zip(INPUT_NAMES, args))
    x, mem, positions, target = p['x'][0], p['mem'][0], p['positions'][0], p['loss_target'][0]
    d = x.shape[-1]
    my_id = _linear_id(_my_position())

    w_kv_pad = jnp.pad(p['w_kv_a'], ((0, 0), (0, KV_PAD - p['w_kv_a'].shape[1])))
    shard = {k: p[k].astype(BF16) for k in LAYERED}
    shard['w_uk'] = shard['w_uk'].reshape(shard['w_uk'].shape[0], shard['w_uk'].shape[1], -1)
    shard['w_uv'] = shard['w_uv'].reshape(shard['w_uv'].shape[0], shard['w_uv'].shape[1], -1)
    shard.update(conv_w=p['conv_w'], g_v=p['g_v'], w_kv_a=w_kv_pad.astype(BF16))
    src_names = list(shard)
    gather_groups = []
    for l in range(DEPTH):
        gather_groups += [(kind, l) for kind in GROUP_ORDER]
    buffers, owner = [], []
    for group in gather_groups:
        kind, l = group
        for k in _group_weights(group):
            buffers.append((src_names.index(k), (l - LAYERED[k],), shard[k].shape[1:], shard[k].dtype, l < TWO_LEVEL_LAYERS))
            owner.append((group, k))
        if group == ('in', 0):
            for k in ('g_v', 'conv_w'):
                buffers.append((src_names.index(k), (), shard[k].shape, shard[k].dtype, True))
                owner.append((group, k))
        if group == ('in', N_A):
            buffers.append((src_names.index('w_kv_a'), (), shard['w_kv_a'].shape, BF16, False))
            owner.append((group, 'w_kv_a'))
    g_sems, g_lands, g_srcs, _ = _exchange_start([shard[k] for k in src_names], buffers, name="gather_start",
                                                 scatter=False)

    def fetch(group, after):
        idx = [i for i, (grp, _) in enumerate(owner) if grp == group]
        landed = _exchange_wait(g_srcs, [buffers[i] for i in idx], [g_sems[i] for i in idx],
                                [g_lands[i] for i in idx], after, name=f"gather_wait_{group[0]}{group[1]}",
                                scatter=False)
        if group[1] < TWO_LEVEL_LAYERS:
            landed = _forward_to_sibling(landed, name=f"gather_forward_{group[0]}{group[1]}")
        got = {owner[i][1]: t for i, t in zip(idx, landed)}
        out = {}
        for k, t in got.items():
            if k in ('w_in_a', 'w_uq'):
                out[k] = _cols_from_stack(t)
            elif k == 'g_v':
                out[k] = t.transpose(1, 0, 2).reshape(t.shape[1], -1)
            elif k in ('w_ffn_up', 'conv_w'):
                out[k] = t
            else:
                out[k] = t.reshape(-1, t.shape[-1])
        if 'w_uq' in out:
            out['w_uqp'] = _permute_uq(out.pop('w_uq'))
        for old, new in (('w_in_a', 'w_in'), ('w_in_b', 'w_in'), ('w_ffn_up', 'w_up'), ('w_ffn_down', 'w_down')):
            if old in out:
                out[new] = out.pop(old)
        return out

    pending = []

    def emit(group, grads):
        send = {}
        for k, t in grads.items():
            if k == 'w_in_a':
                send[k] = _cols_to_stack(t)
            elif k == 'w_uqp':
                send['w_uq'] = _cols_to_stack(_unpermute_uq(t))
            elif k == 'w_ffn_up':
                send[k] = t
            elif k == 'w_kv_a':
                cols = p['w_kv_a'].shape[1]
                send[k] = t[:, :cols].reshape(N_DEV, -1, cols)
            else:
                send[k] = t.reshape(N_DEV, t.shape[0] // N_DEV, t.shape[1])
        keys = list(send)
        bufs = [(i, (), send[k].shape[1:], send[k].dtype, False) for i, k in enumerate(keys)]
        sems, lands, srcs, token = _exchange_start([send[k] for k in keys], bufs,
                                                   name=f"scatter_start_{group[0]}{group[1]}", scatter=True)
        pending.append((group, keys, bufs, sems, lands, srcs))
        return token

    rep = {k: p[k] for k in REPLICATED}
    sq, grad_x, g = _local_step(x, mem, positions, target, rep, fetch, emit)
    loss = (0.5 / d) * lax.psum(sq, ("x", "y", "c"))

    out, running = {}, {}
    order = grad_x
    for group, keys, bufs, sems, lands, srcs in pending:
        landed = _exchange_wait(srcs, bufs, sems, lands, order, name=f"scatter_wait_{group[0]}{group[1]}", scatter=True)
        for k, parts in zip(keys, landed):
            stacked = k in LAYERED
            nl = p[k].shape[0] if stacked else 1
            layer = group[1] - LAYERED[k] if stacked else 0
            rows = p[k].size // nl // p[k].shape[-1]
            view = (nl, rows, p[k].shape[-1])
            running[k] = _sum_adam(parts.reshape(N_DEV, rows, view[2]), p[k].reshape(view), p['m_' + k].reshape(view),
                                   p['v_' + k].reshape(view), layer, running.get(k), name=f"adam_{k}{layer}")
            order = running[k][1]
    for k, res in running.items():
        out[k] = [t.reshape(p[k].shape) for t in res]

    small = {
        'g_mix': jnp.concatenate(g['g_mix']), 'g_ffn': jnp.concatenate(g['g_ffn']), 'g_final': g['g_final'],
        'w_sp': jnp.stack(g['w_sp']), 'b_sp': jnp.stack(g['b_sp']), 'g_kv': g['g_kv'], 'g_kv_lat': g['g_kv_lat'],
        'g_q_lat': jnp.concatenate(g['g_q_lat']), 'g_mem': jnp.concatenate(g['g_mem']),
        'conv_b': jnp.stack(g['conv_b']),
        'g_v': jnp.concatenate(g['g_v']),
        'conv_w': jnp.stack(g['conv_w']).transpose(0, 2, 1, 3),
    }
    small_names = REPLICATED + SMALL_SHARDED
    flat = jnp.concatenate([small[k].reshape(-1).astype(F32) for k in small_names])
    n_small = flat.shape[0]
    padded = -(-n_small // SMALL_ALIGN) * SMALL_ALIGN
    flat = jnp.pad(flat, (0, padded - n_small)).reshape(N_DEV, -1, 128)
    last_update = out[pending[-1][1][-1]][1]
    (small_parts,) = _exchange([flat], name="scatter_small", scatter=True, after=last_update)
    reduced = _sum_parts(small_parts, name="sum_small")
    (small_all,) = _exchange([reduced], name="gather_small", scatter=False)
    small_all = small_all.reshape(-1)
    grads_small, off = {}, 0
    for k in small_names:
        size = small[k].size
        grads_small[k] = small_all[off:off + size].reshape(small[k].shape)
        off += size
    grads_small['g_v'] = lax.dynamic_slice_in_dim(grads_small['g_v'], my_id * p['g_v'].shape[1], p['g_v'].shape[1], axis=1)
    grads_small['conv_w'] = lax.dynamic_index_in_dim(grads_small['conv_w'], my_id, axis=2, keepdims=False)
    gs = jnp.concatenate([grads_small[k].reshape(-1) for k in small_names])
    n_loc = gs.shape[0]
    pad_loc = -(-n_loc // 1024) * 1024 - n_loc

    def pack(prefix):
        t = jnp.concatenate([p[prefix + k].reshape(-1) for k in small_names])
        return jnp.pad(t, (0, pad_loc)).reshape(1, -1, 128)

    res = _sum_adam(jnp.pad(gs, (0, pad_loc)).reshape(1, -1, 128), pack(''), pack('m_'), pack('v_'), 0, None,
                    name="adam_small")
    off = 0
    for k in small_names:
        size = p[k].size
        out[k] = [t.reshape(-1)[off:off + size].reshape(p[k].shape) for t in res]
        off += size

    outs = [loss, grad_x[None]]
    for i in range(4):
        outs += [out[k][i] for k in WEIGHTS]
    return tuple(outs)


def kernel(x, mem, positions, g_mix, g_ffn, g_final, w_in_a, g_v, w_sp, b_sp, g_kv, w_kv_a, g_kv_lat, w_in_b, g_q_lat, w_uq, w_uk, w_uv, g_mem, w_mem_kv, w_out, w_ffn_up, conv_w, conv_b, w_ffn_down, loss_target, m_g_mix, m_g_ffn, m_g_final, m_w_in_a, m_g_v, m_w_sp, m_b_sp, m_g_kv, m_w_kv_a, m_g_kv_lat, m_w_in_b, m_g_q_lat, m_w_uq, m_w_uk, m_w_uv, m_g_mem, m_w_mem_kv, m_w_out, m_w_ffn_up, m_conv_w, m_conv_b, m_w_ffn_down, v_g_mix, v_g_ffn, v_g_final, v_w_in_a, v_g_v, v_w_sp, v_b_sp, v_g_kv, v_w_kv_a, v_g_kv_lat, v_w_in_b, v_g_q_lat, v_w_uq, v_w_uk, v_w_uv, v_g_mem, v_w_mem_kv, v_w_out, v_w_ffn_up, v_conv_w, v_conv_b, v_w_ffn_down):
    return _step((x, mem, positions, g_mix, g_ffn, g_final, w_in_a, g_v, w_sp, b_sp, g_kv, w_kv_a, g_kv_lat, w_in_b, g_q_lat, w_uq, w_uk, w_uv, g_mem, w_mem_kv, w_out, w_ffn_up, conv_w, conv_b, w_ffn_down, loss_target, m_g_mix, m_g_ffn, m_g_final, m_w_in_a, m_g_v, m_w_sp, m_b_sp, m_g_kv, m_w_kv_a, m_g_kv_lat, m_w_in_b, m_g_q_lat, m_w_uq, m_w_uk, m_w_uv, m_g_mem, m_w_mem_kv, m_w_out, m_w_ffn_up, m_conv_w, m_conv_b, m_w_ffn_down, v_g_mix, v_g_ffn, v_g_final, v_w_in_a, v_g_v, v_w_sp, v_b_sp, v_g_kv, v_w_kv_a, v_g_kv_lat, v_w_in_b, v_g_q_lat, v_w_uq, v_w_uk, v_w_uv, v_g_mem, v_w_mem_kv, v_w_out, v_w_ffn_up, v_conv_w, v_conv_b, v_w_ffn_down))
```

```python
import math

import jax
import jax.numpy as jnp
from jax import lax
from jax.experimental import pallas as pl
from jax.experimental.pallas import tpu as pltpu

F32 = jnp.float32
BF16 = jnp.bfloat16

N_DEV = 8
N_A = 2
DEPTH = 4
G_HEADS = 12
HEAD = 128
CHUNK = 128
MEM_HEADS = 4
MEM_W = MEM_HEADS * HEAD
G_W = G_HEADS * HEAD
ROPE_DIM = 64
ROPE_HALF = ROPE_DIM // 2
KV_RANK = 512
Q_RANK = 512
KV_PAD = 640
ROPE_THETA = 10000.0
EPS = 1e-6
CONV_W = 3

ADAM_LR = 0.001
ADAM_B1 = 0.9
ADAM_B2 = 0.999
ADAM_EPS = 1e-08
ADAM_WD = 0.01
ADAM_STEP = 10

VMEM_LIMIT_V7X = 56 * 1024 * 1024
MASK_VALUE = -1e30

WEIGHTS = ['g_mix', 'g_ffn', 'g_final', 'w_in_a', 'g_v', 'w_sp', 'b_sp', 'g_kv', 'w_kv_a', 'g_kv_lat',
           'w_in_b', 'g_q_lat', 'w_uq', 'w_uk', 'w_uv', 'g_mem', 'w_mem_kv', 'w_out', 'w_ffn_up',
           'conv_w', 'conv_b', 'w_ffn_down']
REPLICATED = ['g_mix', 'g_ffn', 'g_final', 'w_sp', 'b_sp', 'g_kv', 'g_kv_lat', 'g_q_lat', 'g_mem', 'conv_b']
SMALL_SHARDED = ['g_v', 'conv_w']


def _params(sem=None):
    return pltpu.CompilerParams(dimension_semantics=sem, vmem_limit_bytes=VMEM_LIMIT_V7X)


def _dot(a, b, dims):
    contract = {'nn': ((1,), (0,)), 'nt': ((1,), (1,)), 'tn': ((0,), (0,))}[dims]
    return lax.dot_general(a, b, (contract, ((), ())), preferred_element_type=F32)


def _erf(x):
    return lax.erf(x)


def _gelu(x):
    return 0.5 * x * (1.0 + _erf(x * (2.0 ** -0.5)))


def _gelu_grad(x):
    cdf = 0.5 * (1.0 + _erf(x * (2.0 ** -0.5)))
    pdf = jnp.exp(-0.5 * x * x) * (1.0 / math.sqrt(2.0 * math.pi))
    return cdf + x * pdf


def _sigmoid(x):
    return 1.0 / (1.0 + jnp.exp(-x))


def _operand_spec(shape, lead, blocked, tr, tc, ridx, cidx):
    if blocked:
        per = shape[-1] // tc
        assert shape[-1] % tc == 0, (shape, tc)
        return pl.BlockSpec(
            (None,) * (1 + len(lead)) + (tr, tc),
            lambda *g: (cidx(*g) // per,) + lead + (ridx(*g), cidx(*g) % per))
    return pl.BlockSpec((None,) * len(lead) + (tr, tc), lambda *g: lead + (ridx(*g), cidx(*g)))


def _view2d(x, blocked):
    return (x.shape[-2], x.shape[0] * x.shape[-1]) if blocked else (x.shape[-2], x.shape[-1])


def _mm(a, b, *, dims, out_dtype, name, tm, tn, tk=None, res=None, a_lead=(), b_lead=(),
        a_blocked=False, b_blocked=False, out_block=None, n_outer=False, after=None):
    ar, ac = _view2d(a, a_blocked)
    br, bc = _view2d(b, b_blocked)
    m, k = (ac, ar) if dims == 'tn' else (ar, ac)
    n, k2 = (br, bc) if dims == 'nt' else (bc, br)
    assert k == k2, (a.shape, b.shape, dims)
    tm, tn = min(tm, m), min(tn, n)
    tk = k if tk is None else tk
    assert m % tm == 0 and n % tn == 0 and k % tk == 0, (name, m, n, k, tm, tn, tk)
    nk = k // tk
    if n_outer:
        gi, gj = (lambda g0, g1, g2: g1), (lambda g0, g1, g2: g0)
        grid = (n // tn, m // tm, nk)
    else:
        gi, gj = (lambda g0, g1, g2: g0), (lambda g0, g1, g2: g1)
        grid = (m // tm, n // tn, nk)
    gk = lambda g0, g1, g2: g2

    if dims == 'tn':
        a_spec = _operand_spec(a.shape, a_lead, a_blocked, tk, tm, gk, gi)
    else:
        a_spec = _operand_spec(a.shape, a_lead, a_blocked, tm, tk, gi, gk)
    if dims == 'nt':
        b_spec = _operand_spec(b.shape, b_lead, b_blocked, tn, tk, gj, gk)
    else:
        b_spec = _operand_spec(b.shape, b_lead, b_blocked, tk, tn, gk, gj)
    in_specs = [a_spec, b_spec]
    operands = [a, b]
    if res is not None:
        in_specs.append(pl.BlockSpec((tm, tn), lambda *g: (gi(*g), gj(*g))))
        operands.append(res)
    if after is not None:
        in_specs.append(pl.BlockSpec(memory_space=pl.ANY))
        operands.append(after)
    n_in = len(operands)
    if out_block is not None:
        out_shape = jax.ShapeDtypeStruct((n // out_block, m, out_block), out_dtype)
        out_spec = _operand_spec(out_shape.shape, (), True, tm, tn, gi, gj)
    else:
        out_shape = jax.ShapeDtypeStruct((m, n), out_dtype)
        out_spec = pl.BlockSpec((tm, tn), lambda *g: (gi(*g), gj(*g)))

    def body(*refs):
        a_ref, b_ref = refs[0], refs[1]
        r_ref = refs[2] if res is not None else None
        o_ref = refs[n_in]
        acc_ref = refs[-1] if nk > 1 else None
        part = _dot(a_ref[...].astype(BF16), b_ref[...].astype(BF16), dims)

        def finish(total):
            if r_ref is not None:
                total = total + r_ref[...]
            o_ref[...] = total.astype(o_ref.dtype)

        if nk == 1:
            finish(part)
        else:
            kk = pl.program_id(2)

            @pl.when(kk == 0)
            def _():
                acc_ref[...] = part

            @pl.when(kk > 0)
            def _():
                acc_ref[...] += part

            @pl.when(kk == nk - 1)
            def _():
                finish(acc_ref[...])

    scratch = [pltpu.VMEM((tm, tn), F32)] if nk > 1 else []
    return pl.pallas_call(
        body, name=name, grid=grid, in_specs=in_specs, out_specs=out_spec,
        out_shape=out_shape, scratch_shapes=scratch,
        compiler_params=_params(("parallel", "parallel", "arbitrary")),
    )(*operands)


def _mm_blocked_nt(a, b, *, out_dtype, name, tm, tn, blocks_per_step, after=None):
    nb, m, bw = a.shape
    n = b.shape[1]
    tm, tn = min(tm, m), min(tn, n)
    assert nb % blocks_per_step == 0 and m % tm == 0 and n % tn == 0
    nk = nb // blocks_per_step

    def body(*refs):
        a_ref, b_ref, o_ref, acc_ref = refs[0], refs[1], refs[-2], refs[-1]
        kk = pl.program_id(2)
        part = _dot(a_ref[0], b_ref[0], 'nt')
        for t in range(1, blocks_per_step):
            part = part + _dot(a_ref[t], b_ref[t], 'nt')

        @pl.when(kk == 0)
        def _():
            acc_ref[...] = part

        @pl.when(kk > 0)
        def _():
            acc_ref[...] += part

        @pl.when(kk == nk - 1)
        def _():
            o_ref[...] = acc_ref[...].astype(o_ref.dtype)

    in_specs = [pl.BlockSpec((blocks_per_step, tm, bw), lambda i, j, k: (k, i, 0)),
                pl.BlockSpec((blocks_per_step, tn, bw), lambda i, j, k: (k, j, 0))]
    operands = [a, b]
    if after is not None:
        in_specs.append(pl.BlockSpec(memory_space=pl.ANY))
        operands.append(after)
    return pl.pallas_call(
        body, name=name, grid=(m // tm, n // tn, nk), in_specs=in_specs,
        out_specs=pl.BlockSpec((tm, tn), lambda i, j, k: (i, j)), out_shape=jax.ShapeDtypeStruct((m, n), out_dtype),
        scratch_shapes=[pltpu.VMEM((tm, tn), F32)],
        compiler_params=_params(("parallel", "parallel", "arbitrary")),
    )(*operands)


def _rmsnorm(x, g, *, name, width=None, out_dtype=BF16, tm=512):
    s = x.shape[0]
    w = x.shape[1] if width is None else width
    tm = min(tm, s)

    def body(x_ref, g_ref, o_ref):
        xv = x_ref[...].astype(F32)
        rstd = lax.rsqrt(jnp.mean(xv * xv, axis=-1, keepdims=True) + EPS)
        o_ref[...] = (xv * rstd * g_ref[...]).astype(o_ref.dtype)

    return pl.pallas_call(
        body, name=name, grid=(s // tm,),
        in_specs=[pl.BlockSpec((tm, w), lambda i: (i, 0)), pl.BlockSpec((1, w), lambda i: (0, 0))],
        out_specs=pl.BlockSpec((tm, w), lambda i: (i, 0)),
        out_shape=jax.ShapeDtypeStruct((s, w), out_dtype),
        compiler_params=_params(("parallel",)),
    )(x, g.reshape(1, w))


def _rmsnorm_bwd(x, g, dy, *, name, width=None, dres=None, after=None, out_dtype=F32, tm=512):
    s = x.shape[0]
    w = x.shape[1] if width is None else width
    tm = min(tm, s)

    def body(*refs):
        x_ref, g_ref, dy_ref = refs[0], refs[1], refs[2]
        r_ref = refs[3] if dres is not None else None
        dx_ref, dg_ref = refs[-2], refs[-1]
        xv = x_ref[...].astype(F32)
        rstd = lax.rsqrt(jnp.mean(xv * xv, axis=-1, keepdims=True) + EPS)
        xhat = xv * rstd
        dyv = dy_ref[...].astype(F32)
        gdy = dyv * g_ref[...]
        dx = rstd * (gdy - xhat * jnp.mean(gdy * xhat, axis=-1, keepdims=True))
        if r_ref is not None:
            dx = dx + r_ref[...]
        dx_ref[...] = dx.astype(dx_ref.dtype)
        part = jnp.sum(dyv * xhat, axis=0, keepdims=True)

        @pl.when(pl.program_id(0) == 0)
        def _():
            dg_ref[...] = part

        @pl.when(pl.program_id(0) > 0)
        def _():
            dg_ref[...] += part

    row = pl.BlockSpec((tm, w), lambda i: (i, 0))
    vec = pl.BlockSpec((1, w), lambda i: (0, 0))
    in_specs = [row, vec, row] + ([row] if dres is not None else [])
    operands = [x, g.reshape(1, w), dy] + ([dres] if dres is not None else [])
    if after is not None:
        in_specs.append(pl.BlockSpec(memory_space=pl.ANY))
        operands.append(after)
    return pl.pallas_call(
        body, name=name, grid=(s // tm,), in_specs=in_specs, out_specs=[row, vec],
        out_shape=[jax.ShapeDtypeStruct((s, w), out_dtype), jax.ShapeDtypeStruct((1, w), F32)],
        compiler_params=_params(("arbitrary",)),
    )(*operands)


def _final_loss(x, target, g, *, name, tm=256):
    s, d = x.shape
    tm = min(tm, s)

    def body(x_ref, t_ref, g_ref, sq_ref, dx_ref, dg_ref):
        xv = x_ref[...]
        rstd = lax.rsqrt(jnp.mean(xv * xv, axis=-1, keepdims=True) + EPS)
        xhat = xv * rstd
        err = xhat * g_ref[...] - t_ref[...]
        dyv = err * (1.0 / d)
        gdy = dyv * g_ref[...]
        dx_ref[...] = rstd * (gdy - xhat * jnp.mean(gdy * xhat, axis=-1, keepdims=True))
        sq = jnp.sum(err * err, axis=0, keepdims=True)
        dg = jnp.sum(dyv * xhat, axis=0, keepdims=True)

        @pl.when(pl.program_id(0) == 0)
        def _():
            sq_ref[...] = sq
            dg_ref[...] = dg

        @pl.when(pl.program_id(0) > 0)
        def _():
            sq_ref[...] += sq
            dg_ref[...] += dg

    row = pl.BlockSpec((tm, d), lambda i: (i, 0))
    vec = pl.BlockSpec((1, d), lambda i: (0, 0))
    return pl.pallas_call(
        body, name=name, grid=(s // tm,), in_specs=[row, row, vec], out_specs=[vec, row, vec],
        out_shape=[jax.ShapeDtypeStruct((1, d), F32), jax.ShapeDtypeStruct((s, d), F32),
                   jax.ShapeDtypeStruct((1, d), F32)],
        compiler_params=_params(("arbitrary",)),
    )(x, target, g.reshape(1, d))


def _tril_mask():
    t = lax.broadcasted_iota(jnp.int32, (CHUNK, CHUNK), 0)
    s = lax.broadcasted_iota(jnp.int32, (CHUNK, CHUNK), 1)
    return t >= s


def _sgu_fwd(z, g_v, w_sp, b_sp_t, *, name):
    s = z.shape[0]

    def body(zu_ref, zv_ref, g_ref, w_ref, b_ref, o_ref):
        u = _gelu(zu_ref[...].astype(F32))
        gv = _gelu(zv_ref[...].astype(F32))
        rstd = lax.rsqrt(jnp.mean(gv * gv, axis=-1, keepdims=True) + EPS)
        v = (gv * rstd * g_ref[...]).astype(BF16)
        mask = _tril_mask()
        for grp in range(G_HEADS):
            cols = slice(grp * HEAD, (grp + 1) * HEAD)
            wm = jnp.where(mask, w_ref[grp], 0.0).astype(BF16)
            sv = _dot(wm, v[:, cols], 'nn') + b_ref[:, grp:grp + 1]
            o_ref[:, cols] = (u[:, cols] * sv).astype(o_ref.dtype)

    return pl.pallas_call(
        body, name=name, grid=(s // CHUNK,),
        in_specs=[pl.BlockSpec((CHUNK, G_W), lambda i: (i, 0)),
                  pl.BlockSpec((CHUNK, G_W), lambda i: (i, 1)),
                  pl.BlockSpec((1, G_W), lambda i: (0, 0)),
                  pl.BlockSpec((G_HEADS, CHUNK, CHUNK), lambda i: (0, 0, 0)),
                  pl.BlockSpec((CHUNK, G_HEADS), lambda i: (0, 0))],
        out_specs=pl.BlockSpec((CHUNK, G_W), lambda i: (i, 0)),
        out_shape=jax.ShapeDtypeStruct((s, G_W), BF16),
        compiler_params=_params(("parallel",)),
    )(z, z, g_v.reshape(1, G_W), w_sp, b_sp_t)


def _sgu_bwd(z, dmix, dqm, g_v, w_sp, b_sp_t, *, name):
    s = z.shape[0]
    zw = z.shape[1]

    def body(zu_ref, zv_ref, dm_ref, dq_ref, g_ref, w_ref, b_ref, dz_ref, dw_ref, db_ref, dg_ref):
        first = pl.program_id(0) == 0

        @pl.when(first)
        def _():
            dw_ref[...] = jnp.zeros_like(dw_ref)
            db_ref[...] = jnp.zeros_like(db_ref)
            dg_ref[...] = jnp.zeros_like(dg_ref)

        zu = zu_ref[...].astype(F32)
        zv = zv_ref[...].astype(F32)
        dmain = dm_ref[...].astype(F32)
        u = _gelu(zu)
        gv = _gelu(zv)
        rstd = lax.rsqrt(jnp.mean(gv * gv, axis=-1, keepdims=True) + EPS)
        vhat = gv * rstd
        gvec = g_ref[...]
        v = (vhat * gvec).astype(BF16)
        dsv = dmain * u
        dsv_b = dsv.astype(BF16)
        mask = _tril_mask()
        dv_parts = []
        for grp in range(G_HEADS):
            cols = slice(grp * HEAD, (grp + 1) * HEAD)
            wm = jnp.where(mask, w_ref[grp], 0.0).astype(BF16)
            sv = _dot(wm, v[:, cols], 'nn') + b_ref[:, grp:grp + 1]
            dz_ref[:, cols] = (dmain[:, cols] * sv * _gelu_grad(zu[:, cols])).astype(dz_ref.dtype)
            dwg = _dot(dsv_b[:, cols], v[:, cols], 'nt')
            dw_ref[grp] += jnp.where(mask, dwg, 0.0)
            db_ref[:, grp:grp + 1] += jnp.sum(dsv[:, cols], axis=-1, keepdims=True)
            dv_parts.append(_dot(wm, dsv_b[:, cols], 'tn'))
        dv = jnp.concatenate(dv_parts, axis=-1)
        dg_ref[...] += jnp.sum(dv * vhat, axis=0, keepdims=True)
        gdv = dv * gvec
        dgv = rstd * (gdv - vhat * jnp.mean(gdv * vhat, axis=-1, keepdims=True))
        dz_ref[:, G_W:2 * G_W] = (dgv * _gelu_grad(zv)).astype(dz_ref.dtype)
        dz_ref[:, 2 * G_W:] = dq_ref[...].astype(dz_ref.dtype)

    return pl.pallas_call(
        body, name=name, grid=(s // CHUNK,),
        in_specs=[pl.BlockSpec((CHUNK, G_W), lambda i: (i, 0)),
                  pl.BlockSpec((CHUNK, G_W), lambda i: (i, 1)),
                  pl.BlockSpec((CHUNK, G_W), lambda i: (i, 0)),
                  pl.BlockSpec((CHUNK, MEM_W), lambda i: (i, 0)),
                  pl.BlockSpec((1, G_W), lambda i: (0, 0)),
                  pl.BlockSpec((G_HEADS, CHUNK, CHUNK), lambda i: (0, 0, 0)),
                  pl.BlockSpec((CHUNK, G_HEADS), lambda i: (0, 0))],
        out_specs=[pl.BlockSpec((CHUNK, zw), lambda i: (i, 0)),
                   pl.BlockSpec((G_HEADS, CHUNK, CHUNK), lambda i: (0, 0, 0)),
                   pl.BlockSpec((CHUNK, G_HEADS), lambda i: (0, 0)),
                   pl.BlockSpec((1, G_W), lambda i: (0, 0))],
        out_shape=[jax.ShapeDtypeStruct((s, zw), BF16),
                   jax.ShapeDtypeStruct((G_HEADS, CHUNK, CHUNK), F32),
                   jax.ShapeDtypeStruct((CHUNK, G_HEADS), F32),
                   jax.ShapeDtypeStruct((1, G_W), F32)],
        compiler_params=_params(("arbitrary",)),
    )(z, z, dmix, dqm, g_v.reshape(1, G_W), w_sp, b_sp_t)


def _mem_probs(q, k):
    sc = _dot(q, k, 'nt') * (HEAD ** -0.5)
    sc = sc - jnp.max(sc, axis=-1, keepdims=True)
    e = jnp.exp(sc)
    return e / jnp.sum(e, axis=-1, keepdims=True)


def _memattn_fwd(z, kvm, main, *, qcol, name, tm=512):
    s = z.shape[0]
    m = kvm.shape[0]
    tm = min(tm, s)

    def body(q_ref, kv_ref, main_ref, o_ref):
        o_ref[:, :G_W] = main_ref[...]
        for h in range(MEM_HEADS):
            cols = slice(h * HEAD, (h + 1) * HEAD)
            k = kv_ref[:, cols]
            v = kv_ref[:, MEM_W + h * HEAD:MEM_W + (h + 1) * HEAD]
            p = _mem_probs(q_ref[:, cols], k)
            o_ref[:, G_W + h * HEAD:G_W + (h + 1) * HEAD] = _dot(p.astype(BF16), v, 'nn').astype(o_ref.dtype)

    return pl.pallas_call(
        body, name=name, grid=(s // tm,),
        in_specs=[pl.BlockSpec((tm, MEM_W), lambda i: (i, qcol)),
                  pl.BlockSpec((m, 2 * MEM_W), lambda i: (0, 0)),
                  pl.BlockSpec((tm, G_W), lambda i: (i, 0))],
        out_specs=pl.BlockSpec((tm, G_W + MEM_W), lambda i: (i, 0)),
        out_shape=jax.ShapeDtypeStruct((s, G_W + MEM_W), BF16),
        compiler_params=_params(("parallel",)),
    )(z, kvm, main)


def _memattn_bwd(z, kvm, dmix, *, qcol, name, tm=512):
    s = z.shape[0]
    m = kvm.shape[0]
    tm = min(tm, s)
    scale = HEAD ** -0.5

    def body(q_ref, kv_ref, do_ref, dq_ref, dkv_ref):
        @pl.when(pl.program_id(0) == 0)
        def _():
            dkv_ref[...] = jnp.zeros_like(dkv_ref)

        for h in range(MEM_HEADS):
            cols = slice(h * HEAD, (h + 1) * HEAD)
            vcols = slice(MEM_W + h * HEAD, MEM_W + (h + 1) * HEAD)
            q = q_ref[:, cols]
            k = kv_ref[:, cols]
            v = kv_ref[:, vcols]
            do = do_ref[:, cols]
            p = _mem_probs(q, k)
            dp = _dot(do, v, 'nt')
            ds = (p * (dp - jnp.sum(dp * p, axis=-1, keepdims=True)) * scale).astype(BF16)
            dq_ref[:, cols] = _dot(ds, k, 'nn').astype(dq_ref.dtype)
            dkv_ref[:, cols] += _dot(ds, q, 'tn')
            dkv_ref[:, vcols] += _dot(p.astype(BF16), do, 'tn')

    mo_block = G_W // MEM_W
    return pl.pallas_call(
        body, name=name, grid=(s // tm,),
        in_specs=[pl.BlockSpec((tm, MEM_W), lambda i: (i, qcol)),
                  pl.BlockSpec((m, 2 * MEM_W), lambda i: (0, 0)),
                  pl.BlockSpec((tm, MEM_W), lambda i: (i, mo_block))],
        out_specs=[pl.BlockSpec((tm, MEM_W), lambda i: (i, 0)),
                   pl.BlockSpec((m, 2 * MEM_W), lambda i: (0, 0))],
        out_shape=[jax.ShapeDtypeStruct((s, MEM_W), BF16), jax.ShapeDtypeStruct((m, 2 * MEM_W), F32)],
        compiler_params=_params(("arbitrary",)),
    )(z, kvm, dmix)


def _rope(x1, x2, cos, sin, *, name, inverse=False, out_dtype=BF16, col1=0, col2=0, tm=512):
    s, w = cos.shape
    tm = min(tm, s)
    sign = -1.0 if inverse else 1.0

    def body(a_ref, b_ref, c_ref, s_ref, o1_ref, o2_ref):
        a = a_ref[...].astype(F32)
        b = b_ref[...].astype(F32)
        c = c_ref[...]
        sn = s_ref[...] * sign
        o1_ref[...] = (a * c - b * sn).astype(o1_ref.dtype)
        o2_ref[...] = (b * c + a * sn).astype(o2_ref.dtype)

    row = pl.BlockSpec((tm, w), lambda i: (i, 0))
    return pl.pallas_call(
        body, name=name, grid=(s // tm,),
        in_specs=[pl.BlockSpec((tm, w), lambda i: (i, col1)), pl.BlockSpec((tm, w), lambda i: (i, col2)), row, row],
        out_specs=[row, row],
        out_shape=[jax.ShapeDtypeStruct((s, w), out_dtype)] * 2,
        compiler_params=_params(("parallel",)),
    )(x1, x2, cos, sin)


MHA_BLOCK = 1024
QK_DIM = HEAD + ROPE_DIM


def _mha_scores(q, k, scale, diagonal):
    sc = _dot(q, k, 'nt') * scale
    if not diagonal:
        return sc, None
    rows = lax.broadcasted_iota(jnp.int32, sc.shape, 0)
    cols = lax.broadcasted_iota(jnp.int32, sc.shape, 1)
    return sc, cols <= rows


def _mha_fwd(q, k, vv, *, name):
    s = k.shape[1]
    tb = min(MHA_BLOCK, s)
    scale = QK_DIM ** -0.5

    def body(q_ref, k_ref, v_ref, o_ref, lse_ref, m_ref, l_ref, acc_ref):
        i = pl.program_id(1)
        qh = q_ref[...]
        m_ref[...] = jnp.full_like(m_ref, MASK_VALUE)
        l_ref[...] = jnp.zeros_like(l_ref)
        acc_ref[...] = jnp.zeros_like(acc_ref)

        def block(j, diagonal):
            ks = pl.multiple_of(j * tb, tb)
            kj, vj = k_ref[pl.ds(ks, tb), :], v_ref[pl.ds(ks, tb), :]
            sc, keep = _mha_scores(qh, kj, scale, diagonal)
            if diagonal:
                sc = jnp.where(keep, sc, MASK_VALUE)
            m_old = m_ref[...]
            m_new = jnp.maximum(m_old, jnp.max(sc, axis=-1, keepdims=True))
            p = jnp.exp(sc - m_new)
            alpha = jnp.exp(m_old - m_new)
            l_ref[...] = alpha * l_ref[...] + jnp.sum(p, axis=-1, keepdims=True)
            acc_ref[...] = alpha * acc_ref[...] + _dot(p.astype(BF16), vj, 'nn')
            m_ref[...] = m_new

        def step(j, carry):
            block(j, False)
            return carry

        lax.fori_loop(0, i, step, 0)
        block(i, True)
        l = l_ref[...]
        o_ref[...] = (acc_ref[...] / l).astype(o_ref.dtype)
        lse_ref[...] = m_ref[...] + jnp.log(l)

    return pl.pallas_call(
        body, name=name, grid=(G_HEADS, s // tb),
        in_specs=[pl.BlockSpec((None, tb, QK_DIM), lambda h, i: (h, i, 0)),
                  pl.BlockSpec((None, s, QK_DIM), lambda h, i: (h, 0, 0)),
                  pl.BlockSpec((s, HEAD), lambda h, i: (0, h))],
        out_specs=[pl.BlockSpec((tb, HEAD), lambda h, i: (i, h)),
                   pl.BlockSpec((None, tb, 1), lambda h, i: (h, i, 0))],
        out_shape=[jax.ShapeDtypeStruct((s, G_W), BF16), jax.ShapeDtypeStruct((G_HEADS, s, 1), F32)],
        scratch_shapes=[pltpu.VMEM((tb, 1), F32), pltpu.VMEM((tb, 1), F32), pltpu.VMEM((tb, HEAD), F32)],
        compiler_params=_params(("parallel", "arbitrary")),
    )(q, k, vv)


def _mha_bwd(q, k, vv, o, do, lse, *, name):
    s = k.shape[1]
    tb = min(MHA_BLOCK, s)
    nq = s // tb
    scale = QK_DIM ** -0.5

    def body(q_ref, k_ref, v_ref, o_ref, do_ref, lse_ref, dq_ref, dk_ref, dv_ref, dqa_ref, dka_ref, dva_ref):
        i = pl.program_id(1)

        @pl.when(i == 0)
        def _():
            dka_ref[...] = jnp.zeros_like(dka_ref)
            dva_ref[...] = jnp.zeros_like(dva_ref)

        qh, dov = q_ref[...], do_ref[...]
        delta = jnp.sum(dov.astype(F32) * o_ref[...].astype(F32), axis=-1, keepdims=True)
        lsev = lse_ref[...]
        dqa_ref[...] = jnp.zeros_like(dqa_ref)

        def block(j, diagonal):
            ks = pl.multiple_of(j * tb, tb)
            kj, vj = k_ref[pl.ds(ks, tb), :], v_ref[pl.ds(ks, tb), :]
            sc, keep = _mha_scores(qh, kj, scale, diagonal)
            p = jnp.exp(sc - lsev)
            if diagonal:
                p = jnp.where(keep, p, 0.0)
            dp = _dot(dov, vj, 'nt')
            ds = (p * (dp - delta) * scale).astype(BF16)
            dqa_ref[...] += _dot(ds, kj, 'nn')
            dka_ref[pl.ds(ks, tb), :] += _dot(ds, qh, 'tn')
            dva_ref[pl.ds(ks, tb), :] += _dot(p.astype(BF16), dov, 'tn')

        def step(j, carry):
            block(j, False)
            return carry

        lax.fori_loop(0, i, step, 0)
        block(i, True)
        dq_ref[...] = dqa_ref[...].astype(dq_ref.dtype)

        @pl.when(i == nq - 1)
        def _():
            dk_ref[...] = dka_ref[...].astype(dk_ref.dtype)
            dv_ref[...] = dva_ref[...].astype(dv_ref.dtype)

    q_tile = pl.BlockSpec((None, tb, QK_DIM), lambda h, i: (h, i, 0))
    k_head = pl.BlockSpec((None, s, QK_DIM), lambda h, i: (h, 0, 0))
    tile = pl.BlockSpec((tb, HEAD), lambda h, i: (i, h))
    v_head = pl.BlockSpec((s, HEAD), lambda h, i: (0, h))
    return pl.pallas_call(
        body, name=name, grid=(G_HEADS, nq),
        in_specs=[q_tile, k_head, v_head, tile, tile, pl.BlockSpec((None, tb, 1), lambda h, i: (h, i, 0))],
        out_specs=[q_tile, k_head, v_head],
        out_shape=[jax.ShapeDtypeStruct((G_HEADS, s, QK_DIM), BF16), jax.ShapeDtypeStruct((G_HEADS, s, QK_DIM), BF16),
                   jax.ShapeDtypeStruct((s, G_W), BF16)],
        scratch_shapes=[pltpu.VMEM((tb, QK_DIM), F32), pltpu.VMEM((s, QK_DIM), F32), pltpu.VMEM((s, HEAD), F32)],
        compiler_params=_params(("parallel", "arbitrary")),
    )(q, k, vv, o, do, lse)


HALO = 16


def _shift_down(prev, cur, shift, first_tile):
    tr = cur.shape[0]
    full = jnp.concatenate([prev, cur], axis=0)
    out = pltpu.roll(full, shift, axis=0)[HALO:]
    row = lax.broadcasted_iota(jnp.int32, (tr, 1), 0)
    return jnp.where(jnp.logical_and(first_tile, row < shift), 0.0, out)


def _shift_up(cur, nxt, shift, last_tile):
    tr = cur.shape[0]
    full = jnp.concatenate([cur, nxt], axis=0)
    out = pltpu.roll(full, tr + HALO - shift, axis=0)[:tr]
    row = lax.broadcasted_iota(jnp.int32, (tr, 1), 0)
    return jnp.where(jnp.logical_and(last_tile, row >= tr - shift), 0.0, out)


def _lane_chunks(width, lanes):
    return [slice(c0, min(c0 + lanes, width)) for c0 in range(0, width, lanes)]


def _conv_taps(prev_ref, cur_ref, cw_ref, cb_ref, first_tile, cs):
    cur = cur_ref[:, cs].astype(F32)
    prev = prev_ref[:, cs].astype(F32)
    a1 = _shift_down(prev, cur, 1, first_tile)
    a2 = _shift_down(prev, cur, 2, first_tile)
    c = a2 * cw_ref[0:1, cs] + a1 * cw_ref[1:2, cs] + cur * cw_ref[2:3, cs] + cb_ref[:, cs]
    return c, (a2, a1, cur)


def _conv_in_specs(tr, bw, half, layer, row_of, blk_of):
    per = tr // HALO
    specs = []
    for off in (0, half):
        specs.append(pl.BlockSpec((None, HALO, bw), lambda *g, off=off: (blk_of(*g) + off, jnp.maximum(row_of(*g) * per - 1, 0), 0)))
        specs.append(pl.BlockSpec((None, tr, bw), lambda *g, off=off: (blk_of(*g) + off, row_of(*g), 0)))
    for off in (0, half):
        specs.append(pl.BlockSpec((None, None, CONV_W, bw), lambda *g, off=off: (blk_of(*g) + off, layer, 0, 0)))
    for off in (0, half):
        specs.append(pl.BlockSpec((None, 1, bw), lambda *g, off=off: (layer * 2 * half + blk_of(*g) + off, 0, 0)))
    return specs


def _conv_fwd(a, cw, cb, layer, *, name, tr=1024):
    nb, s, bw = a.shape
    half = nb // 2
    tr = min(tr, s)

    def body(gp_ref, gc_ref, vp_ref, vc_ref, cwg_ref, cwv_ref, cbg_ref, cbv_ref, o_ref):
        first = pl.program_id(0) == 0
        for cs in _lane_chunks(bw, 256):
            gate, _ = _conv_taps(gp_ref, gc_ref, cwg_ref, cbg_ref, first, cs)
            val, _ = _conv_taps(vp_ref, vc_ref, cwv_ref, cbv_ref, first, cs)
            o_ref[:, cs] = (gate * _sigmoid(gate) * val).astype(o_ref.dtype)

    return pl.pallas_call(
        body, name=name, grid=(s // tr, half),
        in_specs=_conv_in_specs(tr, bw, half, layer, lambda i, j: i, lambda i, j: j),
        out_specs=pl.BlockSpec((tr, bw), lambda i, j: (i, j)),
        out_shape=jax.ShapeDtypeStruct((s, half * bw), BF16),
        compiler_params=_params(("parallel", "parallel")),
    )(a, a, a, a, cw, cw, cb, cb)


def _conv_bwd_dc(a, dact, cw, cb, layer, *, name, after=None, tr=1024):
    nb, s, bw = a.shape
    half = nb // 2
    tr = min(tr, s)

    def body(*refs):
        gp_ref, gc_ref, vp_ref, vc_ref, cwg_ref, cwv_ref, cbg_ref, cbv_ref, da_ref = refs[:9]
        dc_ref, dw_ref, db_ref = refs[-3:]
        first = pl.program_id(1) == 0

        @pl.when(first)
        def _():
            dw_ref[...] = jnp.zeros_like(dw_ref)
            db_ref[...] = jnp.zeros_like(db_ref)

        for cs in _lane_chunks(bw, 128):
            gate, gtaps = _conv_taps(gp_ref, gc_ref, cwg_ref, cbg_ref, first, cs)
            val, vtaps = _conv_taps(vp_ref, vc_ref, cwv_ref, cbv_ref, first, cs)
            dact_v = da_ref[:, cs].astype(F32)
            sg = _sigmoid(gate)
            dgate = dact_v * val * (sg * (1.0 + gate * (1.0 - sg)))
            dval = dact_v * (gate * sg)
            dc_ref[0, :, cs] = dgate.astype(dc_ref.dtype)
            dc_ref[1, :, cs] = dval.astype(dc_ref.dtype)
            for kk in range(CONV_W):
                dw_ref[0, kk:kk + 1, cs] += jnp.sum(dgate * gtaps[kk], axis=0, keepdims=True)
                dw_ref[1, kk:kk + 1, cs] += jnp.sum(dval * vtaps[kk], axis=0, keepdims=True)
            db_ref[0, :, cs] += jnp.sum(dgate, axis=0, keepdims=True)
            db_ref[1, :, cs] += jnp.sum(dval, axis=0, keepdims=True)

    outs = pl.pallas_call(
        body, name=name, grid=(half, s // tr),
        in_specs=_conv_in_specs(tr, bw, half, layer, lambda j, i: i, lambda j, i: j)
        + [pl.BlockSpec((tr, bw), lambda j, i: (i, j))]
        + ([pl.BlockSpec(memory_space=pl.ANY)] if after is not None else []),
        out_specs=[pl.BlockSpec((2, None, tr, bw), lambda j, i: (0, j, i, 0)),
                   pl.BlockSpec((2, None, CONV_W, bw), lambda j, i: (0, j, 0, 0)),
                   pl.BlockSpec((2, None, 1, bw), lambda j, i: (0, j, 0, 0))],
        out_shape=[jax.ShapeDtypeStruct((2, half, s, bw), BF16),
                   jax.ShapeDtypeStruct((2, half, CONV_W, bw), F32),
                   jax.ShapeDtypeStruct((2, half, 1, bw), F32)],
        compiler_params=_params(("parallel", "arbitrary")),
    )(a, a, a, a, cw, cw, cb, cb, dact, *([after] if after is not None else []))
    dc, dw, db = outs
    return dc.reshape(nb, s, bw), dw.reshape(nb, CONV_W, bw), db.reshape(nb, 1, bw)


def _conv_bwd_da(dc, cw, layer, *, name, tr=1024):
    nb, s, bw = dc.shape
    tr = min(tr, s)
    ni = s // tr
    per = tr // HALO
    last_halo = s // HALO - 1

    def body(c_ref, n_ref, w_ref, o_ref):
        last = pl.program_id(0) == ni - 1
        for cs in _lane_chunks(bw, 256):
            cur = c_ref[:, cs].astype(F32)
            nxt = n_ref[:, cs].astype(F32)
            da = (cur * w_ref[2:3, cs] + _shift_up(cur, nxt, 1, last) * w_ref[1:2, cs]
                  + _shift_up(cur, nxt, 2, last) * w_ref[0:1, cs])
            o_ref[:, cs] = da.astype(o_ref.dtype)

    tile = pl.BlockSpec((None, tr, bw), lambda i, j: (j, i, 0))
    return pl.pallas_call(
        body, name=name, grid=(ni, nb),
        in_specs=[tile,
                  pl.BlockSpec((None, HALO, bw), lambda i, j: (j, jnp.minimum((i + 1) * per, last_halo), 0)),
                  pl.BlockSpec((None, None, CONV_W, bw), lambda i, j: (j, layer, 0, 0))],
        out_specs=tile,
        out_shape=jax.ShapeDtypeStruct((nb, s, bw), BF16),
        compiler_params=_params(("parallel", "parallel")),
    )(dc, dc, cw)


def _rope_tables(positions):
    inv = 1.0 / (ROPE_THETA ** (jnp.arange(0, ROPE_DIM, 2, dtype=F32) / ROPE_DIM))
    ang = positions.astype(F32)[:, None] * inv
    return jnp.cos(ang), jnp.sin(ang)


def _heads_to_major(nope, r1, r2):
    s = r1.shape[0]
    parts = [nope[:, :G_W].reshape(s, G_HEADS, HEAD)]
    for r in (r1, r2):
        parts.append(jnp.broadcast_to(r.reshape(s, -1, ROPE_HALF), (s, G_HEADS, ROPE_HALF)))
    return jnp.concatenate(parts, axis=-1).transpose(1, 0, 2)


def _heads_from_major(t):
    s = t.shape[1]
    t = t.transpose(1, 0, 2)
    return t[:, :, :HEAD].reshape(s, G_W), t[:, :, HEAD:HEAD + ROPE_HALF], t[:, :, HEAD + ROPE_HALF:]


def _local_step(x, mem, positions, target, rep, fetch, emit):
    s, d = x.shape
    n_b = DEPTH - N_A
    tm = min(1024, s)
    cos, sin = _rope_tables(positions)
    cos12 = jnp.tile(cos, (1, G_HEADS))
    sin12 = jnp.tile(sin, (1, G_HEADS))
    r1_col = G_W // (G_HEADS * ROPE_HALF)
    b_sp_t = rep['b_sp'].transpose(0, 2, 1)

    saved = []
    kv = None
    shared = None
    for l in range(DEPTH):
        wm = fetch(('in', l), x)
        if l == 0:
            shared = {'g_v': wm['g_v'], 'conv_w': wm['conv_w']}
            bw = shared['conv_w'].shape[-1]
            conv_b = rep['conv_b'].reshape(-1, 1, bw)
        sv = {'x_in': x, 'wm': wm}
        if l == N_A:
            xn_kv = _rmsnorm(x, rep['g_kv'], name="kvnorm")
            kvx = _mm(xn_kv, wm['w_kv_a'], dims='nn', out_dtype=F32, name="kvproj", tm=tm, tn=KV_PAD)
            ckv = _rmsnorm(kvx, rep['g_kv_lat'], width=KV_RANK, name="ckvnorm")
            k1, k2 = _rope(kvx[:, KV_RANK:KV_RANK + ROPE_HALF], kvx[:, KV_RANK + ROPE_HALF:KV_RANK + ROPE_DIM],
                           cos, sin, name="krope")
            kv = {'x': x, 'xn': xn_kv, 'kvx': kvx, 'ckv': ckv, 'k1': k1, 'k2': k2, 'w_kv_a': wm['w_kv_a']}
        h = _rmsnorm(x, rep['g_mix'][l], name=f"mixnorm{l}")
        if l < N_A:
            z = _mm(h, wm['w_in'], dims='nn', out_dtype=BF16, name=f"in_a{l}", tm=tm, tn=wm['w_in'].shape[1] // 2)
            main = _sgu_fwd(z, shared['g_v'][l], rep['w_sp'][l], b_sp_t[l], name=f"sgu{l}")
            qcol = 2 * G_W // MEM_W
        else:
            j = l - N_A
            z = _mm(h, wm['w_in'], dims='nn', out_dtype=BF16, name=f"in_b{j}", tm=tm, tn=1024)
            qn = _rmsnorm(z, rep['g_q_lat'][j], width=Q_RANK, name=f"qnorm{j}")
            qp = _mm(qn, wm['w_uqp'], dims='nn', out_dtype=BF16, name=f"uq{j}", tm=tm, tn=768)
            rr1, rr2 = _rope(qp, qp, cos12, sin12, col1=r1_col, col2=r1_col + 1, name=f"qrope{j}")
            qh = _heads_to_major(qp, rr1, rr2)
            kn = _mm(kv['ckv'], wm['w_uk'], dims='nn', out_dtype=BF16, name=f"k_up{j}", tm=tm, tn=768)
            kh = _heads_to_major(kn, kv['k1'], kv['k2'])
            vv = _mm(kv['ckv'], wm['w_uv'], dims='nn', out_dtype=BF16, name=f"v_up{j}", tm=tm, tn=768)
            main, lse = _mha_fwd(qh, kh, vv, name=f"mha{j}")
            qcol = Q_RANK // MEM_W
            sv.update(qn=qn, qh=qh, kh=kh, vv=vv, lse=lse)
        wm.update(fetch(('rest', l), z))
        memn = _rmsnorm(mem, rep['g_mem'][l], name=f"memnorm{l}")
        kvm = _mm(memn, wm['w_mem_kv'], dims='nn', out_dtype=BF16, name=f"memkv{l}", tm=tm, tn=1024)
        mix = _memattn_fwd(z, kvm, main, qcol=qcol, name=f"memattn{l}")
        x_mid = _mm(mix, wm['w_out'], dims='nn', res=x, out_dtype=F32, name=f"out{l}", tm=tm, tn=1024)
        wf = fetch(('up', l), x_mid)
        h2 = _rmsnorm(x_mid, rep['g_ffn'][l], name=f"ffnnorm{l}")
        a = _mm(h2, wf['w_up'], dims='nn', b_blocked=True, out_dtype=BF16, out_block=bw,
                name=f"up{l}", tm=tm, tn=bw)
        act = _conv_fwd(a, shared['conv_w'], conv_b, l, name=f"conv{l}")
        wf.update(fetch(('down', l), act))
        x = _mm(act, wf['w_down'], dims='nn', res=x_mid, out_dtype=F32, name=f"down{l}", tm=512, tn=1024)
        sv.update(h=h, memn=memn, kvm=kvm, z=z, qcol=qcol, mix=mix, x_mid=x_mid, h2=h2, a=a, act=act, wf=wf)
        saved.append(sv)

    sq, dx, dg_final = _final_loss(x, target, rep['g_final'], name="loss")

    g = {k: [None] * DEPTH for k in ('g_mix', 'g_ffn', 'g_mem', 'conv_w', 'conv_b')}
    for k in ('g_v', 'w_sp', 'b_sp'):
        g[k] = [None] * N_A
    g['g_q_lat'] = [None] * n_b
    g['g_final'] = dg_final
    dckv_sum, dkr_sum = None, None

    for l in reversed(range(DEPTH)):
        sv = saved[l]
        wm, wf = sv['wm'], sv['wf']
        dact = _mm(dx, wf['w_down'], dims='nt', out_dtype=BF16, name=f"d_act{l}", tm=tm, tn=bw)
        dw_down = _mm(sv['act'], dx, dims='tn', out_dtype=BF16, name=f"dw_down{l}", tm=bw, tn=512)
        tok = emit(('down', l), {'w_ffn_down': dw_down})
        dc, dcw, dcb = _conv_bwd_dc(sv['a'], dact, shared['conv_w'], conv_b, l, after=tok, name=f"d_conv{l}")
        g['conv_w'][l], g['conv_b'][l] = dcw, dcb
        da = _conv_bwd_da(dc, shared['conv_w'], l, name=f"d_convin{l}")
        dw_up = _mm(sv['h2'], da, dims='tn', b_blocked=True, out_dtype=BF16, out_block=bw,
                    name=f"dw_up{l}", tm=512, tn=bw, n_outer=True)
        tok = emit(('up', l), {'w_ffn_up': dw_up})
        dh2 = _mm_blocked_nt(da, wf['w_up'], out_dtype=BF16, name=f"d_h2{l}", tm=512, tn=1024, blocks_per_step=4,
                             after=tok)
        dx_mid, g['g_ffn'][l] = _rmsnorm_bwd(sv['x_mid'], rep['g_ffn'][l], dh2, dres=dx, name=f"d_ffnnorm{l}")
        dmix = _mm(dx_mid, wm['w_out'], dims='nt', out_dtype=BF16, name=f"d_mix{l}", tm=tm, tn=1024)
        dw_out = _mm(sv['mix'], dx_mid, dims='tn', out_dtype=BF16, name=f"dw_out{l}", tm=1024, tn=512)
        dqm, dkvm = _memattn_bwd(sv['z'], sv['kvm'], dmix, qcol=sv['qcol'], name=f"d_memattn{l}")
        dw_memkv = _mm(sv['memn'], dkvm, dims='tn', out_dtype=BF16, name=f"dw_memkv{l}", tm=1024, tn=1024)
        tok = emit(('rest', l), {'w_out': dw_out, 'w_mem_kv': dw_memkv})
        gm = {}
        dmemn = _mm(dkvm, wm['w_mem_kv'], dims='nt', out_dtype=F32, name=f"d_memn{l}", tm=tm, tn=1024, after=tok)
        _, g['g_mem'][l] = _rmsnorm_bwd(mem, rep['g_mem'][l], dmemn, out_dtype=BF16, name=f"d_memnorm{l}")
        if l < N_A:
            dz, dwsp, dbsp_t, dgv = _sgu_bwd(sv['z'], dmix, dqm, shared['g_v'][l], rep['w_sp'][l], b_sp_t[l],
                                             name=f"d_sgu{l}")
            g['w_sp'][l], g['b_sp'][l], g['g_v'][l] = dwsp, dbsp_t.T, dgv
            dh = _mm(dz, wm['w_in'], dims='nt', out_dtype=BF16, name=f"d_h_a{l}", tm=tm, tn=1024)
            gm['w_in_a'] = _mm(sv['h'], dz, dims='tn', out_dtype=BF16, name=f"dw_in_a{l}", tm=1024, tn=512)
        else:
            j = l - N_A
            dqh, dkh, dvv = _mha_bwd(sv['qh'], sv['kh'], sv['vv'], sv['mix'], dmix, sv['lse'], name=f"d_mha{j}")
            dq_nope, dr1, dr2 = _heads_from_major(dqh)
            dkn, dk1, dk2 = _heads_from_major(dkh)
            dkr = jnp.concatenate([dk1.astype(F32).sum(axis=1), dk2.astype(F32).sum(axis=1)], axis=-1)
            gm['w_uk'] = _mm(kv['ckv'], dkn, dims='tn', out_dtype=BF16, name=f"dw_uk{j}", tm=512, tn=768)
            gm['w_uv'] = _mm(kv['ckv'], dvv, dims='tn', out_dtype=BF16, name=f"dw_uv{j}", tm=512, tn=768)
            dckv = _mm(dkn, wm['w_uk'], dims='nt', out_dtype=F32, res=dckv_sum, name=f"d_ckv_k{j}", tm=tm, tn=512)
            dckv_sum = _mm(dvv, wm['w_uv'], dims='nt', out_dtype=F32, res=dckv, name=f"d_ckv_v{j}", tm=tm, tn=512)
            dkr_sum = dkr if dkr_sum is None else dkr_sum + dkr
            dq1, dq2 = _rope(dr1.reshape(s, -1), dr2.reshape(s, -1), cos12, sin12, inverse=True, name=f"d_qrope{j}")
            dqp = jnp.concatenate([dq_nope, dq1, dq2], axis=-1)
            dqn = _mm(dqp, wm['w_uqp'], dims='nt', out_dtype=BF16, name=f"d_qn{j}", tm=tm, tn=512)
            gm['w_uqp'] = _mm(sv['qn'], dqp, dims='tn', out_dtype=BF16, name=f"dw_uq{j}", tm=512, tn=768)
            dqlat, g['g_q_lat'][j] = _rmsnorm_bwd(sv['z'], rep['g_q_lat'][j], dqn, width=Q_RANK, out_dtype=BF16,
                                                 name=f"d_qnorm{j}")
            dz = jnp.concatenate([dqlat, dqm], axis=-1)
            dh = _mm(dz, wm['w_in'], dims='nt', out_dtype=BF16, name=f"d_h_b{j}", tm=tm, tn=1024)
            gm['w_in_b'] = _mm(sv['h'], dz, dims='tn', out_dtype=BF16, name=f"dw_in_b{j}", tm=1024, tn=512)
        tok = emit(('mix', l), gm)
        dx, g['g_mix'][l] = _rmsnorm_bwd(sv['x_in'], rep['g_mix'][l], dh, dres=dx_mid, after=tok, name=f"d_mixnorm{l}")
        if l == N_A:
            dkvx_c, g['g_kv_lat'] = _rmsnorm_bwd(kv['kvx'], rep['g_kv_lat'], dckv_sum, width=KV_RANK, out_dtype=BF16,
                                                 name="d_ckvnorm")
            dk1, dk2 = _rope(dkr_sum[:, :ROPE_HALF], dkr_sum[:, ROPE_HALF:], cos, sin, inverse=True, name="d_krope")
            dkvx = jnp.concatenate([dkvx_c, dk1, dk2, jnp.zeros((s, KV_PAD - KV_RANK - ROPE_DIM), BF16)], axis=-1)
            dxn = _mm(dkvx, kv['w_kv_a'], dims='nt', out_dtype=BF16, name="d_kvnorm_in", tm=tm, tn=1024)
            dw_kv = _mm(kv['xn'], dkvx, dims='tn', out_dtype=BF16, name="dw_kv", tm=1024, tn=KV_PAD)
            tok = emit(('kv', 0), {'w_kv_a': dw_kv})
            dx, g['g_kv'] = _rmsnorm_bwd(kv['x'], rep['g_kv'], dxn, dres=dx, after=tok, name="d_kvnorm")
    return jnp.sum(sq), dx, g


MESH_IDS = pl.DeviceIdType.MESH
PEER_MASKS = tuple((k >> 2 & 1, k >> 1 & 1, k & 1) for k in range(1, N_DEV))
CHIP_MASKS = ((1, 0), (0, 1), (1, 1))
N_PEER = N_DEV - 1
SEMS_PER_BUFFER = 2 * N_PEER + 1
DATAFLOW = pltpu.SideEffectType.DATAFLOW_SIDE_EFFECTING
HBM_SPEC = pl.BlockSpec(memory_space=pltpu.HBM)
SEM_SPEC = pl.BlockSpec(memory_space=pltpu.SEMAPHORE)


def _my_position():
    return lax.axis_index("x"), lax.axis_index("y"), lax.axis_index("c")


def _flip(pos, mask):
    return tuple(1 - p if f else p for p, f in zip(pos, mask))


def _linear_id(pos):
    return 4 * pos[0] + 2 * pos[1] + pos[2]


def _hbm(x):
    return pltpu.with_memory_space_constraint(x, pltpu.HBM)


def _buffer_copies(src_ref, lead, land_ref, sems, scatter, near=False):
    me = _my_position()
    my_id = _linear_id(me)
    src = src_ref.at[lead] if lead else src_ref
    own = pltpu.make_async_copy(src.at[my_id] if scatter else src, land_ref.at[my_id], sems.at[2 * N_PEER])
    pairs = []
    for k, mask in enumerate(PEER_MASKS):
        if near and mask[2] == 1 and mask != (0, 0, 1):
            continue
        peer = _flip(me, mask)
        peer_id = _linear_id(peer)
        block = src.at[peer_id] if scatter else src
        send = pltpu.make_async_remote_copy(src_ref=block, dst_ref=land_ref.at[my_id], send_sem=sems.at[k],
                                            recv_sem=sems.at[N_PEER + k], device_id=peer, device_id_type=MESH_IDS)
        arrival = pltpu.make_async_remote_copy(src_ref=block, dst_ref=land_ref.at[peer_id], send_sem=sems.at[k],
                                               recv_sem=sems.at[N_PEER + k], device_id=peer, device_id_type=MESH_IDS)
        pairs.append((send, arrival))
    return own, pairs


def _exchange_start(srcs, buffers, *, name, scatter):
    ns, nb = len(srcs), len(buffers)
    lands = [_hbm(lax.empty((N_DEV,) + tuple(shape), dtype)) for _, _, shape, dtype, _ in buffers]

    def body(*refs):
        src_refs, land_refs = refs[:ns], refs[ns:ns + nb]
        sem_refs = refs[ns + nb:ns + 2 * nb]
        token = refs[-1]
        for b, (si, lead, _, _, near) in enumerate(buffers):
            own, pairs = _buffer_copies(src_refs[si], lead, land_refs[b], sem_refs[b], scatter, near)
            own.start()
            for send, _ in pairs:
                send.start()
        token[...] = jnp.zeros_like(token)

    out_shape = ([pltpu.SemaphoreType.DMA((SEMS_PER_BUFFER,))] * nb
                 + [pltpu.HBM(a.shape, a.dtype) for a in srcs]
                 + [pltpu.HBM(a.shape, a.dtype) for a in lands]
                 + [jax.ShapeDtypeStruct((8, 128), F32)])
    aliases = {i: nb + i for i in range(ns + nb)}
    outs = pl.pallas_call(
        body, name=name, in_specs=[HBM_SPEC] * (ns + nb),
        out_specs=[SEM_SPEC] * nb + [HBM_SPEC] * (ns + nb) + [pl.BlockSpec(memory_space=pltpu.VMEM)],
        out_shape=out_shape, input_output_aliases=aliases,
        compiler_params=pltpu.CompilerParams(has_side_effects=DATAFLOW),
    )(*[_hbm(a) for a in srcs], *lands)
    sems = list(outs[:nb])
    src_thru = list(outs[nb:nb + ns])
    land_thru = list(outs[nb + ns:nb + ns + nb])
    return sems, land_thru, src_thru, outs[-1]


def _exchange_wait(srcs_thru, buffers, sems, lands, after, *, name, scatter):
    ns, nb = len(srcs_thru), len(buffers)
    has_after = after is not None

    def body(*refs):
        src_refs, land_refs = refs[:ns], refs[ns:ns + nb]
        sem_refs = refs[ns + nb:ns + 2 * nb]
        for b, (si, lead, _, _, near) in enumerate(buffers):
            own, pairs = _buffer_copies(src_refs[si], lead, land_refs[b], sem_refs[b], scatter, near)
            for send, arrival in pairs:
                send.wait_send()
                arrival.wait_recv()
            own.wait()

    operands = list(srcs_thru) + list(lands) + list(sems) + ([after] if has_after else [])
    in_specs = ([HBM_SPEC] * (ns + nb) + [SEM_SPEC] * nb + ([pl.BlockSpec(memory_space=pl.ANY)] if has_after else []))
    outs = pl.pallas_call(
        body, name=name, in_specs=in_specs, out_specs=[HBM_SPEC] * nb,
        out_shape=[pltpu.HBM(a.shape, a.dtype) for a in lands],
        input_output_aliases={ns + b: b for b in range(nb)},
        compiler_params=pltpu.CompilerParams(has_side_effects=DATAFLOW),
    )(*operands)
    return list(outs)


def _exchange(arrays, *, name, scatter, near=None, after=None):
    n = len(arrays)
    near = [False] * n if near is None else near
    extra = [] if after is None else [after]
    out_shapes = [jax.ShapeDtypeStruct(a.shape if scatter else (N_DEV,) + a.shape, a.dtype) for a in arrays]

    def body(*refs):
        srcs, outs, sems = refs[:n], refs[n + len(extra):2 * n + len(extra)], refs[2 * n + len(extra):]
        started = []
        for a in range(n):
            own, pairs = _buffer_copies(srcs[a], (), outs[a], sems[a], scatter, near[a])
            own.start()
            for send, _ in pairs:
                send.start()
            started.append((own, pairs))
        for own, pairs in started:
            for send, arrival in pairs:
                arrival.wait_recv()
                send.wait_send()
            own.wait()

    any_spec = pl.BlockSpec(memory_space=pl.ANY)
    outs = pl.pallas_call(
        body, name=name, in_specs=[any_spec] * (n + len(extra)), out_specs=[any_spec] * n, out_shape=out_shapes,
        scratch_shapes=[pltpu.SemaphoreType.DMA((SEMS_PER_BUFFER,))] * n,
    )(*arrays, *extra)
    return list(outs)


def _forward_to_sibling(lands, *, name):
    n = len(lands)

    def body(*refs):
        ins, outs, sems = refs[:n], refs[n:2 * n], refs[2 * n:]
        me = _my_position()
        sibling = _flip(me, (0, 0, 1))
        pairs = []
        for b in range(n):
            for k, (fx, fy) in enumerate(CHIP_MASKS):
                mine = _linear_id(_flip(me, (fx, fy, 0)))
                theirs = _linear_id(_flip(me, (fx, fy, 1)))
                send = pltpu.make_async_remote_copy(
                    src_ref=ins[b].at[mine], dst_ref=outs[b].at[mine], send_sem=sems[b].at[k],
                    recv_sem=sems[b].at[len(CHIP_MASKS) + k], device_id=sibling, device_id_type=MESH_IDS)
                arrival = pltpu.make_async_remote_copy(
                    src_ref=ins[b].at[mine], dst_ref=outs[b].at[theirs], send_sem=sems[b].at[k],
                    recv_sem=sems[b].at[len(CHIP_MASKS) + k], device_id=sibling, device_id_type=MESH_IDS)
                send.start()
                pairs.append((send, arrival))
        for send, arrival in pairs:
            arrival.wait_recv()
            send.wait_send()

    any_spec = pl.BlockSpec(memory_space=pl.ANY)
    outs = pl.pallas_call(
        body, name=name, in_specs=[any_spec] * n, out_specs=[any_spec] * n,
        out_shape=[jax.ShapeDtypeStruct(a.shape, a.dtype) for a in lands],
        input_output_aliases={b: b for b in range(n)},
        scratch_shapes=[pltpu.SemaphoreType.DMA((2 * len(CHIP_MASKS),))] * n,
    )(*lands)
    return list(outs)


def _sum_slots(parts_ref):
    total = parts_ref[0].astype(F32)
    for q in range(1, parts_ref.shape[0]):
        total = total + parts_ref[q].astype(F32)
    return total


def _row_tile(rows, cols, n_arrays):
    budget = (24 * 1024 * 1024) // (4 * n_arrays * max(cols, 128))
    t = rows
    while t > budget and t % 2 == 0 and (t // 2) % 16 == 0:
        t //= 2
    return t


def _sum_adam(parts, w, m, v, layer, outs, *, name):
    q, r, c = parts.shape
    nl = w.shape[0]
    tr = _row_tile(r, c, q + 7)
    c1 = 1.0 - ADAM_B1 ** ADAM_STEP
    c2 = 1.0 - ADAM_B2 ** ADAM_STEP
    if outs is None:
        outs = [lax.empty((nl, r, c), F32) for _ in range(4)]

    def body(p_ref, w_ref, m_ref, v_ref, g_in, d_in, mo_in, vo_in, g_ref, d_ref, mo_ref, vo_ref):
        grad = _sum_slots(p_ref)
        m_new = ADAM_B1 * m_ref[...] + (1.0 - ADAM_B1) * grad
        v_new = ADAM_B2 * v_ref[...] + (1.0 - ADAM_B2) * (grad * grad)
        m_hat = m_new / c1
        v_hat = v_new / c2
        g_ref[...] = grad
        d_ref[...] = -ADAM_LR * (m_hat / (jnp.sqrt(v_hat) + ADAM_EPS) + ADAM_WD * w_ref[...])
        mo_ref[...] = m_new
        vo_ref[...] = v_new

    tile = pl.BlockSpec((None, tr, c), lambda i: (layer, i, 0))
    any_spec = pl.BlockSpec(memory_space=pl.ANY)
    return pl.pallas_call(
        body, name=name, grid=(r // tr,),
        in_specs=[pl.BlockSpec((q, tr, c), lambda i: (0, i, 0)), tile, tile, tile] + [any_spec] * 4,
        out_specs=[tile] * 4, out_shape=[jax.ShapeDtypeStruct((nl, r, c), F32)] * 4,
        input_output_aliases={4: 0, 5: 1, 6: 2, 7: 3},
        compiler_params=_params(("parallel",)),
    )(parts, w, m, v, *outs)


def _sum_parts(parts, *, name):
    q, r, c = parts.shape

    def body(p_ref, o_ref):
        o_ref[...] = _sum_slots(p_ref)

    return pl.pallas_call(
        body, name=name, in_specs=[pl.BlockSpec((q, r, c), lambda: (0, 0, 0))],
        out_specs=pl.BlockSpec((r, c), lambda: (0, 0)), out_shape=jax.ShapeDtypeStruct((r, c), F32),
        compiler_params=_params(),
    )(parts)


INPUT_NAMES = (['x', 'mem', 'positions'] + WEIGHTS + ['loss_target'] + ['m_' + n for n in WEIGHTS]
               + ['v_' + n for n in WEIGHTS])
SMALL_ALIGN = N_DEV * 8 * 128
TWO_LEVEL_LAYERS = N_A
GROUP_ORDER = ('in', 'rest', 'up', 'down')
GROUP_WEIGHTS = {'in': (['w_in_a'], ['w_in_b', 'w_uq', 'w_uk', 'w_uv']), 'rest': (['w_mem_kv', 'w_out'],) * 2,
                 'up': (['w_ffn_up'],) * 2, 'down': (['w_ffn_down'],) * 2}
LAYERED = {'w_in_a': 0, 'w_in_b': N_A, 'w_uq': N_A, 'w_uk': N_A, 'w_uv': N_A, 'w_mem_kv': 0, 'w_out': 0,
           'w_ffn_up': 0, 'w_ffn_down': 0}


def _permute_uq(w_uq):
    r = w_uq.shape[0]
    q = w_uq.reshape(r, G_HEADS, HEAD + ROPE_DIM)
    return jnp.concatenate([q[..., :HEAD].reshape(r, -1), q[..., HEAD:HEAD + ROPE_HALF].reshape(r, -1),
                            q[..., HEAD + ROPE_HALF:].reshape(r, -1)], axis=-1)


def _unpermute_uq(w_uqp):
    r = w_uqp.shape[0]
    nope = w_uqp[..., :G_W].reshape(r, G_HEADS, HEAD)
    r1 = w_uqp[..., G_W:G_W + G_HEADS * ROPE_HALF].reshape(r, G_HEADS, ROPE_HALF)
    r2 = w_uqp[..., G_W + G_HEADS * ROPE_HALF:].reshape(r, G_HEADS, ROPE_HALF)
    return jnp.concatenate([nope, r1, r2], axis=-1).reshape(r, -1)


def _cols_from_stack(st):
    _, r, n = st.shape
    return st.transpose(1, 0, 2).reshape(r, N_DEV * n)


def _cols_to_stack(wh):
    r, c = wh.shape
    return wh.reshape(r, N_DEV, c // N_DEV).transpose(1, 0, 2)


def _group_weights(group):
    kind, l = group
    return GROUP_WEIGHTS[kind][0 if l < N_A else 1]


def _step(args):
    p = dict(zip(INPUT_NAMES, args))
    x, mem, positions, target = p['x'][0], p['mem'][0], p['positions'][0], p['loss_target'][0]
    d = x.shape[-1]
    my_id = _linear_id(_my_position())

    w_kv_pad = jnp.pad(p['w_kv_a'], ((0, 0), (0, KV_PAD - p['w_kv_a'].shape[1])))
    shard = {k: p[k].astype(BF16) for k in LAYERED}
    shard['w_uk'] = shard['w_uk'].reshape(shard['w_uk'].shape[0], shard['w_uk'].shape[1], -1)
    shard['w_uv'] = shard['w_uv'].reshape(shard['w_uv'].shape[0], shard['w_uv'].shape[1], -1)
    shard.update(conv_w=p['conv_w'], g_v=p['g_v'], w_kv_a=w_kv_pad.astype(BF16))
    src_names = list(shard)
    gather_groups = []
    for l in range(DEPTH):
        gather_groups += [(kind, l) for kind in GROUP_ORDER]
    buffers, owner = [], []
    for group in gather_groups:
        kind, l = group
        for k in _group_weights(group):
            buffers.append((src_names.index(k), (l - LAYERED[k],), shard[k].shape[1:], shard[k].dtype, l < TWO_LEVEL_LAYERS))
            owner.append((group, k))
        if group == ('in', 0):
            for k in ('g_v', 'conv_w'):
                buffers.append((src_names.index(k), (), shard[k].shape, shard[k].dtype, True))
                owner.append((group, k))
        if group == ('in', N_A):
            buffers.append((src_names.index('w_kv_a'), (), shard['w_kv_a'].shape, BF16, False))
            owner.append((group, 'w_kv_a'))
    g_sems, g_lands, g_srcs, _ = _exchange_start([shard[k] for k in src_names], buffers, name="gather_start",
                                                 scatter=False)

    def fetch(group, after):
        idx = [i for i, (grp, _) in enumerate(owner) if grp == group]
        landed = _exchange_wait(g_srcs, [buffers[i] for i in idx], [g_sems[i] for i in idx],
                                [g_lands[i] for i in idx], after, name=f"gather_wait_{group[0]}{group[1]}",
                                scatter=False)
        if group[1] < TWO_LEVEL_LAYERS:
            landed = _forward_to_sibling(landed, name=f"gather_forward_{group[0]}{group[1]}")
        got = {owner[i][1]: t for i, t in zip(idx, landed)}
        out = {}
        for k, t in got.items():
            if k in ('w_in_a', 'w_uq'):
                out[k] = _cols_from_stack(t)
            elif k == 'g_v':
                out[k] = t.transpose(1, 0, 2).reshape(t.shape[1], -1)
            elif k in ('w_ffn_up', 'conv_w'):
                out[k] = t
            else:
                out[k] = t.reshape(-1, t.shape[-1])
        if 'w_uq' in out:
            out['w_uqp'] = _permute_uq(out.pop('w_uq'))
        for old, new in (('w_in_a', 'w_in'), ('w_in_b', 'w_in'), ('w_ffn_up', 'w_up'), ('w_ffn_down', 'w_down')):
            if old in out:
                out[new] = out.pop(old)
        return out

    pending = []

    def emit(group, grads):
        send = {}
        for k, t in grads.items():
            if k == 'w_in_a':
                send[k] = _cols_to_stack(t)
            elif k == 'w_uqp':
                send['w_uq'] = _cols_to_stack(_unpermute_uq(t))
            elif k == 'w_ffn_up':
                send[k] = t
            elif k == 'w_kv_a':
                cols = p['w_kv_a'].shape[1]
                send[k] = t[:, :cols].reshape(N_DEV, -1, cols)
            else:
                send[k] = t.reshape(N_DEV, t.shape[0] // N_DEV, t.shape[1])
        keys = list(send)
        bufs = [(i, (), send[k].shape[1:], send[k].dtype, False) for i, k in enumerate(keys)]
        sems, lands, srcs, token = _exchange_start([send[k] for k in keys], bufs,
                                                   name=f"scatter_start_{group[0]}{group[1]}", scatter=True)
        pending.append((group, keys, bufs, sems, lands, srcs))
        return token

    rep = {k: p[k] for k in REPLICATED}
    sq, grad_x, g = _local_step(x, mem, positions, target, rep, fetch, emit)
    loss = (0.5 / d) * lax.psum(sq, ("x", "y", "c"))

    out, running = {}, {}
    order = grad_x
    for group, keys, bufs, sems, lands, srcs in pending:
        landed = _exchange_wait(srcs, bufs, sems, lands, order, name=f"scatter_wait_{group[0]}{group[1]}", scatter=True)
        for k, parts in zip(keys, landed):
            stacked = k in LAYERED
            nl = p[k].shape[0] if stacked else 1
            layer = group[1] - LAYERED[k] if stacked else 0
            rows = p[k].size // nl // p[k].shape[-1]
            view = (nl, rows, p[k].shape[-1])
            running[k] = _sum_adam(parts.reshape(N_DEV, rows, view[2]), p[k].reshape(view), p['m_' + k].reshape(view),
                                   p['v_' + k].reshape(view), layer, running.get(k), name=f"adam_{k}{layer}")
            order = running[k][1]
    for k, res in running.items():
        out[k] = [t.reshape(p[k].shape) for t in res]

    small = {
        'g_mix': jnp.concatenate(g['g_mix']), 'g_ffn': jnp.concatenate(g['g_ffn']), 'g_final': g['g_final'],
        'w_sp': jnp.stack(g['w_sp']), 'b_sp': jnp.stack(g['b_sp']), 'g_kv': g['g_kv'], 'g_kv_lat': g['g_kv_lat'],
        'g_q_lat': jnp.concatenate(g['g_q_lat']), 'g_mem': jnp.concatenate(g['g_mem']),
        'conv_b': jnp.stack(g['conv_b']),
        'g_v': jnp.concatenate(g['g_v']),
        'conv_w': jnp.stack(g['conv_w']).transpose(0, 2, 1, 3),
    }
    small_names = REPLICATED + SMALL_SHARDED
    flat = jnp.concatenate([small[k].reshape(-1).astype(F32) for k in small_names])
    n_small = flat.shape[0]
    padded = -(-n_small // SMALL_ALIGN) * SMALL_ALIGN
    flat = jnp.pad(flat, (0, padded - n_small)).reshape(N_DEV, -1, 128)
    last_update = out[pending[-1][1][-1]][1]
    (small_parts,) = _exchange([flat], name="scatter_small", scatter=True, after=last_update)
    reduced = _sum_parts(small_parts, name="sum_small")
    (small_all,) = _exchange([reduced], name="gather_small", scatter=False)
    small_all = small_all.reshape(-1)
    grads_small, off = {}, 0
    for k in small_names:
        size = small[k].size
        grads_small[k] = small_all[off:off + size].reshape(small[k].shape)
        off += size
    grads_small['g_v'] = lax.dynamic_slice_in_dim(grads_small['g_v'], my_id * p['g_v'].shape[1], p['g_v'].shape[1], axis=1)
    grads_small['conv_w'] = lax.dynamic_index_in_dim(grads_small['conv_w'], my_id, axis=2, keepdims=False)
    gs = jnp.concatenate([grads_small[k].reshape(-1) for k in small_names])
    n_loc = gs.shape[0]
    pad_loc = -(-n_loc // 1024) * 1024 - n_loc

    def pack(prefix):
        t = jnp.concatenate([p[prefix + k].reshape(-1) for k in small_names])
        return jnp.pad(t, (0, pad_loc)).reshape(1, -1, 128)

    res = _sum_adam(jnp.pad(gs, (0, pad_loc)).reshape(1, -1, 128), pack(''), pack('m_'), pack('v_'), 0, None,
                    name="adam_small")
    off = 0
    for k in small_names:
        size = p[k].size
        out[k] = [t.reshape(-1)[off:off + size].reshape(p[k].shape) for t in res]
        off += size

    outs = [loss, grad_x[None]]
    for i in range(4):
        outs += [out[k][i] for k in WEIGHTS]
    return tuple(outs)


def kernel(x, mem, positions, g_mix, g_ffn, g_final, w_in_a, g_v, w_sp, b_sp, g_kv, w_kv_a, g_kv_lat, w_in_b, g_q_lat, w_uq, w_uk, w_uv, g_mem, w_mem_kv, w_out, w_ffn_up, conv_w, conv_b, w_ffn_down, loss_target, m_g_mix, m_g_ffn, m_g_final, m_w_in_a, m_g_v, m_w_sp, m_b_sp, m_g_kv, m_w_kv_a, m_g_kv_lat, m_w_in_b, m_g_q_lat, m_w_uq, m_w_uk, m_w_uv, m_g_mem, m_w_mem_kv, m_w_out, m_w_ffn_up, m_conv_w, m_conv_b, m_w_ffn_down, v_g_mix, v_g_ffn, v_g_final, v_w_in_a, v_g_v, v_w_sp, v_b_sp, v_g_kv, v_w_kv_a, v_g_kv_lat, v_w_in_b, v_g_q_lat, v_w_uq, v_w_uk, v_w_uv, v_g_mem, v_w_mem_kv, v_w_out, v_w_ffn_up, v_conv_w, v_conv_b, v_w_ffn_down):
    return _step((x, mem, positions, g_mix, g_ffn, g_final, w_in_a, g_v, w_sp, b_sp, g_kv, w_kv_a, g_kv_lat, w_in_b, g_q_lat, w_uq, w_uk, w_uv, g_mem, w_mem_kv, w_out, w_ffn_up, conv_w, conv_b, w_ffn_down, loss_target, m_g_mix, m_g_ffn, m_g_final, m_w_in_a, m_g_v, m_w_sp, m_b_sp, m_g_kv, m_w_kv_a, m_g_kv_lat, m_w_in_b, m_g_q_lat, m_w_uq, m_w_uk, m_w_uv, m_g_mem, m_w_mem_kv, m_w_out, m_w_ffn_up, m_conv_w, m_conv_b, m_w_ffn_down, v_g_mix, v_g_ffn, v_g_final, v_w_in_a, v_g_v, v_w_sp, v_b_sp, v_g_kv, v_w_kv_a, v_g_kv_lat, v_w_in_b, v_g_q_lat, v_w_uq, v_w_uk, v_w_uv, v_g_mem, v_w_mem_kv, v_w_out, v_w_ffn_up, v_conv_w, v_conv_b, v_w_ffn_down))
```

```python
import math

import jax
import jax.numpy as jnp
from jax import lax
from jax.experimental import pallas as pl
from jax.experimental.pallas import tpu as pltpu

F32 = jnp.float32
BF16 = jnp.bfloat16

N_DEV = 8
N_A = 2
DEPTH = 4
G_HEADS = 12
HEAD = 128
CHUNK = 128
MEM_HEADS = 4
MEM_W = MEM_HEADS * HEAD
G_W = G_HEADS * HEAD
ROPE_DIM = 64
ROPE_HALF = ROPE_DIM // 2
KV_RANK = 512
Q_RANK = 512
KV_PAD = 640
ROPE_THETA = 10000.0
EPS = 1e-6
CONV_W = 3

ADAM_LR = 0.001
ADAM_B1 = 0.9
ADAM_B2 = 0.999
ADAM_EPS = 1e-08
ADAM_WD = 0.01
ADAM_STEP = 10

VMEM_LIMIT_V7X = 56 * 1024 * 1024
MASK_VALUE = -1e30

WEIGHTS = ['g_mix', 'g_ffn', 'g_final', 'w_in_a', 'g_v', 'w_sp', 'b_sp', 'g_kv', 'w_kv_a', 'g_kv_lat',
           'w_in_b', 'g_q_lat', 'w_uq', 'w_uk', 'w_uv', 'g_mem', 'w_mem_kv', 'w_out', 'w_ffn_up',
           'conv_w', 'conv_b', 'w_ffn_down']
REPLICATED = ['g_mix', 'g_ffn', 'g_final', 'w_sp', 'b_sp', 'g_kv', 'g_kv_lat', 'g_q_lat', 'g_mem', 'conv_b']
SMALL_SHARDED = ['g_v', 'conv_w']


def _params(sem=None):
    return pltpu.CompilerParams(dimension_semantics=sem, vmem_limit_bytes=VMEM_LIMIT_V7X)


def _dot(a, b, dims):
    contract = {'nn': ((1,), (0,)), 'nt': ((1,), (1,)), 'tn': ((0,), (0,))}[dims]
    return lax.dot_general(a, b, (contract, ((), ())), preferred_element_type=F32)


def _erf(x):
    return lax.erf(x)


def _gelu(x):
    return 0.5 * x * (1.0 + _erf(x * (2.0 ** -0.5)))


def _gelu_grad(x):
    cdf = 0.5 * (1.0 + _erf(x * (2.0 ** -0.5)))
    pdf = jnp.exp(-0.5 * x * x) * (1.0 / math.sqrt(2.0 * math.pi))
    return cdf + x * pdf


def _sigmoid(x):
    return 1.0 / (1.0 + jnp.exp(-x))


def _operand_spec(shape, lead, blocked, tr, tc, ridx, cidx):
    if blocked:
        per = shape[-1] // tc
        assert shape[-1] % tc == 0, (shape, tc)
        return pl.BlockSpec(
            (None,) * (1 + len(lead)) + (tr, tc),
            lambda *g: (cidx(*g) // per,) + lead + (ridx(*g), cidx(*g) % per))
    return pl.BlockSpec((None,) * len(lead) + (tr, tc), lambda *g: lead + (ridx(*g), cidx(*g)))


def _view2d(x, blocked):
    return (x.shape[-2], x.shape[0] * x.shape[-1]) if blocked else (x.shape[-2], x.shape[-1])


def _mm(a, b, *, dims, out_dtype, name, tm, tn, tk=None, res=None, a_lead=(), b_lead=(),
        a_blocked=False, b_blocked=False, out_block=None, n_outer=False, after=None):
    ar, ac = _view2d(a, a_blocked)
    br, bc = _view2d(b, b_blocked)
    m, k = (ac, ar) if dims == 'tn' else (ar, ac)
    n, k2 = (br, bc) if dims == 'nt' else (bc, br)
    assert k == k2, (a.shape, b.shape, dims)
    tm, tn = min(tm, m), min(tn, n)
    tk = k if tk is None else tk
    assert m % tm == 0 and n % tn == 0 and k % tk == 0, (name, m, n, k, tm, tn, tk)
    nk = k // tk
    if n_outer:
        gi, gj = (lambda g0, g1, g2: g1), (lambda g0, g1, g2: g0)
        grid = (n // tn, m // tm, nk)
    else:
        gi, gj = (lambda g0, g1, g2: g0), (lambda g0, g1, g2: g1)
        grid = (m // tm, n // tn, nk)
    gk = lambda g0, g1, g2: g2

    if dims == 'tn':
        a_spec = _operand_spec(a.shape, a_lead, a_blocked, tk, tm, gk, gi)
    else:
        a_spec = _operand_spec(a.shape, a_lead, a_blocked, tm, tk, gi, gk)
    if dims == 'nt':
        b_spec = _operand_spec(b.shape, b_lead, b_blocked, tn, tk, gj, gk)
    else:
        b_spec = _operand_spec(b.shape, b_lead, b_blocked, tk, tn, gk, gj)
    in_specs = [a_spec, b_spec]
    operands = [a, b]
    if res is not None:
        in_specs.append(pl.BlockSpec((tm, tn), lambda *g: (gi(*g), gj(*g))))
        operands.append(res)
    if after is not None:
        in_specs.append(pl.BlockSpec(memory_space=pl.ANY))
        operands.append(after)
    n_in = len(operands)
    if out_block is not None:
        out_shape = jax.ShapeDtypeStruct((n // out_block, m, out_block), out_dtype)
        out_spec = _operand_spec(out_shape.shape, (), True, tm, tn, gi, gj)
    else:
        out_shape = jax.ShapeDtypeStruct((m, n), out_dtype)
        out_spec = pl.BlockSpec((tm, tn), lambda *g: (gi(*g), gj(*g)))

    def body(*refs):
        a_ref, b_ref = refs[0], refs[1]
        r_ref = refs[2] if res is not None else None
        o_ref = refs[n_in]
        acc_ref = refs[-1] if nk > 1 else None
        part = _dot(a_ref[...].astype(BF16), b_ref[...].astype(BF16), dims)

        def finish(total):
            if r_ref is not None:
                total = total + r_ref[...]
            o_ref[...] = total.astype(o_ref.dtype)

        if nk == 1:
            finish(part)
        else:
            kk = pl.program_id(2)

            @pl.when(kk == 0)
            def _():
                acc_ref[...] = part

            @pl.when(kk > 0)
            def _():
                acc_ref[...] += part

            @pl.when(kk == nk - 1)
            def _():
                finish(acc_ref[...])

    scratch = [pltpu.VMEM((tm, tn), F32)] if nk > 1 else []
    return pl.pallas_call(
        body, name=name, grid=grid, in_specs=in_specs, out_specs=out_spec,
        out_shape=out_shape, scratch_shapes=scratch,
        compiler_params=_params(("parallel", "parallel", "arbitrary")),
    )(*operands)


def _mm_blocked_nt(a, b, *, out_dtype, name, tm, tn, blocks_per_step, after=None):
    nb, m, bw = a.shape
    n = b.shape[1]
    tm, tn = min(tm, m), min(tn, n)
    assert nb % blocks_per_step == 0 and m % tm == 0 and n % tn == 0
    nk = nb // blocks_per_step

    def body(*refs):
        a_ref, b_ref, o_ref, acc_ref = refs[0], refs[1], refs[-2], refs[-1]
        kk = pl.program_id(2)
        part = _dot(a_ref[0], b_ref[0], 'nt')
        for t in range(1, blocks_per_step):
            part = part + _dot(a_ref[t], b_ref[t], 'nt')

        @pl.when(kk == 0)
        def _():
            acc_ref[...] = part

        @pl.when(kk > 0)
        def _():
            acc_ref[...] += part

        @pl.when(kk == nk - 1)
        def _():
            o_ref[...] = acc_ref[...].astype(o_ref.dtype)

    in_specs = [pl.BlockSpec((blocks_per_step, tm, bw), lambda i, j, k: (k, i, 0)),
                pl.BlockSpec((blocks_per_step, tn, bw), lambda i, j, k: (k, j, 0))]
    operands = [a, b]
    if after is not None:
        in_specs.append(pl.BlockSpec(memory_space=pl.ANY))
        operands.append(after)
    return pl.pallas_call(
        body, name=name, grid=(m // tm, n // tn, nk), in_specs=in_specs,
        out_specs=pl.BlockSpec((tm, tn), lambda i, j, k: (i, j)), out_shape=jax.ShapeDtypeStruct((m, n), out_dtype),
        scratch_shapes=[pltpu.VMEM((tm, tn), F32)],
        compiler_params=_params(("parallel", "parallel", "arbitrary")),
    )(*operands)


def _rmsnorm(x, g, *, name, width=None, out_dtype=BF16, tm=512):
    s = x.shape[0]
    w = x.shape[1] if width is None else width
    tm = min(tm, s)

    def body(x_ref, g_ref, o_ref):
        xv = x_ref[...].astype(F32)
        rstd = lax.rsqrt(jnp.mean(xv * xv, axis=-1, keepdims=True) + EPS)
        o_ref[...] = (xv * rstd * g_ref[...]).astype(o_ref.dtype)

    return pl.pallas_call(
        body, name=name, grid=(s // tm,),
        in_specs=[pl.BlockSpec((tm, w), lambda i: (i, 0)), pl.BlockSpec((1, w), lambda i: (0, 0))],
        out_specs=pl.BlockSpec((tm, w), lambda i: (i, 0)),
        out_shape=jax.ShapeDtypeStruct((s, w), out_dtype),
        compiler_params=_params(("parallel",)),
    )(x, g.reshape(1, w))


def _rmsnorm_bwd(x, g, dy, *, name, width=None, dres=None, after=None, out_dtype=F32, tm=512, bf16_copy=False):
    s = x.shape[0]
    w = x.shape[1] if width is None else width
    tm = min(tm, s)

    def body(*refs):
        x_ref, g_ref, dy_ref = refs[0], refs[1], refs[2]
        r_ref = refs[3] if dres is not None else None
        dx_ref, dg_ref = (refs[-3], refs[-1]) if bf16_copy else (refs[-2], refs[-1])
        xv = x_ref[...].astype(F32)
        rstd = lax.rsqrt(jnp.mean(xv * xv, axis=-1, keepdims=True) + EPS)
        xhat = xv * rstd
        dyv = dy_ref[...].astype(F32)
        gdy = dyv * g_ref[...]
        dx = rstd * (gdy - xhat * jnp.mean(gdy * xhat, axis=-1, keepdims=True))
        if r_ref is not None:
            dx = dx + r_ref[...]
        dx_ref[...] = dx.astype(dx_ref.dtype)
        if bf16_copy:
            refs[-2][...] = dx.astype(BF16)
        part = jnp.sum(dyv * xhat, axis=0, keepdims=True)

        @pl.when(pl.program_id(0) == 0)
        def _():
            dg_ref[...] = part

        @pl.when(pl.program_id(0) > 0)
        def _():
            dg_ref[...] += part

    row = pl.BlockSpec((tm, w), lambda i: (i, 0))
    vec = pl.BlockSpec((1, w), lambda i: (0, 0))
    in_specs = [row, vec, row] + ([row] if dres is not None else [])
    operands = [x, g.reshape(1, w), dy] + ([dres] if dres is not None else [])
    if after is not None:
        in_specs.append(pl.BlockSpec(memory_space=pl.ANY))
        operands.append(after)
    copy_spec = [row] if bf16_copy else []
    copy_shape = [jax.ShapeDtypeStruct((s, w), BF16)] if bf16_copy else []
    return pl.pallas_call(
        body, name=name, grid=(s // tm,), in_specs=in_specs, out_specs=[row] + copy_spec + [vec],
        out_shape=[jax.ShapeDtypeStruct((s, w), out_dtype)] + copy_shape + [jax.ShapeDtypeStruct((1, w), F32)],
        compiler_params=_params(("arbitrary",)),
    )(*operands)


def _final_loss(x, target, g, *, name, tm=256):
    s, d = x.shape
    tm = min(tm, s)

    def body(x_ref, t_ref, g_ref, sq_ref, dx_ref, dxb_ref, dg_ref):
        xv = x_ref[...]
        rstd = lax.rsqrt(jnp.mean(xv * xv, axis=-1, keepdims=True) + EPS)
        xhat = xv * rstd
        err = xhat * g_ref[...] - t_ref[...]
        dyv = err * (1.0 / d)
        gdy = dyv * g_ref[...]
        dx = rstd * (gdy - xhat * jnp.mean(gdy * xhat, axis=-1, keepdims=True))
        dx_ref[...] = dx
        dxb_ref[...] = dx.astype(BF16)
        sq = jnp.sum(err * err, axis=0, keepdims=True)
        dg = jnp.sum(dyv * xhat, axis=0, keepdims=True)

        @pl.when(pl.program_id(0) == 0)
        def _():
            sq_ref[...] = sq
            dg_ref[...] = dg

        @pl.when(pl.program_id(0) > 0)
        def _():
            sq_ref[...] += sq
            dg_ref[...] += dg

    row = pl.BlockSpec((tm, d), lambda i: (i, 0))
    vec = pl.BlockSpec((1, d), lambda i: (0, 0))
    return pl.pallas_call(
        body, name=name, grid=(s // tm,), in_specs=[row, row, vec], out_specs=[vec, row, row, vec],
        out_shape=[jax.ShapeDtypeStruct((1, d), F32), jax.ShapeDtypeStruct((s, d), F32),
                   jax.ShapeDtypeStruct((s, d), BF16), jax.ShapeDtypeStruct((1, d), F32)],
        compiler_params=_params(("arbitrary",)),
    )(x, target, g.reshape(1, d))


def _tril_mask():
    t = lax.broadcasted_iota(jnp.int32, (CHUNK, CHUNK), 0)
    s = lax.broadcasted_iota(jnp.int32, (CHUNK, CHUNK), 1)
    return t >= s


def _sgu_fwd(z, g_v, w_sp, b_sp_t, *, name):
    s = z.shape[0]

    def body(zu_ref, zv_ref, g_ref, w_ref, b_ref, o_ref):
        u = _gelu(zu_ref[...].astype(F32))
        gv = _gelu(zv_ref[...].astype(F32))
        rstd = lax.rsqrt(jnp.mean(gv * gv, axis=-1, keepdims=True) + EPS)
        v = (gv * rstd * g_ref[...]).astype(BF16)
        mask = _tril_mask()
        for grp in range(G_HEADS):
            cols = slice(grp * HEAD, (grp + 1) * HEAD)
            wm = jnp.where(mask, w_ref[grp], 0.0).astype(BF16)
            sv = _dot(wm, v[:, cols], 'nn') + b_ref[:, grp:grp + 1]
            o_ref[:, cols] = (u[:, cols] * sv).astype(o_ref.dtype)

    return pl.pallas_call(
        body, name=name, grid=(s // CHUNK,),
        in_specs=[pl.BlockSpec((CHUNK, G_W), lambda i: (i, 0)),
                  pl.BlockSpec((CHUNK, G_W), lambda i: (i, 1)),
                  pl.BlockSpec((1, G_W), lambda i: (0, 0)),
                  pl.BlockSpec((G_HEADS, CHUNK, CHUNK), lambda i: (0, 0, 0)),
                  pl.BlockSpec((CHUNK, G_HEADS), lambda i: (0, 0))],
        out_specs=pl.BlockSpec((CHUNK, G_W), lambda i: (i, 0)),
        out_shape=jax.ShapeDtypeStruct((s, G_W), BF16),
        compiler_params=_params(("parallel",)),
    )(z, z, g_v.reshape(1, G_W), w_sp, b_sp_t)


def _sgu_bwd(z, dmix, dqm, g_v, w_sp, b_sp_t, *, name):
    s = z.shape[0]
    zw = z.shape[1]

    def body(zu_ref, zv_ref, dm_ref, dq_ref, g_ref, w_ref, b_ref, dz_ref, dw_ref, db_ref, dg_ref):
        first = pl.program_id(0) == 0

        @pl.when(first)
        def _():
            dw_ref[...] = jnp.zeros_like(dw_ref)
            db_ref[...] = jnp.zeros_like(db_ref)
            dg_ref[...] = jnp.zeros_like(dg_ref)

        zu = zu_ref[...].astype(F32)
        zv = zv_ref[...].astype(F32)
        dmain = dm_ref[...].astype(F32)
        u = _gelu(zu)
        gv = _gelu(zv)
        rstd = lax.rsqrt(jnp.mean(gv * gv, axis=-1, keepdims=True) + EPS)
        vhat = gv * rstd
        gvec = g_ref[...]
        v = (vhat * gvec).astype(BF16)
        dsv = dmain * u
        dsv_b = dsv.astype(BF16)
        mask = _tril_mask()
        dv_parts = []
        for grp in range(G_HEADS):
            cols = slice(grp * HEAD, (grp + 1) * HEAD)
            wm = jnp.where(mask, w_ref[grp], 0.0).astype(BF16)
            sv = _dot(wm, v[:, cols], 'nn') + b_ref[:, grp:grp + 1]
            dz_ref[:, cols] = (dmain[:, cols] * sv * _gelu_grad(zu[:, cols])).astype(dz_ref.dtype)
            dwg = _dot(dsv_b[:, cols], v[:, cols], 'nt')
            dw_ref[grp] += jnp.where(mask, dwg, 0.0)
            db_ref[:, grp:grp + 1] += jnp.sum(dsv[:, cols], axis=-1, keepdims=True)
            dv_parts.append(_dot(wm, dsv_b[:, cols], 'tn'))
        dv = jnp.concatenate(dv_parts, axis=-1)
        dg_ref[...] += jnp.sum(dv * vhat, axis=0, keepdims=True)
        gdv = dv * gvec
        dgv = rstd * (gdv - vhat * jnp.mean(gdv * vhat, axis=-1, keepdims=True))
        dz_ref[:, G_W:2 * G_W] = (dgv * _gelu_grad(zv)).astype(dz_ref.dtype)
        dz_ref[:, 2 * G_W:] = dq_ref[...].astype(dz_ref.dtype)

    return pl.pallas_call(
        body, name=name, grid=(s // CHUNK,),
        in_specs=[pl.BlockSpec((CHUNK, G_W), lambda i: (i, 0)),
                  pl.BlockSpec((CHUNK, G_W), lambda i: (i, 1)),
                  pl.BlockSpec((CHUNK, G_W), lambda i: (i, 0)),
                  pl.BlockSpec((CHUNK, MEM_W), lambda i: (i, 0)),
                  pl.BlockSpec((1, G_W), lambda i: (0, 0)),
                  pl.BlockSpec((G_HEADS, CHUNK, CHUNK), lambda i: (0, 0, 0)),
                  pl.BlockSpec((CHUNK, G_HEADS), lambda i: (0, 0))],
        out_specs=[pl.BlockSpec((CHUNK, zw), lambda i: (i, 0)),
                   pl.BlockSpec((G_HEADS, CHUNK, CHUNK), lambda i: (0, 0, 0)),
                   pl.BlockSpec((CHUNK, G_HEADS), lambda i: (0, 0)),
                   pl.BlockSpec((1, G_W), lambda i: (0, 0))],
        out_shape=[jax.ShapeDtypeStruct((s, zw), BF16),
                   jax.ShapeDtypeStruct((G_HEADS, CHUNK, CHUNK), F32),
                   jax.ShapeDtypeStruct((CHUNK, G_HEADS), F32),
                   jax.ShapeDtypeStruct((1, G_W), F32)],
        compiler_params=_params(("arbitrary",)),
    )(z, z, dmix, dqm, g_v.reshape(1, G_W), w_sp, b_sp_t)


def _mem_probs(q, k):
    sc = _dot(q, k, 'nt') * (HEAD ** -0.5)
    sc = sc - jnp.max(sc, axis=-1, keepdims=True)
    e = jnp.exp(sc)
    return e / jnp.sum(e, axis=-1, keepdims=True)


def _memattn_fwd(z, kvm, main, *, qcol, name, tm=512):
    s = z.shape[0]
    m = kvm.shape[0]
    tm = min(tm, s)

    def body(q_ref, kv_ref, main_ref, o_ref):
        o_ref[:, :G_W] = main_ref[...]
        for h in range(MEM_HEADS):
            cols = slice(h * HEAD, (h + 1) * HEAD)
            k = kv_ref[:, cols]
            v = kv_ref[:, MEM_W + h * HEAD:MEM_W + (h + 1) * HEAD]
            p = _mem_probs(q_ref[:, cols], k)
            o_ref[:, G_W + h * HEAD:G_W + (h + 1) * HEAD] = _dot(p.astype(BF16), v, 'nn').astype(o_ref.dtype)

    return pl.pallas_call(
        body, name=name, grid=(s // tm,),
        in_specs=[pl.BlockSpec((tm, MEM_W), lambda i: (i, qcol)),
                  pl.BlockSpec((m, 2 * MEM_W), lambda i: (0, 0)),
                  pl.BlockSpec((tm, G_W), lambda i: (i, 0))],
        out_specs=pl.BlockSpec((tm, G_W + MEM_W), lambda i: (i, 0)),
        out_shape=jax.ShapeDtypeStruct((s, G_W + MEM_W), BF16),
        compiler_params=_params(("parallel",)),
    )(z, kvm, main)


def _memattn_bwd(z, kvm, dmix, *, qcol, name, tm=512):
    s = z.shape[0]
    m = kvm.shape[0]
    tm = min(tm, s)
    scale = HEAD ** -0.5

    def body(q_ref, kv_ref, do_ref, dq_ref, dkv_ref):
        @pl.when(pl.program_id(0) == 0)
        def _():
            dkv_ref[...] = jnp.zeros_like(dkv_ref)

        for h in range(MEM_HEADS):
            cols = slice(h * HEAD, (h + 1) * HEAD)
            vcols = slice(MEM_W + h * HEAD, MEM_W + (h + 1) * HEAD)
            q = q_ref[:, cols]
            k = kv_ref[:, cols]
            v = kv_ref[:, vcols]
            do = do_ref[:, cols]
            p = _mem_probs(q, k)
            dp = _dot(do, v, 'nt')
            ds = (p * (dp - jnp.sum(dp * p, axis=-1, keepdims=True)) * scale).astype(BF16)
            dq_ref[:, cols] = _dot(ds, k, 'nn').astype(dq_ref.dtype)
            dkv_ref[:, cols] += _dot(ds, q, 'tn')
            dkv_ref[:, vcols] += _dot(p.astype(BF16), do, 'tn')

    mo_block = G_W // MEM_W
    return pl.pallas_call(
        body, name=name, grid=(s // tm,),
        in_specs=[pl.BlockSpec((tm, MEM_W), lambda i: (i, qcol)),
                  pl.BlockSpec((m, 2 * MEM_W), lambda i: (0, 0)),
                  pl.BlockSpec((tm, MEM_W), lambda i: (i, mo_block))],
        out_specs=[pl.BlockSpec((tm, MEM_W), lambda i: (i, 0)),
                   pl.BlockSpec((m, 2 * MEM_W), lambda i: (0, 0))],
        out_shape=[jax.ShapeDtypeStruct((s, MEM_W), BF16), jax.ShapeDtypeStruct((m, 2 * MEM_W), F32)],
        compiler_params=_params(("arbitrary",)),
    )(z, kvm, dmix)


def _rope(x1, x2, cos, sin, *, name, inverse=False, out_dtype=BF16, col1=0, col2=0, tm=512):
    s, w = cos.shape
    tm = min(tm, s)
    sign = -1.0 if inverse else 1.0

    def body(a_ref, b_ref, c_ref, s_ref, o1_ref, o2_ref):
        a = a_ref[...].astype(F32)
        b = b_ref[...].astype(F32)
        c = c_ref[...]
        sn = s_ref[...] * sign
        o1_ref[...] = (a * c - b * sn).astype(o1_ref.dtype)
        o2_ref[...] = (b * c + a * sn).astype(o2_ref.dtype)

    row = pl.BlockSpec((tm, w), lambda i: (i, 0))
    return pl.pallas_call(
        body, name=name, grid=(s // tm,),
        in_specs=[pl.BlockSpec((tm, w), lambda i: (i, col1)), pl.BlockSpec((tm, w), lambda i: (i, col2)), row, row],
        out_specs=[row, row],
        out_shape=[jax.ShapeDtypeStruct((s, w), out_dtype)] * 2,
        compiler_params=_params(("parallel",)),
    )(x1, x2, cos, sin)


MHA_BLOCK = 1024
QK_DIM = HEAD + ROPE_DIM


def _mha_scores(q, k, scale, diagonal):
    sc = _dot(q, k, 'nt') * scale
    if not diagonal:
        return sc, None
    rows = lax.broadcasted_iota(jnp.int32, sc.shape, 0)
    cols = lax.broadcasted_iota(jnp.int32, sc.shape, 1)
    return sc, cols <= rows


def _mha_fwd(q, k, vv, *, name):
    s = k.shape[1]
    tb = min(MHA_BLOCK, s)
    scale = QK_DIM ** -0.5

    def body(q_ref, k_ref, v_ref, o_ref, lse_ref, m_ref, l_ref, acc_ref):
        i = pl.program_id(1)
        qh = q_ref[...]
        m_ref[...] = jnp.full_like(m_ref, MASK_VALUE)
        l_ref[...] = jnp.zeros_like(l_ref)
        acc_ref[...] = jnp.zeros_like(acc_ref)

        def block(j, diagonal):
            ks = pl.multiple_of(j * tb, tb)
            kj, vj = k_ref[pl.ds(ks, tb), :], v_ref[pl.ds(ks, tb), :]
            sc, keep = _mha_scores(qh, kj, scale, diagonal)
            if diagonal:
                sc = jnp.where(keep, sc, MASK_VALUE)
            m_old = m_ref[...]
            m_new = jnp.maximum(m_old, jnp.max(sc, axis=-1, keepdims=True))
            p = jnp.exp(sc - m_new)
            alpha = jnp.exp(m_old - m_new)
            l_ref[...] = alpha * l_ref[...] + jnp.sum(p, axis=-1, keepdims=True)
            acc_ref[...] = alpha * acc_ref[...] + _dot(p.astype(BF16), vj, 'nn')
            m_ref[...] = m_new

        def step(j, carry):
            block(j, False)
            return carry

        lax.fori_loop(0, i, step, 0)
        block(i, True)
        l = l_ref[...]
        o_ref[...] = (acc_ref[...] / l).astype(o_ref.dtype)
        lse_ref[...] = m_ref[...] + jnp.log(l)

    return pl.pallas_call(
        body, name=name, grid=(G_HEADS, s // tb),
        in_specs=[pl.BlockSpec((None, tb, QK_DIM), lambda h, i: (h, i, 0)),
                  pl.BlockSpec((None, s, QK_DIM), lambda h, i: (h, 0, 0)),
                  pl.BlockSpec((s, HEAD), lambda h, i: (0, h))],
        out_specs=[pl.BlockSpec((tb, HEAD), lambda h, i: (i, h)),
                   pl.BlockSpec((None, tb, 1), lambda h, i: (h, i, 0))],
        out_shape=[jax.ShapeDtypeStruct((s, G_W), BF16), jax.ShapeDtypeStruct((G_HEADS, s, 1), F32)],
        scratch_shapes=[pltpu.VMEM((tb, 1), F32), pltpu.VMEM((tb, 1), F32), pltpu.VMEM((tb, HEAD), F32)],
        compiler_params=_params(("parallel", "arbitrary")),
    )(q, k, vv)


def _mha_bwd(q, k, vv, o, do, lse, *, name):
    s = k.shape[1]
    tb = min(MHA_BLOCK, s)
    nq = s // tb
    scale = QK_DIM ** -0.5

    def body(q_ref, k_ref, v_ref, o_ref, do_ref, lse_ref, dq_ref, dk_ref, dv_ref, dqa_ref, dka_ref, dva_ref):
        i = pl.program_id(1)

        @pl.when(i == 0)
        def _():
            dka_ref[...] = jnp.zeros_like(dka_ref)
            dva_ref[...] = jnp.zeros_like(dva_ref)

        qh, dov = q_ref[...], do_ref[...]
        delta = jnp.sum(dov.astype(F32) * o_ref[...].astype(F32), axis=-1, keepdims=True)
        lsev = lse_ref[...]
        dqa_ref[...] = jnp.zeros_like(dqa_ref)

        def block(j, diagonal):
            ks = pl.multiple_of(j * tb, tb)
            kj, vj = k_ref[pl.ds(ks, tb), :], v_ref[pl.ds(ks, tb), :]
            sc, keep = _mha_scores(qh, kj, scale, diagonal)
            p = jnp.exp(sc - lsev)
            if diagonal:
                p = jnp.where(keep, p, 0.0)
            dp = _dot(dov, vj, 'nt')
            ds = (p * (dp - delta) * scale).astype(BF16)
            dqa_ref[...] += _dot(ds, kj, 'nn')
            dka_ref[pl.ds(ks, tb), :] += _dot(ds, qh, 'tn')
            dva_ref[pl.ds(ks, tb), :] += _dot(p.astype(BF16), dov, 'tn')

        def step(j, carry):
            block(j, False)
            return carry

        lax.fori_loop(0, i, step, 0)
        block(i, True)
        dq_ref[...] = dqa_ref[...].astype(dq_ref.dtype)

        @pl.when(i == nq - 1)
        def _():
            dk_ref[...] = dka_ref[...].astype(dk_ref.dtype)
            dv_ref[...] = dva_ref[...].astype(dv_ref.dtype)

    q_tile = pl.BlockSpec((None, tb, QK_DIM), lambda h, i: (h, i, 0))
    k_head = pl.BlockSpec((None, s, QK_DIM), lambda h, i: (h, 0, 0))
    tile = pl.BlockSpec((tb, HEAD), lambda h, i: (i, h))
    v_head = pl.BlockSpec((s, HEAD), lambda h, i: (0, h))
    return pl.pallas_call(
        body, name=name, grid=(G_HEADS, nq),
        in_specs=[q_tile, k_head, v_head, tile, tile, pl.BlockSpec((None, tb, 1), lambda h, i: (h, i, 0))],
        out_specs=[q_tile, k_head, v_head],
        out_shape=[jax.ShapeDtypeStruct((G_HEADS, s, QK_DIM), BF16), jax.ShapeDtypeStruct((G_HEADS, s, QK_DIM), BF16),
                   jax.ShapeDtypeStruct((s, G_W), BF16)],
        scratch_shapes=[pltpu.VMEM((tb, QK_DIM), F32), pltpu.VMEM((s, QK_DIM), F32), pltpu.VMEM((s, HEAD), F32)],
        compiler_params=_params(("parallel", "arbitrary")),
    )(q, k, vv, o, do, lse)


HALO = 16


def _shift_down(prev, cur, shift, first_tile):
    tr = cur.shape[0]
    full = jnp.concatenate([prev, cur], axis=0)
    out = pltpu.roll(full, shift, axis=0)[HALO:]
    row = lax.broadcasted_iota(jnp.int32, (tr, 1), 0)
    return jnp.where(jnp.logical_and(first_tile, row < shift), 0.0, out)


def _shift_up(cur, nxt, shift, last_tile):
    tr = cur.shape[0]
    full = jnp.concatenate([cur, nxt], axis=0)
    out = pltpu.roll(full, tr + HALO - shift, axis=0)[:tr]
    row = lax.broadcasted_iota(jnp.int32, (tr, 1), 0)
    return jnp.where(jnp.logical_and(last_tile, row >= tr - shift), 0.0, out)


def _lane_chunks(width, lanes):
    return [slice(c0, min(c0 + lanes, width)) for c0 in range(0, width, lanes)]


def _conv_taps(prev_ref, cur_ref, cw_ref, cb_ref, first_tile, cs):
    cur = cur_ref[:, cs].astype(F32)
    prev = prev_ref[:, cs].astype(F32)
    a1 = _shift_down(prev, cur, 1, first_tile)
    a2 = _shift_down(prev, cur, 2, first_tile)
    c = a2 * cw_ref[0:1, cs] + a1 * cw_ref[1:2, cs] + cur * cw_ref[2:3, cs] + cb_ref[:, cs]
    return c, (a2, a1, cur)


def _conv_in_specs(tr, bw, half, layer, row_of, blk_of):
    per = tr // HALO
    specs = []
    for off in (0, half):
        specs.append(pl.BlockSpec((None, HALO, bw), lambda *g, off=off: (blk_of(*g) + off, jnp.maximum(row_of(*g) * per - 1, 0), 0)))
        specs.append(pl.BlockSpec((None, tr, bw), lambda *g, off=off: (blk_of(*g) + off, row_of(*g), 0)))
    for off in (0, half):
        specs.append(pl.BlockSpec((None, None, CONV_W, bw), lambda *g, off=off: (blk_of(*g) + off, layer, 0, 0)))
    for off in (0, half):
        specs.append(pl.BlockSpec((None, 1, bw), lambda *g, off=off: (layer * 2 * half + blk_of(*g) + off, 0, 0)))
    return specs


def _conv_fwd(a, cw, cb, layer, *, name, tr=1024):
    nb, s, bw = a.shape
    half = nb // 2
    tr = min(tr, s)

    def body(gp_ref, gc_ref, vp_ref, vc_ref, cwg_ref, cwv_ref, cbg_ref, cbv_ref, o_ref):
        first = pl.program_id(0) == 0
        for cs in _lane_chunks(bw, 256):
            gate, _ = _conv_taps(gp_ref, gc_ref, cwg_ref, cbg_ref, first, cs)
            val, _ = _conv_taps(vp_ref, vc_ref, cwv_ref, cbv_ref, first, cs)
            o_ref[:, cs] = (gate * _sigmoid(gate) * val).astype(o_ref.dtype)

    return pl.pallas_call(
        body, name=name, grid=(s // tr, half),
        in_specs=_conv_in_specs(tr, bw, half, layer, lambda i, j: i, lambda i, j: j),
        out_specs=pl.BlockSpec((tr, bw), lambda i, j: (i, j)),
        out_shape=jax.ShapeDtypeStruct((s, half * bw), BF16),
        compiler_params=_params(("parallel", "parallel")),
    )(a, a, a, a, cw, cw, cb, cb)


def _conv_bwd_dc(a, dact, cw, cb, layer, *, name, after=None, tr=1024):
    nb, s, bw = a.shape
    half = nb // 2
    tr = min(tr, s)

    def body(*refs):
        gp_ref, gc_ref, vp_ref, vc_ref, cwg_ref, cwv_ref, cbg_ref, cbv_ref, da_ref = refs[:9]
        dc_ref, dw_ref, db_ref = refs[-3:]
        first = pl.program_id(1) == 0

        @pl.when(first)
        def _():
            dw_ref[...] = jnp.zeros_like(dw_ref)
            db_ref[...] = jnp.zeros_like(db_ref)

        for cs in _lane_chunks(bw, 128):
            gate, gtaps = _conv_taps(gp_ref, gc_ref, cwg_ref, cbg_ref, first, cs)
            val, vtaps = _conv_taps(vp_ref, vc_ref, cwv_ref, cbv_ref, first, cs)
            dact_v = da_ref[:, cs].astype(F32)
            sg = _sigmoid(gate)
            dgate = dact_v * val * (sg * (1.0 + gate * (1.0 - sg)))
            dval = dact_v * (gate * sg)
            dc_ref[0, :, cs] = dgate.astype(dc_ref.dtype)
            dc_ref[1, :, cs] = dval.astype(dc_ref.dtype)
            for kk in range(CONV_W):
                dw_ref[0, kk:kk + 1, cs] += jnp.sum(dgate * gtaps[kk], axis=0, keepdims=True)
                dw_ref[1, kk:kk + 1, cs] += jnp.sum(dval * vtaps[kk], axis=0, keepdims=True)
            db_ref[0, :, cs] += jnp.sum(dgate, axis=0, keepdims=True)
            db_ref[1, :, cs] += jnp.sum(dval, axis=0, keepdims=True)

    outs = pl.pallas_call(
        body, name=name, grid=(half, s // tr),
        in_specs=_conv_in_specs(tr, bw, half, layer, lambda j, i: i, lambda j, i: j)
        + [pl.BlockSpec((tr, bw), lambda j, i: (i, j))]
        + ([pl.BlockSpec(memory_space=pl.ANY)] if after is not None else []),
        out_specs=[pl.BlockSpec((2, None, tr, bw), lambda j, i: (0, j, i, 0)),
                   pl.BlockSpec((2, None, CONV_W, bw), lambda j, i: (0, j, 0, 0)),
                   pl.BlockSpec((2, None, 1, bw), lambda j, i: (0, j, 0, 0))],
        out_shape=[jax.ShapeDtypeStruct((2, half, s, bw), BF16),
                   jax.ShapeDtypeStruct((2, half, CONV_W, bw), F32),
                   jax.ShapeDtypeStruct((2, half, 1, bw), F32)],
        compiler_params=_params(("parallel", "arbitrary")),
    )(a, a, a, a, cw, cw, cb, cb, dact, *([after] if after is not None else []))
    dc, dw, db = outs
    return dc.reshape(nb, s, bw), dw.reshape(nb, CONV_W, bw), db.reshape(nb, 1, bw)


def _conv_bwd_da(dc, cw, layer, *, name, tr=1024):
    nb, s, bw = dc.shape
    tr = min(tr, s)
    ni = s // tr
    per = tr // HALO
    last_halo = s // HALO - 1

    def body(c_ref, n_ref, w_ref, o_ref):
        last = pl.program_id(0) == ni - 1
        for cs in _lane_chunks(bw, 256):
            cur = c_ref[:, cs].astype(F32)
            nxt = n_ref[:, cs].astype(F32)
            da = (cur * w_ref[2:3, cs] + _shift_up(cur, nxt, 1, last) * w_ref[1:2, cs]
                  + _shift_up(cur, nxt, 2, last) * w_ref[0:1, cs])
            o_ref[:, cs] = da.astype(o_ref.dtype)

    tile = pl.BlockSpec((None, tr, bw), lambda i, j: (j, i, 0))
    return pl.pallas_call(
        body, name=name, grid=(ni, nb),
        in_specs=[tile,
                  pl.BlockSpec((None, HALO, bw), lambda i, j: (j, jnp.minimum((i + 1) * per, last_halo), 0)),
                  pl.BlockSpec((None, None, CONV_W, bw), lambda i, j: (j, layer, 0, 0))],
        out_specs=tile,
        out_shape=jax.ShapeDtypeStruct((nb, s, bw), BF16),
        compiler_params=_params(("parallel", "parallel")),
    )(dc, dc, cw)


def _rope_tables(positions):
    inv = 1.0 / (ROPE_THETA ** (jnp.arange(0, ROPE_DIM, 2, dtype=F32) / ROPE_DIM))
    ang = positions.astype(F32)[:, None] * inv
    return jnp.cos(ang), jnp.sin(ang)


def _heads_to_major(nope, r1, r2):
    s = r1.shape[0]
    parts = [nope[:, :G_W].reshape(s, G_HEADS, HEAD)]
    for r in (r1, r2):
        parts.append(jnp.broadcast_to(r.reshape(s, -1, ROPE_HALF), (s, G_HEADS, ROPE_HALF)))
    return jnp.concatenate(parts, axis=-1).transpose(1, 0, 2)


def _heads_from_major(t):
    s = t.shape[1]
    t = t.transpose(1, 0, 2)
    return t[:, :, :HEAD].reshape(s, G_W), t[:, :, HEAD:HEAD + ROPE_HALF], t[:, :, HEAD + ROPE_HALF:]


def _local_step(x, mem, positions, target, rep, fetch, emit):
    s, d = x.shape
    n_b = DEPTH - N_A
    tm = min(1024, s)
    cos, sin = _rope_tables(positions)
    cos12 = jnp.tile(cos, (1, G_HEADS))
    sin12 = jnp.tile(sin, (1, G_HEADS))
    r1_col = G_W // (G_HEADS * ROPE_HALF)
    b_sp_t = rep['b_sp'].transpose(0, 2, 1)

    saved = []
    kv = None
    shared = None
    for l in range(DEPTH):
        wm = fetch(('in', l), x)
        if l == 0:
            shared = {'g_v': wm['g_v'], 'conv_w': wm['conv_w']}
            bw = shared['conv_w'].shape[-1]
            conv_b = rep['conv_b'].reshape(-1, 1, bw)
        sv = {'x_in': x, 'wm': wm}
        if l == N_A:
            xn_kv = _rmsnorm(x, rep['g_kv'], name="kvnorm")
            kvx = _mm(xn_kv, wm['w_kv_a'], dims='nn', out_dtype=F32, name="kvproj", tm=tm, tn=KV_PAD)
            ckv = _rmsnorm(kvx, rep['g_kv_lat'], width=KV_RANK, name="ckvnorm")
            k1, k2 = _rope(kvx[:, KV_RANK:KV_RANK + ROPE_HALF], kvx[:, KV_RANK + ROPE_HALF:KV_RANK + ROPE_DIM],
                           cos, sin, name="krope")
            kv = {'x': x, 'xn': xn_kv, 'kvx': kvx, 'ckv': ckv, 'k1': k1, 'k2': k2, 'w_kv_a': wm['w_kv_a']}
        h = _rmsnorm(x, rep['g_mix'][l], name=f"mixnorm{l}")
        if l < N_A:
            z = _mm(h, wm['w_in'], dims='nn', out_dtype=BF16, name=f"in_a{l}", tm=tm, tn=wm['w_in'].shape[1] // 2)
            main = _sgu_fwd(z, shared['g_v'][l], rep['w_sp'][l], b_sp_t[l], name=f"sgu{l}")
            qcol = 2 * G_W // MEM_W
        else:
            j = l - N_A
            z = _mm(h, wm['w_in'], dims='nn', out_dtype=BF16, name=f"in_b{j}", tm=tm, tn=1024)
            qn = _rmsnorm(z, rep['g_q_lat'][j], width=Q_RANK, name=f"qnorm{j}")
            qp = _mm(qn, wm['w_uqp'], dims='nn', out_dtype=BF16, name=f"uq{j}", tm=tm, tn=768)
            rr1, rr2 = _rope(qp, qp, cos12, sin12, col1=r1_col, col2=r1_col + 1, name=f"qrope{j}")
            qh = _heads_to_major(qp, rr1, rr2)
            kn = _mm(kv['ckv'], wm['w_uk'], dims='nn', out_dtype=BF16, name=f"k_up{j}", tm=tm, tn=768)
            kh = _heads_to_major(kn, kv['k1'], kv['k2'])
            vv = _mm(kv['ckv'], wm['w_uv'], dims='nn', out_dtype=BF16, name=f"v_up{j}", tm=tm, tn=768)
            main, lse = _mha_fwd(qh, kh, vv, name=f"mha{j}")
            qcol = Q_RANK // MEM_W
            sv.update(qn=qn, qh=qh, kh=kh, vv=vv, lse=lse)
        wm.update(fetch(('rest', l), z))
        memn = _rmsnorm(mem, rep['g_mem'][l], name=f"memnorm{l}")
        kvm = _mm(memn, wm['w_mem_kv'], dims='nn', out_dtype=BF16, name=f"memkv{l}", tm=tm, tn=1024)
        mix = _memattn_fwd(z, kvm, main, qcol=qcol, name=f"memattn{l}")
        x_mid = _mm(mix, wm['w_out'], dims='nn', res=x, out_dtype=F32, name=f"out{l}", tm=tm, tn=1024)
        wf = fetch(('up', l), x_mid)
        h2 = _rmsnorm(x_mid, rep['g_ffn'][l], name=f"ffnnorm{l}")
        a = _mm(h2, wf['w_up'], dims='nn', b_blocked=True, out_dtype=BF16, out_block=bw,
                name=f"up{l}", tm=tm, tn=bw)
        act = _conv_fwd(a, shared['conv_w'], conv_b, l, name=f"conv{l}")
        wf.update(fetch(('down', l), act))
        x = _mm(act, wf['w_down'], dims='nn', res=x_mid, out_dtype=F32, name=f"down{l}", tm=512, tn=1024)
        sv.update(h=h, memn=memn, kvm=kvm, z=z, qcol=qcol, mix=mix, x_mid=x_mid, h2=h2, a=a, act=act, wf=wf)
        saved.append(sv)

    sq, dx, dxb, dg_final = _final_loss(x, target, rep['g_final'], name="loss")

    g = {k: [None] * DEPTH for k in ('g_mix', 'g_ffn', 'g_mem', 'conv_w', 'conv_b')}
    for k in ('g_v', 'w_sp', 'b_sp'):
        g[k] = [None] * N_A
    g['g_q_lat'] = [None] * n_b
    g['g_final'] = dg_final
    dckv_sum, dkr_sum = None, None

    for l in reversed(range(DEPTH)):
        sv = saved[l]
        wm, wf = sv['wm'], sv['wf']
        dact = _mm(dxb, wf['w_down'], dims='nt', out_dtype=BF16, name=f"d_act{l}", tm=tm, tn=bw)
        dw_down = _mm(sv['act'], dxb, dims='tn', out_dtype=BF16, name=f"dw_down{l}", tm=bw, tn=512)
        tok = emit(('down', l), {'w_ffn_down': dw_down})
        dc, dcw, dcb = _conv_bwd_dc(sv['a'], dact, shared['conv_w'], conv_b, l, after=tok, name=f"d_conv{l}")
        g['conv_w'][l], g['conv_b'][l] = dcw, dcb
        da = _conv_bwd_da(dc, shared['conv_w'], l, name=f"d_convin{l}")
        dw_up = _mm(sv['h2'], da, dims='tn', b_blocked=True, out_dtype=BF16, out_block=bw,
                    name=f"dw_up{l}", tm=512, tn=bw, n_outer=True)
        tok = emit(('up', l), {'w_ffn_up': dw_up})
        dh2 = _mm_blocked_nt(da, wf['w_up'], out_dtype=BF16, name=f"d_h2{l}", tm=512, tn=1024, blocks_per_step=4,
                             after=tok)
        dx_mid, dx_mid_b, g['g_ffn'][l] = _rmsnorm_bwd(sv['x_mid'], rep['g_ffn'][l], dh2, dres=dx, bf16_copy=True,
                                                       name=f"d_ffnnorm{l}")
        dmix = _mm(dx_mid_b, wm['w_out'], dims='nt', out_dtype=BF16, name=f"d_mix{l}", tm=tm, tn=1024)
        dw_out = _mm(sv['mix'], dx_mid_b, dims='tn', out_dtype=BF16, name=f"dw_out{l}", tm=1024, tn=512)
        dqm, dkvm = _memattn_bwd(sv['z'], sv['kvm'], dmix, qcol=sv['qcol'], name=f"d_memattn{l}")
        dw_memkv = _mm(sv['memn'], dkvm, dims='tn', out_dtype=BF16, name=f"dw_memkv{l}", tm=1024, tn=1024)
        tok = emit(('rest', l), {'w_out': dw_out, 'w_mem_kv': dw_memkv})
        gm = {}
        dmemn = _mm(dkvm, wm['w_mem_kv'], dims='nt', out_dtype=F32, name=f"d_memn{l}", tm=tm, tn=1024, after=tok)
        _, g['g_mem'][l] = _rmsnorm_bwd(mem, rep['g_mem'][l], dmemn, out_dtype=BF16, name=f"d_memnorm{l}")
        if l < N_A:
            dz, dwsp, dbsp_t, dgv = _sgu_bwd(sv['z'], dmix, dqm, shared['g_v'][l], rep['w_sp'][l], b_sp_t[l],
                                             name=f"d_sgu{l}")
            g['w_sp'][l], g['b_sp'][l], g['g_v'][l] = dwsp, dbsp_t.T, dgv
            dh = _mm(dz, wm['w_in'], dims='nt', out_dtype=BF16, name=f"d_h_a{l}", tm=tm, tn=1024)
            gm['w_in_a'] = _mm(sv['h'], dz, dims='tn', out_dtype=BF16, name=f"dw_in_a{l}", tm=1024, tn=512)
        else:
            j = l - N_A
            dqh, dkh, dvv = _mha_bwd(sv['qh'], sv['kh'], sv['vv'], sv['mix'], dmix, sv['lse'], name=f"d_mha{j}")
            dq_nope, dr1, dr2 = _heads_from_major(dqh)
            dkn, dk1, dk2 = _heads_from_major(dkh)
            dkr = jnp.concatenate([dk1.astype(F32).sum(axis=1), dk2.astype(F32).sum(axis=1)], axis=-1)
            gm['w_uk'] = _mm(kv['ckv'], dkn, dims='tn', out_dtype=BF16, name=f"dw_uk{j}", tm=512, tn=768)
            gm['w_uv'] = _mm(kv['ckv'], dvv, dims='tn', out_dtype=BF16, name=f"dw_uv{j}", tm=512, tn=768)
            dckv = _mm(dkn, wm['w_uk'], dims='nt', out_dtype=F32, res=dckv_sum, name=f"d_ckv_k{j}", tm=tm, tn=512)
            dckv_sum = _mm(dvv, wm['w_uv'], dims='nt', out_dtype=F32, res=dckv, name=f"d_ckv_v{j}", tm=tm, tn=512)
            dkr_sum = dkr if dkr_sum is None else dkr_sum + dkr
            dq1, dq2 = _rope(dr1.reshape(s, -1), dr2.reshape(s, -1), cos12, sin12, inverse=True, name=f"d_qrope{j}")
            dqp = jnp.concatenate([dq_nope, dq1, dq2], axis=-1)
            dqn = _mm(dqp, wm['w_uqp'], dims='nt', out_dtype=BF16, name=f"d_qn{j}", tm=tm, tn=512)
            gm['w_uqp'] = _mm(sv['qn'], dqp, dims='tn', out_dtype=BF16, name=f"dw_uq{j}", tm=512, tn=768)
            dqlat, g['g_q_lat'][j] = _rmsnorm_bwd(sv['z'], rep['g_q_lat'][j], dqn, width=Q_RANK, out_dtype=BF16,
                                                 name=f"d_qnorm{j}")
            dz = jnp.concatenate([dqlat, dqm], axis=-1)
            dh = _mm(dz, wm['w_in'], dims='nt', out_dtype=BF16, name=f"d_h_b{j}", tm=tm, tn=1024)
            gm['w_in_b'] = _mm(sv['h'], dz, dims='tn', out_dtype=BF16, name=f"dw_in_b{j}", tm=1024, tn=512)
        tok = emit(('mix', l), gm)
        dx, dxb, g['g_mix'][l] = _rmsnorm_bwd(sv['x_in'], rep['g_mix'][l], dh, dres=dx_mid, after=tok, bf16_copy=True,
                                              name=f"d_mixnorm{l}")
        if l == N_A:
            dkvx_c, g['g_kv_lat'] = _rmsnorm_bwd(kv['kvx'], rep['g_kv_lat'], dckv_sum, width=KV_RANK, out_dtype=BF16,
                                                 name="d_ckvnorm")
            dk1, dk2 = _rope(dkr_sum[:, :ROPE_HALF], dkr_sum[:, ROPE_HALF:], cos, sin, inverse=True, name="d_krope")
            dkvx = jnp.concatenate([dkvx_c, dk1, dk2, jnp.zeros((s, KV_PAD - KV_RANK - ROPE_DIM), BF16)], axis=-1)
            dxn = _mm(dkvx, kv['w_kv_a'], dims='nt', out_dtype=BF16, name="d_kvnorm_in", tm=tm, tn=1024)
            dw_kv = _mm(kv['xn'], dkvx, dims='tn', out_dtype=BF16, name="dw_kv", tm=1024, tn=KV_PAD)
            tok = emit(('kv', 0), {'w_kv_a': dw_kv})
            dx, dxb, g['g_kv'] = _rmsnorm_bwd(kv['x'], rep['g_kv'], dxn, dres=dx, after=tok, bf16_copy=True,
                                              name="d_kvnorm")
    return jnp.sum(sq), dx, g


MESH_IDS = pl.DeviceIdType.MESH
PEER_MASKS = tuple((k >> 2 & 1, k >> 1 & 1, k & 1) for k in range(1, N_DEV))
CHIP_MASKS = ((1, 0), (0, 1), (1, 1))
N_PEER = N_DEV - 1
SEMS_PER_BUFFER = 2 * N_PEER + 1
DATAFLOW = pltpu.SideEffectType.DATAFLOW_SIDE_EFFECTING
HBM_SPEC = pl.BlockSpec(memory_space=pltpu.HBM)
SEM_SPEC = pl.BlockSpec(memory_space=pltpu.SEMAPHORE)


def _my_position():
    return lax.axis_index("x"), lax.axis_index("y"), lax.axis_index("c")


def _flip(pos, mask):
    return tuple(1 - p if f else p for p, f in zip(pos, mask))


def _linear_id(pos):
    return 4 * pos[0] + 2 * pos[1] + pos[2]


def _hbm(x):
    return pltpu.with_memory_space_constraint(x, pltpu.HBM)


def _buffer_copies(src_ref, lead, land_ref, sems, scatter, near=False):
    me = _my_position()
    my_id = _linear_id(me)
    src = src_ref.at[lead] if lead else src_ref
    own = pltpu.make_async_copy(src.at[my_id] if scatter else src, land_ref.at[my_id], sems.at[2 * N_PEER])
    pairs = []
    for k, mask in enumerate(PEER_MASKS):
        if near and mask[2] == 1 and mask != (0, 0, 1):
            continue
        peer = _flip(me, mask)
        peer_id = _linear_id(peer)
        block = src.at[peer_id] if scatter else src
        send = pltpu.make_async_remote_copy(src_ref=block, dst_ref=land_ref.at[my_id], send_sem=sems.at[k],
                                            recv_sem=sems.at[N_PEER + k], device_id=peer, device_id_type=MESH_IDS)
        arrival = pltpu.make_async_remote_copy(src_ref=block, dst_ref=land_ref.at[peer_id], send_sem=sems.at[k],
                                               recv_sem=sems.at[N_PEER + k], device_id=peer, device_id_type=MESH_IDS)
        pairs.append((send, arrival))
    return own, pairs


def _exchange_start(srcs, buffers, *, name, scatter):
    ns, nb = len(srcs), len(buffers)
    lands = [_hbm(lax.empty((N_DEV,) + tuple(shape), dtype)) for _, _, shape, dtype, _ in buffers]

    def body(*refs):
        src_refs, land_refs = refs[:ns], refs[ns:ns + nb]
        sem_refs = refs[ns + nb:ns + 2 * nb]
        token = refs[-1]
        for b, (si, lead, _, _, near) in enumerate(buffers):
            own, pairs = _buffer_copies(src_refs[si], lead, land_refs[b], sem_refs[b], scatter, near)
            own.start()
            for send, _ in pairs:
                send.start()
        token[...] = jnp.zeros_like(token)

    out_shape = ([pltpu.SemaphoreType.DMA((SEMS_PER_BUFFER,))] * nb
                 + [pltpu.HBM(a.shape, a.dtype) for a in srcs]
                 + [pltpu.HBM(a.shape, a.dtype) for a in lands]
                 + [jax.ShapeDtypeStruct((8, 128), F32)])
    aliases = {i: nb + i for i in range(ns + nb)}
    outs = pl.pallas_call(
        body, name=name, in_specs=[HBM_SPEC] * (ns + nb),
        out_specs=[SEM_SPEC] * nb + [HBM_SPEC] * (ns + nb) + [pl.BlockSpec(memory_space=pltpu.VMEM)],
        out_shape=out_shape, input_output_aliases=aliases,
        compiler_params=pltpu.CompilerParams(has_side_effects=DATAFLOW),
    )(*[_hbm(a) for a in srcs], *lands)
    sems = list(outs[:nb])
    src_thru = list(outs[nb:nb + ns])
    land_thru = list(outs[nb + ns:nb + ns + nb])
    return sems, land_thru, src_thru, outs[-1]


def _exchange_wait(srcs_thru, buffers, sems, lands, after, *, name, scatter):
    ns, nb = len(srcs_thru), len(buffers)
    has_after = after is not None

    def body(*refs):
        src_refs, land_refs = refs[:ns], refs[ns:ns + nb]
        sem_refs = refs[ns + nb:ns + 2 * nb]
        for b, (si, lead, _, _, near) in enumerate(buffers):
            own, pairs = _buffer_copies(src_refs[si], lead, land_refs[b], sem_refs[b], scatter, near)
            for send, arrival in pairs:
                send.wait_send()
                arrival.wait_recv()
            own.wait()

    operands = list(srcs_thru) + list(lands) + list(sems) + ([after] if has_after else [])
    in_specs = ([HBM_SPEC] * (ns + nb) + [SEM_SPEC] * nb + ([pl.BlockSpec(memory_space=pl.ANY)] if has_after else []))
    outs = pl.pallas_call(
        body, name=name, in_specs=in_specs, out_specs=[HBM_SPEC] * nb,
        out_shape=[pltpu.HBM(a.shape, a.dtype) for a in lands],
        input_output_aliases={ns + b: b for b in range(nb)},
        compiler_params=pltpu.CompilerParams(has_side_effects=DATAFLOW),
    )(*operands)
    return list(outs)


def _exchange(arrays, *, name, scatter, near=None, after=None):
    n = len(arrays)
    near = [False] * n if near is None else near
    extra = [] if after is None else [after]
    out_shapes = [jax.ShapeDtypeStruct(a.shape if scatter else (N_DEV,) + a.shape, a.dtype) for a in arrays]

    def body(*refs):
        srcs, outs, sems = refs[:n], refs[n + len(extra):2 * n + len(extra)], refs[2 * n + len(extra):]
        started = []
        for a in range(n):
            own, pairs = _buffer_copies(srcs[a], (), outs[a], sems[a], scatter, near[a])
            own.start()
            for send, _ in pairs:
                send.start()
            started.append((own, pairs))
        for own, pairs in started:
            for send, arrival in pairs:
                arrival.wait_recv()
                send.wait_send()
            own.wait()

    any_spec = pl.BlockSpec(memory_space=pl.ANY)
    outs = pl.pallas_call(
        body, name=name, in_specs=[any_spec] * (n + len(extra)), out_specs=[any_spec] * n, out_shape=out_shapes,
        scratch_shapes=[pltpu.SemaphoreType.DMA((SEMS_PER_BUFFER,))] * n,
    )(*arrays, *extra)
    return list(outs)


def _forward_to_sibling(lands, *, name):
    n = len(lands)

    def body(*refs):
        ins, outs, sems = refs[:n], refs[n:2 * n], refs[2 * n:]
        me = _my_position()
        sibling = _flip(me, (0, 0, 1))
        pairs = []
        for b in range(n):
            for k, (fx, fy) in enumerate(CHIP_MASKS):
                mine = _linear_id(_flip(me, (fx, fy, 0)))
                theirs = _linear_id(_flip(me, (fx, fy, 1)))
                send = pltpu.make_async_remote_copy(
                    src_ref=ins[b].at[mine], dst_ref=outs[b].at[mine], send_sem=sems[b].at[k],
                    recv_sem=sems[b].at[len(CHIP_MASKS) + k], device_id=sibling, device_id_type=MESH_IDS)
                arrival = pltpu.make_async_remote_copy(
                    src_ref=ins[b].at[mine], dst_ref=outs[b].at[theirs], send_sem=sems[b].at[k],
                    recv_sem=sems[b].at[len(CHIP_MASKS) + k], device_id=sibling, device_id_type=MESH_IDS)
                send.start()
                pairs.append((send, arrival))
        for send, arrival in pairs:
            arrival.wait_recv()
            send.wait_send()

    any_spec = pl.BlockSpec(memory_space=pl.ANY)
    outs = pl.pallas_call(
        body, name=name, in_specs=[any_spec] * n, out_specs=[any_spec] * n,
        out_shape=[jax.ShapeDtypeStruct(a.shape, a.dtype) for a in lands],
        input_output_aliases={b: b for b in range(n)},
        scratch_shapes=[pltpu.SemaphoreType.DMA((2 * len(CHIP_MASKS),))] * n,
    )(*lands)
    return list(outs)


def _sum_slots(parts_ref):
    total = parts_ref[0].astype(F32)
    for q in range(1, parts_ref.shape[0]):
        total = total + parts_ref[q].astype(F32)
    return total


def _row_tile(rows, cols, n_arrays):
    budget = (24 * 1024 * 1024) // (4 * n_arrays * max(cols, 128))
    t = rows
    while t > budget and t % 2 == 0 and (t // 2) % 16 == 0:
        t //= 2
    return t


def _sum_adam(parts, w, m, v, layer, outs, *, name):
    q, r, c = parts.shape
    nl = w.shape[0]
    tr = _row_tile(r, c, q + 7)
    c1 = 1.0 - ADAM_B1 ** ADAM_STEP
    c2 = 1.0 - ADAM_B2 ** ADAM_STEP
    if outs is None:
        outs = [lax.empty((nl, r, c), F32) for _ in range(4)]

    def body(p_ref, w_ref, m_ref, v_ref, g_in, d_in, mo_in, vo_in, g_ref, d_ref, mo_ref, vo_ref):
        grad = _sum_slots(p_ref)
        m_new = ADAM_B1 * m_ref[...] + (1.0 - ADAM_B1) * grad
        v_new = ADAM_B2 * v_ref[...] + (1.0 - ADAM_B2) * (grad * grad)
        m_hat = m_new / c1
        v_hat = v_new / c2
        g_ref[...] = grad
        d_ref[...] = -ADAM_LR * (m_hat / (jnp.sqrt(v_hat) + ADAM_EPS) + ADAM_WD * w_ref[...])
        mo_ref[...] = m_new
        vo_ref[...] = v_new

    tile = pl.BlockSpec((None, tr, c), lambda i: (layer, i, 0))
    any_spec = pl.BlockSpec(memory_space=pl.ANY)
    return pl.pallas_call(
        body, name=name, grid=(r // tr,),
        in_specs=[pl.BlockSpec((q, tr, c), lambda i: (0, i, 0)), tile, tile, tile] + [any_spec] * 4,
        out_specs=[tile] * 4, out_shape=[jax.ShapeDtypeStruct((nl, r, c), F32)] * 4,
        input_output_aliases={4: 0, 5: 1, 6: 2, 7: 3},
        compiler_params=_params(("parallel",)),
    )(parts, w, m, v, *outs)


def _sum_parts(parts, *, name):
    q, r, c = parts.shape

    def body(p_ref, o_ref):
        o_ref[...] = _sum_slots(p_ref)

    return pl.pallas_call(
        body, name=name, in_specs=[pl.BlockSpec((q, r, c), lambda: (0, 0, 0))],
        out_specs=pl.BlockSpec((r, c), lambda: (0, 0)), out_shape=jax.ShapeDtypeStruct((r, c), F32),
        compiler_params=_params(),
    )(parts)


INPUT_NAMES = (['x', 'mem', 'positions'] + WEIGHTS + ['loss_target'] + ['m_' + n for n in WEIGHTS]
               + ['v_' + n for n in WEIGHTS])
SMALL_ALIGN = N_DEV * 8 * 128
TWO_LEVEL_LAYERS = N_A
GROUP_ORDER = ('in', 'rest', 'up', 'down')
GROUP_WEIGHTS = {'in': (['w_in_a'], ['w_in_b', 'w_uq', 'w_uk', 'w_uv']), 'rest': (['w_mem_kv', 'w_out'],) * 2,
                 'up': (['w_ffn_up'],) * 2, 'down': (['w_ffn_down'],) * 2}
LAYERED = {'w_in_a': 0, 'w_in_b': N_A, 'w_uq': N_A, 'w_uk': N_A, 'w_uv': N_A, 'w_mem_kv': 0, 'w_out': 0,
           'w_ffn_up': 0, 'w_ffn_down': 0}


def _permute_uq(w_uq):
    r = w_uq.shape[0]
    q = w_uq.reshape(r, G_HEADS, HEAD + ROPE_DIM)
    return jnp.concatenate([q[..., :HEAD].reshape(r, -1), q[..., HEAD:HEAD + ROPE_HALF].reshape(r, -1),
                            q[..., HEAD + ROPE_HALF:].reshape(r, -1)], axis=-1)


def _unpermute_uq(w_uqp):
    r = w_uqp.shape[0]
    nope = w_uqp[..., :G_W].reshape(r, G_HEADS, HEAD)
    r1 = w_uqp[..., G_W:G_W + G_HEADS * ROPE_HALF].reshape(r, G_HEADS, ROPE_HALF)
    r2 = w_uqp[..., G_W + G_HEADS * ROPE_HALF:].reshape(r, G_HEADS, ROPE_HALF)
    return jnp.concatenate([nope, r1, r2], axis=-1).reshape(r, -1)


def _cols_from_stack(st):
    _, r, n = st.shape
    return st.transpose(1, 0, 2).reshape(r, N_DEV * n)


def _cols_to_stack(wh):
    r, c = wh.shape
    return wh.reshape(r, N_DEV, c // N_DEV).transpose(1, 0, 2)


def _group_weights(group):
    kind, l = group
    return GROUP_WEIGHTS[kind][0 if l < N_A else 1]


def _step(args):
    p = dict(zip(INPUT_NAMES, args))
    x, mem, positions, target = p['x'][0], p['mem'][0], p['positions'][0], p['loss_target'][0]
    d = x.shape[-1]
    my_id = _linear_id(_my_position())

    w_kv_pad = jnp.pad(p['w_kv_a'], ((0, 0), (0, KV_PAD - p['w_kv_a'].shape[1])))
    shard = {k: p[k].astype(BF16) for k in LAYERED}
    shard['w_uk'] = shard['w_uk'].reshape(shard['w_uk'].shape[0], shard['w_uk'].shape[1], -1)
    shard['w_uv'] = shard['w_uv'].reshape(shard['w_uv'].shape[0], shard['w_uv'].shape[1], -1)
    shard.update(conv_w=p['conv_w'], g_v=p['g_v'], w_kv_a=w_kv_pad.astype(BF16))
    src_names = list(shard)
    gather_groups = []
    for l in range(DEPTH):
        gather_groups += [(kind, l) for kind in GROUP_ORDER]
    buffers, owner = [], []
    for group in gather_groups:
        kind, l = group
        for k in _group_weights(group):
            buffers.append((src_names.index(k), (l - LAYERED[k],), shard[k].shape[1:], shard[k].dtype, l < TWO_LEVEL_LAYERS))
            owner.append((group, k))
        if group == ('in', 0):
            for k in ('g_v', 'conv_w'):
                buffers.append((src_names.index(k), (), shard[k].shape, shard[k].dtype, True))
                owner.append((group, k))
        if group == ('in', N_A):
            buffers.append((src_names.index('w_kv_a'), (), shard['w_kv_a'].shape, BF16, False))
            owner.append((group, 'w_kv_a'))
    g_sems, g_lands, g_srcs, _ = _exchange_start([shard[k] for k in src_names], buffers, name="gather_start",
                                                 scatter=False)

    def fetch(group, after):
        idx = [i for i, (grp, _) in enumerate(owner) if grp == group]
        landed = _exchange_wait(g_srcs, [buffers[i] for i in idx], [g_sems[i] for i in idx],
                                [g_lands[i] for i in idx], after, name=f"gather_wait_{group[0]}{group[1]}",
                                scatter=False)
        if group[1] < TWO_LEVEL_LAYERS:
            landed = _forward_to_sibling(landed, name=f"gather_forward_{group[0]}{group[1]}")
        got = {owner[i][1]: t for i, t in zip(idx, landed)}
        out = {}
        for k, t in got.items():
            if k in ('w_in_a', 'w_uq'):
                out[k] = _cols_from_stack(t)
            elif k == 'g_v':
                out[k] = t.transpose(1, 0, 2).reshape(t.shape[1], -1)
            elif k in ('w_ffn_up', 'conv_w'):
                out[k] = t
            else:
                out[k] = t.reshape(-1, t.shape[-1])
        if 'w_uq' in out:
            out['w_uqp'] = _permute_uq(out.pop('w_uq'))
        for old, new in (('w_in_a', 'w_in'), ('w_in_b', 'w_in'), ('w_ffn_up', 'w_up'), ('w_ffn_down', 'w_down')):
            if old in out:
                out[new] = out.pop(old)
        return out

    pending = []

    def emit(group, grads):
        send = {}
        for k, t in grads.items():
            if k == 'w_in_a':
                send[k] = _cols_to_stack(t)
            elif k == 'w_uqp':
                send['w_uq'] = _cols_to_stack(_unpermute_uq(t))
            elif k == 'w_ffn_up':
                send[k] = t
            elif k == 'w_kv_a':
                cols = p['w_kv_a'].shape[1]
                send[k] = t[:, :cols].reshape(N_DEV, -1, cols)
            else:
                send[k] = t.reshape(N_DEV, t.shape[0] // N_DEV, t.shape[1])
        keys = list(send)
        bufs = [(i, (), send[k].shape[1:], send[k].dtype, False) for i, k in enumerate(keys)]
        sems, lands, srcs, token = _exchange_start([send[k] for k in keys], bufs,
                                                   name=f"scatter_start_{group[0]}{group[1]}", scatter=True)
        pending.append((group, keys, bufs, sems, lands, srcs))
        return token

    rep = {k: p[k] for k in REPLICATED}
    sq, grad_x, g = _local_step(x, mem, positions, target, rep, fetch, emit)
    loss = (0.5 / d) * lax.psum(sq, ("x", "y", "c"))

    out, running = {}, {}
    order = grad_x
    for group, keys, bufs, sems, lands, srcs in pending:
        landed = _exchange_wait(srcs, bufs, sems, lands, order, name=f"scatter_wait_{group[0]}{group[1]}", scatter=True)
        for k, parts in zip(keys, landed):
            stacked = k in LAYERED
            nl = p[k].shape[0] if stacked else 1
            layer = group[1] - LAYERED[k] if stacked else 0
            rows = p[k].size // nl // p[k].shape[-1]
            view = (nl, rows, p[k].shape[-1])
            running[k] = _sum_adam(parts.reshape(N_DEV, rows, view[2]), p[k].reshape(view), p['m_' + k].reshape(view),
                                   p['v_' + k].reshape(view), layer, running.get(k), name=f"adam_{k}{layer}")
            order = running[k][1]
    for k, res in running.items():
        out[k] = [t.reshape(p[k].shape) for t in res]

    small = {
        'g_mix': jnp.concatenate(g['g_mix']), 'g_ffn': jnp.concatenate(g['g_ffn']), 'g_final': g['g_final'],
        'w_sp': jnp.stack(g['w_sp']), 'b_sp': jnp.stack(g['b_sp']), 'g_kv': g['g_kv'], 'g_kv_lat': g['g_kv_lat'],
        'g_q_lat': jnp.concatenate(g['g_q_lat']), 'g_mem': jnp.concatenate(g['g_mem']),
        'conv_b': jnp.stack(g['conv_b']),
        'g_v': jnp.concatenate(g['g_v']),
        'conv_w': jnp.stack(g['conv_w']).transpose(0, 2, 1, 3),
    }
    small_names = REPLICATED + SMALL_SHARDED
    flat = jnp.concatenate([small[k].reshape(-1).astype(F32) for k in small_names])
    n_small = flat.shape[0]
    padded = -(-n_small // SMALL_ALIGN) * SMALL_ALIGN
    flat = jnp.pad(flat, (0, padded - n_small)).reshape(N_DEV, -1, 128)
    last_update = out[pending[-1][1][-1]][1]
    (small_parts,) = _exchange([flat], name="scatter_small", scatter=True, after=last_update)
    reduced = _sum_parts(small_parts, name="sum_small")
    (small_all,) = _exchange([reduced], name="gather_small", scatter=False)
    small_all = small_all.reshape(-1)
    grads_small, off = {}, 0
    for k in small_names:
        size = small[k].size
        grads_small[k] = small_all[off:off + size].reshape(small[k].shape)
        off += size
    grads_small['g_v'] = lax.dynamic_slice_in_dim(grads_small['g_v'], my_id * p['g_v'].shape[1], p['g_v'].shape[1], axis=1)
    grads_small['conv_w'] = lax.dynamic_index_in_dim(grads_small['conv_w'], my_id, axis=2, keepdims=False)
    gs = jnp.concatenate([grads_small[k].reshape(-1) for k in small_names])
    n_loc = gs.shape[0]
    pad_loc = -(-n_loc // 1024) * 1024 - n_loc

    def pack(prefix):
        t = jnp.concatenate([p[prefix + k].reshape(-1) for k in small_names])
        return jnp.pad(t, (0, pad_loc)).reshape(1, -1, 128)

    res = _sum_adam(jnp.pad(gs, (0, pad_loc)).reshape(1, -1, 128), pack(''), pack('m_'), pack('v_'), 0, None,
                    name="adam_small")
    off = 0
    for k in small_names:
        size = p[k].size
        out[k] = [t.reshape(-1)[off:off + size].reshape(p[k].shape) for t in res]
        off += size

    outs = [loss, grad_x[None]]
    for i in range(4):
        outs += [out[k][i] for k in WEIGHTS]
    return tuple(outs)


def kernel(x, mem, positions, g_mix, g_ffn, g_final, w_in_a, g_v, w_sp, b_sp, g_kv, w_kv_a, g_kv_lat, w_in_b, g_q_lat, w_uq, w_uk, w_uv, g_mem, w_mem_kv, w_out, w_ffn_up, conv_w, conv_b, w_ffn_down, loss_target, m_g_mix, m_g_ffn, m_g_final, m_w_in_a, m_g_v, m_w_sp, m_b_sp, m_g_kv, m_w_kv_a, m_g_kv_lat, m_w_in_b, m_g_q_lat, m_w_uq, m_w_uk, m_w_uv, m_g_mem, m_w_mem_kv, m_w_out, m_w_ffn_up, m_conv_w, m_conv_b, m_w_ffn_down, v_g_mix, v_g_ffn, v_g_final, v_w_in_a, v_g_v, v_w_sp, v_b_sp, v_g_kv, v_w_kv_a, v_g_kv_lat, v_w_in_b, v_g_q_lat, v_w_uq, v_w_uk, v_w_uv, v_g_mem, v_w_mem_kv, v_w_out, v_w_ffn_up, v_conv_w, v_conv_b, v_w_ffn_down):
    return _step((x, mem, positions, g_mix, g_ffn, g_final, w_in_a, g_v, w_sp, b_sp, g_kv, w_kv_a, g_kv_lat, w_in_b, g_q_lat, w_uq, w_uk, w_uv, g_mem, w_mem_kv, w_out, w_ffn_up, conv_w, conv_b, w_ffn_down, loss_target, m_g_mix, m_g_ffn, m_g_final, m_w_in_a, m_g_v, m_w_sp, m_b_sp, m_g_kv, m_w_kv_a, m_g_kv_lat, m_w_in_b, m_g_q_lat, m_w_uq, m_w_uk, m_w_uv, m_g_mem, m_w_mem_kv, m_w_out, m_w_ffn_up, m_conv_w, m_conv_b, m_w_ffn_down, v_g_mix, v_g_ffn, v_g_final, v_w_in_a, v_g_v, v_w_sp, v_b_sp, v_g_kv, v_w_kv_a, v_g_kv_lat, v_w_in_b, v_g_q_lat, v_w_uq, v_w_uk, v_w_uv, v_g_mem, v_w_mem_kv, v_w_out, v_w_ffn_up, v_conv_w, v_conv_b, v_w_ffn_down))
```
